```python
import jax, jax.numpy as jnp
from jax import lax
import numpy as np

D_MODEL = 1024
BATCH = 32
SEQ = 256
DEPTH = 2
DEC_BATCH = 2
DEC_SEQ = 1024
PAST_LEN = 256

GRID_W = 64
D_CONV = D_MODEL
D_RNN = D_MODEL
H_RNN = 16
HD_RNN = D_RNN // H_RNN
CONV_A_W = 3
CONV_B_W = 4
CONV_B_LEFT = 2
RGLRU_C = 8.0
D_FF = 2816
N_EXPERTS = 8
TOP_K = 2
N_DENSE = (DEPTH + 1) // 2
N_MOE = DEPTH // 2
N_MOD = 6
N_IN = 3 * D_CONV + 2 * D_RNN + 2 * D_MODEL
SPLITS = [D_CONV, 2 * D_CONV, 3 * D_CONV, 3 * D_CONV + D_RNN, 3 * D_CONV + 2 * D_RNN, 3 * D_CONV + 2 * D_RNN + D_MODEL]
DN_ALPHA = (2.0 * DEPTH) ** 0.25
DN_BETA = (8.0 * DEPTH) ** -0.25

kernel_name = "hybrid_shortconv_rglru_diffusion_step"


def _ln_plain(x, eps=1e-6):
    xf = x.astype(jnp.float32)
    mu = jnp.mean(xf, axis=-1, keepdims=True)
    var = jnp.mean(jnp.square(xf - mu), axis=-1, keepdims=True)
    return ((xf - mu) * lax.rsqrt(var + eps)).astype(x.dtype)


def _ln(x, g, b):
    return _ln_plain(x, 1e-5) * g + b


def _dwconv(x, w, left):
    k = w.shape[0]
    t = x.shape[-2]
    pad = [(0, 0)] * (x.ndim - 2) + [(left, k - 1 - left), (0, 0)]
    xp = jnp.pad(x, pad)
    out = xp[..., 0:t, :] * w[0]
    for j in range(1, k):
        out = out + xp[..., j:j + t, :] * w[j]
    return out


def _conv_seq(x, w, left, grid):
    if not grid:
        return _dwconv(x, w, left)
    b, t, ch = x.shape
    rows = t // GRID_W
    return _dwconv(x.reshape(b, rows, GRID_W, ch), w, left).reshape(b, t, ch)


def _linear_scan(a, bt, h0, reverse):
    def combine(p, q):
        return (p[0] * q[0], q[0] * p[1] + q[1])
    a_cum, b_cum = lax.associative_scan(combine, (a, bt), axis=1, reverse=reverse)
    return a_cum * h0[:, None, :] + b_cum


def _rglru_dir(xr, w_ga, b_ga, w_gx, b_gx, lam, h0, reverse):
    bsz, t, _ = xr.shape
    xh = xr.reshape(bsz, t, H_RNN, HD_RNN)
    r = jax.nn.sigmoid(jnp.einsum('bthi,hij->bthj', xh, w_ga).reshape(bsz, t, D_RNN) + b_ga)
    ig = jax.nn.sigmoid(jnp.einsum('bthi,hij->bthj', xh, w_gx).reshape(bsz, t, D_RNN) + b_gx)
    log_a = -RGLRU_C * r.astype(jnp.float32) * jax.nn.softplus(-lam.astype(jnp.float32))
    a = jnp.exp(log_a)
    bt = jnp.sqrt(-jnp.expm1(2.0 * log_a)) * (ig * xr).astype(jnp.float32)
    return _linear_scan(a, bt, h0.astype(jnp.float32), reverse)


def _token_mixer(u, h0, grid, w_in, conv_a, w_a_out, conv_b, conv_b_bias, w_ga, b_ga, w_gx, b_gx, lam, w_b_out, w_o):
    proj = u @ w_in
    xa, bg, cg, y_r, x_r, g_a, g_b = jnp.split(proj, SPLITS, axis=-1)
    branch_a = (bg * _conv_seq(cg * xa, conv_a, 1, grid)) @ w_a_out
    xr = _conv_seq(x_r, conv_b, CONV_B_LEFT, grid) + conv_b_bias
    h_f = _rglru_dir(xr, w_ga[0], b_ga[0], w_gx[0], b_gx[0], lam[0], h0[:, 0], False)
    h_b = _rglru_dir(xr, w_ga[1], b_ga[1], w_gx[1], b_gx[1], lam[1], h0[:, 1], True)
    branch_b = ((h_f + h_b).astype(u.dtype) * jax.nn.gelu(y_r)) @ w_b_out
    merged = jax.nn.sigmoid(g_a) * branch_a + jax.nn.sigmoid(g_b) * branch_b
    state = jnp.stack([h_f[:, -1], h_b[:, 0]], axis=1)
    return merged @ w_o, state


def _swiglu(u, w1, w3, w2):
    return (jax.nn.silu(u @ w1) * (u @ w3)) @ w2


def _moe(u, w_r, b_r, we1, we3, we2):
    logits = (u @ w_r).astype(jnp.float32) + b_r.astype(jnp.float32)
    top_v, top_i = lax.top_k(logits, TOP_K)
    top_w = jax.nn.softmax(top_v, axis=-1)
    comb = jnp.sum(jax.nn.one_hot(top_i, N_EXPERTS, dtype=jnp.float32) * top_w[..., None], axis=-2).astype(u.dtype)
    out = jnp.zeros_like(u)
    for e in range(N_EXPERTS):
        out = out + comb[..., e:e + 1] * _swiglu(u, we1[e], we3[e], we2[e])
    return out


def _trunk(x, cond, h0_all, grid, params):
    (w_mod, b_mod, w_in, conv_a, w_a_out, conv_b, conv_b_bias, w_gate_a, b_gate_a, w_gate_x, b_gate_x,
     lru_lambda, w_b_out, w_o, ln1_g, ln1_b, ln2_g, ln2_b, ffn_w1, ffn_w3, ffn_w2,
     router_w, router_b, moe_w1, moe_w3, moe_w2) = params
    states = []
    for l in range(DEPTH):
        mod = (jax.nn.silu(cond) @ w_mod[l] + b_mod[l])[:, None, :]
        sh1, sc1, g1, sh2, sc2, g2 = jnp.split(mod, N_MOD, axis=-1)
        u = _ln_plain(x) * (1.0 + sc1) + sh1
        mix, st = _token_mixer(u, h0_all[:, l], grid, w_in[l], conv_a[l], w_a_out[l], conv_b[l], conv_b_bias[l],
                               w_gate_a[l], b_gate_a[l], w_gate_x[l], b_gate_x[l], lru_lambda[l], w_b_out[l], w_o[l])
        x = _ln(DN_ALPHA * x + g1 * mix, ln1_g[l], ln1_b[l])
        u2 = _ln_plain(x) * (1.0 + sc2) + sh2
        if l % 2 == 0:
            f = _swiglu(u2, ffn_w1[l // 2], ffn_w3[l // 2], ffn_w2[l // 2])
        else:
            f = _moe(u2, router_w[l // 2], router_b[l // 2], moe_w1[l // 2], moe_w3[l // 2], moe_w2[l // 2])
        x = _ln(DN_ALPHA * x + g2 * f, ln2_g[l], ln2_b[l])
        states.append(st)
    return x, jnp.stack(states, axis=1)


def setup_inputs(seed: int = 0) -> dict:
    key = jax.random.key(seed)
    ks = jax.random.split(key, 40)
    f32 = jnp.float32

    def nrm(k, shape, scale):
        return jax.random.normal(k, shape, f32) * scale

    a0 = jax.random.uniform(ks[16], (DEPTH, 2, D_RNN), f32, 0.9, 0.999)
    return {
        "x_prompt": nrm(ks[0], (BATCH, SEQ, D_MODEL), 1.0),
        "x_sample": nrm(ks[1], (DEC_BATCH, DEC_SEQ, D_MODEL), 1.0),
        "state_rglru": nrm(ks[2], (DEC_BATCH, DEPTH, 2, D_RNN), 0.5),
        "c": nrm(ks[3], (DEC_BATCH, D_MODEL), 1.0),
        "c_ctx": nrm(ks[4], (D_MODEL,), 1.0),
        "w_mod": nrm(ks[5], (DEPTH, D_MODEL, N_MOD * D_MODEL), 0.5 * D_MODEL ** -0.5),
        "b_mod": nrm(ks[6], (DEPTH, N_MOD * D_MODEL), 0.02),
        "w_in": nrm(ks[7], (DEPTH, D_MODEL, N_IN), D_MODEL ** -0.5),
        "conv_a": nrm(ks[8], (DEPTH, CONV_A_W, D_CONV), CONV_A_W ** -0.5),
        "w_a_out": nrm(ks[9], (DEPTH, D_CONV, D_MODEL), D_CONV ** -0.5),
        "conv_b": nrm(ks[10], (DEPTH, CONV_B_W, D_RNN), CONV_B_W ** -0.5),
        "conv_b_bias": nrm(ks[11], (DEPTH, D_RNN), 0.02),
        "w_gate_a": nrm(ks[12], (DEPTH, 2, H_RNN, HD_RNN, HD_RNN), HD_RNN ** -0.5),
        "b_gate_a": nrm(ks[13], (DEPTH, 2, D_RNN), 0.1),
        "w_gate_x": nrm(ks[14], (DEPTH, 2, H_RNN, HD_RNN, HD_RNN), HD_RNN ** -0.5),
        "b_gate_x": nrm(ks[15], (DEPTH, 2, D_RNN), 0.1),
        "lru_lambda": jnp.log(a0) - jnp.log1p(-a0),
        "w_b_out": nrm(ks[17], (DEPTH, D_RNN, D_MODEL), D_RNN ** -0.5),
        "w_o": nrm(ks[18], (DEPTH, D_MODEL, D_MODEL), DN_BETA * D_MODEL ** -0.5),
        "ln1_g": 1.0 + nrm(ks[19], (DEPTH, D_MODEL), 0.02),
        "ln1_b": nrm(ks[20], (DEPTH, D_MODEL), 0.02),
        "ln2_g": 1.0 + nrm(ks[21], (DEPTH, D_MODEL), 0.02),
        "ln2_b": nrm(ks[22], (DEPTH, D_MODEL), 0.02),
        "ffn_w1": nrm(ks[23], (N_DENSE, D_MODEL, D_FF), D_MODEL ** -0.5),
        "ffn_w3": nrm(ks[24], (N_DENSE, D_MODEL, D_FF), D_MODEL ** -0.5),
        "ffn_w2": nrm(ks[25], (N_DENSE, D_FF, D_MODEL), DN_BETA * D_FF ** -0.5),
        "router_w": nrm(ks[26], (N_MOE, D_MODEL, N_EXPERTS), D_MODEL ** -0.5),
        "router_b": nrm(ks[27], (N_MOE, N_EXPERTS), 0.01),
        "moe_w1": nrm(ks[28], (N_MOE, N_EXPERTS, D_MODEL, D_FF), D_MODEL ** -0.5),
        "moe_w3": nrm(ks[29], (N_MOE, N_EXPERTS, D_MODEL, D_FF), D_MODEL ** -0.5),
        "moe_w2": nrm(ks[30], (N_MOE, N_EXPERTS, D_FF, D_MODEL), DN_BETA * D_FF ** -0.5),
    }


def reference(x_prompt, x_sample, state_rglru, c, c_ctx, w_mod, b_mod, w_in, conv_a, w_a_out, conv_b, conv_b_bias,
              w_gate_a, b_gate_a, w_gate_x, b_gate_x, lru_lambda, w_b_out, w_o, ln1_g, ln1_b, ln2_g, ln2_b,
              ffn_w1, ffn_w3, ffn_w2, router_w, router_b, moe_w1, moe_w3, moe_w2):
    params = (w_mod, b_mod, w_in, conv_a, w_a_out, conv_b, conv_b_bias, w_gate_a, b_gate_a, w_gate_x, b_gate_x,
              lru_lambda, w_b_out, w_o, ln1_g, ln1_b, ln2_g, ln2_b, ffn_w1, ffn_w3, ffn_w2,
              router_w, router_b, moe_w1, moe_w3, moe_w2)
    h0_ctx = jnp.zeros((x_prompt.shape[0], DEPTH, 2, D_RNN), jnp.float32)
    y_prompt, ctx_state = _trunk(x_prompt, c_ctx[None, :], h0_ctx, False, params)
    new_state_rglru = ctx_state.astype(x_prompt.dtype)
    y_sample, _ = _trunk(x_sample, c, state_rglru, True, params)
    return (y_prompt, y_sample, new_state_rglru)
```

```python
import functools

import jax
import jax.numpy as jnp
from jax import lax
from jax.experimental import pallas as pl
from jax.experimental.pallas import tpu as pltpu

F32 = jnp.float32
BF16 = jnp.bfloat16

D = 1024
SEQ = 256
DEC_SEQ = 1024
GRID_W = 64
N_HEAD = 16
HEAD_D = D // N_HEAD
RGLRU_C = 8.0
D_FF = 2816
N_EXP = 8
N_MOD = 6
DEPTH = 2
DN_ALPHA = (2.0 * DEPTH) ** 0.25

V7X_SUBLANES = 8
V7X_LANES = 128
V7X_VMEM_BYTES = 64 * 1024 * 1024
VMEM_LIMIT = V7X_VMEM_BYTES - 12 * 1024 * 1024

TM1 = 1024
CW = 256
N_CH = D // CW
N_LC = CW // V7X_LANES
N_SEG = V7X_SUBLANES
SEG = TM1 // N_SEG
SEG_STRIDE = SEG + 8
TM2 = 512
TM3 = 512
FC = 1408
N_FC = D_FF // FC
MOD_ROWS = 8
MOD_BLK = 1536


def _ln_plain(x, eps):
    mu = jnp.mean(x, axis=-1, keepdims=True)
    xc = x - mu
    var = jnp.mean(xc * xc, axis=-1, keepdims=True)
    return xc * lax.rsqrt(var + eps)


def _mod_row(tok0, n_ctx_tok):
    dec = jnp.maximum(tok0 - n_ctx_tok, 0) // DEC_SEQ
    return jnp.where(tok0 >= n_ctx_tok, 1 + dec, 0)


def _mod_kernel(cond_ref, w_ref, b_ref, o_ref):
    cnd = cond_ref[...]
    s = cnd * jax.nn.sigmoid(cnd)
    o_ref[0] = jnp.dot(s, w_ref[0], preferred_element_type=F32,
                       precision=lax.Precision.HIGHEST) + b_ref[0]


def _modulation(cond, w_mod, b_mod):
    depth = w_mod.shape[0]
    n_out = w_mod.shape[2]
    return pl.pallas_call(
        _mod_kernel,
        grid=(depth, n_out // MOD_BLK),
        in_specs=[
            pl.BlockSpec((MOD_ROWS, D), lambda l, j: (0, 0)),
            pl.BlockSpec((1, D, MOD_BLK), lambda l, j: (l, 0, j)),
            pl.BlockSpec((1, 1, MOD_BLK), lambda l, j: (l, 0, j)),
        ],
        out_specs=pl.BlockSpec((1, MOD_ROWS, MOD_BLK), lambda l, j: (l, 0, j)),
        out_shape=jax.ShapeDtypeStruct((depth, MOD_ROWS, n_out), F32),
        compiler_params=pltpu.CompilerParams(
            dimension_semantics=("arbitrary", "arbitrary"), vmem_limit_bytes=VMEM_LIMIT),
        name="modulation",
    )(cond, w_mod, b_mod.reshape(depth, 1, n_out))


def _scan_dir(a_scr, b_scr, hl_scr, ac_scr, h0, keep, reverse):
    n_lc = a_scr.shape[0]

    def body(k, carry):
        kk = SEG - 1 - k if reverse else k
        idx = pl.ds(kk, N_SEG, stride=SEG_STRIDE)
        new = []
        for c in range(n_lc):
            h, acc = carry[c]
            a_k = a_scr[c, idx, :]
            h = a_k * h + b_scr[c, idx, :]
            acc = a_k * acc
            hl_scr[c, idx, :] = h
            ac_scr[c, idx, :] = acc
            new.append((h, acc))
        return tuple(new)

    init = tuple((jnp.zeros((N_SEG, V7X_LANES), F32), jnp.ones((N_SEG, V7X_LANES), F32))
                 for _ in range(n_lc))
    fin = lax.fori_loop(0, SEG, body, init, unroll=8)
    h_loc = jnp.concatenate([fin[c][0] for c in range(n_lc)], axis=1)
    a_tot = jnp.concatenate([fin[c][1] for c in range(n_lc)], axis=1)

    order = range(N_SEG - 1, -1, -1) if reverse else range(N_SEG)
    h_in = [None] * N_SEG
    prev = None
    for s in order:
        cur = h0[s:s + 1]
        if prev is not None:
            left = a_tot[prev:prev + 1] * h_in[prev] + h_loc[prev:prev + 1]
            cur = keep[s] * left + cur
        h_in[s] = cur
        prev = s
    h_in = jnp.concatenate(h_in, axis=0)
    return h_in, a_tot * h_in + h_loc


def _mix1_kernel(n_ctx_tiles, x_ref, mod_ref, w5_ref, ca_ref, cb_ref, cbias_ref, wbd_ref, gb_ref,
                 lam_ref, h0_ref, apre_ref, bpre_ref, st_ref,
                 u_scr, proj_scr, a_scr, b_scr, hl_scr, ac_scr, hsum_scr):
    i = pl.program_id(0)
    j = pl.program_id(1)
    is_dec = i >= n_ctx_tiles

    @pl.when(j == 0)
    def _():
        row = jnp.where(is_dec, i - (n_ctx_tiles - 1), 0)
        m = mod_ref[pl.ds(row, 1), :]
        sh1 = m[:, 0:D]
        sc1 = m[:, D:2 * D]
        u_scr[...] = (_ln_plain(x_ref[...], 1e-6) * (1.0 + sc1) + sh1).astype(BF16)

    proj_scr[...] = jnp.dot(u_scr[...], w5_ref[0], preferred_element_type=F32)

    pm = jnp.where(is_dec, GRID_W - 1, SEQ - 1)
    pos = lax.broadcasted_iota(jnp.int32, (TM1, CW), 0) & pm
    has_p1 = pos >= 1
    has_p2 = pos >= 2
    has_n1 = pos < pm

    def shifted(v, back, mask):
        return jnp.where(mask, pltpu.roll(v, back % TM1, axis=0), 0.0)

    z = proj_scr[:, 2 * CW:3 * CW] * proj_scr[:, 0:CW]
    ca = ca_ref[...]
    conv_a = (ca[0:1] * shifted(z, 1, has_p1) + ca[1:2] * z + ca[2:3] * shifted(z, -1, has_n1))
    apre_ref[...] = (proj_scr[:, CW:2 * CW] * conv_a).astype(BF16)

    xr_in = proj_scr[:, 4 * CW:5 * CW]
    cb = cb_ref[...]
    xr = (cb[0:1] * shifted(xr_in, 2, has_p2) + cb[1:2] * shifted(xr_in, 1, has_p1)
          + cb[2:3] * xr_in + cb[3:4] * shifted(xr_in, -1, has_n1) + cbias_ref[...])
    gates = jnp.dot(xr.astype(BF16), wbd_ref[0], preferred_element_type=F32)
    gb = gb_ref[...]
    lam = lam_ref[...]
    sp = jnp.maximum(-lam, 0.0) + jnp.log1p(jnp.exp(-jnp.abs(lam)))

    one = jnp.float32(1.0)
    for d in range(2):
        ga = gates[:, (2 * d) * CW:(2 * d + 1) * CW] + gb[2 * d:2 * d + 1]
        gx = gates[:, (2 * d + 1) * CW:(2 * d + 2) * CW] + gb[2 * d + 1:2 * d + 2]
        r = jax.nn.sigmoid(ga)
        ig = jax.nn.sigmoid(gx)
        a = jnp.exp((-RGLRU_C * r) * sp[d:d + 1])
        bt = jnp.sqrt(1.0 - a * a) * (ig * xr)
        for s in range(N_SEG):
            lo = s * SEG_STRIDE
            for c in range(N_LC):
                lanes = slice(c * V7X_LANES, (c + 1) * V7X_LANES)
                a_scr[c, lo:lo + SEG, :] = a[s * SEG:(s + 1) * SEG, lanes]
                b_scr[c, lo:lo + SEG, :] = bt[s * SEG:(s + 1) * SEG, lanes]
        if d == 0:
            keep = [jnp.where(is_dec, one, jnp.float32(s % 2 == 1)) for s in range(N_SEG)]
        else:
            keep = [jnp.where(is_dec, one, jnp.float32(s % 2 == 0)) for s in range(N_SEG)]
        h_in, h_out = _scan_dir(a_scr, b_scr, hl_scr, ac_scr, h0_ref[0, d], keep, reverse=(d == 1))
        st_ref[0, d] = h_out
        for s in range(N_SEG):
            lo = s * SEG_STRIDE
            for c in range(N_LC):
                lanes = slice(c * V7X_LANES, (c + 1) * V7X_LANES)
                h_seg = hl_scr[c, lo:lo + SEG, :] + ac_scr[c, lo:lo + SEG, :] * h_in[s:s + 1, lanes]
                if d == 0:
                    hsum_scr[s * SEG:(s + 1) * SEG, lanes] = h_seg
                else:
                    hsum_scr[s * SEG:(s + 1) * SEG, lanes] += h_seg

    bpre_ref[...] = (hsum_scr[...] * jax.nn.gelu(proj_scr[:, 3 * CW:4 * CW])).astype(BF16)


def _mixer_part1(x, mod_l, w5, conv_a, conv_b, conv_b_bias, wbd, gbias, lam, h0, n_ctx_tok):
    n_tok = x.shape[0]
    n_tiles = n_tok // TM1
    kern = functools.partial(_mix1_kernel, n_ctx_tok // TM1)
    return pl.pallas_call(
        kern,
        grid=(n_tiles, N_CH),
        in_specs=[
            pl.BlockSpec((TM1, D), lambda i, j: (i, 0)),
            pl.BlockSpec((MOD_ROWS, N_MOD * D), lambda i, j: (0, 0)),
            pl.BlockSpec((1, D, 5 * CW), lambda i, j: (j, 0, 0)),
            pl.BlockSpec((3, CW), lambda i, j: (0, j)),
            pl.BlockSpec((4, CW), lambda i, j: (0, j)),
            pl.BlockSpec((1, CW), lambda i, j: (0, j)),
            pl.BlockSpec((1, CW, 4 * CW), lambda i, j: (j, 0, 0)),
            pl.BlockSpec((4, CW), lambda i, j: (0, j)),
            pl.BlockSpec((2, CW), lambda i, j: (0, j)),
            pl.BlockSpec((1, 2, N_SEG, CW), lambda i, j: (i, 0, 0, j)),
        ],
        out_specs=[
            pl.BlockSpec((TM1, CW), lambda i, j: (i, j)),
            pl.BlockSpec((TM1, CW), lambda i, j: (i, j)),
            pl.BlockSpec((1, 2, N_SEG, CW), lambda i, j: (i, 0, 0, j)),
        ],
        out_shape=[
            jax.ShapeDtypeStruct((n_tok, D), BF16),
            jax.ShapeDtypeStruct((n_tok, D), BF16),
            jax.ShapeDtypeStruct((n_tiles, 2, N_SEG, D), F32),
        ],
        scratch_shapes=[
            pltpu.VMEM((TM1, D), BF16),
            pltpu.VMEM((TM1, 5 * CW), F32),
            pltpu.VMEM((N_LC, N_SEG * SEG_STRIDE, V7X_LANES), F32),
            pltpu.VMEM((N_LC, N_SEG * SEG_STRIDE, V7X_LANES), F32),
            pltpu.VMEM((N_LC, N_SEG * SEG_STRIDE, V7X_LANES), F32),
            pltpu.VMEM((N_LC, N_SEG * SEG_STRIDE, V7X_LANES), F32),
            pltpu.VMEM((TM1, CW), F32),
        ],
        compiler_params=pltpu.CompilerParams(
            dimension_semantics=("arbitrary", "arbitrary"), vmem_limit_bytes=VMEM_LIMIT),
        name="mixer_scan",
    )(x, mod_l, w5, conv_a, conv_b, conv_b_bias, wbd, gbias, lam, h0)


def _mix2_kernel(n_ctx_tok, x_ref, mod_ref, ap_ref, bp_ref, wg_ref, wa_ref, wb_ref, wo_ref,
                 g_ref, b_ref, o_ref):
    i = pl.program_id(0)
    m = mod_ref[pl.ds(_mod_row(i * TM2, n_ctx_tok), 1), :]
    sh1 = m[:, 0:D]
    sc1 = m[:, D:2 * D]
    g1 = m[:, 2 * D:3 * D]
    x = x_ref[...]
    u = (_ln_plain(x, 1e-6) * (1.0 + sc1) + sh1).astype(BF16)
    gates = jnp.dot(u, wg_ref[...], preferred_element_type=F32)
    br_a = jnp.dot(ap_ref[...], wa_ref[...], preferred_element_type=F32)
    br_b = jnp.dot(bp_ref[...], wb_ref[...], preferred_element_type=F32)
    merged = jax.nn.sigmoid(gates[:, 0:D]) * br_a + jax.nn.sigmoid(gates[:, D:2 * D]) * br_b
    mix = jnp.dot(merged.astype(BF16), wo_ref[...], preferred_element_type=F32)
    y = DN_ALPHA * x + g1 * mix
    o_ref[...] = _ln_plain(y, 1e-5) * g_ref[...] + b_ref[...]


def _mixer_part2(x, mod_l, a_pre, b_pre, wg, wa, wb, wo, ln_g, ln_b, n_ctx_tok):
    n_tok = x.shape[0]
    const = lambda i: (0, 0)
    tile = lambda i: (i, 0)
    return pl.pallas_call(
        functools.partial(_mix2_kernel, n_ctx_tok),
        grid=(n_tok // TM2,),
        in_specs=[
            pl.BlockSpec((TM2, D), tile),
            pl.BlockSpec((MOD_ROWS, N_MOD * D), const),
            pl.BlockSpec((TM2, D), tile),
            pl.BlockSpec((TM2, D), tile),
            pl.BlockSpec((D, 2 * D), const),
            pl.BlockSpec((D, D), const),
            pl.BlockSpec((D, D), const),
            pl.BlockSpec((D, D), const),
            pl.BlockSpec((1, D), const),
            pl.BlockSpec((1, D), const),
        ],
        out_specs=pl.BlockSpec((TM2, D), tile),
        out_shape=jax.ShapeDtypeStruct((n_tok, D), F32),
        compiler_params=pltpu.CompilerParams(
            dimension_semantics=("arbitrary",), vmem_limit_bytes=VMEM_LIMIT),
        name="mixer_out",
    )(x, mod_l, a_pre, b_pre, wg, wa, wb, wo, ln_g, ln_b)


def _ffn_prologue(n_ctx_tok, x_ref, mod_ref):
    m = mod_ref[pl.ds(_mod_row(pl.program_id(0) * TM3, n_ctx_tok), 1), :]
    sh2 = m[:, 3 * D:4 * D]
    sc2 = m[:, 4 * D:5 * D]
    return _ln_plain(x_ref[...], 1e-6) * (1.0 + sc2) + sh2


def _ffn_epilogue(n_ctx_tok, x_ref, mod_ref, acc, g_ref, b_ref, o_ref):
    m = mod_ref[pl.ds(_mod_row(pl.program_id(0) * TM3, n_ctx_tok), 1), :]
    g2 = m[:, 5 * D:6 * D]
    y = DN_ALPHA * x_ref[...] + g2 * acc
    o_ref[...] = _ln_plain(y, 1e-5) * g_ref[...] + b_ref[...]


def _swiglu_chunk(u, w13, w2):
    h = jnp.dot(u, w13, preferred_element_type=F32)
    h1 = h[:, 0:FC]
    hid = (h1 * jax.nn.sigmoid(h1) * h[:, FC:2 * FC]).astype(BF16)
    return jnp.dot(hid, w2, preferred_element_type=F32)


def _dense_ffn_kernel(n_ctx_tok, x_ref, mod_ref, w13_ref, w2_ref, g_ref, b_ref, o_ref, u_scr, acc_scr):
    f = pl.program_id(1)

    @pl.when(f == 0)
    def _():
        u_scr[...] = _ffn_prologue(n_ctx_tok, x_ref, mod_ref).astype(BF16)

    p = _swiglu_chunk(u_scr[...], w13_ref[0], w2_ref[...])

    @pl.when(f == 0)
    def _():
        acc_scr[...] = p

    @pl.when(f > 0)
    def _():
        acc_scr[...] += p

    @pl.when(f == N_FC - 1)
    def _():
        _ffn_epilogue(n_ctx_tok, x_ref, mod_ref, acc_scr[...], g_ref, b_ref, o_ref)


def _dense_ffn(x, mod_l, w13, w2, ln_g, ln_b, n_ctx_tok):
    n_tok = x.shape[0]
    return pl.pallas_call(
        functools.partial(_dense_ffn_kernel, n_ctx_tok),
        grid=(n_tok // TM3, N_FC),
        in_specs=[
            pl.BlockSpec((TM3, D), lambda i, f: (i, 0)),
            pl.BlockSpec((MOD_ROWS, N_MOD * D), lambda i, f: (0, 0)),
            pl.BlockSpec((1, D, 2 * FC), lambda i, f: (f, 0, 0)),
            pl.BlockSpec((FC, D), lambda i, f: (f, 0)),
            pl.BlockSpec((1, D), lambda i, f: (0, 0)),
            pl.BlockSpec((1, D), lambda i, f: (0, 0)),
        ],
        out_specs=pl.BlockSpec((TM3, D), lambda i, f: (i, 0)),
        out_shape=jax.ShapeDtypeStruct((n_tok, D), F32),
        scratch_shapes=[pltpu.VMEM((TM3, D), BF16), pltpu.VMEM((TM3, D), F32)],
        compiler_params=pltpu.CompilerParams(
            dimension_semantics=("arbitrary", "arbitrary"), vmem_limit_bytes=VMEM_LIMIT),
        name="dense_ffn",
    )(x, mod_l, w13, w2, ln_g, ln_b)


def _moe_kernel(n_ctx_tok, x_ref, mod_ref, wr_ref, br_ref, w13_ref, w2_ref, g_ref, b_ref, o_ref,
                u_scr, comb_scr, acc_scr):
    e = pl.program_id(1)
    f = pl.program_id(2)
    first = jnp.logical_and(e == 0, f == 0)

    @pl.when(first)
    def _():
        u2 = _ffn_prologue(n_ctx_tok, x_ref, mod_ref)
        u_scr[...] = u2.astype(BF16)
        logits = jnp.dot(u2, wr_ref[...], preferred_element_type=F32,
                         precision=lax.Precision.HIGHEST) + br_ref[...]
        lane = lax.broadcasted_iota(jnp.int32, logits.shape, 1).astype(F32)
        neg = jnp.float32(-jnp.inf)
        lg = jnp.where(lane < N_EXP, logits, neg)
        v1 = jnp.max(lg, axis=-1, keepdims=True)
        i1 = jnp.min(jnp.where(lg == v1, lane, float(V7X_LANES)), axis=-1, keepdims=True)
        lg2 = jnp.where(lane == i1, neg, lg)
        v2 = jnp.max(lg2, axis=-1, keepdims=True)
        i2 = jnp.min(jnp.where(lg2 == v2, lane, float(V7X_LANES)), axis=-1, keepdims=True)
        t = jnp.exp(v2 - v1)
        w_top = 1.0 / (1.0 + t)
        comb_scr[...] = jnp.where(lane == i1, w_top, 0.0) + jnp.where(lane == i2, t * w_top, 0.0)
        acc_scr[...] = jnp.zeros_like(acc_scr)

    p = _swiglu_chunk(u_scr[...], w13_ref[0, 0], w2_ref[0])
    lane = lax.broadcasted_iota(jnp.int32, comb_scr.shape, 1)
    ce = jnp.sum(jnp.where(lane == e, comb_scr[...], 0.0), axis=-1, keepdims=True)
    acc_scr[...] += ce * p

    @pl.when(jnp.logical_and(e == N_EXP - 1, f == N_FC - 1))
    def _():
        _ffn_epilogue(n_ctx_tok, x_ref, mod_ref, acc_scr[...], g_ref, b_ref, o_ref)


def _moe_ffn(x, mod_l, wr, br, w13, w2, ln_g, ln_b, n_ctx_tok):
    n_tok = x.shape[0]
    return pl.pallas_call(
        functools.partial(_moe_kernel, n_ctx_tok),
        grid=(n_tok // TM3, N_EXP, N_FC),
        in_specs=[
            pl.BlockSpec((TM3, D), lambda i, e, f: (i, 0)),
            pl.BlockSpec((MOD_ROWS, N_MOD * D), lambda i, e, f: (0, 0)),
            pl.BlockSpec((D, V7X_LANES), lambda i, e, f: (0, 0)),
            pl.BlockSpec((1, V7X_LANES), lambda i, e, f: (0, 0)),
            pl.BlockSpec((1, 1, D, 2 * FC), lambda i, e, f: (e, f, 0, 0)),
            pl.BlockSpec((1, FC, D), lambda i, e, f: (e, f, 0)),
            pl.BlockSpec((1, D), lambda i, e, f: (0, 0)),
            pl.BlockSpec((1, D), lambda i, e, f: (0, 0)),
        ],
        out_specs=pl.BlockSpec((TM3, D), lambda i, e, f: (i, 0)),
        out_shape=jax.ShapeDtypeStruct((n_tok, D), F32),
        scratch_shapes=[pltpu.VMEM((TM3, D), BF16), pltpu.VMEM((TM3, V7X_LANES), F32),
                        pltpu.VMEM((TM3, D), F32)],
        compiler_params=pltpu.CompilerParams(
            dimension_semantics=("arbitrary", "arbitrary", "arbitrary"), vmem_limit_bytes=VMEM_LIMIT),
        name="moe_ffn",
    )(x, mod_l, wr, br, w13, w2, ln_g, ln_b)


def _pack_w13(w1, w3):
    lead = w1.shape[:-2]
    nl = len(lead)
    perm = tuple(range(nl)) + (nl + 1, nl, nl + 2)
    a = w1.reshape(lead + (D, N_FC, FC)).transpose(perm)
    b = w3.reshape(lead + (D, N_FC, FC)).transpose(perm)
    return jnp.concatenate([a, b], axis=-1).astype(BF16)


def _block_diag_chunks(w):
    per = CW // HEAD_D
    w4 = w.reshape(N_CH, per, HEAD_D, HEAD_D)
    eye = jnp.eye(per, dtype=w.dtype)
    return jnp.einsum("cgij,gk->cgikj", w4, eye).reshape(N_CH, CW, CW)


def kernel(x_prompt, x_sample, state_rglru, c, c_ctx, w_mod, b_mod, w_in, conv_a, w_a_out, conv_b, conv_b_bias, w_gate_a, b_gate_a, w_gate_x, b_gate_x, lru_lambda, w_b_out, w_o, ln1_g, ln1_b, ln2_g, ln2_b, ffn_w1, ffn_w3, ffn_w2, router_w, router_b, moe_w1, moe_w3, moe_w2):
    batch, seq, d = x_prompt.shape
    dec_batch, dec_seq, _ = x_sample.shape
    depth = w_mod.shape[0]
    assert (d, seq, dec_seq, depth) == (D, SEQ, DEC_SEQ, DEPTH)
    n_ctx_tok = batch * seq
    n_dec_tok = dec_batch * dec_seq
    assert n_ctx_tok % TM1 == 0 and TM1 == dec_seq and 1 + dec_batch <= MOD_ROWS
    n_ctx_tiles = n_ctx_tok // TM1
    n_tiles = n_ctx_tiles + dec_batch
    seq_per_tile = TM1 // seq
    assert N_SEG == 2 * seq_per_tile

    x = jnp.concatenate([x_prompt.reshape(n_ctx_tok, D), x_sample.reshape(n_dec_tok, D)], axis=0)

    cond = jnp.zeros((MOD_ROWS, D), F32).at[0].set(c_ctx).at[1:1 + dec_batch].set(c)
    mod = _modulation(cond, w_mod, b_mod)

    states = []
    for l in range(depth):
        w5 = (w_in[l][:, :5 * D].reshape(D, 5, N_CH, CW).transpose(2, 0, 1, 3)
              .reshape(N_CH, D, 5 * CW).astype(BF16))
        wg = w_in[l][:, 5 * D:7 * D].astype(BF16)
        wbd = jnp.concatenate(
            [_block_diag_chunks(w_gate_a[l, 0]), _block_diag_chunks(w_gate_x[l, 0]),
             _block_diag_chunks(w_gate_a[l, 1]), _block_diag_chunks(w_gate_x[l, 1])],
            axis=-1).astype(BF16)
        gbias = jnp.stack([b_gate_a[l, 0], b_gate_x[l, 0], b_gate_a[l, 1], b_gate_x[l, 1]], axis=0)
        h0 = jnp.zeros((n_tiles, 2, N_SEG, D), F32)
        h0 = h0.at[n_ctx_tiles:, 0, 0].set(state_rglru[:, l, 0].astype(F32))
        h0 = h0.at[n_ctx_tiles:, 1, N_SEG - 1].set(state_rglru[:, l, 1].astype(F32))

        a_pre, b_pre, st = _mixer_part1(
            x, mod[l], w5, conv_a[l], conv_b[l], conv_b_bias[l].reshape(1, D), wbd, gbias,
            lru_lambda[l], h0, n_ctx_tok)
        x = _mixer_part2(
            x, mod[l], a_pre, b_pre, wg, w_a_out[l].astype(BF16), w_b_out[l].astype(BF16),
            w_o[l].astype(BF16), ln1_g[l].reshape(1, D), ln1_b[l].reshape(1, D), n_ctx_tok)

        if l % 2 == 0:
            k = l // 2
            x = _dense_ffn(x, mod[l], _pack_w13(ffn_w1[k], ffn_w3[k]), ffn_w2[k].astype(BF16),
                           ln2_g[l].reshape(1, D), ln2_b[l].reshape(1, D), n_ctx_tok)
        else:
            k = l // 2
            wr = jnp.zeros((D, V7X_LANES), F32).at[:, :N_EXP].set(router_w[k])
            br = jnp.zeros((1, V7X_LANES), F32).at[0, :N_EXP].set(router_b[k])
            x = _moe_ffn(x, mod[l], wr, br, _pack_w13(moe_w1[k], moe_w3[k]), moe_w2[k].astype(BF16),
                         ln2_g[l].reshape(1, D), ln2_b[l].reshape(1, D), n_ctx_tok)

        st_ctx = st[:n_ctx_tiles]
        fwd = st_ctx[:, 0, 1::2].reshape(batch, D)
        bwd = st_ctx[:, 1, 0::2].reshape(batch, D)
        states.append(jnp.stack([fwd, bwd], axis=1))

    y_prompt = x[:n_ctx_tok].reshape(batch, seq, D)
    y_sample = x[n_ctx_tok:].reshape(dec_batch, dec_seq, D)
    new_state = jnp.stack(states, axis=1).astype(x_prompt.dtype)
    return (y_prompt, y_sample, new_state)
```

```python
import functools

import jax
import jax.numpy as jnp
from jax import lax
from jax.experimental import pallas as pl
from jax.experimental.pallas import tpu as pltpu

F32 = jnp.float32
BF16 = jnp.bfloat16

D = 1024
SEQ = 256
DEC_SEQ = 1024
GRID_W = 64
N_HEAD = 16
HEAD_D = D // N_HEAD
RGLRU_C = 8.0
D_FF = 2816
N_EXP = 8
N_MOD = 6
DEPTH = 2
DN_ALPHA = (2.0 * DEPTH) ** 0.25

V7X_SUBLANES = 8
V7X_LANES = 128
V7X_VMEM_BYTES = 64 * 1024 * 1024
VMEM_LIMIT = V7X_VMEM_BYTES - 12 * 1024 * 1024

TM1 = 1024
CW = 256
N_CH = D // CW
N_LC = CW // V7X_LANES
N_SEG = V7X_SUBLANES
SEG = TM1 // N_SEG
SEG_STRIDE = SEG + 8
TM2 = 512
TM3 = 512
FC = 1408
N_FC = D_FF // FC
TS = 512
TR = 256
MAX_RT = 2 * 10240 // TR + N_EXP
EXPERT_VMEM_LIMIT = V7X_VMEM_BYTES - 6 * 1024 * 1024
MOD_ROWS = 8
MOD_BLK = 1536


def _ln_plain(x, eps):
    mu = jnp.mean(x, axis=-1, keepdims=True)
    xc = x - mu
    var = jnp.mean(xc * xc, axis=-1, keepdims=True)
    return xc * lax.rsqrt(var + eps)


def _mod_row(tok0, n_ctx_tok):
    dec = jnp.maximum(tok0 - n_ctx_tok, 0) // DEC_SEQ
    return jnp.where(tok0 >= n_ctx_tok, 1 + dec, 0)


def _mod_kernel(cond_ref, w_ref, b_ref, o_ref):
    cnd = cond_ref[...]
    s = cnd * jax.nn.sigmoid(cnd)
    o_ref[0] = jnp.dot(s, w_ref[0], preferred_element_type=F32,
                       precision=lax.Precision.HIGHEST) + b_ref[0]


def _modulation(cond, w_mod, b_mod):
    depth = w_mod.shape[0]
    n_out = w_mod.shape[2]
    return pl.pallas_call(
        _mod_kernel,
        grid=(depth, n_out // MOD_BLK),
        in_specs=[
            pl.BlockSpec((MOD_ROWS, D), lambda l, j: (0, 0)),
            pl.BlockSpec((1, D, MOD_BLK), lambda l, j: (l, 0, j)),
            pl.BlockSpec((1, 1, MOD_BLK), lambda l, j: (l, 0, j)),
        ],
        out_specs=pl.BlockSpec((1, MOD_ROWS, MOD_BLK), lambda l, j: (l, 0, j)),
        out_shape=jax.ShapeDtypeStruct((depth, MOD_ROWS, n_out), F32),
        compiler_params=pltpu.CompilerParams(
            dimension_semantics=("arbitrary", "arbitrary"), vmem_limit_bytes=VMEM_LIMIT),
        name="modulation",
    )(cond, w_mod, b_mod.reshape(depth, 1, n_out))


def _scan_dir(a_scr, b_scr, hl_scr, ac_scr, h0, keep, reverse):
    n_lc = a_scr.shape[0]

    def body(k, carry):
        kk = SEG - 1 - k if reverse else k
        idx = pl.ds(kk, N_SEG, stride=SEG_STRIDE)
        new = []
        for c in range(n_lc):
            h, acc = carry[c]
            a_k = a_scr[c, idx, :]
            h = a_k * h + b_scr[c, idx, :]
            acc = a_k * acc
            hl_scr[c, idx, :] = h
            ac_scr[c, idx, :] = acc
            new.append((h, acc))
        return tuple(new)

    init = tuple((jnp.zeros((N_SEG, V7X_LANES), F32), jnp.ones((N_SEG, V7X_LANES), F32))
                 for _ in range(n_lc))
    fin = lax.fori_loop(0, SEG, body, init, unroll=8)
    h_loc = jnp.concatenate([fin[c][0] for c in range(n_lc)], axis=1)
    a_tot = jnp.concatenate([fin[c][1] for c in range(n_lc)], axis=1)

    order = range(N_SEG - 1, -1, -1) if reverse else range(N_SEG)
    h_in = [None] * N_SEG
    prev = None
    for s in order:
        cur = h0[s:s + 1]
        if prev is not None:
            left = a_tot[prev:prev + 1] * h_in[prev] + h_loc[prev:prev + 1]
            cur = keep[s] * left + cur
        h_in[s] = cur
        prev = s
    h_in = jnp.concatenate(h_in, axis=0)
    return h_in, a_tot * h_in + h_loc


def _mix1_kernel(n_ctx_tiles, x_ref, mod_ref, w5_ref, ca_ref, cb_ref, cbias_ref, wbd_ref, gb_ref,
                 lam_ref, h0_ref, apre_ref, bpre_ref, st_ref,
                 u_scr, proj_scr, a_scr, b_scr, hl_scr, ac_scr, hsum_scr):
    i = pl.program_id(0)
    j = pl.program_id(1)
    is_dec = i >= n_ctx_tiles

    @pl.when(j == 0)
    def _():
        row = jnp.where(is_dec, i - (n_ctx_tiles - 1), 0)
        m = mod_ref[pl.ds(row, 1), :]
        sh1 = m[:, 0:D]
        sc1 = m[:, D:2 * D]
        u_scr[...] = (_ln_plain(x_ref[...], 1e-6) * (1.0 + sc1) + sh1).astype(BF16)

    proj_scr[...] = jnp.dot(u_scr[...], w5_ref[0], preferred_element_type=F32)

    pm = jnp.where(is_dec, GRID_W - 1, SEQ - 1)
    pos = lax.broadcasted_iota(jnp.int32, (TM1, CW), 0) & pm
    has_p1 = pos >= 1
    has_p2 = pos >= 2
    has_n1 = pos < pm

    def shifted(v, back, mask):
        return jnp.where(mask, pltpu.roll(v, back % TM1, axis=0), 0.0)

    z = proj_scr[:, 2 * CW:3 * CW] * proj_scr[:, 0:CW]
    ca = ca_ref[...]
    conv_a = (ca[0:1] * shifted(z, 1, has_p1) + ca[1:2] * z + ca[2:3] * shifted(z, -1, has_n1))
    apre_ref[...] = (proj_scr[:, CW:2 * CW] * conv_a).astype(BF16)

    xr_in = proj_scr[:, 4 * CW:5 * CW]
    cb = cb_ref[...]
    xr = (cb[0:1] * shifted(xr_in, 2, has_p2) + cb[1:2] * shifted(xr_in, 1, has_p1)
          + cb[2:3] * xr_in + cb[3:4] * shifted(xr_in, -1, has_n1) + cbias_ref[...])
    gates = jnp.dot(xr.astype(BF16), wbd_ref[0], preferred_element_type=F32)
    gb = gb_ref[...]
    lam = lam_ref[...]
    sp = jnp.maximum(-lam, 0.0) + jnp.log1p(jnp.exp(-jnp.abs(lam)))

    one = jnp.float32(1.0)
    for d in range(2):
        ga = gates[:, (2 * d) * CW:(2 * d + 1) * CW] + gb[2 * d:2 * d + 1]
        gx = gates[:, (2 * d + 1) * CW:(2 * d + 2) * CW] + gb[2 * d + 1:2 * d + 2]
        r = jax.nn.sigmoid(ga)
        ig = jax.nn.sigmoid(gx)
        a = jnp.exp((-RGLRU_C * r) * sp[d:d + 1])
        bt = jnp.sqrt(1.0 - a * a) * (ig * xr)
        for s in range(N_SEG):
            lo = s * SEG_STRIDE
            for c in range(N_LC):
                lanes = slice(c * V7X_LANES, (c + 1) * V7X_LANES)
                a_scr[c, lo:lo + SEG, :] = a[s * SEG:(s + 1) * SEG, lanes]
                b_scr[c, lo:lo + SEG, :] = bt[s * SEG:(s + 1) * SEG, lanes]
        if d == 0:
            keep = [jnp.where(is_dec, one, jnp.float32(s % 2 == 1)) for s in range(N_SEG)]
        else:
            keep = [jnp.where(is_dec, one, jnp.float32(s % 2 == 0)) for s in range(N_SEG)]
        h_in, h_out = _scan_dir(a_scr, b_scr, hl_scr, ac_scr, h0_ref[0, d], keep, reverse=(d == 1))
        st_ref[0, d] = h_out
        for s in range(N_SEG):
            lo = s * SEG_STRIDE
            for c in range(N_LC):
                lanes = slice(c * V7X_LANES, (c + 1) * V7X_LANES)
                h_seg = hl_scr[c, lo:lo + SEG, :] + ac_scr[c, lo:lo + SEG, :] * h_in[s:s + 1, lanes]
                if d == 0:
                    hsum_scr[s * SEG:(s + 1) * SEG, lanes] = h_seg
                else:
                    hsum_scr[s * SEG:(s + 1) * SEG, lanes] += h_seg

    bpre_ref[...] = (hsum_scr[...] * jax.nn.gelu(proj_scr[:, 3 * CW:4 * CW])).astype(BF16)


def _mixer_part1(x, mod_l, w5, conv_a, conv_b, conv_b_bias, wbd, gbias, lam, h0, n_ctx_tok):
    n_tok = x.shape[0]
    n_tiles = n_tok // TM1
    kern = functools.partial(_mix1_kernel, n_ctx_tok // TM1)
    return pl.pallas_call(
        kern,
        grid=(n_tiles, N_CH),
        in_specs=[
            pl.BlockSpec((TM1, D), lambda i, j: (i, 0)),
            pl.BlockSpec((MOD_ROWS, N_MOD * D), lambda i, j: (0, 0)),
            pl.BlockSpec((1, D, 5 * CW), lambda i, j: (j, 0, 0)),
            pl.BlockSpec((3, CW), lambda i, j: (0, j)),
            pl.BlockSpec((4, CW), lambda i, j: (0, j)),
            pl.BlockSpec((1, CW), lambda i, j: (0, j)),
            pl.BlockSpec((1, CW, 4 * CW), lambda i, j: (j, 0, 0)),
            pl.BlockSpec((4, CW), lambda i, j: (0, j)),
            pl.BlockSpec((2, CW), lambda i, j: (0, j)),
            pl.BlockSpec((1, 2, N_SEG, CW), lambda i, j: (i, 0, 0, j)),
        ],
        out_specs=[
            pl.BlockSpec((TM1, CW), lambda i, j: (i, j)),
            pl.BlockSpec((TM1, CW), lambda i, j: (i, j)),
            pl.BlockSpec((1, 2, N_SEG, CW), lambda i, j: (i, 0, 0, j)),
        ],
        out_shape=[
            jax.ShapeDtypeStruct((n_tok, D), BF16),
            jax.ShapeDtypeStruct((n_tok, D), BF16),
            jax.ShapeDtypeStruct((n_tiles, 2, N_SEG, D), F32),
        ],
        scratch_shapes=[
            pltpu.VMEM((TM1, D), BF16),
            pltpu.VMEM((TM1, 5 * CW), F32),
            pltpu.VMEM((N_LC, N_SEG * SEG_STRIDE, V7X_LANES), F32),
            pltpu.VMEM((N_LC, N_SEG * SEG_STRIDE, V7X_LANES), F32),
            pltpu.VMEM((N_LC, N_SEG * SEG_STRIDE, V7X_LANES), F32),
            pltpu.VMEM((N_LC, N_SEG * SEG_STRIDE, V7X_LANES), F32),
            pltpu.VMEM((TM1, CW), F32),
        ],
        compiler_params=pltpu.CompilerParams(
            dimension_semantics=("arbitrary", "arbitrary"), vmem_limit_bytes=VMEM_LIMIT),
        name="mixer_scan",
    )(x, mod_l, w5, conv_a, conv_b, conv_b_bias, wbd, gbias, lam, h0)


def _mix2_kernel(n_ctx_tok, x_ref, mod_ref, ap_ref, bp_ref, wg_ref, wa_ref, wb_ref, wo_ref,
                 g_ref, b_ref, o_ref):
    i = pl.program_id(0)
    m = mod_ref[pl.ds(_mod_row(i * TM2, n_ctx_tok), 1), :]
    sh1 = m[:, 0:D]
    sc1 = m[:, D:2 * D]
    g1 = m[:, 2 * D:3 * D]
    x = x_ref[...]
    u = (_ln_plain(x, 1e-6) * (1.0 + sc1) + sh1).astype(BF16)
    gates = jnp.dot(u, wg_ref[...], preferred_element_type=F32)
    br_a = jnp.dot(ap_ref[...], wa_ref[...], preferred_element_type=F32)
    br_b = jnp.dot(bp_ref[...], wb_ref[...], preferred_element_type=F32)
    merged = jax.nn.sigmoid(gates[:, 0:D]) * br_a + jax.nn.sigmoid(gates[:, D:2 * D]) * br_b
    mix = jnp.dot(merged.astype(BF16), wo_ref[...], preferred_element_type=F32)
    y = DN_ALPHA * x + g1 * mix
    o_ref[...] = _ln_plain(y, 1e-5) * g_ref[...] + b_ref[...]


def _mixer_part2(x, mod_l, a_pre, b_pre, wg, wa, wb, wo, ln_g, ln_b, n_ctx_tok):
    n_tok = x.shape[0]
    const = lambda i: (0, 0)
    tile = lambda i: (i, 0)
    return pl.pallas_call(
        functools.partial(_mix2_kernel, n_ctx_tok),
        grid=(n_tok // TM2,),
        in_specs=[
            pl.BlockSpec((TM2, D), tile),
            pl.BlockSpec((MOD_ROWS, N_MOD * D), const),
            pl.BlockSpec((TM2, D), tile),
            pl.BlockSpec((TM2, D), tile),
            pl.BlockSpec((D, 2 * D), const),
            pl.BlockSpec((D, D), const),
            pl.BlockSpec((D, D), const),
            pl.BlockSpec((D, D), const),
            pl.BlockSpec((1, D), const),
            pl.BlockSpec((1, D), const),
        ],
        out_specs=pl.BlockSpec((TM2, D), tile),
        out_shape=jax.ShapeDtypeStruct((n_tok, D), F32),
        compiler_params=pltpu.CompilerParams(
            dimension_semantics=("arbitrary",), vmem_limit_bytes=VMEM_LIMIT),
        name="mixer_out",
    )(x, mod_l, a_pre, b_pre, wg, wa, wb, wo, ln_g, ln_b)


def _ffn_prologue(n_ctx_tok, x_ref, mod_ref):
    m = mod_ref[pl.ds(_mod_row(pl.program_id(0) * TM3, n_ctx_tok), 1), :]
    sh2 = m[:, 3 * D:4 * D]
    sc2 = m[:, 4 * D:5 * D]
    return _ln_plain(x_ref[...], 1e-6) * (1.0 + sc2) + sh2


def _ffn_epilogue(n_ctx_tok, x_ref, mod_ref, acc, g_ref, b_ref, o_ref):
    m = mod_ref[pl.ds(_mod_row(pl.program_id(0) * TM3, n_ctx_tok), 1), :]
    g2 = m[:, 5 * D:6 * D]
    y = DN_ALPHA * x_ref[...] + g2 * acc
    o_ref[...] = _ln_plain(y, 1e-5) * g_ref[...] + b_ref[...]


def _swiglu_chunk(u, w13, w2):
    h = jnp.dot(u, w13, preferred_element_type=F32)
    h1 = h[:, 0:FC]
    hid = (h1 * jax.nn.sigmoid(h1) * h[:, FC:2 * FC]).astype(BF16)
    return jnp.dot(hid, w2, preferred_element_type=F32)


def _dense_ffn_kernel(n_ctx_tok, x_ref, mod_ref, w13_ref, w2_ref, g_ref, b_ref, o_ref, u_scr, acc_scr):
    f = pl.program_id(1)

    @pl.when(f == 0)
    def _():
        u_scr[...] = _ffn_prologue(n_ctx_tok, x_ref, mod_ref).astype(BF16)

    p = _swiglu_chunk(u_scr[...], w13_ref[0], w2_ref[...])

    @pl.when(f == 0)
    def _():
        acc_scr[...] = p

    @pl.when(f > 0)
    def _():
        acc_scr[...] += p

    @pl.when(f == N_FC - 1)
    def _():
        _ffn_epilogue(n_ctx_tok, x_ref, mod_ref, acc_scr[...], g_ref, b_ref, o_ref)


def _dense_ffn(x, mod_l, w13, w2, ln_g, ln_b, n_ctx_tok):
    n_tok = x.shape[0]
    return pl.pallas_call(
        functools.partial(_dense_ffn_kernel, n_ctx_tok),
        grid=(n_tok // TM3, N_FC),
        in_specs=[
            pl.BlockSpec((TM3, D), lambda i, f: (i, 0)),
            pl.BlockSpec((MOD_ROWS, N_MOD * D), lambda i, f: (0, 0)),
            pl.BlockSpec((1, D, 2 * FC), lambda i, f: (f, 0, 0)),
            pl.BlockSpec((FC, D), lambda i, f: (f, 0)),
            pl.BlockSpec((1, D), lambda i, f: (0, 0)),
            pl.BlockSpec((1, D), lambda i, f: (0, 0)),
        ],
        out_specs=pl.BlockSpec((TM3, D), lambda i, f: (i, 0)),
        out_shape=jax.ShapeDtypeStruct((n_tok, D), F32),
        scratch_shapes=[pltpu.VMEM((TM3, D), BF16), pltpu.VMEM((TM3, D), F32)],
        compiler_params=pltpu.CompilerParams(
            dimension_semantics=("arbitrary", "arbitrary"), vmem_limit_bytes=VMEM_LIMIT),
        name="dense_ffn",
    )(x, mod_l, w13, w2, ln_g, ln_b)


def _route_kernel(n_ctx_tok, cap, x_ref, mod_ref, wrt_ref, brt_ref, xg_ref, info_ref, tot_ref,
                  u_scr, pos_v, pos_s, cnt_s, zero_scr, sem, psem, zsem):
    i = pl.program_id(0)
    n_steps = pl.num_programs(0)
    slot = i % 2

    @pl.when(i == 0)
    def _():
        for e in range(N_EXP):
            cnt_s[e] = 0

    u2 = _ffn_prologue(n_ctx_tok, x_ref, mod_ref)
    u_scr[slot] = u2
    lg = lax.dot_general(wrt_ref[...], u2, (((1,), (1,)), ((), ())), preferred_element_type=F32,
                         precision=lax.Precision.HIGHEST) + brt_ref[:, 0:1]
    eidx = lax.broadcasted_iota(jnp.int32, lg.shape, 0).astype(F32)
    neg = jnp.float32(-jnp.inf)
    v1 = jnp.max(lg, axis=0, keepdims=True)
    i1 = jnp.min(jnp.where(lg == v1, eidx, float(N_EXP)), axis=0, keepdims=True)
    lg2 = jnp.where(eidx == i1, neg, lg)
    v2 = jnp.max(lg2, axis=0, keepdims=True)
    i2 = jnp.min(jnp.where(lg2 == v2, eidx, float(N_EXP)), axis=0, keepdims=True)
    t = jnp.exp(v2 - v1)
    w_top = 1.0 / (1.0 + t)
    m1 = eidx == i1
    m2 = eidx == i2
    member = jnp.where(jnp.logical_or(m1, m2), 1.0, 0.0)
    before = (lax.broadcasted_iota(jnp.int32, (TS, TS), 0)
              < lax.broadcasted_iota(jnp.int32, (TS, TS), 1))
    rank = jnp.dot(member.astype(BF16), jnp.where(before, 1.0, 0.0).astype(BF16),
                   preferred_element_type=F32)
    base = jnp.zeros_like(lg)
    for e in range(N_EXP):
        start = cnt_s[e]
        base = jnp.where(eidx == float(e), (e * cap + start).astype(F32), base)
        cnt_s[e] = start + jnp.sum(member[e:e + 1, :]).astype(jnp.int32)
    pos = base + rank
    pos1 = jnp.sum(jnp.where(m1, pos, 0.0), axis=0, keepdims=True)
    pos2 = jnp.sum(jnp.where(m2, pos, 0.0), axis=0, keepdims=True)
    info = jnp.concatenate(
        [pos1, pos2, w_top, t * w_top, jnp.zeros((V7X_SUBLANES - 4, TS), F32)], axis=0)
    info_ref[0] = info
    pos_v[...] = info.astype(jnp.int32)
    cp = pltpu.make_async_copy(pos_v, pos_s, psem)
    cp.start()
    cp.wait()

    def row_copy(tok, dst_row, s):
        return pltpu.make_async_copy(u_scr.at[s, pl.ds(tok, 1)], xg_ref.at[pl.ds(dst_row, 1)], sem.at[s])

    def wait_tile(s):
        for _ in range(2):
            pltpu.make_async_copy(u_scr.at[s], xg_ref.at[pl.ds(0, TS)], sem.at[s]).wait()

    @pl.when(i > 0)
    def _():
        wait_tile(1 - slot)

    def issue(tok, carry):
        row_copy(tok, pos_s[0, tok], slot).start()
        row_copy(tok, pos_s[1, tok], slot).start()
        return carry

    lax.fori_loop(0, TS, issue, 0, unroll=8)

    @pl.when(i == n_steps - 1)
    def _():
        wait_tile(slot)
        zero_scr[...] = jnp.zeros_like(zero_scr)
        for e in range(N_EXP):
            total = cnt_s[e]
            tot_ref[e] = total
            n_tail = (TR - total % TR) % TR

            def tail_copy(k):
                return pltpu.make_async_copy(zero_scr.at[pl.ds(0, 1)],
                                             xg_ref.at[pl.ds(e * cap + total + k, 1)], zsem)

            def tail_start(k, carry):
                tail_copy(k).start()
                return carry

            def tail_wait(k, carry):
                tail_copy(k).wait()
                return carry

            lax.fori_loop(0, n_tail, tail_start, 0)
            lax.fori_loop(0, n_tail, tail_wait, 0)


def _route(x, mod_l, wrt, brt, n_ctx_tok):
    n_tok = x.shape[0]
    n_tiles = n_tok // TS
    cap = n_tok + TR
    return pl.pallas_call(
        functools.partial(_route_kernel, n_ctx_tok, cap),
        grid=(n_tiles,),
        in_specs=[
            pl.BlockSpec((TS, D), lambda i: (i, 0)),
            pl.BlockSpec((MOD_ROWS, N_MOD * D), lambda i: (0, 0)),
            pl.BlockSpec((N_EXP, D), lambda i: (0, 0)),
            pl.BlockSpec((N_EXP, V7X_LANES), lambda i: (0, 0)),
        ],
        out_specs=[
            pl.BlockSpec(memory_space=pl.ANY),
            pl.BlockSpec((1, V7X_SUBLANES, TS), lambda i: (i, 0, 0)),
            pl.BlockSpec(memory_space=pltpu.SMEM),
        ],
        out_shape=[
            jax.ShapeDtypeStruct((N_EXP * cap, D), F32),
            jax.ShapeDtypeStruct((n_tiles, V7X_SUBLANES, TS), F32),
            jax.ShapeDtypeStruct((N_EXP,), jnp.int32),
        ],
        scratch_shapes=[
            pltpu.VMEM((2, TS, D), F32),
            pltpu.VMEM((V7X_SUBLANES, TS), jnp.int32),
            pltpu.SMEM((V7X_SUBLANES, TS), jnp.int32),
            pltpu.SMEM((N_EXP,), jnp.int32),
            pltpu.VMEM((V7X_SUBLANES, D), F32),
            pltpu.SemaphoreType.DMA((2,)),
            pltpu.SemaphoreType.DMA,
            pltpu.SemaphoreType.DMA,
        ],
        compiler_params=pltpu.CompilerParams(
            dimension_semantics=("arbitrary",), vmem_limit_bytes=VMEM_LIMIT),
        name="moe_route",
    )(x, mod_l, wrt, brt)


def _expert_kernel(second, exp_ref, blk_ref, new_ref, nact_ref, *refs):
    if second:
        xg_ref, w1_ref, w3_ref, w2_ref, yp_ref, o_ref, w1b, w3b, w2b = refs
    else:
        xg_ref, w1_ref, w3_ref, w2_ref, o_ref, w1b, w3b, w2b = refs
    r = pl.program_id(0)
    active = r < nact_ref[0]

    @pl.when(jnp.logical_and(active, new_ref[r] == 1))
    def _():
        w1b[...] = w1_ref[0].astype(BF16)
        w3b[...] = w3_ref[0].astype(BF16)
        w2b[...] = w2_ref[0].astype(BF16)

    @pl.when(active)
    def _():
        xb = xg_ref[...].astype(BF16)
        h1 = jnp.dot(xb, w1b[...], preferred_element_type=F32)
        h3 = jnp.dot(xb, w3b[...], preferred_element_type=F32)
        hid = (h1 * jax.nn.sigmoid(h1) * h3).astype(BF16)
        p = jnp.dot(hid, w2b[...], preferred_element_type=F32)
        if second:
            p = yp_ref[...] + p
        o_ref[...] = p


def _expert_pass(second, tables, xg, w1, w3, w2, yp=None):
    f = 1 if second else 0
    row = lambda r, ex, bk, nw, na: (bk[r], 0)
    in_specs = [
        pl.BlockSpec((TR, D), row),
        pl.BlockSpec((1, D, FC), lambda r, ex, bk, nw, na: (ex[r], 0, f)),
        pl.BlockSpec((1, D, FC), lambda r, ex, bk, nw, na: (ex[r], 0, f)),
        pl.BlockSpec((1, FC, D), lambda r, ex, bk, nw, na: (ex[r], f, 0)),
    ]
    args = [xg, w1, w3, w2]
    if second:
        in_specs.append(pl.BlockSpec((TR, D), row))
        args.append(yp)
    return pl.pallas_call(
        functools.partial(_expert_kernel, second),
        grid_spec=pltpu.PrefetchScalarGridSpec(
            num_scalar_prefetch=4,
            grid=(MAX_RT,),
            in_specs=in_specs,
            out_specs=pl.BlockSpec((TR, D), row),
            scratch_shapes=[pltpu.VMEM((D, FC), BF16), pltpu.VMEM((D, FC), BF16),
                            pltpu.VMEM((FC, D), BF16)],
        ),
        out_shape=jax.ShapeDtypeStruct(xg.shape, F32),
        compiler_params=pltpu.CompilerParams(
            dimension_semantics=("arbitrary",), vmem_limit_bytes=EXPERT_VMEM_LIMIT),
        name="moe_expert_hi" if second else "moe_expert_lo",
    )(*tables, *args)


def _combine_kernel(n_ctx_tok, x_ref, mod_ref, info_ref, yg_ref, g_ref, b_ref, o_ref,
                    ya_scr, yb_scr, pos_v, pos_s, sem, psem):
    info = info_ref[0]
    pos_v[...] = info.astype(jnp.int32)
    cp = pltpu.make_async_copy(pos_v, pos_s, psem)
    cp.start()
    cp.wait()

    def issue(tok, carry):
        pltpu.make_async_copy(yg_ref.at[pl.ds(pos_s[0, tok], 1)], ya_scr.at[pl.ds(tok, 1)], sem).start()
        pltpu.make_async_copy(yg_ref.at[pl.ds(pos_s[1, tok], 1)], yb_scr.at[pl.ds(tok, 1)], sem).start()
        return carry

    lax.fori_loop(0, TS, issue, 0, unroll=8)
    padded = jnp.concatenate([info, jnp.zeros((V7X_LANES - V7X_SUBLANES, TS), F32)], axis=0)
    cols = jnp.transpose(padded, (1, 0))
    w1c = cols[:, 2:3]
    w2c = cols[:, 3:4]
    pltpu.make_async_copy(yg_ref.at[pl.ds(0, TS)], ya_scr, sem).wait()
    pltpu.make_async_copy(yg_ref.at[pl.ds(0, TS)], yb_scr, sem).wait()
    acc = w1c * ya_scr[...] + w2c * yb_scr[...]
    _ffn_epilogue(n_ctx_tok, x_ref, mod_ref, acc, g_ref, b_ref, o_ref)


def _combine(x, mod_l, info, yg, ln_g, ln_b, n_ctx_tok):
    n_tok = x.shape[0]
    return pl.pallas_call(
        functools.partial(_combine_kernel, n_ctx_tok),
        grid=(n_tok // TS,),
        in_specs=[
            pl.BlockSpec((TS, D), lambda i: (i, 0)),
            pl.BlockSpec((MOD_ROWS, N_MOD * D), lambda i: (0, 0)),
            pl.BlockSpec((1, V7X_SUBLANES, TS), lambda i: (i, 0, 0)),
            pl.BlockSpec(memory_space=pl.ANY),
            pl.BlockSpec((1, D), lambda i: (0, 0)),
            pl.BlockSpec((1, D), lambda i: (0, 0)),
        ],
        out_specs=pl.BlockSpec((TS, D), lambda i: (i, 0)),
        out_shape=jax.ShapeDtypeStruct((n_tok, D), F32),
        scratch_shapes=[
            pltpu.VMEM((TS, D), F32),
            pltpu.VMEM((TS, D), F32),
            pltpu.VMEM((V7X_SUBLANES, TS), jnp.int32),
            pltpu.SMEM((V7X_SUBLANES, TS), jnp.int32),
            pltpu.SemaphoreType.DMA,
            pltpu.SemaphoreType.DMA,
        ],
        compiler_params=pltpu.CompilerParams(
            dimension_semantics=("arbitrary",), vmem_limit_bytes=VMEM_LIMIT),
        name="moe_combine",
    )(x, mod_l, info, yg, ln_g, ln_b)


def _row_tile_tables(totals, cap):
    n_rt = (totals + TR - 1) // TR
    cum = jnp.cumsum(n_rt)
    n_act = cum[-1]
    r = jnp.arange(MAX_RT, dtype=jnp.int32)
    rc = jnp.minimum(r, n_act - 1)
    exp = jnp.sum((rc[:, None] >= cum[None, :]).astype(jnp.int32), axis=1)
    first = cum[exp] - n_rt[exp]
    blk = exp * (cap // TR) + (rc - first)
    new = jnp.logical_and(r == first, r < n_act).astype(jnp.int32)
    return exp.astype(jnp.int32), blk.astype(jnp.int32), new, n_act.reshape(1).astype(jnp.int32)


def _moe_ffn(x, mod_l, router_w, router_b, w1, w3, w2, ln_g, ln_b, n_ctx_tok):
    n_tok = x.shape[0]
    assert 2 * n_tok // TR + N_EXP == MAX_RT and N_FC == 2
    cap = n_tok + TR
    wrt = router_w.T
    brt = jnp.broadcast_to(router_b.reshape(N_EXP, 1), (N_EXP, V7X_LANES))
    xg, info, totals = _route(x, mod_l, wrt, brt, n_ctx_tok)
    tables = _row_tile_tables(totals, cap)
    y_lo = _expert_pass(False, tables, xg, w1, w3, w2)
    y = _expert_pass(True, tables, xg, w1, w3, w2, y_lo)
    return _combine(x, mod_l, info, y, ln_g, ln_b, n_ctx_tok)


def _pack_w13(w1, w3):
    lead = w1.shape[:-2]
    nl = len(lead)
    perm = tuple(range(nl)) + (nl + 1, nl, nl + 2)
    a = w1.reshape(lead + (D, N_FC, FC)).transpose(perm)
    b = w3.reshape(lead + (D, N_FC, FC)).transpose(perm)
    return jnp.concatenate([a, b], axis=-1).astype(BF16)


def _block_diag_chunks(w):
    per = CW // HEAD_D
    w4 = w.reshape(N_CH, per, HEAD_D, HEAD_D)
    eye = jnp.eye(per, dtype=w.dtype)
    return jnp.einsum("cgij,gk->cgikj", w4, eye).reshape(N_CH, CW, CW)


def kernel(x_prompt, x_sample, state_rglru, c, c_ctx, w_mod, b_mod, w_in, conv_a, w_a_out, conv_b, conv_b_bias, w_gate_a, b_gate_a, w_gate_x, b_gate_x, lru_lambda, w_b_out, w_o, ln1_g, ln1_b, ln2_g, ln2_b, ffn_w1, ffn_w3, ffn_w2, router_w, router_b, moe_w1, moe_w3, moe_w2):
    batch, seq, d = x_prompt.shape
    dec_batch, dec_seq, _ = x_sample.shape
    depth = w_mod.shape[0]
    assert (d, seq, dec_seq, depth) == (D, SEQ, DEC_SEQ, DEPTH)
    n_ctx_tok = batch * seq
    n_dec_tok = dec_batch * dec_seq
    assert n_ctx_tok % TM1 == 0 and TM1 == dec_seq and 1 + dec_batch <= MOD_ROWS
    n_ctx_tiles = n_ctx_tok // TM1
    n_tiles = n_ctx_tiles + dec_batch
    seq_per_tile = TM1 // seq
    assert N_SEG == 2 * seq_per_tile

    x = jnp.concatenate([x_prompt.reshape(n_ctx_tok, D), x_sample.reshape(n_dec_tok, D)], axis=0)

    cond = jnp.zeros((MOD_ROWS, D), F32).at[0].set(c_ctx).at[1:1 + dec_batch].set(c)
    mod = _modulation(cond, w_mod, b_mod)

    states = []
    for l in range(depth):
        w5 = (w_in[l][:, :5 * D].reshape(D, 5, N_CH, CW).transpose(2, 0, 1, 3)
              .reshape(N_CH, D, 5 * CW).astype(BF16))
        wg = w_in[l][:, 5 * D:7 * D].astype(BF16)
        wbd = jnp.concatenate(
            [_block_diag_chunks(w_gate_a[l, 0]), _block_diag_chunks(w_gate_x[l, 0]),
             _block_diag_chunks(w_gate_a[l, 1]), _block_diag_chunks(w_gate_x[l, 1])],
            axis=-1).astype(BF16)
        gbias = jnp.stack([b_gate_a[l, 0], b_gate_x[l, 0], b_gate_a[l, 1], b_gate_x[l, 1]], axis=0)
        h0 = jnp.zeros((n_tiles, 2, N_SEG, D), F32)
        h0 = h0.at[n_ctx_tiles:, 0, 0].set(state_rglru[:, l, 0].astype(F32))
        h0 = h0.at[n_ctx_tiles:, 1, N_SEG - 1].set(state_rglru[:, l, 1].astype(F32))

        a_pre, b_pre, st = _mixer_part1(
            x, mod[l], w5, conv_a[l], conv_b[l], conv_b_bias[l].reshape(1, D), wbd, gbias,
            lru_lambda[l], h0, n_ctx_tok)
        x = _mixer_part2(
            x, mod[l], a_pre, b_pre, wg, w_a_out[l].astype(BF16), w_b_out[l].astype(BF16),
            w_o[l].astype(BF16), ln1_g[l].reshape(1, D), ln1_b[l].reshape(1, D), n_ctx_tok)

        if l % 2 == 0:
            k = l // 2
            x = _dense_ffn(x, mod[l], _pack_w13(ffn_w1[k], ffn_w3[k]), ffn_w2[k].astype(BF16),
                           ln2_g[l].reshape(1, D), ln2_b[l].reshape(1, D), n_ctx_tok)
        else:
            k = l // 2
            x = _moe_ffn(x, mod[l], router_w[k], router_b[k], moe_w1[k], moe_w3[k], moe_w2[k],
                         ln2_g[l].reshape(1, D), ln2_b[l].reshape(1, D), n_ctx_tok)

        st_ctx = st[:n_ctx_tiles]
        fwd = st_ctx[:, 0, 1::2].reshape(batch, D)
        bwd = st_ctx[:, 1, 0::2].reshape(batch, D)
        states.append(jnp.stack([fwd, bwd], axis=1))

    y_prompt = x[:n_ctx_tok].reshape(batch, seq, D)
    y_sample = x[n_ctx_tok:].reshape(dec_batch, dec_seq, D)
    new_state = jnp.stack(states, axis=1).astype(x_prompt.dtype)
    return (y_prompt, y_sample, new_state)
```

```python
import functools

import jax
import jax.numpy as jnp
from jax import lax
from jax.experimental import pallas as pl
from jax.experimental.pallas import tpu as pltpu

F32 = jnp.float32
BF16 = jnp.bfloat16

D = 1024
SEQ = 256
DEC_SEQ = 1024
GRID_W = 64
N_HEAD = 16
HEAD_D = D // N_HEAD
RGLRU_C = 8.0
D_FF = 2816
N_EXP = 8
N_MOD = 6
DEPTH = 2
DN_ALPHA = (2.0 * DEPTH) ** 0.25

V7X_SUBLANES = 8
V7X_LANES = 128
V7X_VMEM_BYTES = 64 * 1024 * 1024
VMEM_LIMIT = V7X_VMEM_BYTES - 12 * 1024 * 1024

TM1 = 1024
CW = 256
N_CH = D // CW
N_LC = CW // V7X_LANES
N_SEG = V7X_SUBLANES
SEG = TM1 // N_SEG
SEG_STRIDE = SEG + 8
TM2 = 512
TM3 = 512
FC = 1408
N_FC = D_FF // FC
TS = 512
TR = 256
MAX_RT = 2 * 10240 // TR + N_EXP
EXPERT_VMEM_LIMIT = V7X_VMEM_BYTES - 6 * 1024 * 1024
MOD_ROWS = 8
MOD_BLK = 1536


def _ln_plain(x, eps):
    mu = jnp.mean(x, axis=-1, keepdims=True)
    xc = x - mu
    var = jnp.mean(xc * xc, axis=-1, keepdims=True)
    return xc * lax.rsqrt(var + eps)


def _token_source(xs, tile, n_ctx_tok):
    ncb = n_ctx_tok // tile
    if isinstance(xs, tuple):
        x_ctx, x_dec = xs
        dec_off = 0
    else:
        x_ctx = x_dec = xs
        dec_off = ncb
    ctx_map = lambda i: (jnp.minimum(i, ncb - 1), 0)
    dec_map = lambda i: (jnp.maximum(i - ncb, 0) + dec_off, 0)
    return x_ctx, x_dec, ctx_map, dec_map


def _mod_row(tok0, n_ctx_tok):
    dec = jnp.maximum(tok0 - n_ctx_tok, 0) // DEC_SEQ
    return jnp.where(tok0 >= n_ctx_tok, 1 + dec, 0)


def _mod_kernel(cond_ref, w_ref, b_ref, o_ref):
    cnd = cond_ref[...]
    s = cnd * jax.nn.sigmoid(cnd)
    o_ref[0] = jnp.dot(s, w_ref[0], preferred_element_type=F32,
                       precision=lax.Precision.HIGHEST) + b_ref[0]


def _modulation(cond, w_mod, b_mod):
    depth = w_mod.shape[0]
    n_out = w_mod.shape[2]
    return pl.pallas_call(
        _mod_kernel,
        grid=(depth, n_out // MOD_BLK),
        in_specs=[
            pl.BlockSpec((MOD_ROWS, D), lambda l, j: (0, 0)),
            pl.BlockSpec((1, D, MOD_BLK), lambda l, j: (l, 0, j)),
            pl.BlockSpec((1, 1, MOD_BLK), lambda l, j: (l, 0, j)),
        ],
        out_specs=pl.BlockSpec((1, MOD_ROWS, MOD_BLK), lambda l, j: (l, 0, j)),
        out_shape=jax.ShapeDtypeStruct((depth, MOD_ROWS, n_out), F32),
        compiler_params=pltpu.CompilerParams(
            dimension_semantics=("arbitrary", "arbitrary"), vmem_limit_bytes=VMEM_LIMIT),
        name="modulation",
    )(cond, w_mod, b_mod.reshape(depth, 1, n_out))


def _scan_dir(a_scr, b_scr, hl_scr, ac_scr, h0, keep, reverse):
    n_lc = a_scr.shape[0]

    def body(k, carry):
        kk = SEG - 1 - k if reverse else k
        idx = pl.ds(kk, N_SEG, stride=SEG_STRIDE)
        new = []
        for c in range(n_lc):
            h, acc = carry[c]
            a_k = a_scr[c, idx, :]
            h = a_k * h + b_scr[c, idx, :]
            acc = a_k * acc
            hl_scr[c, idx, :] = h
            ac_scr[c, idx, :] = acc
            new.append((h, acc))
        return tuple(new)

    init = tuple((jnp.zeros((N_SEG, V7X_LANES), F32), jnp.ones((N_SEG, V7X_LANES), F32))
                 for _ in range(n_lc))
    fin = lax.fori_loop(0, SEG, body, init, unroll=8)
    h_loc = jnp.concatenate([fin[c][0] for c in range(n_lc)], axis=1)
    a_tot = jnp.concatenate([fin[c][1] for c in range(n_lc)], axis=1)

    order = range(N_SEG - 1, -1, -1) if reverse else range(N_SEG)
    h_in = [None] * N_SEG
    prev = None
    for s in order:
        cur = h0[s:s + 1]
        if prev is not None:
            left = a_tot[prev:prev + 1] * h_in[prev] + h_loc[prev:prev + 1]
            cur = keep[s] * left + cur
        h_in[s] = cur
        prev = s
    h_in = jnp.concatenate(h_in, axis=0)
    return h_in, a_tot * h_in + h_loc


def _mix1_kernel(n_ctx_tiles, xc_ref, xd_ref, mod_ref, w5_ref, ca_ref, cb_ref, cbias_ref, wbd_ref, gb_ref,
                 lam_ref, h0_ref, apre_ref, bpre_ref, st_ref,
                 u_scr, proj_scr, a_scr, b_scr, hl_scr, ac_scr, hsum_scr):
    i = pl.program_id(0)
    j = pl.program_id(1)
    is_dec = i >= n_ctx_tiles

    @pl.when(j == 0)
    def _():
        row = jnp.where(is_dec, i - (n_ctx_tiles - 1), 0)
        m = mod_ref[pl.ds(row, 1), :]
        sh1 = m[:, 0:D]
        sc1 = m[:, D:2 * D]
        x = jnp.where(is_dec, xd_ref[...], xc_ref[...])
        u_scr[...] = (_ln_plain(x, 1e-6) * (1.0 + sc1) + sh1).astype(BF16)

    proj_scr[...] = jnp.dot(u_scr[...], w5_ref[0], preferred_element_type=F32)

    pm = jnp.where(is_dec, GRID_W - 1, SEQ - 1)
    pos = lax.broadcasted_iota(jnp.int32, (TM1, CW), 0) & pm
    has_p1 = pos >= 1
    has_p2 = pos >= 2
    has_n1 = pos < pm

    def shifted(v, back, mask):
        return jnp.where(mask, pltpu.roll(v, back % TM1, axis=0), 0.0)

    z = proj_scr[:, 2 * CW:3 * CW] * proj_scr[:, 0:CW]
    ca = ca_ref[...]
    conv_a = (ca[0:1] * shifted(z, 1, has_p1) + ca[1:2] * z + ca[2:3] * shifted(z, -1, has_n1))
    apre_ref[...] = (proj_scr[:, CW:2 * CW] * conv_a).astype(BF16)

    xr_in = proj_scr[:, 4 * CW:5 * CW]
    cb = cb_ref[...]
    xr = (cb[0:1] * shifted(xr_in, 2, has_p2) + cb[1:2] * shifted(xr_in, 1, has_p1)
          + cb[2:3] * xr_in + cb[3:4] * shifted(xr_in, -1, has_n1) + cbias_ref[...])
    gates = jnp.dot(xr.astype(BF16), wbd_ref[0], preferred_element_type=F32)
    gb = gb_ref[...]
    lam = lam_ref[...]
    sp = jnp.maximum(-lam, 0.0) + jnp.log1p(jnp.exp(-jnp.abs(lam)))

    one = jnp.float32(1.0)
    for d in range(2):
        ga = gates[:, (2 * d) * CW:(2 * d + 1) * CW] + gb[2 * d:2 * d + 1]
        gx = gates[:, (2 * d + 1) * CW:(2 * d + 2) * CW] + gb[2 * d + 1:2 * d + 2]
        r = jax.nn.sigmoid(ga)
        ig = jax.nn.sigmoid(gx)
        a = jnp.exp((-RGLRU_C * r) * sp[d:d + 1])
        bt = jnp.sqrt(1.0 - a * a) * (ig * xr)
        for s in range(N_SEG):
            lo = s * SEG_STRIDE
            for c in range(N_LC):
                lanes = slice(c * V7X_LANES, (c + 1) * V7X_LANES)
                a_scr[c, lo:lo + SEG, :] = a[s * SEG:(s + 1) * SEG, lanes]
                b_scr[c, lo:lo + SEG, :] = bt[s * SEG:(s + 1) * SEG, lanes]
        if d == 0:
            keep = [jnp.where(is_dec, one, jnp.float32(s % 2 == 1)) for s in range(N_SEG)]
        else:
            keep = [jnp.where(is_dec, one, jnp.float32(s % 2 == 0)) for s in range(N_SEG)]
        h_in, h_out = _scan_dir(a_scr, b_scr, hl_scr, ac_scr, h0_ref[0, d], keep, reverse=(d == 1))
        st_ref[0, d] = h_out
        for s in range(N_SEG):
            lo = s * SEG_STRIDE
            for c in range(N_LC):
                lanes = slice(c * V7X_LANES, (c + 1) * V7X_LANES)
                h_seg = hl_scr[c, lo:lo + SEG, :] + ac_scr[c, lo:lo + SEG, :] * h_in[s:s + 1, lanes]
                if d == 0:
                    hsum_scr[s * SEG:(s + 1) * SEG, lanes] = h_seg
                else:
                    hsum_scr[s * SEG:(s + 1) * SEG, lanes] += h_seg

    bpre_ref[...] = (hsum_scr[...] * jax.nn.gelu(proj_scr[:, 3 * CW:4 * CW])).astype(BF16)


def _mixer_part1(xs, mod_l, w5, conv_a, conv_b, conv_b_bias, wbd, gbias, lam, h0, n_tok, n_ctx_tok):
    n_tiles = n_tok // TM1
    x_ctx, x_dec, ctx_map, dec_map = _token_source(xs, TM1, n_ctx_tok)
    kern = functools.partial(_mix1_kernel, n_ctx_tok // TM1)
    return pl.pallas_call(
        kern,
        grid=(n_tiles, N_CH),
        in_specs=[
            pl.BlockSpec((TM1, D), lambda i, j: ctx_map(i)),
            pl.BlockSpec((TM1, D), lambda i, j: dec_map(i)),
            pl.BlockSpec((MOD_ROWS, N_MOD * D), lambda i, j: (0, 0)),
            pl.BlockSpec((1, D, 5 * CW), lambda i, j: (j, 0, 0)),
            pl.BlockSpec((3, CW), lambda i, j: (0, j)),
            pl.BlockSpec((4, CW), lambda i, j: (0, j)),
            pl.BlockSpec((1, CW), lambda i, j: (0, j)),
            pl.BlockSpec((1, CW, 4 * CW), lambda i, j: (j, 0, 0)),
            pl.BlockSpec((4, CW), lambda i, j: (0, j)),
            pl.BlockSpec((2, CW), lambda i, j: (0, j)),
            pl.BlockSpec((1, 2, N_SEG, CW), lambda i, j: (i, 0, 0, j)),
        ],
        out_specs=[
            pl.BlockSpec((TM1, CW), lambda i, j: (i, j)),
            pl.BlockSpec((TM1, CW), lambda i, j: (i, j)),
            pl.BlockSpec((1, 2, N_SEG, CW), lambda i, j: (i, 0, 0, j)),
        ],
        out_shape=[
            jax.ShapeDtypeStruct((n_tok, D), BF16),
            jax.ShapeDtypeStruct((n_tok, D), BF16),
            jax.ShapeDtypeStruct((n_tiles, 2, N_SEG, D), F32),
        ],
        scratch_shapes=[
            pltpu.VMEM((TM1, D), BF16),
            pltpu.VMEM((TM1, 5 * CW), F32),
            pltpu.VMEM((N_LC, N_SEG * SEG_STRIDE, V7X_LANES), F32),
            pltpu.VMEM((N_LC, N_SEG * SEG_STRIDE, V7X_LANES), F32),
            pltpu.VMEM((N_LC, N_SEG * SEG_STRIDE, V7X_LANES), F32),
            pltpu.VMEM((N_LC, N_SEG * SEG_STRIDE, V7X_LANES), F32),
            pltpu.VMEM((TM1, CW), F32),
        ],
        compiler_params=pltpu.CompilerParams(
            dimension_semantics=("arbitrary", "arbitrary"), vmem_limit_bytes=VMEM_LIMIT),
        name="mixer_scan",
    )(x_ctx, x_dec, mod_l, w5, conv_a, conv_b, conv_b_bias, wbd, gbias, lam, h0)


def _mix2_kernel(n_ctx_tok, xc_ref, xd_ref, mod_ref, ap_ref, bp_ref, wg_ref, wa_ref, wb_ref, wo_ref,
                 g_ref, b_ref, o_ref):
    i = pl.program_id(0)
    m = mod_ref[pl.ds(_mod_row(i * TM2, n_ctx_tok), 1), :]
    sh1 = m[:, 0:D]
    sc1 = m[:, D:2 * D]
    g1 = m[:, 2 * D:3 * D]
    x = jnp.where(i * TM2 >= n_ctx_tok, xd_ref[...], xc_ref[...])
    u = (_ln_plain(x, 1e-6) * (1.0 + sc1) + sh1).astype(BF16)
    gates = jnp.dot(u, wg_ref[...], preferred_element_type=F32)
    br_a = jnp.dot(ap_ref[...], wa_ref[...], preferred_element_type=F32)
    br_b = jnp.dot(bp_ref[...], wb_ref[...], preferred_element_type=F32)
    merged = jax.nn.sigmoid(gates[:, 0:D]) * br_a + jax.nn.sigmoid(gates[:, D:2 * D]) * br_b
    mix = jnp.dot(merged.astype(BF16), wo_ref[...], preferred_element_type=F32)
    y = DN_ALPHA * x + g1 * mix
    o_ref[...] = _ln_plain(y, 1e-5) * g_ref[...] + b_ref[...]


def _mixer_part2(xs, mod_l, a_pre, b_pre, wg, wa, wb, wo, ln_g, ln_b, n_ctx_tok):
    n_tok = a_pre.shape[0]
    x_ctx, x_dec, ctx_map, dec_map = _token_source(xs, TM2, n_ctx_tok)
    const = lambda i: (0, 0)
    tile = lambda i: (i, 0)
    return pl.pallas_call(
        functools.partial(_mix2_kernel, n_ctx_tok),
        grid=(n_tok // TM2,),
        in_specs=[
            pl.BlockSpec((TM2, D), ctx_map),
            pl.BlockSpec((TM2, D), dec_map),
            pl.BlockSpec((MOD_ROWS, N_MOD * D), const),
            pl.BlockSpec((TM2, D), tile),
            pl.BlockSpec((TM2, D), tile),
            pl.BlockSpec((D, 2 * D), const),
            pl.BlockSpec((D, D), const),
            pl.BlockSpec((D, D), const),
            pl.BlockSpec((D, D), const),
            pl.BlockSpec((1, D), const),
            pl.BlockSpec((1, D), const),
        ],
        out_specs=pl.BlockSpec((TM2, D), tile),
        out_shape=jax.ShapeDtypeStruct((n_tok, D), F32),
        compiler_params=pltpu.CompilerParams(
            dimension_semantics=("arbitrary",), vmem_limit_bytes=VMEM_LIMIT),
        name="mixer_out",
    )(x_ctx, x_dec, mod_l, a_pre, b_pre, wg, wa, wb, wo, ln_g, ln_b)


def _ffn_prologue(n_ctx_tok, x_ref, mod_ref):
    m = mod_ref[pl.ds(_mod_row(pl.program_id(0) * TM3, n_ctx_tok), 1), :]
    sh2 = m[:, 3 * D:4 * D]
    sc2 = m[:, 4 * D:5 * D]
    return _ln_plain(x_ref[...], 1e-6) * (1.0 + sc2) + sh2


def _ffn_epilogue(n_ctx_tok, x_ref, mod_ref, acc, g_ref, b_ref):
    m = mod_ref[pl.ds(_mod_row(pl.program_id(0) * TM3, n_ctx_tok), 1), :]
    g2 = m[:, 5 * D:6 * D]
    y = DN_ALPHA * x_ref[...] + g2 * acc
    return _ln_plain(y, 1e-5) * g_ref[...] + b_ref[...]


def _swiglu_chunk(u, w13, w2):
    h = jnp.dot(u, w13, preferred_element_type=F32)
    h1 = h[:, 0:FC]
    hid = (h1 * jax.nn.sigmoid(h1) * h[:, FC:2 * FC]).astype(BF16)
    return jnp.dot(hid, w2, preferred_element_type=F32)


def _dense_ffn_kernel(n_ctx_tok, x_ref, mod_ref, w13_ref, w2_ref, g_ref, b_ref, o_ref, u_scr, acc_scr):
    f = pl.program_id(1)

    @pl.when(f == 0)
    def _():
        u_scr[...] = _ffn_prologue(n_ctx_tok, x_ref, mod_ref).astype(BF16)

    p = _swiglu_chunk(u_scr[...], w13_ref[0], w2_ref[...])

    @pl.when(f == 0)
    def _():
        acc_scr[...] = p

    @pl.when(f > 0)
    def _():
        acc_scr[...] += p

    @pl.when(f == N_FC - 1)
    def _():
        o_ref[...] = _ffn_epilogue(n_ctx_tok, x_ref, mod_ref, acc_scr[...], g_ref, b_ref)


def _dense_ffn(x, mod_l, w13, w2, ln_g, ln_b, n_ctx_tok):
    n_tok = x.shape[0]
    return pl.pallas_call(
        functools.partial(_dense_ffn_kernel, n_ctx_tok),
        grid=(n_tok // TM3, N_FC),
        in_specs=[
            pl.BlockSpec((TM3, D), lambda i, f: (i, 0)),
            pl.BlockSpec((MOD_ROWS, N_MOD * D), lambda i, f: (0, 0)),
            pl.BlockSpec((1, D, 2 * FC), lambda i, f: (f, 0, 0)),
            pl.BlockSpec((FC, D), lambda i, f: (f, 0)),
            pl.BlockSpec((1, D), lambda i, f: (0, 0)),
            pl.BlockSpec((1, D), lambda i, f: (0, 0)),
        ],
        out_specs=pl.BlockSpec((TM3, D), lambda i, f: (i, 0)),
        out_shape=jax.ShapeDtypeStruct((n_tok, D), F32),
        scratch_shapes=[pltpu.VMEM((TM3, D), BF16), pltpu.VMEM((TM3, D), F32)],
        compiler_params=pltpu.CompilerParams(
            dimension_semantics=("arbitrary", "arbitrary"), vmem_limit_bytes=VMEM_LIMIT),
        name="dense_ffn",
    )(x, mod_l, w13, w2, ln_g, ln_b)


def _rank_kernel(n_ctx_tok, x_ref, mod_ref, wrt_ref, brt_ref, info_ref, cnt_ref):
    i = pl.program_id(0)
    u2 = _ffn_prologue(n_ctx_tok, x_ref, mod_ref)
    lg = lax.dot_general(wrt_ref[...], u2, (((1,), (1,)), ((), ())), preferred_element_type=F32,
                         precision=lax.Precision.HIGHEST) + brt_ref[:, 0:1]
    eidx = lax.broadcasted_iota(jnp.int32, lg.shape, 0).astype(F32)
    neg = jnp.float32(-jnp.inf)
    v1 = jnp.max(lg, axis=0, keepdims=True)
    i1 = jnp.min(jnp.where(lg == v1, eidx, float(N_EXP)), axis=0, keepdims=True)
    lg2 = jnp.where(eidx == i1, neg, lg)
    v2 = jnp.max(lg2, axis=0, keepdims=True)
    i2 = jnp.min(jnp.where(lg2 == v2, eidx, float(N_EXP)), axis=0, keepdims=True)
    t = jnp.exp(v2 - v1)
    w_top = 1.0 / (1.0 + t)
    m1 = eidx == i1
    m2 = eidx == i2
    member = jnp.where(jnp.logical_or(m1, m2), 1.0, 0.0)
    before = (lax.broadcasted_iota(jnp.int32, (TS, TS), 0)
              < lax.broadcasted_iota(jnp.int32, (TS, TS), 1))
    rank = jnp.dot(member.astype(BF16), jnp.where(before, 1.0, 0.0).astype(BF16),
                   preferred_element_type=F32)
    for e in range(N_EXP):
        cnt_ref[i, e] = jnp.sum(member[e:e + 1, :]).astype(jnp.int32)
    rank1 = jnp.sum(jnp.where(m1, rank, 0.0), axis=0, keepdims=True)
    rank2 = jnp.sum(jnp.where(m2, rank, 0.0), axis=0, keepdims=True)
    info_ref[0] = jnp.concatenate(
        [i1, i2, rank1, rank2, w_top, t * w_top, jnp.zeros((V7X_SUBLANES - 6, TS), F32)], axis=0)


def _rank(x, mod_l, wrt, brt, n_ctx_tok):
    n_tiles = x.shape[0] // TS
    return pl.pallas_call(
        functools.partial(_rank_kernel, n_ctx_tok),
        grid=(n_tiles,),
        in_specs=[
            pl.BlockSpec((TS, D), lambda i: (i, 0)),
            pl.BlockSpec((MOD_ROWS, N_MOD * D), lambda i: (0, 0)),
            pl.BlockSpec((N_EXP, D), lambda i: (0, 0)),
            pl.BlockSpec((N_EXP, V7X_LANES), lambda i: (0, 0)),
        ],
        out_specs=[
            pl.BlockSpec((1, V7X_SUBLANES, TS), lambda i: (i, 0, 0)),
            pl.BlockSpec(memory_space=pltpu.SMEM),
        ],
        out_shape=[
            jax.ShapeDtypeStruct((n_tiles, V7X_SUBLANES, TS), F32),
            jax.ShapeDtypeStruct((n_tiles, N_EXP), jnp.int32),
        ],
        compiler_params=pltpu.CompilerParams(
            dimension_semantics=("arbitrary",), vmem_limit_bytes=VMEM_LIMIT),
        name="moe_rank",
    )(x, mod_l, wrt, brt)


def _dispatch_kernel(n_ctx_tok, off_ref, fill_ref, x_ref, mod_ref, rinfo_ref, xg_ref, info_ref,
                     u_scr, pos_v, pos_s, zero_scr, sem, psem, zsem):
    i = pl.program_id(0)
    n_steps = pl.num_programs(0)
    slot = i % 2
    u_scr[slot] = _ffn_prologue(n_ctx_tok, x_ref, mod_ref)
    rinfo = rinfo_ref[0]
    i1 = rinfo[0:1]
    i2 = rinfo[1:2]
    off1 = jnp.zeros_like(i1)
    off2 = jnp.zeros_like(i2)
    for e in range(N_EXP):
        start = off_ref[i * N_EXP + e].astype(F32)
        off1 = jnp.where(i1 == float(e), start, off1)
        off2 = jnp.where(i2 == float(e), start, off2)
    info = jnp.concatenate(
        [off1 + rinfo[2:3], off2 + rinfo[3:4], rinfo[4:6], jnp.zeros((V7X_SUBLANES - 4, TS), F32)],
        axis=0)
    info_ref[0] = info
    pos_v[...] = info.astype(jnp.int32)
    cp = pltpu.make_async_copy(pos_v, pos_s, psem)
    cp.start()
    cp.wait()

    def row_copy(tok, dst_row, s):
        return pltpu.make_async_copy(u_scr.at[s, pl.ds(tok, 1)], xg_ref.at[pl.ds(dst_row, 1)], sem.at[s])

    def wait_tile(s):
        for _ in range(2):
            pltpu.make_async_copy(u_scr.at[s], xg_ref.at[pl.ds(0, TS)], sem.at[s]).wait()

    @pl.when(i > 0)
    def _():
        wait_tile(1 - slot)

    def issue(tok, carry):
        row_copy(tok, pos_s[0, tok], slot).start()
        row_copy(tok, pos_s[1, tok], slot).start()
        return carry

    lax.fori_loop(0, TS, issue, 0, unroll=8)

    @pl.when(i == n_steps - 1)
    def _():
        wait_tile(slot)
        zero_scr[...] = jnp.zeros_like(zero_scr)
        for e in range(N_EXP):
            end = fill_ref[e]
            n_tail = fill_ref[N_EXP + e]

            def tail_copy(k):
                return pltpu.make_async_copy(zero_scr.at[pl.ds(0, 1)], xg_ref.at[pl.ds(end + k, 1)], zsem)

            def tail_start(k, carry):
                tail_copy(k).start()
                return carry

            def tail_wait(k, carry):
                tail_copy(k).wait()
                return carry

            lax.fori_loop(0, n_tail, tail_start, 0)
            lax.fori_loop(0, n_tail, tail_wait, 0)

        def tile_copy(k):
            return pltpu.make_async_copy(zero_scr, xg_ref.at[pl.ds(pl.multiple_of(k * TR, TR), TR)], zsem)

        def tile_start(k, carry):
            tile_copy(k).start()
            return carry

        def tile_wait(k, carry):
            tile_copy(k).wait()
            return carry

        lax.fori_loop(fill_ref[2 * N_EXP], MAX_RT, tile_start, 0)
        lax.fori_loop(fill_ref[2 * N_EXP], MAX_RT, tile_wait, 0)


def _dispatch(x, mod_l, rinfo, off, fill, n_ctx_tok):
    n_tok = x.shape[0]
    n_tiles = n_tok // TS
    return pl.pallas_call(
        functools.partial(_dispatch_kernel, n_ctx_tok),
        grid_spec=pltpu.PrefetchScalarGridSpec(
            num_scalar_prefetch=2,
            grid=(n_tiles,),
            in_specs=[
                pl.BlockSpec((TS, D), lambda i, off, fill: (i, 0)),
                pl.BlockSpec((MOD_ROWS, N_MOD * D), lambda i, off, fill: (0, 0)),
                pl.BlockSpec((1, V7X_SUBLANES, TS), lambda i, off, fill: (i, 0, 0)),
            ],
            out_specs=[
                pl.BlockSpec(memory_space=pl.ANY),
                pl.BlockSpec((1, V7X_SUBLANES, TS), lambda i, off, fill: (i, 0, 0)),
            ],
            scratch_shapes=[
                pltpu.VMEM((2, TS, D), F32),
                pltpu.VMEM((V7X_SUBLANES, TS), jnp.int32),
                pltpu.SMEM((V7X_SUBLANES, TS), jnp.int32),
                pltpu.VMEM((TR, D), F32),
                pltpu.SemaphoreType.DMA((2,)),
                pltpu.SemaphoreType.DMA,
                pltpu.SemaphoreType.DMA,
            ],
        ),
        out_shape=[
            jax.ShapeDtypeStruct((MAX_RT * TR, D), F32),
            jax.ShapeDtypeStruct((n_tiles, V7X_SUBLANES, TS), F32),
        ],
        compiler_params=pltpu.CompilerParams(
            dimension_semantics=("arbitrary",), vmem_limit_bytes=VMEM_LIMIT),
        name="moe_dispatch",
    )(off, fill, x, mod_l, rinfo)


def _expert_kernel(second, exp_ref, new_ref, nact_ref, *refs):
    if second:
        xg_ref, w1_ref, w3_ref, w2_ref, yp_ref, o_ref, w1b, w3b, w2b = refs
    else:
        xg_ref, w1_ref, w3_ref, w2_ref, o_ref, w1b, w3b, w2b = refs
    r = pl.program_id(0)
    active = r < nact_ref[0]

    @pl.when(jnp.logical_and(active, new_ref[r] == 1))
    def _():
        w1b[...] = w1_ref[0].astype(BF16)
        w3b[...] = w3_ref[0].astype(BF16)
        w2b[...] = w2_ref[0].astype(BF16)

    @pl.when(active)
    def _():
        xb = xg_ref[...].astype(BF16)
        h1 = jnp.dot(xb, w1b[...], preferred_element_type=F32)
        h3 = jnp.dot(xb, w3b[...], preferred_element_type=F32)
        hid = (h1 * jax.nn.sigmoid(h1) * h3).astype(BF16)
        p = jnp.dot(hid, w2b[...], preferred_element_type=F32)
        if second:
            p = yp_ref[...] + p
        o_ref[...] = p

    @pl.when(jnp.logical_not(active))
    def _():
        o_ref[...] = jnp.zeros_like(o_ref)


def _expert_pass(second, tables, xg, w1, w3, w2, yp=None):
    f = 1 if second else 0
    row = lambda r, ex, nw, na: (r, 0)
    in_specs = [
        pl.BlockSpec((TR, D), row),
        pl.BlockSpec((1, D, FC), lambda r, ex, nw, na: (ex[r], 0, f)),
        pl.BlockSpec((1, D, FC), lambda r, ex, nw, na: (ex[r], 0, f)),
        pl.BlockSpec((1, FC, D), lambda r, ex, nw, na: (ex[r], f, 0)),
    ]
    args = [xg, w1, w3, w2]
    if second:
        in_specs.append(pl.BlockSpec((TR, D), row))
        args.append(yp)
    return pl.pallas_call(
        functools.partial(_expert_kernel, second),
        grid_spec=pltpu.PrefetchScalarGridSpec(
            num_scalar_prefetch=3,
            grid=(MAX_RT,),
            in_specs=in_specs,
            out_specs=pl.BlockSpec((TR, D), row),
            scratch_shapes=[pltpu.VMEM((D, FC), BF16), pltpu.VMEM((D, FC), BF16),
                            pltpu.VMEM((FC, D), BF16)],
        ),
        out_shape=jax.ShapeDtypeStruct(xg.shape, F32),
        compiler_params=pltpu.CompilerParams(
            dimension_semantics=("arbitrary",), vmem_limit_bytes=EXPERT_VMEM_LIMIT),
        name="moe_expert_hi" if second else "moe_expert_lo",
    )(*tables, *args)


def _combine_kernel(n_ctx_tok, x_ref, mod_ref, info_ref, yg_ref, g_ref, b_ref, oc_ref, od_ref,
                    ya_scr, yb_scr, pos_v, pos_s, sem, psem):
    info = info_ref[0]
    pos_v[...] = info.astype(jnp.int32)
    cp = pltpu.make_async_copy(pos_v, pos_s, psem)
    cp.start()
    cp.wait()

    def issue(tok, carry):
        pltpu.make_async_copy(yg_ref.at[pl.ds(pos_s[0, tok], 1)], ya_scr.at[pl.ds(tok, 1)], sem).start()
        pltpu.make_async_copy(yg_ref.at[pl.ds(pos_s[1, tok], 1)], yb_scr.at[pl.ds(tok, 1)], sem).start()
        return carry

    lax.fori_loop(0, TS, issue, 0, unroll=8)
    padded = jnp.concatenate([info, jnp.zeros((V7X_LANES - V7X_SUBLANES, TS), F32)], axis=0)
    cols = jnp.transpose(padded, (1, 0))
    w1c = cols[:, 2:3]
    w2c = cols[:, 3:4]
    pltpu.make_async_copy(yg_ref.at[pl.ds(0, TS)], ya_scr, sem).wait()
    pltpu.make_async_copy(yg_ref.at[pl.ds(0, TS)], yb_scr, sem).wait()
    acc = w1c * ya_scr[...] + w2c * yb_scr[...]
    res = _ffn_epilogue(n_ctx_tok, x_ref, mod_ref, acc, g_ref, b_ref)
    is_dec = pl.program_id(0) * TS >= n_ctx_tok

    @pl.when(jnp.logical_not(is_dec))
    def _():
        oc_ref[...] = res

    @pl.when(is_dec)
    def _():
        od_ref[...] = res


def _combine(x, mod_l, info, yg, ln_g, ln_b, n_ctx_tok):
    n_tok = x.shape[0]
    ncb = n_ctx_tok // TS
    return pl.pallas_call(
        functools.partial(_combine_kernel, n_ctx_tok),
        grid=(n_tok // TS,),
        in_specs=[
            pl.BlockSpec((TS, D), lambda i: (i, 0)),
            pl.BlockSpec((MOD_ROWS, N_MOD * D), lambda i: (0, 0)),
            pl.BlockSpec((1, V7X_SUBLANES, TS), lambda i: (i, 0, 0)),
            pl.BlockSpec(memory_space=pl.ANY),
            pl.BlockSpec((1, D), lambda i: (0, 0)),
            pl.BlockSpec((1, D), lambda i: (0, 0)),
        ],
        out_specs=[
            pl.BlockSpec((TS, D), lambda i: (jnp.minimum(i, ncb - 1), 0)),
            pl.BlockSpec((TS, D), lambda i: (jnp.maximum(i - ncb, 0), 0)),
        ],
        out_shape=[
            jax.ShapeDtypeStruct((n_ctx_tok, D), F32),
            jax.ShapeDtypeStruct((n_tok - n_ctx_tok, D), F32),
        ],
        scratch_shapes=[
            pltpu.VMEM((TS, D), F32),
            pltpu.VMEM((TS, D), F32),
            pltpu.VMEM((V7X_SUBLANES, TS), jnp.int32),
            pltpu.SMEM((V7X_SUBLANES, TS), jnp.int32),
            pltpu.SemaphoreType.DMA,
            pltpu.SemaphoreType.DMA,
        ],
        compiler_params=pltpu.CompilerParams(
            dimension_semantics=("arbitrary",), vmem_limit_bytes=VMEM_LIMIT),
        name="moe_combine",
    )(x, mod_l, info, yg, ln_g, ln_b)


def _routing_tables(counts):
    totals = jnp.sum(counts, axis=0)
    n_rt = (totals + TR - 1) // TR
    cum = jnp.cumsum(n_rt)
    first = cum - n_rt
    n_act = cum[-1]
    off = first[None, :] * TR + (jnp.cumsum(counts, axis=0) - counts)
    fill = jnp.concatenate([first * TR + totals, n_rt * TR - totals, n_act[None]])
    r = jnp.arange(MAX_RT, dtype=jnp.int32)
    rc = jnp.minimum(r, n_act - 1)
    exp = jnp.sum((rc[:, None] >= cum[None, :]).astype(jnp.int32), axis=1)
    new = jnp.logical_and(r == first[exp], r < n_act)
    i32 = lambda a: a.astype(jnp.int32)
    return i32(off.reshape(-1)), i32(fill), (i32(exp), i32(new), i32(n_act.reshape(1)))


def _moe_ffn(x, mod_l, router_w, router_b, w1, w3, w2, ln_g, ln_b, n_ctx_tok):
    n_tok = x.shape[0]
    assert 2 * n_tok // TR + N_EXP == MAX_RT and N_FC == 2
    wrt = router_w.T
    brt = jnp.broadcast_to(router_b.reshape(N_EXP, 1), (N_EXP, V7X_LANES))
    rinfo, counts = _rank(x, mod_l, wrt, brt, n_ctx_tok)
    off, fill, tables = _routing_tables(counts)
    xg, info = _dispatch(x, mod_l, rinfo, off, fill, n_ctx_tok)
    y_lo = _expert_pass(False, tables, xg, w1, w3, w2)
    y = _expert_pass(True, tables, xg, w1, w3, w2, y_lo)
    return _combine(x, mod_l, info, y, ln_g, ln_b, n_ctx_tok)


def _pack_w13(w1, w3):
    lead = w1.shape[:-2]
    nl = len(lead)
    perm = tuple(range(nl)) + (nl + 1, nl, nl + 2)
    a = w1.reshape(lead + (D, N_FC, FC)).transpose(perm)
    b = w3.reshape(lead + (D, N_FC, FC)).transpose(perm)
    return jnp.concatenate([a, b], axis=-1).astype(BF16)


def _block_diag_chunks(w):
    per = CW // HEAD_D
    w4 = w.reshape(N_CH, per, HEAD_D, HEAD_D)
    eye = jnp.eye(per, dtype=w.dtype)
    return jnp.einsum("cgij,gk->cgikj", w4, eye).reshape(N_CH, CW, CW)


def kernel(x_prompt, x_sample, state_rglru, c, c_ctx, w_mod, b_mod, w_in, conv_a, w_a_out, conv_b, conv_b_bias, w_gate_a, b_gate_a, w_gate_x, b_gate_x, lru_lambda, w_b_out, w_o, ln1_g, ln1_b, ln2_g, ln2_b, ffn_w1, ffn_w3, ffn_w2, router_w, router_b, moe_w1, moe_w3, moe_w2):
    batch, seq, d = x_prompt.shape
    dec_batch, dec_seq, _ = x_sample.shape
    depth = w_mod.shape[0]
    assert (d, seq, dec_seq, depth) == (D, SEQ, DEC_SEQ, DEPTH)
    n_ctx_tok = batch * seq
    n_dec_tok = dec_batch * dec_seq
    assert n_ctx_tok % TM1 == 0 and TM1 == dec_seq and 1 + dec_batch <= MOD_ROWS
    n_ctx_tiles = n_ctx_tok // TM1
    n_tiles = n_ctx_tiles + dec_batch
    seq_per_tile = TM1 // seq
    assert N_SEG == 2 * seq_per_tile

    assert depth % 2 == 0
    n_tok = n_ctx_tok + n_dec_tok
    x = (x_prompt.reshape(n_ctx_tok, D), x_sample.reshape(n_dec_tok, D))

    cond = jnp.zeros((MOD_ROWS, D), F32).at[0].set(c_ctx).at[1:1 + dec_batch].set(c)
    mod = _modulation(cond, w_mod, b_mod)

    states = []
    for l in range(depth):
        w_in_l = w_in[l].astype(BF16)
        w5 = jnp.stack([
            jnp.concatenate([w_in_l[:, g * D + j * CW:g * D + (j + 1) * CW] for g in range(5)], axis=1)
            for j in range(N_CH)])
        wg = w_in_l[:, 5 * D:7 * D]
        wbd = jnp.concatenate(
            [_block_diag_chunks(w_gate_a[l, 0]), _block_diag_chunks(w_gate_x[l, 0]),
             _block_diag_chunks(w_gate_a[l, 1]), _block_diag_chunks(w_gate_x[l, 1])],
            axis=-1).astype(BF16)
        gbias = jnp.stack([b_gate_a[l, 0], b_gate_x[l, 0], b_gate_a[l, 1], b_gate_x[l, 1]], axis=0)
        h0 = jnp.zeros((n_tiles, 2, N_SEG, D), F32)
        h0 = h0.at[n_ctx_tiles:, 0, 0].set(state_rglru[:, l, 0].astype(F32))
        h0 = h0.at[n_ctx_tiles:, 1, N_SEG - 1].set(state_rglru[:, l, 1].astype(F32))

        a_pre, b_pre, st = _mixer_part1(
            x, mod[l], w5, conv_a[l], conv_b[l], conv_b_bias[l].reshape(1, D), wbd, gbias,
            lru_lambda[l], h0, n_tok, n_ctx_tok)
        x = _mixer_part2(
            x, mod[l], a_pre, b_pre, wg, w_a_out[l].astype(BF16), w_b_out[l].astype(BF16),
            w_o[l].astype(BF16), ln1_g[l].reshape(1, D), ln1_b[l].reshape(1, D), n_ctx_tok)

        if l % 2 == 0:
            k = l // 2
            x = _dense_ffn(x, mod[l], _pack_w13(ffn_w1[k], ffn_w3[k]), ffn_w2[k].astype(BF16),
                           ln2_g[l].reshape(1, D), ln2_b[l].reshape(1, D), n_ctx_tok)
        else:
            k = l // 2
            x = _moe_ffn(x, mod[l], router_w[k], router_b[k], moe_w1[k], moe_w3[k], moe_w2[k],
                         ln2_g[l].reshape(1, D), ln2_b[l].reshape(1, D), n_ctx_tok)

        st_ctx = st[:n_ctx_tiles]
        fwd = st_ctx[:, 0, 1::2].reshape(batch, D)
        bwd = st_ctx[:, 1, 0::2].reshape(batch, D)
        states.append(jnp.stack([fwd, bwd], axis=1))

    y_prompt = x[0].reshape(batch, seq, D)
    y_sample = x[1].reshape(dec_batch, dec_seq, D)
    new_state = jnp.stack(states, axis=1).astype(x_prompt.dtype)
    return (y_prompt, y_sample, new_state)
```

```python
import functools

import jax
import jax.numpy as jnp
from jax import lax
from jax.experimental import pallas as pl
from jax.experimental.pallas import tpu as pltpu

F32 = jnp.float32
BF16 = jnp.bfloat16

D = 1024
SEQ = 256
DEC_SEQ = 1024
GRID_W = 64
N_HEAD = 16
HEAD_D = D // N_HEAD
RGLRU_C = 8.0
D_FF = 2816
N_EXP = 8
N_MOD = 6
DEPTH = 2
DN_ALPHA = (2.0 * DEPTH) ** 0.25

V7X_SUBLANES = 8
V7X_LANES = 128
V7X_VMEM_BYTES = 64 * 1024 * 1024
VMEM_LIMIT = V7X_VMEM_BYTES - 12 * 1024 * 1024

TM1 = 1024
CW = 256
N_CH = D // CW
N_LC = CW // V7X_LANES
N_SEG = V7X_SUBLANES
SEG = TM1 // N_SEG
SEG_STRIDE = SEG + 8
TM2 = 512
TM3 = 512
FC = 1408
N_FC = D_FF // FC
TS = 512
TR = 256
MAX_RT = 2 * 10240 // TR + N_EXP
EXPERT_VMEM_LIMIT = V7X_VMEM_BYTES - 6 * 1024 * 1024
MIX1_VMEM_LIMIT = V7X_VMEM_BYTES - 6 * 1024 * 1024
MOD_ROWS = 8
MOD_BLK = 1536


def _ln_plain(x, eps):
    mu = jnp.mean(x, axis=-1, keepdims=True)
    xc = x - mu
    var = jnp.mean(xc * xc, axis=-1, keepdims=True)
    return xc * lax.rsqrt(var + eps)


def _token_source(xs, tile, n_ctx_tok):
    ncb = n_ctx_tok // tile
    if isinstance(xs, tuple):
        x_ctx, x_dec = xs
        dec_off = 0
    else:
        x_ctx = x_dec = xs
        dec_off = ncb
    ctx_map = lambda i: (jnp.minimum(i, ncb - 1), 0)
    dec_map = lambda i: (jnp.maximum(i - ncb, 0) + dec_off, 0)
    return x_ctx, x_dec, ctx_map, dec_map


def _mod_row(tok0, n_ctx_tok):
    dec = jnp.maximum(tok0 - n_ctx_tok, 0) // DEC_SEQ
    return jnp.where(tok0 >= n_ctx_tok, 1 + dec, 0)


def _mod_kernel(cond_ref, w_ref, b_ref, o_ref):
    cnd = cond_ref[...]
    s = cnd * jax.nn.sigmoid(cnd)
    o_ref[0] = jnp.dot(s, w_ref[0], preferred_element_type=F32,
                       precision=lax.Precision.HIGHEST) + b_ref[0]


def _modulation(cond, w_mod, b_mod):
    depth = w_mod.shape[0]
    n_out = w_mod.shape[2]
    return pl.pallas_call(
        _mod_kernel,
        grid=(depth, n_out // MOD_BLK),
        in_specs=[
            pl.BlockSpec((MOD_ROWS, D), lambda l, j: (0, 0)),
            pl.BlockSpec((1, D, MOD_BLK), lambda l, j: (l, 0, j)),
            pl.BlockSpec((1, 1, MOD_BLK), lambda l, j: (l, 0, j)),
        ],
        out_specs=pl.BlockSpec((1, MOD_ROWS, MOD_BLK), lambda l, j: (l, 0, j)),
        out_shape=jax.ShapeDtypeStruct((depth, MOD_ROWS, n_out), F32),
        compiler_params=pltpu.CompilerParams(
            dimension_semantics=("arbitrary", "arbitrary"), vmem_limit_bytes=VMEM_LIMIT),
        name="modulation",
    )(cond, w_mod, b_mod.reshape(depth, 1, n_out))


def _scan_both(a_scr, b_scr, ac_scr, hl_scr, h0, keep):
    def body(k, carry):
        new = []
        for d in range(2):
            kk = k if d == 0 else SEG - 1 - k
            idx = pl.ds(kk, N_SEG, stride=SEG_STRIDE)
            for c in range(N_LC):
                h, acc = carry[d * N_LC + c]
                a_k = a_scr[d, c, idx, :]
                h = a_k * h + b_scr[d, c, idx, :]
                acc = a_k * acc
                hl_scr[d, c, idx, :] = h
                ac_scr[d, c, idx, :] = acc
                new.append((h, acc))
        return tuple(new)

    init = tuple((jnp.zeros((N_SEG, V7X_LANES), F32), jnp.ones((N_SEG, V7X_LANES), F32))
                 for _ in range(2 * N_LC))
    fin = lax.fori_loop(0, SEG, body, init, unroll=8)

    out = []
    for d in range(2):
        h_loc = jnp.concatenate([fin[d * N_LC + c][0] for c in range(N_LC)], axis=1)
        a_tot = jnp.concatenate([fin[d * N_LC + c][1] for c in range(N_LC)], axis=1)
        order = range(N_SEG) if d == 0 else range(N_SEG - 1, -1, -1)
        h_in = [None] * N_SEG
        prev = None
        for s in order:
            cur = h0[d][s:s + 1]
            if prev is not None:
                left = a_tot[prev:prev + 1] * h_in[prev] + h_loc[prev:prev + 1]
                cur = keep[d][s] * left + cur
            h_in[s] = cur
            prev = s
        h_in = jnp.concatenate(h_in, axis=0)
        out.append((h_in, a_tot * h_in + h_loc))
    return out


def _proj_pieces(u_scr, w_ref, p_scr):
    def piece(lo, hi):
        def run():
            p_scr[:, lo:hi] = jnp.dot(u_scr[...], w_ref[0, :, lo:hi], preferred_element_type=F32)
        return run
    return [piece(0, 2 * CW), piece(2 * CW, 4 * CW), piece(4 * CW, 5 * CW)]


def _mix1_chunk(is_dec, half, p_scr, side, ca_ref, cb_ref, cbias_ref, wbd_ref, gb_ref, lam_ref, h0_ref,
                apre_ref, bpre_ref, st_ref, a_scr, b_scr, ac_scr, hl_scr, hsum_scr):
    cols = slice(half * CW, (half + 1) * CW)
    side[0]()
    pm = jnp.where(is_dec, GRID_W - 1, SEQ - 1)
    pos = lax.broadcasted_iota(jnp.int32, (TM1, CW), 0) & pm
    has_p1 = pos >= 1
    has_p2 = pos >= 2
    has_n1 = pos < pm

    def shifted(v, back, mask):
        return jnp.where(mask, pltpu.roll(v, back % TM1, axis=0), 0.0)

    z = p_scr[:, 2 * CW:3 * CW] * p_scr[:, 0:CW]
    ca = ca_ref[:, cols]
    conv_a = (ca[0:1] * shifted(z, 1, has_p1) + ca[1:2] * z + ca[2:3] * shifted(z, -1, has_n1))
    apre_ref[:, cols] = (p_scr[:, CW:2 * CW] * conv_a).astype(BF16)

    xr_in = p_scr[:, 4 * CW:5 * CW]
    cb = cb_ref[:, cols]
    xr = (cb[0:1] * shifted(xr_in, 2, has_p2) + cb[1:2] * shifted(xr_in, 1, has_p1)
          + cb[2:3] * xr_in + cb[3:4] * shifted(xr_in, -1, has_n1) + cbias_ref[:, cols])
    gates = jnp.dot(xr.astype(BF16), wbd_ref[half], preferred_element_type=F32)
    gb = gb_ref[:, cols]
    lam = lam_ref[:, cols]
    sp = jnp.maximum(-lam, 0.0) + jnp.log1p(jnp.exp(-jnp.abs(lam)))

    side[1]()
    for d in range(2):
        ga = gates[:, (2 * d) * CW:(2 * d + 1) * CW] + gb[2 * d:2 * d + 1]
        gx = gates[:, (2 * d + 1) * CW:(2 * d + 2) * CW] + gb[2 * d + 1:2 * d + 2]
        r = jax.nn.sigmoid(ga)
        ig = jax.nn.sigmoid(gx)
        a = jnp.exp((-RGLRU_C * r) * sp[d:d + 1])
        bt = jnp.sqrt(1.0 - a * a) * (ig * xr)
        for s in range(N_SEG):
            lo = s * SEG_STRIDE
            for c in range(N_LC):
                lanes = slice(c * V7X_LANES, (c + 1) * V7X_LANES)
                a_scr[d, c, lo:lo + SEG, :] = a[s * SEG:(s + 1) * SEG, lanes]
                b_scr[d, c, lo:lo + SEG, :] = bt[s * SEG:(s + 1) * SEG, lanes]

    one = jnp.float32(1.0)
    keep = [[jnp.where(is_dec, one, jnp.float32(s % 2 == 1)) for s in range(N_SEG)],
            [jnp.where(is_dec, one, jnp.float32(s % 2 == 0)) for s in range(N_SEG)]]
    h0 = [h0_ref[0, d, :, cols] for d in range(2)]
    scans = _scan_both(a_scr, b_scr, ac_scr, hl_scr, h0, keep)
    for d in range(2):
        st_ref[0, d, :, cols] = scans[d][1]
    side[2]()
    for s in range(N_SEG):
        lo = s * SEG_STRIDE
        for c in range(N_LC):
            lanes = slice(c * V7X_LANES, (c + 1) * V7X_LANES)
            h_seg = (hl_scr[0, c, lo:lo + SEG, :] + ac_scr[0, c, lo:lo + SEG, :] * scans[0][0][s:s + 1, lanes]
                     + hl_scr[1, c, lo:lo + SEG, :] + ac_scr[1, c, lo:lo + SEG, :] * scans[1][0][s:s + 1, lanes])
            hsum_scr[s * SEG:(s + 1) * SEG, lanes] = h_seg

    bpre_ref[:, cols] = (hsum_scr[...] * jax.nn.gelu(p_scr[:, 3 * CW:4 * CW])).astype(BF16)


def _mix1_kernel(n_ctx_tiles, n_tiles, xc_ref, xd_ref, mod_ref, wa_ref, wb_ref, ca_ref, cb_ref, cbias_ref,
                 wbd_ref, gb_ref, lam_ref, h0_ref, apre_ref, bpre_ref, st_ref,
                 u_scr, p0_scr, p1_scr, a_scr, b_scr, ac_scr, hl_scr, hsum_scr):
    g = pl.program_id(0)
    q = jnp.maximum(g - 1, 0)
    i = q // 2
    jp = q % 2
    is_dec = i >= n_ctx_tiles
    chunk_refs = (ca_ref, cb_ref, cbias_ref, wbd_ref, gb_ref, lam_ref, h0_ref, apre_ref, bpre_ref, st_ref,
                  a_scr, b_scr, ac_scr, hl_scr, hsum_scr)

    def modulated_norm():
        t = jnp.minimum(g // 2, n_tiles - 1)
        dec = t >= n_ctx_tiles
        m = mod_ref[pl.ds(jnp.where(dec, t - (n_ctx_tiles - 1), 0), 1), :]
        sh1 = m[:, 0:D]
        sc1 = m[:, D:2 * D]
        x = jnp.where(dec, xd_ref[...], xc_ref[...])
        u_scr[...] = (_ln_plain(x, 1e-6) * (1.0 + sc1) + sh1).astype(BF16)

    @pl.when(g == 0)
    def _():
        modulated_norm()
        for piece in _proj_pieces(u_scr, wb_ref, p0_scr):
            piece()

    @pl.when(g > 0)
    def _():
        _mix1_chunk(is_dec, 0, p0_scr, _proj_pieces(u_scr, wa_ref, p1_scr), *chunk_refs)

        @pl.when(jp == 1)
        def _():
            modulated_norm()

        _mix1_chunk(is_dec, 1, p1_scr, _proj_pieces(u_scr, wb_ref, p0_scr), *chunk_refs)


def _mixer_part1(xs, mod_l, w5, conv_a, conv_b, conv_b_bias, wbd, gbias, lam, h0, n_tok, n_ctx_tok):
    n_tiles = n_tok // TM1
    x_ctx, x_dec, ctx_map, dec_map = _token_source(xs, TM1, n_ctx_tok)
    assert N_CH == 4
    kern = functools.partial(_mix1_kernel, n_ctx_tok // TM1, n_tiles)
    tile = lambda g: jnp.maximum(g - 1, 0) // 2
    pair = lambda g: jnp.maximum(g - 1, 0) % 2
    ahead = lambda g: jnp.minimum(g // 2, n_tiles - 1)
    once = pl.Buffered(1)
    return pl.pallas_call(
        kern,
        grid=(2 * n_tiles + 1,),
        in_specs=[
            pl.BlockSpec((TM1, D), lambda g: ctx_map(ahead(g)), pipeline_mode=once),
            pl.BlockSpec((TM1, D), lambda g: dec_map(ahead(g)), pipeline_mode=once),
            pl.BlockSpec((MOD_ROWS, N_MOD * D), lambda g: (0, 0)),
            pl.BlockSpec((1, D, 5 * CW), lambda g: (2 * pair(g) + 1, 0, 0)),
            pl.BlockSpec((1, D, 5 * CW), lambda g: (2 * (g % 2), 0, 0)),
            pl.BlockSpec((3, 2 * CW), lambda g: (0, pair(g))),
            pl.BlockSpec((4, 2 * CW), lambda g: (0, pair(g))),
            pl.BlockSpec((1, 2 * CW), lambda g: (0, pair(g))),
            pl.BlockSpec((2, CW, 4 * CW), lambda g: (pair(g), 0, 0)),
            pl.BlockSpec((4, 2 * CW), lambda g: (0, pair(g))),
            pl.BlockSpec((2, 2 * CW), lambda g: (0, pair(g))),
            pl.BlockSpec((1, 2, N_SEG, 2 * CW), lambda g: (tile(g), 0, 0, pair(g))),
        ],
        out_specs=[
            pl.BlockSpec((TM1, 2 * CW), lambda g: (tile(g), pair(g))),
            pl.BlockSpec((TM1, 2 * CW), lambda g: (tile(g), pair(g))),
            pl.BlockSpec((1, 2, N_SEG, 2 * CW), lambda g: (tile(g), 0, 0, pair(g))),
        ],
        out_shape=[
            jax.ShapeDtypeStruct((n_tok, D), BF16),
            jax.ShapeDtypeStruct((n_tok, D), BF16),
            jax.ShapeDtypeStruct((n_tiles, 2, N_SEG, D), F32),
        ],
        scratch_shapes=[
            pltpu.VMEM((TM1, D), BF16),
            pltpu.VMEM((TM1, 5 * CW), F32),
            pltpu.VMEM((TM1, 5 * CW), F32),
            pltpu.VMEM((2, N_LC, N_SEG * SEG_STRIDE, V7X_LANES), F32),
            pltpu.VMEM((2, N_LC, N_SEG * SEG_STRIDE, V7X_LANES), F32),
            pltpu.VMEM((2, N_LC, N_SEG * SEG_STRIDE, V7X_LANES), F32),
            pltpu.VMEM((2, N_LC, N_SEG * SEG_STRIDE, V7X_LANES), F32),
            pltpu.VMEM((TM1, CW), F32),
        ],
        compiler_params=pltpu.CompilerParams(
            dimension_semantics=("arbitrary",), vmem_limit_bytes=MIX1_VMEM_LIMIT),
        name="mixer_scan",
    )(x_ctx, x_dec, mod_l, w5, w5, conv_a, conv_b, conv_b_bias, wbd, gbias, lam, h0)


def _mix2_kernel(n_ctx_tok, xc_ref, xd_ref, mod_ref, ap_ref, bp_ref, wg_ref, wa_ref, wb_ref, wo_ref,
                 g_ref, b_ref, o_ref):
    i = pl.program_id(0)
    m = mod_ref[pl.ds(_mod_row(i * TM2, n_ctx_tok), 1), :]
    sh1 = m[:, 0:D]
    sc1 = m[:, D:2 * D]
    g1 = m[:, 2 * D:3 * D]
    x = jnp.where(i * TM2 >= n_ctx_tok, xd_ref[...], xc_ref[...])
    u = (_ln_plain(x, 1e-6) * (1.0 + sc1) + sh1).astype(BF16)
    gates = jnp.dot(u, wg_ref[...], preferred_element_type=F32)
    br_a = jnp.dot(ap_ref[...], wa_ref[...], preferred_element_type=F32)
    br_b = jnp.dot(bp_ref[...], wb_ref[...], preferred_element_type=F32)
    merged = jax.nn.sigmoid(gates[:, 0:D]) * br_a + jax.nn.sigmoid(gates[:, D:2 * D]) * br_b
    mix = jnp.dot(merged.astype(BF16), wo_ref[...], preferred_element_type=F32)
    y = DN_ALPHA * x + g1 * mix
    o_ref[...] = _ln_plain(y, 1e-5) * g_ref[...] + b_ref[...]


def _mixer_part2(xs, mod_l, a_pre, b_pre, wg, wa, wb, wo, ln_g, ln_b, n_ctx_tok):
    n_tok = a_pre.shape[0]
    x_ctx, x_dec, ctx_map, dec_map = _token_source(xs, TM2, n_ctx_tok)
    const = lambda i: (0, 0)
    tile = lambda i: (i, 0)
    return pl.pallas_call(
        functools.partial(_mix2_kernel, n_ctx_tok),
        grid=(n_tok // TM2,),
        in_specs=[
            pl.BlockSpec((TM2, D), ctx_map),
            pl.BlockSpec((TM2, D), dec_map),
            pl.BlockSpec((MOD_ROWS, N_MOD * D), const),
            pl.BlockSpec((TM2, D), tile),
            pl.BlockSpec((TM2, D), tile),
            pl.BlockSpec((D, 2 * D), const),
            pl.BlockSpec((D, D), const),
            pl.BlockSpec((D, D), const),
            pl.BlockSpec((D, D), const),
            pl.BlockSpec((1, D), const),
            pl.BlockSpec((1, D), const),
        ],
        out_specs=pl.BlockSpec((TM2, D), tile),
        out_shape=jax.ShapeDtypeStruct((n_tok, D), F32),
        compiler_params=pltpu.CompilerParams(
            dimension_semantics=("arbitrary",), vmem_limit_bytes=VMEM_LIMIT),
        name="mixer_out",
    )(x_ctx, x_dec, mod_l, a_pre, b_pre, wg, wa, wb, wo, ln_g, ln_b)


def _ffn_prologue(n_ctx_tok, x_ref, mod_ref):
    m = mod_ref[pl.ds(_mod_row(pl.program_id(0) * TM3, n_ctx_tok), 1), :]
    sh2 = m[:, 3 * D:4 * D]
    sc2 = m[:, 4 * D:5 * D]
    return _ln_plain(x_ref[...], 1e-6) * (1.0 + sc2) + sh2


def _ffn_epilogue(n_ctx_tok, x_ref, mod_ref, acc, g_ref, b_ref):
    m = mod_ref[pl.ds(_mod_row(pl.program_id(0) * TM3, n_ctx_tok), 1), :]
    g2 = m[:, 5 * D:6 * D]
    y = DN_ALPHA * x_ref[...] + g2 * acc
    return _ln_plain(y, 1e-5) * g_ref[...] + b_ref[...]


def _swiglu_chunk(u, w13, w2):
    h = jnp.dot(u, w13, preferred_element_type=F32)
    h1 = h[:, 0:FC]
    hid = (h1 * jax.nn.sigmoid(h1) * h[:, FC:2 * FC]).astype(BF16)
    return jnp.dot(hid, w2, preferred_element_type=F32)


def _dense_ffn_kernel(n_ctx_tok, x_ref, mod_ref, w13_ref, w2_ref, g_ref, b_ref, o_ref, u_scr, acc_scr):
    f = pl.program_id(1)

    @pl.when(f == 0)
    def _():
        u_scr[...] = _ffn_prologue(n_ctx_tok, x_ref, mod_ref).astype(BF16)

    p = _swiglu_chunk(u_scr[...], w13_ref[0], w2_ref[...])

    @pl.when(f == 0)
    def _():
        acc_scr[...] = p

    @pl.when(f > 0)
    def _():
        acc_scr[...] += p

    @pl.when(f == N_FC - 1)
    def _():
        o_ref[...] = _ffn_epilogue(n_ctx_tok, x_ref, mod_ref, acc_scr[...], g_ref, b_ref)


def _dense_ffn(x, mod_l, w13, w2, ln_g, ln_b, n_ctx_tok):
    n_tok = x.shape[0]
    return pl.pallas_call(
        functools.partial(_dense_ffn_kernel, n_ctx_tok),
        grid=(n_tok // TM3, N_FC),
        in_specs=[
            pl.BlockSpec((TM3, D), lambda i, f: (i, 0)),
            pl.BlockSpec((MOD_ROWS, N_MOD * D), lambda i, f: (0, 0)),
            pl.BlockSpec((1, D, 2 * FC), lambda i, f: (f, 0, 0)),
            pl.BlockSpec((FC, D), lambda i, f: (f, 0)),
            pl.BlockSpec((1, D), lambda i, f: (0, 0)),
            pl.BlockSpec((1, D), lambda i, f: (0, 0)),
        ],
        out_specs=pl.BlockSpec((TM3, D), lambda i, f: (i, 0)),
        out_shape=jax.ShapeDtypeStruct((n_tok, D), F32),
        scratch_shapes=[pltpu.VMEM((TM3, D), BF16), pltpu.VMEM((TM3, D), F32)],
        compiler_params=pltpu.CompilerParams(
            dimension_semantics=("arbitrary", "arbitrary"), vmem_limit_bytes=VMEM_LIMIT),
        name="dense_ffn",
    )(x, mod_l, w13, w2, ln_g, ln_b)


def _rank_kernel(n_ctx_tok, x_ref, mod_ref, wrt_ref, brt_ref, info_ref, cnt_ref):
    i = pl.program_id(0)
    u2 = _ffn_prologue(n_ctx_tok, x_ref, mod_ref)
    lg = lax.dot_general(wrt_ref[...], u2, (((1,), (1,)), ((), ())), preferred_element_type=F32,
                         precision=lax.Precision.HIGHEST) + brt_ref[:, 0:1]
    eidx = lax.broadcasted_iota(jnp.int32, lg.shape, 0).astype(F32)
    neg = jnp.float32(-jnp.inf)
    v1 = jnp.max(lg, axis=0, keepdims=True)
    i1 = jnp.min(jnp.where(lg == v1, eidx, float(N_EXP)), axis=0, keepdims=True)
    lg2 = jnp.where(eidx == i1, neg, lg)
    v2 = jnp.max(lg2, axis=0, keepdims=True)
    i2 = jnp.min(jnp.where(lg2 == v2, eidx, float(N_EXP)), axis=0, keepdims=True)
    t = jnp.exp(v2 - v1)
    w_top = 1.0 / (1.0 + t)
    m1 = eidx == i1
    m2 = eidx == i2
    member = jnp.where(jnp.logical_or(m1, m2), 1.0, 0.0)
    before = (lax.broadcasted_iota(jnp.int32, (TS, TS), 0)
              < lax.broadcasted_iota(jnp.int32, (TS, TS), 1))
    rank = jnp.dot(member.astype(BF16), jnp.where(before, 1.0, 0.0).astype(BF16),
                   preferred_element_type=F32)
    for e in range(N_EXP):
        cnt_ref[i, e] = jnp.sum(member[e:e + 1, :]).astype(jnp.int32)
    rank1 = jnp.sum(jnp.where(m1, rank, 0.0), axis=0, keepdims=True)
    rank2 = jnp.sum(jnp.where(m2, rank, 0.0), axis=0, keepdims=True)
    info_ref[0] = jnp.concatenate(
        [i1, i2, rank1, rank2, w_top, t * w_top, jnp.zeros((V7X_SUBLANES - 6, TS), F32)], axis=0)


def _rank(x, mod_l, wrt, brt, n_ctx_tok):
    n_tiles = x.shape[0] // TS
    return pl.pallas_call(
        functools.partial(_rank_kernel, n_ctx_tok),
        grid=(n_tiles,),
        in_specs=[
            pl.BlockSpec((TS, D), lambda i: (i, 0)),
            pl.BlockSpec((MOD_ROWS, N_MOD * D), lambda i: (0, 0)),
            pl.BlockSpec((N_EXP, D), lambda i: (0, 0)),
            pl.BlockSpec((N_EXP, V7X_LANES), lambda i: (0, 0)),
        ],
        out_specs=[
            pl.BlockSpec((1, V7X_SUBLANES, TS), lambda i: (i, 0, 0)),
            pl.BlockSpec(memory_space=pltpu.SMEM),
        ],
        out_shape=[
            jax.ShapeDtypeStruct((n_tiles, V7X_SUBLANES, TS), F32),
            jax.ShapeDtypeStruct((n_tiles, N_EXP), jnp.int32),
        ],
        compiler_params=pltpu.CompilerParams(
            dimension_semantics=("arbitrary",), vmem_limit_bytes=VMEM_LIMIT),
        name="moe_rank",
    )(x, mod_l, wrt, brt)


def _dispatch_kernel(n_ctx_tok, off_ref, fill_ref, x_ref, mod_ref, rinfo_ref, xg_ref, info_ref,
                     u_scr, pos_v, pos_s, zero_scr, sem, psem, zsem):
    i = pl.program_id(0)
    n_steps = pl.num_programs(0)
    slot = i % 2
    u_scr[slot] = _ffn_prologue(n_ctx_tok, x_ref, mod_ref)
    rinfo = rinfo_ref[0]
    i1 = rinfo[0:1]
    i2 = rinfo[1:2]
    off1 = jnp.zeros_like(i1)
    off2 = jnp.zeros_like(i2)
    for e in range(N_EXP):
        start = off_ref[i * N_EXP + e].astype(F32)
        off1 = jnp.where(i1 == float(e), start, off1)
        off2 = jnp.where(i2 == float(e), start, off2)
    info = jnp.concatenate(
        [off1 + rinfo[2:3], off2 + rinfo[3:4], rinfo[4:6], jnp.zeros((V7X_SUBLANES - 4, TS), F32)],
        axis=0)
    info_ref[0] = info
    pos_v[...] = info.astype(jnp.int32)
    cp = pltpu.make_async_copy(pos_v, pos_s, psem)
    cp.start()
    cp.wait()

    def row_copy(tok, dst_row, s):
        return pltpu.make_async_copy(u_scr.at[s, pl.ds(tok, 1)], xg_ref.at[pl.ds(dst_row, 1)], sem.at[s])

    def wait_tile(s):
        for _ in range(2):
            pltpu.make_async_copy(u_scr.at[s], xg_ref.at[pl.ds(0, TS)], sem.at[s]).wait()

    @pl.when(i > 0)
    def _():
        wait_tile(1 - slot)

    def issue(tok, carry):
        row_copy(tok, pos_s[0, tok], slot).start()
        row_copy(tok, pos_s[1, tok], slot).start()
        return carry

    lax.fori_loop(0, TS, issue, 0, unroll=8)

    @pl.when(i == n_steps - 1)
    def _():
        wait_tile(slot)
        zero_scr[...] = jnp.zeros_like(zero_scr)
        for e in range(N_EXP):
            end = fill_ref[e]
            n_tail = fill_ref[N_EXP + e]

            def tail_copy(k):
                return pltpu.make_async_copy(zero_scr.at[pl.ds(0, 1)], xg_ref.at[pl.ds(end + k, 1)], zsem)

            def tail_start(k, carry):
                tail_copy(k).start()
                return carry

            def tail_wait(k, carry):
                tail_copy(k).wait()
                return carry

            lax.fori_loop(0, n_tail, tail_start, 0)
            lax.fori_loop(0, n_tail, tail_wait, 0)

        def tile_copy(k):
            return pltpu.make_async_copy(zero_scr, xg_ref.at[pl.ds(pl.multiple_of(k * TR, TR), TR)], zsem)

        def tile_start(k, carry):
            tile_copy(k).start()
            return carry

        def tile_wait(k, carry):
            tile_copy(k).wait()
            return carry

        lax.fori_loop(fill_ref[2 * N_EXP], MAX_RT, tile_start, 0)
        lax.fori_loop(fill_ref[2 * N_EXP], MAX_RT, tile_wait, 0)


def _dispatch(x, mod_l, rinfo, off, fill, n_ctx_tok):
    n_tok = x.shape[0]
    n_tiles = n_tok // TS
    return pl.pallas_call(
        functools.partial(_dispatch_kernel, n_ctx_tok),
        grid_spec=pltpu.PrefetchScalarGridSpec(
            num_scalar_prefetch=2,
            grid=(n_tiles,),
            in_specs=[
                pl.BlockSpec((TS, D), lambda i, off, fill: (i, 0)),
                pl.BlockSpec((MOD_ROWS, N_MOD * D), lambda i, off, fill: (0, 0)),
                pl.BlockSpec((1, V7X_SUBLANES, TS), lambda i, off, fill: (i, 0, 0)),
            ],
            out_specs=[
                pl.BlockSpec(memory_space=pl.ANY),
                pl.BlockSpec((1, V7X_SUBLANES, TS), lambda i, off, fill: (i, 0, 0)),
            ],
            scratch_shapes=[
                pltpu.VMEM((2, TS, D), F32),
                pltpu.VMEM((V7X_SUBLANES, TS), jnp.int32),
                pltpu.SMEM((V7X_SUBLANES, TS), jnp.int32),
                pltpu.VMEM((TR, D), F32),
                pltpu.SemaphoreType.DMA((2,)),
                pltpu.SemaphoreType.DMA,
                pltpu.SemaphoreType.DMA,
            ],
        ),
        out_shape=[
            jax.ShapeDtypeStruct((MAX_RT * TR, D), F32),
            jax.ShapeDtypeStruct((n_tiles, V7X_SUBLANES, TS), F32),
        ],
        compiler_params=pltpu.CompilerParams(
            dimension_semantics=("arbitrary",), vmem_limit_bytes=VMEM_LIMIT),
        name="moe_dispatch",
    )(off, fill, x, mod_l, rinfo)


def _expert_kernel(second, exp_ref, new_ref, nact_ref, *refs):
    if second:
        xg_ref, w1_ref, w3_ref, w2_ref, yp_ref, o_ref, w1b, w3b, w2b = refs
    else:
        xg_ref, w1_ref, w3_ref, w2_ref, o_ref, w1b, w3b, w2b = refs
    r = pl.program_id(0)
    active = r < nact_ref[0]

    @pl.when(jnp.logical_and(active, new_ref[r] == 1))
    def _():
        w1b[...] = w1_ref[0].astype(BF16)
        w3b[...] = w3_ref[0].astype(BF16)
        w2b[...] = w2_ref[0].astype(BF16)

    @pl.when(active)
    def _():
        xb = xg_ref[...].astype(BF16)
        h1 = jnp.dot(xb, w1b[...], preferred_element_type=F32)
        h3 = jnp.dot(xb, w3b[...], preferred_element_type=F32)
        hid = (h1 * jax.nn.sigmoid(h1) * h3).astype(BF16)
        p = jnp.dot(hid, w2b[...], preferred_element_type=F32)
        if second:
            p = yp_ref[...] + p
        o_ref[...] = p

    @pl.when(jnp.logical_not(active))
    def _():
        o_ref[...] = jnp.zeros_like(o_ref)


def _expert_pass(second, tables, xg, w1, w3, w2, yp=None):
    f = 1 if second else 0
    row = lambda r, ex, nw, na: (r, 0)
    in_specs = [
        pl.BlockSpec((TR, D), row),
        pl.BlockSpec((1, D, FC), lambda r, ex, nw, na: (ex[r], 0, f)),
        pl.BlockSpec((1, D, FC), lambda r, ex, nw, na: (ex[r], 0, f)),
        pl.BlockSpec((1, FC, D), lambda r, ex, nw, na: (ex[r], f, 0)),
    ]
    args = [xg, w1, w3, w2]
    if second:
        in_specs.append(pl.BlockSpec((TR, D), row))
        args.append(yp)
    return pl.pallas_call(
        functools.partial(_expert_kernel, second),
        grid_spec=pltpu.PrefetchScalarGridSpec(
            num_scalar_prefetch=3,
            grid=(MAX_RT,),
            in_specs=in_specs,
            out_specs=pl.BlockSpec((TR, D), row),
            scratch_shapes=[pltpu.VMEM((D, FC), BF16), pltpu.VMEM((D, FC), BF16),
                            pltpu.VMEM((FC, D), BF16)],
        ),
        out_shape=jax.ShapeDtypeStruct(xg.shape, F32),
        compiler_params=pltpu.CompilerParams(
            dimension_semantics=("arbitrary",), vmem_limit_bytes=EXPERT_VMEM_LIMIT),
        name="moe_expert_hi" if second else "moe_expert_lo",
    )(*tables, *args)


def _combine_kernel(n_ctx_tok, x_ref, mod_ref, info_ref, yg_ref, g_ref, b_ref, oc_ref, od_ref,
                    ya_scr, yb_scr, pos_v, pos_s, sem, psem):
    info = info_ref[0]
    pos_v[...] = info.astype(jnp.int32)
    cp = pltpu.make_async_copy(pos_v, pos_s, psem)
    cp.start()
    cp.wait()

    def issue(tok, carry):
        pltpu.make_async_copy(yg_ref.at[pl.ds(pos_s[0, tok], 1)], ya_scr.at[pl.ds(tok, 1)], sem).start()
        pltpu.make_async_copy(yg_ref.at[pl.ds(pos_s[1, tok], 1)], yb_scr.at[pl.ds(tok, 1)], sem).start()
        return carry

    lax.fori_loop(0, TS, issue, 0, unroll=8)
    padded = jnp.concatenate([info, jnp.zeros((V7X_LANES - V7X_SUBLANES, TS), F32)], axis=0)
    cols = jnp.transpose(padded, (1, 0))
    w1c = cols[:, 2:3]
    w2c = cols[:, 3:4]
    pltpu.make_async_copy(yg_ref.at[pl.ds(0, TS)], ya_scr, sem).wait()
    pltpu.make_async_copy(yg_ref.at[pl.ds(0, TS)], yb_scr, sem).wait()
    acc = w1c * ya_scr[...] + w2c * yb_scr[...]
    res = _ffn_epilogue(n_ctx_tok, x_ref, mod_ref, acc, g_ref, b_ref)
    is_dec = pl.program_id(0) * TS >= n_ctx_tok

    @pl.when(jnp.logical_not(is_dec))
    def _():
        oc_ref[...] = res

    @pl.when(is_dec)
    def _():
        od_ref[...] = res


def _combine(x, mod_l, info, yg, ln_g, ln_b, n_ctx_tok):
    n_tok = x.shape[0]
    ncb = n_ctx_tok // TS
    return pl.pallas_call(
        functools.partial(_combine_kernel, n_ctx_tok),
        grid=(n_tok // TS,),
        in_specs=[
            pl.BlockSpec((TS, D), lambda i: (i, 0)),
            pl.BlockSpec((MOD_ROWS, N_MOD * D), lambda i: (0, 0)),
            pl.BlockSpec((1, V7X_SUBLANES, TS), lambda i: (i, 0, 0)),
            pl.BlockSpec(memory_space=pl.ANY),
            pl.BlockSpec((1, D), lambda i: (0, 0)),
            pl.BlockSpec((1, D), lambda i: (0, 0)),
        ],
        out_specs=[
            pl.BlockSpec((TS, D), lambda i: (jnp.minimum(i, ncb - 1), 0)),
            pl.BlockSpec((TS, D), lambda i: (jnp.maximum(i - ncb, 0), 0)),
        ],
        out_shape=[
            jax.ShapeDtypeStruct((n_ctx_tok, D), F32),
            jax.ShapeDtypeStruct((n_tok - n_ctx_tok, D), F32),
        ],
        scratch_shapes=[
            pltpu.VMEM((TS, D), F32),
            pltpu.VMEM((TS, D), F32),
            pltpu.VMEM((V7X_SUBLANES, TS), jnp.int32),
            pltpu.SMEM((V7X_SUBLANES, TS), jnp.int32),
            pltpu.SemaphoreType.DMA,
            pltpu.SemaphoreType.DMA,
        ],
        compiler_params=pltpu.CompilerParams(
            dimension_semantics=("arbitrary",), vmem_limit_bytes=VMEM_LIMIT),
        name="moe_combine",
    )(x, mod_l, info, yg, ln_g, ln_b)


def _routing_tables(counts):
    totals = jnp.sum(counts, axis=0)
    n_rt = (totals + TR - 1) // TR
    cum = jnp.cumsum(n_rt)
    first = cum - n_rt
    n_act = cum[-1]
    off = first[None, :] * TR + (jnp.cumsum(counts, axis=0) - counts)
    fill = jnp.concatenate([first * TR + totals, n_rt * TR - totals, n_act[None]])
    r = jnp.arange(MAX_RT, dtype=jnp.int32)
    rc = jnp.minimum(r, n_act - 1)
    exp = jnp.sum((rc[:, None] >= cum[None, :]).astype(jnp.int32), axis=1)
    new = jnp.logical_and(r == first[exp], r < n_act)
    i32 = lambda a: a.astype(jnp.int32)
    return i32(off.reshape(-1)), i32(fill), (i32(exp), i32(new), i32(n_act.reshape(1)))


def _moe_ffn(x, mod_l, router_w, router_b, w1, w3, w2, ln_g, ln_b, n_ctx_tok):
    n_tok = x.shape[0]
    assert 2 * n_tok // TR + N_EXP == MAX_RT and N_FC == 2
    wrt = router_w.T
    brt = jnp.broadcast_to(router_b.reshape(N_EXP, 1), (N_EXP, V7X_LANES))
    rinfo, counts = _rank(x, mod_l, wrt, brt, n_ctx_tok)
    off, fill, tables = _routing_tables(counts)
    xg, info = _dispatch(x, mod_l, rinfo, off, fill, n_ctx_tok)
    y_lo = _expert_pass(False, tables, xg, w1, w3, w2)
    y = _expert_pass(True, tables, xg, w1, w3, w2, y_lo)
    return _combine(x, mod_l, info, y, ln_g, ln_b, n_ctx_tok)


def _pack_w13(w1, w3):
    lead = w1.shape[:-2]
    nl = len(lead)
    perm = tuple(range(nl)) + (nl + 1, nl, nl + 2)
    a = w1.reshape(lead + (D, N_FC, FC)).transpose(perm)
    b = w3.reshape(lead + (D, N_FC, FC)).transpose(perm)
    return jnp.concatenate([a, b], axis=-1).astype(BF16)


def _block_diag_chunks(w):
    per = CW // HEAD_D
    w4 = w.reshape(N_CH, per, HEAD_D, HEAD_D)
    eye = jnp.eye(per, dtype=w.dtype)
    return jnp.einsum("cgij,gk->cgikj", w4, eye).reshape(N_CH, CW, CW)


def kernel(x_prompt, x_sample, state_rglru, c, c_ctx, w_mod, b_mod, w_in, conv_a, w_a_out, conv_b, conv_b_bias, w_gate_a, b_gate_a, w_gate_x, b_gate_x, lru_lambda, w_b_out, w_o, ln1_g, ln1_b, ln2_g, ln2_b, ffn_w1, ffn_w3, ffn_w2, router_w, router_b, moe_w1, moe_w3, moe_w2):
    batch, seq, d = x_prompt.shape
    dec_batch, dec_seq, _ = x_sample.shape
    depth = w_mod.shape[0]
    assert (d, seq, dec_seq, depth) == (D, SEQ, DEC_SEQ, DEPTH)
    n_ctx_tok = batch * seq
    n_dec_tok = dec_batch * dec_seq
    assert n_ctx_tok % TM1 == 0 and TM1 == dec_seq and 1 + dec_batch <= MOD_ROWS
    n_ctx_tiles = n_ctx_tok // TM1
    n_tiles = n_ctx_tiles + dec_batch
    seq_per_tile = TM1 // seq
    assert N_SEG == 2 * seq_per_tile

    assert depth % 2 == 0
    n_tok = n_ctx_tok + n_dec_tok
    x = (x_prompt.reshape(n_ctx_tok, D), x_sample.reshape(n_dec_tok, D))

    cond = jnp.zeros((MOD_ROWS, D), F32).at[0].set(c_ctx).at[1:1 + dec_batch].set(c)
    mod = _modulation(cond, w_mod, b_mod)

    states = []
    for l in range(depth):
        w_in_l = w_in[l].astype(BF16)
        w5 = jnp.stack([
            jnp.concatenate([w_in_l[:, g * D + j * CW:g * D + (j + 1) * CW] for g in range(5)], axis=1)
            for j in range(N_CH)])
        wg = w_in_l[:, 5 * D:7 * D]
        wbd = jnp.concatenate(
            [_block_diag_chunks(w_gate_a[l, 0]), _block_diag_chunks(w_gate_x[l, 0]),
             _block_diag_chunks(w_gate_a[l, 1]), _block_diag_chunks(w_gate_x[l, 1])],
            axis=-1).astype(BF16)
        gbias = jnp.stack([b_gate_a[l, 0], b_gate_x[l, 0], b_gate_a[l, 1], b_gate_x[l, 1]], axis=0)
        h0 = jnp.zeros((n_tiles, 2, N_SEG, D), F32)
        h0 = h0.at[n_ctx_tiles:, 0, 0].set(state_rglru[:, l, 0].astype(F32))
        h0 = h0.at[n_ctx_tiles:, 1, N_SEG - 1].set(state_rglru[:, l, 1].astype(F32))

        a_pre, b_pre, st = _mixer_part1(
            x, mod[l], w5, conv_a[l], conv_b[l], conv_b_bias[l].reshape(1, D), wbd, gbias,
            lru_lambda[l], h0, n_tok, n_ctx_tok)
        x = _mixer_part2(
            x, mod[l], a_pre, b_pre, wg, w_a_out[l].astype(BF16), w_b_out[l].astype(BF16),
            w_o[l].astype(BF16), ln1_g[l].reshape(1, D), ln1_b[l].reshape(1, D), n_ctx_tok)

        if l % 2 == 0:
            k = l // 2
            x = _dense_ffn(x, mod[l], _pack_w13(ffn_w1[k], ffn_w3[k]), ffn_w2[k].astype(BF16),
                           ln2_g[l].reshape(1, D), ln2_b[l].reshape(1, D), n_ctx_tok)
        else:
            k = l // 2
            x = _moe_ffn(x, mod[l], router_w[k], router_b[k], moe_w1[k], moe_w3[k], moe_w2[k],
                         ln2_g[l].reshape(1, D), ln2_b[l].reshape(1, D), n_ctx_tok)

        st_ctx = st[:n_ctx_tiles]
        fwd = st_ctx[:, 0, 1::2].reshape(batch, D)
        bwd = st_ctx[:, 1, 0::2].reshape(batch, D)
        states.append(jnp.stack([fwd, bwd], axis=1))

    y_prompt = x[0].reshape(batch, seq, D)
    y_sample = x[1].reshape(dec_batch, dec_seq, D)
    new_state = jnp.stack(states, axis=1).astype(x_prompt.dtype)
    return (y_prompt, y_sample, new_state)
```

```python
import functools

import jax
import jax.numpy as jnp
from jax import lax
from jax.experimental import pallas as pl
from jax.experimental.pallas import tpu as pltpu

F32 = jnp.float32
BF16 = jnp.bfloat16

D = 1024
SEQ = 256
DEC_SEQ = 1024
GRID_W = 64
N_HEAD = 16
HEAD_D = D // N_HEAD
RGLRU_C = 8.0
LOG2_E = 1.4426950408889634
D_FF = 2816
N_EXP = 8
N_MOD = 6
DEPTH = 2
DN_ALPHA = (2.0 * DEPTH) ** 0.25

V7X_SUBLANES = 8
V7X_LANES = 128
V7X_VMEM_BYTES = 64 * 1024 * 1024
VMEM_LIMIT = V7X_VMEM_BYTES - 12 * 1024 * 1024

TM1 = 1024
CW = 256
N_CH = D // CW
N_LC = CW // V7X_LANES
N_SEG = V7X_SUBLANES
SEG = TM1 // N_SEG
SEG_STRIDE = SEG + 8
TM2 = 512
TM3 = 512
FC = 1408
N_FC = D_FF // FC
TS = 512
TR = 256
MAX_RT = 2 * 10240 // TR + N_EXP
EXPERT_VMEM_LIMIT = V7X_VMEM_BYTES - 6 * 1024 * 1024
MOD_ROWS = 8
MOD_BLK = 1536


def _ln_plain(x, eps):
    mu = jnp.mean(x, axis=-1, keepdims=True)
    xc = x - mu
    var = jnp.mean(xc * xc, axis=-1, keepdims=True)
    return xc * lax.rsqrt(var + eps)


def _token_source(xs, tile, n_ctx_tok):
    ncb = n_ctx_tok // tile
    if isinstance(xs, tuple):
        x_ctx, x_dec = xs
        dec_off = 0
    else:
        x_ctx = x_dec = xs
        dec_off = ncb
    ctx_map = lambda i: (jnp.minimum(i, ncb - 1), 0)
    dec_map = lambda i: (jnp.maximum(i - ncb, 0) + dec_off, 0)
    return x_ctx, x_dec, ctx_map, dec_map


def _mod_row(tok0, n_ctx_tok):
    dec = jnp.maximum(tok0 - n_ctx_tok, 0) // DEC_SEQ
    return jnp.where(tok0 >= n_ctx_tok, 1 + dec, 0)


def _mod_kernel(cond_ref, w_ref, b_ref, o_ref):
    cnd = cond_ref[...]
    s = cnd * jax.nn.sigmoid(cnd)
    o_ref[0] = jnp.dot(s, w_ref[0], preferred_element_type=F32,
                       precision=lax.Precision.HIGHEST) + b_ref[0]


def _modulation(cond, w_mod, b_mod):
    depth = w_mod.shape[0]
    n_out = w_mod.shape[2]
    return pl.pallas_call(
        _mod_kernel,
        grid=(depth, n_out // MOD_BLK),
        in_specs=[
            pl.BlockSpec((MOD_ROWS, D), lambda l, j: (0, 0)),
            pl.BlockSpec((1, D, MOD_BLK), lambda l, j: (l, 0, j)),
            pl.BlockSpec((1, 1, MOD_BLK), lambda l, j: (l, 0, j)),
        ],
        out_specs=pl.BlockSpec((1, MOD_ROWS, MOD_BLK), lambda l, j: (l, 0, j)),
        out_shape=jax.ShapeDtypeStruct((depth, MOD_ROWS, n_out), F32),
        compiler_params=pltpu.CompilerParams(
            dimension_semantics=("arbitrary", "arbitrary"), vmem_limit_bytes=VMEM_LIMIT),
        name="modulation",
    )(cond, w_mod, b_mod.reshape(depth, 1, n_out))


def _scan_dir(a_scr, b_scr, hl_scr, ac_scr, h0, keep, reverse):
    n_lc = a_scr.shape[0]

    def body(k, carry):
        kk = SEG - 1 - k if reverse else k
        idx = pl.ds(kk, N_SEG, stride=SEG_STRIDE)
        new = []
        for c in range(n_lc):
            h, acc = carry[c]
            a_k = a_scr[c, idx, :]
            h = a_k * h + b_scr[c, idx, :]
            acc = a_k * acc
            hl_scr[c, idx, :] = h
            ac_scr[c, idx, :] = acc
            new.append((h, acc))
        return tuple(new)

    init = tuple((jnp.zeros((N_SEG, V7X_LANES), F32), jnp.ones((N_SEG, V7X_LANES), F32))
                 for _ in range(n_lc))
    fin = lax.fori_loop(0, SEG, body, init, unroll=8)
    h_loc = jnp.concatenate([fin[c][0] for c in range(n_lc)], axis=1)
    a_tot = jnp.concatenate([fin[c][1] for c in range(n_lc)], axis=1)

    order = range(N_SEG - 1, -1, -1) if reverse else range(N_SEG)
    h_in = [None] * N_SEG
    prev = None
    for s in order:
        cur = h0[s:s + 1]
        if prev is not None:
            left = a_tot[prev:prev + 1] * h_in[prev] + h_loc[prev:prev + 1]
            cur = keep[s] * left + cur
        h_in[s] = cur
        prev = s
    h_in = jnp.concatenate(h_in, axis=0)
    return h_in, a_tot * h_in + h_loc


def _mix1_kernel(n_ctx_tiles, xc_ref, xd_ref, mod_ref, w5_ref, ca_ref, cb_ref, cbias_ref, wbd_ref, gb_ref,
                 lam_ref, h0_ref, apre_ref, bpre_ref, st_ref,
                 u_scr, proj_scr, a_scr, b_scr, hl_scr, ac_scr, hsum_scr):
    i = pl.program_id(0)
    j = pl.program_id(1)
    is_dec = i >= n_ctx_tiles

    @pl.when(j == 0)
    def _():
        row = jnp.where(is_dec, i - (n_ctx_tiles - 1), 0)
        m = mod_ref[pl.ds(row, 1), :]
        sh1 = m[:, 0:D]
        sc1 = m[:, D:2 * D]
        x = jnp.where(is_dec, xd_ref[...], xc_ref[...])
        u_scr[...] = (_ln_plain(x, 1e-6) * (1.0 + sc1) + sh1).astype(BF16)

    proj_scr[...] = jnp.dot(u_scr[...], w5_ref[0], preferred_element_type=F32)

    pm = jnp.where(is_dec, GRID_W - 1, SEQ - 1)
    pos = lax.broadcasted_iota(jnp.int32, (TM1, CW), 0) & pm
    has_p1 = pos >= 1
    has_p2 = pos >= 2
    has_n1 = pos < pm

    def shifted(v, back, mask):
        return jnp.where(mask, pltpu.roll(v, back % TM1, axis=0), 0.0)

    z = proj_scr[:, 2 * CW:3 * CW] * proj_scr[:, 0:CW]
    ca = ca_ref[...]
    conv_a = (ca[0:1] * shifted(z, 1, has_p1) + ca[1:2] * z + ca[2:3] * shifted(z, -1, has_n1))
    apre_ref[...] = (proj_scr[:, CW:2 * CW] * conv_a).astype(BF16)

    xr_in = proj_scr[:, 4 * CW:5 * CW]
    cb = cb_ref[...]
    xr = (cb[0:1] * shifted(xr_in, 2, has_p2) + cb[1:2] * shifted(xr_in, 1, has_p1)
          + cb[2:3] * xr_in + cb[3:4] * shifted(xr_in, -1, has_n1) + cbias_ref[...])
    gates = jnp.dot(xr.astype(BF16), wbd_ref[0], preferred_element_type=F32)
    gb = gb_ref[...]
    lam = lam_ref[...]
    sp = jnp.maximum(-lam, 0.0) + jnp.log1p(jnp.exp(-jnp.abs(lam)))
    rate = (-RGLRU_C * LOG2_E) * sp

    one = jnp.float32(1.0)
    for d in range(2):
        ga = gates[:, (2 * d) * CW:(2 * d + 1) * CW] + gb[2 * d:2 * d + 1]
        gx = gates[:, (2 * d + 1) * CW:(2 * d + 2) * CW] + gb[2 * d + 1:2 * d + 2]
        r = jax.nn.sigmoid(ga)
        ig = jax.nn.sigmoid(gx)
        a = jnp.exp2(r * rate[d:d + 1])
        y = 1.0 - a * a
        bt = jnp.where(y > 0.0, y * lax.rsqrt(y), 0.0) * (ig * xr)
        for s in range(N_SEG):
            lo = s * SEG_STRIDE
            for c in range(N_LC):
                lanes = slice(c * V7X_LANES, (c + 1) * V7X_LANES)
                a_scr[c, lo:lo + SEG, :] = a[s * SEG:(s + 1) * SEG, lanes]
                b_scr[c, lo:lo + SEG, :] = bt[s * SEG:(s + 1) * SEG, lanes]
        if d == 0:
            keep = [jnp.where(is_dec, one, jnp.float32(s % 2 == 1)) for s in range(N_SEG)]
        else:
            keep = [jnp.where(is_dec, one, jnp.float32(s % 2 == 0)) for s in range(N_SEG)]
        h_in, h_out = _scan_dir(a_scr, b_scr, hl_scr, ac_scr, h0_ref[0, d], keep, reverse=(d == 1))
        st_ref[0, d] = h_out
        for s in range(N_SEG):
            lo = s * SEG_STRIDE
            for c in range(N_LC):
                lanes = slice(c * V7X_LANES, (c + 1) * V7X_LANES)
                h_seg = hl_scr[c, lo:lo + SEG, :] + ac_scr[c, lo:lo + SEG, :] * h_in[s:s + 1, lanes]
                if d == 0:
                    hsum_scr[s * SEG:(s + 1) * SEG, lanes] = h_seg
                else:
                    hsum_scr[s * SEG:(s + 1) * SEG, lanes] += h_seg

    bpre_ref[...] = (hsum_scr[...] * jax.nn.gelu(proj_scr[:, 3 * CW:4 * CW])).astype(BF16)


def _mixer_part1(xs, mod_l, w5, conv_a, conv_b, conv_b_bias, wbd, gbias, lam, h0, n_tok, n_ctx_tok):
    n_tiles = n_tok // TM1
    x_ctx, x_dec, ctx_map, dec_map = _token_source(xs, TM1, n_ctx_tok)
    kern = functools.partial(_mix1_kernel, n_ctx_tok // TM1)
    return pl.pallas_call(
        kern,
        grid=(n_tiles, N_CH),
        in_specs=[
            pl.BlockSpec((TM1, D), lambda i, j: ctx_map(i)),
            pl.BlockSpec((TM1, D), lambda i, j: dec_map(i)),
            pl.BlockSpec((MOD_ROWS, N_MOD * D), lambda i, j: (0, 0)),
            pl.BlockSpec((1, D, 5 * CW), lambda i, j: (j, 0, 0)),
            pl.BlockSpec((3, CW), lambda i, j: (0, j)),
            pl.BlockSpec((4, CW), lambda i, j: (0, j)),
            pl.BlockSpec((1, CW), lambda i, j: (0, j)),
            pl.BlockSpec((1, CW, 4 * CW), lambda i, j: (j, 0, 0)),
            pl.BlockSpec((4, CW), lambda i, j: (0, j)),
            pl.BlockSpec((2, CW), lambda i, j: (0, j)),
            pl.BlockSpec((1, 2, N_SEG, CW), lambda i, j: (i, 0, 0, j)),
        ],
        out_specs=[
            pl.BlockSpec((TM1, CW), lambda i, j: (i, j)),
            pl.BlockSpec((TM1, CW), lambda i, j: (i, j)),
            pl.BlockSpec((1, 2, N_SEG, CW), lambda i, j: (i, 0, 0, j)),
        ],
        out_shape=[
            jax.ShapeDtypeStruct((n_tok, D), BF16),
            jax.ShapeDtypeStruct((n_tok, D), BF16),
            jax.ShapeDtypeStruct((n_tiles, 2, N_SEG, D), F32),
        ],
        scratch_shapes=[
            pltpu.VMEM((TM1, D), BF16),
            pltpu.VMEM((TM1, 5 * CW), F32),
            pltpu.VMEM((N_LC, N_SEG * SEG_STRIDE, V7X_LANES), F32),
            pltpu.VMEM((N_LC, N_SEG * SEG_STRIDE, V7X_LANES), F32),
            pltpu.VMEM((N_LC, N_SEG * SEG_STRIDE, V7X_LANES), F32),
            pltpu.VMEM((N_LC, N_SEG * SEG_STRIDE, V7X_LANES), F32),
            pltpu.VMEM((TM1, CW), F32),
        ],
        compiler_params=pltpu.CompilerParams(
            dimension_semantics=("arbitrary", "arbitrary"), vmem_limit_bytes=VMEM_LIMIT),
        name="mixer_scan",
    )(x_ctx, x_dec, mod_l, w5, conv_a, conv_b, conv_b_bias, wbd, gbias, lam, h0)


def _mix2_kernel(n_ctx_tok, xc_ref, xd_ref, mod_ref, ap_ref, bp_ref, wg_ref, wa_ref, wb_ref, wo_ref,
                 g_ref, b_ref, o_ref):
    i = pl.program_id(0)
    m = mod_ref[pl.ds(_mod_row(i * TM2, n_ctx_tok), 1), :]
    sh1 = m[:, 0:D]
    sc1 = m[:, D:2 * D]
    g1 = m[:, 2 * D:3 * D]
    x = jnp.where(i * TM2 >= n_ctx_tok, xd_ref[...], xc_ref[...])
    u = (_ln_plain(x, 1e-6) * (1.0 + sc1) + sh1).astype(BF16)
    gates = jnp.dot(u, wg_ref[...], preferred_element_type=F32)
    br_a = jnp.dot(ap_ref[...], wa_ref[...], preferred_element_type=F32)
    br_b = jnp.dot(bp_ref[...], wb_ref[...], preferred_element_type=F32)
    merged = jax.nn.sigmoid(gates[:, 0:D]) * br_a + jax.nn.sigmoid(gates[:, D:2 * D]) * br_b
    mix = jnp.dot(merged.astype(BF16), wo_ref[...], preferred_element_type=F32)
    y = DN_ALPHA * x + g1 * mix
    o_ref[...] = _ln_plain(y, 1e-5) * g_ref[...] + b_ref[...]


def _mixer_part2(xs, mod_l, a_pre, b_pre, wg, wa, wb, wo, ln_g, ln_b, n_ctx_tok):
    n_tok = a_pre.shape[0]
    x_ctx, x_dec, ctx_map, dec_map = _token_source(xs, TM2, n_ctx_tok)
    const = lambda i: (0, 0)
    tile = lambda i: (i, 0)
    return pl.pallas_call(
        functools.partial(_mix2_kernel, n_ctx_tok),
        grid=(n_tok // TM2,),
        in_specs=[
            pl.BlockSpec((TM2, D), ctx_map),
            pl.BlockSpec((TM2, D), dec_map),
            pl.BlockSpec((MOD_ROWS, N_MOD * D), const),
            pl.BlockSpec((TM2, D), tile),
            pl.BlockSpec((TM2, D), tile),
            pl.BlockSpec((D, 2 * D), const),
            pl.BlockSpec((D, D), const),
            pl.BlockSpec((D, D), const),
            pl.BlockSpec((D, D), const),
            pl.BlockSpec((1, D), const),
            pl.BlockSpec((1, D), const),
        ],
        out_specs=pl.BlockSpec((TM2, D), tile),
        out_shape=jax.ShapeDtypeStruct((n_tok, D), F32),
        compiler_params=pltpu.CompilerParams(
            dimension_semantics=("arbitrary",), vmem_limit_bytes=VMEM_LIMIT),
        name="mixer_out",
    )(x_ctx, x_dec, mod_l, a_pre, b_pre, wg, wa, wb, wo, ln_g, ln_b)


def _ffn_prologue(n_ctx_tok, x_ref, mod_ref):
    m = mod_ref[pl.ds(_mod_row(pl.program_id(0) * TM3, n_ctx_tok), 1), :]
    sh2 = m[:, 3 * D:4 * D]
    sc2 = m[:, 4 * D:5 * D]
    return _ln_plain(x_ref[...], 1e-6) * (1.0 + sc2) + sh2


def _ffn_epilogue(n_ctx_tok, x_ref, mod_ref, acc, g_ref, b_ref):
    m = mod_ref[pl.ds(_mod_row(pl.program_id(0) * TM3, n_ctx_tok), 1), :]
    g2 = m[:, 5 * D:6 * D]
    y = DN_ALPHA * x_ref[...] + g2 * acc
    return _ln_plain(y, 1e-5) * g_ref[...] + b_ref[...]


def _swiglu_chunk(u, w13, w2):
    h = jnp.dot(u, w13, preferred_element_type=F32)
    h1 = h[:, 0:FC]
    hid = (h1 * jax.nn.sigmoid(h1) * h[:, FC:2 * FC]).astype(BF16)
    return jnp.dot(hid, w2, preferred_element_type=F32)


def _dense_ffn_kernel(n_ctx_tok, x_ref, mod_ref, w13_ref, w2_ref, g_ref, b_ref, o_ref, u_scr, acc_scr):
    f = pl.program_id(1)

    @pl.when(f == 0)
    def _():
        u_scr[...] = _ffn_prologue(n_ctx_tok, x_ref, mod_ref).astype(BF16)

    p = _swiglu_chunk(u_scr[...], w13_ref[0], w2_ref[...])

    @pl.when(f == 0)
    def _():
        acc_scr[...] = p

    @pl.when(f > 0)
    def _():
        acc_scr[...] += p

    @pl.when(f == N_FC - 1)
    def _():
        o_ref[...] = _ffn_epilogue(n_ctx_tok, x_ref, mod_ref, acc_scr[...], g_ref, b_ref)


def _dense_ffn(x, mod_l, w13, w2, ln_g, ln_b, n_ctx_tok):
    n_tok = x.shape[0]
    return pl.pallas_call(
        functools.partial(_dense_ffn_kernel, n_ctx_tok),
        grid=(n_tok // TM3, N_FC),
        in_specs=[
            pl.BlockSpec((TM3, D), lambda i, f: (i, 0)),
            pl.BlockSpec((MOD_ROWS, N_MOD * D), lambda i, f: (0, 0)),
            pl.BlockSpec((1, D, 2 * FC), lambda i, f: (f, 0, 0)),
            pl.BlockSpec((FC, D), lambda i, f: (f, 0)),
            pl.BlockSpec((1, D), lambda i, f: (0, 0)),
            pl.BlockSpec((1, D), lambda i, f: (0, 0)),
        ],
        out_specs=pl.BlockSpec((TM3, D), lambda i, f: (i, 0)),
        out_shape=jax.ShapeDtypeStruct((n_tok, D), F32),
        scratch_shapes=[pltpu.VMEM((TM3, D), BF16), pltpu.VMEM((TM3, D), F32)],
        compiler_params=pltpu.CompilerParams(
            dimension_semantics=("arbitrary", "arbitrary"), vmem_limit_bytes=VMEM_LIMIT),
        name="dense_ffn",
    )(x, mod_l, w13, w2, ln_g, ln_b)


def _rank_kernel(n_ctx_tok, x_ref, mod_ref, wrt_ref, brt_ref, info_ref, cnt_ref):
    i = pl.program_id(0)
    u2 = _ffn_prologue(n_ctx_tok, x_ref, mod_ref)
    lg = lax.dot_general(wrt_ref[...], u2, (((1,), (1,)), ((), ())), preferred_element_type=F32,
                         precision=lax.Precision.HIGHEST) + brt_ref[:, 0:1]
    eidx = lax.broadcasted_iota(jnp.int32, lg.shape, 0).astype(F32)
    neg = jnp.float32(-jnp.inf)
    v1 = jnp.max(lg, axis=0, keepdims=True)
    i1 = jnp.min(jnp.where(lg == v1, eidx, float(N_EXP)), axis=0, keepdims=True)
    lg2 = jnp.where(eidx == i1, neg, lg)
    v2 = jnp.max(lg2, axis=0, keepdims=True)
    i2 = jnp.min(jnp.where(lg2 == v2, eidx, float(N_EXP)), axis=0, keepdims=True)
    t = jnp.exp(v2 - v1)
    w_top = 1.0 / (1.0 + t)
    m1 = eidx == i1
    m2 = eidx == i2
    member = jnp.where(jnp.logical_or(m1, m2), 1.0, 0.0)
    before = (lax.broadcasted_iota(jnp.int32, (TS, TS), 0)
              < lax.broadcasted_iota(jnp.int32, (TS, TS), 1))
    rank = jnp.dot(member.astype(BF16), jnp.where(before, 1.0, 0.0).astype(BF16),
                   preferred_element_type=F32)
    for e in range(N_EXP):
        cnt_ref[i, e] = jnp.sum(member[e:e + 1, :]).astype(jnp.int32)
    rank1 = jnp.sum(jnp.where(m1, rank, 0.0), axis=0, keepdims=True)
    rank2 = jnp.sum(jnp.where(m2, rank, 0.0), axis=0, keepdims=True)
    info_ref[0] = jnp.concatenate(
        [i1, i2, rank1, rank2, w_top, t * w_top, jnp.zeros((V7X_SUBLANES - 6, TS), F32)], axis=0)


def _rank(x, mod_l, wrt, brt, n_ctx_tok):
    n_tiles = x.shape[0] // TS
    return pl.pallas_call(
        functools.partial(_rank_kernel, n_ctx_tok),
        grid=(n_tiles,),
        in_specs=[
            pl.BlockSpec((TS, D), lambda i: (i, 0)),
            pl.BlockSpec((MOD_ROWS, N_MOD * D), lambda i: (0, 0)),
            pl.BlockSpec((N_EXP, D), lambda i: (0, 0)),
            pl.BlockSpec((N_EXP, V7X_LANES), lambda i: (0, 0)),
        ],
        out_specs=[
            pl.BlockSpec((1, V7X_SUBLANES, TS), lambda i: (i, 0, 0)),
            pl.BlockSpec(memory_space=pltpu.SMEM),
        ],
        out_shape=[
            jax.ShapeDtypeStruct((n_tiles, V7X_SUBLANES, TS), F32),
            jax.ShapeDtypeStruct((n_tiles, N_EXP), jnp.int32),
        ],
        compiler_params=pltpu.CompilerParams(
            dimension_semantics=("arbitrary",), vmem_limit_bytes=VMEM_LIMIT),
        name="moe_rank",
    )(x, mod_l, wrt, brt)


def _dispatch_kernel(n_ctx_tok, off_ref, fill_ref, x_ref, mod_ref, rinfo_ref, xg_ref, info_ref,
                     u_scr, pos_v, pos_s, zero_scr, sem, psem, zsem):
    i = pl.program_id(0)
    n_steps = pl.num_programs(0)
    slot = i % 2
    u_scr[slot] = _ffn_prologue(n_ctx_tok, x_ref, mod_ref)
    rinfo = rinfo_ref[0]
    i1 = rinfo[0:1]
    i2 = rinfo[1:2]
    off1 = jnp.zeros_like(i1)
    off2 = jnp.zeros_like(i2)
    for e in range(N_EXP):
        start = off_ref[i * N_EXP + e].astype(F32)
        off1 = jnp.where(i1 == float(e), start, off1)
        off2 = jnp.where(i2 == float(e), start, off2)
    info = jnp.concatenate(
        [off1 + rinfo[2:3], off2 + rinfo[3:4], rinfo[4:6], jnp.zeros((V7X_SUBLANES - 4, TS), F32)],
        axis=0)
    info_ref[0] = info
    pos_v[...] = info.astype(jnp.int32)
    cp = pltpu.make_async_copy(pos_v, pos_s, psem)
    cp.start()
    cp.wait()

    def row_copy(tok, dst_row, s):
        return pltpu.make_async_copy(u_scr.at[s, pl.ds(tok, 1)], xg_ref.at[pl.ds(dst_row, 1)], sem.at[s])

    def wait_tile(s):
        for _ in range(2):
            pltpu.make_async_copy(u_scr.at[s], xg_ref.at[pl.ds(0, TS)], sem.at[s]).wait()

    @pl.when(i > 0)
    def _():
        wait_tile(1 - slot)

    def issue(tok, carry):
        row_copy(tok, pos_s[0, tok], slot).start(priority=0)
        row_copy(tok, pos_s[1, tok], slot).start(priority=1)
        return carry

    lax.fori_loop(0, TS, issue, 0, unroll=8)

    @pl.when(i == n_steps - 1)
    def _():
        wait_tile(slot)
        zero_scr[...] = jnp.zeros_like(zero_scr)
        for e in range(N_EXP):
            end = fill_ref[e]
            n_tail = fill_ref[N_EXP + e]

            def tail_copy(k):
                return pltpu.make_async_copy(zero_scr.at[pl.ds(0, 1)], xg_ref.at[pl.ds(end + k, 1)], zsem)

            def tail_start(k, carry):
                tail_copy(k).start()
                return carry

            def tail_wait(k, carry):
                tail_copy(k).wait()
                return carry

            lax.fori_loop(0, n_tail, tail_start, 0)
            lax.fori_loop(0, n_tail, tail_wait, 0)

        def tile_copy(k):
            return pltpu.make_async_copy(zero_scr, xg_ref.at[pl.ds(pl.multiple_of(k * TR, TR), TR)], zsem)

        def tile_start(k, carry):
            tile_copy(k).start()
            return carry

        def tile_wait(k, carry):
            tile_copy(k).wait()
            return carry

        lax.fori_loop(fill_ref[2 * N_EXP], MAX_RT, tile_start, 0)
        lax.fori_loop(fill_ref[2 * N_EXP], MAX_RT, tile_wait, 0)


def _dispatch(x, mod_l, rinfo, off, fill, n_ctx_tok):
    n_tok = x.shape[0]
    n_tiles = n_tok // TS
    return pl.pallas_call(
        functools.partial(_dispatch_kernel, n_ctx_tok),
        grid_spec=pltpu.PrefetchScalarGridSpec(
            num_scalar_prefetch=2,
            grid=(n_tiles,),
            in_specs=[
                pl.BlockSpec((TS, D), lambda i, off, fill: (i, 0)),
                pl.BlockSpec((MOD_ROWS, N_MOD * D), lambda i, off, fill: (0, 0)),
                pl.BlockSpec((1, V7X_SUBLANES, TS), lambda i, off, fill: (i, 0, 0)),
            ],
            out_specs=[
                pl.BlockSpec(memory_space=pl.ANY),
                pl.BlockSpec((1, V7X_SUBLANES, TS), lambda i, off, fill: (i, 0, 0)),
            ],
            scratch_shapes=[
                pltpu.VMEM((2, TS, D), F32),
                pltpu.VMEM((V7X_SUBLANES, TS), jnp.int32),
                pltpu.SMEM((V7X_SUBLANES, TS), jnp.int32),
                pltpu.VMEM((TR, D), F32),
                pltpu.SemaphoreType.DMA((2,)),
                pltpu.SemaphoreType.DMA,
                pltpu.SemaphoreType.DMA,
            ],
        ),
        out_shape=[
            jax.ShapeDtypeStruct((MAX_RT * TR, D), F32),
            jax.ShapeDtypeStruct((n_tiles, V7X_SUBLANES, TS), F32),
        ],
        compiler_params=pltpu.CompilerParams(
            dimension_semantics=("arbitrary",), vmem_limit_bytes=VMEM_LIMIT),
        name="moe_dispatch",
    )(off, fill, x, mod_l, rinfo)


def _expert_kernel(second, exp_ref, new_ref, nact_ref, *refs):
    if second:
        xg_ref, w1_ref, w3_ref, w2_ref, yp_ref, o_ref, w1b, w3b, w2b = refs
    else:
        xg_ref, w1_ref, w3_ref, w2_ref, o_ref, w1b, w3b, w2b = refs
    r = pl.program_id(0)
    active = r < nact_ref[0]

    @pl.when(jnp.logical_and(active, new_ref[r] == 1))
    def _():
        w1b[...] = w1_ref[0].astype(BF16)
        w3b[...] = w3_ref[0].astype(BF16)
        w2b[...] = w2_ref[0].astype(BF16)

    @pl.when(active)
    def _():
        xb = xg_ref[...].astype(BF16)
        h1 = jnp.dot(xb, w1b[...], preferred_element_type=F32)
        h3 = jnp.dot(xb, w3b[...], preferred_element_type=F32)
        hid = (h1 * jax.nn.sigmoid(h1) * h3).astype(BF16)
        p = jnp.dot(hid, w2b[...], preferred_element_type=F32)
        if second:
            p = yp_ref[...] + p
        o_ref[...] = p

    @pl.when(jnp.logical_not(active))
    def _():
        o_ref[...] = jnp.zeros_like(o_ref)


def _expert_pass(second, tables, xg, w1, w3, w2, yp=None):
    f = 1 if second else 0
    row = lambda r, ex, nw, na: (r, 0)
    in_specs = [
        pl.BlockSpec((TR, D), row),
        pl.BlockSpec((1, D, FC), lambda r, ex, nw, na: (ex[r], 0, f)),
        pl.BlockSpec((1, D, FC), lambda r, ex, nw, na: (ex[r], 0, f)),
        pl.BlockSpec((1, FC, D), lambda r, ex, nw, na: (ex[r], f, 0)),
    ]
    args = [xg, w1, w3, w2]
    if second:
        in_specs.append(pl.BlockSpec((TR, D), row))
        args.append(yp)
    return pl.pallas_call(
        functools.partial(_expert_kernel, second),
        grid_spec=pltpu.PrefetchScalarGridSpec(
            num_scalar_prefetch=3,
            grid=(MAX_RT,),
            in_specs=in_specs,
            out_specs=pl.BlockSpec((TR, D), row),
            scratch_shapes=[pltpu.VMEM((D, FC), BF16), pltpu.VMEM((D, FC), BF16),
                            pltpu.VMEM((FC, D), BF16)],
        ),
        out_shape=jax.ShapeDtypeStruct(xg.shape, F32),
        compiler_params=pltpu.CompilerParams(
            dimension_semantics=("arbitrary",), vmem_limit_bytes=EXPERT_VMEM_LIMIT),
        name="moe_expert_hi" if second else "moe_expert_lo",
    )(*tables, *args)


def _combine_kernel(n_ctx_tok, x_ref, mod_ref, info_ref, info_next_ref, yg_ref, g_ref, b_ref, oc_ref, od_ref,
                    ya_scr, yb_scr, pos_v, pos_s, sem, psem):
    i = pl.program_id(0)
    n_steps = pl.num_programs(0)
    slot = i % 2

    def gather_tile(info, s):
        pos_v[...] = info.astype(jnp.int32)
        cp = pltpu.make_async_copy(pos_v, pos_s, psem)
        cp.start()
        cp.wait()

        def issue(tok, carry):
            pltpu.make_async_copy(yg_ref.at[pl.ds(pos_s[0, tok], 1)], ya_scr.at[s, pl.ds(tok, 1)],
                                  sem.at[s]).start(priority=0)
            pltpu.make_async_copy(yg_ref.at[pl.ds(pos_s[1, tok], 1)], yb_scr.at[s, pl.ds(tok, 1)],
                                  sem.at[s]).start(priority=1)
            return carry

        lax.fori_loop(0, TS, issue, 0, unroll=8)

    @pl.when(i == 0)
    def _():
        gather_tile(info_ref[0], 0)

    @pl.when(i + 1 < n_steps)
    def _():
        gather_tile(info_next_ref[0], 1 - slot)

    padded = jnp.concatenate([info_ref[0], jnp.zeros((V7X_LANES - V7X_SUBLANES, TS), F32)], axis=0)
    cols = jnp.transpose(padded, (1, 0))
    w1c = cols[:, 2:3]
    w2c = cols[:, 3:4]
    pltpu.make_async_copy(yg_ref.at[pl.ds(0, TS)], ya_scr.at[slot], sem.at[slot]).wait()
    pltpu.make_async_copy(yg_ref.at[pl.ds(0, TS)], yb_scr.at[slot], sem.at[slot]).wait()
    acc = w1c * ya_scr[slot] + w2c * yb_scr[slot]
    res = _ffn_epilogue(n_ctx_tok, x_ref, mod_ref, acc, g_ref, b_ref)
    is_dec = i * TS >= n_ctx_tok

    @pl.when(jnp.logical_not(is_dec))
    def _():
        oc_ref[...] = res

    @pl.when(is_dec)
    def _():
        od_ref[...] = res


def _combine(x, mod_l, info, yg, ln_g, ln_b, n_ctx_tok):
    n_tok = x.shape[0]
    n_tiles = n_tok // TS
    ncb = n_ctx_tok // TS
    return pl.pallas_call(
        functools.partial(_combine_kernel, n_ctx_tok),
        grid=(n_tiles,),
        in_specs=[
            pl.BlockSpec((TS, D), lambda i: (i, 0)),
            pl.BlockSpec((MOD_ROWS, N_MOD * D), lambda i: (0, 0)),
            pl.BlockSpec((1, V7X_SUBLANES, TS), lambda i: (i, 0, 0)),
            pl.BlockSpec((1, V7X_SUBLANES, TS), lambda i: (jnp.minimum(i + 1, n_tiles - 1), 0, 0)),
            pl.BlockSpec(memory_space=pl.ANY),
            pl.BlockSpec((1, D), lambda i: (0, 0)),
            pl.BlockSpec((1, D), lambda i: (0, 0)),
        ],
        out_specs=[
            pl.BlockSpec((TS, D), lambda i: (jnp.minimum(i, ncb - 1), 0)),
            pl.BlockSpec((TS, D), lambda i: (jnp.maximum(i - ncb, 0), 0)),
        ],
        out_shape=[
            jax.ShapeDtypeStruct((n_ctx_tok, D), F32),
            jax.ShapeDtypeStruct((n_tok - n_ctx_tok, D), F32),
        ],
        scratch_shapes=[
            pltpu.VMEM((2, TS, D), F32),
            pltpu.VMEM((2, TS, D), F32),
            pltpu.VMEM((V7X_SUBLANES, TS), jnp.int32),
            pltpu.SMEM((V7X_SUBLANES, TS), jnp.int32),
            pltpu.SemaphoreType.DMA((2,)),
            pltpu.SemaphoreType.DMA,
        ],
        compiler_params=pltpu.CompilerParams(
            dimension_semantics=("arbitrary",), vmem_limit_bytes=VMEM_LIMIT),
        name="moe_combine",
    )(x, mod_l, info, info, yg, ln_g, ln_b)


def _routing_tables(counts):
    totals = jnp.sum(counts, axis=0)
    n_rt = (totals + TR - 1) // TR
    cum = jnp.cumsum(n_rt)
    first = cum - n_rt
    n_act = cum[-1]
    off = first[None, :] * TR + (jnp.cumsum(counts, axis=0) - counts)
    fill = jnp.concatenate([first * TR + totals, n_rt * TR - totals, n_act[None]])
    r = jnp.arange(MAX_RT, dtype=jnp.int32)
    rc = jnp.minimum(r, n_act - 1)
    exp = jnp.sum((rc[:, None] >= cum[None, :]).astype(jnp.int32), axis=1)
    new = jnp.logical_and(r == first[exp], r < n_act)
    i32 = lambda a: a.astype(jnp.int32)
    return i32(off.reshape(-1)), i32(fill), (i32(exp), i32(new), i32(n_act.reshape(1)))


def _moe_ffn(x, mod_l, router_w, router_b, w1, w3, w2, ln_g, ln_b, n_ctx_tok):
    n_tok = x.shape[0]
    assert 2 * n_tok // TR + N_EXP == MAX_RT and N_FC == 2
    wrt = router_w.T
    brt = jnp.broadcast_to(router_b.reshape(N_EXP, 1), (N_EXP, V7X_LANES))
    rinfo, counts = _rank(x, mod_l, wrt, brt, n_ctx_tok)
    off, fill, tables = _routing_tables(counts)
    xg, info = _dispatch(x, mod_l, rinfo, off, fill, n_ctx_tok)
    y_lo = _expert_pass(False, tables, xg, w1, w3, w2)
    y = _expert_pass(True, tables, xg, w1, w3, w2, y_lo)
    return _combine(x, mod_l, info, y, ln_g, ln_b, n_ctx_tok)


def _pack_w13(w1, w3):
    lead = w1.shape[:-2]
    nl = len(lead)
    perm = tuple(range(nl)) + (nl + 1, nl, nl + 2)
    a = w1.reshape(lead + (D, N_FC, FC)).transpose(perm)
    b = w3.reshape(lead + (D, N_FC, FC)).transpose(perm)
    return jnp.concatenate([a, b], axis=-1).astype(BF16)


def _block_diag_chunks(w):
    per = CW // HEAD_D
    w4 = w.reshape(N_CH, per, HEAD_D, HEAD_D)
    eye = jnp.eye(per, dtype=w.dtype)
    return jnp.einsum("cgij,gk->cgikj", w4, eye).reshape(N_CH, CW, CW)


def kernel(x_prompt, x_sample, state_rglru, c, c_ctx, w_mod, b_mod, w_in, conv_a, w_a_out, conv_b, conv_b_bias, w_gate_a, b_gate_a, w_gate_x, b_gate_x, lru_lambda, w_b_out, w_o, ln1_g, ln1_b, ln2_g, ln2_b, ffn_w1, ffn_w3, ffn_w2, router_w, router_b, moe_w1, moe_w3, moe_w2):
    batch, seq, d = x_prompt.shape
    dec_batch, dec_seq, _ = x_sample.shape
    depth = w_mod.shape[0]
    assert (d, seq, dec_seq, depth) == (D, SEQ, DEC_SEQ, DEPTH)
    n_ctx_tok = batch * seq
    n_dec_tok = dec_batch * dec_seq
    assert n_ctx_tok % TM1 == 0 and TM1 == dec_seq and 1 + dec_batch <= MOD_ROWS
    n_ctx_tiles = n_ctx_tok // TM1
    n_tiles = n_ctx_tiles + dec_batch
    seq_per_tile = TM1 // seq
    assert N_SEG == 2 * seq_per_tile

    assert depth % 2 == 0
    n_tok = n_ctx_tok + n_dec_tok
    x = (x_prompt.reshape(n_ctx_tok, D), x_sample.reshape(n_dec_tok, D))

    cond = jnp.zeros((MOD_ROWS, D), F32).at[0].set(c_ctx).at[1:1 + dec_batch].set(c)
    mod = _modulation(cond, w_mod, b_mod)

    states = []
    for l in range(depth):
        w_in_l = w_in[l].astype(BF16)
        w5 = jnp.stack([
            jnp.concatenate([w_in_l[:, g * D + j * CW:g * D + (j + 1) * CW] for g in range(5)], axis=1)
            for j in range(N_CH)])
        wg = w_in_l[:, 5 * D:7 * D]
        wbd = jnp.concatenate(
            [_block_diag_chunks(w_gate_a[l, 0]), _block_diag_chunks(w_gate_x[l, 0]),
             _block_diag_chunks(w_gate_a[l, 1]), _block_diag_chunks(w_gate_x[l, 1])],
            axis=-1).astype(BF16)
        gbias = jnp.stack([b_gate_a[l, 0], b_gate_x[l, 0], b_gate_a[l, 1], b_gate_x[l, 1]], axis=0)
        h0 = jnp.zeros((n_tiles, 2, N_SEG, D), F32)
        h0 = h0.at[n_ctx_tiles:, 0, 0].set(state_rglru[:, l, 0].astype(F32))
        h0 = h0.at[n_ctx_tiles:, 1, N_SEG - 1].set(state_rglru[:, l, 1].astype(F32))

        a_pre, b_pre, st = _mixer_part1(
            x, mod[l], w5, conv_a[l], conv_b[l], conv_b_bias[l].reshape(1, D), wbd, gbias,
            lru_lambda[l], h0, n_tok, n_ctx_tok)
        x = _mixer_part2(
            x, mod[l], a_pre, b_pre, wg, w_a_out[l].astype(BF16), w_b_out[l].astype(BF16),
            w_o[l].astype(BF16), ln1_g[l].reshape(1, D), ln1_b[l].reshape(1, D), n_ctx_tok)

        if l % 2 == 0:
            k = l // 2
            x = _dense_ffn(x, mod[l], _pack_w13(ffn_w1[k], ffn_w3[k]), ffn_w2[k].astype(BF16),
                           ln2_g[l].reshape(1, D), ln2_b[l].reshape(1, D), n_ctx_tok)
        else:
            k = l // 2
            x = _moe_ffn(x, mod[l], router_w[k], router_b[k], moe_w1[k], moe_w3[k], moe_w2[k],
                         ln2_g[l].reshape(1, D), ln2_b[l].reshape(1, D), n_ctx_tok)

        st_ctx = st[:n_ctx_tiles]
        fwd = st_ctx[:, 0, 1::2].reshape(batch, D)
        bwd = st_ctx[:, 1, 0::2].reshape(batch, D)
        states.append(jnp.stack([fwd, bwd], axis=1))

    y_prompt = x[0].reshape(batch, seq, D)
    y_sample = x[1].reshape(dec_batch, dec_seq, D)
    new_state = jnp.stack(states, axis=1).astype(x_prompt.dtype)
    return (y_prompt, y_sample, new_state)
```

```python
import functools

import jax
import jax.numpy as jnp
from jax import lax
from jax.experimental import pallas as pl
from jax.experimental.pallas import tpu as pltpu

F32 = jnp.float32
BF16 = jnp.bfloat16

D = 1024
SEQ = 256
DEC_SEQ = 1024
GRID_W = 64
N_HEAD = 16
HEAD_D = D // N_HEAD
RGLRU_C = 8.0
LOG2_E = 1.4426950408889634
D_FF = 2816
N_EXP = 8
N_MOD = 6
DEPTH = 2
DN_ALPHA = (2.0 * DEPTH) ** 0.25

V7X_SUBLANES = 8
V7X_LANES = 128
V7X_VMEM_BYTES = 64 * 1024 * 1024
VMEM_LIMIT = V7X_VMEM_BYTES - 12 * 1024 * 1024

TM1 = 1024
CW = 256
N_CH = D // CW
N_LC = CW // V7X_LANES
N_SEG = V7X_SUBLANES
SEG = TM1 // N_SEG
SEG_STRIDE = SEG + 8
CONV_GAP = V7X_SUBLANES
TM2 = 512
TM3 = 512
FC = 1408
N_FC = D_FF // FC
TS = 512
TR = 256
MAX_RT = 2 * 10240 // TR + N_EXP
EXPERT_VMEM_LIMIT = V7X_VMEM_BYTES - 6 * 1024 * 1024
MOD_ROWS = 8
MOD_BLK = 1536


def _ln_plain(x, eps):
    mu = jnp.mean(x, axis=-1, keepdims=True)
    xc = x - mu
    var = jnp.mean(xc * xc, axis=-1, keepdims=True)
    return xc * lax.rsqrt(var + eps)


def _token_source(xs, tile, n_ctx_tok):
    ncb = n_ctx_tok // tile
    if isinstance(xs, tuple):
        x_ctx, x_dec = xs
        dec_off = 0
    else:
        x_ctx = x_dec = xs
        dec_off = ncb
    ctx_map = lambda i: (jnp.minimum(i, ncb - 1), 0)
    dec_map = lambda i: (jnp.maximum(i - ncb, 0) + dec_off, 0)
    return x_ctx, x_dec, ctx_map, dec_map


def _mod_row(tok0, n_ctx_tok):
    dec = jnp.maximum(tok0 - n_ctx_tok, 0) // DEC_SEQ
    return jnp.where(tok0 >= n_ctx_tok, 1 + dec, 0)


def _mod_kernel(cond_ref, w_ref, b_ref, o_ref):
    cnd = cond_ref[...]
    s = cnd * jax.nn.sigmoid(cnd)
    o_ref[0] = jnp.dot(s, w_ref[0], preferred_element_type=F32,
                       precision=lax.Precision.HIGHEST) + b_ref[0]


def _modulation(cond, w_mod, b_mod):
    depth = w_mod.shape[0]
    n_out = w_mod.shape[2]
    return pl.pallas_call(
        _mod_kernel,
        grid=(depth, n_out // MOD_BLK),
        in_specs=[
            pl.BlockSpec((MOD_ROWS, D), lambda l, j: (0, 0)),
            pl.BlockSpec((1, D, MOD_BLK), lambda l, j: (l, 0, j)),
            pl.BlockSpec((1, 1, MOD_BLK), lambda l, j: (l, 0, j)),
        ],
        out_specs=pl.BlockSpec((1, MOD_ROWS, MOD_BLK), lambda l, j: (l, 0, j)),
        out_shape=jax.ShapeDtypeStruct((depth, MOD_ROWS, n_out), F32),
        compiler_params=pltpu.CompilerParams(
            dimension_semantics=("arbitrary", "arbitrary"), vmem_limit_bytes=VMEM_LIMIT),
        name="modulation",
    )(cond, w_mod, b_mod.reshape(depth, 1, n_out))


def _scan_dir(a_scr, b_scr, hl_scr, ac_scr, h0, keep, reverse):
    n_lc = a_scr.shape[0]

    def body(k, carry):
        kk = SEG - 1 - k if reverse else k
        idx = pl.ds(kk, N_SEG, stride=SEG_STRIDE)
        new = []
        for c in range(n_lc):
            h, acc = carry[c]
            a_k = a_scr[c, idx, :]
            h = a_k * h + b_scr[c, idx, :]
            acc = a_k * acc
            hl_scr[c, idx, :] = h
            ac_scr[c, idx, :] = acc
            new.append((h, acc))
        return tuple(new)

    init = tuple((jnp.zeros((N_SEG, V7X_LANES), F32), jnp.ones((N_SEG, V7X_LANES), F32))
                 for _ in range(n_lc))
    fin = lax.fori_loop(0, SEG, body, init, unroll=8)
    h_loc = jnp.concatenate([fin[c][0] for c in range(n_lc)], axis=1)
    a_tot = jnp.concatenate([fin[c][1] for c in range(n_lc)], axis=1)

    order = range(N_SEG - 1, -1, -1) if reverse else range(N_SEG)
    h_in = [None] * N_SEG
    prev = None
    for s in order:
        cur = h0[s:s + 1]
        if prev is not None:
            left = a_tot[prev:prev + 1] * h_in[prev] + h_loc[prev:prev + 1]
            cur = keep[s] * left + cur
        h_in[s] = cur
        prev = s
    h_in = jnp.concatenate(h_in, axis=0)
    return h_in, a_tot * h_in + h_loc


def _conv_stage(is_dec, proj_scr, ca_ref, cb_ref, cbias_ref, apre_ref, xr_scr, gap_scr):
    win = GRID_W
    n_win = TM1 // win
    stride = win + CONV_GAP
    ca = ca_ref[...]
    cb = cb_ref[...]
    bias = cbias_ref[...]
    pieces = [(w, c) for w in range(n_win) for c in range(N_LC)]
    sub = lax.broadcasted_iota(jnp.int32, (CONV_GAP, V7X_LANES), 0)
    joined = jnp.where(is_dec, jnp.float32(0.0), jnp.float32(1.0))
    zero_gap = jnp.zeros((CONV_GAP, V7X_LANES), F32)

    def stage(value_of):
        for c in range(N_LC):
            for w in range(n_win + 1):
                edge = w * win
                if edge % SEQ == 0:
                    gap = zero_gap
                else:
                    before = value_of(slice(edge - CONV_GAP, edge), c)
                    after = value_of(slice(edge, edge + CONV_GAP), c)
                    gap = joined * jnp.where(sub >= CONV_GAP - 2, before, jnp.where(sub == 0, after, 0.0))
                gap_scr[c, w * stride:w * stride + CONV_GAP, :] = gap
        for w, c in pieces:
            lo = CONV_GAP + w * stride
            gap_scr[c, lo:lo + win, :] = value_of(slice(w * win, (w + 1) * win), c)

    def tap(w, c, shift):
        lo = CONV_GAP + w * stride + shift
        return gap_scr[c, lo:lo + win, :]

    def chunk_cols(k, c):
        return slice(k * CW + c * V7X_LANES, k * CW + (c + 1) * V7X_LANES)

    stage(lambda rows, c: proj_scr[rows, chunk_cols(2, c)] * proj_scr[rows, chunk_cols(0, c)])
    for w, c in pieces:
        rows = slice(w * win, (w + 1) * win)
        lanes = slice(c * V7X_LANES, (c + 1) * V7X_LANES)
        conv = ca[0:1, lanes] * tap(w, c, -1) + ca[1:2, lanes] * tap(w, c, 0) + ca[2:3, lanes] * tap(w, c, 1)
        apre_ref[rows, lanes] = (proj_scr[rows, chunk_cols(1, c)] * conv).astype(BF16)
    stage(lambda rows, c: proj_scr[rows, chunk_cols(4, c)])
    for w, c in pieces:
        rows = slice(w * win, (w + 1) * win)
        lanes = slice(c * V7X_LANES, (c + 1) * V7X_LANES)
        xr_scr[rows, lanes] = (cb[0:1, lanes] * tap(w, c, -2) + cb[1:2, lanes] * tap(w, c, -1)
                               + cb[2:3, lanes] * tap(w, c, 0) + cb[3:4, lanes] * tap(w, c, 1)
                               + bias[:, lanes])


def _mix1_kernel(n_ctx_tiles, xc_ref, xd_ref, mod_ref, w5_ref, ca_ref, cb_ref, cbias_ref, wbd_ref, gb_ref,
                 lam_ref, h0_ref, apre_ref, bpre_ref, st_ref,
                 u_scr, proj_scr, a_scr, b_scr, hl_scr, ac_scr, hsum_scr, xr_scr, gap_scr):
    i = pl.program_id(0)
    j = pl.program_id(1)
    is_dec = i >= n_ctx_tiles

    @pl.when(j == 0)
    def _():
        row = jnp.where(is_dec, i - (n_ctx_tiles - 1), 0)
        m = mod_ref[pl.ds(row, 1), :]
        sh1 = m[:, 0:D]
        sc1 = m[:, D:2 * D]
        x = jnp.where(is_dec, xd_ref[...], xc_ref[...])
        u_scr[...] = (_ln_plain(x, 1e-6) * (1.0 + sc1) + sh1).astype(BF16)

    proj_scr[...] = jnp.dot(u_scr[...], w5_ref[0], preferred_element_type=F32)

    _conv_stage(is_dec, proj_scr, ca_ref, cb_ref, cbias_ref, apre_ref, xr_scr, gap_scr)

    xr = xr_scr[...]
    gates = jnp.dot(xr.astype(BF16), wbd_ref[0], preferred_element_type=F32)
    gb = gb_ref[...]
    lam = lam_ref[...]
    sp = jnp.maximum(-lam, 0.0) + jnp.log1p(jnp.exp(-jnp.abs(lam)))
    rate = (-RGLRU_C * LOG2_E) * sp

    one = jnp.float32(1.0)
    for d in range(2):
        ga = gates[:, (2 * d) * CW:(2 * d + 1) * CW] + gb[2 * d:2 * d + 1]
        gx = gates[:, (2 * d + 1) * CW:(2 * d + 2) * CW] + gb[2 * d + 1:2 * d + 2]
        r = jax.nn.sigmoid(ga)
        ig = jax.nn.sigmoid(gx)
        a = jnp.exp2(r * rate[d:d + 1])
        y = 1.0 - a * a
        bt = jnp.where(y > 0.0, y * lax.rsqrt(y), 0.0) * (ig * xr)
        for s in range(N_SEG):
            lo = s * SEG_STRIDE
            for c in range(N_LC):
                lanes = slice(c * V7X_LANES, (c + 1) * V7X_LANES)
                a_scr[c, lo:lo + SEG, :] = a[s * SEG:(s + 1) * SEG, lanes]
                b_scr[c, lo:lo + SEG, :] = bt[s * SEG:(s + 1) * SEG, lanes]
        if d == 0:
            keep = [jnp.where(is_dec, one, jnp.float32(s % 2 == 1)) for s in range(N_SEG)]
        else:
            keep = [jnp.where(is_dec, one, jnp.float32(s % 2 == 0)) for s in range(N_SEG)]
        h_in, h_out = _scan_dir(a_scr, b_scr, hl_scr, ac_scr, h0_ref[0, d], keep, reverse=(d == 1))
        st_ref[0, d] = h_out
        for s in range(N_SEG):
            lo = s * SEG_STRIDE
            for c in range(N_LC):
                lanes = slice(c * V7X_LANES, (c + 1) * V7X_LANES)
                h_seg = hl_scr[c, lo:lo + SEG, :] + ac_scr[c, lo:lo + SEG, :] * h_in[s:s + 1, lanes]
                if d == 0:
                    hsum_scr[s * SEG:(s + 1) * SEG, lanes] = h_seg
                else:
                    hsum_scr[s * SEG:(s + 1) * SEG, lanes] += h_seg

    bpre_ref[...] = (hsum_scr[...] * jax.nn.gelu(proj_scr[:, 3 * CW:4 * CW])).astype(BF16)


def _mixer_part1(xs, mod_l, w5, conv_a, conv_b, conv_b_bias, wbd, gbias, lam, h0, n_tok, n_ctx_tok):
    n_tiles = n_tok // TM1
    x_ctx, x_dec, ctx_map, dec_map = _token_source(xs, TM1, n_ctx_tok)
    kern = functools.partial(_mix1_kernel, n_ctx_tok // TM1)
    return pl.pallas_call(
        kern,
        grid=(n_tiles, N_CH),
        in_specs=[
            pl.BlockSpec((TM1, D), lambda i, j: ctx_map(i)),
            pl.BlockSpec((TM1, D), lambda i, j: dec_map(i)),
            pl.BlockSpec((MOD_ROWS, N_MOD * D), lambda i, j: (0, 0)),
            pl.BlockSpec((1, D, 5 * CW), lambda i, j: (j, 0, 0)),
            pl.BlockSpec((3, CW), lambda i, j: (0, j)),
            pl.BlockSpec((4, CW), lambda i, j: (0, j)),
            pl.BlockSpec((1, CW), lambda i, j: (0, j)),
            pl.BlockSpec((1, CW, 4 * CW), lambda i, j: (j, 0, 0)),
            pl.BlockSpec((4, CW), lambda i, j: (0, j)),
            pl.BlockSpec((2, CW), lambda i, j: (0, j)),
            pl.BlockSpec((1, 2, N_SEG, CW), lambda i, j: (i, 0, 0, j)),
        ],
        out_specs=[
            pl.BlockSpec((TM1, CW), lambda i, j: (i, j)),
            pl.BlockSpec((TM1, CW), lambda i, j: (i, j)),
            pl.BlockSpec((1, 2, N_SEG, CW), lambda i, j: (i, 0, 0, j)),
        ],
        out_shape=[
            jax.ShapeDtypeStruct((n_tok, D), BF16),
            jax.ShapeDtypeStruct((n_tok, D), BF16),
            jax.ShapeDtypeStruct((n_tiles, 2, N_SEG, D), F32),
        ],
        scratch_shapes=[
            pltpu.VMEM((TM1, D), BF16),
            pltpu.VMEM((TM1, 5 * CW), F32),
            pltpu.VMEM((N_LC, N_SEG * SEG_STRIDE, V7X_LANES), F32),
            pltpu.VMEM((N_LC, N_SEG * SEG_STRIDE, V7X_LANES), F32),
            pltpu.VMEM((N_LC, N_SEG * SEG_STRIDE, V7X_LANES), F32),
            pltpu.VMEM((N_LC, N_SEG * SEG_STRIDE, V7X_LANES), F32),
            pltpu.VMEM((TM1, CW), F32),
            pltpu.VMEM((TM1, CW), F32),
            pltpu.VMEM((N_LC, CONV_GAP + (TM1 // GRID_W) * (GRID_W + CONV_GAP), V7X_LANES), F32),
        ],
        compiler_params=pltpu.CompilerParams(
            dimension_semantics=("arbitrary", "arbitrary"), vmem_limit_bytes=VMEM_LIMIT),
        name="mixer_scan",
    )(x_ctx, x_dec, mod_l, w5, conv_a, conv_b, conv_b_bias, wbd, gbias, lam, h0)


def _mix2_kernel(n_ctx_tok, xc_ref, xd_ref, mod_ref, ap_ref, bp_ref, wg_ref, wa_ref, wb_ref, wo_ref,
                 g_ref, b_ref, o_ref):
    i = pl.program_id(0)
    m = mod_ref[pl.ds(_mod_row(i * TM2, n_ctx_tok), 1), :]
    sh1 = m[:, 0:D]
    sc1 = m[:, D:2 * D]
    g1 = m[:, 2 * D:3 * D]
    x = jnp.where(i * TM2 >= n_ctx_tok, xd_ref[...], xc_ref[...])
    u = (_ln_plain(x, 1e-6) * (1.0 + sc1) + sh1).astype(BF16)
    gates = jnp.dot(u, wg_ref[...], preferred_element_type=F32)
    br_a = jnp.dot(ap_ref[...], wa_ref[...], preferred_element_type=F32)
    br_b = jnp.dot(bp_ref[...], wb_ref[...], preferred_element_type=F32)
    merged = jax.nn.sigmoid(gates[:, 0:D]) * br_a + jax.nn.sigmoid(gates[:, D:2 * D]) * br_b
    mix = jnp.dot(merged.astype(BF16), wo_ref[...], preferred_element_type=F32)
    y = DN_ALPHA * x + g1 * mix
    o_ref[...] = _ln_plain(y, 1e-5) * g_ref[...] + b_ref[...]


def _mixer_part2(xs, mod_l, a_pre, b_pre, wg, wa, wb, wo, ln_g, ln_b, n_ctx_tok):
    n_tok = a_pre.shape[0]
    x_ctx, x_dec, ctx_map, dec_map = _token_source(xs, TM2, n_ctx_tok)
    const = lambda i: (0, 0)
    tile = lambda i: (i, 0)
    return pl.pallas_call(
        functools.partial(_mix2_kernel, n_ctx_tok),
        grid=(n_tok // TM2,),
        in_specs=[
            pl.BlockSpec((TM2, D), ctx_map),
            pl.BlockSpec((TM2, D), dec_map),
            pl.BlockSpec((MOD_ROWS, N_MOD * D), const),
            pl.BlockSpec((TM2, D), tile),
            pl.BlockSpec((TM2, D), tile),
            pl.BlockSpec((D, 2 * D), const),
            pl.BlockSpec((D, D), const),
            pl.BlockSpec((D, D), const),
            pl.BlockSpec((D, D), const),
            pl.BlockSpec((1, D), const),
            pl.BlockSpec((1, D), const),
        ],
        out_specs=pl.BlockSpec((TM2, D), tile),
        out_shape=jax.ShapeDtypeStruct((n_tok, D), F32),
        compiler_params=pltpu.CompilerParams(
            dimension_semantics=("arbitrary",), vmem_limit_bytes=VMEM_LIMIT),
        name="mixer_out",
    )(x_ctx, x_dec, mod_l, a_pre, b_pre, wg, wa, wb, wo, ln_g, ln_b)


def _ffn_prologue(n_ctx_tok, x_ref, mod_ref):
    m = mod_ref[pl.ds(_mod_row(pl.program_id(0) * TM3, n_ctx_tok), 1), :]
    sh2 = m[:, 3 * D:4 * D]
    sc2 = m[:, 4 * D:5 * D]
    return _ln_plain(x_ref[...], 1e-6) * (1.0 + sc2) + sh2


def _ffn_epilogue(n_ctx_tok, x_ref, mod_ref, acc, g_ref, b_ref):
    m = mod_ref[pl.ds(_mod_row(pl.program_id(0) * TM3, n_ctx_tok), 1), :]
    g2 = m[:, 5 * D:6 * D]
    y = DN_ALPHA * x_ref[...] + g2 * acc
    return _ln_plain(y, 1e-5) * g_ref[...] + b_ref[...]


def _swiglu_chunk(u, w13, w2):
    h = jnp.dot(u, w13, preferred_element_type=F32)
    h1 = h[:, 0:FC]
    hid = (h1 * jax.nn.sigmoid(h1) * h[:, FC:2 * FC]).astype(BF16)
    return jnp.dot(hid, w2, preferred_element_type=F32)


def _dense_ffn_kernel(n_ctx_tok, x_ref, mod_ref, w13_ref, w2_ref, g_ref, b_ref, o_ref, u_scr, acc_scr):
    f = pl.program_id(1)

    @pl.when(f == 0)
    def _():
        u_scr[...] = _ffn_prologue(n_ctx_tok, x_ref, mod_ref).astype(BF16)

    p = _swiglu_chunk(u_scr[...], w13_ref[0], w2_ref[...])

    @pl.when(f == 0)
    def _():
        acc_scr[...] = p

    @pl.when(f > 0)
    def _():
        acc_scr[...] += p

    @pl.when(f == N_FC - 1)
    def _():
        o_ref[...] = _ffn_epilogue(n_ctx_tok, x_ref, mod_ref, acc_scr[...], g_ref, b_ref)


def _dense_ffn(x, mod_l, w13, w2, ln_g, ln_b, n_ctx_tok):
    n_tok = x.shape[0]
    return pl.pallas_call(
        functools.partial(_dense_ffn_kernel, n_ctx_tok),
        grid=(n_tok // TM3, N_FC),
        in_specs=[
            pl.BlockSpec((TM3, D), lambda i, f: (i, 0)),
            pl.BlockSpec((MOD_ROWS, N_MOD * D), lambda i, f: (0, 0)),
            pl.BlockSpec((1, D, 2 * FC), lambda i, f: (f, 0, 0)),
            pl.BlockSpec((FC, D), lambda i, f: (f, 0)),
            pl.BlockSpec((1, D), lambda i, f: (0, 0)),
            pl.BlockSpec((1, D), lambda i, f: (0, 0)),
        ],
        out_specs=pl.BlockSpec((TM3, D), lambda i, f: (i, 0)),
        out_shape=jax.ShapeDtypeStruct((n_tok, D), F32),
        scratch_shapes=[pltpu.VMEM((TM3, D), BF16), pltpu.VMEM((TM3, D), F32)],
        compiler_params=pltpu.CompilerParams(
            dimension_semantics=("arbitrary", "arbitrary"), vmem_limit_bytes=VMEM_LIMIT),
        name="dense_ffn",
    )(x, mod_l, w13, w2, ln_g, ln_b)


N_LT = D // V7X_LANES


def _to_token_major(ref, rows):
    n = rows.shape[0]
    for c in range(N_LT):
        ref[pl.ds(c, n, stride=N_LT), :] = rows[:, c * V7X_LANES:(c + 1) * V7X_LANES]


def _from_token_major(ref, n):
    return jnp.concatenate([ref[pl.ds(c, n, stride=N_LT), :] for c in range(N_LT)], axis=1)
def _rank_kernel(n_ctx_tok, x_ref, mod_ref, wrt_ref, brt_ref, info_ref, cnt_ref):
    i = pl.program_id(0)
    u2 = _ffn_prologue(n_ctx_tok, x_ref, mod_ref)
    lg = lax.dot_general(wrt_ref[...], u2, (((1,), (1,)), ((), ())), preferred_element_type=F32,
                         precision=lax.Precision.HIGHEST) + brt_ref[:, 0:1]
    eidx = lax.broadcasted_iota(jnp.int32, lg.shape, 0).astype(F32)
    neg = jnp.float32(-jnp.inf)
    v1 = jnp.max(lg, axis=0, keepdims=True)
    i1 = jnp.min(jnp.where(lg == v1, eidx, float(N_EXP)), axis=0, keepdims=True)
    lg2 = jnp.where(eidx == i1, neg, lg)
    v2 = jnp.max(lg2, axis=0, keepdims=True)
    i2 = jnp.min(jnp.where(lg2 == v2, eidx, float(N_EXP)), axis=0, keepdims=True)
    t = jnp.exp(v2 - v1)
    w_top = 1.0 / (1.0 + t)
    m1 = eidx == i1
    m2 = eidx == i2
    member = jnp.where(jnp.logical_or(m1, m2), 1.0, 0.0)
    before = (lax.broadcasted_iota(jnp.int32, (TS, TS), 0)
              < lax.broadcasted_iota(jnp.int32, (TS, TS), 1))
    rank = jnp.dot(member.astype(BF16), jnp.where(before, 1.0, 0.0).astype(BF16),
                   preferred_element_type=F32)
    for e in range(N_EXP):
        cnt_ref[i, e] = jnp.sum(member[e:e + 1, :]).astype(jnp.int32)
    rank1 = jnp.sum(jnp.where(m1, rank, 0.0), axis=0, keepdims=True)
    rank2 = jnp.sum(jnp.where(m2, rank, 0.0), axis=0, keepdims=True)
    info_ref[0] = jnp.concatenate(
        [i1, i2, rank1, rank2, w_top, t * w_top, jnp.zeros((V7X_SUBLANES - 6, TS), F32)], axis=0)


def _rank(x, mod_l, wrt, brt, n_ctx_tok):
    n_tiles = x.shape[0] // TS
    return pl.pallas_call(
        functools.partial(_rank_kernel, n_ctx_tok),
        grid=(n_tiles,),
        in_specs=[
            pl.BlockSpec((TS, D), lambda i: (i, 0)),
            pl.BlockSpec((MOD_ROWS, N_MOD * D), lambda i: (0, 0)),
            pl.BlockSpec((N_EXP, D), lambda i: (0, 0)),
            pl.BlockSpec((N_EXP, V7X_LANES), lambda i: (0, 0)),
        ],
        out_specs=[
            pl.BlockSpec((1, V7X_SUBLANES, TS), lambda i: (i, 0, 0)),
            pl.BlockSpec(memory_space=pltpu.SMEM),
        ],
        out_shape=[
            jax.ShapeDtypeStruct((n_tiles, V7X_SUBLANES, TS), F32),
            jax.ShapeDtypeStruct((n_tiles, N_EXP), jnp.int32),
        ],
        compiler_params=pltpu.CompilerParams(
            dimension_semantics=("arbitrary",), vmem_limit_bytes=VMEM_LIMIT),
        name="moe_rank",
    )(x, mod_l, wrt, brt)


def _dispatch_kernel(n_ctx_tok, off_ref, fill_ref, x_ref, mod_ref, rinfo_ref, xg_ref, info_ref,
                     u_scr, pos_v, pos_s, zero_scr, sem, psem, zsem):
    i = pl.program_id(0)
    n_steps = pl.num_programs(0)
    slot = i % 2
    _to_token_major(u_scr.at[slot], _ffn_prologue(n_ctx_tok, x_ref, mod_ref))
    rinfo = rinfo_ref[0]
    i1 = rinfo[0:1]
    i2 = rinfo[1:2]
    off1 = jnp.zeros_like(i1)
    off2 = jnp.zeros_like(i2)
    for e in range(N_EXP):
        start = off_ref[i * N_EXP + e].astype(F32)
        off1 = jnp.where(i1 == float(e), start, off1)
        off2 = jnp.where(i2 == float(e), start, off2)
    info = jnp.concatenate(
        [off1 + rinfo[2:3], off2 + rinfo[3:4], rinfo[4:6], jnp.zeros((V7X_SUBLANES - 4, TS), F32)],
        axis=0)
    info_ref[0] = info
    pos_v[...] = info.astype(jnp.int32)
    cp = pltpu.make_async_copy(pos_v, pos_s, psem)
    cp.start()
    cp.wait()

    def row_tile(ref, row):
        return ref.at[pl.ds(pl.multiple_of(row * N_LT, N_LT), N_LT)]

    def row_copy(tok, dst_row, s):
        return pltpu.make_async_copy(row_tile(u_scr.at[s], tok), row_tile(xg_ref, dst_row), sem.at[s])

    def wait_tile(s):
        for _ in range(2):
            pltpu.make_async_copy(u_scr.at[s], xg_ref.at[pl.ds(0, TS * N_LT)], sem.at[s]).wait()

    @pl.when(i > 0)
    def _():
        wait_tile(1 - slot)

    def issue(tok, carry):
        row_copy(tok, pos_s[0, tok], slot).start(priority=0)
        row_copy(tok, pos_s[1, tok], slot).start(priority=1)
        return carry

    lax.fori_loop(0, TS, issue, 0, unroll=8)

    @pl.when(i == n_steps - 1)
    def _():
        wait_tile(slot)
        zero_scr[...] = jnp.zeros_like(zero_scr)
        for e in range(N_EXP):
            end = fill_ref[e]
            n_tail = fill_ref[N_EXP + e]

            def tail_copy(k):
                return pltpu.make_async_copy(row_tile(zero_scr, 0), row_tile(xg_ref, end + k), zsem)

            def tail_start(k, carry):
                tail_copy(k).start()
                return carry

            def tail_wait(k, carry):
                tail_copy(k).wait()
                return carry

            lax.fori_loop(0, n_tail, tail_start, 0)
            lax.fori_loop(0, n_tail, tail_wait, 0)

        def tile_copy(k):
            rows = TR * N_LT
            return pltpu.make_async_copy(zero_scr, xg_ref.at[pl.ds(pl.multiple_of(k * rows, rows), rows)], zsem)

        def tile_start(k, carry):
            tile_copy(k).start()
            return carry

        def tile_wait(k, carry):
            tile_copy(k).wait()
            return carry

        lax.fori_loop(fill_ref[2 * N_EXP], MAX_RT, tile_start, 0)
        lax.fori_loop(fill_ref[2 * N_EXP], MAX_RT, tile_wait, 0)


def _dispatch(x, mod_l, rinfo, off, fill, n_ctx_tok):
    n_tok = x.shape[0]
    n_tiles = n_tok // TS
    return pl.pallas_call(
        functools.partial(_dispatch_kernel, n_ctx_tok),
        grid_spec=pltpu.PrefetchScalarGridSpec(
            num_scalar_prefetch=2,
            grid=(n_tiles,),
            in_specs=[
                pl.BlockSpec((TS, D), lambda i, off, fill: (i, 0)),
                pl.BlockSpec((MOD_ROWS, N_MOD * D), lambda i, off, fill: (0, 0)),
                pl.BlockSpec((1, V7X_SUBLANES, TS), lambda i, off, fill: (i, 0, 0)),
            ],
            out_specs=[
                pl.BlockSpec(memory_space=pl.ANY),
                pl.BlockSpec((1, V7X_SUBLANES, TS), lambda i, off, fill: (i, 0, 0)),
            ],
            scratch_shapes=[
                pltpu.VMEM((2, TS * N_LT, V7X_LANES), F32),
                pltpu.VMEM((V7X_SUBLANES, TS), jnp.int32),
                pltpu.SMEM((V7X_SUBLANES, TS), jnp.int32),
                pltpu.VMEM((TR * N_LT, V7X_LANES), F32),
                pltpu.SemaphoreType.DMA((2,)),
                pltpu.SemaphoreType.DMA,
                pltpu.SemaphoreType.DMA,
            ],
        ),
        out_shape=[
            jax.ShapeDtypeStruct((MAX_RT * TR * N_LT, V7X_LANES), F32),
            jax.ShapeDtypeStruct((n_tiles, V7X_SUBLANES, TS), F32),
        ],
        compiler_params=pltpu.CompilerParams(
            dimension_semantics=("arbitrary",), vmem_limit_bytes=VMEM_LIMIT),
        name="moe_dispatch",
    )(off, fill, x, mod_l, rinfo)


def _expert_kernel(second, exp_ref, new_ref, nact_ref, *refs):
    if second:
        xg_ref, w1_ref, w3_ref, w2_ref, yp_ref, o_ref, w1b, w3b, w2b = refs
    else:
        xg_ref, w1_ref, w3_ref, w2_ref, o_ref, w1b, w3b, w2b = refs
    r = pl.program_id(0)
    active = r < nact_ref[0]

    @pl.when(jnp.logical_and(active, new_ref[r] == 1))
    def _():
        w1b[...] = w1_ref[0].astype(BF16)
        w3b[...] = w3_ref[0].astype(BF16)
        w2b[...] = w2_ref[0].astype(BF16)

    @pl.when(active)
    def _():
        xb = _from_token_major(xg_ref, TR).astype(BF16)
        h1 = jnp.dot(xb, w1b[...], preferred_element_type=F32)
        h3 = jnp.dot(xb, w3b[...], preferred_element_type=F32)
        hid = (h1 * jax.nn.sigmoid(h1) * h3).astype(BF16)
        p = jnp.dot(hid, w2b[...], preferred_element_type=F32)
        if second:
            _to_token_major(o_ref, yp_ref[...] + p)
        else:
            o_ref[...] = p

    @pl.when(jnp.logical_not(active))
    def _():
        o_ref[...] = jnp.zeros_like(o_ref)


def _expert_pass(second, tables, xg, w1, w3, w2, yp=None):
    f = 1 if second else 0
    row = lambda r, ex, nw, na: (r, 0)
    tm_block = pl.BlockSpec((TR * N_LT, V7X_LANES), row)
    tm_shape = jax.ShapeDtypeStruct((MAX_RT * TR * N_LT, V7X_LANES), F32)
    in_specs = [
        tm_block,
        pl.BlockSpec((1, D, FC), lambda r, ex, nw, na: (ex[r], 0, f)),
        pl.BlockSpec((1, D, FC), lambda r, ex, nw, na: (ex[r], 0, f)),
        pl.BlockSpec((1, FC, D), lambda r, ex, nw, na: (ex[r], f, 0)),
    ]
    args = [xg, w1, w3, w2]
    if second:
        in_specs.append(pl.BlockSpec((TR, D), row))
        args.append(yp)
    return pl.pallas_call(
        functools.partial(_expert_kernel, second),
        grid_spec=pltpu.PrefetchScalarGridSpec(
            num_scalar_prefetch=3,
            grid=(MAX_RT,),
            in_specs=in_specs,
            out_specs=tm_block if second else pl.BlockSpec((TR, D), row),
            scratch_shapes=[pltpu.VMEM((D, FC), BF16), pltpu.VMEM((D, FC), BF16),
                            pltpu.VMEM((FC, D), BF16)],
        ),
        out_shape=tm_shape if second else jax.ShapeDtypeStruct((MAX_RT * TR, D), F32),
        compiler_params=pltpu.CompilerParams(
            dimension_semantics=("arbitrary",), vmem_limit_bytes=EXPERT_VMEM_LIMIT),
        name="moe_expert_hi" if second else "moe_expert_lo",
    )(*tables, *args)


def _combine_kernel(n_ctx_tok, x_ref, mod_ref, info_ref, info_next_ref, yg_ref, g_ref, b_ref, oc_ref, od_ref,
                    ya_scr, yb_scr, pos_v, pos_s, sem, psem):
    i = pl.program_id(0)
    n_steps = pl.num_programs(0)
    slot = i % 2

    def gather_tile(info, s):
        pos_v[...] = info.astype(jnp.int32)
        cp = pltpu.make_async_copy(pos_v, pos_s, psem)
        cp.start()
        cp.wait()

        def row_tile(ref, row):
            return ref.at[pl.ds(pl.multiple_of(row * N_LT, N_LT), N_LT)]

        def issue(tok, carry):
            pltpu.make_async_copy(row_tile(yg_ref, pos_s[0, tok]), row_tile(ya_scr.at[s], tok),
                                  sem.at[s]).start(priority=0)
            pltpu.make_async_copy(row_tile(yg_ref, pos_s[1, tok]), row_tile(yb_scr.at[s], tok),
                                  sem.at[s]).start(priority=1)
            return carry

        lax.fori_loop(0, TS, issue, 0, unroll=8)

    @pl.when(i == 0)
    def _():
        gather_tile(info_ref[0], 0)

    @pl.when(i + 1 < n_steps)
    def _():
        gather_tile(info_next_ref[0], 1 - slot)

    padded = jnp.concatenate([info_ref[0], jnp.zeros((V7X_LANES - V7X_SUBLANES, TS), F32)], axis=0)
    cols = jnp.transpose(padded, (1, 0))
    w1c = cols[:, 2:3]
    w2c = cols[:, 3:4]
    pltpu.make_async_copy(yg_ref.at[pl.ds(0, TS * N_LT)], ya_scr.at[slot], sem.at[slot]).wait()
    pltpu.make_async_copy(yg_ref.at[pl.ds(0, TS * N_LT)], yb_scr.at[slot], sem.at[slot]).wait()
    acc = w1c * _from_token_major(ya_scr.at[slot], TS) + w2c * _from_token_major(yb_scr.at[slot], TS)
    res = _ffn_epilogue(n_ctx_tok, x_ref, mod_ref, acc, g_ref, b_ref)
    is_dec = i * TS >= n_ctx_tok

    @pl.when(jnp.logical_not(is_dec))
    def _():
        oc_ref[...] = res

    @pl.when(is_dec)
    def _():
        od_ref[...] = res


def _combine(x, mod_l, info, yg, ln_g, ln_b, n_ctx_tok):
    n_tok = x.shape[0]
    n_tiles = n_tok // TS
    ncb = n_ctx_tok // TS
    return pl.pallas_call(
        functools.partial(_combine_kernel, n_ctx_tok),
        grid=(n_tiles,),
        in_specs=[
            pl.BlockSpec((TS, D), lambda i: (i, 0)),
            pl.BlockSpec((MOD_ROWS, N_MOD * D), lambda i: (0, 0)),
            pl.BlockSpec((1, V7X_SUBLANES, TS), lambda i: (i, 0, 0)),
            pl.BlockSpec((1, V7X_SUBLANES, TS), lambda i: (jnp.minimum(i + 1, n_tiles - 1), 0, 0)),
            pl.BlockSpec(memory_space=pl.ANY),
            pl.BlockSpec((1, D), lambda i: (0, 0)),
            pl.BlockSpec((1, D), lambda i: (0, 0)),
        ],
        out_specs=[
            pl.BlockSpec((TS, D), lambda i: (jnp.minimum(i, ncb - 1), 0)),
            pl.BlockSpec((TS, D), lambda i: (jnp.maximum(i - ncb, 0), 0)),
        ],
        out_shape=[
            jax.ShapeDtypeStruct((n_ctx_tok, D), F32),
            jax.ShapeDtypeStruct((n_tok - n_ctx_tok, D), F32),
        ],
        scratch_shapes=[
            pltpu.VMEM((2, TS * N_LT, V7X_LANES), F32),
            pltpu.VMEM((2, TS * N_LT, V7X_LANES), F32),
            pltpu.VMEM((V7X_SUBLANES, TS), jnp.int32),
            pltpu.SMEM((V7X_SUBLANES, TS), jnp.int32),
            pltpu.SemaphoreType.DMA((2,)),
            pltpu.SemaphoreType.DMA,
        ],
        compiler_params=pltpu.CompilerParams(
            dimension_semantics=("arbitrary",), vmem_limit_bytes=VMEM_LIMIT),
        name="moe_combine",
    )(x, mod_l, info, info, yg, ln_g, ln_b)


def _routing_tables(counts):
    totals = jnp.sum(counts, axis=0)
    n_rt = (totals + TR - 1) // TR
    cum = jnp.cumsum(n_rt)
    first = cum - n_rt
    n_act = cum[-1]
    off = first[None, :] * TR + (jnp.cumsum(counts, axis=0) - counts)
    fill = jnp.concatenate([first * TR + totals, n_rt * TR - totals, n_act[None]])
    r = jnp.arange(MAX_RT, dtype=jnp.int32)
    rc = jnp.minimum(r, n_act - 1)
    exp = jnp.sum((rc[:, None] >= cum[None, :]).astype(jnp.int32), axis=1)
    new = jnp.logical_and(r == first[exp], r < n_act)
    i32 = lambda a: a.astype(jnp.int32)
    return i32(off.reshape(-1)), i32(fill), (i32(exp), i32(new), i32(n_act.reshape(1)))


def _moe_ffn(x, mod_l, router_w, router_b, w1, w3, w2, ln_g, ln_b, n_ctx_tok):
    n_tok = x.shape[0]
    assert 2 * n_tok // TR + N_EXP == MAX_RT and N_FC == 2
    wrt = router_w.T
    brt = jnp.broadcast_to(router_b.reshape(N_EXP, 1), (N_EXP, V7X_LANES))
    rinfo, counts = _rank(x, mod_l, wrt, brt, n_ctx_tok)
    off, fill, tables = _routing_tables(counts)
    xg, info = _dispatch(x, mod_l, rinfo, off, fill, n_ctx_tok)
    y_lo = _expert_pass(False, tables, xg, w1, w3, w2)
    y = _expert_pass(True, tables, xg, w1, w3, w2, y_lo)
    return _combine(x, mod_l, info, y, ln_g, ln_b, n_ctx_tok)


def _pack_w13(w1, w3):
    lead = w1.shape[:-2]
    nl = len(lead)
    perm = tuple(range(nl)) + (nl + 1, nl, nl + 2)
    a = w1.reshape(lead + (D, N_FC, FC)).transpose(perm)
    b = w3.reshape(lead + (D, N_FC, FC)).transpose(perm)
    return jnp.concatenate([a, b], axis=-1).astype(BF16)


def _block_diag_chunks(w):
    per = CW // HEAD_D
    w4 = w.reshape(N_CH, per, HEAD_D, HEAD_D)
    eye = jnp.eye(per, dtype=w.dtype)
    return jnp.einsum("cgij,gk->cgikj", w4, eye).reshape(N_CH, CW, CW)


def kernel(x_prompt, x_sample, state_rglru, c, c_ctx, w_mod, b_mod, w_in, conv_a, w_a_out, conv_b, conv_b_bias, w_gate_a, b_gate_a, w_gate_x, b_gate_x, lru_lambda, w_b_out, w_o, ln1_g, ln1_b, ln2_g, ln2_b, ffn_w1, ffn_w3, ffn_w2, router_w, router_b, moe_w1, moe_w3, moe_w2):
    batch, seq, d = x_prompt.shape
    dec_batch, dec_seq, _ = x_sample.shape
    depth = w_mod.shape[0]
    assert (d, seq, dec_seq, depth) == (D, SEQ, DEC_SEQ, DEPTH)
    n_ctx_tok = batch * seq
    n_dec_tok = dec_batch * dec_seq
    assert n_ctx_tok % TM1 == 0 and TM1 == dec_seq and 1 + dec_batch <= MOD_ROWS
    n_ctx_tiles = n_ctx_tok // TM1
    n_tiles = n_ctx_tiles + dec_batch
    seq_per_tile = TM1 // seq
    assert N_SEG == 2 * seq_per_tile

    assert depth % 2 == 0
    n_tok = n_ctx_tok + n_dec_tok
    x = (x_prompt.reshape(n_ctx_tok, D), x_sample.reshape(n_dec_tok, D))

    cond = jnp.zeros((MOD_ROWS, D), F32).at[0].set(c_ctx).at[1:1 + dec_batch].set(c)
    mod = _modulation(cond, w_mod, b_mod)

    states = []
    for l in range(depth):
        w_in_l = w_in[l].astype(BF16)
        w5 = jnp.stack([
            jnp.concatenate([w_in_l[:, g * D + j * CW:g * D + (j + 1) * CW] for g in range(5)], axis=1)
            for j in range(N_CH)])
        wg = w_in_l[:, 5 * D:7 * D]
        wbd = jnp.concatenate(
            [_block_diag_chunks(w_gate_a[l, 0]), _block_diag_chunks(w_gate_x[l, 0]),
             _block_diag_chunks(w_gate_a[l, 1]), _block_diag_chunks(w_gate_x[l, 1])],
            axis=-1).astype(BF16)
        gbias = jnp.stack([b_gate_a[l, 0], b_gate_x[l, 0], b_gate_a[l, 1], b_gate_x[l, 1]], axis=0)
        h0 = jnp.zeros((n_tiles, 2, N_SEG, D), F32)
        h0 = h0.at[n_ctx_tiles:, 0, 0].set(state_rglru[:, l, 0].astype(F32))
        h0 = h0.at[n_ctx_tiles:, 1, N_SEG - 1].set(state_rglru[:, l, 1].astype(F32))

        a_pre, b_pre, st = _mixer_part1(
            x, mod[l], w5, conv_a[l], conv_b[l], conv_b_bias[l].reshape(1, D), wbd, gbias,
            lru_lambda[l], h0, n_tok, n_ctx_tok)
        x = _mixer_part2(
            x, mod[l], a_pre, b_pre, wg, w_a_out[l].astype(BF16), w_b_out[l].astype(BF16),
            w_o[l].astype(BF16), ln1_g[l].reshape(1, D), ln1_b[l].reshape(1, D), n_ctx_tok)

        if l % 2 == 0:
            k = l // 2
            x = _dense_ffn(x, mod[l], _pack_w13(ffn_w1[k], ffn_w3[k]), ffn_w2[k].astype(BF16),
                           ln2_g[l].reshape(1, D), ln2_b[l].reshape(1, D), n_ctx_tok)
        else:
            k = l // 2
            x = _moe_ffn(x, mod[l], router_w[k], router_b[k], moe_w1[k], moe_w3[k], moe_w2[k],
                         ln2_g[l].reshape(1, D), ln2_b[l].reshape(1, D), n_ctx_tok)

        st_ctx = st[:n_ctx_tiles]
        fwd = st_ctx[:, 0, 1::2].reshape(batch, D)
        bwd = st_ctx[:, 1, 0::2].reshape(batch, D)
        states.append(jnp.stack([fwd, bwd], axis=1))

    y_prompt = x[0].reshape(batch, seq, D)
    y_sample = x[1].reshape(dec_batch, dec_seq, D)
    new_state = jnp.stack(states, axis=1).astype(x_prompt.dtype)
    return (y_prompt, y_sample, new_state)
```

```python
import functools

import jax
import jax.numpy as jnp
from jax import lax
from jax.experimental import pallas as pl
from jax.experimental.pallas import tpu as pltpu

F32 = jnp.float32
BF16 = jnp.bfloat16

D = 1024
SEQ = 256
DEC_SEQ = 1024
GRID_W = 64
N_HEAD = 16
HEAD_D = D // N_HEAD
RGLRU_C = 8.0
LOG2_E = 1.4426950408889634
D_FF = 2816
N_EXP = 8
N_MOD = 6
DEPTH = 2
DN_ALPHA = (2.0 * DEPTH) ** 0.25

V7X_SUBLANES = 8
V7X_LANES = 128
V7X_VMEM_BYTES = 64 * 1024 * 1024
VMEM_LIMIT = V7X_VMEM_BYTES - 12 * 1024 * 1024

TM1 = 1024
CW = 256
N_CH = D // CW
N_LC = CW // V7X_LANES
N_SEG = V7X_SUBLANES
SEG = TM1 // N_SEG
SEG_STRIDE = SEG + 8
CONV_GAP = V7X_SUBLANES
TM2 = 512
TM3 = 512
FC = 1408
N_FC = D_FF // FC
TS = 512
TR = 256
MAX_RT = 2 * 10240 // TR + N_EXP
EXPERT_VMEM_LIMIT = V7X_VMEM_BYTES - 6 * 1024 * 1024
MOD_ROWS = 8
MOD_BLK = 1536


def _sigmoid(x):
    return 0.5 * jnp.tanh(0.5 * x) + 0.5


def _ln_plain(x, eps):
    mu = jnp.mean(x, axis=-1, keepdims=True)
    xc = x - mu
    var = jnp.mean(xc * xc, axis=-1, keepdims=True)
    return xc * lax.rsqrt(var + eps)


def _token_source(xs, tile, n_ctx_tok):
    ncb = n_ctx_tok // tile
    if isinstance(xs, tuple):
        x_ctx, x_dec = xs
        dec_off = 0
    else:
        x_ctx = x_dec = xs
        dec_off = ncb
    ctx_map = lambda i: (jnp.minimum(i, ncb - 1), 0)
    dec_map = lambda i: (jnp.maximum(i - ncb, 0) + dec_off, 0)
    return x_ctx, x_dec, ctx_map, dec_map


def _mod_row(tok0, n_ctx_tok):
    dec = jnp.maximum(tok0 - n_ctx_tok, 0) // DEC_SEQ
    return jnp.where(tok0 >= n_ctx_tok, 1 + dec, 0)


def _mod_kernel(cond_ref, w_ref, b_ref, o_ref):
    cnd = cond_ref[...]
    s = cnd * _sigmoid(cnd)
    o_ref[0] = jnp.dot(s, w_ref[0], preferred_element_type=F32,
                       precision=lax.Precision.HIGHEST) + b_ref[0]


def _modulation(cond, w_mod, b_mod):
    depth = w_mod.shape[0]
    n_out = w_mod.shape[2]
    return pl.pallas_call(
        _mod_kernel,
        grid=(depth, n_out // MOD_BLK),
        in_specs=[
            pl.BlockSpec((MOD_ROWS, D), lambda l, j: (0, 0)),
            pl.BlockSpec((1, D, MOD_BLK), lambda l, j: (l, 0, j)),
            pl.BlockSpec((1, 1, MOD_BLK), lambda l, j: (l, 0, j)),
        ],
        out_specs=pl.BlockSpec((1, MOD_ROWS, MOD_BLK), lambda l, j: (l, 0, j)),
        out_shape=jax.ShapeDtypeStruct((depth, MOD_ROWS, n_out), F32),
        compiler_params=pltpu.CompilerParams(
            dimension_semantics=("arbitrary", "arbitrary"), vmem_limit_bytes=VMEM_LIMIT),
        name="modulation",
    )(cond, w_mod, b_mod.reshape(depth, 1, n_out))


def _scan_dir(a_scr, b_scr, hl_scr, ac_scr, h0, keep, reverse):
    n_lc = a_scr.shape[0]

    def body(k, carry):
        kk = SEG - 1 - k if reverse else k
        idx = pl.ds(kk, N_SEG, stride=SEG_STRIDE)
        new = []
        for c in range(n_lc):
            h, acc = carry[c]
            a_k = a_scr[c, idx, :]
            h = a_k * h + b_scr[c, idx, :]
            acc = a_k * acc
            hl_scr[c, idx, :] = h
            ac_scr[c, idx, :] = acc
            new.append((h, acc))
        return tuple(new)

    init = tuple((jnp.zeros((N_SEG, V7X_LANES), F32), jnp.ones((N_SEG, V7X_LANES), F32))
                 for _ in range(n_lc))
    fin = lax.fori_loop(0, SEG, body, init, unroll=8)
    h_loc = jnp.concatenate([fin[c][0] for c in range(n_lc)], axis=1)
    a_tot = jnp.concatenate([fin[c][1] for c in range(n_lc)], axis=1)

    order = range(N_SEG - 1, -1, -1) if reverse else range(N_SEG)
    h_in = [None] * N_SEG
    prev = None
    for s in order:
        cur = h0[s:s + 1]
        if prev is not None:
            left = a_tot[prev:prev + 1] * h_in[prev] + h_loc[prev:prev + 1]
            cur = keep[s] * left + cur
        h_in[s] = cur
        prev = s
    h_in = jnp.concatenate(h_in, axis=0)
    return h_in, a_tot * h_in + h_loc


def _conv_stage(is_dec, proj_scr, ca_ref, cb_ref, cbias_ref, apre_ref, xr_scr, gap_scr):
    win = GRID_W
    n_win = TM1 // win
    stride = win + CONV_GAP
    ca = ca_ref[...]
    cb = cb_ref[...]
    bias = cbias_ref[...]
    pieces = [(w, c) for w in range(n_win) for c in range(N_LC)]
    sub = lax.broadcasted_iota(jnp.int32, (CONV_GAP, V7X_LANES), 0)
    joined = jnp.where(is_dec, jnp.float32(0.0), jnp.float32(1.0))
    zero_gap = jnp.zeros((CONV_GAP, V7X_LANES), F32)

    def stage(value_of):
        for c in range(N_LC):
            for w in range(n_win + 1):
                edge = w * win
                if edge % SEQ == 0:
                    gap = zero_gap
                else:
                    before = value_of(slice(edge - CONV_GAP, edge), c)
                    after = value_of(slice(edge, edge + CONV_GAP), c)
                    gap = joined * jnp.where(sub >= CONV_GAP - 2, before, jnp.where(sub == 0, after, 0.0))
                gap_scr[c, w * stride:w * stride + CONV_GAP, :] = gap
        for w, c in pieces:
            lo = CONV_GAP + w * stride
            gap_scr[c, lo:lo + win, :] = value_of(slice(w * win, (w + 1) * win), c)

    def tap(w, c, shift):
        lo = CONV_GAP + w * stride + shift
        return gap_scr[c, lo:lo + win, :]

    def chunk_cols(k, c):
        return slice(k * CW + c * V7X_LANES, k * CW + (c + 1) * V7X_LANES)

    stage(lambda rows, c: proj_scr[rows, chunk_cols(2, c)] * proj_scr[rows, chunk_cols(0, c)])
    for w, c in pieces:
        rows = slice(w * win, (w + 1) * win)
        lanes = slice(c * V7X_LANES, (c + 1) * V7X_LANES)
        conv = ca[0:1, lanes] * tap(w, c, -1) + ca[1:2, lanes] * tap(w, c, 0) + ca[2:3, lanes] * tap(w, c, 1)
        apre_ref[rows, lanes] = (proj_scr[rows, chunk_cols(1, c)] * conv).astype(BF16)
    stage(lambda rows, c: proj_scr[rows, chunk_cols(4, c)])
    for w, c in pieces:
        rows = slice(w * win, (w + 1) * win)
        lanes = slice(c * V7X_LANES, (c + 1) * V7X_LANES)
        xr_scr[rows, lanes] = (cb[0:1, lanes] * tap(w, c, -2) + cb[1:2, lanes] * tap(w, c, -1)
                               + cb[2:3, lanes] * tap(w, c, 0) + cb[3:4, lanes] * tap(w, c, 1)
                               + bias[:, lanes])


def _mix1_kernel(n_ctx_tiles, xc_ref, xd_ref, mod_ref, w5_ref, ca_ref, cb_ref, cbias_ref, wbd_ref, gb_ref,
                 lam_ref, h0_ref, apre_ref, bpre_ref, st_ref,
                 u_scr, proj_scr, a_scr, b_scr, hl_scr, ac_scr, hsum_scr, xr_scr, gap_scr):
    i = pl.program_id(0)
    j = pl.program_id(1)
    is_dec = i >= n_ctx_tiles

    @pl.when(j == 0)
    def _():
        row = jnp.where(is_dec, i - (n_ctx_tiles - 1), 0)
        m = mod_ref[pl.ds(row, 1), :]
        sh1 = m[:, 0:D]
        sc1 = m[:, D:2 * D]
        x = jnp.where(is_dec, xd_ref[...], xc_ref[...])
        u_scr[...] = (_ln_plain(x, 1e-6) * (1.0 + sc1) + sh1).astype(BF16)

    proj_scr[...] = jnp.dot(u_scr[...], w5_ref[0], preferred_element_type=F32)

    _conv_stage(is_dec, proj_scr, ca_ref, cb_ref, cbias_ref, apre_ref, xr_scr, gap_scr)

    xr = xr_scr[...]
    gates = jnp.dot(xr.astype(BF16), wbd_ref[0], preferred_element_type=F32)
    gb = gb_ref[...]
    lam = lam_ref[...]
    sp = jnp.maximum(-lam, 0.0) + jnp.log1p(jnp.exp(-jnp.abs(lam)))
    rate = (-RGLRU_C * LOG2_E) * sp

    one = jnp.float32(1.0)
    for d in range(2):
        ga = gates[:, (2 * d) * CW:(2 * d + 1) * CW] + gb[2 * d:2 * d + 1]
        gx = gates[:, (2 * d + 1) * CW:(2 * d + 2) * CW] + gb[2 * d + 1:2 * d + 2]
        r = _sigmoid(ga)
        ig = _sigmoid(gx)
        a = jnp.exp2(r * rate[d:d + 1])
        y = 1.0 - a * a
        bt = jnp.where(y > 0.0, y * lax.rsqrt(y), 0.0) * (ig * xr)
        for s in range(N_SEG):
            lo = s * SEG_STRIDE
            for c in range(N_LC):
                lanes = slice(c * V7X_LANES, (c + 1) * V7X_LANES)
                a_scr[c, lo:lo + SEG, :] = a[s * SEG:(s + 1) * SEG, lanes]
                b_scr[c, lo:lo + SEG, :] = bt[s * SEG:(s + 1) * SEG, lanes]
        if d == 0:
            keep = [jnp.where(is_dec, one, jnp.float32(s % 2 == 1)) for s in range(N_SEG)]
        else:
            keep = [jnp.where(is_dec, one, jnp.float32(s % 2 == 0)) for s in range(N_SEG)]
        h_in, h_out = _scan_dir(a_scr, b_scr, hl_scr, ac_scr, h0_ref[0, d], keep, reverse=(d == 1))
        st_ref[0, d] = h_out
        for s in range(N_SEG):
            lo = s * SEG_STRIDE
            for c in range(N_LC):
                lanes = slice(c * V7X_LANES, (c + 1) * V7X_LANES)
                h_seg = hl_scr[c, lo:lo + SEG, :] + ac_scr[c, lo:lo + SEG, :] * h_in[s:s + 1, lanes]
                if d == 0:
                    hsum_scr[s * SEG:(s + 1) * SEG, lanes] = h_seg
                else:
                    hsum_scr[s * SEG:(s + 1) * SEG, lanes] += h_seg

    bpre_ref[...] = (hsum_scr[...] * jax.nn.gelu(proj_scr[:, 3 * CW:4 * CW])).astype(BF16)


def _mixer_part1(xs, mod_l, w5, conv_a, conv_b, conv_b_bias, wbd, gbias, lam, h0, n_tok, n_ctx_tok):
    n_tiles = n_tok // TM1
    x_ctx, x_dec, ctx_map, dec_map = _token_source(xs, TM1, n_ctx_tok)
    kern = functools.partial(_mix1_kernel, n_ctx_tok // TM1)
    return pl.pallas_call(
        kern,
        grid=(n_tiles, N_CH),
        in_specs=[
            pl.BlockSpec((TM1, D), lambda i, j: ctx_map(i)),
            pl.BlockSpec((TM1, D), lambda i, j: dec_map(i)),
            pl.BlockSpec((MOD_ROWS, N_MOD * D), lambda i, j: (0, 0)),
            pl.BlockSpec((1, D, 5 * CW), lambda i, j: (j, 0, 0)),
            pl.BlockSpec((3, CW), lambda i, j: (0, j)),
            pl.BlockSpec((4, CW), lambda i, j: (0, j)),
            pl.BlockSpec((1, CW), lambda i, j: (0, j)),
            pl.BlockSpec((1, CW, 4 * CW), lambda i, j: (j, 0, 0)),
            pl.BlockSpec((4, CW), lambda i, j: (0, j)),
            pl.BlockSpec((2, CW), lambda i, j: (0, j)),
            pl.BlockSpec((1, 2, N_SEG, CW), lambda i, j: (i, 0, 0, j)),
        ],
        out_specs=[
            pl.BlockSpec((TM1, CW), lambda i, j: (i, j)),
            pl.BlockSpec((TM1, CW), lambda i, j: (i, j)),
            pl.BlockSpec((1, 2, N_SEG, CW), lambda i, j: (i, 0, 0, j)),
        ],
        out_shape=[
            jax.ShapeDtypeStruct((n_tok, D), BF16),
            jax.ShapeDtypeStruct((n_tok, D), BF16),
            jax.ShapeDtypeStruct((n_tiles, 2, N_SEG, D), F32),
        ],
        scratch_shapes=[
            pltpu.VMEM((TM1, D), BF16),
            pltpu.VMEM((TM1, 5 * CW), F32),
            pltpu.VMEM((N_LC, N_SEG * SEG_STRIDE, V7X_LANES), F32),
            pltpu.VMEM((N_LC, N_SEG * SEG_STRIDE, V7X_LANES), F32),
            pltpu.VMEM((N_LC, N_SEG * SEG_STRIDE, V7X_LANES), F32),
            pltpu.VMEM((N_LC, N_SEG * SEG_STRIDE, V7X_LANES), F32),
            pltpu.VMEM((TM1, CW), F32),
            pltpu.VMEM((TM1, CW), F32),
            pltpu.VMEM((N_LC, CONV_GAP + (TM1 // GRID_W) * (GRID_W + CONV_GAP), V7X_LANES), F32),
        ],
        compiler_params=pltpu.CompilerParams(
            dimension_semantics=("arbitrary", "arbitrary"), vmem_limit_bytes=VMEM_LIMIT),
        name="mixer_scan",
    )(x_ctx, x_dec, mod_l, w5, conv_a, conv_b, conv_b_bias, wbd, gbias, lam, h0)


def _mix2_kernel(n_ctx_tok, xc_ref, xd_ref, mod_ref, ap_ref, bp_ref, wg_ref, wa_ref, wb_ref, wo_ref,
                 g_ref, b_ref, o_ref):
    i = pl.program_id(0)
    m = mod_ref[pl.ds(_mod_row(i * TM2, n_ctx_tok), 1), :]
    sh1 = m[:, 0:D]
    sc1 = m[:, D:2 * D]
    g1 = m[:, 2 * D:3 * D]
    x = jnp.where(i * TM2 >= n_ctx_tok, xd_ref[...], xc_ref[...])
    u = (_ln_plain(x, 1e-6) * (1.0 + sc1) + sh1).astype(BF16)
    gates = jnp.dot(u, wg_ref[...], preferred_element_type=F32)
    br_a = jnp.dot(ap_ref[...], wa_ref[...], preferred_element_type=F32)
    br_b = jnp.dot(bp_ref[...], wb_ref[...], preferred_element_type=F32)
    merged = _sigmoid(gates[:, 0:D]) * br_a + _sigmoid(gates[:, D:2 * D]) * br_b
    mix = jnp.dot(merged.astype(BF16), wo_ref[...], preferred_element_type=F32)
    y = DN_ALPHA * x + g1 * mix
    o_ref[...] = _ln_plain(y, 1e-5) * g_ref[...] + b_ref[...]


def _mixer_part2(xs, mod_l, a_pre, b_pre, wg, wa, wb, wo, ln_g, ln_b, n_ctx_tok):
    n_tok = a_pre.shape[0]
    x_ctx, x_dec, ctx_map, dec_map = _token_source(xs, TM2, n_ctx_tok)
    const = lambda i: (0, 0)
    tile = lambda i: (i, 0)
    return pl.pallas_call(
        functools.partial(_mix2_kernel, n_ctx_tok),
        grid=(n_tok // TM2,),
        in_specs=[
            pl.BlockSpec((TM2, D), ctx_map),
            pl.BlockSpec((TM2, D), dec_map),
            pl.BlockSpec((MOD_ROWS, N_MOD * D), const),
            pl.BlockSpec((TM2, D), tile),
            pl.BlockSpec((TM2, D), tile),
            pl.BlockSpec((D, 2 * D), const),
            pl.BlockSpec((D, D), const),
            pl.BlockSpec((D, D), const),
            pl.BlockSpec((D, D), const),
            pl.BlockSpec((1, D), const),
            pl.BlockSpec((1, D), const),
        ],
        out_specs=pl.BlockSpec((TM2, D), tile),
        out_shape=jax.ShapeDtypeStruct((n_tok, D), F32),
        compiler_params=pltpu.CompilerParams(
            dimension_semantics=("arbitrary",), vmem_limit_bytes=VMEM_LIMIT),
        name="mixer_out",
    )(x_ctx, x_dec, mod_l, a_pre, b_pre, wg, wa, wb, wo, ln_g, ln_b)


def _ffn_prologue(n_ctx_tok, x_ref, mod_ref):
    m = mod_ref[pl.ds(_mod_row(pl.program_id(0) * TM3, n_ctx_tok), 1), :]
    sh2 = m[:, 3 * D:4 * D]
    sc2 = m[:, 4 * D:5 * D]
    return _ln_plain(x_ref[...], 1e-6) * (1.0 + sc2) + sh2


def _ffn_epilogue(n_ctx_tok, x_ref, mod_ref, acc, g_ref, b_ref):
    m = mod_ref[pl.ds(_mod_row(pl.program_id(0) * TM3, n_ctx_tok), 1), :]
    g2 = m[:, 5 * D:6 * D]
    y = DN_ALPHA * x_ref[...] + g2 * acc
    return _ln_plain(y, 1e-5) * g_ref[...] + b_ref[...]


def _swiglu_chunk(u, w13, w2):
    h = jnp.dot(u, w13, preferred_element_type=F32)
    h1 = h[:, 0:FC]
    hid = (h1 * _sigmoid(h1) * h[:, FC:2 * FC]).astype(BF16)
    return jnp.dot(hid, w2, preferred_element_type=F32)


def _dense_ffn_kernel(n_ctx_tok, x_ref, mod_ref, w13_ref, w2_ref, g_ref, b_ref, o_ref, u_scr, acc_scr):
    f = pl.program_id(1)

    @pl.when(f == 0)
    def _():
        u_scr[...] = _ffn_prologue(n_ctx_tok, x_ref, mod_ref).astype(BF16)

    p = _swiglu_chunk(u_scr[...], w13_ref[0], w2_ref[...])

    @pl.when(f == 0)
    def _():
        acc_scr[...] = p

    @pl.when(f > 0)
    def _():
        acc_scr[...] += p

    @pl.when(f == N_FC - 1)
    def _():
        o_ref[...] = _ffn_epilogue(n_ctx_tok, x_ref, mod_ref, acc_scr[...], g_ref, b_ref)


def _dense_ffn(x, mod_l, w13, w2, ln_g, ln_b, n_ctx_tok):
    n_tok = x.shape[0]
    return pl.pallas_call(
        functools.partial(_dense_ffn_kernel, n_ctx_tok),
        grid=(n_tok // TM3, N_FC),
        in_specs=[
            pl.BlockSpec((TM3, D), lambda i, f: (i, 0)),
            pl.BlockSpec((MOD_ROWS, N_MOD * D), lambda i, f: (0, 0)),
            pl.BlockSpec((1, D, 2 * FC), lambda i, f: (f, 0, 0)),
            pl.BlockSpec((FC, D), lambda i, f: (f, 0)),
            pl.BlockSpec((1, D), lambda i, f: (0, 0)),
            pl.BlockSpec((1, D), lambda i, f: (0, 0)),
        ],
        out_specs=pl.BlockSpec((TM3, D), lambda i, f: (i, 0)),
        out_shape=jax.ShapeDtypeStruct((n_tok, D), F32),
        scratch_shapes=[pltpu.VMEM((TM3, D), BF16), pltpu.VMEM((TM3, D), F32)],
        compiler_params=pltpu.CompilerParams(
            dimension_semantics=("arbitrary", "arbitrary"), vmem_limit_bytes=VMEM_LIMIT),
        name="dense_ffn",
    )(x, mod_l, w13, w2, ln_g, ln_b)


N_LT = D // V7X_LANES


def _to_token_major(ref, rows):
    n = rows.shape[0]
    for c in range(N_LT):
        ref[pl.ds(c, n, stride=N_LT), :] = rows[:, c * V7X_LANES:(c + 1) * V7X_LANES]


def _from_token_major(ref, n):
    return jnp.concatenate([ref[pl.ds(c, n, stride=N_LT), :] for c in range(N_LT)], axis=1)


def _rank_kernel(n_ctx_tok, x_ref, mod_ref, wrt_ref, brt_ref, info_ref, cnt_ref, utm_ref):
    i = pl.program_id(0)
    u2 = _ffn_prologue(n_ctx_tok, x_ref, mod_ref)
    _to_token_major(utm_ref, u2)
    lg = lax.dot_general(wrt_ref[...], u2, (((1,), (1,)), ((), ())), preferred_element_type=F32,
                         precision=lax.Precision.HIGHEST) + brt_ref[:, 0:1]
    eidx = lax.broadcasted_iota(jnp.int32, lg.shape, 0).astype(F32)
    neg = jnp.float32(-jnp.inf)
    v1 = jnp.max(lg, axis=0, keepdims=True)
    i1 = jnp.min(jnp.where(lg == v1, eidx, float(N_EXP)), axis=0, keepdims=True)
    lg2 = jnp.where(eidx == i1, neg, lg)
    v2 = jnp.max(lg2, axis=0, keepdims=True)
    i2 = jnp.min(jnp.where(lg2 == v2, eidx, float(N_EXP)), axis=0, keepdims=True)
    t = jnp.exp(v2 - v1)
    w_top = 1.0 / (1.0 + t)
    m1 = eidx == i1
    m2 = eidx == i2
    member = jnp.where(jnp.logical_or(m1, m2), 1.0, 0.0)
    before = (lax.broadcasted_iota(jnp.int32, (TS, TS), 0)
              < lax.broadcasted_iota(jnp.int32, (TS, TS), 1))
    rank = jnp.dot(member.astype(BF16), jnp.where(before, 1.0, 0.0).astype(BF16),
                   preferred_element_type=F32)
    for e in range(N_EXP):
        cnt_ref[i, e] = jnp.sum(member[e:e + 1, :]).astype(jnp.int32)
    rank1 = jnp.sum(jnp.where(m1, rank, 0.0), axis=0, keepdims=True)
    rank2 = jnp.sum(jnp.where(m2, rank, 0.0), axis=0, keepdims=True)
    info_ref[0] = jnp.concatenate(
        [i1, i2, rank1, rank2, w_top, t * w_top, jnp.zeros((V7X_SUBLANES - 6, TS), F32)], axis=0)


def _rank(x, mod_l, wrt, brt, n_ctx_tok):
    n_tiles = x.shape[0] // TS
    return pl.pallas_call(
        functools.partial(_rank_kernel, n_ctx_tok),
        grid=(n_tiles,),
        in_specs=[
            pl.BlockSpec((TS, D), lambda i: (i, 0)),
            pl.BlockSpec((MOD_ROWS, N_MOD * D), lambda i: (0, 0)),
            pl.BlockSpec((N_EXP, D), lambda i: (0, 0)),
            pl.BlockSpec((N_EXP, V7X_LANES), lambda i: (0, 0)),
        ],
        out_specs=[
            pl.BlockSpec((1, V7X_SUBLANES, TS), lambda i: (i, 0, 0)),
            pl.BlockSpec(memory_space=pltpu.SMEM),
            pl.BlockSpec((TS * N_LT, V7X_LANES), lambda i: (i, 0)),
        ],
        out_shape=[
            jax.ShapeDtypeStruct((n_tiles, V7X_SUBLANES, TS), F32),
            jax.ShapeDtypeStruct((n_tiles, N_EXP), jnp.int32),
            jax.ShapeDtypeStruct((n_tiles * TS * N_LT, V7X_LANES), F32),
        ],
        compiler_params=pltpu.CompilerParams(
            dimension_semantics=("arbitrary",), vmem_limit_bytes=VMEM_LIMIT),
        name="moe_rank",
    )(x, mod_l, wrt, brt)


def _dispatch_kernel(off_ref, fill_ref, rinfo_ref, utm_ref, xg_ref, info_ref,
                     pos_v, pos_s, zero_scr, sem, psem, zsem):
    i = pl.program_id(0)
    n_steps = pl.num_programs(0)
    slot = i % 2
    rinfo = rinfo_ref[0]
    i1 = rinfo[0:1]
    i2 = rinfo[1:2]
    off1 = jnp.zeros_like(i1)
    off2 = jnp.zeros_like(i2)
    for e in range(N_EXP):
        start = off_ref[i * N_EXP + e].astype(F32)
        off1 = jnp.where(i1 == float(e), start, off1)
        off2 = jnp.where(i2 == float(e), start, off2)
    info = jnp.concatenate(
        [off1 + rinfo[2:3], off2 + rinfo[3:4], rinfo[4:6], jnp.zeros((V7X_SUBLANES - 4, TS), F32)],
        axis=0)
    info_ref[0] = info
    pos_v[...] = info.astype(jnp.int32)
    cp = pltpu.make_async_copy(pos_v, pos_s, psem)
    cp.start()
    cp.wait()

    def row_tile(ref, row):
        return ref.at[pl.ds(pl.multiple_of(row * N_LT, N_LT), N_LT)]

    def row_copy(tok, dst_row, s):
        return pltpu.make_async_copy(row_tile(utm_ref, i * TS + tok), row_tile(xg_ref, dst_row), sem.at[s])

    def wait_tile(s):
        for _ in range(2):
            pltpu.make_async_copy(utm_ref.at[pl.ds(0, TS * N_LT)], xg_ref.at[pl.ds(0, TS * N_LT)],
                                  sem.at[s]).wait()

    @pl.when(i > 0)
    def _():
        wait_tile(1 - slot)

    def issue(tok, carry):
        row_copy(tok, pos_s[0, tok], slot).start(priority=0)
        row_copy(tok, pos_s[1, tok], slot).start(priority=1)
        return carry

    lax.fori_loop(0, TS, issue, 0, unroll=8)

    @pl.when(i == n_steps - 1)
    def _():
        wait_tile(slot)
        zero_scr[...] = jnp.zeros_like(zero_scr)
        for e in range(N_EXP):
            end = fill_ref[e]
            n_tail = fill_ref[N_EXP + e]

            def tail_copy(k):
                return pltpu.make_async_copy(row_tile(zero_scr, 0), row_tile(xg_ref, end + k), zsem)

            def tail_start(k, carry):
                tail_copy(k).start()
                return carry

            def tail_wait(k, carry):
                tail_copy(k).wait()
                return carry

            lax.fori_loop(0, n_tail, tail_start, 0)
            lax.fori_loop(0, n_tail, tail_wait, 0)

        def tile_copy(k):
            rows = TR * N_LT
            return pltpu.make_async_copy(zero_scr, xg_ref.at[pl.ds(pl.multiple_of(k * rows, rows), rows)], zsem)

        def tile_start(k, carry):
            tile_copy(k).start()
            return carry

        def tile_wait(k, carry):
            tile_copy(k).wait()
            return carry

        lax.fori_loop(fill_ref[2 * N_EXP], MAX_RT, tile_start, 0)
        lax.fori_loop(fill_ref[2 * N_EXP], MAX_RT, tile_wait, 0)


def _dispatch(utm, rinfo, off, fill):
    n_tiles = rinfo.shape[0]
    return pl.pallas_call(
        _dispatch_kernel,
        grid_spec=pltpu.PrefetchScalarGridSpec(
            num_scalar_prefetch=2,
            grid=(n_tiles,),
            in_specs=[
                pl.BlockSpec((1, V7X_SUBLANES, TS), lambda i, off, fill: (i, 0, 0)),
                pl.BlockSpec(memory_space=pl.ANY),
            ],
            out_specs=[
                pl.BlockSpec(memory_space=pl.ANY),
                pl.BlockSpec((1, V7X_SUBLANES, TS), lambda i, off, fill: (i, 0, 0)),
            ],
            scratch_shapes=[
                pltpu.VMEM((V7X_SUBLANES, TS), jnp.int32),
                pltpu.SMEM((V7X_SUBLANES, TS), jnp.int32),
                pltpu.VMEM((TR * N_LT, V7X_LANES), F32),
                pltpu.SemaphoreType.DMA((2,)),
                pltpu.SemaphoreType.DMA,
                pltpu.SemaphoreType.DMA,
            ],
        ),
        out_shape=[
            jax.ShapeDtypeStruct((MAX_RT * TR * N_LT, V7X_LANES), F32),
            jax.ShapeDtypeStruct((n_tiles, V7X_SUBLANES, TS), F32),
        ],
        compiler_params=pltpu.CompilerParams(
            dimension_semantics=("arbitrary",), vmem_limit_bytes=VMEM_LIMIT),
        name="moe_dispatch",
    )(off, fill, rinfo, utm)


def _expert_kernel(second, exp_ref, new_ref, nact_ref, *refs):
    if second:
        xg_ref, w1_ref, w3_ref, w2_ref, yp_ref, o_ref, w1b, w3b, w2b = refs
    else:
        xg_ref, w1_ref, w3_ref, w2_ref, o_ref, w1b, w3b, w2b = refs
    r = pl.program_id(0)
    active = r < nact_ref[0]

    @pl.when(jnp.logical_and(active, new_ref[r] == 1))
    def _():
        w1b[...] = w1_ref[0].astype(BF16)
        w3b[...] = w3_ref[0].astype(BF16)
        w2b[...] = w2_ref[0].astype(BF16)

    @pl.when(active)
    def _():
        xb = _from_token_major(xg_ref, TR).astype(BF16)
        h1 = jnp.dot(xb, w1b[...], preferred_element_type=F32)
        h3 = jnp.dot(xb, w3b[...], preferred_element_type=F32)
        hid = (h1 * _sigmoid(h1) * h3).astype(BF16)
        p = jnp.dot(hid, w2b[...], preferred_element_type=F32)
        if second:
            _to_token_major(o_ref, yp_ref[...] + p)
        else:
            o_ref[...] = p

    @pl.when(jnp.logical_not(active))
    def _():
        o_ref[...] = jnp.zeros_like(o_ref)


def _expert_pass(second, tables, xg, w1, w3, w2, yp=None):
    f = 1 if second else 0
    row = lambda r, ex, nw, na: (r, 0)
    tm_block = pl.BlockSpec((TR * N_LT, V7X_LANES), row)
    tm_shape = jax.ShapeDtypeStruct((MAX_RT * TR * N_LT, V7X_LANES), F32)
    in_specs = [
        tm_block,
        pl.BlockSpec((1, D, FC), lambda r, ex, nw, na: (ex[r], 0, f)),
        pl.BlockSpec((1, D, FC), lambda r, ex, nw, na: (ex[r], 0, f)),
        pl.BlockSpec((1, FC, D), lambda r, ex, nw, na: (ex[r], f, 0)),
    ]
    args = [xg, w1, w3, w2]
    if second:
        in_specs.append(pl.BlockSpec((TR, D), row))
        args.append(yp)
    return pl.pallas_call(
        functools.partial(_expert_kernel, second),
        grid_spec=pltpu.PrefetchScalarGridSpec(
            num_scalar_prefetch=3,
            grid=(MAX_RT,),
            in_specs=in_specs,
            out_specs=tm_block if second else pl.BlockSpec((TR, D), row),
            scratch_shapes=[pltpu.VMEM((D, FC), BF16), pltpu.VMEM((D, FC), BF16),
                            pltpu.VMEM((FC, D), BF16)],
        ),
        out_shape=tm_shape if second else jax.ShapeDtypeStruct((MAX_RT * TR, D), F32),
        compiler_params=pltpu.CompilerParams(
            dimension_semantics=("arbitrary",), vmem_limit_bytes=EXPERT_VMEM_LIMIT),
        name="moe_expert_hi" if second else "moe_expert_lo",
    )(*tables, *args)


def _combine_kernel(n_ctx_tok, x_ref, mod_ref, info_ref, info_next_ref, yg_ref, g_ref, b_ref, oc_ref, od_ref,
                    ya_scr, yb_scr, pos_v, pos_s, sem, psem):
    i = pl.program_id(0)
    n_steps = pl.num_programs(0)
    slot = i % 2

    def gather_tile(info, s):
        pos_v[...] = info.astype(jnp.int32)
        cp = pltpu.make_async_copy(pos_v, pos_s, psem)
        cp.start()
        cp.wait()

        def row_tile(ref, row):
            return ref.at[pl.ds(pl.multiple_of(row * N_LT, N_LT), N_LT)]

        def issue(tok, carry):
            pltpu.make_async_copy(row_tile(yg_ref, pos_s[0, tok]), row_tile(ya_scr.at[s], tok),
                                  sem.at[s]).start(priority=0)
            pltpu.make_async_copy(row_tile(yg_ref, pos_s[1, tok]), row_tile(yb_scr.at[s], tok),
                                  sem.at[s]).start(priority=1)
            return carry

        lax.fori_loop(0, TS, issue, 0, unroll=8)

    @pl.when(i == 0)
    def _():
        gather_tile(info_ref[0], 0)

    @pl.when(i + 1 < n_steps)
    def _():
        gather_tile(info_next_ref[0], 1 - slot)

    padded = jnp.concatenate([info_ref[0], jnp.zeros((V7X_LANES - V7X_SUBLANES, TS), F32)], axis=0)
    cols = jnp.transpose(padded, (1, 0))
    w1c = cols[:, 2:3]
    w2c = cols[:, 3:4]
    pltpu.make_async_copy(yg_ref.at[pl.ds(0, TS * N_LT)], ya_scr.at[slot], sem.at[slot]).wait()
    pltpu.make_async_copy(yg_ref.at[pl.ds(0, TS * N_LT)], yb_scr.at[slot], sem.at[slot]).wait()
    acc = w1c * _from_token_major(ya_scr.at[slot], TS) + w2c * _from_token_major(yb_scr.at[slot], TS)
    res = _ffn_epilogue(n_ctx_tok, x_ref, mod_ref, acc, g_ref, b_ref)
    is_dec = i * TS >= n_ctx_tok

    @pl.when(jnp.logical_not(is_dec))
    def _():
        oc_ref[...] = res

    @pl.when(is_dec)
    def _():
        od_ref[...] = res


def _combine(x, mod_l, info, yg, ln_g, ln_b, n_ctx_tok):
    n_tok = x.shape[0]
    n_tiles = n_tok // TS
    ncb = n_ctx_tok // TS
    return pl.pallas_call(
        functools.partial(_combine_kernel, n_ctx_tok),
        grid=(n_tiles,),
        in_specs=[
            pl.BlockSpec((TS, D), lambda i: (i, 0)),
            pl.BlockSpec((MOD_ROWS, N_MOD * D), lambda i: (0, 0)),
            pl.BlockSpec((1, V7X_SUBLANES, TS), lambda i: (i, 0, 0)),
            pl.BlockSpec((1, V7X_SUBLANES, TS), lambda i: (jnp.minimum(i + 1, n_tiles - 1), 0, 0)),
            pl.BlockSpec(memory_space=pl.ANY),
            pl.BlockSpec((1, D), lambda i: (0, 0)),
            pl.BlockSpec((1, D), lambda i: (0, 0)),
        ],
        out_specs=[
            pl.BlockSpec((TS, D), lambda i: (jnp.minimum(i, ncb - 1), 0)),
            pl.BlockSpec((TS, D), lambda i: (jnp.maximum(i - ncb, 0), 0)),
        ],
        out_shape=[
            jax.ShapeDtypeStruct((n_ctx_tok, D), F32),
            jax.ShapeDtypeStruct((n_tok - n_ctx_tok, D), F32),
        ],
        scratch_shapes=[
            pltpu.VMEM((2, TS * N_LT, V7X_LANES), F32),
            pltpu.VMEM((2, TS * N_LT, V7X_LANES), F32),
            pltpu.VMEM((V7X_SUBLANES, TS), jnp.int32),
            pltpu.SMEM((V7X_SUBLANES, TS), jnp.int32),
            pltpu.SemaphoreType.DMA((2,)),
            pltpu.SemaphoreType.DMA,
        ],
        compiler_params=pltpu.CompilerParams(
            dimension_semantics=("arbitrary",), vmem_limit_bytes=VMEM_LIMIT),
        name="moe_combine",
    )(x, mod_l, info, info, yg, ln_g, ln_b)


def _routing_tables(counts):
    totals = jnp.sum(counts, axis=0)
    n_rt = (totals + TR - 1) // TR
    cum = jnp.cumsum(n_rt)
    first = cum - n_rt
    n_act = cum[-1]
    off = first[None, :] * TR + (jnp.cumsum(counts, axis=0) - counts)
    fill = jnp.concatenate([first * TR + totals, n_rt * TR - totals, n_act[None]])
    r = jnp.arange(MAX_RT, dtype=jnp.int32)
    rc = jnp.minimum(r, n_act - 1)
    exp = jnp.sum((rc[:, None] >= cum[None, :]).astype(jnp.int32), axis=1)
    new = jnp.logical_and(r == first[exp], r < n_act)
    i32 = lambda a: a.astype(jnp.int32)
    return i32(off.reshape(-1)), i32(fill), (i32(exp), i32(new), i32(n_act.reshape(1)))


def _moe_ffn(x, mod_l, router_w, router_b, w1, w3, w2, ln_g, ln_b, n_ctx_tok):
    n_tok = x.shape[0]
    assert 2 * n_tok // TR + N_EXP == MAX_RT and N_FC == 2
    wrt = router_w.T
    brt = jnp.broadcast_to(router_b.reshape(N_EXP, 1), (N_EXP, V7X_LANES))
    rinfo, counts, utm = _rank(x, mod_l, wrt, brt, n_ctx_tok)
    off, fill, tables = _routing_tables(counts)
    xg, info = _dispatch(utm, rinfo, off, fill)
    y_lo = _expert_pass(False, tables, xg, w1, w3, w2)
    y = _expert_pass(True, tables, xg, w1, w3, w2, y_lo)
    return _combine(x, mod_l, info, y, ln_g, ln_b, n_ctx_tok)


def _pack_w13(w1, w3):
    lead = w1.shape[:-2]
    nl = len(lead)
    perm = tuple(range(nl)) + (nl + 1, nl, nl + 2)
    a = w1.reshape(lead + (D, N_FC, FC)).transpose(perm)
    b = w3.reshape(lead + (D, N_FC, FC)).transpose(perm)
    return jnp.concatenate([a, b], axis=-1).astype(BF16)


def _block_diag_chunks(w):
    tiled = jnp.tile(w.reshape(N_CH, CW, HEAD_D), (1, 1, CW // HEAD_D))
    blk = jnp.arange(CW) // HEAD_D
    return jnp.where(blk[:, None] == blk[None, :], tiled, 0.0)


def kernel(x_prompt, x_sample, state_rglru, c, c_ctx, w_mod, b_mod, w_in, conv_a, w_a_out, conv_b, conv_b_bias, w_gate_a, b_gate_a, w_gate_x, b_gate_x, lru_lambda, w_b_out, w_o, ln1_g, ln1_b, ln2_g, ln2_b, ffn_w1, ffn_w3, ffn_w2, router_w, router_b, moe_w1, moe_w3, moe_w2):
    batch, seq, d = x_prompt.shape
    dec_batch, dec_seq, _ = x_sample.shape
    depth = w_mod.shape[0]
    assert (d, seq, dec_seq, depth) == (D, SEQ, DEC_SEQ, DEPTH)
    n_ctx_tok = batch * seq
    n_dec_tok = dec_batch * dec_seq
    assert n_ctx_tok % TM1 == 0 and TM1 == dec_seq and 1 + dec_batch <= MOD_ROWS
    n_ctx_tiles = n_ctx_tok // TM1
    n_tiles = n_ctx_tiles + dec_batch
    seq_per_tile = TM1 // seq
    assert N_SEG == 2 * seq_per_tile

    assert depth % 2 == 0
    n_tok = n_ctx_tok + n_dec_tok
    x = (x_prompt.reshape(n_ctx_tok, D), x_sample.reshape(n_dec_tok, D))

    cond = jnp.zeros((MOD_ROWS, D), F32).at[0].set(c_ctx).at[1:1 + dec_batch].set(c)
    mod = _modulation(cond, w_mod, b_mod)

    states = []
    for l in range(depth):
        w_in_l = w_in[l].astype(BF16)
        w5 = jnp.stack([
            jnp.concatenate([w_in_l[:, g * D + j * CW:g * D + (j + 1) * CW] for g in range(5)], axis=1)
            for j in range(N_CH)])
        wg = w_in_l[:, 5 * D:7 * D]
        wbd = jnp.concatenate(
            [_block_diag_chunks(w_gate_a[l, 0]), _block_diag_chunks(w_gate_x[l, 0]),
             _block_diag_chunks(w_gate_a[l, 1]), _block_diag_chunks(w_gate_x[l, 1])],
            axis=-1).astype(BF16)
        gbias = jnp.stack([b_gate_a[l, 0], b_gate_x[l, 0], b_gate_a[l, 1], b_gate_x[l, 1]], axis=0)
        h0 = jnp.zeros((n_tiles, 2, N_SEG, D), F32)
        h0 = h0.at[n_ctx_tiles:, 0, 0].set(state_rglru[:, l, 0].astype(F32))
        h0 = h0.at[n_ctx_tiles:, 1, N_SEG - 1].set(state_rglru[:, l, 1].astype(F32))

        a_pre, b_pre, st = _mixer_part1(
            x, mod[l], w5, conv_a[l], conv_b[l], conv_b_bias[l].reshape(1, D), wbd, gbias,
            lru_lambda[l], h0, n_tok, n_ctx_tok)
        x = _mixer_part2(
            x, mod[l], a_pre, b_pre, wg, w_a_out[l].astype(BF16), w_b_out[l].astype(BF16),
            w_o[l].astype(BF16), ln1_g[l].reshape(1, D), ln1_b[l].reshape(1, D), n_ctx_tok)

        if l % 2 == 0:
            k = l // 2
            x = _dense_ffn(x, mod[l], _pack_w13(ffn_w1[k], ffn_w3[k]), ffn_w2[k].astype(BF16),
                           ln2_g[l].reshape(1, D), ln2_b[l].reshape(1, D), n_ctx_tok)
        else:
            k = l // 2
            x = _moe_ffn(x, mod[l], router_w[k], router_b[k], moe_w1[k], moe_w3[k], moe_w2[k],
                         ln2_g[l].reshape(1, D), ln2_b[l].reshape(1, D), n_ctx_tok)

        st_ctx = st[:n_ctx_tiles]
        fwd = st_ctx[:, 0, 1::2].reshape(batch, D)
        bwd = st_ctx[:, 1, 0::2].reshape(batch, D)
        states.append(jnp.stack([fwd, bwd], axis=1))

    y_prompt = x[0].reshape(batch, seq, D)
    y_sample = x[1].reshape(dec_batch, dec_seq, D)
    new_state = jnp.stack(states, axis=1).astype(x_prompt.dtype)
    return (y_prompt, y_sample, new_state)
```

```python
import functools

import jax
import jax.numpy as jnp
from jax import lax
from jax.experimental import pallas as pl
from jax.experimental.pallas import tpu as pltpu

F32 = jnp.float32
BF16 = jnp.bfloat16

D = 1024
SEQ = 256
DEC_SEQ = 1024
GRID_W = 64
N_HEAD = 16
HEAD_D = D // N_HEAD
RGLRU_C = 8.0
LOG2_E = 1.4426950408889634
D_FF = 2816
N_EXP = 8
N_MOD = 6
DEPTH = 2
DN_ALPHA = (2.0 * DEPTH) ** 0.25

V7X_SUBLANES = 8
V7X_LANES = 128
V7X_VMEM_BYTES = 64 * 1024 * 1024
VMEM_LIMIT = V7X_VMEM_BYTES - 12 * 1024 * 1024

TM1 = 1024
CW = 256
N_CH = D // CW
N_LC = CW // V7X_LANES
N_SEG = V7X_SUBLANES
SEG = TM1 // N_SEG
SEG_STRIDE = SEG + 8
CONV_GAP = V7X_SUBLANES
TM2 = 512
TM3 = 512
FC = 1408
N_FC = D_FF // FC
TS = 512
TR = 256
MAX_RT = 2 * 10240 // TR + N_EXP
EXPERT_VMEM_LIMIT = V7X_VMEM_BYTES - 6 * 1024 * 1024
MOD_ROWS = 8
MOD_BLK = 1536


def _sigmoid(x):
    return 0.5 * jnp.tanh(0.5 * x) + 0.5


def _ln_plain(x, eps):
    mu = jnp.mean(x, axis=-1, keepdims=True)
    xc = x - mu
    var = jnp.mean(xc * xc, axis=-1, keepdims=True)
    return xc * lax.rsqrt(var + eps)


def _token_source(xs, tile, n_ctx_tok):
    ncb = n_ctx_tok // tile
    if isinstance(xs, tuple):
        x_ctx, x_dec = xs
        dec_off = 0
    else:
        x_ctx = x_dec = xs
        dec_off = ncb
    ctx_map = lambda i: (jnp.minimum(i, ncb - 1), 0)
    dec_map = lambda i: (jnp.maximum(i - ncb, 0) + dec_off, 0)
    return x_ctx, x_dec, ctx_map, dec_map


def _mod_row(tok0, n_ctx_tok):
    dec = jnp.maximum(tok0 - n_ctx_tok, 0) // DEC_SEQ
    return jnp.where(tok0 >= n_ctx_tok, 1 + dec, 0)


def _mod_kernel(cond_ref, w_ref, b_ref, o_ref):
    cnd = cond_ref[...]
    s = cnd * _sigmoid(cnd)
    o_ref[0] = jnp.dot(s, w_ref[0], preferred_element_type=F32,
                       precision=lax.Precision.HIGHEST) + b_ref[0]


def _modulation(cond, w_mod, b_mod):
    depth = w_mod.shape[0]
    n_out = w_mod.shape[2]
    return pl.pallas_call(
        _mod_kernel,
        grid=(depth, n_out // MOD_BLK),
        in_specs=[
            pl.BlockSpec((MOD_ROWS, D), lambda l, j: (0, 0)),
            pl.BlockSpec((1, D, MOD_BLK), lambda l, j: (l, 0, j)),
            pl.BlockSpec((1, 1, MOD_BLK), lambda l, j: (l, 0, j)),
        ],
        out_specs=pl.BlockSpec((1, MOD_ROWS, MOD_BLK), lambda l, j: (l, 0, j)),
        out_shape=jax.ShapeDtypeStruct((depth, MOD_ROWS, n_out), F32),
        compiler_params=pltpu.CompilerParams(
            dimension_semantics=("arbitrary", "arbitrary"), vmem_limit_bytes=VMEM_LIMIT),
        name="modulation",
    )(cond, w_mod, b_mod.reshape(depth, 1, n_out))


def _scan_dir(a_scr, b_scr, hl_scr, ac_scr, h0, keep, reverse):
    n_lc = a_scr.shape[0]

    def body(k, carry):
        kk = SEG - 1 - k if reverse else k
        idx = pl.ds(kk, N_SEG, stride=SEG_STRIDE)
        new = []
        for c in range(n_lc):
            h, acc = carry[c]
            a_k = a_scr[c, idx, :]
            h = a_k * h + b_scr[c, idx, :]
            acc = a_k * acc
            hl_scr[c, idx, :] = h
            ac_scr[c, idx, :] = acc
            new.append((h, acc))
        return tuple(new)

    init = tuple((jnp.zeros((N_SEG, V7X_LANES), F32), jnp.ones((N_SEG, V7X_LANES), F32))
                 for _ in range(n_lc))
    fin = lax.fori_loop(0, SEG, body, init, unroll=8)
    h_loc = jnp.concatenate([fin[c][0] for c in range(n_lc)], axis=1)
    a_tot = jnp.concatenate([fin[c][1] for c in range(n_lc)], axis=1)

    order = range(N_SEG - 1, -1, -1) if reverse else range(N_SEG)
    h_in = [None] * N_SEG
    prev = None
    for s in order:
        cur = h0[s:s + 1]
        if prev is not None:
            left = a_tot[prev:prev + 1] * h_in[prev] + h_loc[prev:prev + 1]
            cur = keep[s] * left + cur
        h_in[s] = cur
        prev = s
    h_in = jnp.concatenate(h_in, axis=0)
    return h_in, a_tot * h_in + h_loc


def _conv_stage(is_dec, proj_scr, ca_ref, cb_ref, cbias_ref, apre_ref, xr_scr, gap_scr):
    win = GRID_W
    n_win = TM1 // win
    stride = win + CONV_GAP
    ca = ca_ref[...]
    cb = cb_ref[...]
    bias = cbias_ref[...]
    pieces = [(w, c) for w in range(n_win) for c in range(N_LC)]
    sub = lax.broadcasted_iota(jnp.int32, (CONV_GAP, V7X_LANES), 0)
    joined = jnp.where(is_dec, jnp.float32(0.0), jnp.float32(1.0))
    zero_gap = jnp.zeros((CONV_GAP, V7X_LANES), F32)

    def stage(value_of):
        for c in range(N_LC):
            for w in range(n_win + 1):
                edge = w * win
                if edge % SEQ == 0:
                    gap = zero_gap
                else:
                    before = value_of(slice(edge - CONV_GAP, edge), c)
                    after = value_of(slice(edge, edge + CONV_GAP), c)
                    gap = joined * jnp.where(sub >= CONV_GAP - 2, before, jnp.where(sub == 0, after, 0.0))
                gap_scr[c, w * stride:w * stride + CONV_GAP, :] = gap
        for w, c in pieces:
            lo = CONV_GAP + w * stride
            gap_scr[c, lo:lo + win, :] = value_of(slice(w * win, (w + 1) * win), c)

    def tap(w, c, shift):
        lo = CONV_GAP + w * stride + shift
        return gap_scr[c, lo:lo + win, :]

    def chunk_cols(k, c):
        return slice(k * CW + c * V7X_LANES, k * CW + (c + 1) * V7X_LANES)

    stage(lambda rows, c: proj_scr[rows, chunk_cols(2, c)] * proj_scr[rows, chunk_cols(0, c)])
    for w, c in pieces:
        rows = slice(w * win, (w + 1) * win)
        lanes = slice(c * V7X_LANES, (c + 1) * V7X_LANES)
        conv = ca[0:1, lanes] * tap(w, c, -1) + ca[1:2, lanes] * tap(w, c, 0) + ca[2:3, lanes] * tap(w, c, 1)
        apre_ref[rows, lanes] = (proj_scr[rows, chunk_cols(1, c)] * conv).astype(BF16)
    stage(lambda rows, c: proj_scr[rows, chunk_cols(4, c)])
    for w, c in pieces:
        rows = slice(w * win, (w + 1) * win)
        lanes = slice(c * V7X_LANES, (c + 1) * V7X_LANES)
        xr_scr[rows, lanes] = (cb[0:1, lanes] * tap(w, c, -2) + cb[1:2, lanes] * tap(w, c, -1)
                               + cb[2:3, lanes] * tap(w, c, 0) + cb[3:4, lanes] * tap(w, c, 1)
                               + bias[:, lanes])


def _mix1_kernel(n_ctx_tiles, xc_ref, xd_ref, mod_ref, w5_ref, ca_ref, cb_ref, cbias_ref, wbd_ref, gb_ref,
                 lam_ref, h0_ref, apre_ref, bpre_ref, st_ref,
                 u_scr, proj_scr, a_scr, b_scr, hl_scr, ac_scr, hsum_scr, xr_scr, gap_scr):
    i = pl.program_id(0)
    j = pl.program_id(1)
    is_dec = i >= n_ctx_tiles

    @pl.when(j == 0)
    def _():
        row = jnp.where(is_dec, i - (n_ctx_tiles - 1), 0)
        m = mod_ref[pl.ds(row, 1), :]
        sh1 = m[:, 0:D]
        sc1 = m[:, D:2 * D]
        x = jnp.where(is_dec, xd_ref[...], xc_ref[...])
        u_scr[...] = (_ln_plain(x, 1e-6) * (1.0 + sc1) + sh1).astype(BF16)

    proj_scr[...] = jnp.dot(u_scr[...], w5_ref[0], preferred_element_type=F32)

    _conv_stage(is_dec, proj_scr, ca_ref, cb_ref, cbias_ref, apre_ref, xr_scr, gap_scr)

    xr = xr_scr[...]
    gates = jnp.dot(xr.astype(BF16), wbd_ref[0], preferred_element_type=F32)
    gb = gb_ref[...]
    lam = lam_ref[...]
    sp = jnp.maximum(-lam, 0.0) + jnp.log1p(jnp.exp(-jnp.abs(lam)))
    rate = (-RGLRU_C * LOG2_E) * sp

    one = jnp.float32(1.0)
    for d in range(2):
        ga = gates[:, (2 * d) * CW:(2 * d + 1) * CW] + gb[2 * d:2 * d + 1]
        gx = gates[:, (2 * d + 1) * CW:(2 * d + 2) * CW] + gb[2 * d + 1:2 * d + 2]
        r = _sigmoid(ga)
        ig = _sigmoid(gx)
        a = jnp.exp2(r * rate[d:d + 1])
        y = 1.0 - a * a
        bt = jnp.where(y > 0.0, y * lax.rsqrt(y), 0.0) * (ig * xr)
        for s in range(N_SEG):
            lo = s * SEG_STRIDE
            for c in range(N_LC):
                lanes = slice(c * V7X_LANES, (c + 1) * V7X_LANES)
                a_scr[c, lo:lo + SEG, :] = a[s * SEG:(s + 1) * SEG, lanes]
                b_scr[c, lo:lo + SEG, :] = bt[s * SEG:(s + 1) * SEG, lanes]
        if d == 0:
            keep = [jnp.where(is_dec, one, jnp.float32(s % 2 == 1)) for s in range(N_SEG)]
        else:
            keep = [jnp.where(is_dec, one, jnp.float32(s % 2 == 0)) for s in range(N_SEG)]
        h_in, h_out = _scan_dir(a_scr, b_scr, hl_scr, ac_scr, h0_ref[0, d], keep, reverse=(d == 1))
        st_ref[0, d] = h_out
        for s in range(N_SEG):
            lo = s * SEG_STRIDE
            for c in range(N_LC):
                lanes = slice(c * V7X_LANES, (c + 1) * V7X_LANES)
                h_seg = hl_scr[c, lo:lo + SEG, :] + ac_scr[c, lo:lo + SEG, :] * h_in[s:s + 1, lanes]
                if d == 0:
                    hsum_scr[s * SEG:(s + 1) * SEG, lanes] = h_seg
                else:
                    hsum_scr[s * SEG:(s + 1) * SEG, lanes] += h_seg

    bpre_ref[...] = (hsum_scr[...] * jax.nn.gelu(proj_scr[:, 3 * CW:4 * CW])).astype(BF16)


def _mixer_part1(xs, mod_l, w5, conv_a, conv_b, conv_b_bias, wbd, gbias, lam, h0, n_tok, n_ctx_tok):
    n_tiles = n_tok // TM1
    x_ctx, x_dec, ctx_map, dec_map = _token_source(xs, TM1, n_ctx_tok)
    kern = functools.partial(_mix1_kernel, n_ctx_tok // TM1)
    return pl.pallas_call(
        kern,
        grid=(n_tiles, N_CH),
        in_specs=[
            pl.BlockSpec((TM1, D), lambda i, j: ctx_map(i)),
            pl.BlockSpec((TM1, D), lambda i, j: dec_map(i)),
            pl.BlockSpec((MOD_ROWS, N_MOD * D), lambda i, j: (0, 0)),
            pl.BlockSpec((1, D, 5 * CW), lambda i, j: (j, 0, 0)),
            pl.BlockSpec((3, CW), lambda i, j: (0, j)),
            pl.BlockSpec((4, CW), lambda i, j: (0, j)),
            pl.BlockSpec((1, CW), lambda i, j: (0, j)),
            pl.BlockSpec((1, CW, 4 * CW), lambda i, j: (j, 0, 0)),
            pl.BlockSpec((4, CW), lambda i, j: (0, j)),
            pl.BlockSpec((2, CW), lambda i, j: (0, j)),
            pl.BlockSpec((1, 2, N_SEG, CW), lambda i, j: (i, 0, 0, j)),
        ],
        out_specs=[
            pl.BlockSpec((TM1, CW), lambda i, j: (i, j)),
            pl.BlockSpec((TM1, CW), lambda i, j: (i, j)),
            pl.BlockSpec((1, 2, N_SEG, CW), lambda i, j: (i, 0, 0, j)),
        ],
        out_shape=[
            jax.ShapeDtypeStruct((n_tok, D), BF16),
            jax.ShapeDtypeStruct((n_tok, D), BF16),
            jax.ShapeDtypeStruct((n_tiles, 2, N_SEG, D), F32),
        ],
        scratch_shapes=[
            pltpu.VMEM((TM1, D), BF16),
            pltpu.VMEM((TM1, 5 * CW), F32),
            pltpu.VMEM((N_LC, N_SEG * SEG_STRIDE, V7X_LANES), F32),
            pltpu.VMEM((N_LC, N_SEG * SEG_STRIDE, V7X_LANES), F32),
            pltpu.VMEM((N_LC, N_SEG * SEG_STRIDE, V7X_LANES), F32),
            pltpu.VMEM((N_LC, N_SEG * SEG_STRIDE, V7X_LANES), F32),
            pltpu.VMEM((TM1, CW), F32),
            pltpu.VMEM((TM1, CW), F32),
            pltpu.VMEM((N_LC, CONV_GAP + (TM1 // GRID_W) * (GRID_W + CONV_GAP), V7X_LANES), F32),
        ],
        compiler_params=pltpu.CompilerParams(
            dimension_semantics=("arbitrary", "arbitrary"), vmem_limit_bytes=VMEM_LIMIT),
        name="mixer_scan",
    )(x_ctx, x_dec, mod_l, w5, conv_a, conv_b, conv_b_bias, wbd, gbias, lam, h0)


def _mix2_kernel(n_ctx_tok, xc_ref, xd_ref, mod_ref, ap_ref, bp_ref, wg_ref, wa_ref, wb_ref, wo_ref,
                 g_ref, b_ref, o_ref):
    i = pl.program_id(0)
    m = mod_ref[pl.ds(_mod_row(i * TM2, n_ctx_tok), 1), :]
    sh1 = m[:, 0:D]
    sc1 = m[:, D:2 * D]
    g1 = m[:, 2 * D:3 * D]
    x = jnp.where(i * TM2 >= n_ctx_tok, xd_ref[...], xc_ref[...])
    u = (_ln_plain(x, 1e-6) * (1.0 + sc1) + sh1).astype(BF16)
    gates = jnp.dot(u, wg_ref[...], preferred_element_type=F32)
    br_a = jnp.dot(ap_ref[...], wa_ref[...], preferred_element_type=F32)
    br_b = jnp.dot(bp_ref[...], wb_ref[...], preferred_element_type=F32)
    merged = _sigmoid(gates[:, 0:D]) * br_a + _sigmoid(gates[:, D:2 * D]) * br_b
    mix = jnp.dot(merged.astype(BF16), wo_ref[...], preferred_element_type=F32)
    y = DN_ALPHA * x + g1 * mix
    o_ref[...] = _ln_plain(y, 1e-5) * g_ref[...] + b_ref[...]


def _mixer_part2(xs, mod_l, a_pre, b_pre, wg, wa, wb, wo, ln_g, ln_b, n_ctx_tok):
    n_tok = a_pre.shape[0]
    x_ctx, x_dec, ctx_map, dec_map = _token_source(xs, TM2, n_ctx_tok)
    const = lambda i: (0, 0)
    tile = lambda i: (i, 0)
    return pl.pallas_call(
        functools.partial(_mix2_kernel, n_ctx_tok),
        grid=(n_tok // TM2,),
        in_specs=[
            pl.BlockSpec((TM2, D), ctx_map),
            pl.BlockSpec((TM2, D), dec_map),
            pl.BlockSpec((MOD_ROWS, N_MOD * D), const),
            pl.BlockSpec((TM2, D), tile),
            pl.BlockSpec((TM2, D), tile),
            pl.BlockSpec((D, 2 * D), const),
            pl.BlockSpec((D, D), const),
            pl.BlockSpec((D, D), const),
            pl.BlockSpec((D, D), const),
            pl.BlockSpec((1, D), const),
            pl.BlockSpec((1, D), const),
        ],
        out_specs=pl.BlockSpec((TM2, D), tile),
        out_shape=jax.ShapeDtypeStruct((n_tok, D), F32),
        compiler_params=pltpu.CompilerParams(
            dimension_semantics=("arbitrary",), vmem_limit_bytes=VMEM_LIMIT),
        name="mixer_out",
    )(x_ctx, x_dec, mod_l, a_pre, b_pre, wg, wa, wb, wo, ln_g, ln_b)


def _ffn_prologue(n_ctx_tok, x_ref, mod_ref):
    m = mod_ref[pl.ds(_mod_row(pl.program_id(0) * TM3, n_ctx_tok), 1), :]
    sh2 = m[:, 3 * D:4 * D]
    sc2 = m[:, 4 * D:5 * D]
    return _ln_plain(x_ref[...], 1e-6) * (1.0 + sc2) + sh2


def _ffn_epilogue(n_ctx_tok, x_ref, mod_ref, acc, g_ref, b_ref):
    m = mod_ref[pl.ds(_mod_row(pl.program_id(0) * TM3, n_ctx_tok), 1), :]
    g2 = m[:, 5 * D:6 * D]
    y = DN_ALPHA * x_ref[...] + g2 * acc
    return _ln_plain(y, 1e-5) * g_ref[...] + b_ref[...]


def _swiglu_chunk(u, w13, w2):
    h = jnp.dot(u, w13, preferred_element_type=F32)
    h1 = h[:, 0:FC]
    hid = (h1 * _sigmoid(h1) * h[:, FC:2 * FC]).astype(BF16)
    return jnp.dot(hid, w2, preferred_element_type=F32)


def _dense_ffn_kernel(n_ctx_tok, x_ref, mod_ref, w13_ref, w2_ref, g_ref, b_ref, o_ref, u_scr, acc_scr):
    f = pl.program_id(1)

    @pl.when(f == 0)
    def _():
        u_scr[...] = _ffn_prologue(n_ctx_tok, x_ref, mod_ref).astype(BF16)

    p = _swiglu_chunk(u_scr[...], w13_ref[0], w2_ref[...])

    @pl.when(f == 0)
    def _():
        acc_scr[...] = p

    @pl.when(f > 0)
    def _():
        acc_scr[...] += p

    @pl.when(f == N_FC - 1)
    def _():
        o_ref[...] = _ffn_epilogue(n_ctx_tok, x_ref, mod_ref, acc_scr[...], g_ref, b_ref)


def _dense_ffn(x, mod_l, w13, w2, ln_g, ln_b, n_ctx_tok):
    n_tok = x.shape[0]
    return pl.pallas_call(
        functools.partial(_dense_ffn_kernel, n_ctx_tok),
        grid=(n_tok // TM3, N_FC),
        in_specs=[
            pl.BlockSpec((TM3, D), lambda i, f: (i, 0)),
            pl.BlockSpec((MOD_ROWS, N_MOD * D), lambda i, f: (0, 0)),
            pl.BlockSpec((1, D, 2 * FC), lambda i, f: (f, 0, 0)),
            pl.BlockSpec((FC, D), lambda i, f: (f, 0)),
            pl.BlockSpec((1, D), lambda i, f: (0, 0)),
            pl.BlockSpec((1, D), lambda i, f: (0, 0)),
        ],
        out_specs=pl.BlockSpec((TM3, D), lambda i, f: (i, 0)),
        out_shape=jax.ShapeDtypeStruct((n_tok, D), F32),
        scratch_shapes=[pltpu.VMEM((TM3, D), BF16), pltpu.VMEM((TM3, D), F32)],
        compiler_params=pltpu.CompilerParams(
            dimension_semantics=("arbitrary", "arbitrary"), vmem_limit_bytes=VMEM_LIMIT),
        name="dense_ffn",
    )(x, mod_l, w13, w2, ln_g, ln_b)


N_LT = D // V7X_LANES


def _to_token_major(ref, rows):
    n = rows.shape[0]
    for c in range(N_LT):
        ref[pl.ds(c, n, stride=N_LT), :] = rows[:, c * V7X_LANES:(c + 1) * V7X_LANES]


def _from_token_major(ref, n):
    return jnp.concatenate([ref[pl.ds(c, n, stride=N_LT), :] for c in range(N_LT)], axis=1)


def _rank_kernel(n_ctx_tok, x_ref, mod_ref, wrt_ref, brt_ref, info_ref, cnt_ref, utm_ref):
    i = pl.program_id(0)
    u2 = _ffn_prologue(n_ctx_tok, x_ref, mod_ref)
    _to_token_major(utm_ref, u2)
    lg = lax.dot_general(wrt_ref[...], u2, (((1,), (1,)), ((), ())), preferred_element_type=F32,
                         precision=lax.Precision.HIGHEST) + brt_ref[:, 0:1]
    eidx = lax.broadcasted_iota(jnp.int32, lg.shape, 0).astype(F32)
    neg = jnp.float32(-jnp.inf)
    v1 = jnp.max(lg, axis=0, keepdims=True)
    i1 = jnp.min(jnp.where(lg == v1, eidx, float(N_EXP)), axis=0, keepdims=True)
    lg2 = jnp.where(eidx == i1, neg, lg)
    v2 = jnp.max(lg2, axis=0, keepdims=True)
    i2 = jnp.min(jnp.where(lg2 == v2, eidx, float(N_EXP)), axis=0, keepdims=True)
    t = jnp.exp(v2 - v1)
    w_top = 1.0 / (1.0 + t)
    m1 = eidx == i1
    m2 = eidx == i2
    member = jnp.where(jnp.logical_or(m1, m2), 1.0, 0.0)
    before = (lax.broadcasted_iota(jnp.int32, (TS, TS), 0)
              < lax.broadcasted_iota(jnp.int32, (TS, TS), 1))
    rank = jnp.dot(member.astype(BF16), jnp.where(before, 1.0, 0.0).astype(BF16),
                   preferred_element_type=F32)
    for e in range(N_EXP):
        cnt_ref[i, e] = jnp.sum(member[e:e + 1, :]).astype(jnp.int32)
    rank1 = jnp.sum(jnp.where(m1, rank, 0.0), axis=0, keepdims=True)
    rank2 = jnp.sum(jnp.where(m2, rank, 0.0), axis=0, keepdims=True)
    info_ref[0] = jnp.concatenate(
        [i1, i2, rank1, rank2, w_top, t * w_top, jnp.zeros((V7X_SUBLANES - 6, TS), F32)], axis=0)


def _rank(x, mod_l, wrt, brt, n_ctx_tok):
    n_tiles = x.shape[0] // TS
    return pl.pallas_call(
        functools.partial(_rank_kernel, n_ctx_tok),
        grid=(n_tiles,),
        in_specs=[
            pl.BlockSpec((TS, D), lambda i: (i, 0)),
            pl.BlockSpec((MOD_ROWS, N_MOD * D), lambda i: (0, 0)),
            pl.BlockSpec((N_EXP, D), lambda i: (0, 0)),
            pl.BlockSpec((N_EXP, V7X_LANES), lambda i: (0, 0)),
        ],
        out_specs=[
            pl.BlockSpec((1, V7X_SUBLANES, TS), lambda i: (i, 0, 0)),
            pl.BlockSpec(memory_space=pltpu.SMEM),
            pl.BlockSpec((TS * N_LT, V7X_LANES), lambda i: (i, 0)),
        ],
        out_shape=[
            jax.ShapeDtypeStruct((n_tiles, V7X_SUBLANES, TS), F32),
            jax.ShapeDtypeStruct((n_tiles, N_EXP), jnp.int32),
            jax.ShapeDtypeStruct((n_tiles * TS * N_LT, V7X_LANES), F32),
        ],
        compiler_params=pltpu.CompilerParams(
            dimension_semantics=("arbitrary",), vmem_limit_bytes=VMEM_LIMIT),
        name="moe_rank",
    )(x, mod_l, wrt, brt)


def _dispatch_kernel(off_ref, fill_ref, rinfo_ref, utm_ref, xg_ref, info_ref,
                     pos_v, pos_s, zero_scr, sem, psem, zsem):
    i = pl.program_id(0)
    n_steps = pl.num_programs(0)
    rinfo = rinfo_ref[0]
    i1 = rinfo[0:1]
    i2 = rinfo[1:2]
    off1 = jnp.zeros_like(i1)
    off2 = jnp.zeros_like(i2)
    for e in range(N_EXP):
        start = off_ref[i * N_EXP + e].astype(F32)
        off1 = jnp.where(i1 == float(e), start, off1)
        off2 = jnp.where(i2 == float(e), start, off2)
    info = jnp.concatenate(
        [off1 + rinfo[2:3], off2 + rinfo[3:4], rinfo[4:6], jnp.zeros((V7X_SUBLANES - 4, TS), F32)],
        axis=0)
    info_ref[0] = info
    pos_v[...] = info.astype(jnp.int32)
    cp = pltpu.make_async_copy(pos_v, pos_s, psem)
    cp.start()
    cp.wait()

    def row_tile(ref, row):
        return ref.at[pl.ds(pl.multiple_of(row * N_LT, N_LT), N_LT)]

    def row_copy(tok, dst_row):
        return pltpu.make_async_copy(row_tile(utm_ref, tok), row_tile(xg_ref, dst_row), sem)

    def issue(tok, carry):
        row_copy(tok, pos_s[0, tok]).start(priority=0)
        row_copy(tok, pos_s[1, tok]).start(priority=1)
        return carry

    lax.fori_loop(0, TS, issue, 0, unroll=8)
    for _ in range(2):
        pltpu.make_async_copy(utm_ref, xg_ref.at[pl.ds(0, TS * N_LT)], sem).wait()

    @pl.when(i == n_steps - 1)
    def _():
        zero_scr[...] = jnp.zeros_like(zero_scr)
        for e in range(N_EXP):
            end = fill_ref[e]
            n_tail = fill_ref[N_EXP + e]

            def tail_copy(k):
                return pltpu.make_async_copy(row_tile(zero_scr, 0), row_tile(xg_ref, end + k), zsem)

            def tail_start(k, carry):
                tail_copy(k).start()
                return carry

            def tail_wait(k, carry):
                tail_copy(k).wait()
                return carry

            lax.fori_loop(0, n_tail, tail_start, 0)
            lax.fori_loop(0, n_tail, tail_wait, 0)

        def tile_copy(k):
            rows = TR * N_LT
            return pltpu.make_async_copy(zero_scr, xg_ref.at[pl.ds(pl.multiple_of(k * rows, rows), rows)], zsem)

        def tile_start(k, carry):
            tile_copy(k).start()
            return carry

        def tile_wait(k, carry):
            tile_copy(k).wait()
            return carry

        lax.fori_loop(fill_ref[2 * N_EXP], MAX_RT, tile_start, 0)
        lax.fori_loop(fill_ref[2 * N_EXP], MAX_RT, tile_wait, 0)


def _dispatch(utm, rinfo, off, fill):
    n_tiles = rinfo.shape[0]
    return pl.pallas_call(
        _dispatch_kernel,
        grid_spec=pltpu.PrefetchScalarGridSpec(
            num_scalar_prefetch=2,
            grid=(n_tiles,),
            in_specs=[
                pl.BlockSpec((1, V7X_SUBLANES, TS), lambda i, off, fill: (i, 0, 0)),
                pl.BlockSpec((TS * N_LT, V7X_LANES), lambda i, off, fill: (i, 0)),
            ],
            out_specs=[
                pl.BlockSpec(memory_space=pl.ANY),
                pl.BlockSpec((1, V7X_SUBLANES, TS), lambda i, off, fill: (i, 0, 0)),
            ],
            scratch_shapes=[
                pltpu.VMEM((V7X_SUBLANES, TS), jnp.int32),
                pltpu.SMEM((V7X_SUBLANES, TS), jnp.int32),
                pltpu.VMEM((TR * N_LT, V7X_LANES), F32),
                pltpu.SemaphoreType.DMA,
                pltpu.SemaphoreType.DMA,
                pltpu.SemaphoreType.DMA,
            ],
        ),
        out_shape=[
            jax.ShapeDtypeStruct((MAX_RT * TR * N_LT, V7X_LANES), F32),
            jax.ShapeDtypeStruct((n_tiles, V7X_SUBLANES, TS), F32),
        ],
        compiler_params=pltpu.CompilerParams(
            dimension_semantics=("arbitrary",), vmem_limit_bytes=VMEM_LIMIT),
        name="moe_dispatch",
    )(off, fill, rinfo, utm)


def _expert_kernel(second, exp_ref, new_ref, nact_ref, *refs):
    if second:
        xg_ref, w1_ref, w3_ref, w2_ref, yp_ref, o_ref, w1b, w3b, w2b = refs
    else:
        xg_ref, w1_ref, w3_ref, w2_ref, o_ref, w1b, w3b, w2b = refs
    r = pl.program_id(0)
    active = r < nact_ref[0]

    @pl.when(jnp.logical_and(active, new_ref[r] == 1))
    def _():
        w1b[...] = w1_ref[0].astype(BF16)
        w3b[...] = w3_ref[0].astype(BF16)
        w2b[...] = w2_ref[0].astype(BF16)

    @pl.when(active)
    def _():
        xb = _from_token_major(xg_ref, TR).astype(BF16)
        h1 = jnp.dot(xb, w1b[...], preferred_element_type=F32)
        h3 = jnp.dot(xb, w3b[...], preferred_element_type=F32)
        hid = (h1 * _sigmoid(h1) * h3).astype(BF16)
        p = jnp.dot(hid, w2b[...], preferred_element_type=F32)
        if second:
            _to_token_major(o_ref, yp_ref[...] + p)
        else:
            o_ref[...] = p

    @pl.when(jnp.logical_not(active))
    def _():
        o_ref[...] = jnp.zeros_like(o_ref)


def _expert_pass(second, tables, xg, w1, w3, w2, yp=None):
    f = 1 if second else 0
    row = lambda r, ex, nw, na: (r, 0)
    tm_block = pl.BlockSpec((TR * N_LT, V7X_LANES), row)
    tm_shape = jax.ShapeDtypeStruct((MAX_RT * TR * N_LT, V7X_LANES), F32)
    in_specs = [
        tm_block,
        pl.BlockSpec((1, D, FC), lambda r, ex, nw, na: (ex[r], 0, f)),
        pl.BlockSpec((1, D, FC), lambda r, ex, nw, na: (ex[r], 0, f)),
        pl.BlockSpec((1, FC, D), lambda r, ex, nw, na: (ex[r], f, 0)),
    ]
    args = [xg, w1, w3, w2]
    if second:
        in_specs.append(pl.BlockSpec((TR, D), row))
        args.append(yp)
    return pl.pallas_call(
        functools.partial(_expert_kernel, second),
        grid_spec=pltpu.PrefetchScalarGridSpec(
            num_scalar_prefetch=3,
            grid=(MAX_RT,),
            in_specs=in_specs,
            out_specs=tm_block if second else pl.BlockSpec((TR, D), row),
            scratch_shapes=[pltpu.VMEM((D, FC), BF16), pltpu.VMEM((D, FC), BF16),
                            pltpu.VMEM((FC, D), BF16)],
        ),
        out_shape=tm_shape if second else jax.ShapeDtypeStruct((MAX_RT * TR, D), F32),
        compiler_params=pltpu.CompilerParams(
            dimension_semantics=("arbitrary",), vmem_limit_bytes=EXPERT_VMEM_LIMIT),
        name="moe_expert_hi" if second else "moe_expert_lo",
    )(*tables, *args)


def _combine_kernel(n_ctx_tok, x_ref, mod_ref, info_ref, info_next_ref, yg_ref, g_ref, b_ref, oc_ref, od_ref,
                    ya_scr, yb_scr, pos_v, pos_s, sem, psem):
    i = pl.program_id(0)
    n_steps = pl.num_programs(0)
    slot = i % 2

    def gather_tile(info, s):
        pos_v[...] = info.astype(jnp.int32)
        cp = pltpu.make_async_copy(pos_v, pos_s, psem)
        cp.start()
        cp.wait()

        def row_tile(ref, row):
            return ref.at[pl.ds(pl.multiple_of(row * N_LT, N_LT), N_LT)]

        def issue(tok, carry):
            pltpu.make_async_copy(row_tile(yg_ref, pos_s[0, tok]), row_tile(ya_scr.at[s], tok),
                                  sem.at[s]).start(priority=0)
            pltpu.make_async_copy(row_tile(yg_ref, pos_s[1, tok]), row_tile(yb_scr.at[s], tok),
                                  sem.at[s]).start(priority=1)
            return carry

        lax.fori_loop(0, TS, issue, 0, unroll=8)

    @pl.when(i == 0)
    def _():
        gather_tile(info_ref[0], 0)

    @pl.when(i + 1 < n_steps)
    def _():
        gather_tile(info_next_ref[0], 1 - slot)

    padded = jnp.concatenate([info_ref[0], jnp.zeros((V7X_LANES - V7X_SUBLANES, TS), F32)], axis=0)
    cols = jnp.transpose(padded, (1, 0))
    w1c = cols[:, 2:3]
    w2c = cols[:, 3:4]
    pltpu.make_async_copy(yg_ref.at[pl.ds(0, TS * N_LT)], ya_scr.at[slot], sem.at[slot]).wait()
    pltpu.make_async_copy(yg_ref.at[pl.ds(0, TS * N_LT)], yb_scr.at[slot], sem.at[slot]).wait()
    acc = w1c * _from_token_major(ya_scr.at[slot], TS) + w2c * _from_token_major(yb_scr.at[slot], TS)
    res = _ffn_epilogue(n_ctx_tok, x_ref, mod_ref, acc, g_ref, b_ref)
    is_dec = i * TS >= n_ctx_tok

    @pl.when(jnp.logical_not(is_dec))
    def _():
        oc_ref[...] = res

    @pl.when(is_dec)
    def _():
        od_ref[...] = res


def _combine(x, mod_l, info, yg, ln_g, ln_b, n_ctx_tok):
    n_tok = x.shape[0]
    n_tiles = n_tok // TS
    ncb = n_ctx_tok // TS
    return pl.pallas_call(
        functools.partial(_combine_kernel, n_ctx_tok),
        grid=(n_tiles,),
        in_specs=[
            pl.BlockSpec((TS, D), lambda i: (i, 0)),
            pl.BlockSpec((MOD_ROWS, N_MOD * D), lambda i: (0, 0)),
            pl.BlockSpec((1, V7X_SUBLANES, TS), lambda i: (i, 0, 0)),
            pl.BlockSpec((1, V7X_SUBLANES, TS), lambda i: (jnp.minimum(i + 1, n_tiles - 1), 0, 0)),
            pl.BlockSpec(memory_space=pl.ANY),
            pl.BlockSpec((1, D), lambda i: (0, 0)),
            pl.BlockSpec((1, D), lambda i: (0, 0)),
        ],
        out_specs=[
            pl.BlockSpec((TS, D), lambda i: (jnp.minimum(i, ncb - 1), 0)),
            pl.BlockSpec((TS, D), lambda i: (jnp.maximum(i - ncb, 0), 0)),
        ],
        out_shape=[
            jax.ShapeDtypeStruct((n_ctx_tok, D), F32),
            jax.ShapeDtypeStruct((n_tok - n_ctx_tok, D), F32),
        ],
        scratch_shapes=[
            pltpu.VMEM((2, TS * N_LT, V7X_LANES), F32),
            pltpu.VMEM((2, TS * N_LT, V7X_LANES), F32),
            pltpu.VMEM((V7X_SUBLANES, TS), jnp.int32),
            pltpu.SMEM((V7X_SUBLANES, TS), jnp.int32),
            pltpu.SemaphoreType.DMA((2,)),
            pltpu.SemaphoreType.DMA,
        ],
        compiler_params=pltpu.CompilerParams(
            dimension_semantics=("arbitrary",), vmem_limit_bytes=VMEM_LIMIT),
        name="moe_combine",
    )(x, mod_l, info, info, yg, ln_g, ln_b)


def _routing_tables(counts):
    totals = jnp.sum(counts, axis=0)
    n_rt = (totals + TR - 1) // TR
    cum = jnp.cumsum(n_rt)
    first = cum - n_rt
    n_act = cum[-1]
    off = first[None, :] * TR + (jnp.cumsum(counts, axis=0) - counts)
    fill = jnp.concatenate([first * TR + totals, n_rt * TR - totals, n_act[None]])
    r = jnp.arange(MAX_RT, dtype=jnp.int32)
    rc = jnp.minimum(r, n_act - 1)
    exp = jnp.sum((rc[:, None] >= cum[None, :]).astype(jnp.int32), axis=1)
    new = jnp.logical_and(r == first[exp], r < n_act)
    i32 = lambda a: a.astype(jnp.int32)
    return i32(off.reshape(-1)), i32(fill), (i32(exp), i32(new), i32(n_act.reshape(1)))


def _moe_ffn(x, mod_l, router_w, router_b, w1, w3, w2, ln_g, ln_b, n_ctx_tok):
    n_tok = x.shape[0]
    assert 2 * n_tok // TR + N_EXP == MAX_RT and N_FC == 2
    wrt = router_w.T
    brt = jnp.broadcast_to(router_b.reshape(N_EXP, 1), (N_EXP, V7X_LANES))
    rinfo, counts, utm = _rank(x, mod_l, wrt, brt, n_ctx_tok)
    off, fill, tables = _routing_tables(counts)
    xg, info = _dispatch(utm, rinfo, off, fill)
    y_lo = _expert_pass(False, tables, xg, w1, w3, w2)
    y = _expert_pass(True, tables, xg, w1, w3, w2, y_lo)
    return _combine(x, mod_l, info, y, ln_g, ln_b, n_ctx_tok)


def _pack_w13(w1, w3):
    lead = w1.shape[:-2]
    nl = len(lead)
    perm = tuple(range(nl)) + (nl + 1, nl, nl + 2)
    a = w1.reshape(lead + (D, N_FC, FC)).transpose(perm)
    b = w3.reshape(lead + (D, N_FC, FC)).transpose(perm)
    return jnp.concatenate([a, b], axis=-1).astype(BF16)


def _block_diag_chunks(w):
    tiled = jnp.tile(w.reshape(N_CH, CW, HEAD_D), (1, 1, CW // HEAD_D))
    blk = jnp.arange(CW) // HEAD_D
    return jnp.where(blk[:, None] == blk[None, :], tiled, 0.0)


def kernel(x_prompt, x_sample, state_rglru, c, c_ctx, w_mod, b_mod, w_in, conv_a, w_a_out, conv_b, conv_b_bias, w_gate_a, b_gate_a, w_gate_x, b_gate_x, lru_lambda, w_b_out, w_o, ln1_g, ln1_b, ln2_g, ln2_b, ffn_w1, ffn_w3, ffn_w2, router_w, router_b, moe_w1, moe_w3, moe_w2):
    batch, seq, d = x_prompt.shape
    dec_batch, dec_seq, _ = x_sample.shape
    depth = w_mod.shape[0]
    assert (d, seq, dec_seq, depth) == (D, SEQ, DEC_SEQ, DEPTH)
    n_ctx_tok = batch * seq
    n_dec_tok = dec_batch * dec_seq
    assert n_ctx_tok % TM1 == 0 and TM1 == dec_seq and 1 + dec_batch <= MOD_ROWS
    n_ctx_tiles = n_ctx_tok // TM1
    n_tiles = n_ctx_tiles + dec_batch
    seq_per_tile = TM1 // seq
    assert N_SEG == 2 * seq_per_tile

    assert depth % 2 == 0
    n_tok = n_ctx_tok + n_dec_tok
    x = (x_prompt.reshape(n_ctx_tok, D), x_sample.reshape(n_dec_tok, D))

    cond = jnp.zeros((MOD_ROWS, D), F32).at[0].set(c_ctx).at[1:1 + dec_batch].set(c)
    mod = _modulation(cond, w_mod, b_mod)

    states = []
    for l in range(depth):
        w_in_l = w_in[l].astype(BF16)
        w5 = jnp.stack([
            jnp.concatenate([w_in_l[:, g * D + j * CW:g * D + (j + 1) * CW] for g in range(5)], axis=1)
            for j in range(N_CH)])
        wg = w_in_l[:, 5 * D:7 * D]
        wbd = jnp.concatenate(
            [_block_diag_chunks(w_gate_a[l, 0]), _block_diag_chunks(w_gate_x[l, 0]),
             _block_diag_chunks(w_gate_a[l, 1]), _block_diag_chunks(w_gate_x[l, 1])],
            axis=-1).astype(BF16)
        gbias = jnp.stack([b_gate_a[l, 0], b_gate_x[l, 0], b_gate_a[l, 1], b_gate_x[l, 1]], axis=0)
        h0 = jnp.zeros((n_tiles, 2, N_SEG, D), F32)
        h0 = h0.at[n_ctx_tiles:, 0, 0].set(state_rglru[:, l, 0].astype(F32))
        h0 = h0.at[n_ctx_tiles:, 1, N_SEG - 1].set(state_rglru[:, l, 1].astype(F32))

        a_pre, b_pre, st = _mixer_part1(
            x, mod[l], w5, conv_a[l], conv_b[l], conv_b_bias[l].reshape(1, D), wbd, gbias,
            lru_lambda[l], h0, n_tok, n_ctx_tok)
        x = _mixer_part2(
            x, mod[l], a_pre, b_pre, wg, w_a_out[l].astype(BF16), w_b_out[l].astype(BF16),
            w_o[l].astype(BF16), ln1_g[l].reshape(1, D), ln1_b[l].reshape(1, D), n_ctx_tok)

        if l % 2 == 0:
            k = l // 2
            x = _dense_ffn(x, mod[l], _pack_w13(ffn_w1[k], ffn_w3[k]), ffn_w2[k].astype(BF16),
                           ln2_g[l].reshape(1, D), ln2_b[l].reshape(1, D), n_ctx_tok)
        else:
            k = l // 2
            x = _moe_ffn(x, mod[l], router_w[k], router_b[k], moe_w1[k], moe_w3[k], moe_w2[k],
                         ln2_g[l].reshape(1, D), ln2_b[l].reshape(1, D), n_ctx_tok)

        st_ctx = st[:n_ctx_tiles]
        fwd = st_ctx[:, 0, 1::2].reshape(batch, D)
        bwd = st_ctx[:, 1, 0::2].reshape(batch, D)
        states.append(jnp.stack([fwd, bwd], axis=1))

    y_prompt = x[0].reshape(batch, seq, D)
    y_sample = x[1].reshape(dec_batch, dec_seq, D)
    new_state = jnp.stack(states, axis=1).astype(x_prompt.dtype)
    return (y_prompt, y_sample, new_state)
```

```python
import functools

import jax
import jax.numpy as jnp
from jax import lax
from jax.experimental import pallas as pl
from jax.experimental.pallas import tpu as pltpu

F32 = jnp.float32
BF16 = jnp.bfloat16

D = 1024
SEQ = 256
DEC_SEQ = 1024
GRID_W = 64
N_HEAD = 16
HEAD_D = D // N_HEAD
RGLRU_C = 8.0
LOG2_E = 1.4426950408889634
D_FF = 2816
N_EXP = 8
N_MOD = 6
DEPTH = 2
DN_ALPHA = (2.0 * DEPTH) ** 0.25

V7X_SUBLANES = 8
V7X_LANES = 128
V7X_VMEM_BYTES = 64 * 1024 * 1024
VMEM_LIMIT = V7X_VMEM_BYTES - 12 * 1024 * 1024

TM1 = 1024
CW = 256
N_CH = D // CW
N_LC = CW // V7X_LANES
N_SEG = V7X_SUBLANES
SEG = TM1 // N_SEG
SEG_STRIDE = SEG + 8
CONV_GAP = V7X_SUBLANES
TM2 = 512
TM3 = 512
FC = 1408
N_FC = D_FF // FC
TS = 512
TR = 256
MAX_RT = 2 * 10240 // TR + N_EXP
EXPERT_VMEM_LIMIT = V7X_VMEM_BYTES - 6 * 1024 * 1024
MOD_ROWS = 8
MOD_BLK = 1536


def _sigmoid(x):
    return 0.5 * jnp.tanh(0.5 * x) + 0.5


def _ln_plain(x, eps):
    mu = jnp.mean(x, axis=-1, keepdims=True)
    xc = x - mu
    var = jnp.mean(xc * xc, axis=-1, keepdims=True)
    return xc * lax.rsqrt(var + eps)


def _token_source(xs, tile, n_ctx_tok):
    ncb = n_ctx_tok // tile
    if isinstance(xs, tuple):
        x_ctx, x_dec = xs
        dec_off = 0
    else:
        x_ctx = x_dec = xs
        dec_off = ncb
    ctx_map = lambda i: (jnp.minimum(i, ncb - 1), 0)
    dec_map = lambda i: (jnp.maximum(i - ncb, 0) + dec_off, 0)
    return x_ctx, x_dec, ctx_map, dec_map


def _mod_row(tok0, n_ctx_tok):
    dec = jnp.maximum(tok0 - n_ctx_tok, 0) // DEC_SEQ
    return jnp.where(tok0 >= n_ctx_tok, 1 + dec, 0)


def _mod_kernel(cond_ref, w_ref, b_ref, o_ref):
    cnd = cond_ref[...]
    s = cnd * _sigmoid(cnd)
    o_ref[0] = jnp.dot(s, w_ref[0], preferred_element_type=F32,
                       precision=lax.Precision.HIGHEST) + b_ref[0]


def _modulation(cond, w_mod, b_mod):
    depth = w_mod.shape[0]
    n_out = w_mod.shape[2]
    return pl.pallas_call(
        _mod_kernel,
        grid=(depth, n_out // MOD_BLK),
        in_specs=[
            pl.BlockSpec((MOD_ROWS, D), lambda l, j: (0, 0)),
            pl.BlockSpec((1, D, MOD_BLK), lambda l, j: (l, 0, j)),
            pl.BlockSpec((1, 1, MOD_BLK), lambda l, j: (l, 0, j)),
        ],
        out_specs=pl.BlockSpec((1, MOD_ROWS, MOD_BLK), lambda l, j: (l, 0, j)),
        out_shape=jax.ShapeDtypeStruct((depth, MOD_ROWS, n_out), F32),
        compiler_params=pltpu.CompilerParams(
            dimension_semantics=("arbitrary", "arbitrary"), vmem_limit_bytes=VMEM_LIMIT),
        name="modulation",
    )(cond, w_mod, b_mod.reshape(depth, 1, n_out))


def _scan_dir(a_scr, b_scr, hl_scr, ac_scr, h0, keep, reverse):
    n_lc = a_scr.shape[0]

    def body(k, carry):
        kk = SEG - 1 - k if reverse else k
        idx = pl.ds(kk, N_SEG, stride=SEG_STRIDE)
        new = []
        for c in range(n_lc):
            h, acc = carry[c]
            a_k = a_scr[c, idx, :]
            h = a_k * h + b_scr[c, idx, :]
            acc = a_k * acc
            hl_scr[c, idx, :] = h
            ac_scr[c, idx, :] = acc
            new.append((h, acc))
        return tuple(new)

    init = tuple((jnp.zeros((N_SEG, V7X_LANES), F32), jnp.ones((N_SEG, V7X_LANES), F32))
                 for _ in range(n_lc))
    fin = lax.fori_loop(0, SEG, body, init, unroll=8)
    h_loc = jnp.concatenate([fin[c][0] for c in range(n_lc)], axis=1)
    a_tot = jnp.concatenate([fin[c][1] for c in range(n_lc)], axis=1)

    order = range(N_SEG - 1, -1, -1) if reverse else range(N_SEG)
    h_in = [None] * N_SEG
    prev = None
    for s in order:
        cur = h0[s:s + 1]
        if prev is not None:
            left = a_tot[prev:prev + 1] * h_in[prev] + h_loc[prev:prev + 1]
            cur = keep[s] * left + cur
        h_in[s] = cur
        prev = s
    h_in = jnp.concatenate(h_in, axis=0)
    return h_in, a_tot * h_in + h_loc


def _conv_stage(is_dec, proj_scr, ca_ref, cb_ref, cbias_ref, apre_ref, xr_scr, gap_scr):
    win = GRID_W
    n_win = TM1 // win
    stride = win + CONV_GAP
    ca = ca_ref[...]
    cb = cb_ref[...]
    bias = cbias_ref[...]
    pieces = [(w, c) for w in range(n_win) for c in range(N_LC)]
    sub = lax.broadcasted_iota(jnp.int32, (CONV_GAP, V7X_LANES), 0)
    joined = jnp.where(is_dec, jnp.float32(0.0), jnp.float32(1.0))
    zero_gap = jnp.zeros((CONV_GAP, V7X_LANES), F32)

    def stage(value_of):
        for c in range(N_LC):
            for w in range(n_win + 1):
                edge = w * win
                if edge % SEQ == 0:
                    gap = zero_gap
                else:
                    before = value_of(slice(edge - CONV_GAP, edge), c)
                    after = value_of(slice(edge, edge + CONV_GAP), c)
                    gap = joined * jnp.where(sub >= CONV_GAP - 2, before, jnp.where(sub == 0, after, 0.0))
                gap_scr[c, w * stride:w * stride + CONV_GAP, :] = gap
        for w, c in pieces:
            lo = CONV_GAP + w * stride
            gap_scr[c, lo:lo + win, :] = value_of(slice(w * win, (w + 1) * win), c)

    def tap(w, c, shift):
        lo = CONV_GAP + w * stride + shift
        return gap_scr[c, lo:lo + win, :]

    def chunk_cols(k, c):
        return slice(k * CW + c * V7X_LANES, k * CW + (c + 1) * V7X_LANES)

    stage(lambda rows, c: proj_scr[rows, chunk_cols(2, c)] * proj_scr[rows, chunk_cols(0, c)])
    for w, c in pieces:
        rows = slice(w * win, (w + 1) * win)
        lanes = slice(c * V7X_LANES, (c + 1) * V7X_LANES)
        conv = ca[0:1, lanes] * tap(w, c, -1) + ca[1:2, lanes] * tap(w, c, 0) + ca[2:3, lanes] * tap(w, c, 1)
        apre_ref[rows, lanes] = (proj_scr[rows, chunk_cols(1, c)] * conv).astype(BF16)
    stage(lambda rows, c: proj_scr[rows, chunk_cols(4, c)])
    for w, c in pieces:
        rows = slice(w * win, (w + 1) * win)
        lanes = slice(c * V7X_LANES, (c + 1) * V7X_LANES)
        xr_scr[rows, lanes] = (cb[0:1, lanes] * tap(w, c, -2) + cb[1:2, lanes] * tap(w, c, -1)
                               + cb[2:3, lanes] * tap(w, c, 0) + cb[3:4, lanes] * tap(w, c, 1)
                               + bias[:, lanes])


def _mix1_kernel(n_ctx_tiles, xc_ref, xd_ref, mod_ref, w5_ref, ca_ref, cb_ref, cbias_ref, wbd_ref, gb_ref,
                 lam_ref, h0_ref, apre_ref, bpre_ref, st_ref,
                 u_scr, proj_scr, a_scr, b_scr, hl_scr, ac_scr, hsum_scr, xr_scr, gap_scr):
    i = pl.program_id(0)
    j = pl.program_id(1)
    is_dec = i >= n_ctx_tiles

    @pl.when(j == 0)
    def _():
        row = jnp.where(is_dec, i - (n_ctx_tiles - 1), 0)
        m = mod_ref[pl.ds(row, 1), :]
        sh1 = m[:, 0:D]
        sc1 = m[:, D:2 * D]
        x = jnp.where(is_dec, xd_ref[...], xc_ref[...])
        u_scr[...] = (_ln_plain(x, 1e-6) * (1.0 + sc1) + sh1).astype(BF16)

    proj_scr[...] = jnp.dot(u_scr[...], w5_ref[0, 0], preferred_element_type=F32)

    _conv_stage(is_dec, proj_scr, ca_ref, cb_ref, cbias_ref, apre_ref, xr_scr, gap_scr)

    xr = xr_scr[...]
    gates = jnp.dot(xr.astype(BF16), wbd_ref[0], preferred_element_type=F32)
    gb = gb_ref[...]
    lam = lam_ref[...]
    sp = jnp.maximum(-lam, 0.0) + jnp.log1p(jnp.exp(-jnp.abs(lam)))
    rate = (-RGLRU_C * LOG2_E) * sp

    one = jnp.float32(1.0)
    for d in range(2):
        ga = gates[:, (2 * d) * CW:(2 * d + 1) * CW] + gb[2 * d:2 * d + 1]
        gx = gates[:, (2 * d + 1) * CW:(2 * d + 2) * CW] + gb[2 * d + 1:2 * d + 2]
        r = _sigmoid(ga)
        ig = _sigmoid(gx)
        a = jnp.exp2(r * rate[d:d + 1])
        y = 1.0 - a * a
        bt = jnp.where(y > 0.0, y * lax.rsqrt(y), 0.0) * (ig * xr)
        for s in range(N_SEG):
            lo = s * SEG_STRIDE
            for c in range(N_LC):
                lanes = slice(c * V7X_LANES, (c + 1) * V7X_LANES)
                a_scr[c, lo:lo + SEG, :] = a[s * SEG:(s + 1) * SEG, lanes]
                b_scr[c, lo:lo + SEG, :] = bt[s * SEG:(s + 1) * SEG, lanes]
        if d == 0:
            keep = [jnp.where(is_dec, one, jnp.float32(s % 2 == 1)) for s in range(N_SEG)]
        else:
            keep = [jnp.where(is_dec, one, jnp.float32(s % 2 == 0)) for s in range(N_SEG)]
        h_in, h_out = _scan_dir(a_scr, b_scr, hl_scr, ac_scr, h0_ref[0, d], keep, reverse=(d == 1))
        st_ref[0, d] = h_out
        for s in range(N_SEG):
            lo = s * SEG_STRIDE
            for c in range(N_LC):
                lanes = slice(c * V7X_LANES, (c + 1) * V7X_LANES)
                h_seg = hl_scr[c, lo:lo + SEG, :] + ac_scr[c, lo:lo + SEG, :] * h_in[s:s + 1, lanes]
                if d == 0:
                    hsum_scr[s * SEG:(s + 1) * SEG, lanes] = h_seg
                else:
                    hsum_scr[s * SEG:(s + 1) * SEG, lanes] += h_seg

    bpre_ref[...] = (hsum_scr[...] * jax.nn.gelu(proj_scr[:, 3 * CW:4 * CW])).astype(BF16)


def _mixer_part1(xs, mod_l, layer, w5, conv_a, conv_b, conv_b_bias, wbd, gbias, lam, h0, n_tok, n_ctx_tok):
    n_tiles = n_tok // TM1
    x_ctx, x_dec, ctx_map, dec_map = _token_source(xs, TM1, n_ctx_tok)
    kern = functools.partial(_mix1_kernel, n_ctx_tok // TM1)
    return pl.pallas_call(
        kern,
        grid=(n_tiles, N_CH),
        in_specs=[
            pl.BlockSpec((TM1, D), lambda i, j: ctx_map(i)),
            pl.BlockSpec((TM1, D), lambda i, j: dec_map(i)),
            pl.BlockSpec((MOD_ROWS, N_MOD * D), lambda i, j: (0, 0)),
            pl.BlockSpec((1, 1, D, 5 * CW), lambda i, j: (layer, j, 0, 0)),
            pl.BlockSpec((3, CW), lambda i, j: (0, j)),
            pl.BlockSpec((4, CW), lambda i, j: (0, j)),
            pl.BlockSpec((1, CW), lambda i, j: (0, j)),
            pl.BlockSpec((1, CW, 4 * CW), lambda i, j: (j, 0, 0)),
            pl.BlockSpec((4, CW), lambda i, j: (0, j)),
            pl.BlockSpec((2, CW), lambda i, j: (0, j)),
            pl.BlockSpec((1, 2, N_SEG, CW), lambda i, j: (i, 0, 0, j)),
        ],
        out_specs=[
            pl.BlockSpec((TM1, CW), lambda i, j: (i, j)),
            pl.BlockSpec((TM1, CW), lambda i, j: (i, j)),
            pl.BlockSpec((1, 2, N_SEG, CW), lambda i, j: (i, 0, 0, j)),
        ],
        out_shape=[
            jax.ShapeDtypeStruct((n_tok, D), BF16),
            jax.ShapeDtypeStruct((n_tok, D), BF16),
            jax.ShapeDtypeStruct((n_tiles, 2, N_SEG, D), F32),
        ],
        scratch_shapes=[
            pltpu.VMEM((TM1, D), BF16),
            pltpu.VMEM((TM1, 5 * CW), F32),
            pltpu.VMEM((N_LC, N_SEG * SEG_STRIDE, V7X_LANES), F32),
            pltpu.VMEM((N_LC, N_SEG * SEG_STRIDE, V7X_LANES), F32),
            pltpu.VMEM((N_LC, N_SEG * SEG_STRIDE, V7X_LANES), F32),
            pltpu.VMEM((N_LC, N_SEG * SEG_STRIDE, V7X_LANES), F32),
            pltpu.VMEM((TM1, CW), F32),
            pltpu.VMEM((TM1, CW), F32),
            pltpu.VMEM((N_LC, CONV_GAP + (TM1 // GRID_W) * (GRID_W + CONV_GAP), V7X_LANES), F32),
        ],
        compiler_params=pltpu.CompilerParams(
            dimension_semantics=("arbitrary", "arbitrary"), vmem_limit_bytes=VMEM_LIMIT),
        name="mixer_scan",
    )(x_ctx, x_dec, mod_l, w5, conv_a, conv_b, conv_b_bias, wbd, gbias, lam, h0)


def _mix2_kernel(n_ctx_tok, xc_ref, xd_ref, mod_ref, ap_ref, bp_ref, wg_ref, wa_ref, wb_ref, wo_ref,
                 g_ref, b_ref, o_ref):
    i = pl.program_id(0)
    m = mod_ref[pl.ds(_mod_row(i * TM2, n_ctx_tok), 1), :]
    sh1 = m[:, 0:D]
    sc1 = m[:, D:2 * D]
    g1 = m[:, 2 * D:3 * D]
    x = jnp.where(i * TM2 >= n_ctx_tok, xd_ref[...], xc_ref[...])
    u = (_ln_plain(x, 1e-6) * (1.0 + sc1) + sh1).astype(BF16)
    gates = jnp.dot(u, wg_ref[0], preferred_element_type=F32)
    br_a = jnp.dot(ap_ref[...], wa_ref[...], preferred_element_type=F32)
    br_b = jnp.dot(bp_ref[...], wb_ref[...], preferred_element_type=F32)
    merged = _sigmoid(gates[:, 0:D]) * br_a + _sigmoid(gates[:, D:2 * D]) * br_b
    mix = jnp.dot(merged.astype(BF16), wo_ref[...], preferred_element_type=F32)
    y = DN_ALPHA * x + g1 * mix
    o_ref[...] = _ln_plain(y, 1e-5) * g_ref[...] + b_ref[...]


def _mixer_part2(xs, mod_l, layer, a_pre, b_pre, wg, wa, wb, wo, ln_g, ln_b, n_ctx_tok):
    n_tok = a_pre.shape[0]
    x_ctx, x_dec, ctx_map, dec_map = _token_source(xs, TM2, n_ctx_tok)
    const = lambda i: (0, 0)
    tile = lambda i: (i, 0)
    return pl.pallas_call(
        functools.partial(_mix2_kernel, n_ctx_tok),
        grid=(n_tok // TM2,),
        in_specs=[
            pl.BlockSpec((TM2, D), ctx_map),
            pl.BlockSpec((TM2, D), dec_map),
            pl.BlockSpec((MOD_ROWS, N_MOD * D), const),
            pl.BlockSpec((TM2, D), tile),
            pl.BlockSpec((TM2, D), tile),
            pl.BlockSpec((1, D, 2 * D), lambda i: (layer, 0, 0)),
            pl.BlockSpec((D, D), const),
            pl.BlockSpec((D, D), const),
            pl.BlockSpec((D, D), const),
            pl.BlockSpec((1, D), const),
            pl.BlockSpec((1, D), const),
        ],
        out_specs=pl.BlockSpec((TM2, D), tile),
        out_shape=jax.ShapeDtypeStruct((n_tok, D), F32),
        compiler_params=pltpu.CompilerParams(
            dimension_semantics=("arbitrary",), vmem_limit_bytes=VMEM_LIMIT),
        name="mixer_out",
    )(x_ctx, x_dec, mod_l, a_pre, b_pre, wg, wa, wb, wo, ln_g, ln_b)


def _ffn_prologue(n_ctx_tok, x_ref, mod_ref):
    m = mod_ref[pl.ds(_mod_row(pl.program_id(0) * TM3, n_ctx_tok), 1), :]
    sh2 = m[:, 3 * D:4 * D]
    sc2 = m[:, 4 * D:5 * D]
    return _ln_plain(x_ref[...], 1e-6) * (1.0 + sc2) + sh2


def _ffn_epilogue(n_ctx_tok, x_ref, mod_ref, acc, g_ref, b_ref):
    m = mod_ref[pl.ds(_mod_row(pl.program_id(0) * TM3, n_ctx_tok), 1), :]
    g2 = m[:, 5 * D:6 * D]
    y = DN_ALPHA * x_ref[...] + g2 * acc
    return _ln_plain(y, 1e-5) * g_ref[...] + b_ref[...]


def _swiglu_chunk(u, w13, w2):
    h = jnp.dot(u, w13, preferred_element_type=F32)
    h1 = h[:, 0:FC]
    hid = (h1 * _sigmoid(h1) * h[:, FC:2 * FC]).astype(BF16)
    return jnp.dot(hid, w2, preferred_element_type=F32)


def _dense_ffn_kernel(n_ctx_tok, x_ref, mod_ref, w13_ref, w2_ref, g_ref, b_ref, o_ref, u_scr, acc_scr):
    f = pl.program_id(1)

    @pl.when(f == 0)
    def _():
        u_scr[...] = _ffn_prologue(n_ctx_tok, x_ref, mod_ref).astype(BF16)

    p = _swiglu_chunk(u_scr[...], w13_ref[0], w2_ref[...])

    @pl.when(f == 0)
    def _():
        acc_scr[...] = p

    @pl.when(f > 0)
    def _():
        acc_scr[...] += p

    @pl.when(f == N_FC - 1)
    def _():
        o_ref[...] = _ffn_epilogue(n_ctx_tok, x_ref, mod_ref, acc_scr[...], g_ref, b_ref)


def _dense_ffn(x, mod_l, w13, w2, ln_g, ln_b, n_ctx_tok):
    n_tok = x.shape[0]
    return pl.pallas_call(
        functools.partial(_dense_ffn_kernel, n_ctx_tok),
        grid=(n_tok // TM3, N_FC),
        in_specs=[
            pl.BlockSpec((TM3, D), lambda i, f: (i, 0)),
            pl.BlockSpec((MOD_ROWS, N_MOD * D), lambda i, f: (0, 0)),
            pl.BlockSpec((1, D, 2 * FC), lambda i, f: (f, 0, 0)),
            pl.BlockSpec((FC, D), lambda i, f: (f, 0)),
            pl.BlockSpec((1, D), lambda i, f: (0, 0)),
            pl.BlockSpec((1, D), lambda i, f: (0, 0)),
        ],
        out_specs=pl.BlockSpec((TM3, D), lambda i, f: (i, 0)),
        out_shape=jax.ShapeDtypeStruct((n_tok, D), F32),
        scratch_shapes=[pltpu.VMEM((TM3, D), BF16), pltpu.VMEM((TM3, D), F32)],
        compiler_params=pltpu.CompilerParams(
            dimension_semantics=("arbitrary", "arbitrary"), vmem_limit_bytes=VMEM_LIMIT),
        name="dense_ffn",
    )(x, mod_l, w13, w2, ln_g, ln_b)


N_LT = D // V7X_LANES


def _to_token_major(ref, rows):
    n = rows.shape[0]
    for c in range(N_LT):
        ref[pl.ds(c, n, stride=N_LT), :] = rows[:, c * V7X_LANES:(c + 1) * V7X_LANES]


def _from_token_major(ref, n):
    return jnp.concatenate([ref[pl.ds(c, n, stride=N_LT), :] for c in range(N_LT)], axis=1)


def _rank_kernel(n_ctx_tok, x_ref, mod_ref, wrt_ref, brt_ref, info_ref, cnt_ref, utm_ref):
    i = pl.program_id(0)
    u2 = _ffn_prologue(n_ctx_tok, x_ref, mod_ref)
    _to_token_major(utm_ref, u2)
    lg = lax.dot_general(wrt_ref[...], u2, (((1,), (1,)), ((), ())), preferred_element_type=F32,
                         precision=lax.Precision.HIGHEST) + brt_ref[:, 0:1]
    eidx = lax.broadcasted_iota(jnp.int32, lg.shape, 0).astype(F32)
    neg = jnp.float32(-jnp.inf)
    v1 = jnp.max(lg, axis=0, keepdims=True)
    i1 = jnp.min(jnp.where(lg == v1, eidx, float(N_EXP)), axis=0, keepdims=True)
    lg2 = jnp.where(eidx == i1, neg, lg)
    v2 = jnp.max(lg2, axis=0, keepdims=True)
    i2 = jnp.min(jnp.where(lg2 == v2, eidx, float(N_EXP)), axis=0, keepdims=True)
    t = jnp.exp(v2 - v1)
    w_top = 1.0 / (1.0 + t)
    m1 = eidx == i1
    m2 = eidx == i2
    member = jnp.where(jnp.logical_or(m1, m2), 1.0, 0.0)
    before = (lax.broadcasted_iota(jnp.int32, (TS, TS), 0)
              < lax.broadcasted_iota(jnp.int32, (TS, TS), 1))
    rank = jnp.dot(member.astype(BF16), jnp.where(before, 1.0, 0.0).astype(BF16),
                   preferred_element_type=F32)
    for e in range(N_EXP):
        cnt_ref[i, e] = jnp.sum(member[e:e + 1, :]).astype(jnp.int32)
    rank1 = jnp.sum(jnp.where(m1, rank, 0.0), axis=0, keepdims=True)
    rank2 = jnp.sum(jnp.where(m2, rank, 0.0), axis=0, keepdims=True)
    info_ref[0] = jnp.concatenate(
        [i1, i2, rank1, rank2, w_top, t * w_top, jnp.zeros((V7X_SUBLANES - 6, TS), F32)], axis=0)


def _rank(x, mod_l, wrt, brt, n_ctx_tok):
    n_tiles = x.shape[0] // TS
    return pl.pallas_call(
        functools.partial(_rank_kernel, n_ctx_tok),
        grid=(n_tiles,),
        in_specs=[
            pl.BlockSpec((TS, D), lambda i: (i, 0)),
            pl.BlockSpec((MOD_ROWS, N_MOD * D), lambda i: (0, 0)),
            pl.BlockSpec((N_EXP, D), lambda i: (0, 0)),
            pl.BlockSpec((N_EXP, V7X_LANES), lambda i: (0, 0)),
        ],
        out_specs=[
            pl.BlockSpec((1, V7X_SUBLANES, TS), lambda i: (i, 0, 0)),
            pl.BlockSpec(memory_space=pltpu.SMEM),
            pl.BlockSpec((TS * N_LT, V7X_LANES), lambda i: (i, 0)),
        ],
        out_shape=[
            jax.ShapeDtypeStruct((n_tiles, V7X_SUBLANES, TS), F32),
            jax.ShapeDtypeStruct((n_tiles, N_EXP), jnp.int32),
            jax.ShapeDtypeStruct((n_tiles * TS * N_LT, V7X_LANES), F32),
        ],
        compiler_params=pltpu.CompilerParams(
            dimension_semantics=("arbitrary",), vmem_limit_bytes=VMEM_LIMIT),
        name="moe_rank",
    )(x, mod_l, wrt, brt)


def _dispatch_kernel(off_ref, fill_ref, rinfo_ref, utm_ref, xg_ref, info_ref,
                     pos_v, pos_s, zero_scr, sem, psem, zsem):
    i = pl.program_id(0)
    n_steps = pl.num_programs(0)
    rinfo = rinfo_ref[0]
    i1 = rinfo[0:1]
    i2 = rinfo[1:2]
    off1 = jnp.zeros_like(i1)
    off2 = jnp.zeros_like(i2)
    for e in range(N_EXP):
        start = off_ref[i * N_EXP + e].astype(F32)
        off1 = jnp.where(i1 == float(e), start, off1)
        off2 = jnp.where(i2 == float(e), start, off2)
    info = jnp.concatenate(
        [off1 + rinfo[2:3], off2 + rinfo[3:4], rinfo[4:6], jnp.zeros((V7X_SUBLANES - 4, TS), F32)],
        axis=0)
    info_ref[0] = info
    pos_v[...] = info.astype(jnp.int32)
    cp = pltpu.make_async_copy(pos_v, pos_s, psem)
    cp.start()
    cp.wait()

    def row_tile(ref, row):
        return ref.at[pl.ds(pl.multiple_of(row * N_LT, N_LT), N_LT)]

    def row_copy(tok, dst_row):
        return pltpu.make_async_copy(row_tile(utm_ref, tok), row_tile(xg_ref, dst_row), sem)

    def issue(tok, carry):
        row_copy(tok, pos_s[0, tok]).start(priority=0)
        row_copy(tok, pos_s[1, tok]).start(priority=1)
        return carry

    lax.fori_loop(0, TS, issue, 0, unroll=8)
    for _ in range(2):
        pltpu.make_async_copy(utm_ref, xg_ref.at[pl.ds(0, TS * N_LT)], sem).wait()

    @pl.when(i == n_steps - 1)
    def _():
        zero_scr[...] = jnp.zeros_like(zero_scr)
        for e in range(N_EXP):
            end = fill_ref[e]
            n_tail = fill_ref[N_EXP + e]

            def tail_copy(k):
                return pltpu.make_async_copy(row_tile(zero_scr, 0), row_tile(xg_ref, end + k), zsem)

            def tail_start(k, carry):
                tail_copy(k).start()
                return carry

            def tail_wait(k, carry):
                tail_copy(k).wait()
                return carry

            lax.fori_loop(0, n_tail, tail_start, 0)
            lax.fori_loop(0, n_tail, tail_wait, 0)

        def tile_copy(k):
            rows = TR * N_LT
            return pltpu.make_async_copy(zero_scr, xg_ref.at[pl.ds(pl.multiple_of(k * rows, rows), rows)], zsem)

        def tile_start(k, carry):
            tile_copy(k).start()
            return carry

        def tile_wait(k, carry):
            tile_copy(k).wait()
            return carry

        lax.fori_loop(fill_ref[2 * N_EXP], MAX_RT, tile_start, 0)
        lax.fori_loop(fill_ref[2 * N_EXP], MAX_RT, tile_wait, 0)


def _dispatch(utm, rinfo, off, fill):
    n_tiles = rinfo.shape[0]
    return pl.pallas_call(
        _dispatch_kernel,
        grid_spec=pltpu.PrefetchScalarGridSpec(
            num_scalar_prefetch=2,
            grid=(n_tiles,),
            in_specs=[
                pl.BlockSpec((1, V7X_SUBLANES, TS), lambda i, off, fill: (i, 0, 0)),
                pl.BlockSpec((TS * N_LT, V7X_LANES), lambda i, off, fill: (i, 0)),
            ],
            out_specs=[
                pl.BlockSpec(memory_space=pl.ANY),
                pl.BlockSpec((1, V7X_SUBLANES, TS), lambda i, off, fill: (i, 0, 0)),
            ],
            scratch_shapes=[
                pltpu.VMEM((V7X_SUBLANES, TS), jnp.int32),
                pltpu.SMEM((V7X_SUBLANES, TS), jnp.int32),
                pltpu.VMEM((TR * N_LT, V7X_LANES), F32),
                pltpu.SemaphoreType.DMA,
                pltpu.SemaphoreType.DMA,
                pltpu.SemaphoreType.DMA,
            ],
        ),
        out_shape=[
            jax.ShapeDtypeStruct((MAX_RT * TR * N_LT, V7X_LANES), F32),
            jax.ShapeDtypeStruct((n_tiles, V7X_SUBLANES, TS), F32),
        ],
        compiler_params=pltpu.CompilerParams(
            dimension_semantics=("arbitrary",), vmem_limit_bytes=VMEM_LIMIT),
        name="moe_dispatch",
    )(off, fill, rinfo, utm)


def _expert_kernel(second, exp_ref, new_ref, nact_ref, *refs):
    if second:
        xg_ref, w1_ref, w3_ref, w2_ref, yp_ref, o_ref, w1b, w3b, w2b = refs
    else:
        xg_ref, w1_ref, w3_ref, w2_ref, o_ref, w1b, w3b, w2b = refs
    r = pl.program_id(0)
    active = r < nact_ref[0]

    @pl.when(jnp.logical_and(active, new_ref[r] == 1))
    def _():
        w1b[...] = w1_ref[0].astype(BF16)
        w3b[...] = w3_ref[0].astype(BF16)
        w2b[...] = w2_ref[0].astype(BF16)

    @pl.when(active)
    def _():
        xb = _from_token_major(xg_ref, TR).astype(BF16)
        h1 = jnp.dot(xb, w1b[...], preferred_element_type=F32)
        h3 = jnp.dot(xb, w3b[...], preferred_element_type=F32)
        hid = (h1 * _sigmoid(h1) * h3).astype(BF16)
        p = jnp.dot(hid, w2b[...], preferred_element_type=F32)
        if second:
            _to_token_major(o_ref, yp_ref[...] + p)
        else:
            o_ref[...] = p

    @pl.when(jnp.logical_not(active))
    def _():
        o_ref[...] = jnp.zeros_like(o_ref)


def _expert_pass(second, tables, xg, w1, w3, w2, yp=None):
    f = 1 if second else 0
    row = lambda r, ex, nw, na: (r, 0)
    tm_block = pl.BlockSpec((TR * N_LT, V7X_LANES), row)
    tm_shape = jax.ShapeDtypeStruct((MAX_RT * TR * N_LT, V7X_LANES), F32)
    in_specs = [
        tm_block,
        pl.BlockSpec((1, D, FC), lambda r, ex, nw, na: (ex[r], 0, f)),
        pl.BlockSpec((1, D, FC), lambda r, ex, nw, na: (ex[r], 0, f)),
        pl.BlockSpec((1, FC, D), lambda r, ex, nw, na: (ex[r], f, 0)),
    ]
    args = [xg, w1, w3, w2]
    if second:
        in_specs.append(pl.BlockSpec((TR, D), row))
        args.append(yp)
    return pl.pallas_call(
        functools.partial(_expert_kernel, second),
        grid_spec=pltpu.PrefetchScalarGridSpec(
            num_scalar_prefetch=3,
            grid=(MAX_RT,),
            in_specs=in_specs,
            out_specs=tm_block if second else pl.BlockSpec((TR, D), row),
            scratch_shapes=[pltpu.VMEM((D, FC), BF16), pltpu.VMEM((D, FC), BF16),
                            pltpu.VMEM((FC, D), BF16)],
        ),
        out_shape=tm_shape if second else jax.ShapeDtypeStruct((MAX_RT * TR, D), F32),
        compiler_params=pltpu.CompilerParams(
            dimension_semantics=("arbitrary",), vmem_limit_bytes=EXPERT_VMEM_LIMIT),
        name="moe_expert_hi" if second else "moe_expert_lo",
    )(*tables, *args)


def _combine_kernel(n_ctx_tok, x_ref, mod_ref, info_ref, info_next_ref, yg_ref, g_ref, b_ref, oc_ref, od_ref,
                    ya_scr, yb_scr, pos_v, pos_s, sem, psem):
    i = pl.program_id(0)
    n_steps = pl.num_programs(0)
    slot = i % 2

    def gather_tile(info, s):
        pos_v[...] = info.astype(jnp.int32)
        cp = pltpu.make_async_copy(pos_v, pos_s, psem)
        cp.start()
        cp.wait()

        def row_tile(ref, row):
            return ref.at[pl.ds(pl.multiple_of(row * N_LT, N_LT), N_LT)]

        def issue(tok, carry):
            pltpu.make_async_copy(row_tile(yg_ref, pos_s[0, tok]), row_tile(ya_scr.at[s], tok),
                                  sem.at[s]).start(priority=0)
            pltpu.make_async_copy(row_tile(yg_ref, pos_s[1, tok]), row_tile(yb_scr.at[s], tok),
                                  sem.at[s]).start(priority=1)
            return carry

        lax.fori_loop(0, TS, issue, 0, unroll=8)

    @pl.when(i == 0)
    def _():
        gather_tile(info_ref[0], 0)

    @pl.when(i + 1 < n_steps)
    def _():
        gather_tile(info_next_ref[0], 1 - slot)

    padded = jnp.concatenate([info_ref[0], jnp.zeros((V7X_LANES - V7X_SUBLANES, TS), F32)], axis=0)
    cols = jnp.transpose(padded, (1, 0))
    w1c = cols[:, 2:3]
    w2c = cols[:, 3:4]
    pltpu.make_async_copy(yg_ref.at[pl.ds(0, TS * N_LT)], ya_scr.at[slot], sem.at[slot]).wait()
    pltpu.make_async_copy(yg_ref.at[pl.ds(0, TS * N_LT)], yb_scr.at[slot], sem.at[slot]).wait()
    acc = w1c * _from_token_major(ya_scr.at[slot], TS) + w2c * _from_token_major(yb_scr.at[slot], TS)
    res = _ffn_epilogue(n_ctx_tok, x_ref, mod_ref, acc, g_ref, b_ref)
    is_dec = i * TS >= n_ctx_tok

    @pl.when(jnp.logical_not(is_dec))
    def _():
        oc_ref[...] = res

    @pl.when(is_dec)
    def _():
        od_ref[...] = res


def _combine(x, mod_l, info, yg, ln_g, ln_b, n_ctx_tok):
    n_tok = x.shape[0]
    n_tiles = n_tok // TS
    ncb = n_ctx_tok // TS
    return pl.pallas_call(
        functools.partial(_combine_kernel, n_ctx_tok),
        grid=(n_tiles,),
        in_specs=[
            pl.BlockSpec((TS, D), lambda i: (i, 0)),
            pl.BlockSpec((MOD_ROWS, N_MOD * D), lambda i: (0, 0)),
            pl.BlockSpec((1, V7X_SUBLANES, TS), lambda i: (i, 0, 0)),
            pl.BlockSpec((1, V7X_SUBLANES, TS), lambda i: (jnp.minimum(i + 1, n_tiles - 1), 0, 0)),
            pl.BlockSpec(memory_space=pl.ANY),
            pl.BlockSpec((1, D), lambda i: (0, 0)),
            pl.BlockSpec((1, D), lambda i: (0, 0)),
        ],
        out_specs=[
            pl.BlockSpec((TS, D), lambda i: (jnp.minimum(i, ncb - 1), 0)),
            pl.BlockSpec((TS, D), lambda i: (jnp.maximum(i - ncb, 0), 0)),
        ],
        out_shape=[
            jax.ShapeDtypeStruct((n_ctx_tok, D), F32),
            jax.ShapeDtypeStruct((n_tok - n_ctx_tok, D), F32),
        ],
        scratch_shapes=[
            pltpu.VMEM((2, TS * N_LT, V7X_LANES), F32),
            pltpu.VMEM((2, TS * N_LT, V7X_LANES), F32),
            pltpu.VMEM((V7X_SUBLANES, TS), jnp.int32),
            pltpu.SMEM((V7X_SUBLANES, TS), jnp.int32),
            pltpu.SemaphoreType.DMA((2,)),
            pltpu.SemaphoreType.DMA,
        ],
        compiler_params=pltpu.CompilerParams(
            dimension_semantics=("arbitrary",), vmem_limit_bytes=VMEM_LIMIT),
        name="moe_combine",
    )(x, mod_l, info, info, yg, ln_g, ln_b)


def _routing_tables(counts):
    totals = jnp.sum(counts, axis=0)
    n_rt = (totals + TR - 1) // TR
    cum = jnp.cumsum(n_rt)
    first = cum - n_rt
    n_act = cum[-1]
    off = first[None, :] * TR + (jnp.cumsum(counts, axis=0) - counts)
    fill = jnp.concatenate([first * TR + totals, n_rt * TR - totals, n_act[None]])
    r = jnp.arange(MAX_RT, dtype=jnp.int32)
    rc = jnp.minimum(r, n_act - 1)
    exp = jnp.sum((rc[:, None] >= cum[None, :]).astype(jnp.int32), axis=1)
    new = jnp.logical_and(r == first[exp], r < n_act)
    i32 = lambda a: a.astype(jnp.int32)
    return i32(off.reshape(-1)), i32(fill), (i32(exp), i32(new), i32(n_act.reshape(1)))


def _moe_ffn(x, mod_l, router_w, router_b, w1, w3, w2, ln_g, ln_b, n_ctx_tok):
    n_tok = x.shape[0]
    assert 2 * n_tok // TR + N_EXP == MAX_RT and N_FC == 2
    wrt = router_w.T
    brt = jnp.broadcast_to(router_b.reshape(N_EXP, 1), (N_EXP, V7X_LANES))
    rinfo, counts, utm = _rank(x, mod_l, wrt, brt, n_ctx_tok)
    off, fill, tables = _routing_tables(counts)
    xg, info = _dispatch(utm, rinfo, off, fill)
    y_lo = _expert_pass(False, tables, xg, w1, w3, w2)
    y = _expert_pass(True, tables, xg, w1, w3, w2, y_lo)
    return _combine(x, mod_l, info, y, ln_g, ln_b, n_ctx_tok)


def _cast_kernel(*refs):
    o_ref = refs[-1]
    off = 0
    for x_ref in refs[:-1]:
        width = x_ref.shape[-1]
        o_ref[..., off:off + width] = x_ref[...].reshape(o_ref.shape[:-1] + (width,)).astype(o_ref.dtype)
        off += width


def _pack_w13(w1, w3):
    return pl.pallas_call(
        _cast_kernel,
        grid=(N_FC,),
        in_specs=[pl.BlockSpec((D, FC), lambda f: (0, f)), pl.BlockSpec((D, FC), lambda f: (0, f))],
        out_specs=pl.BlockSpec((1, D, 2 * FC), lambda f: (f, 0, 0)),
        out_shape=jax.ShapeDtypeStruct((N_FC, D, 2 * FC), BF16),
        compiler_params=pltpu.CompilerParams(vmem_limit_bytes=VMEM_LIMIT),
        name="pack_w13",
    )(w1, w3)


def _pack_w_in(w_in):
    depth = w_in.shape[0]
    w5 = pl.pallas_call(
        _cast_kernel,
        grid=(depth, N_CH, 5),
        in_specs=[pl.BlockSpec((1, D, CW), lambda l, j, g: (l, 0, g * N_CH + j))],
        out_specs=pl.BlockSpec((1, 1, D, CW), lambda l, j, g: (l, j, 0, g)),
        out_shape=jax.ShapeDtypeStruct((depth, N_CH, D, 5 * CW), BF16),
        compiler_params=pltpu.CompilerParams(vmem_limit_bytes=VMEM_LIMIT),
        name="pack_w5",
    )(w_in)
    wg = pl.pallas_call(
        _cast_kernel,
        grid=(depth, 2),
        in_specs=[pl.BlockSpec((1, D, D), lambda l, k: (l, 0, 5 + k))],
        out_specs=pl.BlockSpec((1, D, D), lambda l, k: (l, 0, k)),
        out_shape=jax.ShapeDtypeStruct((depth, D, 2 * D), BF16),
        compiler_params=pltpu.CompilerParams(vmem_limit_bytes=VMEM_LIMIT),
        name="pack_wg",
    )(w_in)
    return w5, wg


def _block_diag_chunks(w):
    tiled = jnp.tile(w.reshape(N_CH, CW, HEAD_D), (1, 1, CW // HEAD_D))
    blk = jnp.arange(CW) // HEAD_D
    return jnp.where(blk[:, None] == blk[None, :], tiled, 0.0)


def kernel(x_prompt, x_sample, state_rglru, c, c_ctx, w_mod, b_mod, w_in, conv_a, w_a_out, conv_b, conv_b_bias, w_gate_a, b_gate_a, w_gate_x, b_gate_x, lru_lambda, w_b_out, w_o, ln1_g, ln1_b, ln2_g, ln2_b, ffn_w1, ffn_w3, ffn_w2, router_w, router_b, moe_w1, moe_w3, moe_w2):
    batch, seq, d = x_prompt.shape
    dec_batch, dec_seq, _ = x_sample.shape
    depth = w_mod.shape[0]
    assert (d, seq, dec_seq, depth) == (D, SEQ, DEC_SEQ, DEPTH)
    n_ctx_tok = batch * seq
    n_dec_tok = dec_batch * dec_seq
    assert n_ctx_tok % TM1 == 0 and TM1 == dec_seq and 1 + dec_batch <= MOD_ROWS
    n_ctx_tiles = n_ctx_tok // TM1
    n_tiles = n_ctx_tiles + dec_batch
    seq_per_tile = TM1 // seq
    assert N_SEG == 2 * seq_per_tile

    assert depth % 2 == 0
    n_tok = n_ctx_tok + n_dec_tok
    x = (x_prompt.reshape(n_ctx_tok, D), x_sample.reshape(n_dec_tok, D))

    cond = jnp.zeros((MOD_ROWS, D), F32).at[0].set(c_ctx).at[1:1 + dec_batch].set(c)
    mod = _modulation(cond, w_mod, b_mod)

    w5, wg = _pack_w_in(w_in)
    states = []
    for l in range(depth):
        wbd = jnp.concatenate(
            [_block_diag_chunks(w_gate_a[l, 0]), _block_diag_chunks(w_gate_x[l, 0]),
             _block_diag_chunks(w_gate_a[l, 1]), _block_diag_chunks(w_gate_x[l, 1])],
            axis=-1).astype(BF16)
        gbias = jnp.stack([b_gate_a[l, 0], b_gate_x[l, 0], b_gate_a[l, 1], b_gate_x[l, 1]], axis=0)
        h0 = jnp.zeros((n_tiles, 2, N_SEG, D), F32)
        h0 = h0.at[n_ctx_tiles:, 0, 0].set(state_rglru[:, l, 0].astype(F32))
        h0 = h0.at[n_ctx_tiles:, 1, N_SEG - 1].set(state_rglru[:, l, 1].astype(F32))

        a_pre, b_pre, st = _mixer_part1(
            x, mod[l], l, w5, conv_a[l], conv_b[l], conv_b_bias[l].reshape(1, D), wbd, gbias,
            lru_lambda[l], h0, n_tok, n_ctx_tok)
        x = _mixer_part2(
            x, mod[l], l, a_pre, b_pre, wg, w_a_out[l].astype(BF16), w_b_out[l].astype(BF16),
            w_o[l].astype(BF16), ln1_g[l].reshape(1, D), ln1_b[l].reshape(1, D), n_ctx_tok)

        if l % 2 == 0:
            k = l // 2
            x = _dense_ffn(x, mod[l], _pack_w13(ffn_w1[k], ffn_w3[k]), ffn_w2[k].astype(BF16),
                           ln2_g[l].reshape(1, D), ln2_b[l].reshape(1, D), n_ctx_tok)
        else:
            k = l // 2
            x = _moe_ffn(x, mod[l], router_w[k], router_b[k], moe_w1[k], moe_w3[k], moe_w2[k],
                         ln2_g[l].reshape(1, D), ln2_b[l].reshape(1, D), n_ctx_tok)

        st_ctx = st[:n_ctx_tiles]
        fwd = st_ctx[:, 0, 1::2].reshape(batch, D)
        bwd = st_ctx[:, 1, 0::2].reshape(batch, D)
        states.append(jnp.stack([fwd, bwd], axis=1))

    y_prompt = x[0].reshape(batch, seq, D)
    y_sample = x[1].reshape(dec_batch, dec_seq, D)
    new_state = jnp.stack(states, axis=1).astype(x_prompt.dtype)
    return (y_prompt, y_sample, new_state)
```

```python
import functools

import jax
import jax.numpy as jnp
from jax import lax
from jax.experimental import pallas as pl
from jax.experimental.pallas import tpu as pltpu

F32 = jnp.float32
BF16 = jnp.bfloat16

D = 1024
SEQ = 256
DEC_SEQ = 1024
GRID_W = 64
N_HEAD = 16
HEAD_D = D // N_HEAD
RGLRU_C = 8.0
LOG2_E = 1.4426950408889634
D_FF = 2816
N_EXP = 8
N_MOD = 6
DEPTH = 2
DN_ALPHA = (2.0 * DEPTH) ** 0.25

V7X_SUBLANES = 8
V7X_LANES = 128
V7X_VMEM_BYTES = 64 * 1024 * 1024
VMEM_LIMIT = V7X_VMEM_BYTES - 12 * 1024 * 1024

TM1 = 1024
CW = 256
N_CH = D // CW
N_LC = CW // V7X_LANES
N_SEG = V7X_SUBLANES
SEG = TM1 // N_SEG
SEG_STRIDE = SEG + 8
CONV_GAP = V7X_SUBLANES
TM2 = 512
TM3 = 512
FC = 1408
N_FC = D_FF // FC
TS = 512
TR = 256
MAX_RT = 2 * 10240 // TR + N_EXP
EXPERT_VMEM_LIMIT = V7X_VMEM_BYTES - 6 * 1024 * 1024
MOD_ROWS = 8
MOD_BLK = 3072


def _sigmoid(x):
    return 0.5 * jnp.tanh(0.5 * x) + 0.5


def _ln_plain(x, eps):
    mu = jnp.mean(x, axis=-1, keepdims=True)
    xc = x - mu
    var = jnp.mean(xc * xc, axis=-1, keepdims=True)
    return xc * lax.rsqrt(var + eps)


def _token_source(xs, tile, n_ctx_tok):
    ncb = n_ctx_tok // tile
    if isinstance(xs, tuple):
        x_ctx, x_dec = xs
        dec_off = 0
    else:
        x_ctx = x_dec = xs
        dec_off = ncb
    ctx_map = lambda i: (jnp.minimum(i, ncb - 1), 0)
    dec_map = lambda i: (jnp.maximum(i - ncb, 0) + dec_off, 0)
    return x_ctx, x_dec, ctx_map, dec_map


def _mod_row(tok0, n_ctx_tok):
    dec = jnp.maximum(tok0 - n_ctx_tok, 0) // DEC_SEQ
    return jnp.where(tok0 >= n_ctx_tok, 1 + dec, 0)


def _mod_kernel(cond_ref, w_ref, b_ref, o_ref):
    cnd = cond_ref[...]
    s = cnd * _sigmoid(cnd)
    o_ref[0] = jnp.dot(s, w_ref[0], preferred_element_type=F32,
                       precision=lax.Precision.HIGHEST) + b_ref[0]


def _modulation(cond, w_mod, b_mod):
    depth = w_mod.shape[0]
    n_out = w_mod.shape[2]
    return pl.pallas_call(
        _mod_kernel,
        grid=(depth, n_out // MOD_BLK),
        in_specs=[
            pl.BlockSpec((MOD_ROWS, D), lambda l, j: (0, 0)),
            pl.BlockSpec((1, D, MOD_BLK), lambda l, j: (l, 0, j)),
            pl.BlockSpec((1, 1, MOD_BLK), lambda l, j: (l, 0, j)),
        ],
        out_specs=pl.BlockSpec((1, MOD_ROWS, MOD_BLK), lambda l, j: (l, 0, j)),
        out_shape=jax.ShapeDtypeStruct((depth, MOD_ROWS, n_out), F32),
        compiler_params=pltpu.CompilerParams(
            dimension_semantics=("arbitrary", "arbitrary"), vmem_limit_bytes=VMEM_LIMIT),
        name="modulation",
    )(cond, w_mod, b_mod.reshape(depth, 1, n_out))


def _scan_dir(a_scr, b_scr, hl_scr, ac_scr, h0, keep, reverse):
    n_lc = a_scr.shape[0]

    def body(k, carry):
        kk = SEG - 1 - k if reverse else k
        idx = pl.ds(kk, N_SEG, stride=SEG_STRIDE)
        new = []
        for c in range(n_lc):
            h, acc = carry[c]
            a_k = a_scr[c, idx, :]
            h = a_k * h + b_scr[c, idx, :]
            acc = a_k * acc
            hl_scr[c, idx, :] = h
            ac_scr[c, idx, :] = acc
            new.append((h, acc))
        return tuple(new)

    init = tuple((jnp.zeros((N_SEG, V7X_LANES), F32), jnp.ones((N_SEG, V7X_LANES), F32))
                 for _ in range(n_lc))
    fin = lax.fori_loop(0, SEG, body, init, unroll=8)
    h_loc = jnp.concatenate([fin[c][0] for c in range(n_lc)], axis=1)
    a_tot = jnp.concatenate([fin[c][1] for c in range(n_lc)], axis=1)

    order = range(N_SEG - 1, -1, -1) if reverse else range(N_SEG)
    h_in = [None] * N_SEG
    prev = None
    for s in order:
        cur = h0[s:s + 1]
        if prev is not None:
            left = a_tot[prev:prev + 1] * h_in[prev] + h_loc[prev:prev + 1]
            cur = keep[s] * left + cur
        h_in[s] = cur
        prev = s
    h_in = jnp.concatenate(h_in, axis=0)
    return h_in, a_tot * h_in + h_loc


def _conv_stage(is_dec, proj_scr, ca_ref, cb_ref, cbias_ref, apre_ref, xr_scr, gap_scr):
    win = GRID_W
    n_win = TM1 // win
    stride = win + CONV_GAP
    ca = ca_ref[...]
    cb = cb_ref[...]
    bias = cbias_ref[...]
    pieces = [(w, c) for w in range(n_win) for c in range(N_LC)]
    sub = lax.broadcasted_iota(jnp.int32, (CONV_GAP, V7X_LANES), 0)
    joined = jnp.where(is_dec, jnp.float32(0.0), jnp.float32(1.0))
    zero_gap = jnp.zeros((CONV_GAP, V7X_LANES), F32)

    def stage(value_of):
        for c in range(N_LC):
            for w in range(n_win + 1):
                edge = w * win
                if edge % SEQ == 0:
                    gap = zero_gap
                else:
                    before = value_of(slice(edge - CONV_GAP, edge), c)
                    after = value_of(slice(edge, edge + CONV_GAP), c)
                    gap = joined * jnp.where(sub >= CONV_GAP - 2, before, jnp.where(sub == 0, after, 0.0))
                gap_scr[c, w * stride:w * stride + CONV_GAP, :] = gap
        for w, c in pieces:
            lo = CONV_GAP + w * stride
            gap_scr[c, lo:lo + win, :] = value_of(slice(w * win, (w + 1) * win), c)

    def tap(w, c, shift):
        lo = CONV_GAP + w * stride + shift
        return gap_scr[c, lo:lo + win, :]

    def chunk_cols(k, c):
        return slice(k * CW + c * V7X_LANES, k * CW + (c + 1) * V7X_LANES)

    stage(lambda rows, c: proj_scr[rows, chunk_cols(2, c)] * proj_scr[rows, chunk_cols(0, c)])
    for w, c in pieces:
        rows = slice(w * win, (w + 1) * win)
        lanes = slice(c * V7X_LANES, (c + 1) * V7X_LANES)
        conv = ca[0:1, lanes] * tap(w, c, -1) + ca[1:2, lanes] * tap(w, c, 0) + ca[2:3, lanes] * tap(w, c, 1)
        apre_ref[rows, lanes] = (proj_scr[rows, chunk_cols(1, c)] * conv).astype(BF16)
    stage(lambda rows, c: proj_scr[rows, chunk_cols(4, c)])
    for w, c in pieces:
        rows = slice(w * win, (w + 1) * win)
        lanes = slice(c * V7X_LANES, (c + 1) * V7X_LANES)
        xr_scr[rows, lanes] = (cb[0:1, lanes] * tap(w, c, -2) + cb[1:2, lanes] * tap(w, c, -1)
                               + cb[2:3, lanes] * tap(w, c, 0) + cb[3:4, lanes] * tap(w, c, 1)
                               + bias[:, lanes])


def _mix1_kernel(n_ctx_tiles, xc_ref, xd_ref, mod_ref, w5_ref, ca_ref, cb_ref, cbias_ref, wbd_ref, gb_ref,
                 lam_ref, h0_ref, apre_ref, bpre_ref, st_ref,
                 u_scr, proj_scr, a_scr, b_scr, hl_scr, ac_scr, hsum_scr, xr_scr, gap_scr):
    i = pl.program_id(0)
    j = pl.program_id(1)
    is_dec = i >= n_ctx_tiles

    @pl.when(j == 0)
    def _():
        row = jnp.where(is_dec, i - (n_ctx_tiles - 1), 0)
        m = mod_ref[pl.ds(row, 1), :]
        sh1 = m[:, 0:D]
        sc1 = m[:, D:2 * D]
        x = jnp.where(is_dec, xd_ref[...], xc_ref[...])
        u_scr[...] = (_ln_plain(x, 1e-6) * (1.0 + sc1) + sh1).astype(BF16)

    proj_scr[...] = jnp.dot(u_scr[...], w5_ref[0, 0], preferred_element_type=F32)

    _conv_stage(is_dec, proj_scr, ca_ref, cb_ref, cbias_ref, apre_ref, xr_scr, gap_scr)

    xr = xr_scr[...]
    gates = jnp.dot(xr.astype(BF16), wbd_ref[0], preferred_element_type=F32)
    gb = gb_ref[...]
    lam = lam_ref[...]
    sp = jnp.maximum(-lam, 0.0) + jnp.log1p(jnp.exp(-jnp.abs(lam)))
    rate = (-RGLRU_C * LOG2_E) * sp

    one = jnp.float32(1.0)
    for d in range(2):
        ga = gates[:, (2 * d) * CW:(2 * d + 1) * CW] + gb[2 * d:2 * d + 1]
        gx = gates[:, (2 * d + 1) * CW:(2 * d + 2) * CW] + gb[2 * d + 1:2 * d + 2]
        r = _sigmoid(ga)
        ig = _sigmoid(gx)
        a = jnp.exp2(r * rate[d:d + 1])
        y = 1.0 - a * a
        bt = jnp.where(y > 0.0, y * lax.rsqrt(y), 0.0) * (ig * xr)
        for s in range(N_SEG):
            lo = s * SEG_STRIDE
            for c in range(N_LC):
                lanes = slice(c * V7X_LANES, (c + 1) * V7X_LANES)
                a_scr[c, lo:lo + SEG, :] = a[s * SEG:(s + 1) * SEG, lanes]
                b_scr[c, lo:lo + SEG, :] = bt[s * SEG:(s + 1) * SEG, lanes]
        if d == 0:
            keep = [jnp.where(is_dec, one, jnp.float32(s % 2 == 1)) for s in range(N_SEG)]
        else:
            keep = [jnp.where(is_dec, one, jnp.float32(s % 2 == 0)) for s in range(N_SEG)]
        h_in, h_out = _scan_dir(a_scr, b_scr, hl_scr, ac_scr, h0_ref[0, d], keep, reverse=(d == 1))
        st_ref[0, d] = h_out
        for s in range(N_SEG):
            lo = s * SEG_STRIDE
            for c in range(N_LC):
                lanes = slice(c * V7X_LANES, (c + 1) * V7X_LANES)
                h_seg = hl_scr[c, lo:lo + SEG, :] + ac_scr[c, lo:lo + SEG, :] * h_in[s:s + 1, lanes]
                if d == 0:
                    hsum_scr[s * SEG:(s + 1) * SEG, lanes] = h_seg
                else:
                    hsum_scr[s * SEG:(s + 1) * SEG, lanes] += h_seg

    bpre_ref[...] = (hsum_scr[...] * jax.nn.gelu(proj_scr[:, 3 * CW:4 * CW])).astype(BF16)


def _mixer_part1(xs, mod_l, layer, w5, conv_a, conv_b, conv_b_bias, wbd, gbias, lam, h0, n_tok, n_ctx_tok):
    n_tiles = n_tok // TM1
    x_ctx, x_dec, ctx_map, dec_map = _token_source(xs, TM1, n_ctx_tok)
    kern = functools.partial(_mix1_kernel, n_ctx_tok // TM1)
    return pl.pallas_call(
        kern,
        grid=(n_tiles, N_CH),
        in_specs=[
            pl.BlockSpec((TM1, D), lambda i, j: ctx_map(i)),
            pl.BlockSpec((TM1, D), lambda i, j: dec_map(i)),
            pl.BlockSpec((MOD_ROWS, N_MOD * D), lambda i, j: (0, 0)),
            pl.BlockSpec((1, 1, D, 5 * CW), lambda i, j: (layer, j, 0, 0)),
            pl.BlockSpec((3, CW), lambda i, j: (0, j)),
            pl.BlockSpec((4, CW), lambda i, j: (0, j)),
            pl.BlockSpec((1, CW), lambda i, j: (0, j)),
            pl.BlockSpec((1, CW, 4 * CW), lambda i, j: (j, 0, 0)),
            pl.BlockSpec((4, CW), lambda i, j: (0, j)),
            pl.BlockSpec((2, CW), lambda i, j: (0, j)),
            pl.BlockSpec((1, 2, N_SEG, CW), lambda i, j: (i, 0, 0, j)),
        ],
        out_specs=[
            pl.BlockSpec((TM1, CW), lambda i, j: (i, j)),
            pl.BlockSpec((TM1, CW), lambda i, j: (i, j)),
            pl.BlockSpec((1, 2, N_SEG, CW), lambda i, j: (i, 0, 0, j)),
        ],
        out_shape=[
            jax.ShapeDtypeStruct((n_tok, D), BF16),
            jax.ShapeDtypeStruct((n_tok, D), BF16),
            jax.ShapeDtypeStruct((n_tiles, 2, N_SEG, D), F32),
        ],
        scratch_shapes=[
            pltpu.VMEM((TM1, D), BF16),
            pltpu.VMEM((TM1, 5 * CW), F32),
            pltpu.VMEM((N_LC, N_SEG * SEG_STRIDE, V7X_LANES), F32),
            pltpu.VMEM((N_LC, N_SEG * SEG_STRIDE, V7X_LANES), F32),
            pltpu.VMEM((N_LC, N_SEG * SEG_STRIDE, V7X_LANES), F32),
            pltpu.VMEM((N_LC, N_SEG * SEG_STRIDE, V7X_LANES), F32),
            pltpu.VMEM((TM1, CW), F32),
            pltpu.VMEM((TM1, CW), F32),
            pltpu.VMEM((N_LC, CONV_GAP + (TM1 // GRID_W) * (GRID_W + CONV_GAP), V7X_LANES), F32),
        ],
        compiler_params=pltpu.CompilerParams(
            dimension_semantics=("arbitrary", "arbitrary"), vmem_limit_bytes=VMEM_LIMIT),
        name="mixer_scan",
    )(x_ctx, x_dec, mod_l, w5, conv_a, conv_b, conv_b_bias, wbd, gbias, lam, h0)


def _mix2_kernel(n_ctx_tok, xc_ref, xd_ref, mod_ref, ap_ref, bp_ref, wg_ref, wa_ref, wb_ref, wo_ref,
                 g_ref, b_ref, o_ref):
    i = pl.program_id(0)
    m = mod_ref[pl.ds(_mod_row(i * TM2, n_ctx_tok), 1), :]
    sh1 = m[:, 0:D]
    sc1 = m[:, D:2 * D]
    g1 = m[:, 2 * D:3 * D]
    x = jnp.where(i * TM2 >= n_ctx_tok, xd_ref[...], xc_ref[...])
    u = (_ln_plain(x, 1e-6) * (1.0 + sc1) + sh1).astype(BF16)
    gates = jnp.dot(u, wg_ref[0], preferred_element_type=F32)
    br_a = jnp.dot(ap_ref[...], wa_ref[...], preferred_element_type=F32)
    br_b = jnp.dot(bp_ref[...], wb_ref[...], preferred_element_type=F32)
    merged = _sigmoid(gates[:, 0:D]) * br_a + _sigmoid(gates[:, D:2 * D]) * br_b
    mix = jnp.dot(merged.astype(BF16), wo_ref[...], preferred_element_type=F32)
    y = DN_ALPHA * x + g1 * mix
    o_ref[...] = _ln_plain(y, 1e-5) * g_ref[...] + b_ref[...]


def _mixer_part2(xs, mod_l, layer, a_pre, b_pre, wg, wa, wb, wo, ln_g, ln_b, n_ctx_tok):
    n_tok = a_pre.shape[0]
    x_ctx, x_dec, ctx_map, dec_map = _token_source(xs, TM2, n_ctx_tok)
    const = lambda i: (0, 0)
    tile = lambda i: (i, 0)
    return pl.pallas_call(
        functools.partial(_mix2_kernel, n_ctx_tok),
        grid=(n_tok // TM2,),
        in_specs=[
            pl.BlockSpec((TM2, D), ctx_map),
            pl.BlockSpec((TM2, D), dec_map),
            pl.BlockSpec((MOD_ROWS, N_MOD * D), const),
            pl.BlockSpec((TM2, D), tile),
            pl.BlockSpec((TM2, D), tile),
            pl.BlockSpec((1, D, 2 * D), lambda i: (layer, 0, 0)),
            pl.BlockSpec((D, D), const),
            pl.BlockSpec((D, D), const),
            pl.BlockSpec((D, D), const),
            pl.BlockSpec((1, D), const),
            pl.BlockSpec((1, D), const),
        ],
        out_specs=pl.BlockSpec((TM2, D), tile),
        out_shape=jax.ShapeDtypeStruct((n_tok, D), F32),
        compiler_params=pltpu.CompilerParams(
            dimension_semantics=("arbitrary",), vmem_limit_bytes=VMEM_LIMIT),
        name="mixer_out",
    )(x_ctx, x_dec, mod_l, a_pre, b_pre, wg, wa, wb, wo, ln_g, ln_b)


def _ffn_prologue(n_ctx_tok, x_ref, mod_ref):
    m = mod_ref[pl.ds(_mod_row(pl.program_id(0) * TM3, n_ctx_tok), 1), :]
    sh2 = m[:, 3 * D:4 * D]
    sc2 = m[:, 4 * D:5 * D]
    return _ln_plain(x_ref[...], 1e-6) * (1.0 + sc2) + sh2


def _ffn_epilogue(n_ctx_tok, x_ref, mod_ref, acc, g_ref, b_ref):
    m = mod_ref[pl.ds(_mod_row(pl.program_id(0) * TM3, n_ctx_tok), 1), :]
    g2 = m[:, 5 * D:6 * D]
    y = DN_ALPHA * x_ref[...] + g2 * acc
    return _ln_plain(y, 1e-5) * g_ref[...] + b_ref[...]


def _swiglu_chunk(u, w13, w2):
    h = jnp.dot(u, w13, preferred_element_type=F32)
    h1 = h[:, 0:FC]
    hid = (h1 * _sigmoid(h1) * h[:, FC:2 * FC]).astype(BF16)
    return jnp.dot(hid, w2, preferred_element_type=F32)


def _dense_ffn_kernel(n_ctx_tok, x_ref, mod_ref, w13_ref, w2_ref, g_ref, b_ref, o_ref, u_scr, acc_scr):
    f = pl.program_id(1)

    @pl.when(f == 0)
    def _():
        u_scr[...] = _ffn_prologue(n_ctx_tok, x_ref, mod_ref).astype(BF16)

    p = _swiglu_chunk(u_scr[...], w13_ref[0], w2_ref[...])

    @pl.when(f == 0)
    def _():
        acc_scr[...] = p

    @pl.when(f > 0)
    def _():
        acc_scr[...] += p

    @pl.when(f == N_FC - 1)
    def _():
        o_ref[...] = _ffn_epilogue(n_ctx_tok, x_ref, mod_ref, acc_scr[...], g_ref, b_ref)


def _dense_ffn(x, mod_l, w13, w2, ln_g, ln_b, n_ctx_tok):
    n_tok = x.shape[0]
    return pl.pallas_call(
        functools.partial(_dense_ffn_kernel, n_ctx_tok),
        grid=(n_tok // TM3, N_FC),
        in_specs=[
            pl.BlockSpec((TM3, D), lambda i, f: (i, 0)),
            pl.BlockSpec((MOD_ROWS, N_MOD * D), lambda i, f: (0, 0)),
            pl.BlockSpec((1, D, 2 * FC), lambda i, f: (f, 0, 0)),
            pl.BlockSpec((FC, D), lambda i, f: (f, 0)),
            pl.BlockSpec((1, D), lambda i, f: (0, 0)),
            pl.BlockSpec((1, D), lambda i, f: (0, 0)),
        ],
        out_specs=pl.BlockSpec((TM3, D), lambda i, f: (i, 0)),
        out_shape=jax.ShapeDtypeStruct((n_tok, D), F32),
        scratch_shapes=[pltpu.VMEM((TM3, D), BF16), pltpu.VMEM((TM3, D), F32)],
        compiler_params=pltpu.CompilerParams(
            dimension_semantics=("arbitrary", "arbitrary"), vmem_limit_bytes=VMEM_LIMIT),
        name="dense_ffn",
    )(x, mod_l, w13, w2, ln_g, ln_b)


N_LT = D // V7X_LANES


def _to_token_major(ref, rows):
    n = rows.shape[0]
    for c in range(N_LT):
        ref[pl.ds(c, n, stride=N_LT), :] = rows[:, c * V7X_LANES:(c + 1) * V7X_LANES]


def _from_token_major(ref, n):
    return jnp.concatenate([ref[pl.ds(c, n, stride=N_LT), :] for c in range(N_LT)], axis=1)


def _rank_kernel(n_ctx_tok, x_ref, mod_ref, wrt_ref, brt_ref, info_ref, cnt_ref, utm_ref):
    i = pl.program_id(0)
    u2 = _ffn_prologue(n_ctx_tok, x_ref, mod_ref)
    _to_token_major(utm_ref, u2)
    lg = lax.dot_general(wrt_ref[...], u2, (((1,), (1,)), ((), ())), preferred_element_type=F32,
                         precision=lax.Precision.HIGHEST) + brt_ref[:, 0:1]
    eidx = lax.broadcasted_iota(jnp.int32, lg.shape, 0).astype(F32)
    neg = jnp.float32(-jnp.inf)
    v1 = jnp.max(lg, axis=0, keepdims=True)
    i1 = jnp.min(jnp.where(lg == v1, eidx, float(N_EXP)), axis=0, keepdims=True)
    lg2 = jnp.where(eidx == i1, neg, lg)
    v2 = jnp.max(lg2, axis=0, keepdims=True)
    i2 = jnp.min(jnp.where(lg2 == v2, eidx, float(N_EXP)), axis=0, keepdims=True)
    t = jnp.exp(v2 - v1)
    w_top = 1.0 / (1.0 + t)
    m1 = eidx == i1
    m2 = eidx == i2
    member = jnp.where(jnp.logical_or(m1, m2), 1.0, 0.0)
    before = (lax.broadcasted_iota(jnp.int32, (TS, TS), 0)
              < lax.broadcasted_iota(jnp.int32, (TS, TS), 1))
    rank = jnp.dot(member.astype(BF16), jnp.where(before, 1.0, 0.0).astype(BF16),
                   preferred_element_type=F32)
    for e in range(N_EXP):
        cnt_ref[i, e] = jnp.sum(member[e:e + 1, :]).astype(jnp.int32)
    rank1 = jnp.sum(jnp.where(m1, rank, 0.0), axis=0, keepdims=True)
    rank2 = jnp.sum(jnp.where(m2, rank, 0.0), axis=0, keepdims=True)
    info_ref[0] = jnp.concatenate(
        [i1, i2, rank1, rank2, w_top, t * w_top, jnp.zeros((V7X_SUBLANES - 6, TS), F32)], axis=0)


def _rank(x, mod_l, wrt, brt, n_ctx_tok):
    n_tiles = x.shape[0] // TS
    return pl.pallas_call(
        functools.partial(_rank_kernel, n_ctx_tok),
        grid=(n_tiles,),
        in_specs=[
            pl.BlockSpec((TS, D), lambda i: (i, 0)),
            pl.BlockSpec((MOD_ROWS, N_MOD * D), lambda i: (0, 0)),
            pl.BlockSpec((N_EXP, D), lambda i: (0, 0)),
            pl.BlockSpec((N_EXP, V7X_LANES), lambda i: (0, 0)),
        ],
        out_specs=[
            pl.BlockSpec((1, V7X_SUBLANES, TS), lambda i: (i, 0, 0)),
            pl.BlockSpec(memory_space=pltpu.SMEM),
            pl.BlockSpec((TS * N_LT, V7X_LANES), lambda i: (i, 0)),
        ],
        out_shape=[
            jax.ShapeDtypeStruct((n_tiles, V7X_SUBLANES, TS), F32),
            jax.ShapeDtypeStruct((n_tiles, N_EXP), jnp.int32),
            jax.ShapeDtypeStruct((n_tiles * TS * N_LT, V7X_LANES), F32),
        ],
        compiler_params=pltpu.CompilerParams(
            dimension_semantics=("arbitrary",), vmem_limit_bytes=VMEM_LIMIT),
        name="moe_rank",
    )(x, mod_l, wrt, brt)


def _dispatch_kernel(off_ref, fill_ref, rinfo_ref, utm_ref, xg_ref, info_ref,
                     pos_v, pos_s, zero_scr, sem, psem, zsem):
    i = pl.program_id(0)
    n_steps = pl.num_programs(0)
    rinfo = rinfo_ref[0]
    i1 = rinfo[0:1]
    i2 = rinfo[1:2]
    off1 = jnp.zeros_like(i1)
    off2 = jnp.zeros_like(i2)
    for e in range(N_EXP):
        start = off_ref[i * N_EXP + e].astype(F32)
        off1 = jnp.where(i1 == float(e), start, off1)
        off2 = jnp.where(i2 == float(e), start, off2)
    info = jnp.concatenate(
        [off1 + rinfo[2:3], off2 + rinfo[3:4], rinfo[4:6], jnp.zeros((V7X_SUBLANES - 4, TS), F32)],
        axis=0)
    info_ref[0] = info
    pos_v[...] = info.astype(jnp.int32)
    cp = pltpu.make_async_copy(pos_v, pos_s, psem)
    cp.start()
    cp.wait()

    def row_tile(ref, row):
        return ref.at[pl.ds(pl.multiple_of(row * N_LT, N_LT), N_LT)]

    def row_copy(tok, dst_row):
        return pltpu.make_async_copy(row_tile(utm_ref, tok), row_tile(xg_ref, dst_row), sem)

    def issue(tok, carry):
        row_copy(tok, pos_s[0, tok]).start(priority=0)
        row_copy(tok, pos_s[1, tok]).start(priority=1)
        return carry

    lax.fori_loop(0, TS, issue, 0, unroll=8)
    for _ in range(2):
        pltpu.make_async_copy(utm_ref, xg_ref.at[pl.ds(0, TS * N_LT)], sem).wait()

    @pl.when(i == n_steps - 1)
    def _():
        zero_scr[...] = jnp.zeros_like(zero_scr)
        for e in range(N_EXP):
            end = fill_ref[e]
            n_tail = fill_ref[N_EXP + e]

            def tail_copy(k):
                return pltpu.make_async_copy(row_tile(zero_scr, 0), row_tile(xg_ref, end + k), zsem)

            def tail_start(k, carry):
                tail_copy(k).start()
                return carry

            def tail_wait(k, carry):
                tail_copy(k).wait()
                return carry

            lax.fori_loop(0, n_tail, tail_start, 0)
            lax.fori_loop(0, n_tail, tail_wait, 0)

        def tile_copy(k):
            rows = TR * N_LT
            return pltpu.make_async_copy(zero_scr, xg_ref.at[pl.ds(pl.multiple_of(k * rows, rows), rows)], zsem)

        def tile_start(k, carry):
            tile_copy(k).start()
            return carry

        def tile_wait(k, carry):
            tile_copy(k).wait()
            return carry

        lax.fori_loop(fill_ref[2 * N_EXP], MAX_RT, tile_start, 0)
        lax.fori_loop(fill_ref[2 * N_EXP], MAX_RT, tile_wait, 0)


def _dispatch(utm, rinfo, off, fill):
    n_tiles = rinfo.shape[0]
    return pl.pallas_call(
        _dispatch_kernel,
        grid_spec=pltpu.PrefetchScalarGridSpec(
            num_scalar_prefetch=2,
            grid=(n_tiles,),
            in_specs=[
                pl.BlockSpec((1, V7X_SUBLANES, TS), lambda i, off, fill: (i, 0, 0)),
                pl.BlockSpec((TS * N_LT, V7X_LANES), lambda i, off, fill: (i, 0)),
            ],
            out_specs=[
                pl.BlockSpec(memory_space=pl.ANY),
                pl.BlockSpec((1, V7X_SUBLANES, TS), lambda i, off, fill: (i, 0, 0)),
            ],
            scratch_shapes=[
                pltpu.VMEM((V7X_SUBLANES, TS), jnp.int32),
                pltpu.SMEM((V7X_SUBLANES, TS), jnp.int32),
                pltpu.VMEM((TR * N_LT, V7X_LANES), F32),
                pltpu.SemaphoreType.DMA,
                pltpu.SemaphoreType.DMA,
                pltpu.SemaphoreType.DMA,
            ],
        ),
        out_shape=[
            jax.ShapeDtypeStruct((MAX_RT * TR * N_LT, V7X_LANES), F32),
            jax.ShapeDtypeStruct((n_tiles, V7X_SUBLANES, TS), F32),
        ],
        compiler_params=pltpu.CompilerParams(
            dimension_semantics=("arbitrary",), vmem_limit_bytes=VMEM_LIMIT),
        name="moe_dispatch",
    )(off, fill, rinfo, utm)


def _expert_kernel(second, exp_ref, new_ref, nact_ref, *refs):
    if second:
        xg_ref, w1_ref, w3_ref, w2_ref, yp_ref, o_ref, w1b, w3b, w2b = refs
    else:
        xg_ref, w1_ref, w3_ref, w2_ref, o_ref, w1b, w3b, w2b = refs
    r = pl.program_id(0)
    active = r < nact_ref[0]

    @pl.when(jnp.logical_and(active, new_ref[r] == 1))
    def _():
        w1b[...] = w1_ref[0].astype(BF16)
        w3b[...] = w3_ref[0].astype(BF16)
        w2b[...] = w2_ref[0].astype(BF16)

    @pl.when(active)
    def _():
        xb = _from_token_major(xg_ref, TR).astype(BF16)
        h1 = jnp.dot(xb, w1b[...], preferred_element_type=F32)
        h3 = jnp.dot(xb, w3b[...], preferred_element_type=F32)
        hid = (h1 * _sigmoid(h1) * h3).astype(BF16)
        p = jnp.dot(hid, w2b[...], preferred_element_type=F32)
        if second:
            _to_token_major(o_ref, yp_ref[...] + p)
        else:
            o_ref[...] = p

    @pl.when(jnp.logical_not(active))
    def _():
        o_ref[...] = jnp.zeros_like(o_ref)


def _expert_pass(second, tables, xg, w1, w3, w2, yp=None):
    f = 1 if second else 0
    row = lambda r, ex, nw, na: (r, 0)
    tm_block = pl.BlockSpec((TR * N_LT, V7X_LANES), row)
    tm_shape = jax.ShapeDtypeStruct((MAX_RT * TR * N_LT, V7X_LANES), F32)
    in_specs = [
        tm_block,
        pl.BlockSpec((1, D, FC), lambda r, ex, nw, na: (ex[r], 0, f)),
        pl.BlockSpec((1, D, FC), lambda r, ex, nw, na: (ex[r], 0, f)),
        pl.BlockSpec((1, FC, D), lambda r, ex, nw, na: (ex[r], f, 0)),
    ]
    args = [xg, w1, w3, w2]
    if second:
        in_specs.append(pl.BlockSpec((TR, D), row))
        args.append(yp)
    return pl.pallas_call(
        functools.partial(_expert_kernel, second),
        grid_spec=pltpu.PrefetchScalarGridSpec(
            num_scalar_prefetch=3,
            grid=(MAX_RT,),
            in_specs=in_specs,
            out_specs=tm_block if second else pl.BlockSpec((TR, D), row),
            scratch_shapes=[pltpu.VMEM((D, FC), BF16), pltpu.VMEM((D, FC), BF16),
                            pltpu.VMEM((FC, D), BF16)],
        ),
        out_shape=tm_shape if second else jax.ShapeDtypeStruct((MAX_RT * TR, D), F32),
        compiler_params=pltpu.CompilerParams(
            dimension_semantics=("arbitrary",), vmem_limit_bytes=EXPERT_VMEM_LIMIT),
        name="moe_expert_hi" if second else "moe_expert_lo",
    )(*tables, *args)


def _combine_kernel(n_ctx_tok, x_ref, mod_ref, info_ref, info_next_ref, yg_ref, g_ref, b_ref, oc_ref, od_ref,
                    ya_scr, yb_scr, pos_v, pos_s, sem, psem):
    i = pl.program_id(0)
    n_steps = pl.num_programs(0)
    slot = i % 2

    def gather_tile(info, s):
        pos_v[...] = info.astype(jnp.int32)
        cp = pltpu.make_async_copy(pos_v, pos_s, psem)
        cp.start()
        cp.wait()

        def row_tile(ref, row):
            return ref.at[pl.ds(pl.multiple_of(row * N_LT, N_LT), N_LT)]

        def issue(tok, carry):
            pltpu.make_async_copy(row_tile(yg_ref, pos_s[0, tok]), row_tile(ya_scr.at[s], tok),
                                  sem.at[s]).start(priority=0)
            pltpu.make_async_copy(row_tile(yg_ref, pos_s[1, tok]), row_tile(yb_scr.at[s], tok),
                                  sem.at[s]).start(priority=1)
            return carry

        lax.fori_loop(0, TS, issue, 0, unroll=8)

    @pl.when(i == 0)
    def _():
        gather_tile(info_ref[0], 0)

    @pl.when(i + 1 < n_steps)
    def _():
        gather_tile(info_next_ref[0], 1 - slot)

    padded = jnp.concatenate([info_ref[0], jnp.zeros((V7X_LANES - V7X_SUBLANES, TS), F32)], axis=0)
    cols = jnp.transpose(padded, (1, 0))
    w1c = cols[:, 2:3]
    w2c = cols[:, 3:4]
    pltpu.make_async_copy(yg_ref.at[pl.ds(0, TS * N_LT)], ya_scr.at[slot], sem.at[slot]).wait()
    pltpu.make_async_copy(yg_ref.at[pl.ds(0, TS * N_LT)], yb_scr.at[slot], sem.at[slot]).wait()
    acc = w1c * _from_token_major(ya_scr.at[slot], TS) + w2c * _from_token_major(yb_scr.at[slot], TS)
    res = _ffn_epilogue(n_ctx_tok, x_ref, mod_ref, acc, g_ref, b_ref)
    is_dec = i * TS >= n_ctx_tok

    @pl.when(jnp.logical_not(is_dec))
    def _():
        oc_ref[...] = res

    @pl.when(is_dec)
    def _():
        od_ref[...] = res


def _combine(x, mod_l, info, yg, ln_g, ln_b, n_ctx_tok):
    n_tok = x.shape[0]
    n_tiles = n_tok // TS
    ncb = n_ctx_tok // TS
    return pl.pallas_call(
        functools.partial(_combine_kernel, n_ctx_tok),
        grid=(n_tiles,),
        in_specs=[
            pl.BlockSpec((TS, D), lambda i: (i, 0)),
            pl.BlockSpec((MOD_ROWS, N_MOD * D), lambda i: (0, 0)),
            pl.BlockSpec((1, V7X_SUBLANES, TS), lambda i: (i, 0, 0)),
            pl.BlockSpec((1, V7X_SUBLANES, TS), lambda i: (jnp.minimum(i + 1, n_tiles - 1), 0, 0)),
            pl.BlockSpec(memory_space=pl.ANY),
            pl.BlockSpec((1, D), lambda i: (0, 0)),
            pl.BlockSpec((1, D), lambda i: (0, 0)),
        ],
        out_specs=[
            pl.BlockSpec((TS, D), lambda i: (jnp.minimum(i, ncb - 1), 0)),
            pl.BlockSpec((TS, D), lambda i: (jnp.maximum(i - ncb, 0), 0)),
        ],
        out_shape=[
            jax.ShapeDtypeStruct((n_ctx_tok, D), F32),
            jax.ShapeDtypeStruct((n_tok - n_ctx_tok, D), F32),
        ],
        scratch_shapes=[
            pltpu.VMEM((2, TS * N_LT, V7X_LANES), F32),
            pltpu.VMEM((2, TS * N_LT, V7X_LANES), F32),
            pltpu.VMEM((V7X_SUBLANES, TS), jnp.int32),
            pltpu.SMEM((V7X_SUBLANES, TS), jnp.int32),
            pltpu.SemaphoreType.DMA((2,)),
            pltpu.SemaphoreType.DMA,
        ],
        compiler_params=pltpu.CompilerParams(
            dimension_semantics=("arbitrary",), vmem_limit_bytes=VMEM_LIMIT),
        name="moe_combine",
    )(x, mod_l, info, info, yg, ln_g, ln_b)


def _routing_tables(counts):
    totals = jnp.sum(counts, axis=0)
    n_rt = (totals + TR - 1) // TR
    cum = jnp.cumsum(n_rt)
    first = cum - n_rt
    n_act = cum[-1]
    off = first[None, :] * TR + (jnp.cumsum(counts, axis=0) - counts)
    fill = jnp.concatenate([first * TR + totals, n_rt * TR - totals, n_act[None]])
    r = jnp.arange(MAX_RT, dtype=jnp.int32)
    rc = jnp.minimum(r, n_act - 1)
    exp = jnp.sum((rc[:, None] >= cum[None, :]).astype(jnp.int32), axis=1)
    new = jnp.logical_and(r == first[exp], r < n_act)
    i32 = lambda a: a.astype(jnp.int32)
    return i32(off.reshape(-1)), i32(fill), (i32(exp), i32(new), i32(n_act.reshape(1)))


def _moe_ffn(x, mod_l, router_w, router_b, w1, w3, w2, ln_g, ln_b, n_ctx_tok):
    n_tok = x.shape[0]
    assert 2 * n_tok // TR + N_EXP == MAX_RT and N_FC == 2
    wrt = router_w.T
    brt = jnp.broadcast_to(router_b.reshape(N_EXP, 1), (N_EXP, V7X_LANES))
    rinfo, counts, utm = _rank(x, mod_l, wrt, brt, n_ctx_tok)
    off, fill, tables = _routing_tables(counts)
    xg, info = _dispatch(utm, rinfo, off, fill)
    y_lo = _expert_pass(False, tables, xg, w1, w3, w2)
    y = _expert_pass(True, tables, xg, w1, w3, w2, y_lo)
    return _combine(x, mod_l, info, y, ln_g, ln_b, n_ctx_tok)


def _cast_kernel(*refs):
    o_ref = refs[-1]
    off = 0
    for x_ref in refs[:-1]:
        width = x_ref.shape[-1]
        o_ref[..., off:off + width] = x_ref[...].reshape(o_ref.shape[:-1] + (width,)).astype(o_ref.dtype)
        off += width


def _pack_w13(w1, w3):
    return pl.pallas_call(
        _cast_kernel,
        grid=(N_FC,),
        in_specs=[pl.BlockSpec((D, FC), lambda f: (0, f)), pl.BlockSpec((D, FC), lambda f: (0, f))],
        out_specs=pl.BlockSpec((1, D, 2 * FC), lambda f: (f, 0, 0)),
        out_shape=jax.ShapeDtypeStruct((N_FC, D, 2 * FC), BF16),
        compiler_params=pltpu.CompilerParams(vmem_limit_bytes=VMEM_LIMIT),
        name="pack_w13",
    )(w1, w3)


def _pack_w_in(w_in):
    depth = w_in.shape[0]
    def regroup_kernel(x_ref, o_ref):
        for j in range(N_CH):
            o_ref[0, j] = x_ref[0, :, j * CW:(j + 1) * CW].astype(BF16)

    w5 = pl.pallas_call(
        regroup_kernel,
        grid=(depth, 5),
        in_specs=[pl.BlockSpec((1, D, D), lambda l, g: (l, 0, g))],
        out_specs=pl.BlockSpec((1, N_CH, D, CW), lambda l, g: (l, 0, 0, g)),
        out_shape=jax.ShapeDtypeStruct((depth, N_CH, D, 5 * CW), BF16),
        compiler_params=pltpu.CompilerParams(vmem_limit_bytes=VMEM_LIMIT),
        name="pack_w5",
    )(w_in)
    wg = pl.pallas_call(
        _cast_kernel,
        grid=(depth, 2),
        in_specs=[pl.BlockSpec((1, D, D), lambda l, k: (l, 0, 5 + k))],
        out_specs=pl.BlockSpec((1, D, D), lambda l, k: (l, 0, k)),
        out_shape=jax.ShapeDtypeStruct((depth, D, 2 * D), BF16),
        compiler_params=pltpu.CompilerParams(vmem_limit_bytes=VMEM_LIMIT),
        name="pack_wg",
    )(w_in)
    return w5, wg


def _block_diag_chunks(w):
    tiled = jnp.tile(w.reshape(N_CH, CW, HEAD_D), (1, 1, CW // HEAD_D))
    blk = jnp.arange(CW) // HEAD_D
    return jnp.where(blk[:, None] == blk[None, :], tiled, 0.0)


def kernel(x_prompt, x_sample, state_rglru, c, c_ctx, w_mod, b_mod, w_in, conv_a, w_a_out, conv_b, conv_b_bias, w_gate_a, b_gate_a, w_gate_x, b_gate_x, lru_lambda, w_b_out, w_o, ln1_g, ln1_b, ln2_g, ln2_b, ffn_w1, ffn_w3, ffn_w2, router_w, router_b, moe_w1, moe_w3, moe_w2):
    batch, seq, d = x_prompt.shape
    dec_batch, dec_seq, _ = x_sample.shape
    depth = w_mod.shape[0]
    assert (d, seq, dec_seq, depth) == (D, SEQ, DEC_SEQ, DEPTH)
    n_ctx_tok = batch * seq
    n_dec_tok = dec_batch * dec_seq
    assert n_ctx_tok % TM1 == 0 and TM1 == dec_seq and 1 + dec_batch <= MOD_ROWS
    n_ctx_tiles = n_ctx_tok // TM1
    n_tiles = n_ctx_tiles + dec_batch
    seq_per_tile = TM1 // seq
    assert N_SEG == 2 * seq_per_tile

    assert depth % 2 == 0
    n_tok = n_ctx_tok + n_dec_tok
    x = (x_prompt.reshape(n_ctx_tok, D), x_sample.reshape(n_dec_tok, D))

    cond = jnp.zeros((MOD_ROWS, D), F32).at[0].set(c_ctx).at[1:1 + dec_batch].set(c)
    mod = _modulation(cond, w_mod, b_mod)

    w5, wg = _pack_w_in(w_in)
    states = []
    for l in range(depth):
        wbd = jnp.concatenate(
            [_block_diag_chunks(w_gate_a[l, 0]), _block_diag_chunks(w_gate_x[l, 0]),
             _block_diag_chunks(w_gate_a[l, 1]), _block_diag_chunks(w_gate_x[l, 1])],
            axis=-1).astype(BF16)
        gbias = jnp.stack([b_gate_a[l, 0], b_gate_x[l, 0], b_gate_a[l, 1], b_gate_x[l, 1]], axis=0)
        h0 = jnp.zeros((n_tiles, 2, N_SEG, D), F32)
        h0 = h0.at[n_ctx_tiles:, 0, 0].set(state_rglru[:, l, 0].astype(F32))
        h0 = h0.at[n_ctx_tiles:, 1, N_SEG - 1].set(state_rglru[:, l, 1].astype(F32))

        a_pre, b_pre, st = _mixer_part1(
            x, mod[l], l, w5, conv_a[l], conv_b[l], conv_b_bias[l].reshape(1, D), wbd, gbias,
            lru_lambda[l], h0, n_tok, n_ctx_tok)
        x = _mixer_part2(
            x, mod[l], l, a_pre, b_pre, wg, w_a_out[l].astype(BF16), w_b_out[l].astype(BF16),
            w_o[l].astype(BF16), ln1_g[l].reshape(1, D), ln1_b[l].reshape(1, D), n_ctx_tok)

        if l % 2 == 0:
            k = l // 2
            x = _dense_ffn(x, mod[l], _pack_w13(ffn_w1[k], ffn_w3[k]), ffn_w2[k].astype(BF16),
                           ln2_g[l].reshape(1, D), ln2_b[l].reshape(1, D), n_ctx_tok)
        else:
            k = l // 2
            x = _moe_ffn(x, mod[l], router_w[k], router_b[k], moe_w1[k], moe_w3[k], moe_w2[k],
                         ln2_g[l].reshape(1, D), ln2_b[l].reshape(1, D), n_ctx_tok)

        st_ctx = st[:n_ctx_tiles]
        fwd = st_ctx[:, 0, 1::2].reshape(batch, D)
        bwd = st_ctx[:, 1, 0::2].reshape(batch, D)
        states.append(jnp.stack([fwd, bwd], axis=1))

    y_prompt = x[0].reshape(batch, seq, D)
    y_sample = x[1].reshape(dec_batch, dec_seq, D)
    new_state = jnp.stack(states, axis=1).astype(x_prompt.dtype)
    return (y_prompt, y_sample, new_state)
```

```python
import functools

import jax
import jax.numpy as jnp
from jax import lax
from jax.experimental import pallas as pl
from jax.experimental.pallas import tpu as pltpu

F32 = jnp.float32
BF16 = jnp.bfloat16

D = 1024
SEQ = 256
DEC_SEQ = 1024
GRID_W = 64
N_HEAD = 16
HEAD_D = D // N_HEAD
RGLRU_C = 8.0
LOG2_E = 1.4426950408889634
D_FF = 2816
N_EXP = 8
N_MOD = 6
DEPTH = 2
DN_ALPHA = (2.0 * DEPTH) ** 0.25

V7X_SUBLANES = 8
V7X_LANES = 128
V7X_VMEM_BYTES = 64 * 1024 * 1024
VMEM_LIMIT = V7X_VMEM_BYTES - 12 * 1024 * 1024

TM1 = 1024
CW = 256
N_CH = D // CW
N_LC = CW // V7X_LANES
N_SEG = V7X_SUBLANES
SEG = TM1 // N_SEG
SEG_STRIDE = SEG + 4
CONV_GAP = V7X_SUBLANES
TM2 = 512
TM3 = 512
FC = 1408
N_FC = D_FF // FC
TS = 512
TR = 256
MAX_RT = 2 * 10240 // TR + N_EXP
EXPERT_VMEM_LIMIT = V7X_VMEM_BYTES - 6 * 1024 * 1024
MOD_ROWS = 8
MOD_BLK = 3072


def _sigmoid(x):
    return 0.5 * jnp.tanh(0.5 * x) + 0.5


def _ln_plain(x, eps):
    mu = jnp.mean(x, axis=-1, keepdims=True)
    xc = x - mu
    var = jnp.mean(xc * xc, axis=-1, keepdims=True)
    return xc * lax.rsqrt(var + eps)


def _token_source(xs, tile, n_ctx_tok):
    ncb = n_ctx_tok // tile
    if isinstance(xs, tuple):
        x_ctx, x_dec = xs
        dec_off = 0
    else:
        x_ctx = x_dec = xs
        dec_off = ncb
    ctx_map = lambda i: (jnp.minimum(i, ncb - 1), 0)
    dec_map = lambda i: (jnp.maximum(i - ncb, 0) + dec_off, 0)
    return x_ctx, x_dec, ctx_map, dec_map


def _mod_row(tok0, n_ctx_tok):
    dec = jnp.maximum(tok0 - n_ctx_tok, 0) // DEC_SEQ
    return jnp.where(tok0 >= n_ctx_tok, 1 + dec, 0)


def _mod_kernel(cond_ref, w_ref, b_ref, o_ref):
    cnd = cond_ref[...]
    s = cnd * _sigmoid(cnd)
    o_ref[0] = jnp.dot(s, w_ref[0], preferred_element_type=F32,
                       precision=lax.Precision.HIGHEST) + b_ref[0]


def _modulation(cond, w_mod, b_mod):
    depth = w_mod.shape[0]
    n_out = w_mod.shape[2]
    return pl.pallas_call(
        _mod_kernel,
        grid=(depth, n_out // MOD_BLK),
        in_specs=[
            pl.BlockSpec((MOD_ROWS, D), lambda l, j: (0, 0)),
            pl.BlockSpec((1, D, MOD_BLK), lambda l, j: (l, 0, j)),
            pl.BlockSpec((1, 1, MOD_BLK), lambda l, j: (l, 0, j)),
        ],
        out_specs=pl.BlockSpec((1, MOD_ROWS, MOD_BLK), lambda l, j: (l, 0, j)),
        out_shape=jax.ShapeDtypeStruct((depth, MOD_ROWS, n_out), F32),
        compiler_params=pltpu.CompilerParams(
            dimension_semantics=("arbitrary", "arbitrary"), vmem_limit_bytes=VMEM_LIMIT),
        name="modulation",
    )(cond, w_mod, b_mod.reshape(depth, 1, n_out))


def _scan_dir(a_scr, b_scr, hl_scr, ac_scr, h0, keep, reverse):
    n_lc = a_scr.shape[0]

    def body(k, carry):
        kk = SEG - 1 - k if reverse else k
        idx = pl.ds(kk, N_SEG, stride=SEG_STRIDE)
        new = []
        for c in range(n_lc):
            h, acc = carry[c]
            a_k = a_scr[c, idx, :]
            h = a_k * h + b_scr[c, idx, :]
            acc = a_k * acc
            hl_scr[c, idx, :] = h
            ac_scr[c, idx, :] = acc
            new.append((h, acc))
        return tuple(new)

    init = tuple((jnp.zeros((N_SEG, V7X_LANES), F32), jnp.ones((N_SEG, V7X_LANES), F32))
                 for _ in range(n_lc))
    fin = lax.fori_loop(0, SEG, body, init, unroll=8)
    h_loc = jnp.concatenate([fin[c][0] for c in range(n_lc)], axis=1)
    a_tot = jnp.concatenate([fin[c][1] for c in range(n_lc)], axis=1)

    order = range(N_SEG - 1, -1, -1) if reverse else range(N_SEG)
    h_in = [None] * N_SEG
    prev = None
    for s in order:
        cur = h0[s:s + 1]
        if prev is not None:
            left = a_tot[prev:prev + 1] * h_in[prev] + h_loc[prev:prev + 1]
            cur = keep[s] * left + cur
        h_in[s] = cur
        prev = s
    h_in = jnp.concatenate(h_in, axis=0)
    return h_in, a_tot * h_in + h_loc


def _conv_stage(is_dec, proj_scr, ca_ref, cb_ref, cbias_ref, apre_ref, xr_scr, gap_scr):
    win = GRID_W
    n_win = TM1 // win
    stride = win + CONV_GAP
    ca = ca_ref[...]
    cb = cb_ref[...]
    bias = cbias_ref[...]
    pieces = [(w, c) for w in range(n_win) for c in range(N_LC)]
    sub = lax.broadcasted_iota(jnp.int32, (CONV_GAP, V7X_LANES), 0)
    joined = jnp.where(is_dec, jnp.float32(0.0), jnp.float32(1.0))
    zero_gap = jnp.zeros((CONV_GAP, V7X_LANES), F32)

    def stage(value_of):
        for c in range(N_LC):
            for w in range(n_win + 1):
                edge = w * win
                if edge % SEQ == 0:
                    gap = zero_gap
                else:
                    before = value_of(slice(edge - CONV_GAP, edge), c)
                    after = value_of(slice(edge, edge + CONV_GAP), c)
                    gap = joined * jnp.where(sub >= CONV_GAP - 2, before, jnp.where(sub == 0, after, 0.0))
                gap_scr[c, w * stride:w * stride + CONV_GAP, :] = gap
        for w, c in pieces:
            lo = CONV_GAP + w * stride
            gap_scr[c, lo:lo + win, :] = value_of(slice(w * win, (w + 1) * win), c)

    def tap(w, c, shift):
        lo = CONV_GAP + w * stride + shift
        return gap_scr[c, lo:lo + win, :]

    def chunk_cols(k, c):
        return slice(k * CW + c * V7X_LANES, k * CW + (c + 1) * V7X_LANES)

    stage(lambda rows, c: proj_scr[rows, chunk_cols(2, c)] * proj_scr[rows, chunk_cols(0, c)])
    for w, c in pieces:
        rows = slice(w * win, (w + 1) * win)
        lanes = slice(c * V7X_LANES, (c + 1) * V7X_LANES)
        conv = ca[0:1, lanes] * tap(w, c, -1) + ca[1:2, lanes] * tap(w, c, 0) + ca[2:3, lanes] * tap(w, c, 1)
        apre_ref[rows, lanes] = (proj_scr[rows, chunk_cols(1, c)] * conv).astype(BF16)
    stage(lambda rows, c: proj_scr[rows, chunk_cols(4, c)])
    for w, c in pieces:
        rows = slice(w * win, (w + 1) * win)
        lanes = slice(c * V7X_LANES, (c + 1) * V7X_LANES)
        xr_scr[rows, lanes] = (cb[0:1, lanes] * tap(w, c, -2) + cb[1:2, lanes] * tap(w, c, -1)
                               + cb[2:3, lanes] * tap(w, c, 0) + cb[3:4, lanes] * tap(w, c, 1)
                               + bias[:, lanes])


def _mix1_kernel(n_ctx_tiles, xc_ref, xd_ref, mod_ref, w5_ref, ca_ref, cb_ref, cbias_ref, wbd_ref, gb_ref,
                 lam_ref, h0_ref, apre_ref, bpre_ref, st_ref,
                 u_scr, proj_scr, a_scr, b_scr, hl_scr, ac_scr, hsum_scr, xr_scr, gap_scr):
    i = pl.program_id(0)
    j = pl.program_id(1)
    is_dec = i >= n_ctx_tiles

    @pl.when(j == 0)
    def _():
        row = jnp.where(is_dec, i - (n_ctx_tiles - 1), 0)
        m = mod_ref[pl.ds(row, 1), :]
        sh1 = m[:, 0:D]
        sc1 = m[:, D:2 * D]
        x = jnp.where(is_dec, xd_ref[...], xc_ref[...])
        u_scr[...] = (_ln_plain(x, 1e-6) * (1.0 + sc1) + sh1).astype(BF16)

    proj_scr[...] = jnp.dot(u_scr[...], w5_ref[0, 0], preferred_element_type=F32)

    _conv_stage(is_dec, proj_scr, ca_ref, cb_ref, cbias_ref, apre_ref, xr_scr, gap_scr)

    xr = xr_scr[...]
    gates = jnp.dot(xr.astype(BF16), wbd_ref[0], preferred_element_type=F32)
    gb = gb_ref[...]
    lam = lam_ref[...]
    sp = jnp.maximum(-lam, 0.0) + jnp.log1p(jnp.exp(-jnp.abs(lam)))
    rate = (-RGLRU_C * LOG2_E) * sp

    one = jnp.float32(1.0)
    for d in range(2):
        ga = gates[:, (2 * d) * CW:(2 * d + 1) * CW] + gb[2 * d:2 * d + 1]
        gx = gates[:, (2 * d + 1) * CW:(2 * d + 2) * CW] + gb[2 * d + 1:2 * d + 2]
        r = _sigmoid(ga)
        ig = _sigmoid(gx)
        a = jnp.exp2(r * rate[d:d + 1])
        y = 1.0 - a * a
        bt = jnp.where(y > 0.0, y * lax.rsqrt(y), 0.0) * (ig * xr)
        for s in range(N_SEG):
            lo = s * SEG_STRIDE
            for c in range(N_LC):
                lanes = slice(c * V7X_LANES, (c + 1) * V7X_LANES)
                a_scr[c, lo:lo + SEG, :] = a[s * SEG:(s + 1) * SEG, lanes]
                b_scr[c, lo:lo + SEG, :] = bt[s * SEG:(s + 1) * SEG, lanes]
        if d == 0:
            keep = [jnp.where(is_dec, one, jnp.float32(s % 2 == 1)) for s in range(N_SEG)]
        else:
            keep = [jnp.where(is_dec, one, jnp.float32(s % 2 == 0)) for s in range(N_SEG)]
        h_in, h_out = _scan_dir(a_scr, b_scr, hl_scr, ac_scr, h0_ref[0, d], keep, reverse=(d == 1))
        st_ref[0, d] = h_out
        for s in range(N_SEG):
            lo = s * SEG_STRIDE
            for c in range(N_LC):
                lanes = slice(c * V7X_LANES, (c + 1) * V7X_LANES)
                h_seg = hl_scr[c, lo:lo + SEG, :] + ac_scr[c, lo:lo + SEG, :] * h_in[s:s + 1, lanes]
                if d == 0:
                    hsum_scr[s * SEG:(s + 1) * SEG, lanes] = h_seg
                else:
                    hsum_scr[s * SEG:(s + 1) * SEG, lanes] += h_seg

    bpre_ref[...] = (hsum_scr[...] * jax.nn.gelu(proj_scr[:, 3 * CW:4 * CW])).astype(BF16)


def _mixer_part1(xs, mod_l, layer, w5, conv_a, conv_b, conv_b_bias, wbd, gbias, lam, h0, n_tok, n_ctx_tok):
    n_tiles = n_tok // TM1
    x_ctx, x_dec, ctx_map, dec_map = _token_source(xs, TM1, n_ctx_tok)
    kern = functools.partial(_mix1_kernel, n_ctx_tok // TM1)
    return pl.pallas_call(
        kern,
        grid=(n_tiles, N_CH),
        in_specs=[
            pl.BlockSpec((TM1, D), lambda i, j: ctx_map(i)),
            pl.BlockSpec((TM1, D), lambda i, j: dec_map(i)),
            pl.BlockSpec((MOD_ROWS, N_MOD * D), lambda i, j: (0, 0)),
            pl.BlockSpec((1, 1, D, 5 * CW), lambda i, j: (layer, j, 0, 0)),
            pl.BlockSpec((3, CW), lambda i, j: (0, j)),
            pl.BlockSpec((4, CW), lambda i, j: (0, j)),
            pl.BlockSpec((1, CW), lambda i, j: (0, j)),
            pl.BlockSpec((1, CW, 4 * CW), lambda i, j: (j, 0, 0)),
            pl.BlockSpec((4, CW), lambda i, j: (0, j)),
            pl.BlockSpec((2, CW), lambda i, j: (0, j)),
            pl.BlockSpec((1, 2, N_SEG, CW), lambda i, j: (i, 0, 0, j)),
        ],
        out_specs=[
            pl.BlockSpec((TM1, CW), lambda i, j: (i, j)),
            pl.BlockSpec((TM1, CW), lambda i, j: (i, j)),
            pl.BlockSpec((1, 2, N_SEG, CW), lambda i, j: (i, 0, 0, j)),
        ],
        out_shape=[
            jax.ShapeDtypeStruct((n_tok, D), BF16),
            jax.ShapeDtypeStruct((n_tok, D), BF16),
            jax.ShapeDtypeStruct((n_tiles, 2, N_SEG, D), F32),
        ],
        scratch_shapes=[
            pltpu.VMEM((TM1, D), BF16),
            pltpu.VMEM((TM1, 5 * CW), F32),
            pltpu.VMEM((N_LC, N_SEG * SEG_STRIDE, V7X_LANES), F32),
            pltpu.VMEM((N_LC, N_SEG * SEG_STRIDE, V7X_LANES), F32),
            pltpu.VMEM((N_LC, N_SEG * SEG_STRIDE, V7X_LANES), F32),
            pltpu.VMEM((N_LC, N_SEG * SEG_STRIDE, V7X_LANES), F32),
            pltpu.VMEM((TM1, CW), F32),
            pltpu.VMEM((TM1, CW), F32),
            pltpu.VMEM((N_LC, CONV_GAP + (TM1 // GRID_W) * (GRID_W + CONV_GAP), V7X_LANES), F32),
        ],
        compiler_params=pltpu.CompilerParams(
            dimension_semantics=("arbitrary", "arbitrary"), vmem_limit_bytes=VMEM_LIMIT),
        name="mixer_scan",
    )(x_ctx, x_dec, mod_l, w5, conv_a, conv_b, conv_b_bias, wbd, gbias, lam, h0)


def _mix2_kernel(n_ctx_tok, xc_ref, xd_ref, mod_ref, ap_ref, bp_ref, wg_ref, wa_ref, wb_ref, wo_ref,
                 g_ref, b_ref, o_ref):
    i = pl.program_id(0)
    m = mod_ref[pl.ds(_mod_row(i * TM2, n_ctx_tok), 1), :]
    sh1 = m[:, 0:D]
    sc1 = m[:, D:2 * D]
    g1 = m[:, 2 * D:3 * D]
    x = jnp.where(i * TM2 >= n_ctx_tok, xd_ref[...], xc_ref[...])
    u = (_ln_plain(x, 1e-6) * (1.0 + sc1) + sh1).astype(BF16)
    gates = jnp.dot(u, wg_ref[0], preferred_element_type=F32)
    br_a = jnp.dot(ap_ref[...], wa_ref[...], preferred_element_type=F32)
    br_b = jnp.dot(bp_ref[...], wb_ref[...], preferred_element_type=F32)
    merged = _sigmoid(gates[:, 0:D]) * br_a + _sigmoid(gates[:, D:2 * D]) * br_b
    mix = jnp.dot(merged.astype(BF16), wo_ref[...], preferred_element_type=F32)
    y = DN_ALPHA * x + g1 * mix
    o_ref[...] = _ln_plain(y, 1e-5) * g_ref[...] + b_ref[...]


def _mixer_part2(xs, mod_l, layer, a_pre, b_pre, wg, wa, wb, wo, ln_g, ln_b, n_ctx_tok):
    n_tok = a_pre.shape[0]
    x_ctx, x_dec, ctx_map, dec_map = _token_source(xs, TM2, n_ctx_tok)
    const = lambda i: (0, 0)
    tile = lambda i: (i, 0)
    return pl.pallas_call(
        functools.partial(_mix2_kernel, n_ctx_tok),
        grid=(n_tok // TM2,),
        in_specs=[
            pl.BlockSpec((TM2, D), ctx_map),
            pl.BlockSpec((TM2, D), dec_map),
            pl.BlockSpec((MOD_ROWS, N_MOD * D), const),
            pl.BlockSpec((TM2, D), tile),
            pl.BlockSpec((TM2, D), tile),
            pl.BlockSpec((1, D, 2 * D), lambda i: (layer, 0, 0)),
            pl.BlockSpec((D, D), const),
            pl.BlockSpec((D, D), const),
            pl.BlockSpec((D, D), const),
            pl.BlockSpec((1, D), const),
            pl.BlockSpec((1, D), const),
        ],
        out_specs=pl.BlockSpec((TM2, D), tile),
        out_shape=jax.ShapeDtypeStruct((n_tok, D), F32),
        compiler_params=pltpu.CompilerParams(
            dimension_semantics=("arbitrary",), vmem_limit_bytes=VMEM_LIMIT),
        name="mixer_out",
    )(x_ctx, x_dec, mod_l, a_pre, b_pre, wg, wa, wb, wo, ln_g, ln_b)


def _ffn_prologue(n_ctx_tok, x_ref, mod_ref):
    m = mod_ref[pl.ds(_mod_row(pl.program_id(0) * TM3, n_ctx_tok), 1), :]
    sh2 = m[:, 3 * D:4 * D]
    sc2 = m[:, 4 * D:5 * D]
    return _ln_plain(x_ref[...], 1e-6) * (1.0 + sc2) + sh2


def _ffn_epilogue(n_ctx_tok, x_ref, mod_ref, acc, g_ref, b_ref):
    m = mod_ref[pl.ds(_mod_row(pl.program_id(0) * TM3, n_ctx_tok), 1), :]
    g2 = m[:, 5 * D:6 * D]
    y = DN_ALPHA * x_ref[...] + g2 * acc
    return _ln_plain(y, 1e-5) * g_ref[...] + b_ref[...]


def _swiglu_chunk(u, w13, w2):
    h = jnp.dot(u, w13, preferred_element_type=F32)
    h1 = h[:, 0:FC]
    hid = (h1 * _sigmoid(h1) * h[:, FC:2 * FC]).astype(BF16)
    return jnp.dot(hid, w2, preferred_element_type=F32)


def _dense_ffn_kernel(n_ctx_tok, x_ref, mod_ref, w13_ref, w2_ref, g_ref, b_ref, o_ref, u_scr, acc_scr):
    f = pl.program_id(1)

    @pl.when(f == 0)
    def _():
        u_scr[...] = _ffn_prologue(n_ctx_tok, x_ref, mod_ref).astype(BF16)

    p = _swiglu_chunk(u_scr[...], w13_ref[0], w2_ref[...])

    @pl.when(f == 0)
    def _():
        acc_scr[...] = p

    @pl.when(f > 0)
    def _():
        acc_scr[...] += p

    @pl.when(f == N_FC - 1)
    def _():
        o_ref[...] = _ffn_epilogue(n_ctx_tok, x_ref, mod_ref, acc_scr[...], g_ref, b_ref)


def _dense_ffn(x, mod_l, w13, w2, ln_g, ln_b, n_ctx_tok):
    n_tok = x.shape[0]
    return pl.pallas_call(
        functools.partial(_dense_ffn_kernel, n_ctx_tok),
        grid=(n_tok // TM3, N_FC),
        in_specs=[
            pl.BlockSpec((TM3, D), lambda i, f: (i, 0)),
            pl.BlockSpec((MOD_ROWS, N_MOD * D), lambda i, f: (0, 0)),
            pl.BlockSpec((1, D, 2 * FC), lambda i, f: (f, 0, 0)),
            pl.BlockSpec((FC, D), lambda i, f: (f, 0)),
            pl.BlockSpec((1, D), lambda i, f: (0, 0)),
            pl.BlockSpec((1, D), lambda i, f: (0, 0)),
        ],
        out_specs=pl.BlockSpec((TM3, D), lambda i, f: (i, 0)),
        out_shape=jax.ShapeDtypeStruct((n_tok, D), F32),
        scratch_shapes=[pltpu.VMEM((TM3, D), BF16), pltpu.VMEM((TM3, D), F32)],
        compiler_params=pltpu.CompilerParams(
            dimension_semantics=("arbitrary", "arbitrary"), vmem_limit_bytes=VMEM_LIMIT),
        name="dense_ffn",
    )(x, mod_l, w13, w2, ln_g, ln_b)


N_LT = D // V7X_LANES


def _to_token_major(ref, rows):
    n = rows.shape[0]
    for c in range(N_LT):
        ref[pl.ds(c, n, stride=N_LT), :] = rows[:, c * V7X_LANES:(c + 1) * V7X_LANES]


def _from_token_major(ref, n):
    return jnp.concatenate([ref[pl.ds(c, n, stride=N_LT), :] for c in range(N_LT)], axis=1)


def _rank_kernel(n_ctx_tok, x_ref, mod_ref, wrt_ref, brt_ref, info_ref, cnt_ref, utm_ref):
    i = pl.program_id(0)
    u2 = _ffn_prologue(n_ctx_tok, x_ref, mod_ref)
    _to_token_major(utm_ref, u2)
    lg = lax.dot_general(wrt_ref[...], u2, (((1,), (1,)), ((), ())), preferred_element_type=F32,
                         precision=lax.Precision.HIGHEST) + brt_ref[:, 0:1]
    eidx = lax.broadcasted_iota(jnp.int32, lg.shape, 0).astype(F32)
    neg = jnp.float32(-jnp.inf)
    v1 = jnp.max(lg, axis=0, keepdims=True)
    i1 = jnp.min(jnp.where(lg == v1, eidx, float(N_EXP)), axis=0, keepdims=True)
    lg2 = jnp.where(eidx == i1, neg, lg)
    v2 = jnp.max(lg2, axis=0, keepdims=True)
    i2 = jnp.min(jnp.where(lg2 == v2, eidx, float(N_EXP)), axis=0, keepdims=True)
    t = jnp.exp(v2 - v1)
    w_top = 1.0 / (1.0 + t)
    m1 = eidx == i1
    m2 = eidx == i2
    member = jnp.where(jnp.logical_or(m1, m2), 1.0, 0.0)
    before = (lax.broadcasted_iota(jnp.int32, (TS, TS), 0)
              < lax.broadcasted_iota(jnp.int32, (TS, TS), 1))
    rank = jnp.dot(member.astype(BF16), jnp.where(before, 1.0, 0.0).astype(BF16),
                   preferred_element_type=F32)
    for e in range(N_EXP):
        cnt_ref[i, e] = jnp.sum(member[e:e + 1, :]).astype(jnp.int32)
    rank1 = jnp.sum(jnp.where(m1, rank, 0.0), axis=0, keepdims=True)
    rank2 = jnp.sum(jnp.where(m2, rank, 0.0), axis=0, keepdims=True)
    info_ref[0] = jnp.concatenate(
        [i1, i2, rank1, rank2, w_top, t * w_top, jnp.zeros((V7X_SUBLANES - 6, TS), F32)], axis=0)


def _rank(x, mod_l, wrt, brt, n_ctx_tok):
    n_tiles = x.shape[0] // TS
    return pl.pallas_call(
        functools.partial(_rank_kernel, n_ctx_tok),
        grid=(n_tiles,),
        in_specs=[
            pl.BlockSpec((TS, D), lambda i: (i, 0)),
            pl.BlockSpec((MOD_ROWS, N_MOD * D), lambda i: (0, 0)),
            pl.BlockSpec((N_EXP, D), lambda i: (0, 0)),
            pl.BlockSpec((N_EXP, V7X_LANES), lambda i: (0, 0)),
        ],
        out_specs=[
            pl.BlockSpec((1, V7X_SUBLANES, TS), lambda i: (i, 0, 0)),
            pl.BlockSpec(memory_space=pltpu.SMEM),
            pl.BlockSpec((TS * N_LT, V7X_LANES), lambda i: (i, 0)),
        ],
        out_shape=[
            jax.ShapeDtypeStruct((n_tiles, V7X_SUBLANES, TS), F32),
            jax.ShapeDtypeStruct((n_tiles, N_EXP), jnp.int32),
            jax.ShapeDtypeStruct((n_tiles * TS * N_LT, V7X_LANES), F32),
        ],
        compiler_params=pltpu.CompilerParams(
            dimension_semantics=("arbitrary",), vmem_limit_bytes=VMEM_LIMIT),
        name="moe_rank",
    )(x, mod_l, wrt, brt)


def _dispatch_kernel(off_ref, fill_ref, rinfo_ref, utm_ref, xg_ref, info_ref,
                     pos_v, pos_s, zero_scr, sem, psem, zsem):
    i = pl.program_id(0)
    n_steps = pl.num_programs(0)
    rinfo = rinfo_ref[0]
    i1 = rinfo[0:1]
    i2 = rinfo[1:2]
    off1 = jnp.zeros_like(i1)
    off2 = jnp.zeros_like(i2)
    for e in range(N_EXP):
        start = off_ref[i * N_EXP + e].astype(F32)
        off1 = jnp.where(i1 == float(e), start, off1)
        off2 = jnp.where(i2 == float(e), start, off2)
    info = jnp.concatenate(
        [off1 + rinfo[2:3], off2 + rinfo[3:4], rinfo[4:6], jnp.zeros((V7X_SUBLANES - 4, TS), F32)],
        axis=0)
    info_ref[0] = info
    pos_v[...] = info.astype(jnp.int32)
    cp = pltpu.make_async_copy(pos_v, pos_s, psem)
    cp.start()
    cp.wait()

    def row_tile(ref, row):
        return ref.at[pl.ds(pl.multiple_of(row * N_LT, N_LT), N_LT)]

    def row_copy(tok, dst_row):
        return pltpu.make_async_copy(row_tile(utm_ref, tok), row_tile(xg_ref, dst_row), sem)

    def issue(tok, carry):
        row_copy(tok, pos_s[0, tok]).start(priority=0)
        row_copy(tok, pos_s[1, tok]).start(priority=1)
        return carry

    lax.fori_loop(0, TS, issue, 0, unroll=8)
    for _ in range(2):
        pltpu.make_async_copy(utm_ref, xg_ref.at[pl.ds(0, TS * N_LT)], sem).wait()

    @pl.when(i == n_steps - 1)
    def _():
        zero_scr[...] = jnp.zeros_like(zero_scr)
        for e in range(N_EXP):
            end = fill_ref[e]
            n_tail = fill_ref[N_EXP + e]

            def tail_copy(k):
                return pltpu.make_async_copy(row_tile(zero_scr, 0), row_tile(xg_ref, end + k), zsem)

            def tail_start(k, carry):
                tail_copy(k).start()
                return carry

            def tail_wait(k, carry):
                tail_copy(k).wait()
                return carry

            lax.fori_loop(0, n_tail, tail_start, 0)
            lax.fori_loop(0, n_tail, tail_wait, 0)

        def tile_copy(k):
            rows = TR * N_LT
            return pltpu.make_async_copy(zero_scr, xg_ref.at[pl.ds(pl.multiple_of(k * rows, rows), rows)], zsem)

        def tile_start(k, carry):
            tile_copy(k).start()
            return carry

        def tile_wait(k, carry):
            tile_copy(k).wait()
            return carry

        lax.fori_loop(fill_ref[2 * N_EXP], MAX_RT, tile_start, 0)
        lax.fori_loop(fill_ref[2 * N_EXP], MAX_RT, tile_wait, 0)


def _dispatch(utm, rinfo, off, fill):
    n_tiles = rinfo.shape[0]
    return pl.pallas_call(
        _dispatch_kernel,
        grid_spec=pltpu.PrefetchScalarGridSpec(
            num_scalar_prefetch=2,
            grid=(n_tiles,),
            in_specs=[
                pl.BlockSpec((1, V7X_SUBLANES, TS), lambda i, off, fill: (i, 0, 0)),
                pl.BlockSpec((TS * N_LT, V7X_LANES), lambda i, off, fill: (i, 0)),
            ],
            out_specs=[
                pl.BlockSpec(memory_space=pl.ANY),
                pl.BlockSpec((1, V7X_SUBLANES, TS), lambda i, off, fill: (i, 0, 0)),
            ],
            scratch_shapes=[
                pltpu.VMEM((V7X_SUBLANES, TS), jnp.int32),
                pltpu.SMEM((V7X_SUBLANES, TS), jnp.int32),
                pltpu.VMEM((TR * N_LT, V7X_LANES), F32),
                pltpu.SemaphoreType.DMA,
                pltpu.SemaphoreType.DMA,
                pltpu.SemaphoreType.DMA,
            ],
        ),
        out_shape=[
            jax.ShapeDtypeStruct((MAX_RT * TR * N_LT, V7X_LANES), F32),
            jax.ShapeDtypeStruct((n_tiles, V7X_SUBLANES, TS), F32),
        ],
        compiler_params=pltpu.CompilerParams(
            dimension_semantics=("arbitrary",), vmem_limit_bytes=VMEM_LIMIT),
        name="moe_dispatch",
    )(off, fill, rinfo, utm)


def _expert_kernel(second, exp_ref, new_ref, nact_ref, *refs):
    if second:
        xg_ref, w1_ref, w3_ref, w2_ref, yp_ref, o_ref, w1b, w3b, w2b = refs
    else:
        xg_ref, w1_ref, w3_ref, w2_ref, o_ref, w1b, w3b, w2b = refs
    r = pl.program_id(0)
    active = r < nact_ref[0]

    @pl.when(jnp.logical_and(active, new_ref[r] == 1))
    def _():
        w1b[...] = w1_ref[0].astype(BF16)
        w3b[...] = w3_ref[0].astype(BF16)
        w2b[...] = w2_ref[0].astype(BF16)

    @pl.when(active)
    def _():
        xb = _from_token_major(xg_ref, TR).astype(BF16)
        h1 = jnp.dot(xb, w1b[...], preferred_element_type=F32)
        h3 = jnp.dot(xb, w3b[...], preferred_element_type=F32)
        hid = (h1 * _sigmoid(h1) * h3).astype(BF16)
        p = jnp.dot(hid, w2b[...], preferred_element_type=F32)
        if second:
            _to_token_major(o_ref, yp_ref[...] + p)
        else:
            o_ref[...] = p

    @pl.when(jnp.logical_not(active))
    def _():
        o_ref[...] = jnp.zeros_like(o_ref)


def _expert_pass(second, tables, xg, w1, w3, w2, yp=None):
    f = 1 if second else 0
    row = lambda r, ex, nw, na: (r, 0)
    tm_block = pl.BlockSpec((TR * N_LT, V7X_LANES), row)
    tm_shape = jax.ShapeDtypeStruct((MAX_RT * TR * N_LT, V7X_LANES), F32)
    in_specs = [
        tm_block,
        pl.BlockSpec((1, D, FC), lambda r, ex, nw, na: (ex[r], 0, f)),
        pl.BlockSpec((1, D, FC), lambda r, ex, nw, na: (ex[r], 0, f)),
        pl.BlockSpec((1, FC, D), lambda r, ex, nw, na: (ex[r], f, 0)),
    ]
    args = [xg, w1, w3, w2]
    if second:
        in_specs.append(pl.BlockSpec((TR, D), row))
        args.append(yp)
    return pl.pallas_call(
        functools.partial(_expert_kernel, second),
        grid_spec=pltpu.PrefetchScalarGridSpec(
            num_scalar_prefetch=3,
            grid=(MAX_RT,),
            in_specs=in_specs,
            out_specs=tm_block if second else pl.BlockSpec((TR, D), row),
            scratch_shapes=[pltpu.VMEM((D, FC), BF16), pltpu.VMEM((D, FC), BF16),
                            pltpu.VMEM((FC, D), BF16)],
        ),
        out_shape=tm_shape if second else jax.ShapeDtypeStruct((MAX_RT * TR, D), F32),
        compiler_params=pltpu.CompilerParams(
            dimension_semantics=("arbitrary",), vmem_limit_bytes=EXPERT_VMEM_LIMIT),
        name="moe_expert_hi" if second else "moe_expert_lo",
    )(*tables, *args)


def _combine_kernel(n_ctx_tok, x_ref, mod_ref, info_ref, info_next_ref, yg_ref, g_ref, b_ref, oc_ref, od_ref,
                    ya_scr, yb_scr, pos_v, pos_s, sem, psem):
    i = pl.program_id(0)
    n_steps = pl.num_programs(0)
    slot = i % 2

    def gather_tile(info, s):
        pos_v[...] = info.astype(jnp.int32)
        cp = pltpu.make_async_copy(pos_v, pos_s, psem)
        cp.start()
        cp.wait()

        def row_tile(ref, row):
            return ref.at[pl.ds(pl.multiple_of(row * N_LT, N_LT), N_LT)]

        def issue(tok, carry):
            pltpu.make_async_copy(row_tile(yg_ref, pos_s[0, tok]), row_tile(ya_scr.at[s], tok),
                                  sem.at[s]).start(priority=0)
            pltpu.make_async_copy(row_tile(yg_ref, pos_s[1, tok]), row_tile(yb_scr.at[s], tok),
                                  sem.at[s]).start(priority=1)
            return carry

        lax.fori_loop(0, TS, issue, 0, unroll=8)

    @pl.when(i == 0)
    def _():
        gather_tile(info_ref[0], 0)

    @pl.when(i + 1 < n_steps)
    def _():
        gather_tile(info_next_ref[0], 1 - slot)

    padded = jnp.concatenate([info_ref[0], jnp.zeros((V7X_LANES - V7X_SUBLANES, TS), F32)], axis=0)
    cols = jnp.transpose(padded, (1, 0))
    w1c = cols[:, 2:3]
    w2c = cols[:, 3:4]
    pltpu.make_async_copy(yg_ref.at[pl.ds(0, TS * N_LT)], ya_scr.at[slot], sem.at[slot]).wait()
    pltpu.make_async_copy(yg_ref.at[pl.ds(0, TS * N_LT)], yb_scr.at[slot], sem.at[slot]).wait()
    acc = w1c * _from_token_major(ya_scr.at[slot], TS) + w2c * _from_token_major(yb_scr.at[slot], TS)
    res = _ffn_epilogue(n_ctx_tok, x_ref, mod_ref, acc, g_ref, b_ref)
    is_dec = i * TS >= n_ctx_tok

    @pl.when(jnp.logical_not(is_dec))
    def _():
        oc_ref[...] = res

    @pl.when(is_dec)
    def _():
        od_ref[...] = res


def _combine(x, mod_l, info, yg, ln_g, ln_b, n_ctx_tok):
    n_tok = x.shape[0]
    n_tiles = n_tok // TS
    ncb = n_ctx_tok // TS
    return pl.pallas_call(
        functools.partial(_combine_kernel, n_ctx_tok),
        grid=(n_tiles,),
        in_specs=[
            pl.BlockSpec((TS, D), lambda i: (i, 0)),
            pl.BlockSpec((MOD_ROWS, N_MOD * D), lambda i: (0, 0)),
            pl.BlockSpec((1, V7X_SUBLANES, TS), lambda i: (i, 0, 0)),
            pl.BlockSpec((1, V7X_SUBLANES, TS), lambda i: (jnp.minimum(i + 1, n_tiles - 1), 0, 0)),
            pl.BlockSpec(memory_space=pl.ANY),
            pl.BlockSpec((1, D), lambda i: (0, 0)),
            pl.BlockSpec((1, D), lambda i: (0, 0)),
        ],
        out_specs=[
            pl.BlockSpec((TS, D), lambda i: (jnp.minimum(i, ncb - 1), 0)),
            pl.BlockSpec((TS, D), lambda i: (jnp.maximum(i - ncb, 0), 0)),
        ],
        out_shape=[
            jax.ShapeDtypeStruct((n_ctx_tok, D), F32),
            jax.ShapeDtypeStruct((n_tok - n_ctx_tok, D), F32),
        ],
        scratch_shapes=[
            pltpu.VMEM((2, TS * N_LT, V7X_LANES), F32),
            pltpu.VMEM((2, TS * N_LT, V7X_LANES), F32),
            pltpu.VMEM((V7X_SUBLANES, TS), jnp.int32),
            pltpu.SMEM((V7X_SUBLANES, TS), jnp.int32),
            pltpu.SemaphoreType.DMA((2,)),
            pltpu.SemaphoreType.DMA,
        ],
        compiler_params=pltpu.CompilerParams(
            dimension_semantics=("arbitrary",), vmem_limit_bytes=VMEM_LIMIT),
        name="moe_combine",
    )(x, mod_l, info, info, yg, ln_g, ln_b)


def _routing_tables(counts):
    totals = jnp.sum(counts, axis=0)
    n_rt = (totals + TR - 1) // TR
    cum = jnp.cumsum(n_rt)
    first = cum - n_rt
    n_act = cum[-1]
    off = first[None, :] * TR + (jnp.cumsum(counts, axis=0) - counts)
    fill = jnp.concatenate([first * TR + totals, n_rt * TR - totals, n_act[None]])
    r = jnp.arange(MAX_RT, dtype=jnp.int32)
    rc = jnp.minimum(r, n_act - 1)
    exp = jnp.sum((rc[:, None] >= cum[None, :]).astype(jnp.int32), axis=1)
    new = jnp.logical_and(r == first[exp], r < n_act)
    i32 = lambda a: a.astype(jnp.int32)
    return i32(off.reshape(-1)), i32(fill), (i32(exp), i32(new), i32(n_act.reshape(1)))


def _moe_ffn(x, mod_l, router_w, router_b, w1, w3, w2, ln_g, ln_b, n_ctx_tok):
    n_tok = x.shape[0]
    assert 2 * n_tok // TR + N_EXP == MAX_RT and N_FC == 2
    wrt = router_w.T
    brt = jnp.broadcast_to(router_b.reshape(N_EXP, 1), (N_EXP, V7X_LANES))
    rinfo, counts, utm = _rank(x, mod_l, wrt, brt, n_ctx_tok)
    off, fill, tables = _routing_tables(counts)
    xg, info = _dispatch(utm, rinfo, off, fill)
    y_lo = _expert_pass(False, tables, xg, w1, w3, w2)
    y = _expert_pass(True, tables, xg, w1, w3, w2, y_lo)
    return _combine(x, mod_l, info, y, ln_g, ln_b, n_ctx_tok)


def _cast_kernel(*refs):
    o_ref = refs[-1]
    off = 0
    for x_ref in refs[:-1]:
        width = x_ref.shape[-1]
        o_ref[..., off:off + width] = x_ref[...].reshape(o_ref.shape[:-1] + (width,)).astype(o_ref.dtype)
        off += width


def _pack_w13(w1, w3):
    return pl.pallas_call(
        _cast_kernel,
        grid=(N_FC,),
        in_specs=[pl.BlockSpec((D, FC), lambda f: (0, f)), pl.BlockSpec((D, FC), lambda f: (0, f))],
        out_specs=pl.BlockSpec((1, D, 2 * FC), lambda f: (f, 0, 0)),
        out_shape=jax.ShapeDtypeStruct((N_FC, D, 2 * FC), BF16),
        compiler_params=pltpu.CompilerParams(vmem_limit_bytes=VMEM_LIMIT),
        name="pack_w13",
    )(w1, w3)


def _pack_w_in(w_in):
    depth = w_in.shape[0]
    def regroup_kernel(x_ref, o_ref):
        for j in range(N_CH):
            o_ref[0, j] = x_ref[0, :, j * CW:(j + 1) * CW].astype(BF16)

    w5 = pl.pallas_call(
        regroup_kernel,
        grid=(depth, 5),
        in_specs=[pl.BlockSpec((1, D, D), lambda l, g: (l, 0, g))],
        out_specs=pl.BlockSpec((1, N_CH, D, CW), lambda l, g: (l, 0, 0, g)),
        out_shape=jax.ShapeDtypeStruct((depth, N_CH, D, 5 * CW), BF16),
        compiler_params=pltpu.CompilerParams(vmem_limit_bytes=VMEM_LIMIT),
        name="pack_w5",
    )(w_in)
    wg = pl.pallas_call(
        _cast_kernel,
        grid=(depth, 2),
        in_specs=[pl.BlockSpec((1, D, D), lambda l, k: (l, 0, 5 + k))],
        out_specs=pl.BlockSpec((1, D, D), lambda l, k: (l, 0, k)),
        out_shape=jax.ShapeDtypeStruct((depth, D, 2 * D), BF16),
        compiler_params=pltpu.CompilerParams(vmem_limit_bytes=VMEM_LIMIT),
        name="pack_wg",
    )(w_in)
    return w5, wg


def _block_diag_chunks(w):
    tiled = jnp.tile(w.reshape(N_CH, CW, HEAD_D), (1, 1, CW // HEAD_D))
    blk = jnp.arange(CW) // HEAD_D
    return jnp.where(blk[:, None] == blk[None, :], tiled, 0.0)


def kernel(x_prompt, x_sample, state_rglru, c, c_ctx, w_mod, b_mod, w_in, conv_a, w_a_out, conv_b, conv_b_bias, w_gate_a, b_gate_a, w_gate_x, b_gate_x, lru_lambda, w_b_out, w_o, ln1_g, ln1_b, ln2_g, ln2_b, ffn_w1, ffn_w3, ffn_w2, router_w, router_b, moe_w1, moe_w3, moe_w2):
    batch, seq, d = x_prompt.shape
    dec_batch, dec_seq, _ = x_sample.shape
    depth = w_mod.shape[0]
    assert (d, seq, dec_seq, depth) == (D, SEQ, DEC_SEQ, DEPTH)
    n_ctx_tok = batch * seq
    n_dec_tok = dec_batch * dec_seq
    assert n_ctx_tok % TM1 == 0 and TM1 == dec_seq and 1 + dec_batch <= MOD_ROWS
    n_ctx_tiles = n_ctx_tok // TM1
    n_tiles = n_ctx_tiles + dec_batch
    seq_per_tile = TM1 // seq
    assert N_SEG == 2 * seq_per_tile

    assert depth % 2 == 0
    n_tok = n_ctx_tok + n_dec_tok
    x = (x_prompt.reshape(n_ctx_tok, D), x_sample.reshape(n_dec_tok, D))

    cond = jnp.zeros((MOD_ROWS, D), F32).at[0].set(c_ctx).at[1:1 + dec_batch].set(c)
    mod = _modulation(cond, w_mod, b_mod)

    w5, wg = _pack_w_in(w_in)
    states = []
    for l in range(depth):
        wbd = jnp.concatenate(
            [_block_diag_chunks(w_gate_a[l, 0]), _block_diag_chunks(w_gate_x[l, 0]),
             _block_diag_chunks(w_gate_a[l, 1]), _block_diag_chunks(w_gate_x[l, 1])],
            axis=-1).astype(BF16)
        gbias = jnp.stack([b_gate_a[l, 0], b_gate_x[l, 0], b_gate_a[l, 1], b_gate_x[l, 1]], axis=0)
        h0 = jnp.zeros((n_tiles, 2, N_SEG, D), F32)
        h0 = h0.at[n_ctx_tiles:, 0, 0].set(state_rglru[:, l, 0].astype(F32))
        h0 = h0.at[n_ctx_tiles:, 1, N_SEG - 1].set(state_rglru[:, l, 1].astype(F32))

        a_pre, b_pre, st = _mixer_part1(
            x, mod[l], l, w5, conv_a[l], conv_b[l], conv_b_bias[l].reshape(1, D), wbd, gbias,
            lru_lambda[l], h0, n_tok, n_ctx_tok)
        x = _mixer_part2(
            x, mod[l], l, a_pre, b_pre, wg, w_a_out[l].astype(BF16), w_b_out[l].astype(BF16),
            w_o[l].astype(BF16), ln1_g[l].reshape(1, D), ln1_b[l].reshape(1, D), n_ctx_tok)

        if l % 2 == 0:
            k = l // 2
            x = _dense_ffn(x, mod[l], _pack_w13(ffn_w1[k], ffn_w3[k]), ffn_w2[k].astype(BF16),
                           ln2_g[l].reshape(1, D), ln2_b[l].reshape(1, D), n_ctx_tok)
        else:
            k = l // 2
            x = _moe_ffn(x, mod[l], router_w[k], router_b[k], moe_w1[k], moe_w3[k], moe_w2[k],
                         ln2_g[l].reshape(1, D), ln2_b[l].reshape(1, D), n_ctx_tok)

        st_ctx = st[:n_ctx_tiles]
        fwd = st_ctx[:, 0, 1::2].reshape(batch, D)
        bwd = st_ctx[:, 1, 0::2].reshape(batch, D)
        states.append(jnp.stack([fwd, bwd], axis=1))

    y_prompt = x[0].reshape(batch, seq, D)
    y_sample = x[1].reshape(dec_batch, dec_seq, D)
    new_state = jnp.stack(states, axis=1).astype(x_prompt.dtype)
    return (y_prompt, y_sample, new_state)
```

```python
import functools

import jax
import jax.numpy as jnp
from jax import lax
from jax.experimental import pallas as pl
from jax.experimental.pallas import tpu as pltpu

F32 = jnp.float32
BF16 = jnp.bfloat16

D = 1024
SEQ = 256
DEC_SEQ = 1024
GRID_W = 64
N_HEAD = 16
HEAD_D = D // N_HEAD
RGLRU_C = 8.0
LOG2_E = 1.4426950408889634
D_FF = 2816
N_EXP = 8
N_MOD = 6
DEPTH = 2
DN_ALPHA = (2.0 * DEPTH) ** 0.25

V7X_SUBLANES = 8
V7X_LANES = 128
V7X_VMEM_BYTES = 64 * 1024 * 1024
VMEM_LIMIT = V7X_VMEM_BYTES - 12 * 1024 * 1024

TM1 = 1024
CW = 256
N_CH = D // CW
N_LC = CW // V7X_LANES
N_SEG = V7X_SUBLANES
SEG = TM1 // N_SEG
SEG_STRIDE = SEG + 4
CONV_GAP = V7X_SUBLANES
TM2 = 512
TM3 = 512
FC = 1408
N_FC = D_FF // FC
TS = 512
TR = 256
SEG_PAD = V7X_SUBLANES
SEG_BITS = 7
SORT_ROWS = 2 * TS + 64
MAX_RT = (2 * 10240 + (10240 // TS) * N_EXP * (SEG_PAD - 1) + TR - 1) // TR + N_EXP
EXPERT_VMEM_LIMIT = V7X_VMEM_BYTES - 6 * 1024 * 1024
MOD_ROWS = 8
MOD_BLK = 3072


def _sigmoid(x):
    return 0.5 * jnp.tanh(0.5 * x) + 0.5


def _ln_plain(x, eps):
    mu = jnp.mean(x, axis=-1, keepdims=True)
    xc = x - mu
    var = jnp.mean(xc * xc, axis=-1, keepdims=True)
    return xc * lax.rsqrt(var + eps)


def _token_source(xs, tile, n_ctx_tok):
    ncb = n_ctx_tok // tile
    if isinstance(xs, tuple):
        x_ctx, x_dec = xs
        dec_off = 0
    else:
        x_ctx = x_dec = xs
        dec_off = ncb
    ctx_map = lambda i: (jnp.minimum(i, ncb - 1), 0)
    dec_map = lambda i: (jnp.maximum(i - ncb, 0) + dec_off, 0)
    return x_ctx, x_dec, ctx_map, dec_map


def _mod_row(tok0, n_ctx_tok):
    dec = jnp.maximum(tok0 - n_ctx_tok, 0) // DEC_SEQ
    return jnp.where(tok0 >= n_ctx_tok, 1 + dec, 0)


def _mod_kernel(cond_ref, w_ref, b_ref, o_ref):
    cnd = cond_ref[...]
    s = cnd * _sigmoid(cnd)
    o_ref[0] = jnp.dot(s, w_ref[0], preferred_element_type=F32,
                       precision=lax.Precision.HIGHEST) + b_ref[0]


def _modulation(cond, w_mod, b_mod):
    depth = w_mod.shape[0]
    n_out = w_mod.shape[2]
    return pl.pallas_call(
        _mod_kernel,
        grid=(depth, n_out // MOD_BLK),
        in_specs=[
            pl.BlockSpec((MOD_ROWS, D), lambda l, j: (0, 0)),
            pl.BlockSpec((1, D, MOD_BLK), lambda l, j: (l, 0, j)),
            pl.BlockSpec((1, 1, MOD_BLK), lambda l, j: (l, 0, j)),
        ],
        out_specs=pl.BlockSpec((1, MOD_ROWS, MOD_BLK), lambda l, j: (l, 0, j)),
        out_shape=jax.ShapeDtypeStruct((depth, MOD_ROWS, n_out), F32),
        compiler_params=pltpu.CompilerParams(
            dimension_semantics=("arbitrary", "arbitrary"), vmem_limit_bytes=VMEM_LIMIT),
        name="modulation",
    )(cond, w_mod, b_mod.reshape(depth, 1, n_out))


def _scan_dir(a_scr, b_scr, hl_scr, ac_scr, h0, keep, reverse):
    n_lc = a_scr.shape[0]

    def body(k, carry):
        kk = SEG - 1 - k if reverse else k
        idx = pl.ds(kk, N_SEG, stride=SEG_STRIDE)
        new = []
        for c in range(n_lc):
            h, acc = carry[c]
            a_k = a_scr[c, idx, :]
            h = a_k * h + b_scr[c, idx, :]
            acc = a_k * acc
            hl_scr[c, idx, :] = h
            ac_scr[c, idx, :] = acc
            new.append((h, acc))
        return tuple(new)

    init = tuple((jnp.zeros((N_SEG, V7X_LANES), F32), jnp.ones((N_SEG, V7X_LANES), F32))
                 for _ in range(n_lc))
    fin = lax.fori_loop(0, SEG, body, init, unroll=8)
    h_loc = jnp.concatenate([fin[c][0] for c in range(n_lc)], axis=1)
    a_tot = jnp.concatenate([fin[c][1] for c in range(n_lc)], axis=1)

    order = range(N_SEG - 1, -1, -1) if reverse else range(N_SEG)
    h_in = [None] * N_SEG
    prev = None
    for s in order:
        cur = h0[s:s + 1]
        if prev is not None:
            left = a_tot[prev:prev + 1] * h_in[prev] + h_loc[prev:prev + 1]
            cur = keep[s] * left + cur
        h_in[s] = cur
        prev = s
    h_in = jnp.concatenate(h_in, axis=0)
    return h_in, a_tot * h_in + h_loc


def _conv_stage(is_dec, proj_scr, ca_ref, cb_ref, cbias_ref, apre_ref, xr_scr, gap_scr):
    win = GRID_W
    n_win = TM1 // win
    stride = win + CONV_GAP
    ca = ca_ref[...]
    cb = cb_ref[...]
    bias = cbias_ref[...]
    pieces = [(w, c) for w in range(n_win) for c in range(N_LC)]
    sub = lax.broadcasted_iota(jnp.int32, (CONV_GAP, V7X_LANES), 0)
    joined = jnp.where(is_dec, jnp.float32(0.0), jnp.float32(1.0))
    zero_gap = jnp.zeros((CONV_GAP, V7X_LANES), F32)

    def stage(value_of):
        for c in range(N_LC):
            for w in range(n_win + 1):
                edge = w * win
                if edge % SEQ == 0:
                    gap = zero_gap
                else:
                    before = value_of(slice(edge - CONV_GAP, edge), c)
                    after = value_of(slice(edge, edge + CONV_GAP), c)
                    gap = joined * jnp.where(sub >= CONV_GAP - 2, before, jnp.where(sub == 0, after, 0.0))
                gap_scr[c, w * stride:w * stride + CONV_GAP, :] = gap
        for w, c in pieces:
            lo = CONV_GAP + w * stride
            gap_scr[c, lo:lo + win, :] = value_of(slice(w * win, (w + 1) * win), c)

    def tap(w, c, shift):
        lo = CONV_GAP + w * stride + shift
        return gap_scr[c, lo:lo + win, :]

    def chunk_cols(k, c):
        return slice(k * CW + c * V7X_LANES, k * CW + (c + 1) * V7X_LANES)

    stage(lambda rows, c: proj_scr[rows, chunk_cols(2, c)] * proj_scr[rows, chunk_cols(0, c)])
    for w, c in pieces:
        rows = slice(w * win, (w + 1) * win)
        lanes = slice(c * V7X_LANES, (c + 1) * V7X_LANES)
        conv = ca[0:1, lanes] * tap(w, c, -1) + ca[1:2, lanes] * tap(w, c, 0) + ca[2:3, lanes] * tap(w, c, 1)
        apre_ref[rows, lanes] = (proj_scr[rows, chunk_cols(1, c)] * conv).astype(BF16)
    stage(lambda rows, c: proj_scr[rows, chunk_cols(4, c)])
    for w, c in pieces:
        rows = slice(w * win, (w + 1) * win)
        lanes = slice(c * V7X_LANES, (c + 1) * V7X_LANES)
        xr_scr[rows, lanes] = (cb[0:1, lanes] * tap(w, c, -2) + cb[1:2, lanes] * tap(w, c, -1)
                               + cb[2:3, lanes] * tap(w, c, 0) + cb[3:4, lanes] * tap(w, c, 1)
                               + bias[:, lanes])


def _mix1_kernel(n_ctx_tiles, xc_ref, xd_ref, mod_ref, w5_ref, ca_ref, cb_ref, cbias_ref, wbd_ref, gb_ref,
                 lam_ref, h0_ref, apre_ref, bpre_ref, st_ref,
                 u_scr, proj_scr, a_scr, b_scr, hl_scr, ac_scr, hsum_scr, xr_scr, gap_scr):
    i = pl.program_id(0)
    j = pl.program_id(1)
    is_dec = i >= n_ctx_tiles

    @pl.when(j == 0)
    def _():
        row = jnp.where(is_dec, i - (n_ctx_tiles - 1), 0)
        m = mod_ref[pl.ds(row, 1), :]
        sh1 = m[:, 0:D]
        sc1 = m[:, D:2 * D]
        x = jnp.where(is_dec, xd_ref[...], xc_ref[...])
        u_scr[...] = (_ln_plain(x, 1e-6) * (1.0 + sc1) + sh1).astype(BF16)

    proj_scr[...] = jnp.dot(u_scr[...], w5_ref[0, 0], preferred_element_type=F32)

    _conv_stage(is_dec, proj_scr, ca_ref, cb_ref, cbias_ref, apre_ref, xr_scr, gap_scr)

    xr = xr_scr[...]
    gates = jnp.dot(xr.astype(BF16), wbd_ref[0], preferred_element_type=F32)
    gb = gb_ref[...]
    lam = lam_ref[...]
    sp = jnp.maximum(-lam, 0.0) + jnp.log1p(jnp.exp(-jnp.abs(lam)))
    rate = (-RGLRU_C * LOG2_E) * sp

    one = jnp.float32(1.0)
    for d in range(2):
        ga = gates[:, (2 * d) * CW:(2 * d + 1) * CW] + gb[2 * d:2 * d + 1]
        gx = gates[:, (2 * d + 1) * CW:(2 * d + 2) * CW] + gb[2 * d + 1:2 * d + 2]
        r = _sigmoid(ga)
        ig = _sigmoid(gx)
        a = jnp.exp2(r * rate[d:d + 1])
        y = 1.0 - a * a
        bt = jnp.where(y > 0.0, y * lax.rsqrt(y), 0.0) * (ig * xr)
        for s in range(N_SEG):
            lo = s * SEG_STRIDE
            for c in range(N_LC):
                lanes = slice(c * V7X_LANES, (c + 1) * V7X_LANES)
                a_scr[c, lo:lo + SEG, :] = a[s * SEG:(s + 1) * SEG, lanes]
                b_scr[c, lo:lo + SEG, :] = bt[s * SEG:(s + 1) * SEG, lanes]
        if d == 0:
            keep = [jnp.where(is_dec, one, jnp.float32(s % 2 == 1)) for s in range(N_SEG)]
        else:
            keep = [jnp.where(is_dec, one, jnp.float32(s % 2 == 0)) for s in range(N_SEG)]
        h_in, h_out = _scan_dir(a_scr, b_scr, hl_scr, ac_scr, h0_ref[0, d], keep, reverse=(d == 1))
        st_ref[0, d] = h_out
        for s in range(N_SEG):
            lo = s * SEG_STRIDE
            for c in range(N_LC):
                lanes = slice(c * V7X_LANES, (c + 1) * V7X_LANES)
                h_seg = hl_scr[c, lo:lo + SEG, :] + ac_scr[c, lo:lo + SEG, :] * h_in[s:s + 1, lanes]
                if d == 0:
                    hsum_scr[s * SEG:(s + 1) * SEG, lanes] = h_seg
                else:
                    hsum_scr[s * SEG:(s + 1) * SEG, lanes] += h_seg

    bpre_ref[...] = (hsum_scr[...] * jax.nn.gelu(proj_scr[:, 3 * CW:4 * CW])).astype(BF16)


def _mixer_part1(xs, mod_l, layer, w5, conv_a, conv_b, conv_b_bias, wbd, gbias, lam, h0, n_tok, n_ctx_tok):
    n_tiles = n_tok // TM1
    x_ctx, x_dec, ctx_map, dec_map = _token_source(xs, TM1, n_ctx_tok)
    kern = functools.partial(_mix1_kernel, n_ctx_tok // TM1)
    return pl.pallas_call(
        kern,
        grid=(n_tiles, N_CH),
        in_specs=[
            pl.BlockSpec((TM1, D), lambda i, j: ctx_map(i)),
            pl.BlockSpec((TM1, D), lambda i, j: dec_map(i)),
            pl.BlockSpec((MOD_ROWS, N_MOD * D), lambda i, j: (0, 0)),
            pl.BlockSpec((1, 1, D, 5 * CW), lambda i, j: (layer, j, 0, 0)),
            pl.BlockSpec((3, CW), lambda i, j: (0, j)),
            pl.BlockSpec((4, CW), lambda i, j: (0, j)),
            pl.BlockSpec((1, CW), lambda i, j: (0, j)),
            pl.BlockSpec((1, CW, 4 * CW), lambda i, j: (j, 0, 0)),
            pl.BlockSpec((4, CW), lambda i, j: (0, j)),
            pl.BlockSpec((2, CW), lambda i, j: (0, j)),
            pl.BlockSpec((1, 2, N_SEG, CW), lambda i, j: (i, 0, 0, j)),
        ],
        out_specs=[
            pl.BlockSpec((TM1, CW), lambda i, j: (i, j)),
            pl.BlockSpec((TM1, CW), lambda i, j: (i, j)),
            pl.BlockSpec((1, 2, N_SEG, CW), lambda i, j: (i, 0, 0, j)),
        ],
        out_shape=[
            jax.ShapeDtypeStruct((n_tok, D), BF16),
            jax.ShapeDtypeStruct((n_tok, D), BF16),
            jax.ShapeDtypeStruct((n_tiles, 2, N_SEG, D), F32),
        ],
        scratch_shapes=[
            pltpu.VMEM((TM1, D), BF16),
            pltpu.VMEM((TM1, 5 * CW), F32),
            pltpu.VMEM((N_LC, N_SEG * SEG_STRIDE, V7X_LANES), F32),
            pltpu.VMEM((N_LC, N_SEG * SEG_STRIDE, V7X_LANES), F32),
            pltpu.VMEM((N_LC, N_SEG * SEG_STRIDE, V7X_LANES), F32),
            pltpu.VMEM((N_LC, N_SEG * SEG_STRIDE, V7X_LANES), F32),
            pltpu.VMEM((TM1, CW), F32),
            pltpu.VMEM((TM1, CW), F32),
            pltpu.VMEM((N_LC, CONV_GAP + (TM1 // GRID_W) * (GRID_W + CONV_GAP), V7X_LANES), F32),
        ],
        compiler_params=pltpu.CompilerParams(
            dimension_semantics=("arbitrary", "arbitrary"), vmem_limit_bytes=VMEM_LIMIT),
        name="mixer_scan",
    )(x_ctx, x_dec, mod_l, w5, conv_a, conv_b, conv_b_bias, wbd, gbias, lam, h0)


def _mix2_kernel(n_ctx_tok, xc_ref, xd_ref, mod_ref, ap_ref, bp_ref, wg_ref, wa_ref, wb_ref, wo_ref,
                 g_ref, b_ref, o_ref):
    i = pl.program_id(0)
    m = mod_ref[pl.ds(_mod_row(i * TM2, n_ctx_tok), 1), :]
    sh1 = m[:, 0:D]
    sc1 = m[:, D:2 * D]
    g1 = m[:, 2 * D:3 * D]
    x = jnp.where(i * TM2 >= n_ctx_tok, xd_ref[...], xc_ref[...])
    u = (_ln_plain(x, 1e-6) * (1.0 + sc1) + sh1).astype(BF16)
    gates = jnp.dot(u, wg_ref[0], preferred_element_type=F32)
    br_a = jnp.dot(ap_ref[...], wa_ref[...], preferred_element_type=F32)
    br_b = jnp.dot(bp_ref[...], wb_ref[...], preferred_element_type=F32)
    merged = _sigmoid(gates[:, 0:D]) * br_a + _sigmoid(gates[:, D:2 * D]) * br_b
    mix = jnp.dot(merged.astype(BF16), wo_ref[...], preferred_element_type=F32)
    y = DN_ALPHA * x + g1 * mix
    o_ref[...] = _ln_plain(y, 1e-5) * g_ref[...] + b_ref[...]


def _mixer_part2(xs, mod_l, layer, a_pre, b_pre, wg, wa, wb, wo, ln_g, ln_b, n_ctx_tok):
    n_tok = a_pre.shape[0]
    x_ctx, x_dec, ctx_map, dec_map = _token_source(xs, TM2, n_ctx_tok)
    const = lambda i: (0, 0)
    tile = lambda i: (i, 0)
    return pl.pallas_call(
        functools.partial(_mix2_kernel, n_ctx_tok),
        grid=(n_tok // TM2,),
        in_specs=[
            pl.BlockSpec((TM2, D), ctx_map),
            pl.BlockSpec((TM2, D), dec_map),
            pl.BlockSpec((MOD_ROWS, N_MOD * D), const),
            pl.BlockSpec((TM2, D), tile),
            pl.BlockSpec((TM2, D), tile),
            pl.BlockSpec((1, D, 2 * D), lambda i: (layer, 0, 0)),
            pl.BlockSpec((D, D), const),
            pl.BlockSpec((D, D), const),
            pl.BlockSpec((D, D), const),
            pl.BlockSpec((1, D), const),
            pl.BlockSpec((1, D), const),
        ],
        out_specs=pl.BlockSpec((TM2, D), tile),
        out_shape=jax.ShapeDtypeStruct((n_tok, D), F32),
        compiler_params=pltpu.CompilerParams(
            dimension_semantics=("arbitrary",), vmem_limit_bytes=VMEM_LIMIT),
        name="mixer_out",
    )(x_ctx, x_dec, mod_l, a_pre, b_pre, wg, wa, wb, wo, ln_g, ln_b)


def _ffn_prologue(n_ctx_tok, x_ref, mod_ref):
    m = mod_ref[pl.ds(_mod_row(pl.program_id(0) * TM3, n_ctx_tok), 1), :]
    sh2 = m[:, 3 * D:4 * D]
    sc2 = m[:, 4 * D:5 * D]
    return _ln_plain(x_ref[...], 1e-6) * (1.0 + sc2) + sh2


def _ffn_epilogue(n_ctx_tok, x_ref, mod_ref, acc, g_ref, b_ref):
    m = mod_ref[pl.ds(_mod_row(pl.program_id(0) * TM3, n_ctx_tok), 1), :]
    g2 = m[:, 5 * D:6 * D]
    y = DN_ALPHA * x_ref[...] + g2 * acc
    return _ln_plain(y, 1e-5) * g_ref[...] + b_ref[...]


def _swiglu_chunk(u, w13, w2):
    h = jnp.dot(u, w13, preferred_element_type=F32)
    h1 = h[:, 0:FC]
    hid = (h1 * _sigmoid(h1) * h[:, FC:2 * FC]).astype(BF16)
    return jnp.dot(hid, w2, preferred_element_type=F32)


def _dense_ffn_kernel(n_ctx_tok, x_ref, mod_ref, w13_ref, w2_ref, g_ref, b_ref, o_ref, u_scr, acc_scr):
    f = pl.program_id(1)

    @pl.when(f == 0)
    def _():
        u_scr[...] = _ffn_prologue(n_ctx_tok, x_ref, mod_ref).astype(BF16)

    p = _swiglu_chunk(u_scr[...], w13_ref[0], w2_ref[...])

    @pl.when(f == 0)
    def _():
        acc_scr[...] = p

    @pl.when(f > 0)
    def _():
        acc_scr[...] += p

    @pl.when(f == N_FC - 1)
    def _():
        o_ref[...] = _ffn_epilogue(n_ctx_tok, x_ref, mod_ref, acc_scr[...], g_ref, b_ref)


def _dense_ffn(x, mod_l, w13, w2, ln_g, ln_b, n_ctx_tok):
    n_tok = x.shape[0]
    return pl.pallas_call(
        functools.partial(_dense_ffn_kernel, n_ctx_tok),
        grid=(n_tok // TM3, N_FC),
        in_specs=[
            pl.BlockSpec((TM3, D), lambda i, f: (i, 0)),
            pl.BlockSpec((MOD_ROWS, N_MOD * D), lambda i, f: (0, 0)),
            pl.BlockSpec((1, D, 2 * FC), lambda i, f: (f, 0, 0)),
            pl.BlockSpec((FC, D), lambda i, f: (f, 0)),
            pl.BlockSpec((1, D), lambda i, f: (0, 0)),
            pl.BlockSpec((1, D), lambda i, f: (0, 0)),
        ],
        out_specs=pl.BlockSpec((TM3, D), lambda i, f: (i, 0)),
        out_shape=jax.ShapeDtypeStruct((n_tok, D), F32),
        scratch_shapes=[pltpu.VMEM((TM3, D), BF16), pltpu.VMEM((TM3, D), F32)],
        compiler_params=pltpu.CompilerParams(
            dimension_semantics=("arbitrary", "arbitrary"), vmem_limit_bytes=VMEM_LIMIT),
        name="dense_ffn",
    )(x, mod_l, w13, w2, ln_g, ln_b)


N_LT = D // V7X_LANES


def _to_token_major(ref, rows):
    n = rows.shape[0]
    for c in range(N_LT):
        ref[pl.ds(c, n, stride=N_LT), :] = rows[:, c * V7X_LANES:(c + 1) * V7X_LANES]


def _from_token_major(ref, n):
    return jnp.concatenate([ref[pl.ds(c, n, stride=N_LT), :] for c in range(N_LT)], axis=1)


def _rank_kernel(n_ctx_tok, x_ref, mod_ref, wrt_ref, brt_ref, info_ref, cnt_ref, u2b_ref):
    i = pl.program_id(0)
    u2 = _ffn_prologue(n_ctx_tok, x_ref, mod_ref)
    u2b_ref[...] = u2.astype(BF16)
    lg = lax.dot_general(wrt_ref[...], u2, (((1,), (1,)), ((), ())), preferred_element_type=F32,
                         precision=lax.Precision.HIGHEST) + brt_ref[:, 0:1]
    eidx = lax.broadcasted_iota(jnp.int32, lg.shape, 0).astype(F32)
    neg = jnp.float32(-jnp.inf)
    v1 = jnp.max(lg, axis=0, keepdims=True)
    i1 = jnp.min(jnp.where(lg == v1, eidx, float(N_EXP)), axis=0, keepdims=True)
    lg2 = jnp.where(eidx == i1, neg, lg)
    v2 = jnp.max(lg2, axis=0, keepdims=True)
    i2 = jnp.min(jnp.where(lg2 == v2, eidx, float(N_EXP)), axis=0, keepdims=True)
    t = jnp.exp(v2 - v1)
    w_top = 1.0 / (1.0 + t)
    m1 = eidx == i1
    m2 = eidx == i2
    member = jnp.where(jnp.logical_or(m1, m2), 1.0, 0.0)
    before = (lax.broadcasted_iota(jnp.int32, (TS, TS), 0)
              < lax.broadcasted_iota(jnp.int32, (TS, TS), 1))
    rank = jnp.dot(member.astype(BF16), jnp.where(before, 1.0, 0.0).astype(BF16),
                   preferred_element_type=F32)
    for e in range(N_EXP):
        cnt_ref[i, e] = jnp.sum(member[e:e + 1, :]).astype(jnp.int32)
    rank1 = jnp.sum(jnp.where(m1, rank, 0.0), axis=0, keepdims=True)
    rank2 = jnp.sum(jnp.where(m2, rank, 0.0), axis=0, keepdims=True)
    info_ref[0] = jnp.concatenate(
        [i1, i2, rank1, rank2, w_top, t * w_top, jnp.zeros((V7X_SUBLANES - 6, TS), F32)], axis=0)


def _rank(x, mod_l, wrt, brt, n_ctx_tok):
    n_tiles = x.shape[0] // TS
    return pl.pallas_call(
        functools.partial(_rank_kernel, n_ctx_tok),
        grid=(n_tiles,),
        in_specs=[
            pl.BlockSpec((TS, D), lambda i: (i, 0)),
            pl.BlockSpec((MOD_ROWS, N_MOD * D), lambda i: (0, 0)),
            pl.BlockSpec((N_EXP, D), lambda i: (0, 0)),
            pl.BlockSpec((N_EXP, V7X_LANES), lambda i: (0, 0)),
        ],
        out_specs=[
            pl.BlockSpec((1, V7X_SUBLANES, TS), lambda i: (i, 0, 0)),
            pl.BlockSpec(memory_space=pltpu.SMEM),
            pl.BlockSpec((TS, D), lambda i: (i, 0)),
        ],
        out_shape=[
            jax.ShapeDtypeStruct((n_tiles, V7X_SUBLANES, TS), F32),
            jax.ShapeDtypeStruct((n_tiles, N_EXP), jnp.int32),
            jax.ShapeDtypeStruct((n_tiles * TS, D), BF16),
        ],
        compiler_params=pltpu.CompilerParams(
            dimension_semantics=("arbitrary",), vmem_limit_bytes=VMEM_LIMIT),
        name="moe_rank",
    )(x, mod_l, wrt, brt)


def _dispatch_kernel(off_ref, loc_ref, cnt_ref, fill_ref, rinfo_ref, u_ref, xg_ref, info_ref,
                     sorted_scr, zero_scr, sem, zsem):
    i = pl.program_id(0)
    n_steps = pl.num_programs(0)
    slot = i % 2
    rinfo = rinfo_ref[0]
    i1 = rinfo[0:1]
    i2 = rinfo[1:2]
    off1 = jnp.zeros_like(i1)
    off2 = jnp.zeros_like(i2)
    loc1 = jnp.zeros_like(i1)
    loc2 = jnp.zeros_like(i2)
    for e in range(N_EXP):
        start = off_ref[i * N_EXP + e].astype(F32)
        local = loc_ref[i * N_EXP + e].astype(F32)
        off1 = jnp.where(i1 == float(e), start, off1)
        off2 = jnp.where(i2 == float(e), start, off2)
        loc1 = jnp.where(i1 == float(e), local, loc1)
        loc2 = jnp.where(i2 == float(e), local, loc2)
    info_ref[0] = jnp.concatenate(
        [off1 + rinfo[2:3], off2 + rinfo[3:4], rinfo[4:6], jnp.zeros((V7X_SUBLANES - 4, TS), F32)],
        axis=0)
    p_idx = lax.broadcasted_iota(jnp.int32, (SORT_ROWS, TS), 0).astype(F32)
    hit = jnp.logical_or(p_idx == loc1 + rinfo[2:3], p_idx == loc2 + rinfo[3:4])
    sel = jnp.where(hit, 1.0, 0.0).astype(BF16)
    sorted_scr[slot] = jnp.dot(sel, u_ref[...], preferred_element_type=F32)

    def segment_copies(tile, s, do):
        for e in range(N_EXP):
            n = cnt_ref[tile * N_EXP + e]
            src = loc_ref[tile * N_EXP + e]
            dst = off_ref[tile * N_EXP + e]
            for k in range(SEG_BITS - 1, -1, -1):
                size = SEG_PAD << k
                take = (n & size) != 0

                @pl.when(take)
                def _():
                    do(pltpu.make_async_copy(
                        sorted_scr.at[s, pl.ds(pl.multiple_of(src, SEG_PAD), size)],
                        xg_ref.at[pl.ds(pl.multiple_of(dst, SEG_PAD), size)], sem.at[s]))

                step = jnp.where(take, size, 0)
                src = src + step
                dst = dst + step

    @pl.when(i > 0)
    def _():
        segment_copies(i - 1, 1 - slot, lambda cp: cp.wait())

    segment_copies(i, slot, lambda cp: cp.start())

    @pl.when(i == n_steps - 1)
    def _():
        segment_copies(i, slot, lambda cp: cp.wait())
        zero_scr[...] = jnp.zeros_like(zero_scr)
        for e in range(N_EXP):
            end = fill_ref[e]
            n_tail = fill_ref[N_EXP + e] // SEG_PAD

            def tail_copy(k):
                return pltpu.make_async_copy(
                    zero_scr.at[pl.ds(0, SEG_PAD)],
                    xg_ref.at[pl.ds(pl.multiple_of(end + k * SEG_PAD, SEG_PAD), SEG_PAD)], zsem)

            def tail_start(k, carry):
                tail_copy(k).start()
                return carry

            def tail_wait(k, carry):
                tail_copy(k).wait()
                return carry

            lax.fori_loop(0, n_tail, tail_start, 0)
            lax.fori_loop(0, n_tail, tail_wait, 0)

        def tile_copy(k):
            return pltpu.make_async_copy(zero_scr, xg_ref.at[pl.ds(pl.multiple_of(k * TR, TR), TR)], zsem)

        def tile_start(k, carry):
            tile_copy(k).start()
            return carry

        def tile_wait(k, carry):
            tile_copy(k).wait()
            return carry

        lax.fori_loop(fill_ref[2 * N_EXP], MAX_RT, tile_start, 0)
        lax.fori_loop(fill_ref[2 * N_EXP], MAX_RT, tile_wait, 0)


def _dispatch(u2b, rinfo, off, loc, cnt, fill):
    n_tiles = rinfo.shape[0]
    tile3 = lambda i, *_: (i, 0, 0)
    return pl.pallas_call(
        _dispatch_kernel,
        grid_spec=pltpu.PrefetchScalarGridSpec(
            num_scalar_prefetch=4,
            grid=(n_tiles,),
            in_specs=[
                pl.BlockSpec((1, V7X_SUBLANES, TS), tile3),
                pl.BlockSpec((TS, D), lambda i, *_: (i, 0)),
            ],
            out_specs=[
                pl.BlockSpec(memory_space=pl.ANY),
                pl.BlockSpec((1, V7X_SUBLANES, TS), tile3),
            ],
            scratch_shapes=[
                pltpu.VMEM((2, SORT_ROWS, D), F32),
                pltpu.VMEM((TR, D), F32),
                pltpu.SemaphoreType.DMA((2,)),
                pltpu.SemaphoreType.DMA,
            ],
        ),
        out_shape=[
            jax.ShapeDtypeStruct((MAX_RT * TR, D), F32),
            jax.ShapeDtypeStruct((n_tiles, V7X_SUBLANES, TS), F32),
        ],
        compiler_params=pltpu.CompilerParams(
            dimension_semantics=("arbitrary",), vmem_limit_bytes=VMEM_LIMIT),
        name="moe_dispatch",
    )(off, loc, cnt, fill, rinfo, u2b)


def _expert_kernel(second, exp_ref, new_ref, nact_ref, *refs):
    if second:
        xg_ref, w1_ref, w3_ref, w2_ref, yp_ref, o_ref, w1b, w3b, w2b = refs
    else:
        xg_ref, w1_ref, w3_ref, w2_ref, o_ref, w1b, w3b, w2b = refs
    r = pl.program_id(0)
    active = r < nact_ref[0]

    @pl.when(jnp.logical_and(active, new_ref[r] == 1))
    def _():
        w1b[...] = w1_ref[0].astype(BF16)
        w3b[...] = w3_ref[0].astype(BF16)
        w2b[...] = w2_ref[0].astype(BF16)

    @pl.when(active)
    def _():
        xb = xg_ref[...].astype(BF16)
        h1 = jnp.dot(xb, w1b[...], preferred_element_type=F32)
        h3 = jnp.dot(xb, w3b[...], preferred_element_type=F32)
        hid = (h1 * _sigmoid(h1) * h3).astype(BF16)
        p = jnp.dot(hid, w2b[...], preferred_element_type=F32)
        if second:
            _to_token_major(o_ref, yp_ref[...] + p)
        else:
            o_ref[...] = p

    @pl.when(jnp.logical_not(active))
    def _():
        o_ref[...] = jnp.zeros_like(o_ref)


def _expert_pass(second, tables, xg, w1, w3, w2, yp=None):
    f = 1 if second else 0
    row = lambda r, ex, nw, na: (r, 0)
    tm_block = pl.BlockSpec((TR * N_LT, V7X_LANES), row)
    tm_shape = jax.ShapeDtypeStruct((MAX_RT * TR * N_LT, V7X_LANES), F32)
    in_specs = [
        pl.BlockSpec((TR, D), row),
        pl.BlockSpec((1, D, FC), lambda r, ex, nw, na: (ex[r], 0, f)),
        pl.BlockSpec((1, D, FC), lambda r, ex, nw, na: (ex[r], 0, f)),
        pl.BlockSpec((1, FC, D), lambda r, ex, nw, na: (ex[r], f, 0)),
    ]
    args = [xg, w1, w3, w2]
    if second:
        in_specs.append(pl.BlockSpec((TR, D), row))
        args.append(yp)
    return pl.pallas_call(
        functools.partial(_expert_kernel, second),
        grid_spec=pltpu.PrefetchScalarGridSpec(
            num_scalar_prefetch=3,
            grid=(MAX_RT,),
            in_specs=in_specs,
            out_specs=tm_block if second else pl.BlockSpec((TR, D), row),
            scratch_shapes=[pltpu.VMEM((D, FC), BF16), pltpu.VMEM((D, FC), BF16),
                            pltpu.VMEM((FC, D), BF16)],
        ),
        out_shape=tm_shape if second else jax.ShapeDtypeStruct((MAX_RT * TR, D), F32),
        compiler_params=pltpu.CompilerParams(
            dimension_semantics=("arbitrary",), vmem_limit_bytes=EXPERT_VMEM_LIMIT),
        name="moe_expert_hi" if second else "moe_expert_lo",
    )(*tables, *args)


def _combine_kernel(n_ctx_tok, x_ref, mod_ref, info_ref, info_next_ref, yg_ref, g_ref, b_ref, oc_ref, od_ref,
                    ya_scr, yb_scr, pos_v, pos_s, sem, psem):
    i = pl.program_id(0)
    n_steps = pl.num_programs(0)
    slot = i % 2

    def gather_tile(info, s):
        pos_v[...] = info.astype(jnp.int32)
        cp = pltpu.make_async_copy(pos_v, pos_s, psem)
        cp.start()
        cp.wait()

        def row_tile(ref, row):
            return ref.at[pl.ds(pl.multiple_of(row * N_LT, N_LT), N_LT)]

        def issue(tok, carry):
            pltpu.make_async_copy(row_tile(yg_ref, pos_s[0, tok]), row_tile(ya_scr.at[s], tok),
                                  sem.at[s]).start(priority=0)
            pltpu.make_async_copy(row_tile(yg_ref, pos_s[1, tok]), row_tile(yb_scr.at[s], tok),
                                  sem.at[s]).start(priority=1)
            return carry

        lax.fori_loop(0, TS, issue, 0, unroll=8)

    @pl.when(i == 0)
    def _():
        gather_tile(info_ref[0], 0)

    @pl.when(i + 1 < n_steps)
    def _():
        gather_tile(info_next_ref[0], 1 - slot)

    padded = jnp.concatenate([info_ref[0], jnp.zeros((V7X_LANES - V7X_SUBLANES, TS), F32)], axis=0)
    cols = jnp.transpose(padded, (1, 0))
    w1c = cols[:, 2:3]
    w2c = cols[:, 3:4]
    pltpu.make_async_copy(yg_ref.at[pl.ds(0, TS * N_LT)], ya_scr.at[slot], sem.at[slot]).wait()
    pltpu.make_async_copy(yg_ref.at[pl.ds(0, TS * N_LT)], yb_scr.at[slot], sem.at[slot]).wait()
    acc = w1c * _from_token_major(ya_scr.at[slot], TS) + w2c * _from_token_major(yb_scr.at[slot], TS)
    res = _ffn_epilogue(n_ctx_tok, x_ref, mod_ref, acc, g_ref, b_ref)
    is_dec = i * TS >= n_ctx_tok

    @pl.when(jnp.logical_not(is_dec))
    def _():
        oc_ref[...] = res

    @pl.when(is_dec)
    def _():
        od_ref[...] = res


def _combine(x, mod_l, info, yg, ln_g, ln_b, n_ctx_tok):
    n_tok = x.shape[0]
    n_tiles = n_tok // TS
    ncb = n_ctx_tok // TS
    return pl.pallas_call(
        functools.partial(_combine_kernel, n_ctx_tok),
        grid=(n_tiles,),
        in_specs=[
            pl.BlockSpec((TS, D), lambda i: (i, 0)),
            pl.BlockSpec((MOD_ROWS, N_MOD * D), lambda i: (0, 0)),
            pl.BlockSpec((1, V7X_SUBLANES, TS), lambda i: (i, 0, 0)),
            pl.BlockSpec((1, V7X_SUBLANES, TS), lambda i: (jnp.minimum(i + 1, n_tiles - 1), 0, 0)),
            pl.BlockSpec(memory_space=pl.ANY),
            pl.BlockSpec((1, D), lambda i: (0, 0)),
            pl.BlockSpec((1, D), lambda i: (0, 0)),
        ],
        out_specs=[
            pl.BlockSpec((TS, D), lambda i: (jnp.minimum(i, ncb - 1), 0)),
            pl.BlockSpec((TS, D), lambda i: (jnp.maximum(i - ncb, 0), 0)),
        ],
        out_shape=[
            jax.ShapeDtypeStruct((n_ctx_tok, D), F32),
            jax.ShapeDtypeStruct((n_tok - n_ctx_tok, D), F32),
        ],
        scratch_shapes=[
            pltpu.VMEM((2, TS * N_LT, V7X_LANES), F32),
            pltpu.VMEM((2, TS * N_LT, V7X_LANES), F32),
            pltpu.VMEM((V7X_SUBLANES, TS), jnp.int32),
            pltpu.SMEM((V7X_SUBLANES, TS), jnp.int32),
            pltpu.SemaphoreType.DMA((2,)),
            pltpu.SemaphoreType.DMA,
        ],
        compiler_params=pltpu.CompilerParams(
            dimension_semantics=("arbitrary",), vmem_limit_bytes=VMEM_LIMIT),
        name="moe_combine",
    )(x, mod_l, info, info, yg, ln_g, ln_b)


def _routing_tables(counts):
    counts = (counts + SEG_PAD - 1) // SEG_PAD * SEG_PAD
    totals = jnp.sum(counts, axis=0)
    n_rt = (totals + TR - 1) // TR
    cum = jnp.cumsum(n_rt)
    first = cum - n_rt
    n_act = cum[-1]
    off = first[None, :] * TR + (jnp.cumsum(counts, axis=0) - counts)
    loc = jnp.cumsum(counts, axis=1) - counts
    fill = jnp.concatenate([first * TR + totals, n_rt * TR - totals, n_act[None]])
    r = jnp.arange(MAX_RT, dtype=jnp.int32)
    rc = jnp.minimum(r, n_act - 1)
    exp = jnp.sum((rc[:, None] >= cum[None, :]).astype(jnp.int32), axis=1)
    new = jnp.logical_and(r == first[exp], r < n_act)
    i32 = lambda a: a.astype(jnp.int32)
    segs = (i32(off.reshape(-1)), i32(loc.reshape(-1)), i32(counts.reshape(-1)))
    return segs, i32(fill), (i32(exp), i32(new), i32(n_act.reshape(1)))


def _moe_ffn(x, mod_l, router_w, router_b, w1, w3, w2, ln_g, ln_b, n_ctx_tok):
    n_tok = x.shape[0]
    n_seg_pad = (n_tok // TS) * N_EXP * (SEG_PAD - 1)
    assert (2 * n_tok + n_seg_pad + TR - 1) // TR + N_EXP == MAX_RT and N_FC == 2
    wrt = router_w.T
    brt = jnp.broadcast_to(router_b.reshape(N_EXP, 1), (N_EXP, V7X_LANES))
    rinfo, counts, u2b = _rank(x, mod_l, wrt, brt, n_ctx_tok)
    (off, loc, cnt), fill, tables = _routing_tables(counts)
    xg, info = _dispatch(u2b, rinfo, off, loc, cnt, fill)
    y_lo = _expert_pass(False, tables, xg, w1, w3, w2)
    y = _expert_pass(True, tables, xg, w1, w3, w2, y_lo)
    return _combine(x, mod_l, info, y, ln_g, ln_b, n_ctx_tok)


def _cast_kernel(*refs):
    o_ref = refs[-1]
    off = 0
    for x_ref in refs[:-1]:
        width = x_ref.shape[-1]
        o_ref[..., off:off + width] = x_ref[...].reshape(o_ref.shape[:-1] + (width,)).astype(o_ref.dtype)
        off += width


def _pack_w13(w1, w3):
    return pl.pallas_call(
        _cast_kernel,
        grid=(N_FC,),
        in_specs=[pl.BlockSpec((D, FC), lambda f: (0, f)), pl.BlockSpec((D, FC), lambda f: (0, f))],
        out_specs=pl.BlockSpec((1, D, 2 * FC), lambda f: (f, 0, 0)),
        out_shape=jax.ShapeDtypeStruct((N_FC, D, 2 * FC), BF16),
        compiler_params=pltpu.CompilerParams(vmem_limit_bytes=VMEM_LIMIT),
        name="pack_w13",
    )(w1, w3)


def _pack_w_in(w_in):
    depth = w_in.shape[0]
    def regroup_kernel(x_ref, o_ref):
        for j in range(N_CH):
            o_ref[0, j] = x_ref[0, :, j * CW:(j + 1) * CW].astype(BF16)

    w5 = pl.pallas_call(
        regroup_kernel,
        grid=(depth, 5),
        in_specs=[pl.BlockSpec((1, D, D), lambda l, g: (l, 0, g))],
        out_specs=pl.BlockSpec((1, N_CH, D, CW), lambda l, g: (l, 0, 0, g)),
        out_shape=jax.ShapeDtypeStruct((depth, N_CH, D, 5 * CW), BF16),
        compiler_params=pltpu.CompilerParams(vmem_limit_bytes=VMEM_LIMIT),
        name="pack_w5",
    )(w_in)
    wg = pl.pallas_call(
        _cast_kernel,
        grid=(depth, 2),
        in_specs=[pl.BlockSpec((1, D, D), lambda l, k: (l, 0, 5 + k))],
        out_specs=pl.BlockSpec((1, D, D), lambda l, k: (l, 0, k)),
        out_shape=jax.ShapeDtypeStruct((depth, D, 2 * D), BF16),
        compiler_params=pltpu.CompilerParams(vmem_limit_bytes=VMEM_LIMIT),
        name="pack_wg",
    )(w_in)
    return w5, wg


def _block_diag_chunks(w):
    tiled = jnp.tile(w.reshape(N_CH, CW, HEAD_D), (1, 1, CW // HEAD_D))
    blk = jnp.arange(CW) // HEAD_D
    return jnp.where(blk[:, None] == blk[None, :], tiled, 0.0)


def kernel(x_prompt, x_sample, state_rglru, c, c_ctx, w_mod, b_mod, w_in, conv_a, w_a_out, conv_b, conv_b_bias, w_gate_a, b_gate_a, w_gate_x, b_gate_x, lru_lambda, w_b_out, w_o, ln1_g, ln1_b, ln2_g, ln2_b, ffn_w1, ffn_w3, ffn_w2, router_w, router_b, moe_w1, moe_w3, moe_w2):
    batch, seq, d = x_prompt.shape
    dec_batch, dec_seq, _ = x_sample.shape
    depth = w_mod.shape[0]
    assert (d, seq, dec_seq, depth) == (D, SEQ, DEC_SEQ, DEPTH)
    n_ctx_tok = batch * seq
    n_dec_tok = dec_batch * dec_seq
    assert n_ctx_tok % TM1 == 0 and TM1 == dec_seq and 1 + dec_batch <= MOD_ROWS
    n_ctx_tiles = n_ctx_tok // TM1
    n_tiles = n_ctx_tiles + dec_batch
    seq_per_tile = TM1 // seq
    assert N_SEG == 2 * seq_per_tile

    assert depth % 2 == 0
    n_tok = n_ctx_tok + n_dec_tok
    x = (x_prompt.reshape(n_ctx_tok, D), x_sample.reshape(n_dec_tok, D))

    cond = jnp.zeros((MOD_ROWS, D), F32).at[0].set(c_ctx).at[1:1 + dec_batch].set(c)
    mod = _modulation(cond, w_mod, b_mod)

    w5, wg = _pack_w_in(w_in)
    states = []
    for l in range(depth):
        wbd = jnp.concatenate(
            [_block_diag_chunks(w_gate_a[l, 0]), _block_diag_chunks(w_gate_x[l, 0]),
             _block_diag_chunks(w_gate_a[l, 1]), _block_diag_chunks(w_gate_x[l, 1])],
            axis=-1).astype(BF16)
        gbias = jnp.stack([b_gate_a[l, 0], b_gate_x[l, 0], b_gate_a[l, 1], b_gate_x[l, 1]], axis=0)
        h0 = jnp.zeros((n_tiles, 2, N_SEG, D), F32)
        h0 = h0.at[n_ctx_tiles:, 0, 0].set(state_rglru[:, l, 0].astype(F32))
        h0 = h0.at[n_ctx_tiles:, 1, N_SEG - 1].set(state_rglru[:, l, 1].astype(F32))

        a_pre, b_pre, st = _mixer_part1(
            x, mod[l], l, w5, conv_a[l], conv_b[l], conv_b_bias[l].reshape(1, D), wbd, gbias,
            lru_lambda[l], h0, n_tok, n_ctx_tok)
        x = _mixer_part2(
            x, mod[l], l, a_pre, b_pre, wg, w_a_out[l].astype(BF16), w_b_out[l].astype(BF16),
            w_o[l].astype(BF16), ln1_g[l].reshape(1, D), ln1_b[l].reshape(1, D), n_ctx_tok)

        if l % 2 == 0:
            k = l // 2
            x = _dense_ffn(x, mod[l], _pack_w13(ffn_w1[k], ffn_w3[k]), ffn_w2[k].astype(BF16),
                           ln2_g[l].reshape(1, D), ln2_b[l].reshape(1, D), n_ctx_tok)
        else:
            k = l // 2
            x = _moe_ffn(x, mod[l], router_w[k], router_b[k], moe_w1[k], moe_w3[k], moe_w2[k],
                         ln2_g[l].reshape(1, D), ln2_b[l].reshape(1, D), n_ctx_tok)

        st_ctx = st[:n_ctx_tiles]
        fwd = st_ctx[:, 0, 1::2].reshape(batch, D)
        bwd = st_ctx[:, 1, 0::2].reshape(batch, D)
        states.append(jnp.stack([fwd, bwd], axis=1))

    y_prompt = x[0].reshape(batch, seq, D)
    y_sample = x[1].reshape(dec_batch, dec_seq, D)
    new_state = jnp.stack(states, axis=1).astype(x_prompt.dtype)
    return (y_prompt, y_sample, new_state)
```

```python
import functools

import jax
import jax.numpy as jnp
from jax import lax
from jax.experimental import pallas as pl
from jax.experimental.pallas import tpu as pltpu

F32 = jnp.float32
BF16 = jnp.bfloat16

D = 1024
SEQ = 256
DEC_SEQ = 1024
GRID_W = 64
N_HEAD = 16
HEAD_D = D // N_HEAD
RGLRU_C = 8.0
LOG2_E = 1.4426950408889634
D_FF = 2816
N_EXP = 8
N_MOD = 6
DEPTH = 2
DN_ALPHA = (2.0 * DEPTH) ** 0.25

V7X_SUBLANES = 8
V7X_LANES = 128
V7X_VMEM_BYTES = 64 * 1024 * 1024
VMEM_LIMIT = V7X_VMEM_BYTES - 12 * 1024 * 1024

TM1 = 1024
CW = 256
N_CH = D // CW
N_LC = CW // V7X_LANES
N_SEG = V7X_SUBLANES
SEG = TM1 // N_SEG
SEG_STRIDE = SEG + 4
CONV_GAP = V7X_SUBLANES
TM2 = 512
TM3 = 512
FC = 1408
N_FC = D_FF // FC
TS = 512
TR = 256
SEG_PAD = V7X_SUBLANES
SEG_BITS = 7
SORT_ROWS = 2 * TS + V7X_LANES
MAX_RT = (2 * 10240 + (10240 // TS) * N_EXP * (SEG_PAD - 1) + TR - 1) // TR + N_EXP
EXPERT_VMEM_LIMIT = V7X_VMEM_BYTES - 6 * 1024 * 1024
MOD_ROWS = 8
MOD_BLK = 3072


def _sigmoid(x):
    return 0.5 * jnp.tanh(0.5 * x) + 0.5


def _ln_plain(x, eps):
    mu = jnp.mean(x, axis=-1, keepdims=True)
    xc = x - mu
    var = jnp.mean(xc * xc, axis=-1, keepdims=True)
    return xc * lax.rsqrt(var + eps)


def _token_source(xs, tile, n_ctx_tok):
    ncb = n_ctx_tok // tile
    if isinstance(xs, tuple):
        x_ctx, x_dec = xs
        dec_off = 0
    else:
        x_ctx = x_dec = xs
        dec_off = ncb
    ctx_map = lambda i: (jnp.minimum(i, ncb - 1), 0)
    dec_map = lambda i: (jnp.maximum(i - ncb, 0) + dec_off, 0)
    return x_ctx, x_dec, ctx_map, dec_map


def _mod_row(tok0, n_ctx_tok):
    dec = jnp.maximum(tok0 - n_ctx_tok, 0) // DEC_SEQ
    return jnp.where(tok0 >= n_ctx_tok, 1 + dec, 0)


def _mod_kernel(cond_ref, w_ref, b_ref, o_ref):
    cnd = cond_ref[...]
    s = cnd * _sigmoid(cnd)
    o_ref[0] = jnp.dot(s, w_ref[0], preferred_element_type=F32,
                       precision=lax.Precision.HIGHEST) + b_ref[0]


def _modulation(cond, w_mod, b_mod):
    depth = w_mod.shape[0]
    n_out = w_mod.shape[2]
    return pl.pallas_call(
        _mod_kernel,
        grid=(depth, n_out // MOD_BLK),
        in_specs=[
            pl.BlockSpec((MOD_ROWS, D), lambda l, j: (0, 0)),
            pl.BlockSpec((1, D, MOD_BLK), lambda l, j: (l, 0, j)),
            pl.BlockSpec((1, 1, MOD_BLK), lambda l, j: (l, 0, j)),
        ],
        out_specs=pl.BlockSpec((1, MOD_ROWS, MOD_BLK), lambda l, j: (l, 0, j)),
        out_shape=jax.ShapeDtypeStruct((depth, MOD_ROWS, n_out), F32),
        compiler_params=pltpu.CompilerParams(
            dimension_semantics=("arbitrary", "arbitrary"), vmem_limit_bytes=VMEM_LIMIT),
        name="modulation",
    )(cond, w_mod, b_mod.reshape(depth, 1, n_out))


def _scan_dir(a_scr, b_scr, hl_scr, ac_scr, h0, keep, reverse):
    n_lc = a_scr.shape[0]

    def body(k, carry):
        kk = SEG - 1 - k if reverse else k
        idx = pl.ds(kk, N_SEG, stride=SEG_STRIDE)
        new = []
        for c in range(n_lc):
            h, acc = carry[c]
            a_k = a_scr[c, idx, :]
            h = a_k * h + b_scr[c, idx, :]
            acc = a_k * acc
            hl_scr[c, idx, :] = h
            ac_scr[c, idx, :] = acc
            new.append((h, acc))
        return tuple(new)

    init = tuple((jnp.zeros((N_SEG, V7X_LANES), F32), jnp.ones((N_SEG, V7X_LANES), F32))
                 for _ in range(n_lc))
    fin = lax.fori_loop(0, SEG, body, init, unroll=8)
    h_loc = jnp.concatenate([fin[c][0] for c in range(n_lc)], axis=1)
    a_tot = jnp.concatenate([fin[c][1] for c in range(n_lc)], axis=1)

    order = range(N_SEG - 1, -1, -1) if reverse else range(N_SEG)
    h_in = [None] * N_SEG
    prev = None
    for s in order:
        cur = h0[s:s + 1]
        if prev is not None:
            left = a_tot[prev:prev + 1] * h_in[prev] + h_loc[prev:prev + 1]
            cur = keep[s] * left + cur
        h_in[s] = cur
        prev = s
    h_in = jnp.concatenate(h_in, axis=0)
    return h_in, a_tot * h_in + h_loc


def _conv_stage(is_dec, proj_scr, ca_ref, cb_ref, cbias_ref, apre_ref, xr_scr, gap_scr):
    win = GRID_W
    n_win = TM1 // win
    stride = win + CONV_GAP
    ca = ca_ref[...]
    cb = cb_ref[...]
    bias = cbias_ref[...]
    pieces = [(w, c) for w in range(n_win) for c in range(N_LC)]
    sub = lax.broadcasted_iota(jnp.int32, (CONV_GAP, V7X_LANES), 0)
    joined = jnp.where(is_dec, jnp.float32(0.0), jnp.float32(1.0))
    zero_gap = jnp.zeros((CONV_GAP, V7X_LANES), F32)

    def stage(value_of):
        for c in range(N_LC):
            for w in range(n_win + 1):
                edge = w * win
                if edge % SEQ == 0:
                    gap = zero_gap
                else:
                    before = value_of(slice(edge - CONV_GAP, edge), c)
                    after = value_of(slice(edge, edge + CONV_GAP), c)
                    gap = joined * jnp.where(sub >= CONV_GAP - 2, before, jnp.where(sub == 0, after, 0.0))
                gap_scr[c, w * stride:w * stride + CONV_GAP, :] = gap
        for w, c in pieces:
            lo = CONV_GAP + w * stride
            gap_scr[c, lo:lo + win, :] = value_of(slice(w * win, (w + 1) * win), c)

    def tap(w, c, shift):
        lo = CONV_GAP + w * stride + shift
        return gap_scr[c, lo:lo + win, :]

    def chunk_cols(k, c):
        return slice(k * CW + c * V7X_LANES, k * CW + (c + 1) * V7X_LANES)

    stage(lambda rows, c: proj_scr[rows, chunk_cols(2, c)] * proj_scr[rows, chunk_cols(0, c)])
    for w, c in pieces:
        rows = slice(w * win, (w + 1) * win)
        lanes = slice(c * V7X_LANES, (c + 1) * V7X_LANES)
        conv = ca[0:1, lanes] * tap(w, c, -1) + ca[1:2, lanes] * tap(w, c, 0) + ca[2:3, lanes] * tap(w, c, 1)
        apre_ref[rows, lanes] = (proj_scr[rows, chunk_cols(1, c)] * conv).astype(BF16)
    stage(lambda rows, c: proj_scr[rows, chunk_cols(4, c)])
    for w, c in pieces:
        rows = slice(w * win, (w + 1) * win)
        lanes = slice(c * V7X_LANES, (c + 1) * V7X_LANES)
        xr_scr[rows, lanes] = (cb[0:1, lanes] * tap(w, c, -2) + cb[1:2, lanes] * tap(w, c, -1)
                               + cb[2:3, lanes] * tap(w, c, 0) + cb[3:4, lanes] * tap(w, c, 1)
                               + bias[:, lanes])


def _mix1_kernel(n_ctx_tiles, xc_ref, xd_ref, mod_ref, w5_ref, ca_ref, cb_ref, cbias_ref, wbd_ref, gb_ref,
                 lam_ref, h0_ref, apre_ref, bpre_ref, st_ref,
                 u_scr, proj_scr, a_scr, b_scr, hl_scr, ac_scr, hsum_scr, xr_scr, gap_scr):
    i = pl.program_id(0)
    j = pl.program_id(1)
    is_dec = i >= n_ctx_tiles

    @pl.when(j == 0)
    def _():
        row = jnp.where(is_dec, i - (n_ctx_tiles - 1), 0)
        m = mod_ref[pl.ds(row, 1), :]
        sh1 = m[:, 0:D]
        sc1 = m[:, D:2 * D]
        x = jnp.where(is_dec, xd_ref[...], xc_ref[...])
        u_scr[...] = (_ln_plain(x, 1e-6) * (1.0 + sc1) + sh1).astype(BF16)

    proj_scr[...] = jnp.dot(u_scr[...], w5_ref[0, 0], preferred_element_type=F32)

    _conv_stage(is_dec, proj_scr, ca_ref, cb_ref, cbias_ref, apre_ref, xr_scr, gap_scr)

    xr = xr_scr[...]
    gates = jnp.dot(xr.astype(BF16), wbd_ref[0], preferred_element_type=F32)
    gb = gb_ref[...]
    lam = lam_ref[...]
    sp = jnp.maximum(-lam, 0.0) + jnp.log1p(jnp.exp(-jnp.abs(lam)))
    rate = (-RGLRU_C * LOG2_E) * sp

    one = jnp.float32(1.0)
    for d in range(2):
        ga = gates[:, (2 * d) * CW:(2 * d + 1) * CW] + gb[2 * d:2 * d + 1]
        gx = gates[:, (2 * d + 1) * CW:(2 * d + 2) * CW] + gb[2 * d + 1:2 * d + 2]
        r = _sigmoid(ga)
        ig = _sigmoid(gx)
        a = jnp.exp2(r * rate[d:d + 1])
        y = 1.0 - a * a
        bt = jnp.where(y > 0.0, y * lax.rsqrt(y), 0.0) * (ig * xr)
        for s in range(N_SEG):
            lo = s * SEG_STRIDE
            for c in range(N_LC):
                lanes = slice(c * V7X_LANES, (c + 1) * V7X_LANES)
                a_scr[c, lo:lo + SEG, :] = a[s * SEG:(s + 1) * SEG, lanes]
                b_scr[c, lo:lo + SEG, :] = bt[s * SEG:(s + 1) * SEG, lanes]
        if d == 0:
            keep = [jnp.where(is_dec, one, jnp.float32(s % 2 == 1)) for s in range(N_SEG)]
        else:
            keep = [jnp.where(is_dec, one, jnp.float32(s % 2 == 0)) for s in range(N_SEG)]
        h_in, h_out = _scan_dir(a_scr, b_scr, hl_scr, ac_scr, h0_ref[0, d], keep, reverse=(d == 1))
        st_ref[0, d] = h_out
        for s in range(N_SEG):
            lo = s * SEG_STRIDE
            for c in range(N_LC):
                lanes = slice(c * V7X_LANES, (c + 1) * V7X_LANES)
                h_seg = hl_scr[c, lo:lo + SEG, :] + ac_scr[c, lo:lo + SEG, :] * h_in[s:s + 1, lanes]
                if d == 0:
                    hsum_scr[s * SEG:(s + 1) * SEG, lanes] = h_seg
                else:
                    hsum_scr[s * SEG:(s + 1) * SEG, lanes] += h_seg

    bpre_ref[...] = (hsum_scr[...] * jax.nn.gelu(proj_scr[:, 3 * CW:4 * CW])).astype(BF16)


def _mixer_part1(xs, mod_l, layer, w5, conv_a, conv_b, conv_b_bias, wbd, gbias, lam, h0, n_tok, n_ctx_tok):
    n_tiles = n_tok // TM1
    x_ctx, x_dec, ctx_map, dec_map = _token_source(xs, TM1, n_ctx_tok)
    kern = functools.partial(_mix1_kernel, n_ctx_tok // TM1)
    return pl.pallas_call(
        kern,
        grid=(n_tiles, N_CH),
        in_specs=[
            pl.BlockSpec((TM1, D), lambda i, j: ctx_map(i)),
            pl.BlockSpec((TM1, D), lambda i, j: dec_map(i)),
            pl.BlockSpec((MOD_ROWS, N_MOD * D), lambda i, j: (0, 0)),
            pl.BlockSpec((1, 1, D, 5 * CW), lambda i, j: (layer, j, 0, 0)),
            pl.BlockSpec((3, CW), lambda i, j: (0, j)),
            pl.BlockSpec((4, CW), lambda i, j: (0, j)),
            pl.BlockSpec((1, CW), lambda i, j: (0, j)),
            pl.BlockSpec((1, CW, 4 * CW), lambda i, j: (j, 0, 0)),
            pl.BlockSpec((4, CW), lambda i, j: (0, j)),
            pl.BlockSpec((2, CW), lambda i, j: (0, j)),
            pl.BlockSpec((1, 2, N_SEG, CW), lambda i, j: (i, 0, 0, j)),
        ],
        out_specs=[
            pl.BlockSpec((TM1, CW), lambda i, j: (i, j)),
            pl.BlockSpec((TM1, CW), lambda i, j: (i, j)),
            pl.BlockSpec((1, 2, N_SEG, CW), lambda i, j: (i, 0, 0, j)),
        ],
        out_shape=[
            jax.ShapeDtypeStruct((n_tok, D), BF16),
            jax.ShapeDtypeStruct((n_tok, D), BF16),
            jax.ShapeDtypeStruct((n_tiles, 2, N_SEG, D), F32),
        ],
        scratch_shapes=[
            pltpu.VMEM((TM1, D), BF16),
            pltpu.VMEM((TM1, 5 * CW), F32),
            pltpu.VMEM((N_LC, N_SEG * SEG_STRIDE, V7X_LANES), F32),
            pltpu.VMEM((N_LC, N_SEG * SEG_STRIDE, V7X_LANES), F32),
            pltpu.VMEM((N_LC, N_SEG * SEG_STRIDE, V7X_LANES), F32),
            pltpu.VMEM((N_LC, N_SEG * SEG_STRIDE, V7X_LANES), F32),
            pltpu.VMEM((TM1, CW), F32),
            pltpu.VMEM((TM1, CW), F32),
            pltpu.VMEM((N_LC, CONV_GAP + (TM1 // GRID_W) * (GRID_W + CONV_GAP), V7X_LANES), F32),
        ],
        compiler_params=pltpu.CompilerParams(
            dimension_semantics=("arbitrary", "arbitrary"), vmem_limit_bytes=VMEM_LIMIT),
        name="mixer_scan",
    )(x_ctx, x_dec, mod_l, w5, conv_a, conv_b, conv_b_bias, wbd, gbias, lam, h0)


def _mix2_kernel(n_ctx_tok, xc_ref, xd_ref, mod_ref, ap_ref, bp_ref, wg_ref, wa_ref, wb_ref, wo_ref,
                 g_ref, b_ref, o_ref):
    i = pl.program_id(0)
    m = mod_ref[pl.ds(_mod_row(i * TM2, n_ctx_tok), 1), :]
    sh1 = m[:, 0:D]
    sc1 = m[:, D:2 * D]
    g1 = m[:, 2 * D:3 * D]
    x = jnp.where(i * TM2 >= n_ctx_tok, xd_ref[...], xc_ref[...])
    u = (_ln_plain(x, 1e-6) * (1.0 + sc1) + sh1).astype(BF16)
    gates = jnp.dot(u, wg_ref[0], preferred_element_type=F32)
    br_a = jnp.dot(ap_ref[...], wa_ref[...], preferred_element_type=F32)
    br_b = jnp.dot(bp_ref[...], wb_ref[...], preferred_element_type=F32)
    merged = _sigmoid(gates[:, 0:D]) * br_a + _sigmoid(gates[:, D:2 * D]) * br_b
    mix = jnp.dot(merged.astype(BF16), wo_ref[...], preferred_element_type=F32)
    y = DN_ALPHA * x + g1 * mix
    o_ref[...] = _ln_plain(y, 1e-5) * g_ref[...] + b_ref[...]


def _mixer_part2(xs, mod_l, layer, a_pre, b_pre, wg, wa, wb, wo, ln_g, ln_b, n_ctx_tok):
    n_tok = a_pre.shape[0]
    x_ctx, x_dec, ctx_map, dec_map = _token_source(xs, TM2, n_ctx_tok)
    const = lambda i: (0, 0)
    tile = lambda i: (i, 0)
    return pl.pallas_call(
        functools.partial(_mix2_kernel, n_ctx_tok),
        grid=(n_tok // TM2,),
        in_specs=[
            pl.BlockSpec((TM2, D), ctx_map),
            pl.BlockSpec((TM2, D), dec_map),
            pl.BlockSpec((MOD_ROWS, N_MOD * D), const),
            pl.BlockSpec((TM2, D), tile),
            pl.BlockSpec((TM2, D), tile),
            pl.BlockSpec((1, D, 2 * D), lambda i: (layer, 0, 0)),
            pl.BlockSpec((D, D), const),
            pl.BlockSpec((D, D), const),
            pl.BlockSpec((D, D), const),
            pl.BlockSpec((1, D), const),
            pl.BlockSpec((1, D), const),
        ],
        out_specs=pl.BlockSpec((TM2, D), tile),
        out_shape=jax.ShapeDtypeStruct((n_tok, D), F32),
        compiler_params=pltpu.CompilerParams(
            dimension_semantics=("arbitrary",), vmem_limit_bytes=VMEM_LIMIT),
        name="mixer_out",
    )(x_ctx, x_dec, mod_l, a_pre, b_pre, wg, wa, wb, wo, ln_g, ln_b)


def _ffn_prologue(n_ctx_tok, x_ref, mod_ref):
    m = mod_ref[pl.ds(_mod_row(pl.program_id(0) * TM3, n_ctx_tok), 1), :]
    sh2 = m[:, 3 * D:4 * D]
    sc2 = m[:, 4 * D:5 * D]
    return _ln_plain(x_ref[...], 1e-6) * (1.0 + sc2) + sh2


def _ffn_epilogue(n_ctx_tok, x_ref, mod_ref, acc, g_ref, b_ref):
    m = mod_ref[pl.ds(_mod_row(pl.program_id(0) * TM3, n_ctx_tok), 1), :]
    g2 = m[:, 5 * D:6 * D]
    y = DN_ALPHA * x_ref[...] + g2 * acc
    return _ln_plain(y, 1e-5) * g_ref[...] + b_ref[...]


def _swiglu_chunk(u, w13, w2):
    h = jnp.dot(u, w13, preferred_element_type=F32)
    h1 = h[:, 0:FC]
    hid = (h1 * _sigmoid(h1) * h[:, FC:2 * FC]).astype(BF16)
    return jnp.dot(hid, w2, preferred_element_type=F32)


def _dense_ffn_kernel(n_ctx_tok, x_ref, mod_ref, w13_ref, w2_ref, g_ref, b_ref, o_ref, u_scr, acc_scr):
    f = pl.program_id(1)

    @pl.when(f == 0)
    def _():
        u_scr[...] = _ffn_prologue(n_ctx_tok, x_ref, mod_ref).astype(BF16)

    p = _swiglu_chunk(u_scr[...], w13_ref[0], w2_ref[...])

    @pl.when(f == 0)
    def _():
        acc_scr[...] = p

    @pl.when(f > 0)
    def _():
        acc_scr[...] += p

    @pl.when(f == N_FC - 1)
    def _():
        o_ref[...] = _ffn_epilogue(n_ctx_tok, x_ref, mod_ref, acc_scr[...], g_ref, b_ref)


def _dense_ffn(x, mod_l, w13, w2, ln_g, ln_b, n_ctx_tok):
    n_tok = x.shape[0]
    return pl.pallas_call(
        functools.partial(_dense_ffn_kernel, n_ctx_tok),
        grid=(n_tok // TM3, N_FC),
        in_specs=[
            pl.BlockSpec((TM3, D), lambda i, f: (i, 0)),
            pl.BlockSpec((MOD_ROWS, N_MOD * D), lambda i, f: (0, 0)),
            pl.BlockSpec((1, D, 2 * FC), lambda i, f: (f, 0, 0)),
            pl.BlockSpec((FC, D), lambda i, f: (f, 0)),
            pl.BlockSpec((1, D), lambda i, f: (0, 0)),
            pl.BlockSpec((1, D), lambda i, f: (0, 0)),
        ],
        out_specs=pl.BlockSpec((TM3, D), lambda i, f: (i, 0)),
        out_shape=jax.ShapeDtypeStruct((n_tok, D), F32),
        scratch_shapes=[pltpu.VMEM((TM3, D), BF16), pltpu.VMEM((TM3, D), F32)],
        compiler_params=pltpu.CompilerParams(
            dimension_semantics=("arbitrary", "arbitrary"), vmem_limit_bytes=VMEM_LIMIT),
        name="dense_ffn",
    )(x, mod_l, w13, w2, ln_g, ln_b)


def _segment_copies(off_ref, loc_ref, cnt_ref, tile, make_copy, do):
    for e in range(N_EXP):
        n = cnt_ref[tile * N_EXP + e]
        local = loc_ref[tile * N_EXP + e]
        glob = off_ref[tile * N_EXP + e]
        for k in range(SEG_BITS - 1, -1, -1):
            size = SEG_PAD << k
            take = (n & size) != 0

            @pl.when(take)
            def _():
                do(make_copy(pl.multiple_of(local, SEG_PAD), pl.multiple_of(glob, SEG_PAD), size))

            step = jnp.where(take, size, 0)
            local = local + step
            glob = glob + step


def _rank_kernel(n_ctx_tok, x_ref, mod_ref, wrt_ref, brt_ref, info_ref, cnt_ref, u2b_ref):
    i = pl.program_id(0)
    u2 = _ffn_prologue(n_ctx_tok, x_ref, mod_ref)
    u2b_ref[...] = u2.astype(BF16)
    lg = lax.dot_general(wrt_ref[...], u2, (((1,), (1,)), ((), ())), preferred_element_type=F32,
                         precision=lax.Precision.HIGHEST) + brt_ref[:, 0:1]
    eidx = lax.broadcasted_iota(jnp.int32, lg.shape, 0).astype(F32)
    neg = jnp.float32(-jnp.inf)
    v1 = jnp.max(lg, axis=0, keepdims=True)
    i1 = jnp.min(jnp.where(lg == v1, eidx, float(N_EXP)), axis=0, keepdims=True)
    lg2 = jnp.where(eidx == i1, neg, lg)
    v2 = jnp.max(lg2, axis=0, keepdims=True)
    i2 = jnp.min(jnp.where(lg2 == v2, eidx, float(N_EXP)), axis=0, keepdims=True)
    t = jnp.exp(v2 - v1)
    w_top = 1.0 / (1.0 + t)
    m1 = eidx == i1
    m2 = eidx == i2
    member = jnp.where(jnp.logical_or(m1, m2), 1.0, 0.0)
    before = (lax.broadcasted_iota(jnp.int32, (TS, TS), 0)
              < lax.broadcasted_iota(jnp.int32, (TS, TS), 1))
    rank = jnp.dot(member.astype(BF16), jnp.where(before, 1.0, 0.0).astype(BF16),
                   preferred_element_type=F32)
    for e in range(N_EXP):
        cnt_ref[i, e] = jnp.sum(member[e:e + 1, :]).astype(jnp.int32)
    rank1 = jnp.sum(jnp.where(m1, rank, 0.0), axis=0, keepdims=True)
    rank2 = jnp.sum(jnp.where(m2, rank, 0.0), axis=0, keepdims=True)
    info_ref[0] = jnp.concatenate(
        [i1, i2, rank1, rank2, w_top, t * w_top, jnp.zeros((V7X_SUBLANES - 6, TS), F32)], axis=0)


def _rank(x, mod_l, wrt, brt, n_ctx_tok):
    n_tiles = x.shape[0] // TS
    return pl.pallas_call(
        functools.partial(_rank_kernel, n_ctx_tok),
        grid=(n_tiles,),
        in_specs=[
            pl.BlockSpec((TS, D), lambda i: (i, 0)),
            pl.BlockSpec((MOD_ROWS, N_MOD * D), lambda i: (0, 0)),
            pl.BlockSpec((N_EXP, D), lambda i: (0, 0)),
            pl.BlockSpec((N_EXP, V7X_LANES), lambda i: (0, 0)),
        ],
        out_specs=[
            pl.BlockSpec((1, V7X_SUBLANES, TS), lambda i: (i, 0, 0)),
            pl.BlockSpec(memory_space=pltpu.SMEM),
            pl.BlockSpec((TS, D), lambda i: (i, 0)),
        ],
        out_shape=[
            jax.ShapeDtypeStruct((n_tiles, V7X_SUBLANES, TS), F32),
            jax.ShapeDtypeStruct((n_tiles, N_EXP), jnp.int32),
            jax.ShapeDtypeStruct((n_tiles * TS, D), BF16),
        ],
        compiler_params=pltpu.CompilerParams(
            dimension_semantics=("arbitrary",), vmem_limit_bytes=VMEM_LIMIT),
        name="moe_rank",
    )(x, mod_l, wrt, brt)


def _dispatch_kernel(off_ref, loc_ref, cnt_ref, fill_ref, rinfo_ref, u_ref, xg_ref, info_ref,
                     sorted_scr, zero_scr, sem, zsem):
    i = pl.program_id(0)
    n_steps = pl.num_programs(0)
    slot = i % 2
    rinfo = rinfo_ref[0]
    i1 = rinfo[0:1]
    i2 = rinfo[1:2]
    loc1 = jnp.zeros_like(i1)
    loc2 = jnp.zeros_like(i2)
    for e in range(N_EXP):
        local = loc_ref[i * N_EXP + e].astype(F32)
        loc1 = jnp.where(i1 == float(e), local, loc1)
        loc2 = jnp.where(i2 == float(e), local, loc2)
    row1 = loc1 + rinfo[2:3]
    row2 = loc2 + rinfo[3:4]
    info_ref[0] = jnp.concatenate([row1, row2, rinfo[4:6], jnp.zeros((V7X_SUBLANES - 4, TS), F32)], axis=0)
    p_idx = lax.broadcasted_iota(jnp.int32, (SORT_ROWS, TS), 0).astype(F32)
    sel = jnp.where(jnp.logical_or(p_idx == row1, p_idx == row2), 1.0, 0.0).astype(BF16)
    sorted_scr[slot] = jnp.dot(sel, u_ref[...], preferred_element_type=F32)

    def segment_copies(tile, s, do):
        def make_copy(local, glob, size):
            return pltpu.make_async_copy(sorted_scr.at[s, pl.ds(local, size)], xg_ref.at[pl.ds(glob, size)],
                                         sem.at[s])
        _segment_copies(off_ref, loc_ref, cnt_ref, tile, make_copy, do)

    @pl.when(i > 0)
    def _():
        segment_copies(i - 1, 1 - slot, lambda cp: cp.wait())

    segment_copies(i, slot, lambda cp: cp.start())

    @pl.when(i == n_steps - 1)
    def _():
        segment_copies(i, slot, lambda cp: cp.wait())
        zero_scr[...] = jnp.zeros_like(zero_scr)
        for e in range(N_EXP):
            end = fill_ref[e]
            n_tail = fill_ref[N_EXP + e] // SEG_PAD

            def tail_copy(k):
                return pltpu.make_async_copy(
                    zero_scr.at[pl.ds(0, SEG_PAD)],
                    xg_ref.at[pl.ds(pl.multiple_of(end + k * SEG_PAD, SEG_PAD), SEG_PAD)], zsem)

            def tail_start(k, carry):
                tail_copy(k).start()
                return carry

            def tail_wait(k, carry):
                tail_copy(k).wait()
                return carry

            lax.fori_loop(0, n_tail, tail_start, 0)
            lax.fori_loop(0, n_tail, tail_wait, 0)

        def tile_copy(k):
            return pltpu.make_async_copy(zero_scr, xg_ref.at[pl.ds(pl.multiple_of(k * TR, TR), TR)], zsem)

        def tile_start(k, carry):
            tile_copy(k).start()
            return carry

        def tile_wait(k, carry):
            tile_copy(k).wait()
            return carry

        lax.fori_loop(fill_ref[2 * N_EXP], MAX_RT, tile_start, 0)
        lax.fori_loop(fill_ref[2 * N_EXP], MAX_RT, tile_wait, 0)


def _dispatch(u2b, rinfo, off, loc, cnt, fill):
    n_tiles = rinfo.shape[0]
    tile3 = lambda i, *_: (i, 0, 0)
    return pl.pallas_call(
        _dispatch_kernel,
        grid_spec=pltpu.PrefetchScalarGridSpec(
            num_scalar_prefetch=4,
            grid=(n_tiles,),
            in_specs=[
                pl.BlockSpec((1, V7X_SUBLANES, TS), tile3),
                pl.BlockSpec((TS, D), lambda i, *_: (i, 0)),
            ],
            out_specs=[
                pl.BlockSpec(memory_space=pl.ANY),
                pl.BlockSpec((1, V7X_SUBLANES, TS), tile3),
            ],
            scratch_shapes=[
                pltpu.VMEM((2, SORT_ROWS, D), F32),
                pltpu.VMEM((TR, D), F32),
                pltpu.SemaphoreType.DMA((2,)),
                pltpu.SemaphoreType.DMA,
            ],
        ),
        out_shape=[
            jax.ShapeDtypeStruct((MAX_RT * TR, D), F32),
            jax.ShapeDtypeStruct((n_tiles, V7X_SUBLANES, TS), F32),
        ],
        compiler_params=pltpu.CompilerParams(
            dimension_semantics=("arbitrary",), vmem_limit_bytes=VMEM_LIMIT),
        name="moe_dispatch",
    )(off, loc, cnt, fill, rinfo, u2b)


def _expert_kernel(second, exp_ref, new_ref, nact_ref, *refs):
    if second:
        xg_ref, w1_ref, w3_ref, w2_ref, yp_ref, o_ref, w1b, w3b, w2b = refs
    else:
        xg_ref, w1_ref, w3_ref, w2_ref, o_ref, w1b, w3b, w2b = refs
    r = pl.program_id(0)
    active = r < nact_ref[0]

    @pl.when(jnp.logical_and(active, new_ref[r] == 1))
    def _():
        w1b[...] = w1_ref[0].astype(BF16)
        w3b[...] = w3_ref[0].astype(BF16)
        w2b[...] = w2_ref[0].astype(BF16)

    @pl.when(active)
    def _():
        xb = xg_ref[...].astype(BF16)
        h1 = jnp.dot(xb, w1b[...], preferred_element_type=F32)
        h3 = jnp.dot(xb, w3b[...], preferred_element_type=F32)
        hid = (h1 * _sigmoid(h1) * h3).astype(BF16)
        p = jnp.dot(hid, w2b[...], preferred_element_type=F32)
        if second:
            p = yp_ref[...] + p
        o_ref[...] = p

    @pl.when(jnp.logical_not(active))
    def _():
        o_ref[...] = jnp.zeros_like(o_ref)


def _expert_pass(second, tables, xg, w1, w3, w2, yp=None):
    f = 1 if second else 0
    row = lambda r, ex, nw, na: (r, 0)
    in_specs = [
        pl.BlockSpec((TR, D), row),
        pl.BlockSpec((1, D, FC), lambda r, ex, nw, na: (ex[r], 0, f)),
        pl.BlockSpec((1, D, FC), lambda r, ex, nw, na: (ex[r], 0, f)),
        pl.BlockSpec((1, FC, D), lambda r, ex, nw, na: (ex[r], f, 0)),
    ]
    args = [xg, w1, w3, w2]
    if second:
        in_specs.append(pl.BlockSpec((TR, D), row))
        args.append(yp)
    return pl.pallas_call(
        functools.partial(_expert_kernel, second),
        grid_spec=pltpu.PrefetchScalarGridSpec(
            num_scalar_prefetch=3,
            grid=(MAX_RT,),
            in_specs=in_specs,
            out_specs=pl.BlockSpec((TR, D), row),
            scratch_shapes=[pltpu.VMEM((D, FC), BF16), pltpu.VMEM((D, FC), BF16),
                            pltpu.VMEM((FC, D), BF16)],
        ),
        out_shape=jax.ShapeDtypeStruct((MAX_RT * TR, D), F32),
        compiler_params=pltpu.CompilerParams(
            dimension_semantics=("arbitrary",), vmem_limit_bytes=EXPERT_VMEM_LIMIT),
        name="moe_expert_hi" if second else "moe_expert_lo",
    )(*tables, *args)


def _combine_kernel(n_ctx_tok, off_ref, loc_ref, cnt_ref, x_ref, mod_ref, info_ref, y_ref, g_ref, b_ref,
                    oc_ref, od_ref, ys_scr, sem):
    i = pl.program_id(0)
    n_steps = pl.num_programs(0)
    slot = i % 2

    def fetch(tile, s, do):
        def make_copy(local, glob, size):
            return pltpu.make_async_copy(y_ref.at[pl.ds(glob, size)], ys_scr.at[s, pl.ds(local, size)],
                                         sem.at[s])
        _segment_copies(off_ref, loc_ref, cnt_ref, tile, make_copy, do)

    def start_fetch(tile, s):
        ys_scr[s] = jnp.zeros((SORT_ROWS, D), F32)
        fetch(tile, s, lambda cp: cp.start())

    @pl.when(i == 0)
    def _():
        start_fetch(0, 0)

    @pl.when(i + 1 < n_steps)
    def _():
        start_fetch(i + 1, 1 - slot)

    info = info_ref[0]
    row1, row2, w1, w2 = info[0:1], info[1:2], info[2:3], info[3:4]
    p_idx = lax.broadcasted_iota(jnp.int32, (SORT_ROWS, TS), 0).astype(F32)
    w_rows = jnp.sum(jnp.where(p_idx == row1, w1, 0.0) + jnp.where(p_idx == row2, w2, 0.0),
                     axis=1, keepdims=True)
    padded = jnp.concatenate([info, jnp.zeros((V7X_LANES - V7X_SUBLANES, TS), F32)], axis=0)
    cols = jnp.transpose(padded, (1, 0))
    q_idx = lax.broadcasted_iota(jnp.int32, (TS, SORT_ROWS), 1).astype(F32)
    pick = jnp.where(jnp.logical_or(q_idx == cols[:, 0:1], q_idx == cols[:, 1:2]), 1.0, 0.0).astype(BF16)
    fetch(i, slot, lambda cp: cp.wait())
    scaled = (ys_scr[slot] * w_rows).astype(BF16)
    acc = jnp.dot(pick, scaled, preferred_element_type=F32)
    res = _ffn_epilogue(n_ctx_tok, x_ref, mod_ref, acc, g_ref, b_ref)
    is_dec = i * TS >= n_ctx_tok

    @pl.when(jnp.logical_not(is_dec))
    def _():
        oc_ref[...] = res

    @pl.when(is_dec)
    def _():
        od_ref[...] = res


def _combine(x, mod_l, info, y, segs, ln_g, ln_b, n_ctx_tok):
    n_tok = x.shape[0]
    n_tiles = n_tok // TS
    ncb = n_ctx_tok // TS
    const = lambda i, *_: (0, 0)
    return pl.pallas_call(
        functools.partial(_combine_kernel, n_ctx_tok),
        grid_spec=pltpu.PrefetchScalarGridSpec(
            num_scalar_prefetch=3,
            grid=(n_tiles,),
            in_specs=[
                pl.BlockSpec((TS, D), lambda i, *_: (i, 0)),
                pl.BlockSpec((MOD_ROWS, N_MOD * D), const),
                pl.BlockSpec((1, V7X_SUBLANES, TS), lambda i, *_: (i, 0, 0)),
                pl.BlockSpec(memory_space=pl.ANY),
                pl.BlockSpec((1, D), const),
                pl.BlockSpec((1, D), const),
            ],
            out_specs=[
                pl.BlockSpec((TS, D), lambda i, *_: (jnp.minimum(i, ncb - 1), 0)),
                pl.BlockSpec((TS, D), lambda i, *_: (jnp.maximum(i - ncb, 0), 0)),
            ],
            scratch_shapes=[
                pltpu.VMEM((2, SORT_ROWS, D), F32),
                pltpu.SemaphoreType.DMA((2,)),
            ],
        ),
        out_shape=[
            jax.ShapeDtypeStruct((n_ctx_tok, D), F32),
            jax.ShapeDtypeStruct((n_tok - n_ctx_tok, D), F32),
        ],
        compiler_params=pltpu.CompilerParams(
            dimension_semantics=("arbitrary",), vmem_limit_bytes=VMEM_LIMIT),
        name="moe_combine",
    )(*segs, x, mod_l, info, y, ln_g, ln_b)


def _routing_tables(counts):
    counts = (counts + SEG_PAD - 1) // SEG_PAD * SEG_PAD
    totals = jnp.sum(counts, axis=0)
    n_rt = (totals + TR - 1) // TR
    cum = jnp.cumsum(n_rt)
    first = cum - n_rt
    n_act = cum[-1]
    off = first[None, :] * TR + (jnp.cumsum(counts, axis=0) - counts)
    loc = jnp.cumsum(counts, axis=1) - counts
    fill = jnp.concatenate([first * TR + totals, n_rt * TR - totals, n_act[None]])
    r = jnp.arange(MAX_RT, dtype=jnp.int32)
    rc = jnp.minimum(r, n_act - 1)
    exp = jnp.sum((rc[:, None] >= cum[None, :]).astype(jnp.int32), axis=1)
    new = jnp.logical_and(r == first[exp], r < n_act)
    i32 = lambda a: a.astype(jnp.int32)
    segs = (i32(off.reshape(-1)), i32(loc.reshape(-1)), i32(counts.reshape(-1)))
    return segs, i32(fill), (i32(exp), i32(new), i32(n_act.reshape(1)))


def _moe_ffn(x, mod_l, router_w, router_b, w1, w3, w2, ln_g, ln_b, n_ctx_tok):
    n_tok = x.shape[0]
    n_seg_pad = (n_tok // TS) * N_EXP * (SEG_PAD - 1)
    assert (2 * n_tok + n_seg_pad + TR - 1) // TR + N_EXP == MAX_RT and N_FC == 2
    wrt = router_w.T
    brt = jnp.broadcast_to(router_b.reshape(N_EXP, 1), (N_EXP, V7X_LANES))
    rinfo, counts, u2b = _rank(x, mod_l, wrt, brt, n_ctx_tok)
    segs, fill, tables = _routing_tables(counts)
    xg, info = _dispatch(u2b, rinfo, *segs, fill)
    y_lo = _expert_pass(False, tables, xg, w1, w3, w2)
    y = _expert_pass(True, tables, xg, w1, w3, w2, y_lo)
    return _combine(x, mod_l, info, y, segs, ln_g, ln_b, n_ctx_tok)


def _cast_kernel(*refs):
    o_ref = refs[-1]
    off = 0
    for x_ref in refs[:-1]:
        width = x_ref.shape[-1]
        o_ref[..., off:off + width] = x_ref[...].reshape(o_ref.shape[:-1] + (width,)).astype(o_ref.dtype)
        off += width


def _pack_w13(w1, w3):
    return pl.pallas_call(
        _cast_kernel,
        grid=(N_FC,),
        in_specs=[pl.BlockSpec((D, FC), lambda f: (0, f)), pl.BlockSpec((D, FC), lambda f: (0, f))],
        out_specs=pl.BlockSpec((1, D, 2 * FC), lambda f: (f, 0, 0)),
        out_shape=jax.ShapeDtypeStruct((N_FC, D, 2 * FC), BF16),
        compiler_params=pltpu.CompilerParams(vmem_limit_bytes=VMEM_LIMIT),
        name="pack_w13",
    )(w1, w3)


def _pack_w_in(w_in):
    depth = w_in.shape[0]
    def regroup_kernel(x_ref, o_ref):
        for j in range(N_CH):
            o_ref[0, j] = x_ref[0, :, j * CW:(j + 1) * CW].astype(BF16)

    w5 = pl.pallas_call(
        regroup_kernel,
        grid=(depth, 5),
        in_specs=[pl.BlockSpec((1, D, D), lambda l, g: (l, 0, g))],
        out_specs=pl.BlockSpec((1, N_CH, D, CW), lambda l, g: (l, 0, 0, g)),
        out_shape=jax.ShapeDtypeStruct((depth, N_CH, D, 5 * CW), BF16),
        compiler_params=pltpu.CompilerParams(vmem_limit_bytes=VMEM_LIMIT),
        name="pack_w5",
    )(w_in)
    wg = pl.pallas_call(
        _cast_kernel,
        grid=(depth, 2),
        in_specs=[pl.BlockSpec((1, D, D), lambda l, k: (l, 0, 5 + k))],
        out_specs=pl.BlockSpec((1, D, D), lambda l, k: (l, 0, k)),
        out_shape=jax.ShapeDtypeStruct((depth, D, 2 * D), BF16),
        compiler_params=pltpu.CompilerParams(vmem_limit_bytes=VMEM_LIMIT),
        name="pack_wg",
    )(w_in)
    return w5, wg


def _block_diag_chunks(w):
    tiled = jnp.tile(w.reshape(N_CH, CW, HEAD_D), (1, 1, CW // HEAD_D))
    blk = jnp.arange(CW) // HEAD_D
    return jnp.where(blk[:, None] == blk[None, :], tiled, 0.0)


def kernel(x_prompt, x_sample, state_rglru, c, c_ctx, w_mod, b_mod, w_in, conv_a, w_a_out, conv_b, conv_b_bias, w_gate_a, b_gate_a, w_gate_x, b_gate_x, lru_lambda, w_b_out, w_o, ln1_g, ln1_b, ln2_g, ln2_b, ffn_w1, ffn_w3, ffn_w2, router_w, router_b, moe_w1, moe_w3, moe_w2):
    batch, seq, d = x_prompt.shape
    dec_batch, dec_seq, _ = x_sample.shape
    depth = w_mod.shape[0]
    assert (d, seq, dec_seq, depth) == (D, SEQ, DEC_SEQ, DEPTH)
    n_ctx_tok = batch * seq
    n_dec_tok = dec_batch * dec_seq
    assert n_ctx_tok % TM1 == 0 and TM1 == dec_seq and 1 + dec_batch <= MOD_ROWS
    n_ctx_tiles = n_ctx_tok // TM1
    n_tiles = n_ctx_tiles + dec_batch
    seq_per_tile = TM1 // seq
    assert N_SEG == 2 * seq_per_tile

    assert depth % 2 == 0
    n_tok = n_ctx_tok + n_dec_tok
    x = (x_prompt.reshape(n_ctx_tok, D), x_sample.reshape(n_dec_tok, D))

    cond = jnp.zeros((MOD_ROWS, D), F32).at[0].set(c_ctx).at[1:1 + dec_batch].set(c)
    mod = _modulation(cond, w_mod, b_mod)

    w5, wg = _pack_w_in(w_in)
    states = []
    for l in range(depth):
        wbd = jnp.concatenate(
            [_block_diag_chunks(w_gate_a[l, 0]), _block_diag_chunks(w_gate_x[l, 0]),
             _block_diag_chunks(w_gate_a[l, 1]), _block_diag_chunks(w_gate_x[l, 1])],
            axis=-1).astype(BF16)
        gbias = jnp.stack([b_gate_a[l, 0], b_gate_x[l, 0], b_gate_a[l, 1], b_gate_x[l, 1]], axis=0)
        h0 = jnp.zeros((n_tiles, 2, N_SEG, D), F32)
        h0 = h0.at[n_ctx_tiles:, 0, 0].set(state_rglru[:, l, 0].astype(F32))
        h0 = h0.at[n_ctx_tiles:, 1, N_SEG - 1].set(state_rglru[:, l, 1].astype(F32))

        a_pre, b_pre, st = _mixer_part1(
            x, mod[l], l, w5, conv_a[l], conv_b[l], conv_b_bias[l].reshape(1, D), wbd, gbias,
            lru_lambda[l], h0, n_tok, n_ctx_tok)
        x = _mixer_part2(
            x, mod[l], l, a_pre, b_pre, wg, w_a_out[l].astype(BF16), w_b_out[l].astype(BF16),
            w_o[l].astype(BF16), ln1_g[l].reshape(1, D), ln1_b[l].reshape(1, D), n_ctx_tok)

        if l % 2 == 0:
            k = l // 2
            x = _dense_ffn(x, mod[l], _pack_w13(ffn_w1[k], ffn_w3[k]), ffn_w2[k].astype(BF16),
                           ln2_g[l].reshape(1, D), ln2_b[l].reshape(1, D), n_ctx_tok)
        else:
            k = l // 2
            x = _moe_ffn(x, mod[l], router_w[k], router_b[k], moe_w1[k], moe_w3[k], moe_w2[k],
                         ln2_g[l].reshape(1, D), ln2_b[l].reshape(1, D), n_ctx_tok)

        st_ctx = st[:n_ctx_tiles]
        fwd = st_ctx[:, 0, 1::2].reshape(batch, D)
        bwd = st_ctx[:, 1, 0::2].reshape(batch, D)
        states.append(jnp.stack([fwd, bwd], axis=1))

    y_prompt = x[0].reshape(batch, seq, D)
    y_sample = x[1].reshape(dec_batch, dec_seq, D)
    new_state = jnp.stack(states, axis=1).astype(x_prompt.dtype)
    return (y_prompt, y_sample, new_state)
```

```python
import functools

import jax
import jax.numpy as jnp
from jax import lax
from jax.experimental import pallas as pl
from jax.experimental.pallas import tpu as pltpu

F32 = jnp.float32
BF16 = jnp.bfloat16

D = 1024
SEQ = 256
DEC_SEQ = 1024
GRID_W = 64
N_HEAD = 16
HEAD_D = D // N_HEAD
RGLRU_C = 8.0
LOG2_E = 1.4426950408889634
D_FF = 2816
N_EXP = 8
N_MOD = 6
DEPTH = 2
DN_ALPHA = (2.0 * DEPTH) ** 0.25

V7X_SUBLANES = 8
V7X_LANES = 128
V7X_VMEM_BYTES = 64 * 1024 * 1024
VMEM_LIMIT = V7X_VMEM_BYTES - 12 * 1024 * 1024

TM1 = 1024
CW = 256
N_CH = D // CW
N_LC = CW // V7X_LANES
N_SEG = V7X_SUBLANES
SEG = TM1 // N_SEG
SEG_STRIDE = SEG + 4
CONV_GAP = V7X_SUBLANES
TM2 = 512
TM3 = 512
FC = 1408
N_FC = D_FF // FC
TS = 512
TR = 256
SEG_PAD = V7X_SUBLANES
SEG_BITS = 7
SORT_ROWS = 2 * TS + V7X_LANES
MAX_RT = (2 * 10240 + (10240 // TS) * N_EXP * (SEG_PAD - 1) + TR - 1) // TR + N_EXP
EXPERT_VMEM_LIMIT = V7X_VMEM_BYTES - 6 * 1024 * 1024
MOD_ROWS = 8
MOD_BLK = 3072


def _sigmoid(x):
    return 0.5 * jnp.tanh(0.5 * x) + 0.5


def _ln_plain(x, eps):
    mu = jnp.mean(x, axis=-1, keepdims=True)
    xc = x - mu
    var = jnp.mean(xc * xc, axis=-1, keepdims=True)
    return xc * lax.rsqrt(var + eps)


def _token_source(xs, tile, n_ctx_tok):
    ncb = n_ctx_tok // tile
    if isinstance(xs, tuple):
        x_ctx, x_dec = xs
        dec_off = 0
    else:
        x_ctx = x_dec = xs
        dec_off = ncb
    ctx_map = lambda i: (jnp.minimum(i, ncb - 1), 0)
    dec_map = lambda i: (jnp.maximum(i - ncb, 0) + dec_off, 0)
    return x_ctx, x_dec, ctx_map, dec_map


def _mod_row(tok0, n_ctx_tok):
    dec = jnp.maximum(tok0 - n_ctx_tok, 0) // DEC_SEQ
    return jnp.where(tok0 >= n_ctx_tok, 1 + dec, 0)


def _mod_kernel(cond_ref, w_ref, b_ref, o_ref):
    cnd = cond_ref[...]
    s = cnd * _sigmoid(cnd)
    o_ref[0] = jnp.dot(s, w_ref[0], preferred_element_type=F32,
                       precision=lax.Precision.HIGHEST) + b_ref[0]


def _modulation(cond, w_mod, b_mod):
    depth = w_mod.shape[0]
    n_out = w_mod.shape[2]
    return pl.pallas_call(
        _mod_kernel,
        grid=(depth, n_out // MOD_BLK),
        in_specs=[
            pl.BlockSpec((MOD_ROWS, D), lambda l, j: (0, 0)),
            pl.BlockSpec((1, D, MOD_BLK), lambda l, j: (l, 0, j)),
            pl.BlockSpec((1, 1, MOD_BLK), lambda l, j: (l, 0, j)),
        ],
        out_specs=pl.BlockSpec((1, MOD_ROWS, MOD_BLK), lambda l, j: (l, 0, j)),
        out_shape=jax.ShapeDtypeStruct((depth, MOD_ROWS, n_out), F32),
        compiler_params=pltpu.CompilerParams(
            dimension_semantics=("arbitrary", "arbitrary"), vmem_limit_bytes=VMEM_LIMIT),
        name="modulation",
    )(cond, w_mod, b_mod.reshape(depth, 1, n_out))


def _scan_dir(a_scr, b_scr, hl_scr, ac_scr, h0, keep, reverse):
    n_lc = a_scr.shape[0]

    def body(k, carry):
        kk = SEG - 1 - k if reverse else k
        idx = pl.ds(kk, N_SEG, stride=SEG_STRIDE)
        new = []
        for c in range(n_lc):
            h, acc = carry[c]
            a_k = a_scr[c, idx, :]
            h = a_k * h + b_scr[c, idx, :]
            acc = a_k * acc
            hl_scr[c, idx, :] = h
            ac_scr[c, idx, :] = acc
            new.append((h, acc))
        return tuple(new)

    init = tuple((jnp.zeros((N_SEG, V7X_LANES), F32), jnp.ones((N_SEG, V7X_LANES), F32))
                 for _ in range(n_lc))
    fin = lax.fori_loop(0, SEG, body, init, unroll=8)
    h_loc = jnp.concatenate([fin[c][0] for c in range(n_lc)], axis=1)
    a_tot = jnp.concatenate([fin[c][1] for c in range(n_lc)], axis=1)

    order = range(N_SEG - 1, -1, -1) if reverse else range(N_SEG)
    h_in = [None] * N_SEG
    prev = None
    for s in order:
        cur = h0[s:s + 1]
        if prev is not None:
            left = a_tot[prev:prev + 1] * h_in[prev] + h_loc[prev:prev + 1]
            cur = keep[s] * left + cur
        h_in[s] = cur
        prev = s
    h_in = jnp.concatenate(h_in, axis=0)
    return h_in, a_tot * h_in + h_loc


def _conv_stage(is_dec, proj_scr, ca_ref, cb_ref, cbias_ref, apre_ref, xr_scr, gap_scr):
    win = GRID_W
    n_win = TM1 // win
    stride = win + CONV_GAP
    ca = ca_ref[...]
    cb = cb_ref[...]
    bias = cbias_ref[...]
    pieces = [(w, c) for w in range(n_win) for c in range(N_LC)]
    sub = lax.broadcasted_iota(jnp.int32, (CONV_GAP, V7X_LANES), 0)
    joined = jnp.where(is_dec, jnp.float32(0.0), jnp.float32(1.0))
    zero_gap = jnp.zeros((CONV_GAP, V7X_LANES), F32)

    def stage(value_of):
        for c in range(N_LC):
            for w in range(n_win + 1):
                edge = w * win
                if edge % SEQ == 0:
                    gap = zero_gap
                else:
                    before = value_of(slice(edge - CONV_GAP, edge), c)
                    after = value_of(slice(edge, edge + CONV_GAP), c)
                    gap = joined * jnp.where(sub >= CONV_GAP - 2, before, jnp.where(sub == 0, after, 0.0))
                gap_scr[c, w * stride:w * stride + CONV_GAP, :] = gap
        for w, c in pieces:
            lo = CONV_GAP + w * stride
            gap_scr[c, lo:lo + win, :] = value_of(slice(w * win, (w + 1) * win), c)

    def tap(w, c, shift):
        lo = CONV_GAP + w * stride + shift
        return gap_scr[c, lo:lo + win, :]

    def chunk_cols(k, c):
        return slice(k * CW + c * V7X_LANES, k * CW + (c + 1) * V7X_LANES)

    stage(lambda rows, c: proj_scr[rows, chunk_cols(2, c)] * proj_scr[rows, chunk_cols(0, c)])
    for w, c in pieces:
        rows = slice(w * win, (w + 1) * win)
        lanes = slice(c * V7X_LANES, (c + 1) * V7X_LANES)
        conv = ca[0:1, lanes] * tap(w, c, -1) + ca[1:2, lanes] * tap(w, c, 0) + ca[2:3, lanes] * tap(w, c, 1)
        apre_ref[rows, lanes] = (proj_scr[rows, chunk_cols(1, c)] * conv).astype(BF16)
    stage(lambda rows, c: proj_scr[rows, chunk_cols(4, c)])
    for w, c in pieces:
        rows = slice(w * win, (w + 1) * win)
        lanes = slice(c * V7X_LANES, (c + 1) * V7X_LANES)
        xr_scr[rows, lanes] = (cb[0:1, lanes] * tap(w, c, -2) + cb[1:2, lanes] * tap(w, c, -1)
                               + cb[2:3, lanes] * tap(w, c, 0) + cb[3:4, lanes] * tap(w, c, 1)
                               + bias[:, lanes])


def _mix1_kernel(n_ctx_tiles, xc_ref, xd_ref, mod_ref, w5_ref, ca_ref, cb_ref, cbias_ref, wbd_ref, gb_ref,
                 lam_ref, h0_ref, apre_ref, bpre_ref, st_ref,
                 u_scr, proj_scr, a_scr, b_scr, hl_scr, ac_scr, hsum_scr, xr_scr, gap_scr):
    i = pl.program_id(0)
    j = pl.program_id(1)
    is_dec = i >= n_ctx_tiles

    @pl.when(j == 0)
    def _():
        row = jnp.where(is_dec, i - (n_ctx_tiles - 1), 0)
        m = mod_ref[pl.ds(row, 1), :]
        sh1 = m[:, 0:D]
        sc1 = m[:, D:2 * D]
        x = jnp.where(is_dec, xd_ref[...], xc_ref[...])
        u_scr[...] = (_ln_plain(x, 1e-6) * (1.0 + sc1) + sh1).astype(BF16)

    proj_scr[...] = jnp.dot(u_scr[...], w5_ref[0, 0], preferred_element_type=F32)

    _conv_stage(is_dec, proj_scr, ca_ref, cb_ref, cbias_ref, apre_ref, xr_scr, gap_scr)

    xr = xr_scr[...]
    gates = jnp.dot(xr.astype(BF16), wbd_ref[0], preferred_element_type=F32)
    gb = gb_ref[...]
    lam = lam_ref[...]
    sp = jnp.maximum(-lam, 0.0) + jnp.log1p(jnp.exp(-jnp.abs(lam)))
    rate = (-RGLRU_C * LOG2_E) * sp

    one = jnp.float32(1.0)
    for d in range(2):
        ga = gates[:, (2 * d) * CW:(2 * d + 1) * CW] + gb[2 * d:2 * d + 1]
        gx = gates[:, (2 * d + 1) * CW:(2 * d + 2) * CW] + gb[2 * d + 1:2 * d + 2]
        r = _sigmoid(ga)
        ig = _sigmoid(gx)
        a = jnp.exp2(r * rate[d:d + 1])
        y = 1.0 - a * a
        bt = jnp.where(y > 0.0, y * lax.rsqrt(y), 0.0) * (ig * xr)
        for s in range(N_SEG):
            lo = s * SEG_STRIDE
            for c in range(N_LC):
                lanes = slice(c * V7X_LANES, (c + 1) * V7X_LANES)
                a_scr[c, lo:lo + SEG, :] = a[s * SEG:(s + 1) * SEG, lanes]
                b_scr[c, lo:lo + SEG, :] = bt[s * SEG:(s + 1) * SEG, lanes]
        if d == 0:
            keep = [jnp.where(is_dec, one, jnp.float32(s % 2 == 1)) for s in range(N_SEG)]
        else:
            keep = [jnp.where(is_dec, one, jnp.float32(s % 2 == 0)) for s in range(N_SEG)]
        h_in, h_out = _scan_dir(a_scr, b_scr, hl_scr, ac_scr, h0_ref[0, d], keep, reverse=(d == 1))
        st_ref[0, d] = h_out
        for s in range(N_SEG):
            lo = s * SEG_STRIDE
            for c in range(N_LC):
                lanes = slice(c * V7X_LANES, (c + 1) * V7X_LANES)
                h_seg = hl_scr[c, lo:lo + SEG, :] + ac_scr[c, lo:lo + SEG, :] * h_in[s:s + 1, lanes]
                if d == 0:
                    hsum_scr[s * SEG:(s + 1) * SEG, lanes] = h_seg
                else:
                    hsum_scr[s * SEG:(s + 1) * SEG, lanes] += h_seg

    bpre_ref[...] = (hsum_scr[...] * jax.nn.gelu(proj_scr[:, 3 * CW:4 * CW])).astype(BF16)


def _mixer_part1(xs, mod_l, layer, w5, conv_a, conv_b, conv_b_bias, wbd, gbias, lam, h0, n_tok, n_ctx_tok):
    n_tiles = n_tok // TM1
    x_ctx, x_dec, ctx_map, dec_map = _token_source(xs, TM1, n_ctx_tok)
    kern = functools.partial(_mix1_kernel, n_ctx_tok // TM1)
    return pl.pallas_call(
        kern,
        grid=(n_tiles, N_CH),
        in_specs=[
            pl.BlockSpec((TM1, D), lambda i, j: ctx_map(i)),
            pl.BlockSpec((TM1, D), lambda i, j: dec_map(i)),
            pl.BlockSpec((MOD_ROWS, N_MOD * D), lambda i, j: (0, 0)),
            pl.BlockSpec((1, 1, D, 5 * CW), lambda i, j: (layer, j, 0, 0)),
            pl.BlockSpec((3, CW), lambda i, j: (0, j)),
            pl.BlockSpec((4, CW), lambda i, j: (0, j)),
            pl.BlockSpec((1, CW), lambda i, j: (0, j)),
            pl.BlockSpec((1, CW, 4 * CW), lambda i, j: (j, 0, 0)),
            pl.BlockSpec((4, CW), lambda i, j: (0, j)),
            pl.BlockSpec((2, CW), lambda i, j: (0, j)),
            pl.BlockSpec((1, 2, N_SEG, CW), lambda i, j: (i, 0, 0, j)),
        ],
        out_specs=[
            pl.BlockSpec((TM1, CW), lambda i, j: (i, j)),
            pl.BlockSpec((TM1, CW), lambda i, j: (i, j)),
            pl.BlockSpec((1, 2, N_SEG, CW), lambda i, j: (i, 0, 0, j)),
        ],
        out_shape=[
            jax.ShapeDtypeStruct((n_tok, D), BF16),
            jax.ShapeDtypeStruct((n_tok, D), BF16),
            jax.ShapeDtypeStruct((n_tiles, 2, N_SEG, D), F32),
        ],
        scratch_shapes=[
            pltpu.VMEM((TM1, D), BF16),
            pltpu.VMEM((TM1, 5 * CW), F32),
            pltpu.VMEM((N_LC, N_SEG * SEG_STRIDE, V7X_LANES), F32),
            pltpu.VMEM((N_LC, N_SEG * SEG_STRIDE, V7X_LANES), F32),
            pltpu.VMEM((N_LC, N_SEG * SEG_STRIDE, V7X_LANES), F32),
            pltpu.VMEM((N_LC, N_SEG * SEG_STRIDE, V7X_LANES), F32),
            pltpu.VMEM((TM1, CW), F32),
            pltpu.VMEM((TM1, CW), F32),
            pltpu.VMEM((N_LC, CONV_GAP + (TM1 // GRID_W) * (GRID_W + CONV_GAP), V7X_LANES), F32),
        ],
        compiler_params=pltpu.CompilerParams(
            dimension_semantics=("arbitrary", "arbitrary"), vmem_limit_bytes=VMEM_LIMIT),
        name="mixer_scan",
    )(x_ctx, x_dec, mod_l, w5, conv_a, conv_b, conv_b_bias, wbd, gbias, lam, h0)


def _mix2_kernel(n_ctx_tok, xc_ref, xd_ref, mod_ref, ap_ref, bp_ref, wg_ref, wa_ref, wb_ref, wo_ref,
                 g_ref, b_ref, o_ref):
    i = pl.program_id(0)
    m = mod_ref[pl.ds(_mod_row(i * TM2, n_ctx_tok), 1), :]
    sh1 = m[:, 0:D]
    sc1 = m[:, D:2 * D]
    g1 = m[:, 2 * D:3 * D]
    x = jnp.where(i * TM2 >= n_ctx_tok, xd_ref[...], xc_ref[...])
    u = (_ln_plain(x, 1e-6) * (1.0 + sc1) + sh1).astype(BF16)
    gates = jnp.dot(u, wg_ref[0], preferred_element_type=F32)
    br_a = jnp.dot(ap_ref[...], wa_ref[...], preferred_element_type=F32)
    br_b = jnp.dot(bp_ref[...], wb_ref[...], preferred_element_type=F32)
    merged = _sigmoid(gates[:, 0:D]) * br_a + _sigmoid(gates[:, D:2 * D]) * br_b
    mix = jnp.dot(merged.astype(BF16), wo_ref[...], preferred_element_type=F32)
    y = DN_ALPHA * x + g1 * mix
    o_ref[...] = _ln_plain(y, 1e-5) * g_ref[...] + b_ref[...]


def _mixer_part2(xs, mod_l, layer, a_pre, b_pre, wg, wa, wb, wo, ln_g, ln_b, n_ctx_tok):
    n_tok = a_pre.shape[0]
    x_ctx, x_dec, ctx_map, dec_map = _token_source(xs, TM2, n_ctx_tok)
    const = lambda i: (0, 0)
    tile = lambda i: (i, 0)
    return pl.pallas_call(
        functools.partial(_mix2_kernel, n_ctx_tok),
        grid=(n_tok // TM2,),
        in_specs=[
            pl.BlockSpec((TM2, D), ctx_map),
            pl.BlockSpec((TM2, D), dec_map),
            pl.BlockSpec((MOD_ROWS, N_MOD * D), const),
            pl.BlockSpec((TM2, D), tile),
            pl.BlockSpec((TM2, D), tile),
            pl.BlockSpec((1, D, 2 * D), lambda i: (layer, 0, 0)),
            pl.BlockSpec((D, D), const),
            pl.BlockSpec((D, D), const),
            pl.BlockSpec((D, D), const),
            pl.BlockSpec((1, D), const),
            pl.BlockSpec((1, D), const),
        ],
        out_specs=pl.BlockSpec((TM2, D), tile),
        out_shape=jax.ShapeDtypeStruct((n_tok, D), F32),
        compiler_params=pltpu.CompilerParams(
            dimension_semantics=("arbitrary",), vmem_limit_bytes=VMEM_LIMIT),
        name="mixer_out",
    )(x_ctx, x_dec, mod_l, a_pre, b_pre, wg, wa, wb, wo, ln_g, ln_b)


def _ffn_prologue(n_ctx_tok, x_ref, mod_ref):
    m = mod_ref[pl.ds(_mod_row(pl.program_id(0) * TM3, n_ctx_tok), 1), :]
    sh2 = m[:, 3 * D:4 * D]
    sc2 = m[:, 4 * D:5 * D]
    return _ln_plain(x_ref[...], 1e-6) * (1.0 + sc2) + sh2


def _ffn_epilogue(n_ctx_tok, x_ref, mod_ref, acc, g_ref, b_ref):
    m = mod_ref[pl.ds(_mod_row(pl.program_id(0) * TM3, n_ctx_tok), 1), :]
    g2 = m[:, 5 * D:6 * D]
    y = DN_ALPHA * x_ref[...] + g2 * acc
    return _ln_plain(y, 1e-5) * g_ref[...] + b_ref[...]


def _swiglu_chunk(u, w13, w2):
    h = jnp.dot(u, w13, preferred_element_type=F32)
    h1 = h[:, 0:FC]
    hid = (h1 * _sigmoid(h1) * h[:, FC:2 * FC]).astype(BF16)
    return jnp.dot(hid, w2, preferred_element_type=F32)


def _dense_ffn_kernel(n_ctx_tok, x_ref, mod_ref, w13_ref, w2_ref, g_ref, b_ref, o_ref, u_scr, acc_scr):
    f = pl.program_id(1)

    @pl.when(f == 0)
    def _():
        u_scr[...] = _ffn_prologue(n_ctx_tok, x_ref, mod_ref).astype(BF16)

    p = _swiglu_chunk(u_scr[...], w13_ref[0], w2_ref[...])

    @pl.when(f == 0)
    def _():
        acc_scr[...] = p

    @pl.when(f > 0)
    def _():
        acc_scr[...] += p

    @pl.when(f == N_FC - 1)
    def _():
        o_ref[...] = _ffn_epilogue(n_ctx_tok, x_ref, mod_ref, acc_scr[...], g_ref, b_ref)


def _dense_ffn(x, mod_l, w13, w2, ln_g, ln_b, n_ctx_tok):
    n_tok = x.shape[0]
    return pl.pallas_call(
        functools.partial(_dense_ffn_kernel, n_ctx_tok),
        grid=(n_tok // TM3, N_FC),
        in_specs=[
            pl.BlockSpec((TM3, D), lambda i, f: (i, 0)),
            pl.BlockSpec((MOD_ROWS, N_MOD * D), lambda i, f: (0, 0)),
            pl.BlockSpec((1, D, 2 * FC), lambda i, f: (f, 0, 0)),
            pl.BlockSpec((FC, D), lambda i, f: (f, 0)),
            pl.BlockSpec((1, D), lambda i, f: (0, 0)),
            pl.BlockSpec((1, D), lambda i, f: (0, 0)),
        ],
        out_specs=pl.BlockSpec((TM3, D), lambda i, f: (i, 0)),
        out_shape=jax.ShapeDtypeStruct((n_tok, D), F32),
        scratch_shapes=[pltpu.VMEM((TM3, D), BF16), pltpu.VMEM((TM3, D), F32)],
        compiler_params=pltpu.CompilerParams(
            dimension_semantics=("arbitrary", "arbitrary"), vmem_limit_bytes=VMEM_LIMIT),
        name="dense_ffn",
    )(x, mod_l, w13, w2, ln_g, ln_b)


def _segment_copies(off_ref, loc_ref, cnt_ref, tile, make_copy, do):
    for e in range(N_EXP):
        n = cnt_ref[tile * N_EXP + e]
        local = loc_ref[tile * N_EXP + e]
        glob = off_ref[tile * N_EXP + e]
        for k in range(SEG_BITS - 1, -1, -1):
            size = SEG_PAD << k
            take = (n & size) != 0

            @pl.when(take)
            def _():
                do(make_copy(pl.multiple_of(local, SEG_PAD), pl.multiple_of(glob, SEG_PAD), size))

            step = jnp.where(take, size, 0)
            local = local + step
            glob = glob + step


def _rank_kernel(n_ctx_tok, x_ref, mod_ref, wrt_ref, brt_ref, info_ref, cnt_ref, u2b_ref):
    i = pl.program_id(0)
    u2 = _ffn_prologue(n_ctx_tok, x_ref, mod_ref)
    u2b_ref[...] = u2.astype(BF16)
    lg = lax.dot_general(wrt_ref[...], u2, (((1,), (1,)), ((), ())), preferred_element_type=F32,
                         precision=lax.Precision.HIGHEST) + brt_ref[:, 0:1]
    eidx = lax.broadcasted_iota(jnp.int32, lg.shape, 0).astype(F32)
    neg = jnp.float32(-jnp.inf)
    v1 = jnp.max(lg, axis=0, keepdims=True)
    i1 = jnp.min(jnp.where(lg == v1, eidx, float(N_EXP)), axis=0, keepdims=True)
    lg2 = jnp.where(eidx == i1, neg, lg)
    v2 = jnp.max(lg2, axis=0, keepdims=True)
    i2 = jnp.min(jnp.where(lg2 == v2, eidx, float(N_EXP)), axis=0, keepdims=True)
    t = jnp.exp(v2 - v1)
    w_top = 1.0 / (1.0 + t)
    m1 = eidx == i1
    m2 = eidx == i2
    member = jnp.where(jnp.logical_or(m1, m2), 1.0, 0.0)
    before = (lax.broadcasted_iota(jnp.int32, (TS, TS), 0)
              < lax.broadcasted_iota(jnp.int32, (TS, TS), 1))
    rank = jnp.dot(member.astype(BF16), jnp.where(before, 1.0, 0.0).astype(BF16),
                   preferred_element_type=F32)
    for e in range(N_EXP):
        cnt_ref[i, e] = jnp.sum(member[e:e + 1, :]).astype(jnp.int32)
    rank1 = jnp.sum(jnp.where(m1, rank, 0.0), axis=0, keepdims=True)
    rank2 = jnp.sum(jnp.where(m2, rank, 0.0), axis=0, keepdims=True)
    info_ref[0] = jnp.concatenate(
        [i1, i2, rank1, rank2, w_top, t * w_top, jnp.zeros((V7X_SUBLANES - 6, TS), F32)], axis=0)


def _rank(x, mod_l, wrt, brt, n_ctx_tok):
    n_tiles = x.shape[0] // TS
    return pl.pallas_call(
        functools.partial(_rank_kernel, n_ctx_tok),
        grid=(n_tiles,),
        in_specs=[
            pl.BlockSpec((TS, D), lambda i: (i, 0)),
            pl.BlockSpec((MOD_ROWS, N_MOD * D), lambda i: (0, 0)),
            pl.BlockSpec((N_EXP, D), lambda i: (0, 0)),
            pl.BlockSpec((N_EXP, V7X_LANES), lambda i: (0, 0)),
        ],
        out_specs=[
            pl.BlockSpec((1, V7X_SUBLANES, TS), lambda i: (i, 0, 0)),
            pl.BlockSpec(memory_space=pltpu.SMEM),
            pl.BlockSpec((TS, D), lambda i: (i, 0)),
        ],
        out_shape=[
            jax.ShapeDtypeStruct((n_tiles, V7X_SUBLANES, TS), F32),
            jax.ShapeDtypeStruct((n_tiles, N_EXP), jnp.int32),
            jax.ShapeDtypeStruct((n_tiles * TS, D), BF16),
        ],
        compiler_params=pltpu.CompilerParams(
            dimension_semantics=("arbitrary",), vmem_limit_bytes=VMEM_LIMIT),
        name="moe_rank",
    )(x, mod_l, wrt, brt)


def _dispatch_kernel(off_ref, loc_ref, cnt_ref, fill_ref, rinfo_ref, u_ref, xg_ref, info_ref,
                     sorted_scr, zero_scr, sem, zsem):
    i = pl.program_id(0)
    n_steps = pl.num_programs(0)
    slot = i % 2
    rinfo = rinfo_ref[0]
    i1 = rinfo[0:1]
    i2 = rinfo[1:2]
    loc1 = jnp.zeros_like(i1)
    loc2 = jnp.zeros_like(i2)
    for e in range(N_EXP):
        local = loc_ref[i * N_EXP + e].astype(F32)
        loc1 = jnp.where(i1 == float(e), local, loc1)
        loc2 = jnp.where(i2 == float(e), local, loc2)
    row1 = loc1 + rinfo[2:3]
    row2 = loc2 + rinfo[3:4]
    info_ref[0] = jnp.concatenate([row1, row2, rinfo[4:6], jnp.zeros((V7X_SUBLANES - 4, TS), F32)], axis=0)
    p_idx = lax.broadcasted_iota(jnp.int32, (SORT_ROWS, TS), 0).astype(F32)
    sel = jnp.where(jnp.logical_or(p_idx == row1, p_idx == row2), 1.0, 0.0).astype(BF16)
    sorted_scr[slot] = jnp.dot(sel, u_ref[...], preferred_element_type=F32)

    def segment_copies(tile, s, do):
        def make_copy(local, glob, size):
            return pltpu.make_async_copy(sorted_scr.at[s, pl.ds(local, size)], xg_ref.at[pl.ds(glob, size)],
                                         sem.at[s])
        _segment_copies(off_ref, loc_ref, cnt_ref, tile, make_copy, do)

    @pl.when(i > 0)
    def _():
        segment_copies(i - 1, 1 - slot, lambda cp: cp.wait())

    segment_copies(i, slot, lambda cp: cp.start())

    @pl.when(i == n_steps - 1)
    def _():
        segment_copies(i, slot, lambda cp: cp.wait())
        zero_scr[...] = jnp.zeros_like(zero_scr)
        for e in range(N_EXP):
            end = fill_ref[e]
            n_tail = fill_ref[N_EXP + e] // SEG_PAD

            def tail_copy(k):
                return pltpu.make_async_copy(
                    zero_scr.at[pl.ds(0, SEG_PAD)],
                    xg_ref.at[pl.ds(pl.multiple_of(end + k * SEG_PAD, SEG_PAD), SEG_PAD)], zsem)

            def tail_start(k, carry):
                tail_copy(k).start()
                return carry

            def tail_wait(k, carry):
                tail_copy(k).wait()
                return carry

            lax.fori_loop(0, n_tail, tail_start, 0)
            lax.fori_loop(0, n_tail, tail_wait, 0)

        def tile_copy(k):
            return pltpu.make_async_copy(zero_scr, xg_ref.at[pl.ds(pl.multiple_of(k * TR, TR), TR)], zsem)

        def tile_start(k, carry):
            tile_copy(k).start()
            return carry

        def tile_wait(k, carry):
            tile_copy(k).wait()
            return carry

        lax.fori_loop(fill_ref[2 * N_EXP], MAX_RT, tile_start, 0)
        lax.fori_loop(fill_ref[2 * N_EXP], MAX_RT, tile_wait, 0)


def _dispatch(u2b, rinfo, off, loc, cnt, fill):
    n_tiles = rinfo.shape[0]
    tile3 = lambda i, *_: (i, 0, 0)
    return pl.pallas_call(
        _dispatch_kernel,
        grid_spec=pltpu.PrefetchScalarGridSpec(
            num_scalar_prefetch=4,
            grid=(n_tiles,),
            in_specs=[
                pl.BlockSpec((1, V7X_SUBLANES, TS), tile3),
                pl.BlockSpec((TS, D), lambda i, *_: (i, 0)),
            ],
            out_specs=[
                pl.BlockSpec(memory_space=pl.ANY),
                pl.BlockSpec((1, V7X_SUBLANES, TS), tile3),
            ],
            scratch_shapes=[
                pltpu.VMEM((2, SORT_ROWS, D), F32),
                pltpu.VMEM((TR, D), F32),
                pltpu.SemaphoreType.DMA((2,)),
                pltpu.SemaphoreType.DMA,
            ],
        ),
        out_shape=[
            jax.ShapeDtypeStruct((MAX_RT * TR, D), F32),
            jax.ShapeDtypeStruct((n_tiles, V7X_SUBLANES, TS), F32),
        ],
        compiler_params=pltpu.CompilerParams(
            dimension_semantics=("arbitrary",), vmem_limit_bytes=VMEM_LIMIT),
        name="moe_dispatch",
    )(off, loc, cnt, fill, rinfo, u2b)


def _expert_kernel(second, exp_ref, new_ref, nact_ref, *refs):
    if second:
        xg_ref, w1_ref, w3_ref, w2_ref, yp_ref, o_ref, w13b, w2b = refs
    else:
        xg_ref, w1_ref, w3_ref, w2_ref, o_ref, w13b, w2b = refs
    r = pl.program_id(0)
    active = r < nact_ref[0]

    @pl.when(jnp.logical_and(active, new_ref[r] == 1))
    def _():
        w13b[:, 0:FC] = w1_ref[0].astype(BF16)
        w13b[:, FC:2 * FC] = w3_ref[0].astype(BF16)
        w2b[...] = w2_ref[0].astype(BF16)

    @pl.when(active)
    def _():
        xb = xg_ref[...].astype(BF16)
        h = jnp.dot(xb, w13b[...], preferred_element_type=F32)
        h1 = h[:, 0:FC]
        hid = (h1 * _sigmoid(h1) * h[:, FC:2 * FC]).astype(BF16)
        p = jnp.dot(hid, w2b[...], preferred_element_type=F32)
        if second:
            p = yp_ref[...] + p
        o_ref[...] = p

    @pl.when(jnp.logical_not(active))
    def _():
        o_ref[...] = jnp.zeros_like(o_ref)


def _expert_pass(second, tables, xg, w1, w3, w2, yp=None):
    f = 1 if second else 0
    row = lambda r, ex, nw, na: (r, 0)
    in_specs = [
        pl.BlockSpec((TR, D), row),
        pl.BlockSpec((1, D, FC), lambda r, ex, nw, na: (ex[r], 0, f)),
        pl.BlockSpec((1, D, FC), lambda r, ex, nw, na: (ex[r], 0, f)),
        pl.BlockSpec((1, FC, D), lambda r, ex, nw, na: (ex[r], f, 0)),
    ]
    args = [xg, w1, w3, w2]
    if second:
        in_specs.append(pl.BlockSpec((TR, D), row))
        args.append(yp)
    return pl.pallas_call(
        functools.partial(_expert_kernel, second),
        grid_spec=pltpu.PrefetchScalarGridSpec(
            num_scalar_prefetch=3,
            grid=(MAX_RT,),
            in_specs=in_specs,
            out_specs=pl.BlockSpec((TR, D), row),
            scratch_shapes=[pltpu.VMEM((D, 2 * FC), BF16), pltpu.VMEM((FC, D), BF16)],
        ),
        out_shape=jax.ShapeDtypeStruct((MAX_RT * TR, D), F32),
        compiler_params=pltpu.CompilerParams(
            dimension_semantics=("arbitrary",), vmem_limit_bytes=EXPERT_VMEM_LIMIT),
        name="moe_expert_hi" if second else "moe_expert_lo",
    )(*tables, *args)


def _combine_kernel(n_ctx_tok, off_ref, loc_ref, cnt_ref, x_ref, mod_ref, info_ref, y_ref, g_ref, b_ref,
                    oc_ref, od_ref, ys_scr, sem):
    i = pl.program_id(0)
    n_steps = pl.num_programs(0)
    slot = i % 2

    def fetch(tile, s, do):
        def make_copy(local, glob, size):
            return pltpu.make_async_copy(y_ref.at[pl.ds(glob, size)], ys_scr.at[s, pl.ds(local, size)],
                                         sem.at[s])
        _segment_copies(off_ref, loc_ref, cnt_ref, tile, make_copy, do)

    def start_fetch(tile, s):
        ys_scr[s] = jnp.zeros((SORT_ROWS, D), F32)
        fetch(tile, s, lambda cp: cp.start())

    @pl.when(i == 0)
    def _():
        start_fetch(0, 0)

    @pl.when(i + 1 < n_steps)
    def _():
        start_fetch(i + 1, 1 - slot)

    info = info_ref[0]
    row1, row2, w1, w2 = info[0:1], info[1:2], info[2:3], info[3:4]
    p_idx = lax.broadcasted_iota(jnp.int32, (SORT_ROWS, TS), 0).astype(F32)
    w_rows = jnp.sum(jnp.where(p_idx == row1, w1, 0.0) + jnp.where(p_idx == row2, w2, 0.0),
                     axis=1, keepdims=True)
    padded = jnp.concatenate([info, jnp.zeros((V7X_LANES - V7X_SUBLANES, TS), F32)], axis=0)
    cols = jnp.transpose(padded, (1, 0))
    q_idx = lax.broadcasted_iota(jnp.int32, (TS, SORT_ROWS), 1).astype(F32)
    pick = jnp.where(jnp.logical_or(q_idx == cols[:, 0:1], q_idx == cols[:, 1:2]), 1.0, 0.0).astype(BF16)
    fetch(i, slot, lambda cp: cp.wait())
    scaled = (ys_scr[slot] * w_rows).astype(BF16)
    acc = jnp.dot(pick, scaled, preferred_element_type=F32)
    res = _ffn_epilogue(n_ctx_tok, x_ref, mod_ref, acc, g_ref, b_ref)
    is_dec = i * TS >= n_ctx_tok

    @pl.when(jnp.logical_not(is_dec))
    def _():
        oc_ref[...] = res

    @pl.when(is_dec)
    def _():
        od_ref[...] = res


def _combine(x, mod_l, info, y, segs, ln_g, ln_b, n_ctx_tok):
    n_tok = x.shape[0]
    n_tiles = n_tok // TS
    ncb = n_ctx_tok // TS
    const = lambda i, *_: (0, 0)
    return pl.pallas_call(
        functools.partial(_combine_kernel, n_ctx_tok),
        grid_spec=pltpu.PrefetchScalarGridSpec(
            num_scalar_prefetch=3,
            grid=(n_tiles,),
            in_specs=[
                pl.BlockSpec((TS, D), lambda i, *_: (i, 0)),
                pl.BlockSpec((MOD_ROWS, N_MOD * D), const),
                pl.BlockSpec((1, V7X_SUBLANES, TS), lambda i, *_: (i, 0, 0)),
                pl.BlockSpec(memory_space=pl.ANY),
                pl.BlockSpec((1, D), const),
                pl.BlockSpec((1, D), const),
            ],
            out_specs=[
                pl.BlockSpec((TS, D), lambda i, *_: (jnp.minimum(i, ncb - 1), 0)),
                pl.BlockSpec((TS, D), lambda i, *_: (jnp.maximum(i - ncb, 0), 0)),
            ],
            scratch_shapes=[
                pltpu.VMEM((2, SORT_ROWS, D), F32),
                pltpu.SemaphoreType.DMA((2,)),
            ],
        ),
        out_shape=[
            jax.ShapeDtypeStruct((n_ctx_tok, D), F32),
            jax.ShapeDtypeStruct((n_tok - n_ctx_tok, D), F32),
        ],
        compiler_params=pltpu.CompilerParams(
            dimension_semantics=("arbitrary",), vmem_limit_bytes=VMEM_LIMIT),
        name="moe_combine",
    )(*segs, x, mod_l, info, y, ln_g, ln_b)


def _routing_tables(counts):
    counts = (counts + SEG_PAD - 1) // SEG_PAD * SEG_PAD
    totals = jnp.sum(counts, axis=0)
    n_rt = (totals + TR - 1) // TR
    cum = jnp.cumsum(n_rt)
    first = cum - n_rt
    n_act = cum[-1]
    off = first[None, :] * TR + (jnp.cumsum(counts, axis=0) - counts)
    loc = jnp.cumsum(counts, axis=1) - counts
    fill = jnp.concatenate([first * TR + totals, n_rt * TR - totals, n_act[None]])
    r = jnp.arange(MAX_RT, dtype=jnp.int32)
    rc = jnp.minimum(r, n_act - 1)
    exp = jnp.sum((rc[:, None] >= cum[None, :]).astype(jnp.int32), axis=1)
    new = jnp.logical_and(r == first[exp], r < n_act)
    i32 = lambda a: a.astype(jnp.int32)
    segs = (i32(off.reshape(-1)), i32(loc.reshape(-1)), i32(counts.reshape(-1)))
    return segs, i32(fill), (i32(exp), i32(new), i32(n_act.reshape(1)))


def _moe_ffn(x, mod_l, router_w, router_b, w1, w3, w2, ln_g, ln_b, n_ctx_tok):
    n_tok = x.shape[0]
    n_seg_pad = (n_tok // TS) * N_EXP * (SEG_PAD - 1)
    assert (2 * n_tok + n_seg_pad + TR - 1) // TR + N_EXP == MAX_RT and N_FC == 2
    wrt = router_w.T
    brt = jnp.broadcast_to(router_b.reshape(N_EXP, 1), (N_EXP, V7X_LANES))
    rinfo, counts, u2b = _rank(x, mod_l, wrt, brt, n_ctx_tok)
    segs, fill, tables = _routing_tables(counts)
    xg, info = _dispatch(u2b, rinfo, *segs, fill)
    y_lo = _expert_pass(False, tables, xg, w1, w3, w2)
    y = _expert_pass(True, tables, xg, w1, w3, w2, y_lo)
    return _combine(x, mod_l, info, y, segs, ln_g, ln_b, n_ctx_tok)


def _cast_kernel(*refs):
    o_ref = refs[-1]
    off = 0
    for x_ref in refs[:-1]:
        width = x_ref.shape[-1]
        o_ref[..., off:off + width] = x_ref[...].reshape(o_ref.shape[:-1] + (width,)).astype(o_ref.dtype)
        off += width


def _pack_w13(w1, w3):
    return pl.pallas_call(
        _cast_kernel,
        grid=(N_FC,),
        in_specs=[pl.BlockSpec((D, FC), lambda f: (0, f)), pl.BlockSpec((D, FC), lambda f: (0, f))],
        out_specs=pl.BlockSpec((1, D, 2 * FC), lambda f: (f, 0, 0)),
        out_shape=jax.ShapeDtypeStruct((N_FC, D, 2 * FC), BF16),
        compiler_params=pltpu.CompilerParams(vmem_limit_bytes=VMEM_LIMIT),
        name="pack_w13",
    )(w1, w3)


def _pack_w_in(w_in):
    depth = w_in.shape[0]
    def regroup_kernel(x_ref, o_ref):
        for j in range(N_CH):
            o_ref[0, j] = x_ref[0, :, j * CW:(j + 1) * CW].astype(BF16)

    w5 = pl.pallas_call(
        regroup_kernel,
        grid=(depth, 5),
        in_specs=[pl.BlockSpec((1, D, D), lambda l, g: (l, 0, g))],
        out_specs=pl.BlockSpec((1, N_CH, D, CW), lambda l, g: (l, 0, 0, g)),
        out_shape=jax.ShapeDtypeStruct((depth, N_CH, D, 5 * CW), BF16),
        compiler_params=pltpu.CompilerParams(vmem_limit_bytes=VMEM_LIMIT),
        name="pack_w5",
    )(w_in)
    wg = pl.pallas_call(
        _cast_kernel,
        grid=(depth, 2),
        in_specs=[pl.BlockSpec((1, D, D), lambda l, k: (l, 0, 5 + k))],
        out_specs=pl.BlockSpec((1, D, D), lambda l, k: (l, 0, k)),
        out_shape=jax.ShapeDtypeStruct((depth, D, 2 * D), BF16),
        compiler_params=pltpu.CompilerParams(vmem_limit_bytes=VMEM_LIMIT),
        name="pack_wg",
    )(w_in)
    return w5, wg


def _block_diag_chunks(w):
    tiled = jnp.tile(w.reshape(N_CH, CW, HEAD_D), (1, 1, CW // HEAD_D))
    blk = jnp.arange(CW) // HEAD_D
    return jnp.where(blk[:, None] == blk[None, :], tiled, 0.0)


def kernel(x_prompt, x_sample, state_rglru, c, c_ctx, w_mod, b_mod, w_in, conv_a, w_a_out, conv_b, conv_b_bias, w_gate_a, b_gate_a, w_gate_x, b_gate_x, lru_lambda, w_b_out, w_o, ln1_g, ln1_b, ln2_g, ln2_b, ffn_w1, ffn_w3, ffn_w2, router_w, router_b, moe_w1, moe_w3, moe_w2):
    batch, seq, d = x_prompt.shape
    dec_batch, dec_seq, _ = x_sample.shape
    depth = w_mod.shape[0]
    assert (d, seq, dec_seq, depth) == (D, SEQ, DEC_SEQ, DEPTH)
    n_ctx_tok = batch * seq
    n_dec_tok = dec_batch * dec_seq
    assert n_ctx_tok % TM1 == 0 and TM1 == dec_seq and 1 + dec_batch <= MOD_ROWS
    n_ctx_tiles = n_ctx_tok // TM1
    n_tiles = n_ctx_tiles + dec_batch
    seq_per_tile = TM1 // seq
    assert N_SEG == 2 * seq_per_tile

    assert depth % 2 == 0
    n_tok = n_ctx_tok + n_dec_tok
    x = (x_prompt.reshape(n_ctx_tok, D), x_sample.reshape(n_dec_tok, D))

    cond = jnp.zeros((MOD_ROWS, D), F32).at[0].set(c_ctx).at[1:1 + dec_batch].set(c)
    mod = _modulation(cond, w_mod, b_mod)

    w5, wg = _pack_w_in(w_in)
    states = []
    for l in range(depth):
        wbd = jnp.concatenate(
            [_block_diag_chunks(w_gate_a[l, 0]), _block_diag_chunks(w_gate_x[l, 0]),
             _block_diag_chunks(w_gate_a[l, 1]), _block_diag_chunks(w_gate_x[l, 1])],
            axis=-1).astype(BF16)
        gbias = jnp.stack([b_gate_a[l, 0], b_gate_x[l, 0], b_gate_a[l, 1], b_gate_x[l, 1]], axis=0)
        h0 = jnp.zeros((n_tiles, 2, N_SEG, D), F32)
        h0 = h0.at[n_ctx_tiles:, 0, 0].set(state_rglru[:, l, 0].astype(F32))
        h0 = h0.at[n_ctx_tiles:, 1, N_SEG - 1].set(state_rglru[:, l, 1].astype(F32))

        a_pre, b_pre, st = _mixer_part1(
            x, mod[l], l, w5, conv_a[l], conv_b[l], conv_b_bias[l].reshape(1, D), wbd, gbias,
            lru_lambda[l], h0, n_tok, n_ctx_tok)
        x = _mixer_part2(
            x, mod[l], l, a_pre, b_pre, wg, w_a_out[l].astype(BF16), w_b_out[l].astype(BF16),
            w_o[l].astype(BF16), ln1_g[l].reshape(1, D), ln1_b[l].reshape(1, D), n_ctx_tok)

        if l % 2 == 0:
            k = l // 2
            x = _dense_ffn(x, mod[l], _pack_w13(ffn_w1[k], ffn_w3[k]), ffn_w2[k].astype(BF16),
                           ln2_g[l].reshape(1, D), ln2_b[l].reshape(1, D), n_ctx_tok)
        else:
            k = l // 2
            x = _moe_ffn(x, mod[l], router_w[k], router_b[k], moe_w1[k], moe_w3[k], moe_w2[k],
                         ln2_g[l].reshape(1, D), ln2_b[l].reshape(1, D), n_ctx_tok)

        st_ctx = st[:n_ctx_tiles]
        fwd = st_ctx[:, 0, 1::2].reshape(batch, D)
        bwd = st_ctx[:, 1, 0::2].reshape(batch, D)
        states.append(jnp.stack([fwd, bwd], axis=1))

    y_prompt = x[0].reshape(batch, seq, D)
    y_sample = x[1].reshape(dec_batch, dec_seq, D)
    new_state = jnp.stack(states, axis=1).astype(x_prompt.dtype)
    return (y_prompt, y_sample, new_state)
```

```python
import functools

import jax
import jax.numpy as jnp
from jax import lax
from jax.experimental import pallas as pl
from jax.experimental.pallas import tpu as pltpu

F32 = jnp.float32
BF16 = jnp.bfloat16

D = 1024
SEQ = 256
DEC_SEQ = 1024
GRID_W = 64
N_HEAD = 16
HEAD_D = D // N_HEAD
RGLRU_C = 8.0
LOG2_E = 1.4426950408889634
D_FF = 2816
N_EXP = 8
N_MOD = 6
DEPTH = 2
DN_ALPHA = (2.0 * DEPTH) ** 0.25

V7X_SUBLANES = 8
V7X_LANES = 128
V7X_VMEM_BYTES = 64 * 1024 * 1024
VMEM_LIMIT = V7X_VMEM_BYTES - 12 * 1024 * 1024

TM1 = 1024
CW = 256
N_CH = D // CW
N_LC = CW // V7X_LANES
N_SEG = V7X_SUBLANES
SEG = TM1 // N_SEG
SEG_STRIDE = SEG + 4
CONV_GAP = V7X_SUBLANES
TM2 = 512
TM3 = 512
FC = 1408
N_FC = D_FF // FC
TS = 512
TR = 256
SEG_PAD = V7X_SUBLANES
SEG_BITS = 7
SORT_ROWS = 2 * TS + V7X_LANES
MAX_RT = (2 * 10240 + (10240 // TS) * N_EXP * (SEG_PAD - 1) + TR - 1) // TR + N_EXP
EXPERT_VMEM_LIMIT = V7X_VMEM_BYTES - 6 * 1024 * 1024
MOD_ROWS = 8
MOD_BLK = 3072


def _sigmoid(x):
    return 0.5 * jnp.tanh(0.5 * x) + 0.5


def _ln_plain(x, eps):
    mu = jnp.mean(x, axis=-1, keepdims=True)
    xc = x - mu
    var = jnp.mean(xc * xc, axis=-1, keepdims=True)
    return xc * lax.rsqrt(var + eps)


def _token_source(xs, tile, n_ctx_tok):
    ncb = n_ctx_tok // tile
    if isinstance(xs, tuple):
        x_ctx, x_dec = xs
        dec_off = 0
    else:
        x_ctx = x_dec = xs
        dec_off = ncb
    ctx_map = lambda i: (jnp.minimum(i, ncb - 1), 0)
    dec_map = lambda i: (jnp.maximum(i - ncb, 0) + dec_off, 0)
    return x_ctx, x_dec, ctx_map, dec_map


def _mod_row(tok0, n_ctx_tok):
    dec = jnp.maximum(tok0 - n_ctx_tok, 0) // DEC_SEQ
    return jnp.where(tok0 >= n_ctx_tok, 1 + dec, 0)


def _mod_kernel(cond_ref, w_ref, b_ref, o_ref):
    cnd = cond_ref[...]
    s = cnd * _sigmoid(cnd)
    o_ref[0] = jnp.dot(s, w_ref[0], preferred_element_type=F32,
                       precision=lax.Precision.HIGHEST) + b_ref[0]


def _modulation(cond, w_mod, b_mod):
    depth = w_mod.shape[0]
    n_out = w_mod.shape[2]
    return pl.pallas_call(
        _mod_kernel,
        grid=(depth, n_out // MOD_BLK),
        in_specs=[
            pl.BlockSpec((MOD_ROWS, D), lambda l, j: (0, 0)),
            pl.BlockSpec((1, D, MOD_BLK), lambda l, j: (l, 0, j)),
            pl.BlockSpec((1, 1, MOD_BLK), lambda l, j: (l, 0, j)),
        ],
        out_specs=pl.BlockSpec((1, MOD_ROWS, MOD_BLK), lambda l, j: (l, 0, j)),
        out_shape=jax.ShapeDtypeStruct((depth, MOD_ROWS, n_out), F32),
        compiler_params=pltpu.CompilerParams(
            dimension_semantics=("arbitrary", "arbitrary"), vmem_limit_bytes=VMEM_LIMIT),
        name="modulation",
    )(cond, w_mod, b_mod.reshape(depth, 1, n_out))


def _scan_dir(a_scr, b_scr, hl_scr, ac_scr, h0, keep, reverse):
    n_lc = a_scr.shape[0]

    def body(k, carry):
        kk = SEG - 1 - k if reverse else k
        idx = pl.ds(kk, N_SEG, stride=SEG_STRIDE)
        new = []
        for c in range(n_lc):
            h, acc = carry[c]
            a_k = a_scr[c, idx, :]
            h = a_k * h + b_scr[c, idx, :]
            acc = a_k * acc
            hl_scr[c, idx, :] = h
            ac_scr[c, idx, :] = acc
            new.append((h, acc))
        return tuple(new)

    init = tuple((jnp.zeros((N_SEG, V7X_LANES), F32), jnp.ones((N_SEG, V7X_LANES), F32))
                 for _ in range(n_lc))
    fin = lax.fori_loop(0, SEG, body, init, unroll=8)
    h_loc = jnp.concatenate([fin[c][0] for c in range(n_lc)], axis=1)
    a_tot = jnp.concatenate([fin[c][1] for c in range(n_lc)], axis=1)

    order = range(N_SEG - 1, -1, -1) if reverse else range(N_SEG)
    h_in = [None] * N_SEG
    prev = None
    for s in order:
        cur = h0[s:s + 1]
        if prev is not None:
            left = a_tot[prev:prev + 1] * h_in[prev] + h_loc[prev:prev + 1]
            cur = keep[s] * left + cur
        h_in[s] = cur
        prev = s
    h_in = jnp.concatenate(h_in, axis=0)
    return h_in, a_tot * h_in + h_loc


def _conv_stage(is_dec, proj_scr, ca_ref, cb_ref, cbias_ref, apre_ref, xr_scr, gap_scr):
    win = GRID_W
    n_win = TM1 // win
    stride = win + CONV_GAP
    ca = ca_ref[...]
    cb = cb_ref[...]
    bias = cbias_ref[...]
    pieces = [(w, c) for w in range(n_win) for c in range(N_LC)]
    sub = lax.broadcasted_iota(jnp.int32, (CONV_GAP, V7X_LANES), 0)
    joined = jnp.where(is_dec, jnp.float32(0.0), jnp.float32(1.0))
    zero_gap = jnp.zeros((CONV_GAP, V7X_LANES), F32)

    def stage(value_of):
        for c in range(N_LC):
            for w in range(n_win + 1):
                edge = w * win
                if edge % SEQ == 0:
                    gap = zero_gap
                else:
                    before = value_of(slice(edge - CONV_GAP, edge), c)
                    after = value_of(slice(edge, edge + CONV_GAP), c)
                    gap = joined * jnp.where(sub >= CONV_GAP - 2, before, jnp.where(sub == 0, after, 0.0))
                gap_scr[c, w * stride:w * stride + CONV_GAP, :] = gap
        for w, c in pieces:
            lo = CONV_GAP + w * stride
            gap_scr[c, lo:lo + win, :] = value_of(slice(w * win, (w + 1) * win), c)

    def tap(w, c, shift):
        lo = CONV_GAP + w * stride + shift
        return gap_scr[c, lo:lo + win, :]

    def chunk_cols(k, c):
        return slice(k * CW + c * V7X_LANES, k * CW + (c + 1) * V7X_LANES)

    stage(lambda rows, c: proj_scr[rows, chunk_cols(2, c)] * proj_scr[rows, chunk_cols(0, c)])
    for w, c in pieces:
        rows = slice(w * win, (w + 1) * win)
        lanes = slice(c * V7X_LANES, (c + 1) * V7X_LANES)
        conv = ca[0:1, lanes] * tap(w, c, -1) + ca[1:2, lanes] * tap(w, c, 0) + ca[2:3, lanes] * tap(w, c, 1)
        apre_ref[rows, lanes] = (proj_scr[rows, chunk_cols(1, c)] * conv).astype(BF16)
    stage(lambda rows, c: proj_scr[rows, chunk_cols(4, c)])
    for w, c in pieces:
        rows = slice(w * win, (w + 1) * win)
        lanes = slice(c * V7X_LANES, (c + 1) * V7X_LANES)
        xr_scr[rows, lanes] = (cb[0:1, lanes] * tap(w, c, -2) + cb[1:2, lanes] * tap(w, c, -1)
                               + cb[2:3, lanes] * tap(w, c, 0) + cb[3:4, lanes] * tap(w, c, 1)
                               + bias[:, lanes])


def _mix1_kernel(n_ctx_tiles, xc_ref, xd_ref, mod_ref, w5_ref, ca_ref, cb_ref, cbias_ref, wbd_ref, gb_ref,
                 lam_ref, h0_ref, apre_ref, bpre_ref, st_ref,
                 u_scr, proj_scr, a_scr, b_scr, hl_scr, ac_scr, hsum_scr, xr_scr, gap_scr):
    i = pl.program_id(0)
    j = pl.program_id(1)
    is_dec = i >= n_ctx_tiles

    @pl.when(j == 0)
    def _():
        row = jnp.where(is_dec, i - (n_ctx_tiles - 1), 0)
        m = mod_ref[pl.ds(row, 1), :]
        sh1 = m[:, 0:D]
        sc1 = m[:, D:2 * D]
        x = jnp.where(is_dec, xd_ref[...], xc_ref[...])
        u_scr[...] = (_ln_plain(x, 1e-6) * (1.0 + sc1) + sh1).astype(BF16)

    proj_scr[...] = jnp.dot(u_scr[...], w5_ref[0, 0], preferred_element_type=F32)

    _conv_stage(is_dec, proj_scr, ca_ref, cb_ref, cbias_ref, apre_ref, xr_scr, gap_scr)

    xr = xr_scr[...]
    gates = jnp.dot(xr.astype(BF16), wbd_ref[0], preferred_element_type=F32)
    gb = gb_ref[...]
    lam = lam_ref[...]
    sp = jnp.maximum(-lam, 0.0) + jnp.log1p(jnp.exp(-jnp.abs(lam)))
    rate = (-RGLRU_C * LOG2_E) * sp

    one = jnp.float32(1.0)
    for d in range(2):
        ga = gates[:, (2 * d) * CW:(2 * d + 1) * CW] + gb[2 * d:2 * d + 1]
        gx = gates[:, (2 * d + 1) * CW:(2 * d + 2) * CW] + gb[2 * d + 1:2 * d + 2]
        r = _sigmoid(ga)
        ig = _sigmoid(gx)
        a = jnp.exp2(r * rate[d:d + 1])
        y = 1.0 - a * a
        bt = jnp.where(y > 0.0, y * lax.rsqrt(y), 0.0) * (ig * xr)
        for s in range(N_SEG):
            lo = s * SEG_STRIDE
            for c in range(N_LC):
                lanes = slice(c * V7X_LANES, (c + 1) * V7X_LANES)
                a_scr[c, lo:lo + SEG, :] = a[s * SEG:(s + 1) * SEG, lanes]
                b_scr[c, lo:lo + SEG, :] = bt[s * SEG:(s + 1) * SEG, lanes]
        if d == 0:
            keep = [jnp.where(is_dec, one, jnp.float32(s % 2 == 1)) for s in range(N_SEG)]
        else:
            keep = [jnp.where(is_dec, one, jnp.float32(s % 2 == 0)) for s in range(N_SEG)]
        h_in, h_out = _scan_dir(a_scr, b_scr, hl_scr, ac_scr, h0_ref[0, d], keep, reverse=(d == 1))
        st_ref[0, d] = h_out
        for s in range(N_SEG):
            lo = s * SEG_STRIDE
            for c in range(N_LC):
                lanes = slice(c * V7X_LANES, (c + 1) * V7X_LANES)
                h_seg = hl_scr[c, lo:lo + SEG, :] + ac_scr[c, lo:lo + SEG, :] * h_in[s:s + 1, lanes]
                if d == 0:
                    hsum_scr[s * SEG:(s + 1) * SEG, lanes] = h_seg
                else:
                    hsum_scr[s * SEG:(s + 1) * SEG, lanes] += h_seg

    bpre_ref[...] = (hsum_scr[...] * jax.nn.gelu(proj_scr[:, 3 * CW:4 * CW])).astype(BF16)


def _mixer_part1(xs, mod_l, layer, w5, conv_a, conv_b, conv_b_bias, wbd, gbias, lam, h0, n_tok, n_ctx_tok):
    n_tiles = n_tok // TM1
    x_ctx, x_dec, ctx_map, dec_map = _token_source(xs, TM1, n_ctx_tok)
    kern = functools.partial(_mix1_kernel, n_ctx_tok // TM1)
    return pl.pallas_call(
        kern,
        grid=(n_tiles, N_CH),
        in_specs=[
            pl.BlockSpec((TM1, D), lambda i, j: ctx_map(i)),
            pl.BlockSpec((TM1, D), lambda i, j: dec_map(i)),
            pl.BlockSpec((MOD_ROWS, N_MOD * D), lambda i, j: (0, 0)),
            pl.BlockSpec((1, 1, D, 5 * CW), lambda i, j: (layer, j, 0, 0)),
            pl.BlockSpec((3, CW), lambda i, j: (0, j)),
            pl.BlockSpec((4, CW), lambda i, j: (0, j)),
            pl.BlockSpec((1, CW), lambda i, j: (0, j)),
            pl.BlockSpec((1, CW, 4 * CW), lambda i, j: (j, 0, 0)),
            pl.BlockSpec((4, CW), lambda i, j: (0, j)),
            pl.BlockSpec((2, CW), lambda i, j: (0, j)),
            pl.BlockSpec((1, 2, N_SEG, CW), lambda i, j: (i, 0, 0, j)),
        ],
        out_specs=[
            pl.BlockSpec((TM1, CW), lambda i, j: (i, j)),
            pl.BlockSpec((TM1, CW), lambda i, j: (i, j)),
            pl.BlockSpec((1, 2, N_SEG, CW), lambda i, j: (i, 0, 0, j)),
        ],
        out_shape=[
            jax.ShapeDtypeStruct((n_tok, D), BF16),
            jax.ShapeDtypeStruct((n_tok, D), BF16),
            jax.ShapeDtypeStruct((n_tiles, 2, N_SEG, D), F32),
        ],
        scratch_shapes=[
            pltpu.VMEM((TM1, D), BF16),
            pltpu.VMEM((TM1, 5 * CW), F32),
            pltpu.VMEM((N_LC, N_SEG * SEG_STRIDE, V7X_LANES), F32),
            pltpu.VMEM((N_LC, N_SEG * SEG_STRIDE, V7X_LANES), F32),
            pltpu.VMEM((N_LC, N_SEG * SEG_STRIDE, V7X_LANES), F32),
            pltpu.VMEM((N_LC, N_SEG * SEG_STRIDE, V7X_LANES), F32),
            pltpu.VMEM((TM1, CW), F32),
            pltpu.VMEM((TM1, CW), F32),
            pltpu.VMEM((N_LC, CONV_GAP + (TM1 // GRID_W) * (GRID_W + CONV_GAP), V7X_LANES), F32),
        ],
        compiler_params=pltpu.CompilerParams(
            dimension_semantics=("arbitrary", "arbitrary"), vmem_limit_bytes=VMEM_LIMIT),
        name="mixer_scan",
    )(x_ctx, x_dec, mod_l, w5, conv_a, conv_b, conv_b_bias, wbd, gbias, lam, h0)


def _mix2_kernel(n_ctx_tok, xc_ref, xd_ref, mod_ref, ap_ref, bp_ref, wg_ref, wa_ref, wb_ref, wo_ref,
                 g_ref, b_ref, o_ref):
    i = pl.program_id(0)
    m = mod_ref[pl.ds(_mod_row(i * TM2, n_ctx_tok), 1), :]
    sh1 = m[:, 0:D]
    sc1 = m[:, D:2 * D]
    g1 = m[:, 2 * D:3 * D]
    x = jnp.where(i * TM2 >= n_ctx_tok, xd_ref[...], xc_ref[...])
    u = (_ln_plain(x, 1e-6) * (1.0 + sc1) + sh1).astype(BF16)
    gates = jnp.dot(u, wg_ref[0], preferred_element_type=F32)
    br_a = jnp.dot(ap_ref[...], wa_ref[...], preferred_element_type=F32)
    br_b = jnp.dot(bp_ref[...], wb_ref[...], preferred_element_type=F32)
    merged = _sigmoid(gates[:, 0:D]) * br_a + _sigmoid(gates[:, D:2 * D]) * br_b
    mix = jnp.dot(merged.astype(BF16), wo_ref[...], preferred_element_type=F32)
    y = DN_ALPHA * x + g1 * mix
    o_ref[...] = _ln_plain(y, 1e-5) * g_ref[...] + b_ref[...]


def _mixer_part2(xs, mod_l, layer, a_pre, b_pre, wg, wa, wb, wo, ln_g, ln_b, n_ctx_tok):
    n_tok = a_pre.shape[0]
    x_ctx, x_dec, ctx_map, dec_map = _token_source(xs, TM2, n_ctx_tok)
    const = lambda i: (0, 0)
    tile = lambda i: (i, 0)
    return pl.pallas_call(
        functools.partial(_mix2_kernel, n_ctx_tok),
        grid=(n_tok // TM2,),
        in_specs=[
            pl.BlockSpec((TM2, D), ctx_map),
            pl.BlockSpec((TM2, D), dec_map),
            pl.BlockSpec((MOD_ROWS, N_MOD * D), const),
            pl.BlockSpec((TM2, D), tile),
            pl.BlockSpec((TM2, D), tile),
            pl.BlockSpec((1, D, 2 * D), lambda i: (layer, 0, 0)),
            pl.BlockSpec((D, D), const),
            pl.BlockSpec((D, D), const),
            pl.BlockSpec((D, D), const),
            pl.BlockSpec((1, D), const),
            pl.BlockSpec((1, D), const),
        ],
        out_specs=pl.BlockSpec((TM2, D), tile),
        out_shape=jax.ShapeDtypeStruct((n_tok, D), F32),
        compiler_params=pltpu.CompilerParams(
            dimension_semantics=("arbitrary",), vmem_limit_bytes=VMEM_LIMIT),
        name="mixer_out",
    )(x_ctx, x_dec, mod_l, a_pre, b_pre, wg, wa, wb, wo, ln_g, ln_b)


def _ffn_prologue(n_ctx_tok, x_ref, mod_ref):
    m = mod_ref[pl.ds(_mod_row(pl.program_id(0) * TM3, n_ctx_tok), 1), :]
    sh2 = m[:, 3 * D:4 * D]
    sc2 = m[:, 4 * D:5 * D]
    return _ln_plain(x_ref[...], 1e-6) * (1.0 + sc2) + sh2


def _ffn_epilogue(n_ctx_tok, x_ref, mod_ref, acc, g_ref, b_ref):
    m = mod_ref[pl.ds(_mod_row(pl.program_id(0) * TM3, n_ctx_tok), 1), :]
    g2 = m[:, 5 * D:6 * D]
    y = DN_ALPHA * x_ref[...] + g2 * acc
    return _ln_plain(y, 1e-5) * g_ref[...] + b_ref[...]


def _swiglu_chunk(u, w13, w2):
    h = jnp.dot(u, w13, preferred_element_type=F32)
    h1 = h[:, 0:FC]
    hid = (h1 * _sigmoid(h1) * h[:, FC:2 * FC]).astype(BF16)
    return jnp.dot(hid, w2, preferred_element_type=F32)


def _dense_ffn_kernel(n_ctx_tok, x_ref, mod_ref, w13_ref, w2_ref, g_ref, b_ref, o_ref, u_scr, acc_scr):
    f = pl.program_id(1)

    @pl.when(f == 0)
    def _():
        u_scr[...] = _ffn_prologue(n_ctx_tok, x_ref, mod_ref).astype(BF16)

    p = _swiglu_chunk(u_scr[...], w13_ref[0], w2_ref[...])

    @pl.when(f == 0)
    def _():
        acc_scr[...] = p

    @pl.when(f > 0)
    def _():
        acc_scr[...] += p

    @pl.when(f == N_FC - 1)
    def _():
        o_ref[...] = _ffn_epilogue(n_ctx_tok, x_ref, mod_ref, acc_scr[...], g_ref, b_ref)


def _dense_ffn(x, mod_l, w13, w2, ln_g, ln_b, n_ctx_tok):
    n_tok = x.shape[0]
    return pl.pallas_call(
        functools.partial(_dense_ffn_kernel, n_ctx_tok),
        grid=(n_tok // TM3, N_FC),
        in_specs=[
            pl.BlockSpec((TM3, D), lambda i, f: (i, 0)),
            pl.BlockSpec((MOD_ROWS, N_MOD * D), lambda i, f: (0, 0)),
            pl.BlockSpec((1, D, 2 * FC), lambda i, f: (f, 0, 0)),
            pl.BlockSpec((FC, D), lambda i, f: (f, 0)),
            pl.BlockSpec((1, D), lambda i, f: (0, 0)),
            pl.BlockSpec((1, D), lambda i, f: (0, 0)),
        ],
        out_specs=pl.BlockSpec((TM3, D), lambda i, f: (i, 0)),
        out_shape=jax.ShapeDtypeStruct((n_tok, D), F32),
        scratch_shapes=[pltpu.VMEM((TM3, D), BF16), pltpu.VMEM((TM3, D), F32)],
        compiler_params=pltpu.CompilerParams(
            dimension_semantics=("arbitrary", "arbitrary"), vmem_limit_bytes=VMEM_LIMIT),
        name="dense_ffn",
    )(x, mod_l, w13, w2, ln_g, ln_b)


def _segment_copies(off_ref, loc_ref, cnt_ref, tile, make_copy, do):
    for e in range(N_EXP):
        n = cnt_ref[tile * N_EXP + e]
        local = loc_ref[tile * N_EXP + e]
        glob = off_ref[tile * N_EXP + e]
        for k in range(SEG_BITS - 1, -1, -1):
            size = SEG_PAD << k
            take = (n & size) != 0

            @pl.when(take)
            def _():
                do(make_copy(pl.multiple_of(local, SEG_PAD), pl.multiple_of(glob, SEG_PAD), size))

            step = jnp.where(take, size, 0)
            local = local + step
            glob = glob + step


def _rank_kernel(n_ctx_tok, x_ref, mod_ref, wrt_ref, brt_ref, info_ref, cnt_ref, u2b_ref):
    i = pl.program_id(0)
    u2 = _ffn_prologue(n_ctx_tok, x_ref, mod_ref)
    u2b_ref[...] = u2.astype(BF16)
    lg = lax.dot_general(wrt_ref[...], u2, (((1,), (1,)), ((), ())), preferred_element_type=F32,
                         precision=lax.Precision.HIGHEST) + brt_ref[:, 0:1]
    eidx = lax.broadcasted_iota(jnp.int32, lg.shape, 0).astype(F32)
    neg = jnp.float32(-jnp.inf)
    v1 = jnp.max(lg, axis=0, keepdims=True)
    i1 = jnp.min(jnp.where(lg == v1, eidx, float(N_EXP)), axis=0, keepdims=True)
    lg2 = jnp.where(eidx == i1, neg, lg)
    v2 = jnp.max(lg2, axis=0, keepdims=True)
    i2 = jnp.min(jnp.where(lg2 == v2, eidx, float(N_EXP)), axis=0, keepdims=True)
    t = jnp.exp(v2 - v1)
    w_top = 1.0 / (1.0 + t)
    m1 = eidx == i1
    m2 = eidx == i2
    member = jnp.where(jnp.logical_or(m1, m2), 1.0, 0.0)
    before = (lax.broadcasted_iota(jnp.int32, (TS, TS), 0)
              < lax.broadcasted_iota(jnp.int32, (TS, TS), 1))
    rank = jnp.dot(member.astype(BF16), jnp.where(before, 1.0, 0.0).astype(BF16),
                   preferred_element_type=F32)
    for e in range(N_EXP):
        cnt_ref[i, e] = jnp.sum(member[e:e + 1, :]).astype(jnp.int32)
    rank1 = jnp.sum(jnp.where(m1, rank, 0.0), axis=0, keepdims=True)
    rank2 = jnp.sum(jnp.where(m2, rank, 0.0), axis=0, keepdims=True)
    info_ref[0] = jnp.concatenate(
        [i1, i2, rank1, rank2, w_top, t * w_top, jnp.zeros((V7X_SUBLANES - 6, TS), F32)], axis=0)


def _rank(x, mod_l, wrt, brt, n_ctx_tok):
    n_tiles = x.shape[0] // TS
    return pl.pallas_call(
        functools.partial(_rank_kernel, n_ctx_tok),
        grid=(n_tiles,),
        in_specs=[
            pl.BlockSpec((TS, D), lambda i: (i, 0)),
            pl.BlockSpec((MOD_ROWS, N_MOD * D), lambda i: (0, 0)),
            pl.BlockSpec((N_EXP, D), lambda i: (0, 0)),
            pl.BlockSpec((N_EXP, V7X_LANES), lambda i: (0, 0)),
        ],
        out_specs=[
            pl.BlockSpec((1, V7X_SUBLANES, TS), lambda i: (i, 0, 0)),
            pl.BlockSpec(memory_space=pltpu.SMEM),
            pl.BlockSpec((TS, D), lambda i: (i, 0)),
        ],
        out_shape=[
            jax.ShapeDtypeStruct((n_tiles, V7X_SUBLANES, TS), F32),
            jax.ShapeDtypeStruct((n_tiles, N_EXP), jnp.int32),
            jax.ShapeDtypeStruct((n_tiles * TS, D), BF16),
        ],
        compiler_params=pltpu.CompilerParams(
            dimension_semantics=("arbitrary",), vmem_limit_bytes=VMEM_LIMIT),
        name="moe_rank",
    )(x, mod_l, wrt, brt)


def _dispatch_kernel(off_ref, loc_ref, cnt_ref, fill_ref, rinfo_ref, u_ref, xg_ref, info_ref,
                     sorted_scr, zero_scr, sem, zsem):
    i = pl.program_id(0)
    n_steps = pl.num_programs(0)
    slot = i % 2
    rinfo = rinfo_ref[0]
    i1 = rinfo[0:1]
    i2 = rinfo[1:2]
    loc1 = jnp.zeros_like(i1)
    loc2 = jnp.zeros_like(i2)
    for e in range(N_EXP):
        local = loc_ref[i * N_EXP + e].astype(F32)
        loc1 = jnp.where(i1 == float(e), local, loc1)
        loc2 = jnp.where(i2 == float(e), local, loc2)
    row1 = loc1 + rinfo[2:3]
    row2 = loc2 + rinfo[3:4]
    info_ref[0] = jnp.concatenate([row1, row2, rinfo[4:6], jnp.zeros((V7X_SUBLANES - 4, TS), F32)], axis=0)
    p_idx = lax.broadcasted_iota(jnp.int32, (SORT_ROWS, TS), 0).astype(F32)
    sel = jnp.where(jnp.logical_or(p_idx == row1, p_idx == row2), 1.0, 0.0).astype(BF16)
    sorted_scr[slot] = jnp.dot(sel, u_ref[...], preferred_element_type=F32)

    def segment_copies(tile, s, do):
        def make_copy(local, glob, size):
            return pltpu.make_async_copy(sorted_scr.at[s, pl.ds(local, size)], xg_ref.at[pl.ds(glob, size)],
                                         sem.at[s])
        _segment_copies(off_ref, loc_ref, cnt_ref, tile, make_copy, do)

    @pl.when(i > 0)
    def _():
        segment_copies(i - 1, 1 - slot, lambda cp: cp.wait())

    segment_copies(i, slot, lambda cp: cp.start())

    @pl.when(i == n_steps - 1)
    def _():
        segment_copies(i, slot, lambda cp: cp.wait())
        zero_scr[...] = jnp.zeros_like(zero_scr)
        for e in range(N_EXP):
            end = fill_ref[e]
            n_tail = fill_ref[N_EXP + e] // SEG_PAD

            def tail_copy(k):
                return pltpu.make_async_copy(
                    zero_scr.at[pl.ds(0, SEG_PAD)],
                    xg_ref.at[pl.ds(pl.multiple_of(end + k * SEG_PAD, SEG_PAD), SEG_PAD)], zsem)

            def tail_start(k, carry):
                tail_copy(k).start()
                return carry

            def tail_wait(k, carry):
                tail_copy(k).wait()
                return carry

            lax.fori_loop(0, n_tail, tail_start, 0)
            lax.fori_loop(0, n_tail, tail_wait, 0)

        def tile_copy(k):
            return pltpu.make_async_copy(zero_scr, xg_ref.at[pl.ds(pl.multiple_of(k * TR, TR), TR)], zsem)

        def tile_start(k, carry):
            tile_copy(k).start()
            return carry

        def tile_wait(k, carry):
            tile_copy(k).wait()
            return carry

        lax.fori_loop(fill_ref[2 * N_EXP], MAX_RT, tile_start, 0)
        lax.fori_loop(fill_ref[2 * N_EXP], MAX_RT, tile_wait, 0)


def _dispatch(u2b, rinfo, off, loc, cnt, fill):
    n_tiles = rinfo.shape[0]
    tile3 = lambda i, *_: (i, 0, 0)
    return pl.pallas_call(
        _dispatch_kernel,
        grid_spec=pltpu.PrefetchScalarGridSpec(
            num_scalar_prefetch=4,
            grid=(n_tiles,),
            in_specs=[
                pl.BlockSpec((1, V7X_SUBLANES, TS), tile3),
                pl.BlockSpec((TS, D), lambda i, *_: (i, 0)),
            ],
            out_specs=[
                pl.BlockSpec(memory_space=pl.ANY),
                pl.BlockSpec((1, V7X_SUBLANES, TS), tile3),
            ],
            scratch_shapes=[
                pltpu.VMEM((2, SORT_ROWS, D), F32),
                pltpu.VMEM((TR, D), F32),
                pltpu.SemaphoreType.DMA((2,)),
                pltpu.SemaphoreType.DMA,
            ],
        ),
        out_shape=[
            jax.ShapeDtypeStruct((MAX_RT * TR, D), F32),
            jax.ShapeDtypeStruct((n_tiles, V7X_SUBLANES, TS), F32),
        ],
        compiler_params=pltpu.CompilerParams(
            dimension_semantics=("arbitrary",), vmem_limit_bytes=VMEM_LIMIT),
        name="moe_dispatch",
    )(off, loc, cnt, fill, rinfo, u2b)


def _expert_kernel(second, exp_ref, new_ref, nact_ref, *refs):
    if second:
        xg_ref, w1_ref, w3_ref, w2_ref, yp_ref, o_ref, w13b, w2b = refs
    else:
        xg_ref, w1_ref, w3_ref, w2_ref, o_ref, w13b, w2b = refs
    s = pl.program_id(0)
    r = s - 1
    active = jnp.logical_and(s >= 1, r < nact_ref[0])

    @pl.when(jnp.logical_and(active, new_ref[jnp.maximum(r, 0)] == 1))
    def _():
        w2b[...] = w2_ref[0].astype(BF16)

    @pl.when(active)
    def _():
        xb = xg_ref[...].astype(BF16)
        h = jnp.dot(xb, w13b[...], preferred_element_type=F32)
        h1 = h[:, 0:FC]
        hid = (h1 * _sigmoid(h1) * h[:, FC:2 * FC]).astype(BF16)
        p = jnp.dot(hid, w2b[...], preferred_element_type=F32)
        if second:
            p = yp_ref[...] + p
        o_ref[...] = p

    @pl.when(jnp.logical_and(s >= 1, jnp.logical_not(active)))
    def _():
        o_ref[...] = jnp.zeros_like(o_ref)

    @pl.when(new_ref[s] == 1)
    def _():
        w13b[:, 0:FC] = w1_ref[0].astype(BF16)
        w13b[:, FC:2 * FC] = w3_ref[0].astype(BF16)


def _expert_pass(second, tables, xg, w1, w3, w2, yp=None):
    f = 1 if second else 0
    tile = lambda s: jnp.maximum(s - 1, 0)
    ahead = lambda s: jnp.minimum(s, MAX_RT - 1)
    row = lambda s, ex, nw, na: (tile(s), 0)
    in_specs = [
        pl.BlockSpec((TR, D), row),
        pl.BlockSpec((1, D, FC), lambda s, ex, nw, na: (ex[ahead(s)], 0, f)),
        pl.BlockSpec((1, D, FC), lambda s, ex, nw, na: (ex[ahead(s)], 0, f)),
        pl.BlockSpec((1, FC, D), lambda s, ex, nw, na: (ex[tile(s)], f, 0)),
    ]
    args = [xg, w1, w3, w2]
    if second:
        in_specs.append(pl.BlockSpec((TR, D), row))
        args.append(yp)
    return pl.pallas_call(
        functools.partial(_expert_kernel, second),
        grid_spec=pltpu.PrefetchScalarGridSpec(
            num_scalar_prefetch=3,
            grid=(MAX_RT + 1,),
            in_specs=in_specs,
            out_specs=pl.BlockSpec((TR, D), row),
            scratch_shapes=[pltpu.VMEM((D, 2 * FC), BF16), pltpu.VMEM((FC, D), BF16)],
        ),
        out_shape=jax.ShapeDtypeStruct((MAX_RT * TR, D), F32),
        compiler_params=pltpu.CompilerParams(
            dimension_semantics=("arbitrary",), vmem_limit_bytes=EXPERT_VMEM_LIMIT),
        name="moe_expert_hi" if second else "moe_expert_lo",
    )(*tables, *args)


def _combine_kernel(n_ctx_tok, off_ref, loc_ref, cnt_ref, x_ref, mod_ref, info_ref, y_ref, g_ref, b_ref,
                    oc_ref, od_ref, ys_scr, sem):
    i = pl.program_id(0)
    n_steps = pl.num_programs(0)
    slot = i % 2

    def fetch(tile, s, do):
        def make_copy(local, glob, size):
            return pltpu.make_async_copy(y_ref.at[pl.ds(glob, size)], ys_scr.at[s, pl.ds(local, size)],
                                         sem.at[s])
        _segment_copies(off_ref, loc_ref, cnt_ref, tile, make_copy, do)

    def start_fetch(tile, s):
        ys_scr[s] = jnp.zeros((SORT_ROWS, D), F32)
        fetch(tile, s, lambda cp: cp.start())

    @pl.when(i == 0)
    def _():
        start_fetch(0, 0)

    @pl.when(i + 1 < n_steps)
    def _():
        start_fetch(i + 1, 1 - slot)

    info = info_ref[0]
    row1, row2, w1, w2 = info[0:1], info[1:2], info[2:3], info[3:4]
    p_idx = lax.broadcasted_iota(jnp.int32, (SORT_ROWS, TS), 0).astype(F32)
    w_rows = jnp.sum(jnp.where(p_idx == row1, w1, 0.0) + jnp.where(p_idx == row2, w2, 0.0),
                     axis=1, keepdims=True)
    padded = jnp.concatenate([info, jnp.zeros((V7X_LANES - V7X_SUBLANES, TS), F32)], axis=0)
    cols = jnp.transpose(padded, (1, 0))
    q_idx = lax.broadcasted_iota(jnp.int32, (TS, SORT_ROWS), 1).astype(F32)
    pick = jnp.where(jnp.logical_or(q_idx == cols[:, 0:1], q_idx == cols[:, 1:2]), 1.0, 0.0).astype(BF16)
    fetch(i, slot, lambda cp: cp.wait())
    scaled = (ys_scr[slot] * w_rows).astype(BF16)
    acc = jnp.dot(pick, scaled, preferred_element_type=F32)
    res = _ffn_epilogue(n_ctx_tok, x_ref, mod_ref, acc, g_ref, b_ref)
    is_dec = i * TS >= n_ctx_tok

    @pl.when(jnp.logical_not(is_dec))
    def _():
        oc_ref[...] = res

    @pl.when(is_dec)
    def _():
        od_ref[...] = res


def _combine(x, mod_l, info, y, segs, ln_g, ln_b, n_ctx_tok):
    n_tok = x.shape[0]
    n_tiles = n_tok // TS
    ncb = n_ctx_tok // TS
    const = lambda i, *_: (0, 0)
    return pl.pallas_call(
        functools.partial(_combine_kernel, n_ctx_tok),
        grid_spec=pltpu.PrefetchScalarGridSpec(
            num_scalar_prefetch=3,
            grid=(n_tiles,),
            in_specs=[
                pl.BlockSpec((TS, D), lambda i, *_: (i, 0)),
                pl.BlockSpec((MOD_ROWS, N_MOD * D), const),
                pl.BlockSpec((1, V7X_SUBLANES, TS), lambda i, *_: (i, 0, 0)),
                pl.BlockSpec(memory_space=pl.ANY),
                pl.BlockSpec((1, D), const),
                pl.BlockSpec((1, D), const),
            ],
            out_specs=[
                pl.BlockSpec((TS, D), lambda i, *_: (jnp.minimum(i, ncb - 1), 0)),
                pl.BlockSpec((TS, D), lambda i, *_: (jnp.maximum(i - ncb, 0), 0)),
            ],
            scratch_shapes=[
                pltpu.VMEM((2, SORT_ROWS, D), F32),
                pltpu.SemaphoreType.DMA((2,)),
            ],
        ),
        out_shape=[
            jax.ShapeDtypeStruct((n_ctx_tok, D), F32),
            jax.ShapeDtypeStruct((n_tok - n_ctx_tok, D), F32),
        ],
        compiler_params=pltpu.CompilerParams(
            dimension_semantics=("arbitrary",), vmem_limit_bytes=VMEM_LIMIT),
        name="moe_combine",
    )(*segs, x, mod_l, info, y, ln_g, ln_b)


def _routing_tables(counts):
    counts = (counts + SEG_PAD - 1) // SEG_PAD * SEG_PAD
    totals = jnp.sum(counts, axis=0)
    n_rt = (totals + TR - 1) // TR
    cum = jnp.cumsum(n_rt)
    first = cum - n_rt
    n_act = cum[-1]
    off = first[None, :] * TR + (jnp.cumsum(counts, axis=0) - counts)
    loc = jnp.cumsum(counts, axis=1) - counts
    fill = jnp.concatenate([first * TR + totals, n_rt * TR - totals, n_act[None]])
    r = jnp.arange(MAX_RT, dtype=jnp.int32)
    rc = jnp.minimum(r, n_act - 1)
    exp = jnp.sum((rc[:, None] >= cum[None, :]).astype(jnp.int32), axis=1)
    new = jnp.concatenate([jnp.logical_and(r == first[exp], r < n_act), jnp.zeros((1,), bool)])
    i32 = lambda a: a.astype(jnp.int32)
    segs = (i32(off.reshape(-1)), i32(loc.reshape(-1)), i32(counts.reshape(-1)))
    return segs, i32(fill), (i32(exp), i32(new), i32(n_act.reshape(1)))


def _moe_ffn(x, mod_l, router_w, router_b, w1, w3, w2, ln_g, ln_b, n_ctx_tok):
    n_tok = x.shape[0]
    n_seg_pad = (n_tok // TS) * N_EXP * (SEG_PAD - 1)
    assert (2 * n_tok + n_seg_pad + TR - 1) // TR + N_EXP == MAX_RT and N_FC == 2
    wrt = router_w.T
    brt = jnp.broadcast_to(router_b.reshape(N_EXP, 1), (N_EXP, V7X_LANES))
    rinfo, counts, u2b = _rank(x, mod_l, wrt, brt, n_ctx_tok)
    segs, fill, tables = _routing_tables(counts)
    xg, info = _dispatch(u2b, rinfo, *segs, fill)
    y_lo = _expert_pass(False, tables, xg, w1, w3, w2)
    y = _expert_pass(True, tables, xg, w1, w3, w2, y_lo)
    return _combine(x, mod_l, info, y, segs, ln_g, ln_b, n_ctx_tok)


def _cast_kernel(*refs):
    o_ref = refs[-1]
    off = 0
    for x_ref in refs[:-1]:
        width = x_ref.shape[-1]
        o_ref[..., off:off + width] = x_ref[...].reshape(o_ref.shape[:-1] + (width,)).astype(o_ref.dtype)
        off += width


def _pack_w13(w1, w3):
    return pl.pallas_call(
        _cast_kernel,
        grid=(N_FC,),
        in_specs=[pl.BlockSpec((D, FC), lambda f: (0, f)), pl.BlockSpec((D, FC), lambda f: (0, f))],
        out_specs=pl.BlockSpec((1, D, 2 * FC), lambda f: (f, 0, 0)),
        out_shape=jax.ShapeDtypeStruct((N_FC, D, 2 * FC), BF16),
        compiler_params=pltpu.CompilerParams(vmem_limit_bytes=VMEM_LIMIT),
        name="pack_w13",
    )(w1, w3)


def _pack_w_in(w_in):
    depth = w_in.shape[0]
    def regroup_kernel(x_ref, o_ref):
        for j in range(N_CH):
            o_ref[0, j] = x_ref[0, :, j * CW:(j + 1) * CW].astype(BF16)

    w5 = pl.pallas_call(
        regroup_kernel,
        grid=(depth, 5),
        in_specs=[pl.BlockSpec((1, D, D), lambda l, g: (l, 0, g))],
        out_specs=pl.BlockSpec((1, N_CH, D, CW), lambda l, g: (l, 0, 0, g)),
        out_shape=jax.ShapeDtypeStruct((depth, N_CH, D, 5 * CW), BF16),
        compiler_params=pltpu.CompilerParams(vmem_limit_bytes=VMEM_LIMIT),
        name="pack_w5",
    )(w_in)
    wg = pl.pallas_call(
        _cast_kernel,
        grid=(depth, 2),
        in_specs=[pl.BlockSpec((1, D, D), lambda l, k: (l, 0, 5 + k))],
        out_specs=pl.BlockSpec((1, D, D), lambda l, k: (l, 0, k)),
        out_shape=jax.ShapeDtypeStruct((depth, D, 2 * D), BF16),
        compiler_params=pltpu.CompilerParams(vmem_limit_bytes=VMEM_LIMIT),
        name="pack_wg",
    )(w_in)
    return w5, wg


def _block_diag_chunks(w):
    tiled = jnp.tile(w.reshape(N_CH, CW, HEAD_D), (1, 1, CW // HEAD_D))
    blk = jnp.arange(CW) // HEAD_D
    return jnp.where(blk[:, None] == blk[None, :], tiled, 0.0)


def kernel(x_prompt, x_sample, state_rglru, c, c_ctx, w_mod, b_mod, w_in, conv_a, w_a_out, conv_b, conv_b_bias, w_gate_a, b_gate_a, w_gate_x, b_gate_x, lru_lambda, w_b_out, w_o, ln1_g, ln1_b, ln2_g, ln2_b, ffn_w1, ffn_w3, ffn_w2, router_w, router_b, moe_w1, moe_w3, moe_w2):
    batch, seq, d = x_prompt.shape
    dec_batch, dec_seq, _ = x_sample.shape
    depth = w_mod.shape[0]
    assert (d, seq, dec_seq, depth) == (D, SEQ, DEC_SEQ, DEPTH)
    n_ctx_tok = batch * seq
    n_dec_tok = dec_batch * dec_seq
    assert n_ctx_tok % TM1 == 0 and TM1 == dec_seq and 1 + dec_batch <= MOD_ROWS
    n_ctx_tiles = n_ctx_tok // TM1
    n_tiles = n_ctx_tiles + dec_batch
    seq_per_tile = TM1 // seq
    assert N_SEG == 2 * seq_per_tile

    assert depth % 2 == 0
    n_tok = n_ctx_tok + n_dec_tok
    x = (x_prompt.reshape(n_ctx_tok, D), x_sample.reshape(n_dec_tok, D))

    cond = jnp.zeros((MOD_ROWS, D), F32).at[0].set(c_ctx).at[1:1 + dec_batch].set(c)
    mod = _modulation(cond, w_mod, b_mod)

    w5, wg = _pack_w_in(w_in)
    states = []
    for l in range(depth):
        wbd = jnp.concatenate(
            [_block_diag_chunks(w_gate_a[l, 0]), _block_diag_chunks(w_gate_x[l, 0]),
             _block_diag_chunks(w_gate_a[l, 1]), _block_diag_chunks(w_gate_x[l, 1])],
            axis=-1).astype(BF16)
        gbias = jnp.stack([b_gate_a[l, 0], b_gate_x[l, 0], b_gate_a[l, 1], b_gate_x[l, 1]], axis=0)
        h0 = jnp.zeros((n_tiles, 2, N_SEG, D), F32)
        h0 = h0.at[n_ctx_tiles:, 0, 0].set(state_rglru[:, l, 0].astype(F32))
        h0 = h0.at[n_ctx_tiles:, 1, N_SEG - 1].set(state_rglru[:, l, 1].astype(F32))

        a_pre, b_pre, st = _mixer_part1(
            x, mod[l], l, w5, conv_a[l], conv_b[l], conv_b_bias[l].reshape(1, D), wbd, gbias,
            lru_lambda[l], h0, n_tok, n_ctx_tok)
        x = _mixer_part2(
            x, mod[l], l, a_pre, b_pre, wg, w_a_out[l].astype(BF16), w_b_out[l].astype(BF16),
            w_o[l].astype(BF16), ln1_g[l].reshape(1, D), ln1_b[l].reshape(1, D), n_ctx_tok)

        if l % 2 == 0:
            k = l // 2
            x = _dense_ffn(x, mod[l], _pack_w13(ffn_w1[k], ffn_w3[k]), ffn_w2[k].astype(BF16),
                           ln2_g[l].reshape(1, D), ln2_b[l].reshape(1, D), n_ctx_tok)
        else:
            k = l // 2
            x = _moe_ffn(x, mod[l], router_w[k], router_b[k], moe_w1[k], moe_w3[k], moe_w2[k],
                         ln2_g[l].reshape(1, D), ln2_b[l].reshape(1, D), n_ctx_tok)

        st_ctx = st[:n_ctx_tiles]
        fwd = st_ctx[:, 0, 1::2].reshape(batch, D)
        bwd = st_ctx[:, 1, 0::2].reshape(batch, D)
        states.append(jnp.stack([fwd, bwd], axis=1))

    y_prompt = x[0].reshape(batch, seq, D)
    y_sample = x[1].reshape(dec_batch, dec_seq, D)
    new_state = jnp.stack(states, axis=1).astype(x_prompt.dtype)
    return (y_prompt, y_sample, new_state)
```

```python
import functools

import jax
import jax.numpy as jnp
from jax import lax
from jax.experimental import pallas as pl
from jax.experimental.pallas import tpu as pltpu

F32 = jnp.float32
BF16 = jnp.bfloat16

D = 1024
SEQ = 256
DEC_SEQ = 1024
GRID_W = 64
N_HEAD = 16
HEAD_D = D // N_HEAD
RGLRU_C = 8.0
LOG2_E = 1.4426950408889634
D_FF = 2816
N_EXP = 8
N_MOD = 6
DEPTH = 2
DN_ALPHA = (2.0 * DEPTH) ** 0.25

V7X_SUBLANES = 8
V7X_LANES = 128
V7X_VMEM_BYTES = 64 * 1024 * 1024
VMEM_LIMIT = V7X_VMEM_BYTES - 12 * 1024 * 1024

TM1 = 1024
CW = 256
N_CH = D // CW
N_LC = CW // V7X_LANES
N_SEG = V7X_SUBLANES
SEG = TM1 // N_SEG
SEG_STRIDE = SEG + 4
CONV_GAP = V7X_SUBLANES
TM2 = 512
TM3 = 512
FC = 1408
N_FC = D_FF // FC
TS = 512
TR = 256
SEG_PAD = V7X_SUBLANES
SEG_BITS = 7
SORT_ROWS = 2 * TS + V7X_LANES
MAX_RT = (2 * 10240 + (10240 // TS) * N_EXP * (SEG_PAD - 1) + TR - 1) // TR + N_EXP
EXPERT_VMEM_LIMIT = V7X_VMEM_BYTES - 6 * 1024 * 1024
MOD_ROWS = 8
MOD_BLK = 3072


def _sigmoid(x):
    return 0.5 * jnp.tanh(0.5 * x) + 0.5


def _ln_plain(x, eps):
    mu = jnp.mean(x, axis=-1, keepdims=True)
    xc = x - mu
    var = jnp.mean(xc * xc, axis=-1, keepdims=True)
    return xc * lax.rsqrt(var + eps)


def _token_source(xs, tile, n_ctx_tok):
    ncb = n_ctx_tok // tile
    if isinstance(xs, tuple):
        x_ctx, x_dec = xs
        dec_off = 0
    else:
        x_ctx = x_dec = xs
        dec_off = ncb
    ctx_map = lambda i: (jnp.minimum(i, ncb - 1), 0)
    dec_map = lambda i: (jnp.maximum(i - ncb, 0) + dec_off, 0)
    return x_ctx, x_dec, ctx_map, dec_map


def _mod_row(tok0, n_ctx_tok):
    dec = jnp.maximum(tok0 - n_ctx_tok, 0) // DEC_SEQ
    return jnp.where(tok0 >= n_ctx_tok, 1 + dec, 0)


def _mod_kernel(cond_ref, w_ref, b_ref, o_ref):
    cnd = cond_ref[...]
    s = cnd * _sigmoid(cnd)
    o_ref[0] = jnp.dot(s, w_ref[0], preferred_element_type=F32,
                       precision=lax.Precision.HIGHEST) + b_ref[0]


def _modulation(cond, w_mod, b_mod):
    depth = w_mod.shape[0]
    n_out = w_mod.shape[2]
    return pl.pallas_call(
        _mod_kernel,
        grid=(depth, n_out // MOD_BLK),
        in_specs=[
            pl.BlockSpec((MOD_ROWS, D), lambda l, j: (0, 0)),
            pl.BlockSpec((1, D, MOD_BLK), lambda l, j: (l, 0, j)),
            pl.BlockSpec((1, 1, MOD_BLK), lambda l, j: (l, 0, j)),
        ],
        out_specs=pl.BlockSpec((1, MOD_ROWS, MOD_BLK), lambda l, j: (l, 0, j)),
        out_shape=jax.ShapeDtypeStruct((depth, MOD_ROWS, n_out), F32),
        compiler_params=pltpu.CompilerParams(
            dimension_semantics=("arbitrary", "arbitrary"), vmem_limit_bytes=VMEM_LIMIT),
        name="modulation",
    )(cond, w_mod, b_mod.reshape(depth, 1, n_out))


def _scan_dir(a_scr, b_scr, hl_scr, ac_scr, h0, keep, reverse):
    n_lc = a_scr.shape[0]

    def body(k, carry):
        kk = SEG - 1 - k if reverse else k
        idx = pl.ds(kk, N_SEG, stride=SEG_STRIDE)
        new = []
        for c in range(n_lc):
            h, acc = carry[c]
            a_k = a_scr[c, idx, :]
            h = a_k * h + b_scr[c, idx, :]
            acc = a_k * acc
            hl_scr[c, idx, :] = h
            ac_scr[c, idx, :] = acc
            new.append((h, acc))
        return tuple(new)

    init = tuple((jnp.zeros((N_SEG, V7X_LANES), F32), jnp.ones((N_SEG, V7X_LANES), F32))
                 for _ in range(n_lc))
    fin = lax.fori_loop(0, SEG, body, init, unroll=8)
    h_loc = jnp.concatenate([fin[c][0] for c in range(n_lc)], axis=1)
    a_tot = jnp.concatenate([fin[c][1] for c in range(n_lc)], axis=1)

    order = range(N_SEG - 1, -1, -1) if reverse else range(N_SEG)
    h_in = [None] * N_SEG
    prev = None
    for s in order:
        cur = h0[s:s + 1]
        if prev is not None:
            left = a_tot[prev:prev + 1] * h_in[prev] + h_loc[prev:prev + 1]
            cur = keep[s] * left + cur
        h_in[s] = cur
        prev = s
    h_in = jnp.concatenate(h_in, axis=0)
    return h_in, a_tot * h_in + h_loc


def _conv_stage(is_dec, proj_scr, ca_ref, cb_ref, cbias_ref, apre_ref, xr_scr, gap_scr):
    win = GRID_W
    n_win = TM1 // win
    stride = win + CONV_GAP
    ca = ca_ref[...]
    cb = cb_ref[...]
    bias = cbias_ref[...]
    pieces = [(w, c) for w in range(n_win) for c in range(N_LC)]
    sub = lax.broadcasted_iota(jnp.int32, (CONV_GAP, V7X_LANES), 0)
    joined = jnp.where(is_dec, jnp.float32(0.0), jnp.float32(1.0))
    zero_gap = jnp.zeros((CONV_GAP, V7X_LANES), F32)

    def stage(value_of):
        for c in range(N_LC):
            for w in range(n_win + 1):
                edge = w * win
                if edge % SEQ == 0:
                    gap = zero_gap
                else:
                    before = value_of(slice(edge - CONV_GAP, edge), c)
                    after = value_of(slice(edge, edge + CONV_GAP), c)
                    gap = joined * jnp.where(sub >= CONV_GAP - 2, before, jnp.where(sub == 0, after, 0.0))
                gap_scr[c, w * stride:w * stride + CONV_GAP, :] = gap
        for w, c in pieces:
            lo = CONV_GAP + w * stride
            gap_scr[c, lo:lo + win, :] = value_of(slice(w * win, (w + 1) * win), c)

    def tap(w, c, shift):
        lo = CONV_GAP + w * stride + shift
        return gap_scr[c, lo:lo + win, :]

    def chunk_cols(k, c):
        return slice(k * CW + c * V7X_LANES, k * CW + (c + 1) * V7X_LANES)

    stage(lambda rows, c: proj_scr[rows, chunk_cols(2, c)] * proj_scr[rows, chunk_cols(0, c)])
    for w, c in pieces:
        rows = slice(w * win, (w + 1) * win)
        lanes = slice(c * V7X_LANES, (c + 1) * V7X_LANES)
        conv = ca[0:1, lanes] * tap(w, c, -1) + ca[1:2, lanes] * tap(w, c, 0) + ca[2:3, lanes] * tap(w, c, 1)
        apre_ref[rows, lanes] = (proj_scr[rows, chunk_cols(1, c)] * conv).astype(BF16)
    stage(lambda rows, c: proj_scr[rows, chunk_cols(4, c)])
    for w, c in pieces:
        rows = slice(w * win, (w + 1) * win)
        lanes = slice(c * V7X_LANES, (c + 1) * V7X_LANES)
        xr_scr[rows, lanes] = (cb[0:1, lanes] * tap(w, c, -2) + cb[1:2, lanes] * tap(w, c, -1)
                               + cb[2:3, lanes] * tap(w, c, 0) + cb[3:4, lanes] * tap(w, c, 1)
                               + bias[:, lanes])


def _mix1_kernel(n_ctx_tiles, pre_normed, *refs):
    if pre_normed:
        u_ref, refs = refs[0], refs[1:]
    else:
        xc_ref, xd_ref, mod_ref, refs = refs[0], refs[1], refs[2], refs[3:]
    (w5_ref, ca_ref, cb_ref, cbias_ref, wbd_ref, gb_ref, lam_ref, h0_ref, apre_ref, bpre_ref, st_ref,
     u_scr, proj_scr, a_scr, b_scr, hl_scr, ac_scr, hsum_scr, xr_scr, gap_scr) = refs
    i = pl.program_id(0)
    j = pl.program_id(1)
    is_dec = i >= n_ctx_tiles

    if pre_normed:
        proj_scr[...] = jnp.dot(u_ref[...], w5_ref[0, 0], preferred_element_type=F32)
    else:
        @pl.when(j == 0)
        def _():
            row = jnp.where(is_dec, i - (n_ctx_tiles - 1), 0)
            m = mod_ref[pl.ds(row, 1), :]
            sh1 = m[:, 0:D]
            sc1 = m[:, D:2 * D]
            x = jnp.where(is_dec, xd_ref[...], xc_ref[...])
            u_scr[...] = (_ln_plain(x, 1e-6) * (1.0 + sc1) + sh1).astype(BF16)

        proj_scr[...] = jnp.dot(u_scr[...], w5_ref[0, 0], preferred_element_type=F32)

    _conv_stage(is_dec, proj_scr, ca_ref, cb_ref, cbias_ref, apre_ref, xr_scr, gap_scr)

    xr = xr_scr[...]
    gates = jnp.dot(xr.astype(BF16), wbd_ref[0], preferred_element_type=F32)
    gb = gb_ref[...]
    lam = lam_ref[...]
    sp = jnp.maximum(-lam, 0.0) + jnp.log1p(jnp.exp(-jnp.abs(lam)))
    rate = (-RGLRU_C * LOG2_E) * sp

    one = jnp.float32(1.0)
    for d in range(2):
        ga = gates[:, (2 * d) * CW:(2 * d + 1) * CW] + gb[2 * d:2 * d + 1]
        gx = gates[:, (2 * d + 1) * CW:(2 * d + 2) * CW] + gb[2 * d + 1:2 * d + 2]
        r = _sigmoid(ga)
        ig = _sigmoid(gx)
        a = jnp.exp2(r * rate[d:d + 1])
        y = 1.0 - a * a
        bt = jnp.where(y > 0.0, y * lax.rsqrt(y), 0.0) * (ig * xr)
        for s in range(N_SEG):
            lo = s * SEG_STRIDE
            for c in range(N_LC):
                lanes = slice(c * V7X_LANES, (c + 1) * V7X_LANES)
                a_scr[c, lo:lo + SEG, :] = a[s * SEG:(s + 1) * SEG, lanes]
                b_scr[c, lo:lo + SEG, :] = bt[s * SEG:(s + 1) * SEG, lanes]
        if d == 0:
            keep = [jnp.where(is_dec, one, jnp.float32(s % 2 == 1)) for s in range(N_SEG)]
        else:
            keep = [jnp.where(is_dec, one, jnp.float32(s % 2 == 0)) for s in range(N_SEG)]
        h_in, h_out = _scan_dir(a_scr, b_scr, hl_scr, ac_scr, h0_ref[0, d], keep, reverse=(d == 1))
        st_ref[0, d] = h_out
        for s in range(N_SEG):
            lo = s * SEG_STRIDE
            for c in range(N_LC):
                lanes = slice(c * V7X_LANES, (c + 1) * V7X_LANES)
                h_seg = hl_scr[c, lo:lo + SEG, :] + ac_scr[c, lo:lo + SEG, :] * h_in[s:s + 1, lanes]
                if d == 0:
                    hsum_scr[s * SEG:(s + 1) * SEG, lanes] = h_seg
                else:
                    hsum_scr[s * SEG:(s + 1) * SEG, lanes] += h_seg

    bpre_ref[...] = (hsum_scr[...] * jax.nn.gelu(proj_scr[:, 3 * CW:4 * CW])).astype(BF16)


def _mixer_part1(xs, u_pre, mod_l, layer, w5, conv_a, conv_b, conv_b_bias, wbd, gbias, lam, h0, n_tok,
                 n_ctx_tok):
    n_tiles = n_tok // TM1
    if u_pre is None:
        x_ctx, x_dec, ctx_map, dec_map = _token_source(xs, TM1, n_ctx_tok)
        lead_args = [x_ctx, x_dec, mod_l]
        lead_specs = [
            pl.BlockSpec((TM1, D), lambda i, j: ctx_map(i)),
            pl.BlockSpec((TM1, D), lambda i, j: dec_map(i)),
            pl.BlockSpec((MOD_ROWS, N_MOD * D), lambda i, j: (0, 0)),
        ]
    else:
        lead_args = [u_pre]
        lead_specs = [pl.BlockSpec((TM1, D), lambda i, j: (i, 0))]
    kern = functools.partial(_mix1_kernel, n_ctx_tok // TM1, u_pre is not None)
    return pl.pallas_call(
        kern,
        grid=(n_tiles, N_CH),
        in_specs=lead_specs + [
            pl.BlockSpec((1, 1, D, 5 * CW), lambda i, j: (layer, j, 0, 0)),
            pl.BlockSpec((3, CW), lambda i, j: (0, j)),
            pl.BlockSpec((4, CW), lambda i, j: (0, j)),
            pl.BlockSpec((1, CW), lambda i, j: (0, j)),
            pl.BlockSpec((1, CW, 4 * CW), lambda i, j: (j, 0, 0)),
            pl.BlockSpec((4, CW), lambda i, j: (0, j)),
            pl.BlockSpec((2, CW), lambda i, j: (0, j)),
            pl.BlockSpec((1, 2, N_SEG, CW), lambda i, j: (i, 0, 0, j)),
        ],
        out_specs=[
            pl.BlockSpec((TM1, CW), lambda i, j: (i, j)),
            pl.BlockSpec((TM1, CW), lambda i, j: (i, j)),
            pl.BlockSpec((1, 2, N_SEG, CW), lambda i, j: (i, 0, 0, j)),
        ],
        out_shape=[
            jax.ShapeDtypeStruct((n_tok, D), BF16),
            jax.ShapeDtypeStruct((n_tok, D), BF16),
            jax.ShapeDtypeStruct((n_tiles, 2, N_SEG, D), F32),
        ],
        scratch_shapes=[
            pltpu.VMEM((TM1, D), BF16),
            pltpu.VMEM((TM1, 5 * CW), F32),
            pltpu.VMEM((N_LC, N_SEG * SEG_STRIDE, V7X_LANES), F32),
            pltpu.VMEM((N_LC, N_SEG * SEG_STRIDE, V7X_LANES), F32),
            pltpu.VMEM((N_LC, N_SEG * SEG_STRIDE, V7X_LANES), F32),
            pltpu.VMEM((N_LC, N_SEG * SEG_STRIDE, V7X_LANES), F32),
            pltpu.VMEM((TM1, CW), F32),
            pltpu.VMEM((TM1, CW), F32),
            pltpu.VMEM((N_LC, CONV_GAP + (TM1 // GRID_W) * (GRID_W + CONV_GAP), V7X_LANES), F32),
        ],
        compiler_params=pltpu.CompilerParams(
            dimension_semantics=("arbitrary", "arbitrary"), vmem_limit_bytes=VMEM_LIMIT),
        name="mixer_scan",
    )(*lead_args, w5, conv_a, conv_b, conv_b_bias, wbd, gbias, lam, h0)


def _mix2_kernel(n_ctx_tok, pre_normed, *refs):
    if pre_normed:
        u_ref, refs = refs[0], refs[1:]
    xc_ref, xd_ref, mod_ref, ap_ref, bp_ref, wg_ref, wa_ref, wb_ref, wo_ref, g_ref, b_ref, o_ref = refs
    i = pl.program_id(0)
    m = mod_ref[pl.ds(_mod_row(i * TM2, n_ctx_tok), 1), :]
    sh1 = m[:, 0:D]
    sc1 = m[:, D:2 * D]
    g1 = m[:, 2 * D:3 * D]
    x = jnp.where(i * TM2 >= n_ctx_tok, xd_ref[...], xc_ref[...])
    if pre_normed:
        u = u_ref[...]
    else:
        u = (_ln_plain(x, 1e-6) * (1.0 + sc1) + sh1).astype(BF16)
    gates = jnp.dot(u, wg_ref[0], preferred_element_type=F32)
    br_a = jnp.dot(ap_ref[...], wa_ref[...], preferred_element_type=F32)
    br_b = jnp.dot(bp_ref[...], wb_ref[...], preferred_element_type=F32)
    merged = _sigmoid(gates[:, 0:D]) * br_a + _sigmoid(gates[:, D:2 * D]) * br_b
    mix = jnp.dot(merged.astype(BF16), wo_ref[...], preferred_element_type=F32)
    y = DN_ALPHA * x + g1 * mix
    o_ref[...] = _ln_plain(y, 1e-5) * g_ref[...] + b_ref[...]


def _mixer_part2(xs, u_pre, mod_l, layer, a_pre, b_pre, wg, wa, wb, wo, ln_g, ln_b, n_ctx_tok):
    n_tok = a_pre.shape[0]
    x_ctx, x_dec, ctx_map, dec_map = _token_source(xs, TM2, n_ctx_tok)
    const = lambda i: (0, 0)
    tile = lambda i: (i, 0)
    lead_args = [] if u_pre is None else [u_pre]
    lead_specs = [] if u_pre is None else [pl.BlockSpec((TM2, D), tile)]
    return pl.pallas_call(
        functools.partial(_mix2_kernel, n_ctx_tok, u_pre is not None),
        grid=(n_tok // TM2,),
        in_specs=lead_specs + [
            pl.BlockSpec((TM2, D), ctx_map),
            pl.BlockSpec((TM2, D), dec_map),
            pl.BlockSpec((MOD_ROWS, N_MOD * D), const),
            pl.BlockSpec((TM2, D), tile),
            pl.BlockSpec((TM2, D), tile),
            pl.BlockSpec((1, D, 2 * D), lambda i: (layer, 0, 0)),
            pl.BlockSpec((D, D), const),
            pl.BlockSpec((D, D), const),
            pl.BlockSpec((D, D), const),
            pl.BlockSpec((1, D), const),
            pl.BlockSpec((1, D), const),
        ],
        out_specs=pl.BlockSpec((TM2, D), tile),
        out_shape=jax.ShapeDtypeStruct((n_tok, D), F32),
        compiler_params=pltpu.CompilerParams(
            dimension_semantics=("arbitrary",), vmem_limit_bytes=VMEM_LIMIT),
        name="mixer_out",
    )(*lead_args, x_ctx, x_dec, mod_l, a_pre, b_pre, wg, wa, wb, wo, ln_g, ln_b)


def _ffn_prologue(n_ctx_tok, x_ref, mod_ref):
    m = mod_ref[pl.ds(_mod_row(pl.program_id(0) * TM3, n_ctx_tok), 1), :]
    sh2 = m[:, 3 * D:4 * D]
    sc2 = m[:, 4 * D:5 * D]
    return _ln_plain(x_ref[...], 1e-6) * (1.0 + sc2) + sh2


def _ffn_epilogue(n_ctx_tok, x_ref, mod_ref, acc, g_ref, b_ref):
    m = mod_ref[pl.ds(_mod_row(pl.program_id(0) * TM3, n_ctx_tok), 1), :]
    g2 = m[:, 5 * D:6 * D]
    y = DN_ALPHA * x_ref[...] + g2 * acc
    return _ln_plain(y, 1e-5) * g_ref[...] + b_ref[...]


def _swiglu_chunk(u, w13, w2):
    h = jnp.dot(u, w13, preferred_element_type=F32)
    h1 = h[:, 0:FC]
    hid = (h1 * _sigmoid(h1) * h[:, FC:2 * FC]).astype(BF16)
    return jnp.dot(hid, w2, preferred_element_type=F32)


def _dense_ffn_kernel(n_ctx_tok, x_ref, mod_ref, modn_ref, w13_ref, w2_ref, g_ref, b_ref, o_ref, un_ref,
                      u_scr, acc_scr):
    f = pl.program_id(1)

    @pl.when(f == 0)
    def _():
        u_scr[...] = _ffn_prologue(n_ctx_tok, x_ref, mod_ref).astype(BF16)

    p = _swiglu_chunk(u_scr[...], w13_ref[0], w2_ref[...])

    @pl.when(f == 0)
    def _():
        acc_scr[...] = p

    @pl.when(f > 0)
    def _():
        acc_scr[...] += p

    @pl.when(f == N_FC - 1)
    def _():
        res = _ffn_epilogue(n_ctx_tok, x_ref, mod_ref, acc_scr[...], g_ref, b_ref)
        o_ref[...] = res
        m = modn_ref[pl.ds(_mod_row(pl.program_id(0) * TM3, n_ctx_tok), 1), :]
        un_ref[...] = (_ln_plain(res, 1e-6) * (1.0 + m[:, D:2 * D]) + m[:, 0:D]).astype(BF16)


def _dense_ffn(x, mod_l, mod_next, w13, w2, ln_g, ln_b, n_ctx_tok):
    n_tok = x.shape[0]
    tile = pl.BlockSpec((TM3, D), lambda i, f: (i, 0))
    table = pl.BlockSpec((MOD_ROWS, N_MOD * D), lambda i, f: (0, 0))
    return pl.pallas_call(
        functools.partial(_dense_ffn_kernel, n_ctx_tok),
        grid=(n_tok // TM3, N_FC),
        in_specs=[
            tile,
            table,
            table,
            pl.BlockSpec((1, D, 2 * FC), lambda i, f: (f, 0, 0)),
            pl.BlockSpec((FC, D), lambda i, f: (f, 0)),
            pl.BlockSpec((1, D), lambda i, f: (0, 0)),
            pl.BlockSpec((1, D), lambda i, f: (0, 0)),
        ],
        out_specs=[tile, tile],
        out_shape=[jax.ShapeDtypeStruct((n_tok, D), F32), jax.ShapeDtypeStruct((n_tok, D), BF16)],
        scratch_shapes=[pltpu.VMEM((TM3, D), BF16), pltpu.VMEM((TM3, D), F32)],
        compiler_params=pltpu.CompilerParams(
            dimension_semantics=("arbitrary", "arbitrary"), vmem_limit_bytes=VMEM_LIMIT),
        name="dense_ffn",
    )(x, mod_l, mod_next, w13, w2, ln_g, ln_b)


def _segment_copies(off_ref, loc_ref, cnt_ref, tile, make_copy, do):
    for e in range(N_EXP):
        n = cnt_ref[tile * N_EXP + e]
        local = loc_ref[tile * N_EXP + e]
        glob = off_ref[tile * N_EXP + e]
        for k in range(SEG_BITS - 1, -1, -1):
            size = SEG_PAD << k
            take = (n & size) != 0

            @pl.when(take)
            def _():
                do(make_copy(pl.multiple_of(local, SEG_PAD), pl.multiple_of(glob, SEG_PAD), size))

            step = jnp.where(take, size, 0)
            local = local + step
            glob = glob + step


def _rank_kernel(n_ctx_tok, x_ref, mod_ref, wrt_ref, brt_ref, info_ref, cnt_ref, u2b_ref):
    i = pl.program_id(0)
    u2 = _ffn_prologue(n_ctx_tok, x_ref, mod_ref)
    u2b_ref[...] = u2.astype(BF16)
    lg = lax.dot_general(wrt_ref[...], u2, (((1,), (1,)), ((), ())), preferred_element_type=F32,
                         precision=lax.Precision.HIGHEST) + brt_ref[:, 0:1]
    eidx = lax.broadcasted_iota(jnp.int32, lg.shape, 0).astype(F32)
    neg = jnp.float32(-jnp.inf)
    v1 = jnp.max(lg, axis=0, keepdims=True)
    i1 = jnp.min(jnp.where(lg == v1, eidx, float(N_EXP)), axis=0, keepdims=True)
    lg2 = jnp.where(eidx == i1, neg, lg)
    v2 = jnp.max(lg2, axis=0, keepdims=True)
    i2 = jnp.min(jnp.where(lg2 == v2, eidx, float(N_EXP)), axis=0, keepdims=True)
    t = jnp.exp(v2 - v1)
    w_top = 1.0 / (1.0 + t)
    m1 = eidx == i1
    m2 = eidx == i2
    member = jnp.where(jnp.logical_or(m1, m2), 1.0, 0.0)
    before = (lax.broadcasted_iota(jnp.int32, (TS, TS), 0)
              < lax.broadcasted_iota(jnp.int32, (TS, TS), 1))
    rank = jnp.dot(member.astype(BF16), jnp.where(before, 1.0, 0.0).astype(BF16),
                   preferred_element_type=F32)
    for e in range(N_EXP):
        cnt_ref[i, e] = jnp.sum(member[e:e + 1, :]).astype(jnp.int32)
    rank1 = jnp.sum(jnp.where(m1, rank, 0.0), axis=0, keepdims=True)
    rank2 = jnp.sum(jnp.where(m2, rank, 0.0), axis=0, keepdims=True)
    info_ref[0] = jnp.concatenate(
        [i1, i2, rank1, rank2, w_top, t * w_top, jnp.zeros((V7X_SUBLANES - 6, TS), F32)], axis=0)


def _rank(x, mod_l, wrt, brt, n_ctx_tok):
    n_tiles = x.shape[0] // TS
    return pl.pallas_call(
        functools.partial(_rank_kernel, n_ctx_tok),
        grid=(n_tiles,),
        in_specs=[
            pl.BlockSpec((TS, D), lambda i: (i, 0)),
            pl.BlockSpec((MOD_ROWS, N_MOD * D), lambda i: (0, 0)),
            pl.BlockSpec((N_EXP, D), lambda i: (0, 0)),
            pl.BlockSpec((N_EXP, V7X_LANES), lambda i: (0, 0)),
        ],
        out_specs=[
            pl.BlockSpec((1, V7X_SUBLANES, TS), lambda i: (i, 0, 0)),
            pl.BlockSpec(memory_space=pltpu.SMEM),
            pl.BlockSpec((TS, D), lambda i: (i, 0)),
        ],
        out_shape=[
            jax.ShapeDtypeStruct((n_tiles, V7X_SUBLANES, TS), F32),
            jax.ShapeDtypeStruct((n_tiles, N_EXP), jnp.int32),
            jax.ShapeDtypeStruct((n_tiles * TS, D), BF16),
        ],
        compiler_params=pltpu.CompilerParams(
            dimension_semantics=("arbitrary",), vmem_limit_bytes=VMEM_LIMIT),
        name="moe_rank",
    )(x, mod_l, wrt, brt)


def _dispatch_kernel(off_ref, loc_ref, cnt_ref, fill_ref, rinfo_ref, u_ref, xg_ref, info_ref,
                     sorted_scr, zero_scr, sem, zsem):
    i = pl.program_id(0)
    n_steps = pl.num_programs(0)
    slot = i % 2
    rinfo = rinfo_ref[0]
    i1 = rinfo[0:1]
    i2 = rinfo[1:2]
    loc1 = jnp.zeros_like(i1)
    loc2 = jnp.zeros_like(i2)
    for e in range(N_EXP):
        local = loc_ref[i * N_EXP + e].astype(F32)
        loc1 = jnp.where(i1 == float(e), local, loc1)
        loc2 = jnp.where(i2 == float(e), local, loc2)
    row1 = loc1 + rinfo[2:3]
    row2 = loc2 + rinfo[3:4]
    info_ref[0] = jnp.concatenate([row1, row2, rinfo[4:6], jnp.zeros((V7X_SUBLANES - 4, TS), F32)], axis=0)
    p_idx = lax.broadcasted_iota(jnp.int32, (SORT_ROWS, TS), 0).astype(F32)
    sel = jnp.where(jnp.logical_or(p_idx == row1, p_idx == row2), 1.0, 0.0).astype(BF16)
    sorted_scr[slot] = jnp.dot(sel, u_ref[...], preferred_element_type=F32)

    def segment_copies(tile, s, do):
        def make_copy(local, glob, size):
            return pltpu.make_async_copy(sorted_scr.at[s, pl.ds(local, size)], xg_ref.at[pl.ds(glob, size)],
                                         sem.at[s])
        _segment_copies(off_ref, loc_ref, cnt_ref, tile, make_copy, do)

    @pl.when(i > 0)
    def _():
        segment_copies(i - 1, 1 - slot, lambda cp: cp.wait())

    segment_copies(i, slot, lambda cp: cp.start())

    @pl.when(i == n_steps - 1)
    def _():
        segment_copies(i, slot, lambda cp: cp.wait())
        zero_scr[...] = jnp.zeros_like(zero_scr)
        for e in range(N_EXP):
            end = fill_ref[e]
            n_tail = fill_ref[N_EXP + e] // SEG_PAD

            def tail_copy(k):
                return pltpu.make_async_copy(
                    zero_scr.at[pl.ds(0, SEG_PAD)],
                    xg_ref.at[pl.ds(pl.multiple_of(end + k * SEG_PAD, SEG_PAD), SEG_PAD)], zsem)

            def tail_start(k, carry):
                tail_copy(k).start()
                return carry

            def tail_wait(k, carry):
                tail_copy(k).wait()
                return carry

            lax.fori_loop(0, n_tail, tail_start, 0)
            lax.fori_loop(0, n_tail, tail_wait, 0)

        def tile_copy(k):
            return pltpu.make_async_copy(zero_scr, xg_ref.at[pl.ds(pl.multiple_of(k * TR, TR), TR)], zsem)

        def tile_start(k, carry):
            tile_copy(k).start()
            return carry

        def tile_wait(k, carry):
            tile_copy(k).wait()
            return carry

        lax.fori_loop(fill_ref[2 * N_EXP], MAX_RT, tile_start, 0)
        lax.fori_loop(fill_ref[2 * N_EXP], MAX_RT, tile_wait, 0)


def _dispatch(u2b, rinfo, off, loc, cnt, fill):
    n_tiles = rinfo.shape[0]
    tile3 = lambda i, *_: (i, 0, 0)
    return pl.pallas_call(
        _dispatch_kernel,
        grid_spec=pltpu.PrefetchScalarGridSpec(
            num_scalar_prefetch=4,
            grid=(n_tiles,),
            in_specs=[
                pl.BlockSpec((1, V7X_SUBLANES, TS), tile3),
                pl.BlockSpec((TS, D), lambda i, *_: (i, 0)),
            ],
            out_specs=[
                pl.BlockSpec(memory_space=pl.ANY),
                pl.BlockSpec((1, V7X_SUBLANES, TS), tile3),
            ],
            scratch_shapes=[
                pltpu.VMEM((2, SORT_ROWS, D), F32),
                pltpu.VMEM((TR, D), F32),
                pltpu.SemaphoreType.DMA((2,)),
                pltpu.SemaphoreType.DMA,
            ],
        ),
        out_shape=[
            jax.ShapeDtypeStruct((MAX_RT * TR, D), F32),
            jax.ShapeDtypeStruct((n_tiles, V7X_SUBLANES, TS), F32),
        ],
        compiler_params=pltpu.CompilerParams(
            dimension_semantics=("arbitrary",), vmem_limit_bytes=VMEM_LIMIT),
        name="moe_dispatch",
    )(off, loc, cnt, fill, rinfo, u2b)


def _expert_kernel(second, exp_ref, new_ref, nact_ref, *refs):
    if second:
        xg_ref, w1_ref, w3_ref, w2_ref, yp_ref, o_ref, w13b, w2b = refs
    else:
        xg_ref, w1_ref, w3_ref, w2_ref, o_ref, w13b, w2b = refs
    s = pl.program_id(0)
    r = s - 1
    active = jnp.logical_and(s >= 1, r < nact_ref[0])

    @pl.when(jnp.logical_and(active, new_ref[jnp.maximum(r, 0)] == 1))
    def _():
        w2b[...] = w2_ref[0].astype(BF16)

    @pl.when(active)
    def _():
        xb = xg_ref[...].astype(BF16)
        h = jnp.dot(xb, w13b[...], preferred_element_type=F32)
        h1 = h[:, 0:FC]
        hid = (h1 * _sigmoid(h1) * h[:, FC:2 * FC]).astype(BF16)
        p = jnp.dot(hid, w2b[...], preferred_element_type=F32)
        if second:
            p = yp_ref[...] + p
        o_ref[...] = p

    @pl.when(jnp.logical_and(s >= 1, jnp.logical_not(active)))
    def _():
        o_ref[...] = jnp.zeros_like(o_ref)

    @pl.when(new_ref[s] == 1)
    def _():
        w13b[:, 0:FC] = w1_ref[0].astype(BF16)
        w13b[:, FC:2 * FC] = w3_ref[0].astype(BF16)


def _expert_pass(second, tables, xg, w1, w3, w2, yp=None):
    f = 1 if second else 0
    tile = lambda s: jnp.maximum(s - 1, 0)
    ahead = lambda s: jnp.minimum(s, MAX_RT - 1)
    row = lambda s, ex, nw, na: (tile(s), 0)
    in_specs = [
        pl.BlockSpec((TR, D), row),
        pl.BlockSpec((1, D, FC), lambda s, ex, nw, na: (ex[ahead(s)], 0, f)),
        pl.BlockSpec((1, D, FC), lambda s, ex, nw, na: (ex[ahead(s)], 0, f)),
        pl.BlockSpec((1, FC, D), lambda s, ex, nw, na: (ex[tile(s)], f, 0)),
    ]
    args = [xg, w1, w3, w2]
    if second:
        in_specs.append(pl.BlockSpec((TR, D), row))
        args.append(yp)
    return pl.pallas_call(
        functools.partial(_expert_kernel, second),
        grid_spec=pltpu.PrefetchScalarGridSpec(
            num_scalar_prefetch=3,
            grid=(MAX_RT + 1,),
            in_specs=in_specs,
            out_specs=pl.BlockSpec((TR, D), row),
            scratch_shapes=[pltpu.VMEM((D, 2 * FC), BF16), pltpu.VMEM((FC, D), BF16)],
        ),
        out_shape=jax.ShapeDtypeStruct((MAX_RT * TR, D), F32),
        compiler_params=pltpu.CompilerParams(
            dimension_semantics=("arbitrary",), vmem_limit_bytes=EXPERT_VMEM_LIMIT),
        name="moe_expert_hi" if second else "moe_expert_lo",
    )(*tables, *args)


def _combine_kernel(n_ctx_tok, off_ref, loc_ref, cnt_ref, x_ref, mod_ref, info_ref, y_ref, g_ref, b_ref,
                    oc_ref, od_ref, ys_scr, sem):
    i = pl.program_id(0)
    n_steps = pl.num_programs(0)
    slot = i % 2

    def fetch(tile, s, do):
        def make_copy(local, glob, size):
            return pltpu.make_async_copy(y_ref.at[pl.ds(glob, size)], ys_scr.at[s, pl.ds(local, size)],
                                         sem.at[s])
        _segment_copies(off_ref, loc_ref, cnt_ref, tile, make_copy, do)

    def start_fetch(tile, s):
        ys_scr[s] = jnp.zeros((SORT_ROWS, D), F32)
        fetch(tile, s, lambda cp: cp.start())

    @pl.when(i == 0)
    def _():
        start_fetch(0, 0)

    @pl.when(i + 1 < n_steps)
    def _():
        start_fetch(i + 1, 1 - slot)

    info = info_ref[0]
    row1, row2, w1, w2 = info[0:1], info[1:2], info[2:3], info[3:4]
    p_idx = lax.broadcasted_iota(jnp.int32, (SORT_ROWS, TS), 0).astype(F32)
    w_rows = jnp.sum(jnp.where(p_idx == row1, w1, 0.0) + jnp.where(p_idx == row2, w2, 0.0),
                     axis=1, keepdims=True)
    padded = jnp.concatenate([info, jnp.zeros((V7X_LANES - V7X_SUBLANES, TS), F32)], axis=0)
    cols = jnp.transpose(padded, (1, 0))
    q_idx = lax.broadcasted_iota(jnp.int32, (TS, SORT_ROWS), 1).astype(F32)
    pick = jnp.where(jnp.logical_or(q_idx == cols[:, 0:1], q_idx == cols[:, 1:2]), 1.0, 0.0).astype(BF16)
    fetch(i, slot, lambda cp: cp.wait())
    scaled = (ys_scr[slot] * w_rows).astype(BF16)
    acc = jnp.dot(pick, scaled, preferred_element_type=F32)
    res = _ffn_epilogue(n_ctx_tok, x_ref, mod_ref, acc, g_ref, b_ref)
    is_dec = i * TS >= n_ctx_tok

    @pl.when(jnp.logical_not(is_dec))
    def _():
        oc_ref[...] = res

    @pl.when(is_dec)
    def _():
        od_ref[...] = res


def _combine(x, mod_l, info, y, segs, ln_g, ln_b, n_ctx_tok):
    n_tok = x.shape[0]
    n_tiles = n_tok // TS
    ncb = n_ctx_tok // TS
    const = lambda i, *_: (0, 0)
    return pl.pallas_call(
        functools.partial(_combine_kernel, n_ctx_tok),
        grid_spec=pltpu.PrefetchScalarGridSpec(
            num_scalar_prefetch=3,
            grid=(n_tiles,),
            in_specs=[
                pl.BlockSpec((TS, D), lambda i, *_: (i, 0)),
                pl.BlockSpec((MOD_ROWS, N_MOD * D), const),
                pl.BlockSpec((1, V7X_SUBLANES, TS), lambda i, *_: (i, 0, 0)),
                pl.BlockSpec(memory_space=pl.ANY),
                pl.BlockSpec((1, D), const),
                pl.BlockSpec((1, D), const),
            ],
            out_specs=[
                pl.BlockSpec((TS, D), lambda i, *_: (jnp.minimum(i, ncb - 1), 0)),
                pl.BlockSpec((TS, D), lambda i, *_: (jnp.maximum(i - ncb, 0), 0)),
            ],
            scratch_shapes=[
                pltpu.VMEM((2, SORT_ROWS, D), F32),
                pltpu.SemaphoreType.DMA((2,)),
            ],
        ),
        out_shape=[
            jax.ShapeDtypeStruct((n_ctx_tok, D), F32),
            jax.ShapeDtypeStruct((n_tok - n_ctx_tok, D), F32),
        ],
        compiler_params=pltpu.CompilerParams(
            dimension_semantics=("arbitrary",), vmem_limit_bytes=VMEM_LIMIT),
        name="moe_combine",
    )(*segs, x, mod_l, info, y, ln_g, ln_b)


def _routing_tables(counts):
    counts = (counts + SEG_PAD - 1) // SEG_PAD * SEG_PAD
    totals = jnp.sum(counts, axis=0)
    n_rt = (totals + TR - 1) // TR
    cum = jnp.cumsum(n_rt)
    first = cum - n_rt
    n_act = cum[-1]
    off = first[None, :] * TR + (jnp.cumsum(counts, axis=0) - counts)
    loc = jnp.cumsum(counts, axis=1) - counts
    fill = jnp.concatenate([first * TR + totals, n_rt * TR - totals, n_act[None]])
    r = jnp.arange(MAX_RT, dtype=jnp.int32)
    rc = jnp.minimum(r, n_act - 1)
    exp = jnp.sum((rc[:, None] >= cum[None, :]).astype(jnp.int32), axis=1)
    new = jnp.concatenate([jnp.logical_and(r == first[exp], r < n_act), jnp.zeros((1,), bool)])
    i32 = lambda a: a.astype(jnp.int32)
    segs = (i32(off.reshape(-1)), i32(loc.reshape(-1)), i32(counts.reshape(-1)))
    return segs, i32(fill), (i32(exp), i32(new), i32(n_act.reshape(1)))


def _moe_ffn(x, mod_l, router_w, router_b, w1, w3, w2, ln_g, ln_b, n_ctx_tok):
    n_tok = x.shape[0]
    n_seg_pad = (n_tok // TS) * N_EXP * (SEG_PAD - 1)
    assert (2 * n_tok + n_seg_pad + TR - 1) // TR + N_EXP == MAX_RT and N_FC == 2
    wrt = router_w.T
    brt = jnp.broadcast_to(router_b.reshape(N_EXP, 1), (N_EXP, V7X_LANES))
    rinfo, counts, u2b = _rank(x, mod_l, wrt, brt, n_ctx_tok)
    segs, fill, tables = _routing_tables(counts)
    xg, info = _dispatch(u2b, rinfo, *segs, fill)
    y_lo = _expert_pass(False, tables, xg, w1, w3, w2)
    y = _expert_pass(True, tables, xg, w1, w3, w2, y_lo)
    return _combine(x, mod_l, info, y, segs, ln_g, ln_b, n_ctx_tok)


def _cast_kernel(*refs):
    o_ref = refs[-1]
    off = 0
    for x_ref in refs[:-1]:
        width = x_ref.shape[-1]
        o_ref[..., off:off + width] = x_ref[...].reshape(o_ref.shape[:-1] + (width,)).astype(o_ref.dtype)
        off += width


def _pack_w13(w1, w3):
    return pl.pallas_call(
        _cast_kernel,
        grid=(N_FC,),
        in_specs=[pl.BlockSpec((D, FC), lambda f: (0, f)), pl.BlockSpec((D, FC), lambda f: (0, f))],
        out_specs=pl.BlockSpec((1, D, 2 * FC), lambda f: (f, 0, 0)),
        out_shape=jax.ShapeDtypeStruct((N_FC, D, 2 * FC), BF16),
        compiler_params=pltpu.CompilerParams(vmem_limit_bytes=VMEM_LIMIT),
        name="pack_w13",
    )(w1, w3)


def _pack_w_in(w_in):
    depth = w_in.shape[0]
    def regroup_kernel(x_ref, o_ref):
        for j in range(N_CH):
            o_ref[0, j] = x_ref[0, :, j * CW:(j + 1) * CW].astype(BF16)

    w5 = pl.pallas_call(
        regroup_kernel,
        grid=(depth, 5),
        in_specs=[pl.BlockSpec((1, D, D), lambda l, g: (l, 0, g))],
        out_specs=pl.BlockSpec((1, N_CH, D, CW), lambda l, g: (l, 0, 0, g)),
        out_shape=jax.ShapeDtypeStruct((depth, N_CH, D, 5 * CW), BF16),
        compiler_params=pltpu.CompilerParams(vmem_limit_bytes=VMEM_LIMIT),
        name="pack_w5",
    )(w_in)
    wg = pl.pallas_call(
        _cast_kernel,
        grid=(depth, 2),
        in_specs=[pl.BlockSpec((1, D, D), lambda l, k: (l, 0, 5 + k))],
        out_specs=pl.BlockSpec((1, D, D), lambda l, k: (l, 0, k)),
        out_shape=jax.ShapeDtypeStruct((depth, D, 2 * D), BF16),
        compiler_params=pltpu.CompilerParams(vmem_limit_bytes=VMEM_LIMIT),
        name="pack_wg",
    )(w_in)
    return w5, wg


def _block_diag_chunks(w):
    tiled = jnp.tile(w.reshape(N_CH, CW, HEAD_D), (1, 1, CW // HEAD_D))
    blk = jnp.arange(CW) // HEAD_D
    return jnp.where(blk[:, None] == blk[None, :], tiled, 0.0)


def kernel(x_prompt, x_sample, state_rglru, c, c_ctx, w_mod, b_mod, w_in, conv_a, w_a_out, conv_b, conv_b_bias, w_gate_a, b_gate_a, w_gate_x, b_gate_x, lru_lambda, w_b_out, w_o, ln1_g, ln1_b, ln2_g, ln2_b, ffn_w1, ffn_w3, ffn_w2, router_w, router_b, moe_w1, moe_w3, moe_w2):
    batch, seq, d = x_prompt.shape
    dec_batch, dec_seq, _ = x_sample.shape
    depth = w_mod.shape[0]
    assert (d, seq, dec_seq, depth) == (D, SEQ, DEC_SEQ, DEPTH)
    n_ctx_tok = batch * seq
    n_dec_tok = dec_batch * dec_seq
    assert n_ctx_tok % TM1 == 0 and TM1 == dec_seq and 1 + dec_batch <= MOD_ROWS
    n_ctx_tiles = n_ctx_tok // TM1
    n_tiles = n_ctx_tiles + dec_batch
    seq_per_tile = TM1 // seq
    assert N_SEG == 2 * seq_per_tile

    assert depth % 2 == 0
    n_tok = n_ctx_tok + n_dec_tok
    x = (x_prompt.reshape(n_ctx_tok, D), x_sample.reshape(n_dec_tok, D))

    cond = jnp.zeros((MOD_ROWS, D), F32).at[0].set(c_ctx).at[1:1 + dec_batch].set(c)
    mod = _modulation(cond, w_mod, b_mod)

    w5, wg = _pack_w_in(w_in)
    states = []
    u_pre = None
    for l in range(depth):
        wbd = jnp.concatenate(
            [_block_diag_chunks(w_gate_a[l, 0]), _block_diag_chunks(w_gate_x[l, 0]),
             _block_diag_chunks(w_gate_a[l, 1]), _block_diag_chunks(w_gate_x[l, 1])],
            axis=-1).astype(BF16)
        gbias = jnp.stack([b_gate_a[l, 0], b_gate_x[l, 0], b_gate_a[l, 1], b_gate_x[l, 1]], axis=0)
        h0 = jnp.zeros((n_tiles, 2, N_SEG, D), F32)
        h0 = h0.at[n_ctx_tiles:, 0, 0].set(state_rglru[:, l, 0].astype(F32))
        h0 = h0.at[n_ctx_tiles:, 1, N_SEG - 1].set(state_rglru[:, l, 1].astype(F32))

        a_pre, b_pre, st = _mixer_part1(
            x, u_pre, mod[l], l, w5, conv_a[l], conv_b[l], conv_b_bias[l].reshape(1, D), wbd, gbias,
            lru_lambda[l], h0, n_tok, n_ctx_tok)
        x = _mixer_part2(
            x, u_pre, mod[l], l, a_pre, b_pre, wg, w_a_out[l].astype(BF16), w_b_out[l].astype(BF16),
            w_o[l].astype(BF16), ln1_g[l].reshape(1, D), ln1_b[l].reshape(1, D), n_ctx_tok)
        u_pre = None

        if l % 2 == 0:
            k = l // 2
            x, u_pre = _dense_ffn(x, mod[l], mod[l + 1], _pack_w13(ffn_w1[k], ffn_w3[k]),
                                  ffn_w2[k].astype(BF16), ln2_g[l].reshape(1, D), ln2_b[l].reshape(1, D),
                                  n_ctx_tok)
        else:
            k = l // 2
            x = _moe_ffn(x, mod[l], router_w[k], router_b[k], moe_w1[k], moe_w3[k], moe_w2[k],
                         ln2_g[l].reshape(1, D), ln2_b[l].reshape(1, D), n_ctx_tok)

        st_ctx = st[:n_ctx_tiles]
        fwd = st_ctx[:, 0, 1::2].reshape(batch, D)
        bwd = st_ctx[:, 1, 0::2].reshape(batch, D)
        states.append(jnp.stack([fwd, bwd], axis=1))

    y_prompt = x[0].reshape(batch, seq, D)
    y_sample = x[1].reshape(dec_batch, dec_seq, D)
    new_state = jnp.stack(states, axis=1).astype(x_prompt.dtype)
    return (y_prompt, y_sample, new_state)
```

```python
import functools

import jax
import jax.numpy as jnp
from jax import lax
from jax.experimental import pallas as pl
from jax.experimental.pallas import tpu as pltpu

F32 = jnp.float32
BF16 = jnp.bfloat16

D = 1024
SEQ = 256
DEC_SEQ = 1024
GRID_W = 64
N_HEAD = 16
HEAD_D = D // N_HEAD
RGLRU_C = 8.0
LOG2_E = 1.4426950408889634
D_FF = 2816
N_EXP = 8
N_MOD = 6
DEPTH = 2
DN_ALPHA = (2.0 * DEPTH) ** 0.25

V7X_SUBLANES = 8
V7X_LANES = 128
V7X_VMEM_BYTES = 64 * 1024 * 1024
VMEM_LIMIT = V7X_VMEM_BYTES - 12 * 1024 * 1024

TM1 = 1024
CW = 256
N_CH = D // CW
N_LC = CW // V7X_LANES
N_SEG = V7X_SUBLANES
SEG = TM1 // N_SEG
SEG_STRIDE = SEG + 4
CONV_GAP = V7X_SUBLANES
TM2 = 512
TM3 = 512
FC = 1408
N_FC = D_FF // FC
TS = 512
TR = 256
SEG_PAD = V7X_SUBLANES
SEG_BITS = 7
SORT_ROWS = 2 * TS + V7X_LANES
MAX_RT = (2 * 10240 + (10240 // TS) * N_EXP * (SEG_PAD - 1) + TR - 1) // TR + N_EXP
EXPERT_VMEM_LIMIT = V7X_VMEM_BYTES - 6 * 1024 * 1024
MOD_ROWS = 8
MOD_BLK = 3072


def _sigmoid(x):
    return 0.5 * jnp.tanh(0.5 * x) + 0.5


def _ln_plain(x, eps):
    mu = jnp.mean(x, axis=-1, keepdims=True)
    xc = x - mu
    var = jnp.mean(xc * xc, axis=-1, keepdims=True)
    return xc * lax.rsqrt(var + eps)


def _token_source(xs, tile, n_ctx_tok):
    ncb = n_ctx_tok // tile
    if isinstance(xs, tuple):
        x_ctx, x_dec = xs
        dec_off = 0
    else:
        x_ctx = x_dec = xs
        dec_off = ncb
    ctx_map = lambda i: (jnp.minimum(i, ncb - 1), 0)
    dec_map = lambda i: (jnp.maximum(i - ncb, 0) + dec_off, 0)
    return x_ctx, x_dec, ctx_map, dec_map


def _mod_row(tok0, n_ctx_tok):
    dec = jnp.maximum(tok0 - n_ctx_tok, 0) // DEC_SEQ
    return jnp.where(tok0 >= n_ctx_tok, 1 + dec, 0)


def _mod_kernel(cond_ref, w_ref, b_ref, o_ref):
    cnd = cond_ref[...]
    s = cnd * _sigmoid(cnd)
    o_ref[0] = jnp.dot(s, w_ref[0], preferred_element_type=F32,
                       precision=lax.Precision.HIGHEST) + b_ref[0]


def _modulation(cond, w_mod, b_mod):
    depth = w_mod.shape[0]
    n_out = w_mod.shape[2]
    return pl.pallas_call(
        _mod_kernel,
        grid=(depth, n_out // MOD_BLK),
        in_specs=[
            pl.BlockSpec((MOD_ROWS, D), lambda l, j: (0, 0)),
            pl.BlockSpec((1, D, MOD_BLK), lambda l, j: (l, 0, j)),
            pl.BlockSpec((1, 1, MOD_BLK), lambda l, j: (l, 0, j)),
        ],
        out_specs=pl.BlockSpec((1, MOD_ROWS, MOD_BLK), lambda l, j: (l, 0, j)),
        out_shape=jax.ShapeDtypeStruct((depth, MOD_ROWS, n_out), F32),
        compiler_params=pltpu.CompilerParams(
            dimension_semantics=("arbitrary", "arbitrary"), vmem_limit_bytes=VMEM_LIMIT),
        name="modulation",
    )(cond, w_mod, b_mod.reshape(depth, 1, n_out))


def _scan_dir(a_scr, b_scr, hl_scr, ac_scr, h0, keep, reverse):
    n_lc = a_scr.shape[0]

    def body(k, carry):
        kk = SEG - 1 - k if reverse else k
        idx = pl.ds(kk, N_SEG, stride=SEG_STRIDE)
        new = []
        for c in range(n_lc):
            h, acc = carry[c]
            a_k = a_scr[c, idx, :]
            h = a_k * h + b_scr[c, idx, :]
            acc = a_k * acc
            hl_scr[c, idx, :] = h
            ac_scr[c, idx, :] = acc
            new.append((h, acc))
        return tuple(new)

    init = tuple((jnp.zeros((N_SEG, V7X_LANES), F32), jnp.ones((N_SEG, V7X_LANES), F32))
                 for _ in range(n_lc))
    fin = lax.fori_loop(0, SEG, body, init, unroll=8)
    h_loc = jnp.concatenate([fin[c][0] for c in range(n_lc)], axis=1)
    a_tot = jnp.concatenate([fin[c][1] for c in range(n_lc)], axis=1)

    order = range(N_SEG - 1, -1, -1) if reverse else range(N_SEG)
    h_in = [None] * N_SEG
    prev = None
    for s in order:
        cur = h0[s:s + 1]
        if prev is not None:
            left = a_tot[prev:prev + 1] * h_in[prev] + h_loc[prev:prev + 1]
            cur = keep[s] * left + cur
        h_in[s] = cur
        prev = s
    h_in = jnp.concatenate(h_in, axis=0)
    return h_in, a_tot * h_in + h_loc


def _conv_stage(is_dec, proj_scr, ca_ref, cb_ref, cbias_ref, apre_ref, xr_scr, gap_scr):
    win = GRID_W
    n_win = TM1 // win
    stride = win + CONV_GAP
    ca = ca_ref[...]
    cb = cb_ref[...]
    bias = cbias_ref[...]
    pieces = [(w, c) for w in range(n_win) for c in range(N_LC)]
    sub = lax.broadcasted_iota(jnp.int32, (CONV_GAP, V7X_LANES), 0)
    joined = jnp.where(is_dec, jnp.float32(0.0), jnp.float32(1.0))
    zero_gap = jnp.zeros((CONV_GAP, V7X_LANES), F32)

    def stage(value_of):
        for c in range(N_LC):
            for w in range(n_win + 1):
                edge = w * win
                if edge % SEQ == 0:
                    gap = zero_gap
                else:
                    before = value_of(slice(edge - CONV_GAP, edge), c)
                    after = value_of(slice(edge, edge + CONV_GAP), c)
                    gap = joined * jnp.where(sub >= CONV_GAP - 2, before, jnp.where(sub == 0, after, 0.0))
                gap_scr[c, w * stride:w * stride + CONV_GAP, :] = gap
        for w, c in pieces:
            lo = CONV_GAP + w * stride
            gap_scr[c, lo:lo + win, :] = value_of(slice(w * win, (w + 1) * win), c)

    def tap(w, c, shift):
        lo = CONV_GAP + w * stride + shift
        return gap_scr[c, lo:lo + win, :]

    def chunk_cols(k, c):
        return slice(k * CW + c * V7X_LANES, k * CW + (c + 1) * V7X_LANES)

    stage(lambda rows, c: proj_scr[rows, chunk_cols(2, c)] * proj_scr[rows, chunk_cols(0, c)])
    for w, c in pieces:
        rows = slice(w * win, (w + 1) * win)
        lanes = slice(c * V7X_LANES, (c + 1) * V7X_LANES)
        conv = ca[0:1, lanes] * tap(w, c, -1) + ca[1:2, lanes] * tap(w, c, 0) + ca[2:3, lanes] * tap(w, c, 1)
        apre_ref[rows, lanes] = (proj_scr[rows, chunk_cols(1, c)] * conv).astype(BF16)
    stage(lambda rows, c: proj_scr[rows, chunk_cols(4, c)])
    for w, c in pieces:
        rows = slice(w * win, (w + 1) * win)
        lanes = slice(c * V7X_LANES, (c + 1) * V7X_LANES)
        xr_scr[rows, lanes] = (cb[0:1, lanes] * tap(w, c, -2) + cb[1:2, lanes] * tap(w, c, -1)
                               + cb[2:3, lanes] * tap(w, c, 0) + cb[3:4, lanes] * tap(w, c, 1)
                               + bias[:, lanes])


def _mix1_kernel(n_ctx_tiles, pre_normed, *refs):
    if pre_normed:
        u_ref, refs = refs[0], refs[1:]
    else:
        xc_ref, xd_ref, mod_ref, refs = refs[0], refs[1], refs[2], refs[3:]
    (w5_ref, ca_ref, cb_ref, cbias_ref, wbd_ref, gb_ref, lam_ref, h0_ref, apre_ref, bpre_ref, st_ref,
     u_scr, proj_scr, a_scr, b_scr, hl_scr, ac_scr, hsum_scr, xr_scr, gap_scr) = refs
    i = pl.program_id(0)
    j = pl.program_id(1)
    is_dec = i >= n_ctx_tiles

    if pre_normed:
        proj_scr[...] = jnp.dot(u_ref[...], w5_ref[0, 0], preferred_element_type=F32)
    else:
        @pl.when(j == 0)
        def _():
            row = jnp.where(is_dec, i - (n_ctx_tiles - 1), 0)
            m = mod_ref[pl.ds(row, 1), :]
            sh1 = m[:, 0:D]
            sc1 = m[:, D:2 * D]
            x = jnp.where(is_dec, xd_ref[...], xc_ref[...])
            u_scr[...] = (_ln_plain(x, 1e-6) * (1.0 + sc1) + sh1).astype(BF16)

        proj_scr[...] = jnp.dot(u_scr[...], w5_ref[0, 0], preferred_element_type=F32)

    _conv_stage(is_dec, proj_scr, ca_ref, cb_ref, cbias_ref, apre_ref, xr_scr, gap_scr)

    xr = xr_scr[...]
    gates = jnp.dot(xr.astype(BF16), wbd_ref[0], preferred_element_type=F32)
    gb = gb_ref[...]
    lam = lam_ref[...]
    sp = jnp.maximum(-lam, 0.0) + jnp.log1p(jnp.exp(-jnp.abs(lam)))
    rate = (-RGLRU_C * LOG2_E) * sp

    one = jnp.float32(1.0)
    for d in range(2):
        ga = gates[:, (2 * d) * CW:(2 * d + 1) * CW] + gb[2 * d:2 * d + 1]
        gx = gates[:, (2 * d + 1) * CW:(2 * d + 2) * CW] + gb[2 * d + 1:2 * d + 2]
        r = _sigmoid(ga)
        ig = _sigmoid(gx)
        a = jnp.exp2(r * rate[d:d + 1])
        y = 1.0 - a * a
        bt = jnp.where(y > 0.0, y * lax.rsqrt(y), 0.0) * (ig * xr)
        for s in range(N_SEG):
            lo = s * SEG_STRIDE
            for c in range(N_LC):
                lanes = slice(c * V7X_LANES, (c + 1) * V7X_LANES)
                a_scr[c, lo:lo + SEG, :] = a[s * SEG:(s + 1) * SEG, lanes]
                b_scr[c, lo:lo + SEG, :] = bt[s * SEG:(s + 1) * SEG, lanes]
        if d == 0:
            keep = [jnp.where(is_dec, one, jnp.float32(s % 2 == 1)) for s in range(N_SEG)]
        else:
            keep = [jnp.where(is_dec, one, jnp.float32(s % 2 == 0)) for s in range(N_SEG)]
        h_in, h_out = _scan_dir(a_scr, b_scr, hl_scr, ac_scr, h0_ref[0, d], keep, reverse=(d == 1))
        st_ref[0, d] = h_out
        for s in range(N_SEG):
            lo = s * SEG_STRIDE
            for c in range(N_LC):
                lanes = slice(c * V7X_LANES, (c + 1) * V7X_LANES)
                h_seg = hl_scr[c, lo:lo + SEG, :] + ac_scr[c, lo:lo + SEG, :] * h_in[s:s + 1, lanes]
                if d == 0:
                    hsum_scr[s * SEG:(s + 1) * SEG, lanes] = h_seg
                else:
                    hsum_scr[s * SEG:(s + 1) * SEG, lanes] += h_seg

    bpre_ref[...] = (hsum_scr[...] * jax.nn.gelu(proj_scr[:, 3 * CW:4 * CW])).astype(BF16)


def _mixer_part1(xs, u_pre, mod_l, layer, w5, conv_a, conv_b, conv_b_bias, wbd, gbias, lam, h0, n_tok,
                 n_ctx_tok):
    n_tiles = n_tok // TM1
    if u_pre is None:
        x_ctx, x_dec, ctx_map, dec_map = _token_source(xs, TM1, n_ctx_tok)
        lead_args = [x_ctx, x_dec, mod_l]
        lead_specs = [
            pl.BlockSpec((TM1, D), lambda i, j: ctx_map(i)),
            pl.BlockSpec((TM1, D), lambda i, j: dec_map(i)),
            pl.BlockSpec((MOD_ROWS, N_MOD * D), lambda i, j: (0, 0)),
        ]
    else:
        lead_args = [u_pre]
        lead_specs = [pl.BlockSpec((TM1, D), lambda i, j: (i, 0))]
    kern = functools.partial(_mix1_kernel, n_ctx_tok // TM1, u_pre is not None)
    return pl.pallas_call(
        kern,
        grid=(n_tiles, N_CH),
        in_specs=lead_specs + [
            pl.BlockSpec((1, 1, D, 5 * CW), lambda i, j: (layer, j, 0, 0)),
            pl.BlockSpec((3, CW), lambda i, j: (0, j)),
            pl.BlockSpec((4, CW), lambda i, j: (0, j)),
            pl.BlockSpec((1, CW), lambda i, j: (0, j)),
            pl.BlockSpec((1, CW, 4 * CW), lambda i, j: (j, 0, 0)),
            pl.BlockSpec((4, CW), lambda i, j: (0, j)),
            pl.BlockSpec((2, CW), lambda i, j: (0, j)),
            pl.BlockSpec((1, 2, N_SEG, CW), lambda i, j: (i, 0, 0, j)),
        ],
        out_specs=[
            pl.BlockSpec((TM1, CW), lambda i, j: (i, j)),
            pl.BlockSpec((TM1, CW), lambda i, j: (i, j)),
            pl.BlockSpec((1, 2, N_SEG, CW), lambda i, j: (i, 0, 0, j)),
        ],
        out_shape=[
            jax.ShapeDtypeStruct((n_tok, D), BF16),
            jax.ShapeDtypeStruct((n_tok, D), BF16),
            jax.ShapeDtypeStruct((n_tiles, 2, N_SEG, D), F32),
        ],
        scratch_shapes=[
            pltpu.VMEM((TM1, D), BF16),
            pltpu.VMEM((TM1, 5 * CW), F32),
            pltpu.VMEM((N_LC, N_SEG * SEG_STRIDE, V7X_LANES), F32),
            pltpu.VMEM((N_LC, N_SEG * SEG_STRIDE, V7X_LANES), F32),
            pltpu.VMEM((N_LC, N_SEG * SEG_STRIDE, V7X_LANES), F32),
            pltpu.VMEM((N_LC, N_SEG * SEG_STRIDE, V7X_LANES), F32),
            pltpu.VMEM((TM1, CW), F32),
            pltpu.VMEM((TM1, CW), F32),
            pltpu.VMEM((N_LC, CONV_GAP + (TM1 // GRID_W) * (GRID_W + CONV_GAP), V7X_LANES), F32),
        ],
        compiler_params=pltpu.CompilerParams(
            dimension_semantics=("arbitrary", "arbitrary"), vmem_limit_bytes=VMEM_LIMIT),
        name="mixer_scan",
    )(*lead_args, w5, conv_a, conv_b, conv_b_bias, wbd, gbias, lam, h0)


def _mix2_kernel(n_ctx_tok, pre_normed, *refs):
    if pre_normed:
        u_ref, refs = refs[0], refs[1:]
    xc_ref, xd_ref, mod_ref, ap_ref, bp_ref, wg_ref, wa_ref, wb_ref, wo_ref, g_ref, b_ref, o_ref = refs
    i = pl.program_id(0)
    m = mod_ref[pl.ds(_mod_row(i * TM2, n_ctx_tok), 1), :]
    sh1 = m[:, 0:D]
    sc1 = m[:, D:2 * D]
    g1 = m[:, 2 * D:3 * D]
    x = jnp.where(i * TM2 >= n_ctx_tok, xd_ref[...], xc_ref[...])
    if pre_normed:
        u = u_ref[...]
    else:
        u = (_ln_plain(x, 1e-6) * (1.0 + sc1) + sh1).astype(BF16)
    gates = jnp.dot(u, wg_ref[0], preferred_element_type=F32)
    br_a = jnp.dot(ap_ref[...], wa_ref[...], preferred_element_type=F32)
    br_b = jnp.dot(bp_ref[...], wb_ref[...], preferred_element_type=F32)
    merged = _sigmoid(gates[:, 0:D]) * br_a + _sigmoid(gates[:, D:2 * D]) * br_b
    mix = jnp.dot(merged.astype(BF16), wo_ref[...], preferred_element_type=F32)
    y = DN_ALPHA * x + g1 * mix
    o_ref[...] = _ln_plain(y, 1e-5) * g_ref[...] + b_ref[...]


def _mixer_part2(xs, u_pre, mod_l, layer, a_pre, b_pre, wg, wa, wb, wo, ln_g, ln_b, n_ctx_tok):
    n_tok = a_pre.shape[0]
    x_ctx, x_dec, ctx_map, dec_map = _token_source(xs, TM2, n_ctx_tok)
    const = lambda i: (0, 0)
    tile = lambda i: (i, 0)
    lead_args = [] if u_pre is None else [u_pre]
    lead_specs = [] if u_pre is None else [pl.BlockSpec((TM2, D), tile)]
    return pl.pallas_call(
        functools.partial(_mix2_kernel, n_ctx_tok, u_pre is not None),
        grid=(n_tok // TM2,),
        in_specs=lead_specs + [
            pl.BlockSpec((TM2, D), ctx_map),
            pl.BlockSpec((TM2, D), dec_map),
            pl.BlockSpec((MOD_ROWS, N_MOD * D), const),
            pl.BlockSpec((TM2, D), tile),
            pl.BlockSpec((TM2, D), tile),
            pl.BlockSpec((1, D, 2 * D), lambda i: (layer, 0, 0)),
            pl.BlockSpec((D, D), const),
            pl.BlockSpec((D, D), const),
            pl.BlockSpec((D, D), const),
            pl.BlockSpec((1, D), const),
            pl.BlockSpec((1, D), const),
        ],
        out_specs=pl.BlockSpec((TM2, D), tile),
        out_shape=jax.ShapeDtypeStruct((n_tok, D), F32),
        compiler_params=pltpu.CompilerParams(
            dimension_semantics=("arbitrary",), vmem_limit_bytes=VMEM_LIMIT),
        name="mixer_out",
    )(*lead_args, x_ctx, x_dec, mod_l, a_pre, b_pre, wg, wa, wb, wo, ln_g, ln_b)


def _ffn_prologue(n_ctx_tok, x_ref, mod_ref):
    m = mod_ref[pl.ds(_mod_row(pl.program_id(0) * TM3, n_ctx_tok), 1), :]
    sh2 = m[:, 3 * D:4 * D]
    sc2 = m[:, 4 * D:5 * D]
    return _ln_plain(x_ref[...], 1e-6) * (1.0 + sc2) + sh2


def _ffn_epilogue(n_ctx_tok, x_ref, mod_ref, acc, g_ref, b_ref):
    m = mod_ref[pl.ds(_mod_row(pl.program_id(0) * TM3, n_ctx_tok), 1), :]
    g2 = m[:, 5 * D:6 * D]
    y = DN_ALPHA * x_ref[...] + g2 * acc
    return _ln_plain(y, 1e-5) * g_ref[...] + b_ref[...]


def _swiglu_chunk(u, w13, w2):
    h = jnp.dot(u, w13, preferred_element_type=F32)
    h1 = h[:, 0:FC]
    hid = (h1 * _sigmoid(h1) * h[:, FC:2 * FC]).astype(BF16)
    return jnp.dot(hid, w2, preferred_element_type=F32)


def _dense_ffn_kernel(n_ctx_tok, x_ref, mod_ref, modn_ref, w13_ref, w2_ref, g_ref, b_ref, o_ref, un_ref,
                      u_scr, acc_scr):
    f = pl.program_id(1)

    @pl.when(f == 0)
    def _():
        u_scr[...] = _ffn_prologue(n_ctx_tok, x_ref, mod_ref).astype(BF16)

    p = _swiglu_chunk(u_scr[...], w13_ref[0], w2_ref[...])

    @pl.when(f == 0)
    def _():
        acc_scr[...] = p

    @pl.when(f > 0)
    def _():
        acc_scr[...] += p

    @pl.when(f == N_FC - 1)
    def _():
        res = _ffn_epilogue(n_ctx_tok, x_ref, mod_ref, acc_scr[...], g_ref, b_ref)
        o_ref[...] = res
        m = modn_ref[pl.ds(_mod_row(pl.program_id(0) * TM3, n_ctx_tok), 1), :]
        un_ref[...] = (_ln_plain(res, 1e-6) * (1.0 + m[:, D:2 * D]) + m[:, 0:D]).astype(BF16)


def _dense_ffn(x, mod_l, mod_next, w13, w2, ln_g, ln_b, n_ctx_tok):
    n_tok = x.shape[0]
    tile = pl.BlockSpec((TM3, D), lambda i, f: (i, 0))
    table = pl.BlockSpec((MOD_ROWS, N_MOD * D), lambda i, f: (0, 0))
    return pl.pallas_call(
        functools.partial(_dense_ffn_kernel, n_ctx_tok),
        grid=(n_tok // TM3, N_FC),
        in_specs=[
            tile,
            table,
            table,
            pl.BlockSpec((1, D, 2 * FC), lambda i, f: (f, 0, 0)),
            pl.BlockSpec((FC, D), lambda i, f: (f, 0)),
            pl.BlockSpec((1, D), lambda i, f: (0, 0)),
            pl.BlockSpec((1, D), lambda i, f: (0, 0)),
        ],
        out_specs=[tile, tile],
        out_shape=[jax.ShapeDtypeStruct((n_tok, D), F32), jax.ShapeDtypeStruct((n_tok, D), BF16)],
        scratch_shapes=[pltpu.VMEM((TM3, D), BF16), pltpu.VMEM((TM3, D), F32)],
        compiler_params=pltpu.CompilerParams(
            dimension_semantics=("arbitrary", "arbitrary"), vmem_limit_bytes=VMEM_LIMIT),
        name="dense_ffn",
    )(x, mod_l, mod_next, w13, w2, ln_g, ln_b)


def _segment_copies(off_ref, loc_ref, cnt_ref, tile, make_copy, do):
    for e in range(N_EXP):
        n = cnt_ref[tile * N_EXP + e]
        local = loc_ref[tile * N_EXP + e]
        glob = off_ref[tile * N_EXP + e]
        for k in range(SEG_BITS - 1, -1, -1):
            size = SEG_PAD << k
            take = (n & size) != 0

            @pl.when(take)
            def _():
                do(make_copy(pl.multiple_of(local, SEG_PAD), pl.multiple_of(glob, SEG_PAD), size))

            step = jnp.where(take, size, 0)
            local = local + step
            glob = glob + step


def _rank_kernel(n_ctx_tok, x_ref, mod_ref, wrt_ref, brt_ref, info_ref, cnt_ref, u2b_ref):
    i = pl.program_id(0)
    u2 = _ffn_prologue(n_ctx_tok, x_ref, mod_ref)
    u2b_ref[...] = u2.astype(BF16)
    lg = lax.dot_general(wrt_ref[...], u2, (((1,), (1,)), ((), ())), preferred_element_type=F32,
                         precision=lax.Precision.HIGHEST) + brt_ref[:, 0:1]
    eidx = lax.broadcasted_iota(jnp.int32, lg.shape, 0).astype(F32)
    neg = jnp.float32(-jnp.inf)
    v1 = jnp.max(lg, axis=0, keepdims=True)
    i1 = jnp.min(jnp.where(lg == v1, eidx, float(N_EXP)), axis=0, keepdims=True)
    lg2 = jnp.where(eidx == i1, neg, lg)
    v2 = jnp.max(lg2, axis=0, keepdims=True)
    i2 = jnp.min(jnp.where(lg2 == v2, eidx, float(N_EXP)), axis=0, keepdims=True)
    t = jnp.exp(v2 - v1)
    w_top = 1.0 / (1.0 + t)
    m1 = eidx == i1
    m2 = eidx == i2
    member = jnp.where(jnp.logical_or(m1, m2), 1.0, 0.0)
    before = (lax.broadcasted_iota(jnp.int32, (TS, TS), 0)
              < lax.broadcasted_iota(jnp.int32, (TS, TS), 1))
    rank = jnp.dot(member.astype(BF16), jnp.where(before, 1.0, 0.0).astype(BF16),
                   preferred_element_type=F32)
    for e in range(N_EXP):
        cnt_ref[i, e] = jnp.sum(member[e:e + 1, :]).astype(jnp.int32)
    rank1 = jnp.sum(jnp.where(m1, rank, 0.0), axis=0, keepdims=True)
    rank2 = jnp.sum(jnp.where(m2, rank, 0.0), axis=0, keepdims=True)
    info_ref[0] = jnp.concatenate(
        [i1, i2, rank1, rank2, w_top, t * w_top, jnp.zeros((V7X_SUBLANES - 6, TS), F32)], axis=0)


def _rank(x, mod_l, wrt, brt, n_ctx_tok):
    n_tiles = x.shape[0] // TS
    return pl.pallas_call(
        functools.partial(_rank_kernel, n_ctx_tok),
        grid=(n_tiles,),
        in_specs=[
            pl.BlockSpec((TS, D), lambda i: (i, 0)),
            pl.BlockSpec((MOD_ROWS, N_MOD * D), lambda i: (0, 0)),
            pl.BlockSpec((N_EXP, D), lambda i: (0, 0)),
            pl.BlockSpec((N_EXP, V7X_LANES), lambda i: (0, 0)),
        ],
        out_specs=[
            pl.BlockSpec((1, V7X_SUBLANES, TS), lambda i: (i, 0, 0)),
            pl.BlockSpec(memory_space=pltpu.SMEM),
            pl.BlockSpec((TS, D), lambda i: (i, 0)),
        ],
        out_shape=[
            jax.ShapeDtypeStruct((n_tiles, V7X_SUBLANES, TS), F32),
            jax.ShapeDtypeStruct((n_tiles, N_EXP), jnp.int32),
            jax.ShapeDtypeStruct((n_tiles * TS, D), BF16),
        ],
        compiler_params=pltpu.CompilerParams(
            dimension_semantics=("arbitrary",), vmem_limit_bytes=VMEM_LIMIT),
        name="moe_rank",
    )(x, mod_l, wrt, brt)


def _dispatch_kernel(off_ref, loc_ref, cnt_ref, fill_ref, rinfo_ref, u_ref, xg_ref, info_ref,
                     sorted_scr, zero_scr, sem, zsem):
    i = pl.program_id(0)
    n_steps = pl.num_programs(0)
    slot = i % 2
    rinfo = rinfo_ref[0]
    i1 = rinfo[0:1]
    i2 = rinfo[1:2]
    loc1 = jnp.zeros_like(i1)
    loc2 = jnp.zeros_like(i2)
    for e in range(N_EXP):
        local = loc_ref[i * N_EXP + e].astype(F32)
        loc1 = jnp.where(i1 == float(e), local, loc1)
        loc2 = jnp.where(i2 == float(e), local, loc2)
    row1 = loc1 + rinfo[2:3]
    row2 = loc2 + rinfo[3:4]
    info_ref[0] = jnp.concatenate([row1, row2, rinfo[4:6], jnp.zeros((V7X_SUBLANES - 4, TS), F32)], axis=0)
    p_idx = lax.broadcasted_iota(jnp.int32, (SORT_ROWS, TS), 0).astype(F32)
    sel = jnp.where(jnp.logical_or(p_idx == row1, p_idx == row2), 1.0, 0.0).astype(BF16)
    sorted_scr[slot] = jnp.dot(sel, u_ref[...], preferred_element_type=F32)

    def segment_copies(tile, s, do):
        def make_copy(local, glob, size):
            return pltpu.make_async_copy(sorted_scr.at[s, pl.ds(local, size)], xg_ref.at[pl.ds(glob, size)],
                                         sem.at[s])
        _segment_copies(off_ref, loc_ref, cnt_ref, tile, make_copy, do)

    @pl.when(i > 0)
    def _():
        segment_copies(i - 1, 1 - slot, lambda cp: cp.wait())

    segment_copies(i, slot, lambda cp: cp.start())

    @pl.when(i == n_steps - 1)
    def _():
        segment_copies(i, slot, lambda cp: cp.wait())
        zero_scr[...] = jnp.zeros_like(zero_scr)
        for e in range(N_EXP):
            end = fill_ref[e]
            n_tail = fill_ref[N_EXP + e] // SEG_PAD

            def tail_copy(k):
                return pltpu.make_async_copy(
                    zero_scr.at[pl.ds(0, SEG_PAD)],
                    xg_ref.at[pl.ds(pl.multiple_of(end + k * SEG_PAD, SEG_PAD), SEG_PAD)], zsem)

            def tail_start(k, carry):
                tail_copy(k).start()
                return carry

            def tail_wait(k, carry):
                tail_copy(k).wait()
                return carry

            lax.fori_loop(0, n_tail, tail_start, 0)
            lax.fori_loop(0, n_tail, tail_wait, 0)

        def tile_copy(k):
            return pltpu.make_async_copy(zero_scr, xg_ref.at[pl.ds(pl.multiple_of(k * TR, TR), TR)], zsem)

        def tile_start(k, carry):
            tile_copy(k).start()
            return carry

        def tile_wait(k, carry):
            tile_copy(k).wait()
            return carry

        lax.fori_loop(fill_ref[2 * N_EXP], MAX_RT, tile_start, 0)
        lax.fori_loop(fill_ref[2 * N_EXP], MAX_RT, tile_wait, 0)


def _dispatch(u2b, rinfo, off, loc, cnt, fill):
    n_tiles = rinfo.shape[0]
    tile3 = lambda i, *_: (i, 0, 0)
    return pl.pallas_call(
        _dispatch_kernel,
        grid_spec=pltpu.PrefetchScalarGridSpec(
            num_scalar_prefetch=4,
            grid=(n_tiles,),
            in_specs=[
                pl.BlockSpec((1, V7X_SUBLANES, TS), tile3),
                pl.BlockSpec((TS, D), lambda i, *_: (i, 0)),
            ],
            out_specs=[
                pl.BlockSpec(memory_space=pl.ANY),
                pl.BlockSpec((1, V7X_SUBLANES, TS), tile3),
            ],
            scratch_shapes=[
                pltpu.VMEM((2, SORT_ROWS, D), F32),
                pltpu.VMEM((TR, D), F32),
                pltpu.SemaphoreType.DMA((2,)),
                pltpu.SemaphoreType.DMA,
            ],
        ),
        out_shape=[
            jax.ShapeDtypeStruct((MAX_RT * TR, D), F32),
            jax.ShapeDtypeStruct((n_tiles, V7X_SUBLANES, TS), F32),
        ],
        compiler_params=pltpu.CompilerParams(
            dimension_semantics=("arbitrary",), vmem_limit_bytes=VMEM_LIMIT),
        name="moe_dispatch",
    )(off, loc, cnt, fill, rinfo, u2b)


def _expert_kernel(second, exp_ref, new_ref, nact_ref, *refs):
    if second:
        xg_ref, w1_ref, w3_ref, w2_ref, yp_ref, o_ref, w13b, w2b = refs
    else:
        xg_ref, w1_ref, w3_ref, w2_ref, o_ref, w13b, w2b = refs
    s = pl.program_id(0)
    r = s - 1
    active = jnp.logical_and(s >= 1, r < nact_ref[0])

    @pl.when(jnp.logical_and(active, new_ref[jnp.maximum(r, 0)] == 1))
    def _():
        w2b[...] = w2_ref[0].astype(BF16)

    @pl.when(active)
    def _():
        xb = xg_ref[...].astype(BF16)
        h = jnp.dot(xb, w13b[...], preferred_element_type=F32)
        h1 = h[:, 0:FC]
        hid = (h1 * _sigmoid(h1) * h[:, FC:2 * FC]).astype(BF16)
        p = jnp.dot(hid, w2b[...], preferred_element_type=F32)
        if second:
            p = yp_ref[...] + p
        o_ref[...] = p

    @pl.when(jnp.logical_and(s >= 1, jnp.logical_not(active)))
    def _():
        o_ref[...] = jnp.zeros_like(o_ref)

    @pl.when(new_ref[s] == 1)
    def _():
        w13b[:, 0:FC] = w1_ref[0].astype(BF16)
        w13b[:, FC:2 * FC] = w3_ref[0].astype(BF16)


def _expert_pass(second, tables, xg, w1, w3, w2, yp=None):
    f = 1 if second else 0
    tile = lambda s: jnp.maximum(s - 1, 0)
    ahead = lambda s: jnp.minimum(s, MAX_RT - 1)
    row = lambda s, ex, nw, na: (tile(s), 0)
    row_in = lambda s, ex, nw, na: (jnp.minimum(tile(s), na[0] - 1), 0)
    in_specs = [
        pl.BlockSpec((TR, D), row_in),
        pl.BlockSpec((1, D, FC), lambda s, ex, nw, na: (ex[ahead(s)], 0, f)),
        pl.BlockSpec((1, D, FC), lambda s, ex, nw, na: (ex[ahead(s)], 0, f)),
        pl.BlockSpec((1, FC, D), lambda s, ex, nw, na: (ex[tile(s)], f, 0)),
    ]
    args = [xg, w1, w3, w2]
    if second:
        in_specs.append(pl.BlockSpec((TR, D), row_in))
        args.append(yp)
    return pl.pallas_call(
        functools.partial(_expert_kernel, second),
        grid_spec=pltpu.PrefetchScalarGridSpec(
            num_scalar_prefetch=3,
            grid=(MAX_RT + 1,),
            in_specs=in_specs,
            out_specs=pl.BlockSpec((TR, D), row),
            scratch_shapes=[pltpu.VMEM((D, 2 * FC), BF16), pltpu.VMEM((FC, D), BF16)],
        ),
        out_shape=jax.ShapeDtypeStruct((MAX_RT * TR, D), F32),
        compiler_params=pltpu.CompilerParams(
            dimension_semantics=("arbitrary",), vmem_limit_bytes=EXPERT_VMEM_LIMIT),
        name="moe_expert_hi" if second else "moe_expert_lo",
    )(*tables, *args)


def _combine_kernel(n_ctx_tok, off_ref, loc_ref, cnt_ref, x_ref, mod_ref, info_ref, y_ref, g_ref, b_ref,
                    oc_ref, od_ref, ys_scr, sem):
    i = pl.program_id(0)
    n_steps = pl.num_programs(0)
    slot = i % 2

    def fetch(tile, s, do):
        def make_copy(local, glob, size):
            return pltpu.make_async_copy(y_ref.at[pl.ds(glob, size)], ys_scr.at[s, pl.ds(local, size)],
                                         sem.at[s])
        _segment_copies(off_ref, loc_ref, cnt_ref, tile, make_copy, do)

    def start_fetch(tile, s):
        ys_scr[s, 2 * TS:SORT_ROWS, :] = jnp.zeros((SORT_ROWS - 2 * TS, D), F32)
        fetch(tile, s, lambda cp: cp.start())

    @pl.when(i == 0)
    def _():
        start_fetch(0, 0)

    @pl.when(i + 1 < n_steps)
    def _():
        start_fetch(i + 1, 1 - slot)

    info = info_ref[0]
    row1, row2, w1, w2 = info[0:1], info[1:2], info[2:3], info[3:4]
    p_idx = lax.broadcasted_iota(jnp.int32, (SORT_ROWS, TS), 0).astype(F32)
    w_rows = jnp.sum(jnp.where(p_idx == row1, w1, 0.0) + jnp.where(p_idx == row2, w2, 0.0),
                     axis=1, keepdims=True)
    padded = jnp.concatenate([info, jnp.zeros((V7X_LANES - V7X_SUBLANES, TS), F32)], axis=0)
    cols = jnp.transpose(padded, (1, 0))
    q_idx = lax.broadcasted_iota(jnp.int32, (TS, SORT_ROWS), 1).astype(F32)
    pick = jnp.where(jnp.logical_or(q_idx == cols[:, 0:1], q_idx == cols[:, 1:2]), 1.0, 0.0).astype(BF16)
    fetch(i, slot, lambda cp: cp.wait())
    scaled = (ys_scr[slot] * w_rows).astype(BF16)
    acc = jnp.dot(pick, scaled, preferred_element_type=F32)
    res = _ffn_epilogue(n_ctx_tok, x_ref, mod_ref, acc, g_ref, b_ref)
    is_dec = i * TS >= n_ctx_tok

    @pl.when(jnp.logical_not(is_dec))
    def _():
        oc_ref[...] = res

    @pl.when(is_dec)
    def _():
        od_ref[...] = res


def _combine(x, mod_l, info, y, segs, ln_g, ln_b, n_ctx_tok):
    n_tok = x.shape[0]
    n_tiles = n_tok // TS
    ncb = n_ctx_tok // TS
    const = lambda i, *_: (0, 0)
    return pl.pallas_call(
        functools.partial(_combine_kernel, n_ctx_tok),
        grid_spec=pltpu.PrefetchScalarGridSpec(
            num_scalar_prefetch=3,
            grid=(n_tiles,),
            in_specs=[
                pl.BlockSpec((TS, D), lambda i, *_: (i, 0)),
                pl.BlockSpec((MOD_ROWS, N_MOD * D), const),
                pl.BlockSpec((1, V7X_SUBLANES, TS), lambda i, *_: (i, 0, 0)),
                pl.BlockSpec(memory_space=pl.ANY),
                pl.BlockSpec((1, D), const),
                pl.BlockSpec((1, D), const),
            ],
            out_specs=[
                pl.BlockSpec((TS, D), lambda i, *_: (jnp.minimum(i, ncb - 1), 0)),
                pl.BlockSpec((TS, D), lambda i, *_: (jnp.maximum(i - ncb, 0), 0)),
            ],
            scratch_shapes=[
                pltpu.VMEM((2, SORT_ROWS, D), F32),
                pltpu.SemaphoreType.DMA((2,)),
            ],
        ),
        out_shape=[
            jax.ShapeDtypeStruct((n_ctx_tok, D), F32),
            jax.ShapeDtypeStruct((n_tok - n_ctx_tok, D), F32),
        ],
        compiler_params=pltpu.CompilerParams(
            dimension_semantics=("arbitrary",), vmem_limit_bytes=VMEM_LIMIT),
        name="moe_combine",
    )(*segs, x, mod_l, info, y, ln_g, ln_b)


def _routing_tables(counts):
    counts = (counts + SEG_PAD - 1) // SEG_PAD * SEG_PAD
    totals = jnp.sum(counts, axis=0)
    n_rt = (totals + TR - 1) // TR
    cum = jnp.cumsum(n_rt)
    first = cum - n_rt
    n_act = cum[-1]
    off = first[None, :] * TR + (jnp.cumsum(counts, axis=0) - counts)
    loc = jnp.cumsum(counts, axis=1) - counts
    fill = jnp.concatenate([first * TR + totals, n_rt * TR - totals, n_act[None]])
    r = jnp.arange(MAX_RT, dtype=jnp.int32)
    rc = jnp.minimum(r, n_act - 1)
    exp = jnp.sum((rc[:, None] >= cum[None, :]).astype(jnp.int32), axis=1)
    new = jnp.concatenate([jnp.logical_and(r == first[exp], r < n_act), jnp.zeros((1,), bool)])
    i32 = lambda a: a.astype(jnp.int32)
    segs = (i32(off.reshape(-1)), i32(loc.reshape(-1)), i32(counts.reshape(-1)))
    return segs, i32(fill), (i32(exp), i32(new), i32(n_act.reshape(1)))


def _moe_ffn(x, mod_l, router_w, router_b, w1, w3, w2, ln_g, ln_b, n_ctx_tok):
    n_tok = x.shape[0]
    n_seg_pad = (n_tok // TS) * N_EXP * (SEG_PAD - 1)
    assert (2 * n_tok + n_seg_pad + TR - 1) // TR + N_EXP == MAX_RT and N_FC == 2
    wrt = router_w.T
    brt = jnp.broadcast_to(router_b.reshape(N_EXP, 1), (N_EXP, V7X_LANES))
    rinfo, counts, u2b = _rank(x, mod_l, wrt, brt, n_ctx_tok)
    segs, fill, tables = _routing_tables(counts)
    xg, info = _dispatch(u2b, rinfo, *segs, fill)
    y_lo = _expert_pass(False, tables, xg, w1, w3, w2)
    y = _expert_pass(True, tables, xg, w1, w3, w2, y_lo)
    return _combine(x, mod_l, info, y, segs, ln_g, ln_b, n_ctx_tok)


def _cast_kernel(*refs):
    o_ref = refs[-1]
    off = 0
    for x_ref in refs[:-1]:
        width = x_ref.shape[-1]
        o_ref[..., off:off + width] = x_ref[...].reshape(o_ref.shape[:-1] + (width,)).astype(o_ref.dtype)
        off += width


def _pack_w13(w1, w3):
    return pl.pallas_call(
        _cast_kernel,
        grid=(N_FC,),
        in_specs=[pl.BlockSpec((D, FC), lambda f: (0, f)), pl.BlockSpec((D, FC), lambda f: (0, f))],
        out_specs=pl.BlockSpec((1, D, 2 * FC), lambda f: (f, 0, 0)),
        out_shape=jax.ShapeDtypeStruct((N_FC, D, 2 * FC), BF16),
        compiler_params=pltpu.CompilerParams(vmem_limit_bytes=VMEM_LIMIT),
        name="pack_w13",
    )(w1, w3)


def _pack_w_in(w_in):
    depth = w_in.shape[0]
    def regroup_kernel(x_ref, o_ref):
        for j in range(N_CH):
            o_ref[0, j] = x_ref[0, :, j * CW:(j + 1) * CW].astype(BF16)

    w5 = pl.pallas_call(
        regroup_kernel,
        grid=(depth, 5),
        in_specs=[pl.BlockSpec((1, D, D), lambda l, g: (l, 0, g))],
        out_specs=pl.BlockSpec((1, N_CH, D, CW), lambda l, g: (l, 0, 0, g)),
        out_shape=jax.ShapeDtypeStruct((depth, N_CH, D, 5 * CW), BF16),
        compiler_params=pltpu.CompilerParams(vmem_limit_bytes=VMEM_LIMIT),
        name="pack_w5",
    )(w_in)
    wg = pl.pallas_call(
        _cast_kernel,
        grid=(depth, 2),
        in_specs=[pl.BlockSpec((1, D, D), lambda l, k: (l, 0, 5 + k))],
        out_specs=pl.BlockSpec((1, D, D), lambda l, k: (l, 0, k)),
        out_shape=jax.ShapeDtypeStruct((depth, D, 2 * D), BF16),
        compiler_params=pltpu.CompilerParams(vmem_limit_bytes=VMEM_LIMIT),
        name="pack_wg",
    )(w_in)
    return w5, wg


def _block_diag_chunks(w):
    tiled = jnp.tile(w.reshape(N_CH, CW, HEAD_D), (1, 1, CW // HEAD_D))
    blk = jnp.arange(CW) // HEAD_D
    return jnp.where(blk[:, None] == blk[None, :], tiled, 0.0)


def kernel(x_prompt, x_sample, state_rglru, c, c_ctx, w_mod, b_mod, w_in, conv_a, w_a_out, conv_b, conv_b_bias, w_gate_a, b_gate_a, w_gate_x, b_gate_x, lru_lambda, w_b_out, w_o, ln1_g, ln1_b, ln2_g, ln2_b, ffn_w1, ffn_w3, ffn_w2, router_w, router_b, moe_w1, moe_w3, moe_w2):
    batch, seq, d = x_prompt.shape
    dec_batch, dec_seq, _ = x_sample.shape
    depth = w_mod.shape[0]
    assert (d, seq, dec_seq, depth) == (D, SEQ, DEC_SEQ, DEPTH)
    n_ctx_tok = batch * seq
    n_dec_tok = dec_batch * dec_seq
    assert n_ctx_tok % TM1 == 0 and TM1 == dec_seq and 1 + dec_batch <= MOD_ROWS
    n_ctx_tiles = n_ctx_tok // TM1
    n_tiles = n_ctx_tiles + dec_batch
    seq_per_tile = TM1 // seq
    assert N_SEG == 2 * seq_per_tile

    assert depth % 2 == 0
    n_tok = n_ctx_tok + n_dec_tok
    x = (x_prompt.reshape(n_ctx_tok, D), x_sample.reshape(n_dec_tok, D))

    cond = jnp.zeros((MOD_ROWS, D), F32).at[0].set(c_ctx).at[1:1 + dec_batch].set(c)
    mod = _modulation(cond, w_mod, b_mod)

    w5, wg = _pack_w_in(w_in)
    states = []
    u_pre = None
    for l in range(depth):
        wbd = jnp.concatenate(
            [_block_diag_chunks(w_gate_a[l, 0]), _block_diag_chunks(w_gate_x[l, 0]),
             _block_diag_chunks(w_gate_a[l, 1]), _block_diag_chunks(w_gate_x[l, 1])],
            axis=-1).astype(BF16)
        gbias = jnp.stack([b_gate_a[l, 0], b_gate_x[l, 0], b_gate_a[l, 1], b_gate_x[l, 1]], axis=0)
        h0 = jnp.zeros((n_tiles, 2, N_SEG, D), F32)
        h0 = h0.at[n_ctx_tiles:, 0, 0].set(state_rglru[:, l, 0].astype(F32))
        h0 = h0.at[n_ctx_tiles:, 1, N_SEG - 1].set(state_rglru[:, l, 1].astype(F32))

        a_pre, b_pre, st = _mixer_part1(
            x, u_pre, mod[l], l, w5, conv_a[l], conv_b[l], conv_b_bias[l].reshape(1, D), wbd, gbias,
            lru_lambda[l], h0, n_tok, n_ctx_tok)
        x = _mixer_part2(
            x, u_pre, mod[l], l, a_pre, b_pre, wg, w_a_out[l].astype(BF16), w_b_out[l].astype(BF16),
            w_o[l].astype(BF16), ln1_g[l].reshape(1, D), ln1_b[l].reshape(1, D), n_ctx_tok)
        u_pre = None

        if l % 2 == 0:
            k = l // 2
            x, u_pre = _dense_ffn(x, mod[l], mod[l + 1], _pack_w13(ffn_w1[k], ffn_w3[k]),
                                  ffn_w2[k].astype(BF16), ln2_g[l].reshape(1, D), ln2_b[l].reshape(1, D),
                                  n_ctx_tok)
        else:
            k = l // 2
            x = _moe_ffn(x, mod[l], router_w[k], router_b[k], moe_w1[k], moe_w3[k], moe_w2[k],
                         ln2_g[l].reshape(1, D), ln2_b[l].reshape(1, D), n_ctx_tok)

        st_ctx = st[:n_ctx_tiles]
        fwd = st_ctx[:, 0, 1::2].reshape(batch, D)
        bwd = st_ctx[:, 1, 0::2].reshape(batch, D)
        states.append(jnp.stack([fwd, bwd], axis=1))

    y_prompt = x[0].reshape(batch, seq, D)
    y_sample = x[1].reshape(dec_batch, dec_seq, D)
    new_state = jnp.stack(states, axis=1).astype(x_prompt.dtype)
    return (y_prompt, y_sample, new_state)
```

```python
import functools

import jax
import jax.numpy as jnp
from jax import lax
from jax.experimental import pallas as pl
from jax.experimental.pallas import tpu as pltpu

F32 = jnp.float32
BF16 = jnp.bfloat16

D = 1024
SEQ = 256
DEC_SEQ = 1024
GRID_W = 64
N_HEAD = 16
HEAD_D = D // N_HEAD
RGLRU_C = 8.0
LOG2_E = 1.4426950408889634
D_FF = 2816
N_EXP = 8
N_MOD = 6
DEPTH = 2
DN_ALPHA = (2.0 * DEPTH) ** 0.25

V7X_SUBLANES = 8
V7X_LANES = 128
V7X_VMEM_BYTES = 64 * 1024 * 1024
VMEM_LIMIT = V7X_VMEM_BYTES - 12 * 1024 * 1024

TM1 = 1024
CW = 256
N_CH = D // CW
N_LC = CW // V7X_LANES
N_SEG = V7X_SUBLANES
SEG = TM1 // N_SEG
SEG_STRIDE = SEG + 4
CONV_GAP = V7X_SUBLANES
TM2 = 512
TM3 = 512
FC = 1408
N_FC = D_FF // FC
TS = 512
TR = 256
SEG_PAD = V7X_SUBLANES
SEG_BITS = 7
SORT_ROWS = 2 * TS + V7X_LANES
MAX_RT = (2 * 10240 + (10240 // TS) * N_EXP * (SEG_PAD - 1) + TR - 1) // TR + N_EXP
EXPERT_VMEM_LIMIT = V7X_VMEM_BYTES - 6 * 1024 * 1024
MOD_ROWS = 8
MOD_BLK = 3072


def _sigmoid(x):
    return 0.5 * jnp.tanh(0.5 * x) + 0.5


def _ln_plain(x, eps):
    mu = jnp.mean(x, axis=-1, keepdims=True)
    xc = x - mu
    var = jnp.mean(xc * xc, axis=-1, keepdims=True)
    return xc * lax.rsqrt(var + eps)


def _token_source(xs, tile, n_ctx_tok):
    ncb = n_ctx_tok // tile
    if isinstance(xs, tuple):
        x_ctx, x_dec = xs
        dec_off = 0
    else:
        x_ctx = x_dec = xs
        dec_off = ncb
    ctx_map = lambda i: (jnp.minimum(i, ncb - 1), 0)
    dec_map = lambda i: (jnp.maximum(i - ncb, 0) + dec_off, 0)
    return x_ctx, x_dec, ctx_map, dec_map


def _mod_row(tok0, n_ctx_tok):
    dec = jnp.maximum(tok0 - n_ctx_tok, 0) // DEC_SEQ
    return jnp.where(tok0 >= n_ctx_tok, 1 + dec, 0)


def _mod_kernel(cond_ref, w_ref, b_ref, o_ref):
    cnd = cond_ref[...]
    s = cnd * _sigmoid(cnd)
    o_ref[0] = jnp.dot(s, w_ref[0], preferred_element_type=F32,
                       precision=lax.Precision.HIGHEST) + b_ref[0]


def _modulation(cond, w_mod, b_mod):
    depth = w_mod.shape[0]
    n_out = w_mod.shape[2]
    return pl.pallas_call(
        _mod_kernel,
        grid=(depth, n_out // MOD_BLK),
        in_specs=[
            pl.BlockSpec((MOD_ROWS, D), lambda l, j: (0, 0)),
            pl.BlockSpec((1, D, MOD_BLK), lambda l, j: (l, 0, j)),
            pl.BlockSpec((1, 1, MOD_BLK), lambda l, j: (l, 0, j)),
        ],
        out_specs=pl.BlockSpec((1, MOD_ROWS, MOD_BLK), lambda l, j: (l, 0, j)),
        out_shape=jax.ShapeDtypeStruct((depth, MOD_ROWS, n_out), F32),
        compiler_params=pltpu.CompilerParams(
            dimension_semantics=("arbitrary", "arbitrary"), vmem_limit_bytes=VMEM_LIMIT),
        name="modulation",
    )(cond, w_mod, b_mod.reshape(depth, 1, n_out))


def _scan_dir(a_scr, b_scr, hl_scr, ac_scr, h0, keep, reverse):
    n_lc = a_scr.shape[0]

    def body(k, carry):
        kk = SEG - 1 - k if reverse else k
        idx = pl.ds(kk, N_SEG, stride=SEG_STRIDE)
        new = []
        for c in range(n_lc):
            h, acc = carry[c]
            a_k = a_scr[c, idx, :]
            h = a_k * h + b_scr[c, idx, :]
            acc = a_k * acc
            hl_scr[c, idx, :] = h
            ac_scr[c, idx, :] = acc
            new.append((h, acc))
        return tuple(new)

    init = tuple((jnp.zeros((N_SEG, V7X_LANES), F32), jnp.ones((N_SEG, V7X_LANES), F32))
                 for _ in range(n_lc))
    fin = lax.fori_loop(0, SEG, body, init, unroll=8)
    h_loc = jnp.concatenate([fin[c][0] for c in range(n_lc)], axis=1)
    a_tot = jnp.concatenate([fin[c][1] for c in range(n_lc)], axis=1)

    order = range(N_SEG - 1, -1, -1) if reverse else range(N_SEG)
    h_in = [None] * N_SEG
    prev = None
    for s in order:
        cur = h0[s:s + 1]
        if prev is not None:
            left = a_tot[prev:prev + 1] * h_in[prev] + h_loc[prev:prev + 1]
            cur = keep[s] * left + cur
        h_in[s] = cur
        prev = s
    h_in = jnp.concatenate(h_in, axis=0)
    return h_in, a_tot * h_in + h_loc


def _conv_stage(is_dec, proj_scr, ca_ref, cb_ref, cbias_ref, apre_ref, xr_scr, gap_scr):
    win = GRID_W
    n_win = TM1 // win
    stride = win + CONV_GAP
    ca = ca_ref[...]
    cb = cb_ref[...]
    bias = cbias_ref[...]
    pieces = [(w, c) for w in range(n_win) for c in range(N_LC)]
    sub = lax.broadcasted_iota(jnp.int32, (CONV_GAP, V7X_LANES), 0)
    joined = jnp.where(is_dec, jnp.float32(0.0), jnp.float32(1.0))
    zero_gap = jnp.zeros((CONV_GAP, V7X_LANES), F32)

    def stage(value_of):
        for c in range(N_LC):
            for w in range(n_win + 1):
                edge = w * win
                if edge % SEQ == 0:
                    gap = zero_gap
                else:
                    before = value_of(slice(edge - CONV_GAP, edge), c)
                    after = value_of(slice(edge, edge + CONV_GAP), c)
                    gap = joined * jnp.where(sub >= CONV_GAP - 2, before, jnp.where(sub == 0, after, 0.0))
                gap_scr[c, w * stride:w * stride + CONV_GAP, :] = gap
        for w, c in pieces:
            lo = CONV_GAP + w * stride
            gap_scr[c, lo:lo + win, :] = value_of(slice(w * win, (w + 1) * win), c)

    def tap(w, c, shift):
        lo = CONV_GAP + w * stride + shift
        return gap_scr[c, lo:lo + win, :]

    def chunk_cols(k, c):
        return slice(k * CW + c * V7X_LANES, k * CW + (c + 1) * V7X_LANES)

    stage(lambda rows, c: proj_scr[rows, chunk_cols(2, c)] * proj_scr[rows, chunk_cols(0, c)])
    for w, c in pieces:
        rows = slice(w * win, (w + 1) * win)
        lanes = slice(c * V7X_LANES, (c + 1) * V7X_LANES)
        conv = ca[0:1, lanes] * tap(w, c, -1) + ca[1:2, lanes] * tap(w, c, 0) + ca[2:3, lanes] * tap(w, c, 1)
        apre_ref[rows, lanes] = (proj_scr[rows, chunk_cols(1, c)] * conv).astype(BF16)
    stage(lambda rows, c: proj_scr[rows, chunk_cols(4, c)])
    for w, c in pieces:
        rows = slice(w * win, (w + 1) * win)
        lanes = slice(c * V7X_LANES, (c + 1) * V7X_LANES)
        xr_scr[rows, lanes] = (cb[0:1, lanes] * tap(w, c, -2) + cb[1:2, lanes] * tap(w, c, -1)
                               + cb[2:3, lanes] * tap(w, c, 0) + cb[3:4, lanes] * tap(w, c, 1)
                               + bias[:, lanes])


def _mix1_kernel(n_ctx_tiles, pre_normed, *refs):
    if pre_normed:
        u_ref, refs = refs[0], refs[1:]
    else:
        xc_ref, xd_ref, mod_ref, refs = refs[0], refs[1], refs[2], refs[3:]
    (w5_ref, ca_ref, cb_ref, cbias_ref, wbd_ref, gb_ref, lam_ref, h0_ref, apre_ref, bpre_ref, st_ref,
     u_scr, proj_scr, a_scr, b_scr, hl_scr, ac_scr, hsum_scr, xr_scr, gap_scr) = refs
    i = pl.program_id(0)
    j = pl.program_id(1)
    is_dec = i >= n_ctx_tiles

    if pre_normed:
        proj_scr[...] = jnp.dot(u_ref[...], w5_ref[0, 0], preferred_element_type=F32)
    else:
        @pl.when(j == 0)
        def _():
            row = jnp.where(is_dec, i - (n_ctx_tiles - 1), 0)
            m = mod_ref[pl.ds(row, 1), :]
            sh1 = m[:, 0:D]
            sc1 = m[:, D:2 * D]
            x = jnp.where(is_dec, xd_ref[...], xc_ref[...])
            u_scr[...] = (_ln_plain(x, 1e-6) * (1.0 + sc1) + sh1).astype(BF16)

        proj_scr[...] = jnp.dot(u_scr[...], w5_ref[0, 0], preferred_element_type=F32)

    _conv_stage(is_dec, proj_scr, ca_ref, cb_ref, cbias_ref, apre_ref, xr_scr, gap_scr)

    xr = xr_scr[...]
    gates = jnp.dot(xr.astype(BF16), wbd_ref[0], preferred_element_type=F32)
    gb = gb_ref[...]
    lam = lam_ref[...]
    sp = jnp.maximum(-lam, 0.0) + jnp.log1p(jnp.exp(-jnp.abs(lam)))
    rate = (-RGLRU_C * LOG2_E) * sp

    one = jnp.float32(1.0)
    for d in range(2):
        ga = gates[:, (2 * d) * CW:(2 * d + 1) * CW] + gb[2 * d:2 * d + 1]
        gx = gates[:, (2 * d + 1) * CW:(2 * d + 2) * CW] + gb[2 * d + 1:2 * d + 2]
        r = _sigmoid(ga)
        ig = _sigmoid(gx)
        a = jnp.exp2(r * rate[d:d + 1])
        y = 1.0 - a * a
        bt = jnp.where(y > 0.0, y * lax.rsqrt(y), 0.0) * (ig * xr)
        for s in range(N_SEG):
            lo = s * SEG_STRIDE
            for c in range(N_LC):
                lanes = slice(c * V7X_LANES, (c + 1) * V7X_LANES)
                a_scr[c, lo:lo + SEG, :] = a[s * SEG:(s + 1) * SEG, lanes]
                b_scr[c, lo:lo + SEG, :] = bt[s * SEG:(s + 1) * SEG, lanes]
        if d == 0:
            keep = [jnp.where(is_dec, one, jnp.float32(s % 2 == 1)) for s in range(N_SEG)]
        else:
            keep = [jnp.where(is_dec, one, jnp.float32(s % 2 == 0)) for s in range(N_SEG)]
        h_in, h_out = _scan_dir(a_scr, b_scr, hl_scr, ac_scr, h0_ref[0, d], keep, reverse=(d == 1))
        st_ref[0, d] = h_out
        for s in range(N_SEG):
            lo = s * SEG_STRIDE
            for c in range(N_LC):
                lanes = slice(c * V7X_LANES, (c + 1) * V7X_LANES)
                h_seg = hl_scr[c, lo:lo + SEG, :] + ac_scr[c, lo:lo + SEG, :] * h_in[s:s + 1, lanes]
                if d == 0:
                    hsum_scr[s * SEG:(s + 1) * SEG, lanes] = h_seg
                else:
                    hsum_scr[s * SEG:(s + 1) * SEG, lanes] += h_seg

    bpre_ref[...] = (hsum_scr[...] * jax.nn.gelu(proj_scr[:, 3 * CW:4 * CW])).astype(BF16)


def _mixer_part1(xs, u_pre, mod_l, layer, w5, conv_a, conv_b, conv_b_bias, wbd, gbias, lam, h0, n_tok,
                 n_ctx_tok):
    n_tiles = n_tok // TM1
    if u_pre is None:
        x_ctx, x_dec, ctx_map, dec_map = _token_source(xs, TM1, n_ctx_tok)
        lead_args = [x_ctx, x_dec, mod_l]
        lead_specs = [
            pl.BlockSpec((TM1, D), lambda i, j: ctx_map(i)),
            pl.BlockSpec((TM1, D), lambda i, j: dec_map(i)),
            pl.BlockSpec((MOD_ROWS, N_MOD * D), lambda i, j: (0, 0)),
        ]
    else:
        lead_args = [u_pre]
        lead_specs = [pl.BlockSpec((TM1, D), lambda i, j: (i, 0))]
    kern = functools.partial(_mix1_kernel, n_ctx_tok // TM1, u_pre is not None)
    return pl.pallas_call(
        kern,
        grid=(n_tiles, N_CH),
        in_specs=lead_specs + [
            pl.BlockSpec((1, 1, D, 5 * CW), lambda i, j: (layer, j, 0, 0)),
            pl.BlockSpec((3, CW), lambda i, j: (0, j)),
            pl.BlockSpec((4, CW), lambda i, j: (0, j)),
            pl.BlockSpec((1, CW), lambda i, j: (0, j)),
            pl.BlockSpec((1, CW, 4 * CW), lambda i, j: (j, 0, 0)),
            pl.BlockSpec((4, CW), lambda i, j: (0, j)),
            pl.BlockSpec((2, CW), lambda i, j: (0, j)),
            pl.BlockSpec((1, 2, N_SEG, CW), lambda i, j: (i, 0, 0, j)),
        ],
        out_specs=[
            pl.BlockSpec((TM1, CW), lambda i, j: (i, j)),
            pl.BlockSpec((TM1, CW), lambda i, j: (i, j)),
            pl.BlockSpec((1, 2, N_SEG, CW), lambda i, j: (i, 0, 0, j)),
        ],
        out_shape=[
            jax.ShapeDtypeStruct((n_tok, D), BF16),
            jax.ShapeDtypeStruct((n_tok, D), BF16),
            jax.ShapeDtypeStruct((n_tiles, 2, N_SEG, D), F32),
        ],
        scratch_shapes=[
            pltpu.VMEM((TM1, D), BF16),
            pltpu.VMEM((TM1, 5 * CW), F32),
            pltpu.VMEM((N_LC, N_SEG * SEG_STRIDE, V7X_LANES), F32),
            pltpu.VMEM((N_LC, N_SEG * SEG_STRIDE, V7X_LANES), F32),
            pltpu.VMEM((N_LC, N_SEG * SEG_STRIDE, V7X_LANES), F32),
            pltpu.VMEM((N_LC, N_SEG * SEG_STRIDE, V7X_LANES), F32),
            pltpu.VMEM((TM1, CW), F32),
            pltpu.VMEM((TM1, CW), F32),
            pltpu.VMEM((N_LC, CONV_GAP + (TM1 // GRID_W) * (GRID_W + CONV_GAP), V7X_LANES), F32),
        ],
        compiler_params=pltpu.CompilerParams(
            dimension_semantics=("arbitrary", "arbitrary"), vmem_limit_bytes=VMEM_LIMIT),
        name="mixer_scan",
    )(*lead_args, w5, conv_a, conv_b, conv_b_bias, wbd, gbias, lam, h0)


def _mix2_kernel(n_ctx_tok, pre_normed, *refs):
    if pre_normed:
        u_ref, refs = refs[0], refs[1:]
    xc_ref, xd_ref, mod_ref, ap_ref, bp_ref, wg_ref, wa_ref, wb_ref, wo_ref, g_ref, b_ref, o_ref = refs
    i = pl.program_id(0)
    m = mod_ref[pl.ds(_mod_row(i * TM2, n_ctx_tok), 1), :]
    sh1 = m[:, 0:D]
    sc1 = m[:, D:2 * D]
    g1 = m[:, 2 * D:3 * D]
    x = jnp.where(i * TM2 >= n_ctx_tok, xd_ref[...], xc_ref[...])
    if pre_normed:
        u = u_ref[...]
    else:
        u = (_ln_plain(x, 1e-6) * (1.0 + sc1) + sh1).astype(BF16)
    gates = jnp.dot(u, wg_ref[0], preferred_element_type=F32)
    br_a = jnp.dot(ap_ref[...], wa_ref[...], preferred_element_type=F32)
    br_b = jnp.dot(bp_ref[...], wb_ref[...], preferred_element_type=F32)
    merged = _sigmoid(gates[:, 0:D]) * br_a + _sigmoid(gates[:, D:2 * D]) * br_b
    mix = jnp.dot(merged.astype(BF16), wo_ref[...], preferred_element_type=F32)
    y = DN_ALPHA * x + g1 * mix
    o_ref[...] = _ln_plain(y, 1e-5) * g_ref[...] + b_ref[...]


def _mixer_part2(xs, u_pre, mod_l, layer, a_pre, b_pre, wg, wa, wb, wo, ln_g, ln_b, n_ctx_tok):
    n_tok = a_pre.shape[0]
    x_ctx, x_dec, ctx_map, dec_map = _token_source(xs, TM2, n_ctx_tok)
    const = lambda i: (0, 0)
    tile = lambda i: (i, 0)
    lead_args = [] if u_pre is None else [u_pre]
    lead_specs = [] if u_pre is None else [pl.BlockSpec((TM2, D), tile)]
    return pl.pallas_call(
        functools.partial(_mix2_kernel, n_ctx_tok, u_pre is not None),
        grid=(n_tok // TM2,),
        in_specs=lead_specs + [
            pl.BlockSpec((TM2, D), ctx_map),
            pl.BlockSpec((TM2, D), dec_map),
            pl.BlockSpec((MOD_ROWS, N_MOD * D), const),
            pl.BlockSpec((TM2, D), tile),
            pl.BlockSpec((TM2, D), tile),
            pl.BlockSpec((1, D, 2 * D), lambda i: (layer, 0, 0)),
            pl.BlockSpec((D, D), const),
            pl.BlockSpec((D, D), const),
            pl.BlockSpec((D, D), const),
            pl.BlockSpec((1, D), const),
            pl.BlockSpec((1, D), const),
        ],
        out_specs=pl.BlockSpec((TM2, D), tile),
        out_shape=jax.ShapeDtypeStruct((n_tok, D), F32),
        compiler_params=pltpu.CompilerParams(
            dimension_semantics=("arbitrary",), vmem_limit_bytes=VMEM_LIMIT),
        name="mixer_out",
    )(*lead_args, x_ctx, x_dec, mod_l, a_pre, b_pre, wg, wa, wb, wo, ln_g, ln_b)


def _ffn_prologue(n_ctx_tok, x_ref, mod_ref):
    m = mod_ref[pl.ds(_mod_row(pl.program_id(0) * TM3, n_ctx_tok), 1), :]
    sh2 = m[:, 3 * D:4 * D]
    sc2 = m[:, 4 * D:5 * D]
    return _ln_plain(x_ref[...], 1e-6) * (1.0 + sc2) + sh2


def _ffn_epilogue(n_ctx_tok, x_ref, mod_ref, acc, g_ref, b_ref):
    m = mod_ref[pl.ds(_mod_row(pl.program_id(0) * TM3, n_ctx_tok), 1), :]
    g2 = m[:, 5 * D:6 * D]
    y = DN_ALPHA * x_ref[...] + g2 * acc
    return _ln_plain(y, 1e-5) * g_ref[...] + b_ref[...]


def _swiglu_chunk(u, w13, w2):
    h = jnp.dot(u, w13, preferred_element_type=F32)
    h1 = h[:, 0:FC]
    hid = (h1 * _sigmoid(h1) * h[:, FC:2 * FC]).astype(BF16)
    return jnp.dot(hid, w2, preferred_element_type=F32)


def _dense_ffn_kernel(n_ctx_tok, x_ref, mod_ref, modn_ref, w13_ref, w2_ref, g_ref, b_ref, o_ref, un_ref,
                      u_scr, acc_scr):
    f = pl.program_id(1)

    @pl.when(f == 0)
    def _():
        u_scr[...] = _ffn_prologue(n_ctx_tok, x_ref, mod_ref).astype(BF16)

    p = _swiglu_chunk(u_scr[...], w13_ref[0], w2_ref[...])

    @pl.when(f == 0)
    def _():
        acc_scr[...] = p

    @pl.when(f > 0)
    def _():
        acc_scr[...] += p

    @pl.when(f == N_FC - 1)
    def _():
        res = _ffn_epilogue(n_ctx_tok, x_ref, mod_ref, acc_scr[...], g_ref, b_ref)
        o_ref[...] = res
        m = modn_ref[pl.ds(_mod_row(pl.program_id(0) * TM3, n_ctx_tok), 1), :]
        un_ref[...] = (_ln_plain(res, 1e-6) * (1.0 + m[:, D:2 * D]) + m[:, 0:D]).astype(BF16)


def _dense_ffn(x, mod_l, mod_next, w13, w2, ln_g, ln_b, n_ctx_tok):
    n_tok = x.shape[0]
    tile = pl.BlockSpec((TM3, D), lambda i, f: (i, 0))
    table = pl.BlockSpec((MOD_ROWS, N_MOD * D), lambda i, f: (0, 0))
    return pl.pallas_call(
        functools.partial(_dense_ffn_kernel, n_ctx_tok),
        grid=(n_tok // TM3, N_FC),
        in_specs=[
            tile,
            table,
            table,
            pl.BlockSpec((1, D, 2 * FC), lambda i, f: (f, 0, 0)),
            pl.BlockSpec((FC, D), lambda i, f: (f, 0)),
            pl.BlockSpec((1, D), lambda i, f: (0, 0)),
            pl.BlockSpec((1, D), lambda i, f: (0, 0)),
        ],
        out_specs=[tile, tile],
        out_shape=[jax.ShapeDtypeStruct((n_tok, D), F32), jax.ShapeDtypeStruct((n_tok, D), BF16)],
        scratch_shapes=[pltpu.VMEM((TM3, D), BF16), pltpu.VMEM((TM3, D), F32)],
        compiler_params=pltpu.CompilerParams(
            dimension_semantics=("arbitrary", "arbitrary"), vmem_limit_bytes=VMEM_LIMIT),
        name="dense_ffn",
    )(x, mod_l, mod_next, w13, w2, ln_g, ln_b)


def _segment_copies(off_ref, loc_ref, cnt_ref, tile, make_copy, do):
    for e in range(N_EXP):
        n = cnt_ref[tile * N_EXP + e]
        local = loc_ref[tile * N_EXP + e]
        glob = off_ref[tile * N_EXP + e]
        for k in range(SEG_BITS - 1, -1, -1):
            size = SEG_PAD << k
            take = (n & size) != 0

            @pl.when(take)
            def _():
                do(make_copy(pl.multiple_of(local, SEG_PAD), pl.multiple_of(glob, SEG_PAD), size))

            step = jnp.where(take, size, 0)
            local = local + step
            glob = glob + step


def _rank_kernel(n_ctx_tok, x_ref, mod_ref, wrt_ref, brt_ref, info_ref, cnt_ref, u2b_ref):
    i = pl.program_id(0)
    u2 = _ffn_prologue(n_ctx_tok, x_ref, mod_ref)
    u2b_ref[...] = u2.astype(BF16)
    lg = lax.dot_general(wrt_ref[...], u2, (((1,), (1,)), ((), ())), preferred_element_type=F32,
                         precision=lax.Precision.HIGHEST) + brt_ref[:, 0:1]
    eidx = lax.broadcasted_iota(jnp.int32, lg.shape, 0).astype(F32)
    neg = jnp.float32(-jnp.inf)
    v1 = jnp.max(lg, axis=0, keepdims=True)
    i1 = jnp.min(jnp.where(lg == v1, eidx, float(N_EXP)), axis=0, keepdims=True)
    lg2 = jnp.where(eidx == i1, neg, lg)
    v2 = jnp.max(lg2, axis=0, keepdims=True)
    i2 = jnp.min(jnp.where(lg2 == v2, eidx, float(N_EXP)), axis=0, keepdims=True)
    t = jnp.exp(v2 - v1)
    w_top = 1.0 / (1.0 + t)
    m1 = eidx == i1
    m2 = eidx == i2
    member = jnp.where(jnp.logical_or(m1, m2), 1.0, 0.0)
    before = (lax.broadcasted_iota(jnp.int32, (TS, TS), 0)
              < lax.broadcasted_iota(jnp.int32, (TS, TS), 1))
    rank = jnp.dot(member.astype(BF16), jnp.where(before, 1.0, 0.0).astype(BF16),
                   preferred_element_type=F32)
    for e in range(N_EXP):
        cnt_ref[i, e] = jnp.sum(member[e:e + 1, :]).astype(jnp.int32)
    rank1 = jnp.sum(jnp.where(m1, rank, 0.0), axis=0, keepdims=True)
    rank2 = jnp.sum(jnp.where(m2, rank, 0.0), axis=0, keepdims=True)
    info_ref[0] = jnp.concatenate(
        [i1, i2, rank1, rank2, w_top, t * w_top, jnp.zeros((V7X_SUBLANES - 6, TS), F32)], axis=0)


def _rank(x, mod_l, wrt, brt, n_ctx_tok):
    n_tiles = x.shape[0] // TS
    return pl.pallas_call(
        functools.partial(_rank_kernel, n_ctx_tok),
        grid=(n_tiles,),
        in_specs=[
            pl.BlockSpec((TS, D), lambda i: (i, 0)),
            pl.BlockSpec((MOD_ROWS, N_MOD * D), lambda i: (0, 0)),
            pl.BlockSpec((N_EXP, D), lambda i: (0, 0)),
            pl.BlockSpec((N_EXP, V7X_LANES), lambda i: (0, 0)),
        ],
        out_specs=[
            pl.BlockSpec((1, V7X_SUBLANES, TS), lambda i: (i, 0, 0)),
            pl.BlockSpec(memory_space=pltpu.SMEM),
            pl.BlockSpec((TS, D), lambda i: (i, 0)),
        ],
        out_shape=[
            jax.ShapeDtypeStruct((n_tiles, V7X_SUBLANES, TS), F32),
            jax.ShapeDtypeStruct((n_tiles, N_EXP), jnp.int32),
            jax.ShapeDtypeStruct((n_tiles * TS, D), BF16),
        ],
        compiler_params=pltpu.CompilerParams(
            dimension_semantics=("arbitrary",), vmem_limit_bytes=VMEM_LIMIT),
        name="moe_rank",
    )(x, mod_l, wrt, brt)


def _dispatch_kernel(off_ref, loc_ref, cnt_ref, fill_ref, rinfo_ref, u_ref, xg_ref, info_ref,
                     sorted_scr, zero_scr, sem, zsem):
    i = pl.program_id(0)
    n_steps = pl.num_programs(0)
    slot = i % 2
    rinfo = rinfo_ref[0]
    i1 = rinfo[0:1]
    i2 = rinfo[1:2]
    loc1 = jnp.zeros_like(i1)
    loc2 = jnp.zeros_like(i2)
    for e in range(N_EXP):
        local = loc_ref[i * N_EXP + e].astype(F32)
        loc1 = jnp.where(i1 == float(e), local, loc1)
        loc2 = jnp.where(i2 == float(e), local, loc2)
    row1 = loc1 + rinfo[2:3]
    row2 = loc2 + rinfo[3:4]
    info_ref[0] = jnp.concatenate([row1, row2, rinfo[4:6], jnp.zeros((V7X_SUBLANES - 4, TS), F32)], axis=0)
    p_idx = lax.broadcasted_iota(jnp.int32, (SORT_ROWS, TS), 0).astype(F32)
    sel = jnp.where(jnp.logical_or(p_idx == row1, p_idx == row2), 1.0, 0.0).astype(BF16)
    sorted_scr[slot] = jnp.dot(sel, u_ref[...], preferred_element_type=F32)

    def segment_copies(tile, s, do):
        def make_copy(local, glob, size):
            return pltpu.make_async_copy(sorted_scr.at[s, pl.ds(local, size)], xg_ref.at[pl.ds(glob, size)],
                                         sem.at[s])
        _segment_copies(off_ref, loc_ref, cnt_ref, tile, make_copy, do)

    @pl.when(i > 0)
    def _():
        segment_copies(i - 1, 1 - slot, lambda cp: cp.wait())

    segment_copies(i, slot, lambda cp: cp.start())

    @pl.when(i == n_steps - 1)
    def _():
        segment_copies(i, slot, lambda cp: cp.wait())
        zero_scr[...] = jnp.zeros_like(zero_scr)
        for e in range(N_EXP):
            end = fill_ref[e]
            n_tail = fill_ref[N_EXP + e] // SEG_PAD

            def tail_copy(k):
                return pltpu.make_async_copy(
                    zero_scr.at[pl.ds(0, SEG_PAD)],
                    xg_ref.at[pl.ds(pl.multiple_of(end + k * SEG_PAD, SEG_PAD), SEG_PAD)], zsem)

            def tail_start(k, carry):
                tail_copy(k).start()
                return carry

            def tail_wait(k, carry):
                tail_copy(k).wait()
                return carry

            lax.fori_loop(0, n_tail, tail_start, 0)
            lax.fori_loop(0, n_tail, tail_wait, 0)

        def tile_copy(k):
            return pltpu.make_async_copy(zero_scr, xg_ref.at[pl.ds(pl.multiple_of(k * TR, TR), TR)], zsem)

        def tile_start(k, carry):
            tile_copy(k).start()
            return carry

        def tile_wait(k, carry):
            tile_copy(k).wait()
            return carry

        lax.fori_loop(fill_ref[2 * N_EXP], MAX_RT, tile_start, 0)
        lax.fori_loop(fill_ref[2 * N_EXP], MAX_RT, tile_wait, 0)


def _dispatch(u2b, rinfo, off, loc, cnt, fill):
    n_tiles = rinfo.shape[0]
    tile3 = lambda i, *_: (i, 0, 0)
    return pl.pallas_call(
        _dispatch_kernel,
        grid_spec=pltpu.PrefetchScalarGridSpec(
            num_scalar_prefetch=4,
            grid=(n_tiles,),
            in_specs=[
                pl.BlockSpec((1, V7X_SUBLANES, TS), tile3),
                pl.BlockSpec((TS, D), lambda i, *_: (i, 0)),
            ],
            out_specs=[
                pl.BlockSpec(memory_space=pl.ANY),
                pl.BlockSpec((1, V7X_SUBLANES, TS), tile3),
            ],
            scratch_shapes=[
                pltpu.VMEM((2, SORT_ROWS, D), F32),
                pltpu.VMEM((TR, D), F32),
                pltpu.SemaphoreType.DMA((2,)),
                pltpu.SemaphoreType.DMA,
            ],
        ),
        out_shape=[
            jax.ShapeDtypeStruct((MAX_RT * TR, D), F32),
            jax.ShapeDtypeStruct((n_tiles, V7X_SUBLANES, TS), F32),
        ],
        compiler_params=pltpu.CompilerParams(
            dimension_semantics=("arbitrary",), vmem_limit_bytes=VMEM_LIMIT),
        name="moe_dispatch",
    )(off, loc, cnt, fill, rinfo, u2b)


def _expert_kernel(second, exp_ref, new_ref, nact_ref, *refs):
    if second:
        xg_ref, w1_ref, w3_ref, w2_ref, yp_ref, o_ref, w13b, w2b = refs
    else:
        xg_ref, w1_ref, w3_ref, w2_ref, o_ref, w13b, w2b = refs
    s = pl.program_id(0)
    r = s - 1
    active = jnp.logical_and(s >= 1, r < nact_ref[0])

    @pl.when(jnp.logical_and(active, new_ref[jnp.maximum(r, 0)] == 1))
    def _():
        w2b[...] = w2_ref[0].astype(BF16)

    @pl.when(active)
    def _():
        xb = xg_ref[...].astype(BF16)
        h = jnp.dot(xb, w13b[...], preferred_element_type=F32)
        h1 = h[:, 0:FC]
        hid = (h1 * _sigmoid(h1) * h[:, FC:2 * FC]).astype(BF16)
        p = jnp.dot(hid, w2b[...], preferred_element_type=F32)
        if second:
            p = yp_ref[...] + p
        o_ref[...] = p

    @pl.when(jnp.logical_and(s >= 1, jnp.logical_not(active)))
    def _():
        o_ref[...] = jnp.zeros_like(o_ref)

    @pl.when(new_ref[s] == 1)
    def _():
        w13b[:, 0:FC] = w1_ref[0].astype(BF16)
        w13b[:, FC:2 * FC] = w3_ref[0].astype(BF16)


def _expert_pass(second, tables, xg, w1, w3, w2, yp=None):
    f = 1 if second else 0
    tile = lambda s: jnp.maximum(s - 1, 0)
    ahead = lambda s: jnp.minimum(s, MAX_RT - 1)
    row = lambda s, ex, nw, na: (tile(s), 0)
    row_in = lambda s, ex, nw, na: (jnp.clip(tile(s), 0, jnp.maximum(na[0] - 1, 0)), 0)
    in_specs = [
        pl.BlockSpec((TR, D), row_in),
        pl.BlockSpec((1, D, FC), lambda s, ex, nw, na: (ex[ahead(s)], 0, f)),
        pl.BlockSpec((1, D, FC), lambda s, ex, nw, na: (ex[ahead(s)], 0, f)),
        pl.BlockSpec((1, FC, D), lambda s, ex, nw, na: (ex[tile(s)], f, 0)),
    ]
    args = [xg, w1, w3, w2]
    if second:
        in_specs.append(pl.BlockSpec((TR, D), row_in))
        args.append(yp)
    return pl.pallas_call(
        functools.partial(_expert_kernel, second),
        grid_spec=pltpu.PrefetchScalarGridSpec(
            num_scalar_prefetch=3,
            grid=(MAX_RT + 1,),
            in_specs=in_specs,
            out_specs=pl.BlockSpec((TR, D), row),
            scratch_shapes=[pltpu.VMEM((D, 2 * FC), BF16), pltpu.VMEM((FC, D), BF16)],
        ),
        out_shape=jax.ShapeDtypeStruct((MAX_RT * TR, D), F32),
        compiler_params=pltpu.CompilerParams(
            dimension_semantics=("arbitrary",), vmem_limit_bytes=EXPERT_VMEM_LIMIT),
        name="moe_expert_hi" if second else "moe_expert_lo",
    )(*tables, *args)


def _combine_kernel(n_ctx_tok, off_ref, loc_ref, cnt_ref, x_ref, mod_ref, info_ref, y_ref, g_ref, b_ref,
                    oc_ref, od_ref, ys_scr, sem):
    i = pl.program_id(0)
    n_steps = pl.num_programs(0)
    slot = i % 2

    def fetch(tile, s, do):
        def make_copy(local, glob, size):
            return pltpu.make_async_copy(y_ref.at[pl.ds(glob, size)], ys_scr.at[s, pl.ds(local, size)],
                                         sem.at[s])
        _segment_copies(off_ref, loc_ref, cnt_ref, tile, make_copy, do)

    def start_fetch(tile, s):
        ys_scr[s, 2 * TS:SORT_ROWS, :] = jnp.zeros((SORT_ROWS - 2 * TS, D), F32)
        fetch(tile, s, lambda cp: cp.start())

    @pl.when(i == 0)
    def _():
        start_fetch(0, 0)

    @pl.when(i + 1 < n_steps)
    def _():
        start_fetch(i + 1, 1 - slot)

    info = info_ref[0]
    row1, row2, w1, w2 = info[0:1], info[1:2], info[2:3], info[3:4]
    p_idx = lax.broadcasted_iota(jnp.int32, (SORT_ROWS, TS), 0).astype(F32)
    w_rows = jnp.sum(jnp.where(p_idx == row1, w1, 0.0) + jnp.where(p_idx == row2, w2, 0.0),
                     axis=1, keepdims=True)
    padded = jnp.concatenate([info, jnp.zeros((V7X_LANES - V7X_SUBLANES, TS), F32)], axis=0)
    cols = jnp.transpose(padded, (1, 0))
    q_idx = lax.broadcasted_iota(jnp.int32, (TS, SORT_ROWS), 1).astype(F32)
    pick = jnp.where(jnp.logical_or(q_idx == cols[:, 0:1], q_idx == cols[:, 1:2]), 1.0, 0.0).astype(BF16)
    fetch(i, slot, lambda cp: cp.wait())
    scaled = (ys_scr[slot] * w_rows).astype(BF16)
    acc = jnp.dot(pick, scaled, preferred_element_type=F32)
    res = _ffn_epilogue(n_ctx_tok, x_ref, mod_ref, acc, g_ref, b_ref)
    is_dec = i * TS >= n_ctx_tok

    @pl.when(jnp.logical_not(is_dec))
    def _():
        oc_ref[...] = res

    @pl.when(is_dec)
    def _():
        od_ref[...] = res


def _combine(x, mod_l, info, y, segs, ln_g, ln_b, n_ctx_tok):
    n_tok = x.shape[0]
    n_tiles = n_tok // TS
    ncb = n_ctx_tok // TS
    const = lambda i, *_: (0, 0)
    return pl.pallas_call(
        functools.partial(_combine_kernel, n_ctx_tok),
        grid_spec=pltpu.PrefetchScalarGridSpec(
            num_scalar_prefetch=3,
            grid=(n_tiles,),
            in_specs=[
                pl.BlockSpec((TS, D), lambda i, *_: (i, 0)),
                pl.BlockSpec((MOD_ROWS, N_MOD * D), const),
                pl.BlockSpec((1, V7X_SUBLANES, TS), lambda i, *_: (i, 0, 0)),
                pl.BlockSpec(memory_space=pl.ANY),
                pl.BlockSpec((1, D), const),
                pl.BlockSpec((1, D), const),
            ],
            out_specs=[
                pl.BlockSpec((TS, D), lambda i, *_: (jnp.minimum(i, ncb - 1), 0)),
                pl.BlockSpec((TS, D), lambda i, *_: (jnp.maximum(i - ncb, 0), 0)),
            ],
            scratch_shapes=[
                pltpu.VMEM((2, SORT_ROWS, D), F32),
                pltpu.SemaphoreType.DMA((2,)),
            ],
        ),
        out_shape=[
            jax.ShapeDtypeStruct((n_ctx_tok, D), F32),
            jax.ShapeDtypeStruct((n_tok - n_ctx_tok, D), F32),
        ],
        compiler_params=pltpu.CompilerParams(
            dimension_semantics=("arbitrary",), vmem_limit_bytes=VMEM_LIMIT),
        name="moe_combine",
    )(*segs, x, mod_l, info, y, ln_g, ln_b)


def _routing_tables(counts):
    counts = (counts + SEG_PAD - 1) // SEG_PAD * SEG_PAD
    totals = jnp.sum(counts, axis=0)
    n_rt = (totals + TR - 1) // TR
    cum = jnp.cumsum(n_rt)
    first = cum - n_rt
    n_act = cum[-1]
    off = first[None, :] * TR + (jnp.cumsum(counts, axis=0) - counts)
    loc = jnp.cumsum(counts, axis=1) - counts
    fill = jnp.concatenate([first * TR + totals, n_rt * TR - totals, n_act[None]])
    r = jnp.arange(MAX_RT, dtype=jnp.int32)
    rc = jnp.minimum(r, n_act - 1)
    exp = jnp.sum((rc[:, None] >= cum[None, :]).astype(jnp.int32), axis=1)
    new = jnp.concatenate([jnp.logical_and(r == first[exp], r < n_act), jnp.zeros((1,), bool)])
    i32 = lambda a: a.astype(jnp.int32)
    segs = (i32(off.reshape(-1)), i32(loc.reshape(-1)), i32(counts.reshape(-1)))
    return segs, i32(fill), (i32(exp), i32(new), i32(n_act.reshape(1)))


def _moe_ffn(x, mod_l, router_w, router_b, w1, w3, w2, ln_g, ln_b, n_ctx_tok):
    n_tok = x.shape[0]
    n_seg_pad = (n_tok // TS) * N_EXP * (SEG_PAD - 1)
    assert (2 * n_tok + n_seg_pad + TR - 1) // TR + N_EXP == MAX_RT and N_FC == 2
    wrt = router_w.T
    brt = jnp.broadcast_to(router_b.reshape(N_EXP, 1), (N_EXP, V7X_LANES))
    rinfo, counts, u2b = _rank(x, mod_l, wrt, brt, n_ctx_tok)
    segs, fill, tables = _routing_tables(counts)
    xg, info = _dispatch(u2b, rinfo, *segs, fill)
    y_lo = _expert_pass(False, tables, xg, w1, w3, w2)
    y = _expert_pass(True, tables, xg, w1, w3, w2, y_lo)
    return _combine(x, mod_l, info, y, segs, ln_g, ln_b, n_ctx_tok)


def _cast_kernel(*refs):
    o_ref = refs[-1]
    off = 0
    for x_ref in refs[:-1]:
        width = x_ref.shape[-1]
        o_ref[..., off:off + width] = x_ref[...].reshape(o_ref.shape[:-1] + (width,)).astype(o_ref.dtype)
        off += width


def _pack_w13(w1, w3):
    return pl.pallas_call(
        _cast_kernel,
        grid=(N_FC,),
        in_specs=[pl.BlockSpec((D, FC), lambda f: (0, f)), pl.BlockSpec((D, FC), lambda f: (0, f))],
        out_specs=pl.BlockSpec((1, D, 2 * FC), lambda f: (f, 0, 0)),
        out_shape=jax.ShapeDtypeStruct((N_FC, D, 2 * FC), BF16),
        compiler_params=pltpu.CompilerParams(vmem_limit_bytes=VMEM_LIMIT),
        name="pack_w13",
    )(w1, w3)


def _pack_w_in(w_in):
    depth = w_in.shape[0]
    def regroup_kernel(x_ref, o_ref):
        for j in range(N_CH):
            o_ref[0, j] = x_ref[0, :, j * CW:(j + 1) * CW].astype(BF16)

    w5 = pl.pallas_call(
        regroup_kernel,
        grid=(depth, 5),
        in_specs=[pl.BlockSpec((1, D, D), lambda l, g: (l, 0, g))],
        out_specs=pl.BlockSpec((1, N_CH, D, CW), lambda l, g: (l, 0, 0, g)),
        out_shape=jax.ShapeDtypeStruct((depth, N_CH, D, 5 * CW), BF16),
        compiler_params=pltpu.CompilerParams(vmem_limit_bytes=VMEM_LIMIT),
        name="pack_w5",
    )(w_in)
    wg = pl.pallas_call(
        _cast_kernel,
        grid=(depth, 2),
        in_specs=[pl.BlockSpec((1, D, D), lambda l, k: (l, 0, 5 + k))],
        out_specs=pl.BlockSpec((1, D, D), lambda l, k: (l, 0, k)),
        out_shape=jax.ShapeDtypeStruct((depth, D, 2 * D), BF16),
        compiler_params=pltpu.CompilerParams(vmem_limit_bytes=VMEM_LIMIT),
        name="pack_wg",
    )(w_in)
    return w5, wg


def _block_diag_chunks(w):
    tiled = jnp.tile(w.reshape(N_CH, CW, HEAD_D), (1, 1, CW // HEAD_D))
    blk = jnp.arange(CW) // HEAD_D
    return jnp.where(blk[:, None] == blk[None, :], tiled, 0.0)


def kernel(x_prompt, x_sample, state_rglru, c, c_ctx, w_mod, b_mod, w_in, conv_a, w_a_out, conv_b, conv_b_bias, w_gate_a, b_gate_a, w_gate_x, b_gate_x, lru_lambda, w_b_out, w_o, ln1_g, ln1_b, ln2_g, ln2_b, ffn_w1, ffn_w3, ffn_w2, router_w, router_b, moe_w1, moe_w3, moe_w2):
    batch, seq, d = x_prompt.shape
    dec_batch, dec_seq, _ = x_sample.shape
    depth = w_mod.shape[0]
    assert (d, seq, dec_seq, depth) == (D, SEQ, DEC_SEQ, DEPTH)
    n_ctx_tok = batch * seq
    n_dec_tok = dec_batch * dec_seq
    assert n_ctx_tok % TM1 == 0 and TM1 == dec_seq and 1 + dec_batch <= MOD_ROWS
    n_ctx_tiles = n_ctx_tok // TM1
    n_tiles = n_ctx_tiles + dec_batch
    seq_per_tile = TM1 // seq
    assert N_SEG == 2 * seq_per_tile

    assert depth % 2 == 0
    n_tok = n_ctx_tok + n_dec_tok
    x = (x_prompt.reshape(n_ctx_tok, D), x_sample.reshape(n_dec_tok, D))

    cond = jnp.zeros((MOD_ROWS, D), F32).at[0].set(c_ctx).at[1:1 + dec_batch].set(c)
    mod = _modulation(cond, w_mod, b_mod)

    w5, wg = _pack_w_in(w_in)
    states = []
    u_pre = None
    for l in range(depth):
        wbd = jnp.concatenate(
            [_block_diag_chunks(w_gate_a[l, 0]), _block_diag_chunks(w_gate_x[l, 0]),
             _block_diag_chunks(w_gate_a[l, 1]), _block_diag_chunks(w_gate_x[l, 1])],
            axis=-1).astype(BF16)
        gbias = jnp.stack([b_gate_a[l, 0], b_gate_x[l, 0], b_gate_a[l, 1], b_gate_x[l, 1]], axis=0)
        h0 = jnp.zeros((n_tiles, 2, N_SEG, D), F32)
        h0 = h0.at[n_ctx_tiles:, 0, 0].set(state_rglru[:, l, 0].astype(F32))
        h0 = h0.at[n_ctx_tiles:, 1, N_SEG - 1].set(state_rglru[:, l, 1].astype(F32))

        a_pre, b_pre, st = _mixer_part1(
            x, u_pre, mod[l], l, w5, conv_a[l], conv_b[l], conv_b_bias[l].reshape(1, D), wbd, gbias,
            lru_lambda[l], h0, n_tok, n_ctx_tok)
        x = _mixer_part2(
            x, u_pre, mod[l], l, a_pre, b_pre, wg, w_a_out[l].astype(BF16), w_b_out[l].astype(BF16),
            w_o[l].astype(BF16), ln1_g[l].reshape(1, D), ln1_b[l].reshape(1, D), n_ctx_tok)
        u_pre = None

        if l % 2 == 0:
            k = l // 2
            x, u_pre = _dense_ffn(x, mod[l], mod[l + 1], _pack_w13(ffn_w1[k], ffn_w3[k]),
                                  ffn_w2[k].astype(BF16), ln2_g[l].reshape(1, D), ln2_b[l].reshape(1, D),
                                  n_ctx_tok)
        else:
            k = l // 2
            x = _moe_ffn(x, mod[l], router_w[k], router_b[k], moe_w1[k], moe_w3[k], moe_w2[k],
                         ln2_g[l].reshape(1, D), ln2_b[l].reshape(1, D), n_ctx_tok)

        st_ctx = st[:n_ctx_tiles]
        fwd = st_ctx[:, 0, 1::2].reshape(batch, D)
        bwd = st_ctx[:, 1, 0::2].reshape(batch, D)
        states.append(jnp.stack([fwd, bwd], axis=1))

    y_prompt = x[0].reshape(batch, seq, D)
    y_sample = x[1].reshape(dec_batch, dec_seq, D)
    new_state = jnp.stack(states, axis=1).astype(x_prompt.dtype)
    return (y_prompt, y_sample, new_state)
```

```python
import functools

import jax
import jax.numpy as jnp
from jax import lax
from jax.experimental import pallas as pl
from jax.experimental.pallas import tpu as pltpu

F32 = jnp.float32
BF16 = jnp.bfloat16

D = 1024
SEQ = 256
DEC_SEQ = 1024
GRID_W = 64
N_HEAD = 16
HEAD_D = D // N_HEAD
RGLRU_C = 8.0
LOG2_E = 1.4426950408889634
D_FF = 2816
N_EXP = 8
N_MOD = 6
DEPTH = 2
DN_ALPHA = (2.0 * DEPTH) ** 0.25

V7X_SUBLANES = 8
V7X_LANES = 128
V7X_VMEM_BYTES = 64 * 1024 * 1024
VMEM_LIMIT = V7X_VMEM_BYTES - 12 * 1024 * 1024

TM1 = 1024
CW = 256
N_CH = D // CW
N_LC = CW // V7X_LANES
N_SEG = V7X_SUBLANES
SEG = TM1 // N_SEG
SEG_STRIDE = SEG + 4
CONV_GAP = V7X_SUBLANES
TM2 = 512
TM3 = 512
FC = 1408
N_FC = D_FF // FC
TS = 512
TR = 256
SEG_PAD = V7X_SUBLANES
SEG_BITS = 7
SORT_ROWS = 2 * TS + V7X_LANES
MAX_RT = (2 * 10240 + (10240 // TS) * N_EXP * (SEG_PAD - 1) + TR - 1) // TR + N_EXP
EXPERT_VMEM_LIMIT = V7X_VMEM_BYTES - 6 * 1024 * 1024
MOD_ROWS = 8
MOD_BLK = 3072


def _sigmoid(x):
    return 0.5 * jnp.tanh(0.5 * x) + 0.5


def _ln_plain(x, eps):
    mu = jnp.mean(x, axis=-1, keepdims=True)
    xc = x - mu
    var = jnp.mean(xc * xc, axis=-1, keepdims=True)
    return xc * lax.rsqrt(var + eps)


def _token_source(xs, tile, n_ctx_tok):
    ncb = n_ctx_tok // tile
    if isinstance(xs, tuple):
        x_ctx, x_dec = xs
        dec_off = 0
    else:
        x_ctx = x_dec = xs
        dec_off = ncb
    ctx_map = lambda i: (jnp.minimum(i, ncb - 1), 0)
    dec_map = lambda i: (jnp.maximum(i - ncb, 0) + dec_off, 0)
    return x_ctx, x_dec, ctx_map, dec_map


def _mod_row(tok0, n_ctx_tok):
    dec = jnp.maximum(tok0 - n_ctx_tok, 0) // DEC_SEQ
    return jnp.where(tok0 >= n_ctx_tok, 1 + dec, 0)


def _mod_kernel(cond_ref, w_ref, b_ref, o_ref):
    cnd = cond_ref[...]
    s = cnd * _sigmoid(cnd)
    o_ref[0] = jnp.dot(s, w_ref[0], preferred_element_type=F32,
                       precision=lax.Precision.HIGHEST) + b_ref[0]


def _modulation(cond, w_mod, b_mod):
    depth = w_mod.shape[0]
    n_out = w_mod.shape[2]
    return pl.pallas_call(
        _mod_kernel,
        grid=(depth, n_out // MOD_BLK),
        in_specs=[
            pl.BlockSpec((MOD_ROWS, D), lambda l, j: (0, 0)),
            pl.BlockSpec((1, D, MOD_BLK), lambda l, j: (l, 0, j)),
            pl.BlockSpec((1, 1, MOD_BLK), lambda l, j: (l, 0, j)),
        ],
        out_specs=pl.BlockSpec((1, MOD_ROWS, MOD_BLK), lambda l, j: (l, 0, j)),
        out_shape=jax.ShapeDtypeStruct((depth, MOD_ROWS, n_out), F32),
        compiler_params=pltpu.CompilerParams(
            dimension_semantics=("arbitrary", "arbitrary"), vmem_limit_bytes=VMEM_LIMIT),
        name="modulation",
    )(cond, w_mod, b_mod.reshape(depth, 1, n_out))


def _scan_dir(a_scr, b_scr, hl_scr, ac_scr, h0, keep, reverse):
    n_lc = a_scr.shape[0]

    def body(k, carry):
        kk = SEG - 1 - k if reverse else k
        idx = pl.ds(kk, N_SEG, stride=SEG_STRIDE)
        new = []
        for c in range(n_lc):
            h, acc = carry[c]
            a_k = a_scr[c, idx, :]
            h = a_k * h + b_scr[c, idx, :]
            acc = a_k * acc
            hl_scr[c, idx, :] = h
            ac_scr[c, idx, :] = acc
            new.append((h, acc))
        return tuple(new)

    init = tuple((jnp.zeros((N_SEG, V7X_LANES), F32), jnp.ones((N_SEG, V7X_LANES), F32))
                 for _ in range(n_lc))
    fin = lax.fori_loop(0, SEG, body, init, unroll=8)
    h_loc = jnp.concatenate([fin[c][0] for c in range(n_lc)], axis=1)
    a_tot = jnp.concatenate([fin[c][1] for c in range(n_lc)], axis=1)

    order = range(N_SEG - 1, -1, -1) if reverse else range(N_SEG)
    h_in = [None] * N_SEG
    prev = None
    for s in order:
        cur = h0[s:s + 1]
        if prev is not None:
            left = a_tot[prev:prev + 1] * h_in[prev] + h_loc[prev:prev + 1]
            cur = keep[s] * left + cur
        h_in[s] = cur
        prev = s
    h_in = jnp.concatenate(h_in, axis=0)
    return h_in, a_tot * h_in + h_loc


def _conv_stage(is_dec, proj_scr, ca_ref, cb_ref, cbias_ref, apre_ref, xr_scr, gap_scr):
    win = GRID_W
    n_win = TM1 // win
    stride = win + CONV_GAP
    ca = ca_ref[...]
    cb = cb_ref[...]
    bias = cbias_ref[...]
    pieces = [(w, c) for w in range(n_win) for c in range(N_LC)]
    sub = lax.broadcasted_iota(jnp.int32, (CONV_GAP, V7X_LANES), 0)
    joined = jnp.where(is_dec, jnp.float32(0.0), jnp.float32(1.0))
    zero_gap = jnp.zeros((CONV_GAP, V7X_LANES), F32)

    def stage(value_of):
        for c in range(N_LC):
            for w in range(n_win + 1):
                edge = w * win
                if edge % SEQ == 0:
                    gap = zero_gap
                else:
                    before = value_of(slice(edge - CONV_GAP, edge), c)
                    after = value_of(slice(edge, edge + CONV_GAP), c)
                    gap = joined * jnp.where(sub >= CONV_GAP - 2, before, jnp.where(sub == 0, after, 0.0))
                gap_scr[c, w * stride:w * stride + CONV_GAP, :] = gap
        for w, c in pieces:
            lo = CONV_GAP + w * stride
            gap_scr[c, lo:lo + win, :] = value_of(slice(w * win, (w + 1) * win), c)

    def tap(w, c, shift):
        lo = CONV_GAP + w * stride + shift
        return gap_scr[c, lo:lo + win, :]

    def chunk_cols(k, c):
        return slice(k * CW + c * V7X_LANES, k * CW + (c + 1) * V7X_LANES)

    stage(lambda rows, c: proj_scr[rows, chunk_cols(2, c)] * proj_scr[rows, chunk_cols(0, c)])
    for w, c in pieces:
        rows = slice(w * win, (w + 1) * win)
        lanes = slice(c * V7X_LANES, (c + 1) * V7X_LANES)
        conv = ca[0:1, lanes] * tap(w, c, -1) + ca[1:2, lanes] * tap(w, c, 0) + ca[2:3, lanes] * tap(w, c, 1)
        apre_ref[rows, lanes] = (proj_scr[rows, chunk_cols(1, c)] * conv).astype(BF16)
    stage(lambda rows, c: proj_scr[rows, chunk_cols(4, c)])
    for w, c in pieces:
        rows = slice(w * win, (w + 1) * win)
        lanes = slice(c * V7X_LANES, (c + 1) * V7X_LANES)
        xr_scr[rows, lanes] = (cb[0:1, lanes] * tap(w, c, -2) + cb[1:2, lanes] * tap(w, c, -1)
                               + cb[2:3, lanes] * tap(w, c, 0) + cb[3:4, lanes] * tap(w, c, 1)
                               + bias[:, lanes])


def _mix1_kernel(n_ctx_tiles, pre_normed, *refs):
    if pre_normed:
        u_ref, refs = refs[0], refs[1:]
    else:
        xc_ref, xd_ref, mod_ref, refs = refs[0], refs[1], refs[2], refs[3:]
    (w5_ref, ca_ref, cb_ref, cbias_ref, wbd_ref, gb_ref, lam_ref, h0_ref, apre_ref, bpre_ref, st_ref,
     u_scr, proj_scr, a_scr, b_scr, hl_scr, ac_scr, hsum_scr, xr_scr, gap_scr) = refs
    i = pl.program_id(0)
    j = pl.program_id(1)
    is_dec = i >= n_ctx_tiles

    if pre_normed:
        proj_scr[...] = jnp.dot(u_ref[...], w5_ref[0, 0], preferred_element_type=F32)
    else:
        @pl.when(j == 0)
        def _():
            row = jnp.where(is_dec, i - (n_ctx_tiles - 1), 0)
            m = mod_ref[pl.ds(row, 1), :]
            sh1 = m[:, 0:D]
            sc1 = m[:, D:2 * D]
            x = jnp.where(is_dec, xd_ref[...], xc_ref[...])
            u_scr[...] = (_ln_plain(x, 1e-6) * (1.0 + sc1) + sh1).astype(BF16)

        proj_scr[...] = jnp.dot(u_scr[...], w5_ref[0, 0], preferred_element_type=F32)

    _conv_stage(is_dec, proj_scr, ca_ref, cb_ref, cbias_ref, apre_ref, xr_scr, gap_scr)

    xr = xr_scr[...]
    gates = jnp.dot(xr.astype(BF16), wbd_ref[0], preferred_element_type=F32)
    gb = gb_ref[...]
    lam = lam_ref[...]
    sp = jnp.maximum(-lam, 0.0) + jnp.log1p(jnp.exp(-jnp.abs(lam)))
    rate = (-RGLRU_C * LOG2_E) * sp

    one = jnp.float32(1.0)
    for d in range(2):
        ga = gates[:, (2 * d) * CW:(2 * d + 1) * CW] + gb[2 * d:2 * d + 1]
        gx = gates[:, (2 * d + 1) * CW:(2 * d + 2) * CW] + gb[2 * d + 1:2 * d + 2]
        r = _sigmoid(ga)
        ig = _sigmoid(gx)
        a = jnp.exp2(r * rate[d:d + 1])
        y = 1.0 - a * a
        bt = jnp.where(y > 0.0, y * lax.rsqrt(y), 0.0) * (ig * xr)
        for s in range(N_SEG):
            lo = s * SEG_STRIDE
            for c in range(N_LC):
                lanes = slice(c * V7X_LANES, (c + 1) * V7X_LANES)
                a_scr[c, lo:lo + SEG, :] = a[s * SEG:(s + 1) * SEG, lanes]
                b_scr[c, lo:lo + SEG, :] = bt[s * SEG:(s + 1) * SEG, lanes]
        if d == 0:
            keep = [jnp.where(is_dec, one, jnp.float32(s % 2 == 1)) for s in range(N_SEG)]
        else:
            keep = [jnp.where(is_dec, one, jnp.float32(s % 2 == 0)) for s in range(N_SEG)]
        h_in, h_out = _scan_dir(a_scr, b_scr, hl_scr, ac_scr, h0_ref[0, d], keep, reverse=(d == 1))
        st_ref[0, d] = h_out
        for s in range(N_SEG):
            lo = s * SEG_STRIDE
            for c in range(N_LC):
                lanes = slice(c * V7X_LANES, (c + 1) * V7X_LANES)
                h_seg = hl_scr[c, lo:lo + SEG, :] + ac_scr[c, lo:lo + SEG, :] * h_in[s:s + 1, lanes]
                if d == 0:
                    hsum_scr[s * SEG:(s + 1) * SEG, lanes] = h_seg
                else:
                    hsum_scr[s * SEG:(s + 1) * SEG, lanes] += h_seg

    bpre_ref[...] = (hsum_scr[...] * jax.nn.gelu(proj_scr[:, 3 * CW:4 * CW])).astype(BF16)


def _mixer_part1(xs, u_pre, mod_l, layer, w5, conv_a, conv_b, conv_b_bias, wbd, gbias, lam, h0, n_tok,
                 n_ctx_tok):
    n_tiles = n_tok // TM1
    if u_pre is None:
        x_ctx, x_dec, ctx_map, dec_map = _token_source(xs, TM1, n_ctx_tok)
        lead_args = [x_ctx, x_dec, mod_l]
        lead_specs = [
            pl.BlockSpec((TM1, D), lambda i, j: ctx_map(i)),
            pl.BlockSpec((TM1, D), lambda i, j: dec_map(i)),
            pl.BlockSpec((MOD_ROWS, N_MOD * D), lambda i, j: (0, 0)),
        ]
    else:
        lead_args = [u_pre]
        lead_specs = [pl.BlockSpec((TM1, D), lambda i, j: (i, 0))]
    kern = functools.partial(_mix1_kernel, n_ctx_tok // TM1, u_pre is not None)
    return pl.pallas_call(
        kern,
        grid=(n_tiles, N_CH),
        in_specs=lead_specs + [
            pl.BlockSpec((1, 1, D, 5 * CW), lambda i, j: (layer, j, 0, 0)),
            pl.BlockSpec((3, CW), lambda i, j: (0, j)),
            pl.BlockSpec((4, CW), lambda i, j: (0, j)),
            pl.BlockSpec((1, CW), lambda i, j: (0, j)),
            pl.BlockSpec((1, CW, 4 * CW), lambda i, j: (j, 0, 0)),
            pl.BlockSpec((4, CW), lambda i, j: (0, j)),
            pl.BlockSpec((2, CW), lambda i, j: (0, j)),
            pl.BlockSpec((1, 2, N_SEG, CW), lambda i, j: (i, 0, 0, j)),
        ],
        out_specs=[
            pl.BlockSpec((TM1, CW), lambda i, j: (i, j)),
            pl.BlockSpec((TM1, CW), lambda i, j: (i, j)),
            pl.BlockSpec((1, 2, N_SEG, CW), lambda i, j: (i, 0, 0, j)),
        ],
        out_shape=[
            jax.ShapeDtypeStruct((n_tok, D), BF16),
            jax.ShapeDtypeStruct((n_tok, D), BF16),
            jax.ShapeDtypeStruct((n_tiles, 2, N_SEG, D), F32),
        ],
        scratch_shapes=[
            pltpu.VMEM((TM1, D), BF16),
            pltpu.VMEM((TM1, 5 * CW), F32),
            pltpu.VMEM((N_LC, N_SEG * SEG_STRIDE, V7X_LANES), F32),
            pltpu.VMEM((N_LC, N_SEG * SEG_STRIDE, V7X_LANES), F32),
            pltpu.VMEM((N_LC, N_SEG * SEG_STRIDE, V7X_LANES), F32),
            pltpu.VMEM((N_LC, N_SEG * SEG_STRIDE, V7X_LANES), F32),
            pltpu.VMEM((TM1, CW), F32),
            pltpu.VMEM((TM1, CW), F32),
            pltpu.VMEM((N_LC, CONV_GAP + (TM1 // GRID_W) * (GRID_W + CONV_GAP), V7X_LANES), F32),
        ],
        compiler_params=pltpu.CompilerParams(
            dimension_semantics=("arbitrary", "arbitrary"), vmem_limit_bytes=VMEM_LIMIT),
        name="mixer_scan",
    )(*lead_args, w5, conv_a, conv_b, conv_b_bias, wbd, gbias, lam, h0)


def _mix2_kernel(n_ctx_tok, pre_normed, *refs):
    if pre_normed:
        u_ref, refs = refs[0], refs[1:]
    xc_ref, xd_ref, mod_ref, ap_ref, bp_ref, wg_ref, wa_ref, wb_ref, wo_ref, g_ref, b_ref, o_ref = refs
    i = pl.program_id(0)
    m = mod_ref[pl.ds(_mod_row(i * TM2, n_ctx_tok), 1), :]
    sh1 = m[:, 0:D]
    sc1 = m[:, D:2 * D]
    g1 = m[:, 2 * D:3 * D]
    x = jnp.where(i * TM2 >= n_ctx_tok, xd_ref[...], xc_ref[...])
    if pre_normed:
        u = u_ref[...]
    else:
        u = (_ln_plain(x, 1e-6) * (1.0 + sc1) + sh1).astype(BF16)
    gates = jnp.dot(u, wg_ref[0], preferred_element_type=F32)
    br_a = jnp.dot(ap_ref[...], wa_ref[...], preferred_element_type=F32)
    br_b = jnp.dot(bp_ref[...], wb_ref[...], preferred_element_type=F32)
    merged = _sigmoid(gates[:, 0:D]) * br_a + _sigmoid(gates[:, D:2 * D]) * br_b
    mix = jnp.dot(merged.astype(BF16), wo_ref[...], preferred_element_type=F32)
    y = DN_ALPHA * x + g1 * mix
    o_ref[...] = _ln_plain(y, 1e-5) * g_ref[...] + b_ref[...]


def _mixer_part2(xs, u_pre, mod_l, layer, a_pre, b_pre, wg, wa, wb, wo, ln_g, ln_b, n_ctx_tok):
    n_tok = a_pre.shape[0]
    x_ctx, x_dec, ctx_map, dec_map = _token_source(xs, TM2, n_ctx_tok)
    const = lambda i: (0, 0)
    tile = lambda i: (i, 0)
    lead_args = [] if u_pre is None else [u_pre]
    lead_specs = [] if u_pre is None else [pl.BlockSpec((TM2, D), tile)]
    return pl.pallas_call(
        functools.partial(_mix2_kernel, n_ctx_tok, u_pre is not None),
        grid=(n_tok // TM2,),
        in_specs=lead_specs + [
            pl.BlockSpec((TM2, D), ctx_map),
            pl.BlockSpec((TM2, D), dec_map),
            pl.BlockSpec((MOD_ROWS, N_MOD * D), const),
            pl.BlockSpec((TM2, D), tile),
            pl.BlockSpec((TM2, D), tile),
            pl.BlockSpec((1, D, 2 * D), lambda i: (layer, 0, 0)),
            pl.BlockSpec((D, D), const),
            pl.BlockSpec((D, D), const),
            pl.BlockSpec((D, D), const),
            pl.BlockSpec((1, D), const),
            pl.BlockSpec((1, D), const),
        ],
        out_specs=pl.BlockSpec((TM2, D), tile),
        out_shape=jax.ShapeDtypeStruct((n_tok, D), F32),
        compiler_params=pltpu.CompilerParams(
            dimension_semantics=("arbitrary",), vmem_limit_bytes=VMEM_LIMIT),
        name="mixer_out",
    )(*lead_args, x_ctx, x_dec, mod_l, a_pre, b_pre, wg, wa, wb, wo, ln_g, ln_b)


def _ffn_prologue(n_ctx_tok, x_ref, mod_ref):
    m = mod_ref[pl.ds(_mod_row(pl.program_id(0) * TM3, n_ctx_tok), 1), :]
    sh2 = m[:, 3 * D:4 * D]
    sc2 = m[:, 4 * D:5 * D]
    return _ln_plain(x_ref[...], 1e-6) * (1.0 + sc2) + sh2


def _ffn_epilogue(n_ctx_tok, x_ref, mod_ref, acc, g_ref, b_ref):
    m = mod_ref[pl.ds(_mod_row(pl.program_id(0) * TM3, n_ctx_tok), 1), :]
    g2 = m[:, 5 * D:6 * D]
    y = DN_ALPHA * x_ref[...] + g2 * acc
    return _ln_plain(y, 1e-5) * g_ref[...] + b_ref[...]


def _swiglu_chunk(u, w13, w2):
    h = jnp.dot(u, w13, preferred_element_type=F32)
    h1 = h[:, 0:FC]
    hid = (h1 * _sigmoid(h1) * h[:, FC:2 * FC]).astype(BF16)
    return jnp.dot(hid, w2, preferred_element_type=F32)


def _dense_ffn_kernel(n_ctx_tok, x_ref, mod_ref, modn_ref, w13_ref, w2_ref, g_ref, b_ref, o_ref, un_ref,
                      u_scr, acc_scr):
    f = pl.program_id(1)

    @pl.when(f == 0)
    def _():
        u_scr[...] = _ffn_prologue(n_ctx_tok, x_ref, mod_ref).astype(BF16)

    p = _swiglu_chunk(u_scr[...], w13_ref[0], w2_ref[...])

    @pl.when(f == 0)
    def _():
        acc_scr[...] = p

    @pl.when(f > 0)
    def _():
        acc_scr[...] += p

    @pl.when(f == N_FC - 1)
    def _():
        res = _ffn_epilogue(n_ctx_tok, x_ref, mod_ref, acc_scr[...], g_ref, b_ref)
        o_ref[...] = res
        m = modn_ref[pl.ds(_mod_row(pl.program_id(0) * TM3, n_ctx_tok), 1), :]
        un_ref[...] = (_ln_plain(res, 1e-6) * (1.0 + m[:, D:2 * D]) + m[:, 0:D]).astype(BF16)


def _dense_ffn(x, mod_l, mod_next, w13, w2, ln_g, ln_b, n_ctx_tok):
    n_tok = x.shape[0]
    tile = pl.BlockSpec((TM3, D), lambda i, f: (i, 0))
    table = pl.BlockSpec((MOD_ROWS, N_MOD * D), lambda i, f: (0, 0))
    return pl.pallas_call(
        functools.partial(_dense_ffn_kernel, n_ctx_tok),
        grid=(n_tok // TM3, N_FC),
        in_specs=[
            tile,
            table,
            table,
            pl.BlockSpec((1, D, 2 * FC), lambda i, f: (f, 0, 0)),
            pl.BlockSpec((FC, D), lambda i, f: (f, 0)),
            pl.BlockSpec((1, D), lambda i, f: (0, 0)),
            pl.BlockSpec((1, D), lambda i, f: (0, 0)),
        ],
        out_specs=[tile, tile],
        out_shape=[jax.ShapeDtypeStruct((n_tok, D), F32), jax.ShapeDtypeStruct((n_tok, D), BF16)],
        scratch_shapes=[pltpu.VMEM((TM3, D), BF16), pltpu.VMEM((TM3, D), F32)],
        compiler_params=pltpu.CompilerParams(
            dimension_semantics=("arbitrary", "arbitrary"), vmem_limit_bytes=VMEM_LIMIT),
        name="dense_ffn",
    )(x, mod_l, mod_next, w13, w2, ln_g, ln_b)


def _segment_copies(off_ref, loc_ref, cnt_ref, tile, make_copy, do):
    for e in range(N_EXP):
        n = cnt_ref[tile * N_EXP + e]
        local = loc_ref[tile * N_EXP + e]
        glob = off_ref[tile * N_EXP + e]
        for k in range(SEG_BITS - 1, -1, -1):
            size = SEG_PAD << k
            take = (n & size) != 0

            @pl.when(take)
            def _():
                do(make_copy(pl.multiple_of(local, SEG_PAD), pl.multiple_of(glob, SEG_PAD), size))

            step = jnp.where(take, size, 0)
            local = local + step
            glob = glob + step


def _rank_kernel(n_ctx_tok, x_ref, mod_ref, wrt_ref, brt_ref, info_ref, cnt_ref, u2b_ref):
    i = pl.program_id(0)
    u2 = _ffn_prologue(n_ctx_tok, x_ref, mod_ref)
    u2b_ref[...] = u2.astype(BF16)
    lg = lax.dot_general(wrt_ref[...], u2, (((1,), (1,)), ((), ())), preferred_element_type=F32,
                         precision=lax.Precision.HIGHEST) + brt_ref[:, 0:1]
    eidx = lax.broadcasted_iota(jnp.int32, lg.shape, 0).astype(F32)
    neg = jnp.float32(-jnp.inf)
    v1 = jnp.max(lg, axis=0, keepdims=True)
    i1 = jnp.min(jnp.where(lg == v1, eidx, float(N_EXP)), axis=0, keepdims=True)
    lg2 = jnp.where(eidx == i1, neg, lg)
    v2 = jnp.max(lg2, axis=0, keepdims=True)
    i2 = jnp.min(jnp.where(lg2 == v2, eidx, float(N_EXP)), axis=0, keepdims=True)
    t = jnp.exp(v2 - v1)
    w_top = 1.0 / (1.0 + t)
    m1 = eidx == i1
    m2 = eidx == i2
    member = jnp.where(jnp.logical_or(m1, m2), 1.0, 0.0)
    before = (lax.broadcasted_iota(jnp.int32, (TS, TS), 0)
              < lax.broadcasted_iota(jnp.int32, (TS, TS), 1))
    rank = jnp.dot(member.astype(BF16), jnp.where(before, 1.0, 0.0).astype(BF16),
                   preferred_element_type=F32)
    for e in range(N_EXP):
        cnt_ref[i, e] = jnp.sum(member[e:e + 1, :]).astype(jnp.int32)
    rank1 = jnp.sum(jnp.where(m1, rank, 0.0), axis=0, keepdims=True)
    rank2 = jnp.sum(jnp.where(m2, rank, 0.0), axis=0, keepdims=True)
    info_ref[0] = jnp.concatenate(
        [i1, i2, rank1, rank2, w_top, t * w_top, jnp.zeros((V7X_SUBLANES - 6, TS), F32)], axis=0)


def _rank(x, mod_l, wrt, brt, n_ctx_tok):
    n_tiles = x.shape[0] // TS
    return pl.pallas_call(
        functools.partial(_rank_kernel, n_ctx_tok),
        grid=(n_tiles,),
        in_specs=[
            pl.BlockSpec((TS, D), lambda i: (i, 0)),
            pl.BlockSpec((MOD_ROWS, N_MOD * D), lambda i: (0, 0)),
            pl.BlockSpec((N_EXP, D), lambda i: (0, 0)),
            pl.BlockSpec((N_EXP, V7X_LANES), lambda i: (0, 0)),
        ],
        out_specs=[
            pl.BlockSpec((1, V7X_SUBLANES, TS), lambda i: (i, 0, 0)),
            pl.BlockSpec(memory_space=pltpu.SMEM),
            pl.BlockSpec((TS, D), lambda i: (i, 0)),
        ],
        out_shape=[
            jax.ShapeDtypeStruct((n_tiles, V7X_SUBLANES, TS), F32),
            jax.ShapeDtypeStruct((n_tiles, N_EXP), jnp.int32),
            jax.ShapeDtypeStruct((n_tiles * TS, D), BF16),
        ],
        compiler_params=pltpu.CompilerParams(
            dimension_semantics=("arbitrary",), vmem_limit_bytes=VMEM_LIMIT),
        name="moe_rank",
    )(x, mod_l, wrt, brt)


def _dispatch_kernel(off_ref, loc_ref, cnt_ref, fill_ref, rinfo_ref, u_ref, xg_ref, info_ref,
                     sorted_scr, zero_scr, sem, zsem):
    i = pl.program_id(0)
    n_steps = pl.num_programs(0)
    slot = i % 2
    rinfo = rinfo_ref[0]
    i1 = rinfo[0:1]
    i2 = rinfo[1:2]
    loc1 = jnp.zeros_like(i1)
    loc2 = jnp.zeros_like(i2)
    for e in range(N_EXP):
        local = loc_ref[i * N_EXP + e].astype(F32)
        loc1 = jnp.where(i1 == float(e), local, loc1)
        loc2 = jnp.where(i2 == float(e), local, loc2)
    row1 = loc1 + rinfo[2:3]
    row2 = loc2 + rinfo[3:4]
    info_ref[0] = jnp.concatenate([row1, row2, rinfo[4:6], jnp.zeros((V7X_SUBLANES - 4, TS), F32)], axis=0)
    p_idx = lax.broadcasted_iota(jnp.int32, (SORT_ROWS, TS), 0).astype(F32)
    sel = jnp.where(jnp.logical_or(p_idx == row1, p_idx == row2), 1.0, 0.0).astype(BF16)
    sorted_scr[slot] = jnp.dot(sel, u_ref[...], preferred_element_type=F32)

    def segment_copies(tile, s, do):
        def make_copy(local, glob, size):
            return pltpu.make_async_copy(sorted_scr.at[s, pl.ds(local, size)], xg_ref.at[pl.ds(glob, size)],
                                         sem.at[s])
        _segment_copies(off_ref, loc_ref, cnt_ref, tile, make_copy, do)

    @pl.when(i > 0)
    def _():
        segment_copies(i - 1, 1 - slot, lambda cp: cp.wait())

    segment_copies(i, slot, lambda cp: cp.start())

    @pl.when(i == n_steps - 1)
    def _():
        segment_copies(i, slot, lambda cp: cp.wait())
        zero_scr[...] = jnp.zeros_like(zero_scr)
        for e in range(N_EXP):
            end = fill_ref[e]
            n_tail = fill_ref[N_EXP + e] // SEG_PAD

            def tail_copy(k):
                return pltpu.make_async_copy(
                    zero_scr.at[pl.ds(0, SEG_PAD)],
                    xg_ref.at[pl.ds(pl.multiple_of(end + k * SEG_PAD, SEG_PAD), SEG_PAD)], zsem)

            def tail_start(k, carry):
                tail_copy(k).start()
                return carry

            def tail_wait(k, carry):
                tail_copy(k).wait()
                return carry

            lax.fori_loop(0, n_tail, tail_start, 0)
            lax.fori_loop(0, n_tail, tail_wait, 0)

        def tile_copy(k):
            return pltpu.make_async_copy(zero_scr, xg_ref.at[pl.ds(pl.multiple_of(k * TR, TR), TR)], zsem)

        def tile_start(k, carry):
            tile_copy(k).start()
            return carry

        def tile_wait(k, carry):
            tile_copy(k).wait()
            return carry

        lax.fori_loop(fill_ref[2 * N_EXP], MAX_RT, tile_start, 0)
        lax.fori_loop(fill_ref[2 * N_EXP], MAX_RT, tile_wait, 0)


def _dispatch(u2b, rinfo, off, loc, cnt, fill):
    n_tiles = rinfo.shape[0]
    tile3 = lambda i, *_: (i, 0, 0)
    return pl.pallas_call(
        _dispatch_kernel,
        grid_spec=pltpu.PrefetchScalarGridSpec(
            num_scalar_prefetch=4,
            grid=(n_tiles,),
            in_specs=[
                pl.BlockSpec((1, V7X_SUBLANES, TS), tile3),
                pl.BlockSpec((TS, D), lambda i, *_: (i, 0)),
            ],
            out_specs=[
                pl.BlockSpec(memory_space=pl.ANY),
                pl.BlockSpec((1, V7X_SUBLANES, TS), tile3),
            ],
            scratch_shapes=[
                pltpu.VMEM((2, SORT_ROWS, D), F32),
                pltpu.VMEM((TR, D), F32),
                pltpu.SemaphoreType.DMA((2,)),
                pltpu.SemaphoreType.DMA,
            ],
        ),
        out_shape=[
            jax.ShapeDtypeStruct((MAX_RT * TR, D), F32),
            jax.ShapeDtypeStruct((n_tiles, V7X_SUBLANES, TS), F32),
        ],
        compiler_params=pltpu.CompilerParams(
            dimension_semantics=("arbitrary",), vmem_limit_bytes=VMEM_LIMIT),
        name="moe_dispatch",
    )(off, loc, cnt, fill, rinfo, u2b)


def _expert_kernel(second, exp_ref, new_ref, nact_ref, *refs):
    if second:
        xg_ref, w1_ref, w3t_ref, w3b_ref, w2_ref, yp_ref, o_ref, w13b, w2b = refs
    else:
        xg_ref, w1_ref, w3t_ref, w3b_ref, w2_ref, o_ref, w13b, w2b = refs
    s = pl.program_id(0)
    r = s - 1
    active = jnp.logical_and(s >= 1, r < nact_ref[0])

    @pl.when(jnp.logical_and(active, new_ref[jnp.maximum(r, 0)] == 1))
    def _():
        w13b[D // 2:D, FC:2 * FC] = w3b_ref[0].astype(BF16)
        w2b[...] = w2_ref[0].astype(BF16)

    @pl.when(active)
    def _():
        xb = xg_ref[...].astype(BF16)
        h = jnp.dot(xb, w13b[...], preferred_element_type=F32)
        h1 = h[:, 0:FC]
        hid = (h1 * _sigmoid(h1) * h[:, FC:2 * FC]).astype(BF16)
        p = jnp.dot(hid, w2b[...], preferred_element_type=F32)
        if second:
            p = yp_ref[...] + p
        o_ref[...] = p

    @pl.when(jnp.logical_and(s >= 1, jnp.logical_not(active)))
    def _():
        o_ref[...] = jnp.zeros_like(o_ref)

    @pl.when(new_ref[s] == 1)
    def _():
        w13b[:, 0:FC] = w1_ref[0].astype(BF16)
        w13b[0:D // 2, FC:2 * FC] = w3t_ref[0].astype(BF16)


def _expert_pass(second, tables, xg, w1, w3, w2, yp=None):
    f = 1 if second else 0
    tile = lambda s: jnp.maximum(s - 1, 0)
    ahead = lambda s: jnp.minimum(s, MAX_RT - 1)
    row = lambda s, ex, nw, na: (tile(s), 0)
    row_in = lambda s, ex, nw, na: (jnp.clip(tile(s), 0, jnp.maximum(na[0] - 1, 0)), 0)
    in_specs = [
        pl.BlockSpec((TR, D), row_in),
        pl.BlockSpec((1, D, FC), lambda s, ex, nw, na: (ex[ahead(s)], 0, f)),
        pl.BlockSpec((1, D // 2, FC), lambda s, ex, nw, na: (ex[ahead(s)], 0, f)),
        pl.BlockSpec((1, D // 2, FC), lambda s, ex, nw, na: (ex[tile(s)], 1, f)),
        pl.BlockSpec((1, FC, D), lambda s, ex, nw, na: (ex[tile(s)], f, 0)),
    ]
    args = [xg, w1, w3, w3, w2]
    if second:
        in_specs.append(pl.BlockSpec((TR, D), row_in))
        args.append(yp)
    return pl.pallas_call(
        functools.partial(_expert_kernel, second),
        grid_spec=pltpu.PrefetchScalarGridSpec(
            num_scalar_prefetch=3,
            grid=(MAX_RT + 1,),
            in_specs=in_specs,
            out_specs=pl.BlockSpec((TR, D), row),
            scratch_shapes=[pltpu.VMEM((D, 2 * FC), BF16), pltpu.VMEM((FC, D), BF16)],
        ),
        out_shape=jax.ShapeDtypeStruct((MAX_RT * TR, D), F32),
        compiler_params=pltpu.CompilerParams(
            dimension_semantics=("arbitrary",), vmem_limit_bytes=EXPERT_VMEM_LIMIT),
        name="moe_expert_hi" if second else "moe_expert_lo",
    )(*tables, *args)


def _combine_kernel(n_ctx_tok, off_ref, loc_ref, cnt_ref, x_ref, mod_ref, info_ref, y_ref, g_ref, b_ref,
                    oc_ref, od_ref, ys_scr, sem):
    i = pl.program_id(0)
    n_steps = pl.num_programs(0)
    slot = i % 2

    def fetch(tile, s, do):
        def make_copy(local, glob, size):
            return pltpu.make_async_copy(y_ref.at[pl.ds(glob, size)], ys_scr.at[s, pl.ds(local, size)],
                                         sem.at[s])
        _segment_copies(off_ref, loc_ref, cnt_ref, tile, make_copy, do)

    def start_fetch(tile, s):
        ys_scr[s, 2 * TS:SORT_ROWS, :] = jnp.zeros((SORT_ROWS - 2 * TS, D), F32)
        fetch(tile, s, lambda cp: cp.start())

    @pl.when(i == 0)
    def _():
        start_fetch(0, 0)

    @pl.when(i + 1 < n_steps)
    def _():
        start_fetch(i + 1, 1 - slot)

    info = info_ref[0]
    row1, row2, w1, w2 = info[0:1], info[1:2], info[2:3], info[3:4]
    p_idx = lax.broadcasted_iota(jnp.int32, (SORT_ROWS, TS), 0).astype(F32)
    w_rows = jnp.sum(jnp.where(p_idx == row1, w1, 0.0) + jnp.where(p_idx == row2, w2, 0.0),
                     axis=1, keepdims=True)
    padded = jnp.concatenate([info, jnp.zeros((V7X_LANES - V7X_SUBLANES, TS), F32)], axis=0)
    cols = jnp.transpose(padded, (1, 0))
    q_idx = lax.broadcasted_iota(jnp.int32, (TS, SORT_ROWS), 1).astype(F32)
    pick = jnp.where(jnp.logical_or(q_idx == cols[:, 0:1], q_idx == cols[:, 1:2]), 1.0, 0.0).astype(BF16)
    fetch(i, slot, lambda cp: cp.wait())
    scaled = (ys_scr[slot] * w_rows).astype(BF16)
    acc = jnp.dot(pick, scaled, preferred_element_type=F32)
    res = _ffn_epilogue(n_ctx_tok, x_ref, mod_ref, acc, g_ref, b_ref)
    is_dec = i * TS >= n_ctx_tok

    @pl.when(jnp.logical_not(is_dec))
    def _():
        oc_ref[...] = res

    @pl.when(is_dec)
    def _():
        od_ref[...] = res


def _combine(x, mod_l, info, y, segs, ln_g, ln_b, n_ctx_tok):
    n_tok = x.shape[0]
    n_tiles = n_tok // TS
    ncb = n_ctx_tok // TS
    const = lambda i, *_: (0, 0)
    return pl.pallas_call(
        functools.partial(_combine_kernel, n_ctx_tok),
        grid_spec=pltpu.PrefetchScalarGridSpec(
            num_scalar_prefetch=3,
            grid=(n_tiles,),
            in_specs=[
                pl.BlockSpec((TS, D), lambda i, *_: (i, 0)),
                pl.BlockSpec((MOD_ROWS, N_MOD * D), const),
                pl.BlockSpec((1, V7X_SUBLANES, TS), lambda i, *_: (i, 0, 0)),
                pl.BlockSpec(memory_space=pl.ANY),
                pl.BlockSpec((1, D), const),
                pl.BlockSpec((1, D), const),
            ],
            out_specs=[
                pl.BlockSpec((TS, D), lambda i, *_: (jnp.minimum(i, ncb - 1), 0)),
                pl.BlockSpec((TS, D), lambda i, *_: (jnp.maximum(i - ncb, 0), 0)),
            ],
            scratch_shapes=[
                pltpu.VMEM((2, SORT_ROWS, D), F32),
                pltpu.SemaphoreType.DMA((2,)),
            ],
        ),
        out_shape=[
            jax.ShapeDtypeStruct((n_ctx_tok, D), F32),
            jax.ShapeDtypeStruct((n_tok - n_ctx_tok, D), F32),
        ],
        compiler_params=pltpu.CompilerParams(
            dimension_semantics=("arbitrary",), vmem_limit_bytes=VMEM_LIMIT),
        name="moe_combine",
    )(*segs, x, mod_l, info, y, ln_g, ln_b)


def _routing_tables(counts):
    counts = (counts + SEG_PAD - 1) // SEG_PAD * SEG_PAD
    totals = jnp.sum(counts, axis=0)
    n_rt = (totals + TR - 1) // TR
    cum = jnp.cumsum(n_rt)
    first = cum - n_rt
    n_act = cum[-1]
    off = first[None, :] * TR + (jnp.cumsum(counts, axis=0) - counts)
    loc = jnp.cumsum(counts, axis=1) - counts
    fill = jnp.concatenate([first * TR + totals, n_rt * TR - totals, n_act[None]])
    r = jnp.arange(MAX_RT, dtype=jnp.int32)
    rc = jnp.minimum(r, n_act - 1)
    exp = jnp.sum((rc[:, None] >= cum[None, :]).astype(jnp.int32), axis=1)
    new = jnp.concatenate([jnp.logical_and(r == first[exp], r < n_act), jnp.zeros((1,), bool)])
    i32 = lambda a: a.astype(jnp.int32)
    segs = (i32(off.reshape(-1)), i32(loc.reshape(-1)), i32(counts.reshape(-1)))
    return segs, i32(fill), (i32(exp), i32(new), i32(n_act.reshape(1)))


def _moe_ffn(x, mod_l, router_w, router_b, w1, w3, w2, ln_g, ln_b, n_ctx_tok):
    n_tok = x.shape[0]
    n_seg_pad = (n_tok // TS) * N_EXP * (SEG_PAD - 1)
    assert (2 * n_tok + n_seg_pad + TR - 1) // TR + N_EXP == MAX_RT and N_FC == 2
    wrt = router_w.T
    brt = jnp.broadcast_to(router_b.reshape(N_EXP, 1), (N_EXP, V7X_LANES))
    rinfo, counts, u2b = _rank(x, mod_l, wrt, brt, n_ctx_tok)
    segs, fill, tables = _routing_tables(counts)
    xg, info = _dispatch(u2b, rinfo, *segs, fill)
    y_lo = _expert_pass(False, tables, xg, w1, w3, w2)
    y = _expert_pass(True, tables, xg, w1, w3, w2, y_lo)
    return _combine(x, mod_l, info, y, segs, ln_g, ln_b, n_ctx_tok)


def _cast_kernel(*refs):
    o_ref = refs[-1]
    off = 0
    for x_ref in refs[:-1]:
        width = x_ref.shape[-1]
        o_ref[..., off:off + width] = x_ref[...].reshape(o_ref.shape[:-1] + (width,)).astype(o_ref.dtype)
        off += width


def _pack_w13(w1, w3):
    return pl.pallas_call(
        _cast_kernel,
        grid=(N_FC,),
        in_specs=[pl.BlockSpec((D, FC), lambda f: (0, f)), pl.BlockSpec((D, FC), lambda f: (0, f))],
        out_specs=pl.BlockSpec((1, D, 2 * FC), lambda f: (f, 0, 0)),
        out_shape=jax.ShapeDtypeStruct((N_FC, D, 2 * FC), BF16),
        compiler_params=pltpu.CompilerParams(vmem_limit_bytes=VMEM_LIMIT),
        name="pack_w13",
    )(w1, w3)


def _pack_w_in(w_in):
    depth = w_in.shape[0]
    def regroup_kernel(x_ref, o_ref):
        for j in range(N_CH):
            o_ref[0, j] = x_ref[0, :, j * CW:(j + 1) * CW].astype(BF16)

    w5 = pl.pallas_call(
        regroup_kernel,
        grid=(depth, 5),
        in_specs=[pl.BlockSpec((1, D, D), lambda l, g: (l, 0, g))],
        out_specs=pl.BlockSpec((1, N_CH, D, CW), lambda l, g: (l, 0, 0, g)),
        out_shape=jax.ShapeDtypeStruct((depth, N_CH, D, 5 * CW), BF16),
        compiler_params=pltpu.CompilerParams(vmem_limit_bytes=VMEM_LIMIT),
        name="pack_w5",
    )(w_in)
    wg = pl.pallas_call(
        _cast_kernel,
        grid=(depth, 2),
        in_specs=[pl.BlockSpec((1, D, D), lambda l, k: (l, 0, 5 + k))],
        out_specs=pl.BlockSpec((1, D, D), lambda l, k: (l, 0, k)),
        out_shape=jax.ShapeDtypeStruct((depth, D, 2 * D), BF16),
        compiler_params=pltpu.CompilerParams(vmem_limit_bytes=VMEM_LIMIT),
        name="pack_wg",
    )(w_in)
    return w5, wg


def _block_diag_chunks(w):
    tiled = jnp.tile(w.reshape(N_CH, CW, HEAD_D), (1, 1, CW // HEAD_D))
    blk = jnp.arange(CW) // HEAD_D
    return jnp.where(blk[:, None] == blk[None, :], tiled, 0.0)


def kernel(x_prompt, x_sample, state_rglru, c, c_ctx, w_mod, b_mod, w_in, conv_a, w_a_out, conv_b, conv_b_bias, w_gate_a, b_gate_a, w_gate_x, b_gate_x, lru_lambda, w_b_out, w_o, ln1_g, ln1_b, ln2_g, ln2_b, ffn_w1, ffn_w3, ffn_w2, router_w, router_b, moe_w1, moe_w3, moe_w2):
    batch, seq, d = x_prompt.shape
    dec_batch, dec_seq, _ = x_sample.shape
    depth = w_mod.shape[0]
    assert (d, seq, dec_seq, depth) == (D, SEQ, DEC_SEQ, DEPTH)
    n_ctx_tok = batch * seq
    n_dec_tok = dec_batch * dec_seq
    assert n_ctx_tok % TM1 == 0 and TM1 == dec_seq and 1 + dec_batch <= MOD_ROWS
    n_ctx_tiles = n_ctx_tok // TM1
    n_tiles = n_ctx_tiles + dec_batch
    seq_per_tile = TM1 // seq
    assert N_SEG == 2 * seq_per_tile

    assert depth % 2 == 0
    n_tok = n_ctx_tok + n_dec_tok
    x = (x_prompt.reshape(n_ctx_tok, D), x_sample.reshape(n_dec_tok, D))

    cond = jnp.zeros((MOD_ROWS, D), F32).at[0].set(c_ctx).at[1:1 + dec_batch].set(c)
    mod = _modulation(cond, w_mod, b_mod)

    w5, wg = _pack_w_in(w_in)
    states = []
    u_pre = None
    for l in range(depth):
        wbd = jnp.concatenate(
            [_block_diag_chunks(w_gate_a[l, 0]), _block_diag_chunks(w_gate_x[l, 0]),
             _block_diag_chunks(w_gate_a[l, 1]), _block_diag_chunks(w_gate_x[l, 1])],
            axis=-1).astype(BF16)
        gbias = jnp.stack([b_gate_a[l, 0], b_gate_x[l, 0], b_gate_a[l, 1], b_gate_x[l, 1]], axis=0)
        h0 = jnp.zeros((n_tiles, 2, N_SEG, D), F32)
        h0 = h0.at[n_ctx_tiles:, 0, 0].set(state_rglru[:, l, 0].astype(F32))
        h0 = h0.at[n_ctx_tiles:, 1, N_SEG - 1].set(state_rglru[:, l, 1].astype(F32))

        a_pre, b_pre, st = _mixer_part1(
            x, u_pre, mod[l], l, w5, conv_a[l], conv_b[l], conv_b_bias[l].reshape(1, D), wbd, gbias,
            lru_lambda[l], h0, n_tok, n_ctx_tok)
        x = _mixer_part2(
            x, u_pre, mod[l], l, a_pre, b_pre, wg, w_a_out[l].astype(BF16), w_b_out[l].astype(BF16),
            w_o[l].astype(BF16), ln1_g[l].reshape(1, D), ln1_b[l].reshape(1, D), n_ctx_tok)
        u_pre = None

        if l % 2 == 0:
            k = l // 2
            x, u_pre = _dense_ffn(x, mod[l], mod[l + 1], _pack_w13(ffn_w1[k], ffn_w3[k]),
                                  ffn_w2[k].astype(BF16), ln2_g[l].reshape(1, D), ln2_b[l].reshape(1, D),
                                  n_ctx_tok)
        else:
            k = l // 2
            x = _moe_ffn(x, mod[l], router_w[k], router_b[k], moe_w1[k], moe_w3[k], moe_w2[k],
                         ln2_g[l].reshape(1, D), ln2_b[l].reshape(1, D), n_ctx_tok)

        st_ctx = st[:n_ctx_tiles]
        fwd = st_ctx[:, 0, 1::2].reshape(batch, D)
        bwd = st_ctx[:, 1, 0::2].reshape(batch, D)
        states.append(jnp.stack([fwd, bwd], axis=1))

    y_prompt = x[0].reshape(batch, seq, D)
    y_sample = x[1].reshape(dec_batch, dec_seq, D)
    new_state = jnp.stack(states, axis=1).astype(x_prompt.dtype)
    return (y_prompt, y_sample, new_state)
```

```python
import functools

import jax
import jax.numpy as jnp
from jax import lax
from jax.experimental import pallas as pl
from jax.experimental.pallas import tpu as pltpu

F32 = jnp.float32
BF16 = jnp.bfloat16

D = 1024
SEQ = 256
DEC_SEQ = 1024
GRID_W = 64
N_HEAD = 16
HEAD_D = D // N_HEAD
RGLRU_C = 8.0
LOG2_E = 1.4426950408889634
D_FF = 2816
N_EXP = 8
N_MOD = 6
DEPTH = 2
DN_ALPHA = (2.0 * DEPTH) ** 0.25

V7X_SUBLANES = 8
V7X_LANES = 128
V7X_VMEM_BYTES = 64 * 1024 * 1024
VMEM_LIMIT = V7X_VMEM_BYTES - 12 * 1024 * 1024

TM1 = 1024
CW = 256
N_CH = D // CW
N_LC = CW // V7X_LANES
N_SEG = V7X_SUBLANES
SEG = TM1 // N_SEG
SEG_STRIDE = SEG + 4
CONV_GAP = V7X_SUBLANES
TM2 = 512
TM3 = 512
FC = 1408
N_FC = D_FF // FC
TS = 512
TR = 256
SEG_PAD = V7X_SUBLANES
SEG_BITS = 7
SORT_ROWS = 2 * TS + V7X_LANES
MAX_RT = (2 * 10240 + (10240 // TS) * N_EXP * (SEG_PAD - 1) + TR - 1) // TR + N_EXP
EXPERT_VMEM_LIMIT = V7X_VMEM_BYTES - 6 * 1024 * 1024
MOD_ROWS = 8
MOD_BLK = 3072


def _sigmoid(x):
    return 0.5 * jnp.tanh(0.5 * x) + 0.5


def _ln_plain(x, eps):
    mu = jnp.mean(x, axis=-1, keepdims=True)
    xc = x - mu
    var = jnp.mean(xc * xc, axis=-1, keepdims=True)
    return xc * lax.rsqrt(var + eps)


def _token_source(xs, tile, n_ctx_tok):
    ncb = n_ctx_tok // tile
    if isinstance(xs, tuple):
        x_ctx, x_dec = xs
        dec_off = 0
    else:
        x_ctx = x_dec = xs
        dec_off = ncb
    ctx_map = lambda i: (jnp.minimum(i, ncb - 1), 0)
    dec_map = lambda i: (jnp.maximum(i - ncb, 0) + dec_off, 0)
    return x_ctx, x_dec, ctx_map, dec_map


def _mod_row(tok0, n_ctx_tok):
    dec = jnp.maximum(tok0 - n_ctx_tok, 0) // DEC_SEQ
    return jnp.where(tok0 >= n_ctx_tok, 1 + dec, 0)


def _mod_kernel(cond_ref, w_ref, b_ref, o_ref):
    cnd = cond_ref[...]
    s = cnd * _sigmoid(cnd)
    o_ref[0] = jnp.dot(s, w_ref[0], preferred_element_type=F32,
                       precision=lax.Precision.HIGHEST) + b_ref[0]


def _modulation(cond, w_mod, b_mod):
    depth = w_mod.shape[0]
    n_out = w_mod.shape[2]
    return pl.pallas_call(
        _mod_kernel,
        grid=(depth, n_out // MOD_BLK),
        in_specs=[
            pl.BlockSpec((MOD_ROWS, D), lambda l, j: (0, 0)),
            pl.BlockSpec((1, D, MOD_BLK), lambda l, j: (l, 0, j)),
            pl.BlockSpec((1, 1, MOD_BLK), lambda l, j: (l, 0, j)),
        ],
        out_specs=pl.BlockSpec((1, MOD_ROWS, MOD_BLK), lambda l, j: (l, 0, j)),
        out_shape=jax.ShapeDtypeStruct((depth, MOD_ROWS, n_out), F32),
        compiler_params=pltpu.CompilerParams(
            dimension_semantics=("arbitrary", "arbitrary"), vmem_limit_bytes=VMEM_LIMIT),
        name="modulation",
    )(cond, w_mod, b_mod.reshape(depth, 1, n_out))


def _scan_dir(a_scr, b_scr, hl_scr, ac_scr, h0, keep, reverse):
    n_lc = a_scr.shape[0]

    def body(k, carry):
        kk = SEG - 1 - k if reverse else k
        idx = pl.ds(kk, N_SEG, stride=SEG_STRIDE)
        new = []
        for c in range(n_lc):
            h, acc = carry[c]
            a_k = a_scr[c, idx, :]
            h = a_k * h + b_scr[c, idx, :]
            acc = a_k * acc
            hl_scr[c, idx, :] = h
            ac_scr[c, idx, :] = acc
            new.append((h, acc))
        return tuple(new)

    init = tuple((jnp.zeros((N_SEG, V7X_LANES), F32), jnp.ones((N_SEG, V7X_LANES), F32))
                 for _ in range(n_lc))
    fin = lax.fori_loop(0, SEG, body, init, unroll=8)
    h_loc = jnp.concatenate([fin[c][0] for c in range(n_lc)], axis=1)
    a_tot = jnp.concatenate([fin[c][1] for c in range(n_lc)], axis=1)

    order = range(N_SEG - 1, -1, -1) if reverse else range(N_SEG)
    h_in = [None] * N_SEG
    prev = None
    for s in order:
        cur = h0[s:s + 1]
        if prev is not None:
            left = a_tot[prev:prev + 1] * h_in[prev] + h_loc[prev:prev + 1]
            cur = keep[s] * left + cur
        h_in[s] = cur
        prev = s
    h_in = jnp.concatenate(h_in, axis=0)
    return h_in, a_tot * h_in + h_loc


def _conv_stage(is_dec, proj_scr, ca_ref, cb_ref, cbias_ref, apre_ref, xr_scr, gap_scr):
    win = GRID_W
    n_win = TM1 // win
    stride = win + CONV_GAP
    ca = ca_ref[...]
    cb = cb_ref[...]
    bias = cbias_ref[...]
    pieces = [(w, c) for w in range(n_win) for c in range(N_LC)]
    sub = lax.broadcasted_iota(jnp.int32, (CONV_GAP, V7X_LANES), 0)
    joined = jnp.where(is_dec, jnp.float32(0.0), jnp.float32(1.0))
    zero_gap = jnp.zeros((CONV_GAP, V7X_LANES), F32)

    def stage(value_of):
        for c in range(N_LC):
            for w in range(n_win + 1):
                edge = w * win
                if edge % SEQ == 0:
                    gap = zero_gap
                else:
                    before = value_of(slice(edge - CONV_GAP, edge), c)
                    after = value_of(slice(edge, edge + CONV_GAP), c)
                    gap = joined * jnp.where(sub >= CONV_GAP - 2, before, jnp.where(sub == 0, after, 0.0))
                gap_scr[c, w * stride:w * stride + CONV_GAP, :] = gap
        for w, c in pieces:
            lo = CONV_GAP + w * stride
            gap_scr[c, lo:lo + win, :] = value_of(slice(w * win, (w + 1) * win), c)

    def tap(w, c, shift):
        lo = CONV_GAP + w * stride + shift
        return gap_scr[c, lo:lo + win, :]

    def chunk_cols(k, c):
        return slice(k * CW + c * V7X_LANES, k * CW + (c + 1) * V7X_LANES)

    stage(lambda rows, c: proj_scr[rows, chunk_cols(2, c)] * proj_scr[rows, chunk_cols(0, c)])
    for w, c in pieces:
        rows = slice(w * win, (w + 1) * win)
        lanes = slice(c * V7X_LANES, (c + 1) * V7X_LANES)
        conv = ca[0:1, lanes] * tap(w, c, -1) + ca[1:2, lanes] * tap(w, c, 0) + ca[2:3, lanes] * tap(w, c, 1)
        apre_ref[rows, lanes] = (proj_scr[rows, chunk_cols(1, c)] * conv).astype(BF16)
    stage(lambda rows, c: proj_scr[rows, chunk_cols(4, c)])
    for w, c in pieces:
        rows = slice(w * win, (w + 1) * win)
        lanes = slice(c * V7X_LANES, (c + 1) * V7X_LANES)
        xr_scr[rows, lanes] = (cb[0:1, lanes] * tap(w, c, -2) + cb[1:2, lanes] * tap(w, c, -1)
                               + cb[2:3, lanes] * tap(w, c, 0) + cb[3:4, lanes] * tap(w, c, 1)
                               + bias[:, lanes])


def _mix1_kernel(n_ctx_tiles, pre_normed, *refs):
    if pre_normed:
        u_ref, refs = refs[0], refs[1:]
    else:
        xc_ref, xd_ref, mod_ref, refs = refs[0], refs[1], refs[2], refs[3:]
    (w5_ref, ca_ref, cb_ref, cbias_ref, wbd_ref, gb_ref, lam_ref, h0_ref, apre_ref, bpre_ref, st_ref,
     u_scr, proj_scr, a_scr, b_scr, hl_scr, ac_scr, hsum_scr, xr_scr, gap_scr) = refs
    i = pl.program_id(0)
    j = pl.program_id(1)
    is_dec = i >= n_ctx_tiles

    if pre_normed:
        proj_scr[...] = jnp.dot(u_ref[...], w5_ref[0, 0], preferred_element_type=F32)
    else:
        @pl.when(j == 0)
        def _():
            row = jnp.where(is_dec, i - (n_ctx_tiles - 1), 0)
            m = mod_ref[pl.ds(row, 1), :]
            sh1 = m[:, 0:D]
            sc1 = m[:, D:2 * D]
            x = jnp.where(is_dec, xd_ref[...], xc_ref[...])
            u_scr[...] = (_ln_plain(x, 1e-6) * (1.0 + sc1) + sh1).astype(BF16)

        proj_scr[...] = jnp.dot(u_scr[...], w5_ref[0, 0], preferred_element_type=F32)

    _conv_stage(is_dec, proj_scr, ca_ref, cb_ref, cbias_ref, apre_ref, xr_scr, gap_scr)

    xr = xr_scr[...]
    gates = jnp.dot(xr.astype(BF16), wbd_ref[0], preferred_element_type=F32)
    gb = gb_ref[...]
    lam = lam_ref[...]
    sp = jnp.maximum(-lam, 0.0) + jnp.log1p(jnp.exp(-jnp.abs(lam)))
    rate = (-RGLRU_C * LOG2_E) * sp

    one = jnp.float32(1.0)
    for d in range(2):
        ga = gates[:, (2 * d) * CW:(2 * d + 1) * CW] + gb[2 * d:2 * d + 1]
        gx = gates[:, (2 * d + 1) * CW:(2 * d + 2) * CW] + gb[2 * d + 1:2 * d + 2]
        r = _sigmoid(ga)
        ig = _sigmoid(gx)
        a = jnp.exp2(r * rate[d:d + 1])
        y = 1.0 - a * a
        bt = jnp.where(y > 0.0, y * lax.rsqrt(y), 0.0) * (ig * xr)
        for s in range(N_SEG):
            lo = s * SEG_STRIDE
            for c in range(N_LC):
                lanes = slice(c * V7X_LANES, (c + 1) * V7X_LANES)
                a_scr[c, lo:lo + SEG, :] = a[s * SEG:(s + 1) * SEG, lanes]
                b_scr[c, lo:lo + SEG, :] = bt[s * SEG:(s + 1) * SEG, lanes]
        if d == 0:
            keep = [jnp.where(is_dec, one, jnp.float32(s % 2 == 1)) for s in range(N_SEG)]
        else:
            keep = [jnp.where(is_dec, one, jnp.float32(s % 2 == 0)) for s in range(N_SEG)]
        h_in, h_out = _scan_dir(a_scr, b_scr, hl_scr, ac_scr, h0_ref[0, d], keep, reverse=(d == 1))
        st_ref[0, d] = h_out
        for s in range(N_SEG):
            lo = s * SEG_STRIDE
            for c in range(N_LC):
                lanes = slice(c * V7X_LANES, (c + 1) * V7X_LANES)
                h_seg = hl_scr[c, lo:lo + SEG, :] + ac_scr[c, lo:lo + SEG, :] * h_in[s:s + 1, lanes]
                if d == 0:
                    hsum_scr[s * SEG:(s + 1) * SEG, lanes] = h_seg
                else:
                    hsum_scr[s * SEG:(s + 1) * SEG, lanes] += h_seg

    bpre_ref[...] = (hsum_scr[...] * jax.nn.gelu(proj_scr[:, 3 * CW:4 * CW])).astype(BF16)


def _mixer_part1(xs, u_pre, mod_l, layer, w5, conv_a, conv_b, conv_b_bias, wbd, gbias, lam, h0, n_tok,
                 n_ctx_tok):
    n_tiles = n_tok // TM1
    if u_pre is None:
        x_ctx, x_dec, ctx_map, dec_map = _token_source(xs, TM1, n_ctx_tok)
        lead_args = [x_ctx, x_dec, mod_l]
        lead_specs = [
            pl.BlockSpec((TM1, D), lambda i, j: ctx_map(i)),
            pl.BlockSpec((TM1, D), lambda i, j: dec_map(i)),
            pl.BlockSpec((MOD_ROWS, N_MOD * D), lambda i, j: (0, 0)),
        ]
    else:
        lead_args = [u_pre]
        lead_specs = [pl.BlockSpec((TM1, D), lambda i, j: (i, 0))]
    kern = functools.partial(_mix1_kernel, n_ctx_tok // TM1, u_pre is not None)
    return pl.pallas_call(
        kern,
        grid=(n_tiles, N_CH),
        in_specs=lead_specs + [
            pl.BlockSpec((1, 1, D, 5 * CW), lambda i, j: (layer, j, 0, 0)),
            pl.BlockSpec((3, CW), lambda i, j: (0, j)),
            pl.BlockSpec((4, CW), lambda i, j: (0, j)),
            pl.BlockSpec((1, CW), lambda i, j: (0, j)),
            pl.BlockSpec((1, CW, 4 * CW), lambda i, j: (j, 0, 0)),
            pl.BlockSpec((4, CW), lambda i, j: (0, j)),
            pl.BlockSpec((2, CW), lambda i, j: (0, j)),
            pl.BlockSpec((1, 2, N_SEG, CW), lambda i, j: (i, 0, 0, j)),
        ],
        out_specs=[
            pl.BlockSpec((TM1, CW), lambda i, j: (i, j)),
            pl.BlockSpec((TM1, CW), lambda i, j: (i, j)),
            pl.BlockSpec((1, 2, N_SEG, CW), lambda i, j: (i, 0, 0, j)),
        ],
        out_shape=[
            jax.ShapeDtypeStruct((n_tok, D), BF16),
            jax.ShapeDtypeStruct((n_tok, D), BF16),
            jax.ShapeDtypeStruct((n_tiles, 2, N_SEG, D), F32),
        ],
        scratch_shapes=[
            pltpu.VMEM((TM1, D), BF16),
            pltpu.VMEM((TM1, 5 * CW), F32),
            pltpu.VMEM((N_LC, N_SEG * SEG_STRIDE, V7X_LANES), F32),
            pltpu.VMEM((N_LC, N_SEG * SEG_STRIDE, V7X_LANES), F32),
            pltpu.VMEM((N_LC, N_SEG * SEG_STRIDE, V7X_LANES), F32),
            pltpu.VMEM((N_LC, N_SEG * SEG_STRIDE, V7X_LANES), F32),
            pltpu.VMEM((TM1, CW), F32),
            pltpu.VMEM((TM1, CW), F32),
            pltpu.VMEM((N_LC, CONV_GAP + (TM1 // GRID_W) * (GRID_W + CONV_GAP), V7X_LANES), F32),
        ],
        compiler_params=pltpu.CompilerParams(
            dimension_semantics=("arbitrary", "arbitrary"), vmem_limit_bytes=VMEM_LIMIT),
        name="mixer_scan",
    )(*lead_args, w5, conv_a, conv_b, conv_b_bias, wbd, gbias, lam, h0)


def _mix2_kernel(n_ctx_tok, pre_normed, *refs):
    if pre_normed:
        u_ref, refs = refs[0], refs[1:]
    xc_ref, xd_ref, mod_ref, ap_ref, bp_ref, wg_ref, wa_ref, wb_ref, wo_ref, g_ref, b_ref, o_ref = refs
    i = pl.program_id(0)
    m = mod_ref[pl.ds(_mod_row(i * TM2, n_ctx_tok), 1), :]
    sh1 = m[:, 0:D]
    sc1 = m[:, D:2 * D]
    g1 = m[:, 2 * D:3 * D]
    x = jnp.where(i * TM2 >= n_ctx_tok, xd_ref[...], xc_ref[...])
    if pre_normed:
        u = u_ref[...]
    else:
        u = (_ln_plain(x, 1e-6) * (1.0 + sc1) + sh1).astype(BF16)
    gates = jnp.dot(u, wg_ref[0], preferred_element_type=F32)
    br_a = jnp.dot(ap_ref[...], wa_ref[...], preferred_element_type=F32)
    br_b = jnp.dot(bp_ref[...], wb_ref[...], preferred_element_type=F32)
    merged = _sigmoid(gates[:, 0:D]) * br_a + _sigmoid(gates[:, D:2 * D]) * br_b
    mix = jnp.dot(merged.astype(BF16), wo_ref[...], preferred_element_type=F32)
    y = DN_ALPHA * x + g1 * mix
    o_ref[...] = _ln_plain(y, 1e-5) * g_ref[...] + b_ref[...]


def _mixer_part2(xs, u_pre, mod_l, layer, a_pre, b_pre, wg, wa, wb, wo, ln_g, ln_b, n_ctx_tok):
    n_tok = a_pre.shape[0]
    x_ctx, x_dec, ctx_map, dec_map = _token_source(xs, TM2, n_ctx_tok)
    const = lambda i: (0, 0)
    tile = lambda i: (i, 0)
    lead_args = [] if u_pre is None else [u_pre]
    lead_specs = [] if u_pre is None else [pl.BlockSpec((TM2, D), tile)]
    return pl.pallas_call(
        functools.partial(_mix2_kernel, n_ctx_tok, u_pre is not None),
        grid=(n_tok // TM2,),
        in_specs=lead_specs + [
            pl.BlockSpec((TM2, D), ctx_map),
            pl.BlockSpec((TM2, D), dec_map),
            pl.BlockSpec((MOD_ROWS, N_MOD * D), const),
            pl.BlockSpec((TM2, D), tile),
            pl.BlockSpec((TM2, D), tile),
            pl.BlockSpec((1, D, 2 * D), lambda i: (layer, 0, 0)),
            pl.BlockSpec((D, D), const),
            pl.BlockSpec((D, D), const),
            pl.BlockSpec((D, D), const),
            pl.BlockSpec((1, D), const),
            pl.BlockSpec((1, D), const),
        ],
        out_specs=pl.BlockSpec((TM2, D), tile),
        out_shape=jax.ShapeDtypeStruct((n_tok, D), F32),
        compiler_params=pltpu.CompilerParams(
            dimension_semantics=("arbitrary",), vmem_limit_bytes=VMEM_LIMIT),
        name="mixer_out",
    )(*lead_args, x_ctx, x_dec, mod_l, a_pre, b_pre, wg, wa, wb, wo, ln_g, ln_b)


def _ffn_prologue(n_ctx_tok, x_ref, mod_ref):
    m = mod_ref[pl.ds(_mod_row(pl.program_id(0) * TM3, n_ctx_tok), 1), :]
    sh2 = m[:, 3 * D:4 * D]
    sc2 = m[:, 4 * D:5 * D]
    return _ln_plain(x_ref[...], 1e-6) * (1.0 + sc2) + sh2


def _ffn_epilogue(n_ctx_tok, x_ref, mod_ref, acc, g_ref, b_ref):
    m = mod_ref[pl.ds(_mod_row(pl.program_id(0) * TM3, n_ctx_tok), 1), :]
    g2 = m[:, 5 * D:6 * D]
    y = DN_ALPHA * x_ref[...] + g2 * acc
    return _ln_plain(y, 1e-5) * g_ref[...] + b_ref[...]


def _swiglu_chunk(u, w13, w2):
    h = jnp.dot(u, w13, preferred_element_type=F32)
    h1 = h[:, 0:FC]
    hid = (h1 * _sigmoid(h1) * h[:, FC:2 * FC]).astype(BF16)
    return jnp.dot(hid, w2, preferred_element_type=F32)


def _dense_ffn_kernel(n_ctx_tok, x_ref, mod_ref, modn_ref, w13_ref, w2_ref, g_ref, b_ref, o_ref, un_ref):
    u = _ffn_prologue(n_ctx_tok, x_ref, mod_ref).astype(BF16)
    acc = _swiglu_chunk(u, w13_ref[0], w2_ref[0:FC, :])
    for f in range(1, N_FC):
        acc = acc + _swiglu_chunk(u, w13_ref[f], w2_ref[f * FC:(f + 1) * FC, :])
    res = _ffn_epilogue(n_ctx_tok, x_ref, mod_ref, acc, g_ref, b_ref)
    o_ref[...] = res
    m = modn_ref[pl.ds(_mod_row(pl.program_id(0) * TM3, n_ctx_tok), 1), :]
    un_ref[...] = (_ln_plain(res, 1e-6) * (1.0 + m[:, D:2 * D]) + m[:, 0:D]).astype(BF16)


def _dense_ffn(x, mod_l, mod_next, w13, w2, ln_g, ln_b, n_ctx_tok):
    n_tok = x.shape[0]
    tile = pl.BlockSpec((TM3, D), lambda i: (i, 0))
    table = pl.BlockSpec((MOD_ROWS, N_MOD * D), lambda i: (0, 0))
    once = pl.Buffered(1)
    return pl.pallas_call(
        functools.partial(_dense_ffn_kernel, n_ctx_tok),
        grid=(n_tok // TM3,),
        in_specs=[
            tile,
            table,
            table,
            pl.BlockSpec((N_FC, D, 2 * FC), lambda i: (0, 0, 0), pipeline_mode=once),
            pl.BlockSpec((D_FF, D), lambda i: (0, 0), pipeline_mode=once),
            pl.BlockSpec((1, D), lambda i: (0, 0)),
            pl.BlockSpec((1, D), lambda i: (0, 0)),
        ],
        out_specs=[tile, tile],
        out_shape=[jax.ShapeDtypeStruct((n_tok, D), F32), jax.ShapeDtypeStruct((n_tok, D), BF16)],
        compiler_params=pltpu.CompilerParams(
            dimension_semantics=("arbitrary",), vmem_limit_bytes=VMEM_LIMIT),
        name="dense_ffn",
    )(x, mod_l, mod_next, w13, w2, ln_g, ln_b)


def _segment_copies(off_ref, loc_ref, cnt_ref, tile, make_copy, do):
    for e in range(N_EXP):
        n = cnt_ref[tile * N_EXP + e]
        local = loc_ref[tile * N_EXP + e]
        glob = off_ref[tile * N_EXP + e]
        for k in range(SEG_BITS - 1, -1, -1):
            size = SEG_PAD << k
            take = (n & size) != 0

            @pl.when(take)
            def _():
                do(make_copy(pl.multiple_of(local, SEG_PAD), pl.multiple_of(glob, SEG_PAD), size))

            step = jnp.where(take, size, 0)
            local = local + step
            glob = glob + step


def _rank_kernel(n_ctx_tok, x_ref, mod_ref, wrt_ref, brt_ref, info_ref, cnt_ref, u2b_ref):
    i = pl.program_id(0)
    u2 = _ffn_prologue(n_ctx_tok, x_ref, mod_ref)
    u2b_ref[...] = u2.astype(BF16)
    lg = lax.dot_general(wrt_ref[...], u2, (((1,), (1,)), ((), ())), preferred_element_type=F32,
                         precision=lax.Precision.HIGHEST) + brt_ref[:, 0:1]
    eidx = lax.broadcasted_iota(jnp.int32, lg.shape, 0).astype(F32)
    neg = jnp.float32(-jnp.inf)
    v1 = jnp.max(lg, axis=0, keepdims=True)
    i1 = jnp.min(jnp.where(lg == v1, eidx, float(N_EXP)), axis=0, keepdims=True)
    lg2 = jnp.where(eidx == i1, neg, lg)
    v2 = jnp.max(lg2, axis=0, keepdims=True)
    i2 = jnp.min(jnp.where(lg2 == v2, eidx, float(N_EXP)), axis=0, keepdims=True)
    t = jnp.exp(v2 - v1)
    w_top = 1.0 / (1.0 + t)
    m1 = eidx == i1
    m2 = eidx == i2
    member = jnp.where(jnp.logical_or(m1, m2), 1.0, 0.0)
    before = (lax.broadcasted_iota(jnp.int32, (TS, TS), 0)
              < lax.broadcasted_iota(jnp.int32, (TS, TS), 1))
    rank = jnp.dot(member.astype(BF16), jnp.where(before, 1.0, 0.0).astype(BF16),
                   preferred_element_type=F32)
    for e in range(N_EXP):
        cnt_ref[i, e] = jnp.sum(member[e:e + 1, :]).astype(jnp.int32)
    rank1 = jnp.sum(jnp.where(m1, rank, 0.0), axis=0, keepdims=True)
    rank2 = jnp.sum(jnp.where(m2, rank, 0.0), axis=0, keepdims=True)
    info_ref[0] = jnp.concatenate(
        [i1, i2, rank1, rank2, w_top, t * w_top, jnp.zeros((V7X_SUBLANES - 6, TS), F32)], axis=0)


def _rank(x, mod_l, wrt, brt, n_ctx_tok):
    n_tiles = x.shape[0] // TS
    return pl.pallas_call(
        functools.partial(_rank_kernel, n_ctx_tok),
        grid=(n_tiles,),
        in_specs=[
            pl.BlockSpec((TS, D), lambda i: (i, 0)),
            pl.BlockSpec((MOD_ROWS, N_MOD * D), lambda i: (0, 0)),
            pl.BlockSpec((N_EXP, D), lambda i: (0, 0)),
            pl.BlockSpec((N_EXP, V7X_LANES), lambda i: (0, 0)),
        ],
        out_specs=[
            pl.BlockSpec((1, V7X_SUBLANES, TS), lambda i: (i, 0, 0)),
            pl.BlockSpec(memory_space=pltpu.SMEM),
            pl.BlockSpec((TS, D), lambda i: (i, 0)),
        ],
        out_shape=[
            jax.ShapeDtypeStruct((n_tiles, V7X_SUBLANES, TS), F32),
            jax.ShapeDtypeStruct((n_tiles, N_EXP), jnp.int32),
            jax.ShapeDtypeStruct((n_tiles * TS, D), BF16),
        ],
        compiler_params=pltpu.CompilerParams(
            dimension_semantics=("arbitrary",), vmem_limit_bytes=VMEM_LIMIT),
        name="moe_rank",
    )(x, mod_l, wrt, brt)


def _dispatch_kernel(off_ref, loc_ref, cnt_ref, fill_ref, rinfo_ref, u_ref, xg_ref, info_ref,
                     sorted_scr, zero_scr, sem, zsem):
    i = pl.program_id(0)
    n_steps = pl.num_programs(0)
    slot = i % 2
    rinfo = rinfo_ref[0]
    i1 = rinfo[0:1]
    i2 = rinfo[1:2]
    loc1 = jnp.zeros_like(i1)
    loc2 = jnp.zeros_like(i2)
    for e in range(N_EXP):
        local = loc_ref[i * N_EXP + e].astype(F32)
        loc1 = jnp.where(i1 == float(e), local, loc1)
        loc2 = jnp.where(i2 == float(e), local, loc2)
    row1 = loc1 + rinfo[2:3]
    row2 = loc2 + rinfo[3:4]
    info_ref[0] = jnp.concatenate([row1, row2, rinfo[4:6], jnp.zeros((V7X_SUBLANES - 4, TS), F32)], axis=0)
    p_idx = lax.broadcasted_iota(jnp.int32, (SORT_ROWS, TS), 0).astype(F32)
    sel = jnp.where(jnp.logical_or(p_idx == row1, p_idx == row2), 1.0, 0.0).astype(BF16)
    sorted_scr[slot] = jnp.dot(sel, u_ref[...], preferred_element_type=F32)

    def segment_copies(tile, s, do):
        def make_copy(local, glob, size):
            return pltpu.make_async_copy(sorted_scr.at[s, pl.ds(local, size)], xg_ref.at[pl.ds(glob, size)],
                                         sem.at[s])
        _segment_copies(off_ref, loc_ref, cnt_ref, tile, make_copy, do)

    @pl.when(i > 0)
    def _():
        segment_copies(i - 1, 1 - slot, lambda cp: cp.wait())

    segment_copies(i, slot, lambda cp: cp.start())

    @pl.when(i == n_steps - 1)
    def _():
        segment_copies(i, slot, lambda cp: cp.wait())
        zero_scr[...] = jnp.zeros_like(zero_scr)
        for e in range(N_EXP):
            end = fill_ref[e]
            n_tail = fill_ref[N_EXP + e] // SEG_PAD

            def tail_copy(k):
                return pltpu.make_async_copy(
                    zero_scr.at[pl.ds(0, SEG_PAD)],
                    xg_ref.at[pl.ds(pl.multiple_of(end + k * SEG_PAD, SEG_PAD), SEG_PAD)], zsem)

            def tail_start(k, carry):
                tail_copy(k).start()
                return carry

            def tail_wait(k, carry):
                tail_copy(k).wait()
                return carry

            lax.fori_loop(0, n_tail, tail_start, 0)
            lax.fori_loop(0, n_tail, tail_wait, 0)

        def tile_copy(k):
            return pltpu.make_async_copy(zero_scr, xg_ref.at[pl.ds(pl.multiple_of(k * TR, TR), TR)], zsem)

        def tile_start(k, carry):
            tile_copy(k).start()
            return carry

        def tile_wait(k, carry):
            tile_copy(k).wait()
            return carry

        lax.fori_loop(fill_ref[2 * N_EXP], MAX_RT, tile_start, 0)
        lax.fori_loop(fill_ref[2 * N_EXP], MAX_RT, tile_wait, 0)


def _dispatch(u2b, rinfo, off, loc, cnt, fill):
    n_tiles = rinfo.shape[0]
    tile3 = lambda i, *_: (i, 0, 0)
    return pl.pallas_call(
        _dispatch_kernel,
        grid_spec=pltpu.PrefetchScalarGridSpec(
            num_scalar_prefetch=4,
            grid=(n_tiles,),
            in_specs=[
                pl.BlockSpec((1, V7X_SUBLANES, TS), tile3),
                pl.BlockSpec((TS, D), lambda i, *_: (i, 0)),
            ],
            out_specs=[
                pl.BlockSpec(memory_space=pl.ANY),
                pl.BlockSpec((1, V7X_SUBLANES, TS), tile3),
            ],
            scratch_shapes=[
                pltpu.VMEM((2, SORT_ROWS, D), F32),
                pltpu.VMEM((TR, D), F32),
                pltpu.SemaphoreType.DMA((2,)),
                pltpu.SemaphoreType.DMA,
            ],
        ),
        out_shape=[
            jax.ShapeDtypeStruct((MAX_RT * TR, D), F32),
            jax.ShapeDtypeStruct((n_tiles, V7X_SUBLANES, TS), F32),
        ],
        compiler_params=pltpu.CompilerParams(
            dimension_semantics=("arbitrary",), vmem_limit_bytes=VMEM_LIMIT),
        name="moe_dispatch",
    )(off, loc, cnt, fill, rinfo, u2b)


def _expert_kernel(second, exp_ref, new_ref, nact_ref, *refs):
    if second:
        xg_ref, w1_ref, w3t_ref, w3b_ref, w2_ref, yp_ref, o_ref, w13b, w2b = refs
    else:
        xg_ref, w1_ref, w3t_ref, w3b_ref, w2_ref, o_ref, w13b, w2b = refs
    s = pl.program_id(0)
    r = s - 1
    active = jnp.logical_and(s >= 1, r < nact_ref[0])

    @pl.when(jnp.logical_and(active, new_ref[jnp.maximum(r, 0)] == 1))
    def _():
        w13b[D // 2:D, FC:2 * FC] = w3b_ref[0].astype(BF16)
        w2b[...] = w2_ref[0].astype(BF16)

    @pl.when(active)
    def _():
        xb = xg_ref[...].astype(BF16)
        h = jnp.dot(xb, w13b[...], preferred_element_type=F32)
        h1 = h[:, 0:FC]
        hid = (h1 * _sigmoid(h1) * h[:, FC:2 * FC]).astype(BF16)
        p = jnp.dot(hid, w2b[...], preferred_element_type=F32)
        if second:
            p = yp_ref[...] + p
        o_ref[...] = p

    @pl.when(jnp.logical_and(s >= 1, jnp.logical_not(active)))
    def _():
        o_ref[...] = jnp.zeros_like(o_ref)

    @pl.when(new_ref[s] == 1)
    def _():
        w13b[:, 0:FC] = w1_ref[0].astype(BF16)
        w13b[0:D // 2, FC:2 * FC] = w3t_ref[0].astype(BF16)


def _expert_pass(second, tables, xg, w1, w3, w2, yp=None):
    f = 1 if second else 0
    tile = lambda s: jnp.maximum(s - 1, 0)
    ahead = lambda s: jnp.minimum(s, MAX_RT - 1)
    row = lambda s, ex, nw, na: (tile(s), 0)
    row_in = lambda s, ex, nw, na: (jnp.clip(tile(s), 0, jnp.maximum(na[0] - 1, 0)), 0)
    in_specs = [
        pl.BlockSpec((TR, D), row_in),
        pl.BlockSpec((1, D, FC), lambda s, ex, nw, na: (ex[ahead(s)], 0, f)),
        pl.BlockSpec((1, D // 2, FC), lambda s, ex, nw, na: (ex[ahead(s)], 0, f)),
        pl.BlockSpec((1, D // 2, FC), lambda s, ex, nw, na: (ex[tile(s)], 1, f)),
        pl.BlockSpec((1, FC, D), lambda s, ex, nw, na: (ex[tile(s)], f, 0)),
    ]
    args = [xg, w1, w3, w3, w2]
    if second:
        in_specs.append(pl.BlockSpec((TR, D), row_in))
        args.append(yp)
    return pl.pallas_call(
        functools.partial(_expert_kernel, second),
        grid_spec=pltpu.PrefetchScalarGridSpec(
            num_scalar_prefetch=3,
            grid=(MAX_RT + 1,),
            in_specs=in_specs,
            out_specs=pl.BlockSpec((TR, D), row),
            scratch_shapes=[pltpu.VMEM((D, 2 * FC), BF16), pltpu.VMEM((FC, D), BF16)],
        ),
        out_shape=jax.ShapeDtypeStruct((MAX_RT * TR, D), F32),
        compiler_params=pltpu.CompilerParams(
            dimension_semantics=("arbitrary",), vmem_limit_bytes=EXPERT_VMEM_LIMIT),
        name="moe_expert_hi" if second else "moe_expert_lo",
    )(*tables, *args)


def _combine_kernel(n_ctx_tok, off_ref, loc_ref, cnt_ref, x_ref, mod_ref, info_ref, y_ref, g_ref, b_ref,
                    oc_ref, od_ref, ys_scr, sem):
    i = pl.program_id(0)
    n_steps = pl.num_programs(0)
    slot = i % 2

    def fetch(tile, s, do):
        def make_copy(local, glob, size):
            return pltpu.make_async_copy(y_ref.at[pl.ds(glob, size)], ys_scr.at[s, pl.ds(local, size)],
                                         sem.at[s])
        _segment_copies(off_ref, loc_ref, cnt_ref, tile, make_copy, do)

    def start_fetch(tile, s):
        ys_scr[s, 2 * TS:SORT_ROWS, :] = jnp.zeros((SORT_ROWS - 2 * TS, D), F32)
        fetch(tile, s, lambda cp: cp.start())

    @pl.when(i == 0)
    def _():
        start_fetch(0, 0)

    @pl.when(i + 1 < n_steps)
    def _():
        start_fetch(i + 1, 1 - slot)

    info = info_ref[0]
    row1, row2, w1, w2 = info[0:1], info[1:2], info[2:3], info[3:4]
    p_idx = lax.broadcasted_iota(jnp.int32, (SORT_ROWS, TS), 0).astype(F32)
    w_rows = jnp.sum(jnp.where(p_idx == row1, w1, 0.0) + jnp.where(p_idx == row2, w2, 0.0),
                     axis=1, keepdims=True)
    padded = jnp.concatenate([info, jnp.zeros((V7X_LANES - V7X_SUBLANES, TS), F32)], axis=0)
    cols = jnp.transpose(padded, (1, 0))
    q_idx = lax.broadcasted_iota(jnp.int32, (TS, SORT_ROWS), 1).astype(F32)
    pick = jnp.where(jnp.logical_or(q_idx == cols[:, 0:1], q_idx == cols[:, 1:2]), 1.0, 0.0).astype(BF16)
    fetch(i, slot, lambda cp: cp.wait())
    scaled = (ys_scr[slot] * w_rows).astype(BF16)
    acc = jnp.dot(pick, scaled, preferred_element_type=F32)
    res = _ffn_epilogue(n_ctx_tok, x_ref, mod_ref, acc, g_ref, b_ref)
    is_dec = i * TS >= n_ctx_tok

    @pl.when(jnp.logical_not(is_dec))
    def _():
        oc_ref[...] = res

    @pl.when(is_dec)
    def _():
        od_ref[...] = res


def _combine(x, mod_l, info, y, segs, ln_g, ln_b, n_ctx_tok):
    n_tok = x.shape[0]
    n_tiles = n_tok // TS
    ncb = n_ctx_tok // TS
    const = lambda i, *_: (0, 0)
    return pl.pallas_call(
        functools.partial(_combine_kernel, n_ctx_tok),
        grid_spec=pltpu.PrefetchScalarGridSpec(
            num_scalar_prefetch=3,
            grid=(n_tiles,),
            in_specs=[
                pl.BlockSpec((TS, D), lambda i, *_: (i, 0)),
                pl.BlockSpec((MOD_ROWS, N_MOD * D), const),
                pl.BlockSpec((1, V7X_SUBLANES, TS), lambda i, *_: (i, 0, 0)),
                pl.BlockSpec(memory_space=pl.ANY),
                pl.BlockSpec((1, D), const),
                pl.BlockSpec((1, D), const),
            ],
            out_specs=[
                pl.BlockSpec((TS, D), lambda i, *_: (jnp.minimum(i, ncb - 1), 0)),
                pl.BlockSpec((TS, D), lambda i, *_: (jnp.maximum(i - ncb, 0), 0)),
            ],
            scratch_shapes=[
                pltpu.VMEM((2, SORT_ROWS, D), F32),
                pltpu.SemaphoreType.DMA((2,)),
            ],
        ),
        out_shape=[
            jax.ShapeDtypeStruct((n_ctx_tok, D), F32),
            jax.ShapeDtypeStruct((n_tok - n_ctx_tok, D), F32),
        ],
        compiler_params=pltpu.CompilerParams(
            dimension_semantics=("arbitrary",), vmem_limit_bytes=VMEM_LIMIT),
        name="moe_combine",
    )(*segs, x, mod_l, info, y, ln_g, ln_b)


def _routing_tables(counts):
    counts = (counts + SEG_PAD - 1) // SEG_PAD * SEG_PAD
    totals = jnp.sum(counts, axis=0)
    n_rt = (totals + TR - 1) // TR
    cum = jnp.cumsum(n_rt)
    first = cum - n_rt
    n_act = cum[-1]
    off = first[None, :] * TR + (jnp.cumsum(counts, axis=0) - counts)
    loc = jnp.cumsum(counts, axis=1) - counts
    fill = jnp.concatenate([first * TR + totals, n_rt * TR - totals, n_act[None]])
    r = jnp.arange(MAX_RT, dtype=jnp.int32)
    rc = jnp.minimum(r, n_act - 1)
    exp = jnp.sum((rc[:, None] >= cum[None, :]).astype(jnp.int32), axis=1)
    new = jnp.concatenate([jnp.logical_and(r == first[exp], r < n_act), jnp.zeros((1,), bool)])
    i32 = lambda a: a.astype(jnp.int32)
    segs = (i32(off.reshape(-1)), i32(loc.reshape(-1)), i32(counts.reshape(-1)))
    return segs, i32(fill), (i32(exp), i32(new), i32(n_act.reshape(1)))


def _moe_ffn(x, mod_l, router_w, router_b, w1, w3, w2, ln_g, ln_b, n_ctx_tok):
    n_tok = x.shape[0]
    n_seg_pad = (n_tok // TS) * N_EXP * (SEG_PAD - 1)
    assert (2 * n_tok + n_seg_pad + TR - 1) // TR + N_EXP == MAX_RT and N_FC == 2
    wrt = router_w.T
    brt = jnp.broadcast_to(router_b.reshape(N_EXP, 1), (N_EXP, V7X_LANES))
    rinfo, counts, u2b = _rank(x, mod_l, wrt, brt, n_ctx_tok)
    segs, fill, tables = _routing_tables(counts)
    xg, info = _dispatch(u2b, rinfo, *segs, fill)
    y_lo = _expert_pass(False, tables, xg, w1, w3, w2)
    y = _expert_pass(True, tables, xg, w1, w3, w2, y_lo)
    return _combine(x, mod_l, info, y, segs, ln_g, ln_b, n_ctx_tok)


def _cast_kernel(*refs):
    o_ref = refs[-1]
    off = 0
    for x_ref in refs[:-1]:
        width = x_ref.shape[-1]
        o_ref[..., off:off + width] = x_ref[...].reshape(o_ref.shape[:-1] + (width,)).astype(o_ref.dtype)
        off += width


def _pack_w13(w1, w3):
    return pl.pallas_call(
        _cast_kernel,
        grid=(N_FC,),
        in_specs=[pl.BlockSpec((D, FC), lambda f: (0, f)), pl.BlockSpec((D, FC), lambda f: (0, f))],
        out_specs=pl.BlockSpec((1, D, 2 * FC), lambda f: (f, 0, 0)),
        out_shape=jax.ShapeDtypeStruct((N_FC, D, 2 * FC), BF16),
        compiler_params=pltpu.CompilerParams(vmem_limit_bytes=VMEM_LIMIT),
        name="pack_w13",
    )(w1, w3)


def _pack_w_in(w_in):
    depth = w_in.shape[0]
    def regroup_kernel(x_ref, o_ref):
        for j in range(N_CH):
            o_ref[0, j] = x_ref[0, :, j * CW:(j + 1) * CW].astype(BF16)

    w5 = pl.pallas_call(
        regroup_kernel,
        grid=(depth, 5),
        in_specs=[pl.BlockSpec((1, D, D), lambda l, g: (l, 0, g))],
        out_specs=pl.BlockSpec((1, N_CH, D, CW), lambda l, g: (l, 0, 0, g)),
        out_shape=jax.ShapeDtypeStruct((depth, N_CH, D, 5 * CW), BF16),
        compiler_params=pltpu.CompilerParams(vmem_limit_bytes=VMEM_LIMIT),
        name="pack_w5",
    )(w_in)
    wg = pl.pallas_call(
        _cast_kernel,
        grid=(depth, 2),
        in_specs=[pl.BlockSpec((1, D, D), lambda l, k: (l, 0, 5 + k))],
        out_specs=pl.BlockSpec((1, D, D), lambda l, k: (l, 0, k)),
        out_shape=jax.ShapeDtypeStruct((depth, D, 2 * D), BF16),
        compiler_params=pltpu.CompilerParams(vmem_limit_bytes=VMEM_LIMIT),
        name="pack_wg",
    )(w_in)
    return w5, wg


def _block_diag_chunks(w):
    tiled = jnp.tile(w.reshape(N_CH, CW, HEAD_D), (1, 1, CW // HEAD_D))
    blk = jnp.arange(CW) // HEAD_D
    return jnp.where(blk[:, None] == blk[None, :], tiled, 0.0)


def kernel(x_prompt, x_sample, state_rglru, c, c_ctx, w_mod, b_mod, w_in, conv_a, w_a_out, conv_b, conv_b_bias, w_gate_a, b_gate_a, w_gate_x, b_gate_x, lru_lambda, w_b_out, w_o, ln1_g, ln1_b, ln2_g, ln2_b, ffn_w1, ffn_w3, ffn_w2, router_w, router_b, moe_w1, moe_w3, moe_w2):
    batch, seq, d = x_prompt.shape
    dec_batch, dec_seq, _ = x_sample.shape
    depth = w_mod.shape[0]
    assert (d, seq, dec_seq, depth) == (D, SEQ, DEC_SEQ, DEPTH)
    n_ctx_tok = batch * seq
    n_dec_tok = dec_batch * dec_seq
    assert n_ctx_tok % TM1 == 0 and TM1 == dec_seq and 1 + dec_batch <= MOD_ROWS
    n_ctx_tiles = n_ctx_tok // TM1
    n_tiles = n_ctx_tiles + dec_batch
    seq_per_tile = TM1 // seq
    assert N_SEG == 2 * seq_per_tile

    assert depth % 2 == 0
    n_tok = n_ctx_tok + n_dec_tok
    x = (x_prompt.reshape(n_ctx_tok, D), x_sample.reshape(n_dec_tok, D))

    cond = jnp.zeros((MOD_ROWS, D), F32).at[0].set(c_ctx).at[1:1 + dec_batch].set(c)
    mod = _modulation(cond, w_mod, b_mod)

    w5, wg = _pack_w_in(w_in)
    states = []
    u_pre = None
    for l in range(depth):
        wbd = jnp.concatenate(
            [_block_diag_chunks(w_gate_a[l, 0]), _block_diag_chunks(w_gate_x[l, 0]),
             _block_diag_chunks(w_gate_a[l, 1]), _block_diag_chunks(w_gate_x[l, 1])],
            axis=-1).astype(BF16)
        gbias = jnp.stack([b_gate_a[l, 0], b_gate_x[l, 0], b_gate_a[l, 1], b_gate_x[l, 1]], axis=0)
        h0 = jnp.zeros((n_tiles, 2, N_SEG, D), F32)
        h0 = h0.at[n_ctx_tiles:, 0, 0].set(state_rglru[:, l, 0].astype(F32))
        h0 = h0.at[n_ctx_tiles:, 1, N_SEG - 1].set(state_rglru[:, l, 1].astype(F32))

        a_pre, b_pre, st = _mixer_part1(
            x, u_pre, mod[l], l, w5, conv_a[l], conv_b[l], conv_b_bias[l].reshape(1, D), wbd, gbias,
            lru_lambda[l], h0, n_tok, n_ctx_tok)
        x = _mixer_part2(
            x, u_pre, mod[l], l, a_pre, b_pre, wg, w_a_out[l].astype(BF16), w_b_out[l].astype(BF16),
            w_o[l].astype(BF16), ln1_g[l].reshape(1, D), ln1_b[l].reshape(1, D), n_ctx_tok)
        u_pre = None

        if l % 2 == 0:
            k = l // 2
            x, u_pre = _dense_ffn(x, mod[l], mod[l + 1], _pack_w13(ffn_w1[k], ffn_w3[k]),
                                  ffn_w2[k].astype(BF16), ln2_g[l].reshape(1, D), ln2_b[l].reshape(1, D),
                                  n_ctx_tok)
        else:
            k = l // 2
            x = _moe_ffn(x, mod[l], router_w[k], router_b[k], moe_w1[k], moe_w3[k], moe_w2[k],
                         ln2_g[l].reshape(1, D), ln2_b[l].reshape(1, D), n_ctx_tok)

        st_ctx = st[:n_ctx_tiles]
        fwd = st_ctx[:, 0, 1::2].reshape(batch, D)
        bwd = st_ctx[:, 1, 0::2].reshape(batch, D)
        states.append(jnp.stack([fwd, bwd], axis=1))

    y_prompt = x[0].reshape(batch, seq, D)
    y_sample = x[1].reshape(dec_batch, dec_seq, D)
    new_state = jnp.stack(states, axis=1).astype(x_prompt.dtype)
    return (y_prompt, y_sample, new_state)
```
